```python
import jax, jax.numpy as jnp
from jax import lax
import numpy as np


D_MODEL = 1024
BATCH = 8
SEQ = 8192
DEPTH = 4

MLA_HEADS = 4
QK_NOPE_DIM = 64
QK_ROPE_DIM = 32
V_HEAD_DIM = 64
Q_RANK = D_MODEL // 4
KV_RANK = D_MODEL // 8
ROPE_BASE = 10000.0
Q_BLOCK = 128
SG_GROUPS = 4
SG_WIDTH = D_MODEL // 4
SG_CHUNK = 128
CONV_WIDTH = D_MODEL // 4
CONV_K = 3
POOL_WINDOWS = (2, 4, 8, 16)
POOL_WIDTH = D_MODEL // 4
POOL_GROUP = POOL_WIDTH // 4
N_BRANCH = 4
D_FF = 4 * D_MODEL
EPS = 1e-6
N_IN = Q_RANK + KV_RANK + QK_ROPE_DIM + 2 * SG_WIDTH + 3 * CONV_WIDTH + POOL_WIDTH + N_BRANCH * D_MODEL

kernel_name = 'hybrid_gated_mla_sgmlp_conv_pool_block'


def rmsnorm(x, g):
    xf = x.astype(jnp.float32)
    y = xf * lax.rsqrt(jnp.mean(xf * xf, axis=-1, keepdims=True) + EPS)
    return (y * g.astype(jnp.float32)).astype(x.dtype)


def layernorm(x, g, b):
    xf = x.astype(jnp.float32)
    mu = jnp.mean(xf, axis=-1, keepdims=True)
    xc = xf - mu
    y = xc * lax.rsqrt(jnp.mean(xc * xc, axis=-1, keepdims=True) + EPS)
    return (y * g.astype(jnp.float32) + b.astype(jnp.float32)).astype(x.dtype)


def split_cols(proj):
    sizes = (Q_RANK, KV_RANK, QK_ROPE_DIM, SG_WIDTH, SG_WIDTH, CONV_WIDTH, CONV_WIDTH, CONV_WIDTH,
             POOL_WIDTH, N_BRANCH * D_MODEL)
    offs = []
    acc = 0
    for s in sizes[:-1]:
        acc += s
        offs.append(acc)
    return jnp.split(proj, offs, axis=-1)


def rope(x, cos, sin):
    x1, x2 = jnp.split(x, 2, axis=-1)
    return jnp.concatenate([x1 * cos - x2 * sin, x2 * cos + x1 * sin], axis=-1)


def mla(c_q, c_kv, k_rope, positions, q_norm, w_uq, kv_norm, w_ukv):
    B_, S_, _ = c_q.shape
    q = (rmsnorm(c_q, q_norm) @ w_uq).reshape(B_, S_, MLA_HEADS, QK_NOPE_DIM + QK_ROPE_DIM)
    q_nope, q_rope = q[..., :QK_NOPE_DIM], q[..., QK_NOPE_DIM:]
    kv = (rmsnorm(c_kv, kv_norm) @ w_ukv).reshape(B_, S_, MLA_HEADS, QK_NOPE_DIM + V_HEAD_DIM)
    k_nope, v = kv[..., :QK_NOPE_DIM], kv[..., QK_NOPE_DIM:]
    inv_freq = ROPE_BASE ** (-jnp.arange(0, QK_ROPE_DIM, 2, dtype=jnp.float32) / QK_ROPE_DIM)
    ang = positions.astype(jnp.float32)[..., None] * inv_freq
    cos = jnp.cos(ang).astype(q.dtype)
    sin = jnp.sin(ang).astype(q.dtype)
    q_rope = rope(q_rope, cos[:, :, None, :], sin[:, :, None, :])
    k_rope = rope(k_rope, cos, sin)
    scale = (QK_NOPE_DIM + QK_ROPE_DIM) ** -0.5
    nb = S_ // Q_BLOCK

    def to_blocks(a):
        return jnp.moveaxis(a.reshape((B_, nb, Q_BLOCK) + a.shape[2:]), 1, 0)

    k_idx = jnp.arange(S_)

    def block(args):
        qn, qr, start = args
        s = jnp.einsum('bqhd,bkhd->bhqk', qn, k_nope) + jnp.einsum('bqhr,bkr->bhqk', qr, k_rope)
        s = s.astype(jnp.float32) * scale
        q_idx = start + jnp.arange(Q_BLOCK)
        s = jnp.where(k_idx[None, :] <= q_idx[:, None], s, -jnp.inf)
        p = jax.nn.softmax(s, axis=-1).astype(v.dtype)
        return jnp.einsum('bhqk,bkhd->bqhd', p, v)

    starts = jnp.arange(nb, dtype=jnp.int32) * Q_BLOCK
    o = lax.map(block, (to_blocks(q_nope), to_blocks(q_rope), starts))
    return jnp.moveaxis(o, 0, 1).reshape(B_, S_, MLA_HEADS * V_HEAD_DIM)


def spatial_gating(u, v, ln_g, ln_b, w_s, b_s):
    B_, S_, _ = u.shape
    u = jax.nn.gelu(u)
    v = layernorm(jax.nn.gelu(v), ln_g, ln_b)
    n = S_ // SG_CHUNK
    vc = v.reshape(B_, n, SG_CHUNK, SG_GROUPS, SG_WIDTH // SG_GROUPS)
    mask = jnp.tril(jnp.ones((SG_CHUNK, SG_CHUNK), dtype=bool))
    w = jnp.where(mask, w_s, 0)
    s = jnp.einsum('gts,bnsgc->bntgc', w, vc) + b_s.T[:, :, None]
    return u * s.reshape(B_, S_, SG_WIDTH)


def short_conv(xin, bg, cg, conv_w):
    z = cg * xin
    S_ = z.shape[1]
    zp = jnp.pad(z, ((0, 0), (CONV_K - 1, 0), (0, 0)))
    y = sum(conv_w[k] * zp[:, k:k + S_] for k in range(CONV_K))
    return bg * y


def multiscale_pool(p, w_pool, scale):
    B_, S_, _ = p.shape
    pg = p.reshape(B_, S_, len(POOL_WINDOWS), POOL_GROUP).astype(jnp.float32)
    cs = jnp.cumsum(pg, axis=1)
    t = jnp.arange(S_)
    outs = []
    for g, win in enumerate(POOL_WINDOWS):
        c = cs[:, :, g]
        lagged = jnp.pad(c, ((0, 0), (win, 0), (0, 0)))[:, :S_]
        cnt = jnp.minimum(t + 1, win).astype(jnp.float32)[None, :, None]
        outs.append((c - lagged) / cnt - pg[:, :, g])
    pooled = jnp.stack(outs, axis=2).astype(p.dtype)
    mixed = jnp.einsum('bsgc,gcd->bsgd', pooled, w_pool)
    return mixed.reshape(B_, S_, POOL_WIDTH) * scale


def _fwd_setup_inputs(seed: int = 0) -> dict:
    key = jax.random.key(seed)
    ks = jax.random.split(key, 26)
    L = DEPTH

    def nrm(k, shape, fan_in):
        return jax.random.normal(k, shape, jnp.float32) * fan_in ** -0.5

    def gain(k, shape):
        return 1.0 + 0.05 * jax.random.normal(k, shape, jnp.float32)

    def small(k, shape, s):
        return s * jax.random.normal(k, shape, jnp.float32)

    x = jax.random.normal(ks[0], (BATCH, SEQ, D_MODEL), jnp.float32)
    positions = (jax.random.randint(ks[1], (BATCH, 1), 0, 4096, dtype=jnp.int32)
                 + jnp.arange(SEQ, dtype=jnp.int32)[None, :])
    return {
        'x': x,
        'positions': positions,
        'norm_mix_pre': gain(ks[2], (L, D_MODEL)),
        'w_in': nrm(ks[3], (L, D_MODEL, N_IN), D_MODEL),
        'gate_b': small(ks[4], (L, N_BRANCH * D_MODEL), 0.01),
        'q_norm': gain(ks[5], (L, Q_RANK)),
        'w_uq': nrm(ks[6], (L, Q_RANK, MLA_HEADS * (QK_NOPE_DIM + QK_ROPE_DIM)), Q_RANK),
        'kv_norm': gain(ks[7], (L, KV_RANK)),
        'w_ukv': nrm(ks[8], (L, KV_RANK, MLA_HEADS * (QK_NOPE_DIM + V_HEAD_DIM)), KV_RANK),
        'w_br_mla': nrm(ks[9], (L, MLA_HEADS * V_HEAD_DIM, D_MODEL), MLA_HEADS * V_HEAD_DIM),
        'sg_ln_g': gain(ks[10], (L, SG_WIDTH)),
        'sg_ln_b': small(ks[11], (L, SG_WIDTH), 0.02),
        'sg_w': nrm(ks[12], (L, SG_GROUPS, SG_CHUNK, SG_CHUNK), SG_CHUNK),
        'sg_b': gain(ks[13], (L, SG_GROUPS, SG_CHUNK)),
        'w_br_sg': nrm(ks[14], (L, SG_WIDTH, D_MODEL), SG_WIDTH),
        'conv_w': nrm(ks[15], (L, CONV_K, CONV_WIDTH), CONV_K),
        'w_br_conv': nrm(ks[16], (L, CONV_WIDTH, D_MODEL), CONV_WIDTH),
        'pool_w': nrm(ks[17], (L, len(POOL_WINDOWS), POOL_GROUP, POOL_GROUP), POOL_GROUP),
        'pool_scale': gain(ks[18], (L, POOL_WIDTH)),
        'w_br_pool': nrm(ks[19], (L, POOL_WIDTH, D_MODEL), POOL_WIDTH),
        'w_out': nrm(ks[20], (L, D_MODEL, D_MODEL), D_MODEL),
        'norm_mix_post': gain(ks[21], (L, D_MODEL)),
        'norm_ffn_pre': gain(ks[22], (L, D_MODEL)),
        'w_ff1': nrm(ks[23], (L, D_MODEL, D_FF), D_MODEL),
        'w_ff2': nrm(ks[24], (L, D_FF, D_MODEL), D_FF),
        'norm_ffn_post': gain(ks[25], (L, D_MODEL)),
    }


def _fwd_reference(x, positions, norm_mix_pre, w_in, gate_b, q_norm, w_uq, kv_norm, w_ukv, w_br_mla,
              sg_ln_g, sg_ln_b, sg_w, sg_b, w_br_sg, conv_w, w_br_conv, pool_w, pool_scale, w_br_pool,
              w_out, norm_mix_post, norm_ffn_pre, w_ff1, w_ff2, norm_ffn_post):
    B_, S_, D_ = x.shape
    for l in range(DEPTH):
        h = rmsnorm(x, norm_mix_pre[l])
        (c_q, c_kv, k_r, sg_u, sg_v, cv_x, cv_b, cv_c, pool_in, gate_pre) = split_cols(h @ w_in[l])
        gates = jax.nn.sigmoid(gate_pre + gate_b[l]).reshape(B_, S_, N_BRANCH, D_)
        y_a = mla(c_q, c_kv, k_r, positions, q_norm[l], w_uq[l], kv_norm[l], w_ukv[l]) @ w_br_mla[l]
        y_b = spatial_gating(sg_u, sg_v, sg_ln_g[l], sg_ln_b[l], sg_w[l], sg_b[l]) @ w_br_sg[l]
        y_c = short_conv(cv_x, cv_b, cv_c, conv_w[l]) @ w_br_conv[l]
        y_d = multiscale_pool(pool_in, pool_w[l], pool_scale[l]) @ w_br_pool[l]
        merged = (gates[:, :, 0] * y_a + gates[:, :, 1] * y_b
                  + gates[:, :, 2] * y_c + gates[:, :, 3] * y_d)
        x = x + rmsnorm(merged @ w_out[l], norm_mix_post[l])
        h = rmsnorm(x, norm_ffn_pre[l])
        f = jnp.square(jax.nn.relu(h @ w_ff1[l])) @ w_ff2[l]
        x = x + rmsnorm(f, norm_ffn_post[l])
    return x


import jax as _jax
import jax.numpy as _jnp

TWIN_FORMAT = 'train_step'
FWD_PARAMS = ['x', 'positions', 'norm_mix_pre', 'w_in', 'gate_b', 'q_norm', 'w_uq', 'kv_norm', 'w_ukv', 'w_br_mla', 'sg_ln_g', 'sg_ln_b', 'sg_w', 'sg_b', 'w_br_sg', 'conv_w', 'w_br_conv', 'pool_w', 'pool_scale', 'w_br_pool', 'w_out', 'norm_mix_post', 'norm_ffn_pre', 'w_ff1', 'w_ff2', 'norm_ffn_post']
TWIN_WEIGHTS = ['norm_mix_pre', 'w_in', 'gate_b', 'q_norm', 'w_uq', 'kv_norm', 'w_ukv', 'w_br_mla', 'sg_ln_g', 'sg_ln_b', 'sg_w', 'sg_b', 'w_br_sg', 'conv_w', 'w_br_conv', 'pool_w', 'pool_scale', 'w_br_pool', 'w_out', 'norm_mix_post', 'norm_ffn_pre', 'w_ff1', 'w_ff2', 'norm_ffn_post']
TWIN_DIFF_INPUT = 'x'
TWIN_INPUTS = ['x', 'positions', 'norm_mix_pre', 'w_in', 'gate_b', 'q_norm', 'w_uq', 'kv_norm', 'w_ukv', 'w_br_mla', 'sg_ln_g', 'sg_ln_b', 'sg_w', 'sg_b', 'w_br_sg', 'conv_w', 'w_br_conv', 'pool_w', 'pool_scale', 'w_br_pool', 'w_out', 'norm_mix_post', 'norm_ffn_pre', 'w_ff1', 'w_ff2', 'norm_ffn_post', 'loss_target', 'm_norm_mix_pre', 'm_w_in', 'm_gate_b', 'm_q_norm', 'm_w_uq', 'm_kv_norm', 'm_w_ukv', 'm_w_br_mla', 'm_sg_ln_g', 'm_sg_ln_b', 'm_sg_w', 'm_sg_b', 'm_w_br_sg', 'm_conv_w', 'm_w_br_conv', 'm_pool_w', 'm_pool_scale', 'm_w_br_pool', 'm_w_out', 'm_norm_mix_post', 'm_norm_ffn_pre', 'm_w_ff1', 'm_w_ff2', 'm_norm_ffn_post', 'v_norm_mix_pre', 'v_w_in', 'v_gate_b', 'v_q_norm', 'v_w_uq', 'v_kv_norm', 'v_w_ukv', 'v_w_br_mla', 'v_sg_ln_g', 'v_sg_ln_b', 'v_sg_w', 'v_sg_b', 'v_w_br_sg', 'v_conv_w', 'v_w_br_conv', 'v_pool_w', 'v_pool_scale', 'v_w_br_pool', 'v_w_out', 'v_norm_mix_post', 'v_norm_ffn_pre', 'v_w_ff1', 'v_w_ff2', 'v_norm_ffn_post']
TWIN_OUTPUTS = ['loss', 'grad_x', 'grad_norm_mix_pre', 'grad_w_in', 'grad_gate_b', 'grad_q_norm', 'grad_w_uq', 'grad_kv_norm', 'grad_w_ukv', 'grad_w_br_mla', 'grad_sg_ln_g', 'grad_sg_ln_b', 'grad_sg_w', 'grad_sg_b', 'grad_w_br_sg', 'grad_conv_w', 'grad_w_br_conv', 'grad_pool_w', 'grad_pool_scale', 'grad_w_br_pool', 'grad_w_out', 'grad_norm_mix_post', 'grad_norm_ffn_pre', 'grad_w_ff1', 'grad_w_ff2', 'grad_norm_ffn_post', 'delta_norm_mix_pre', 'delta_w_in', 'delta_gate_b', 'delta_q_norm', 'delta_w_uq', 'delta_kv_norm', 'delta_w_ukv', 'delta_w_br_mla', 'delta_sg_ln_g', 'delta_sg_ln_b', 'delta_sg_w', 'delta_sg_b', 'delta_w_br_sg', 'delta_conv_w', 'delta_w_br_conv', 'delta_pool_w', 'delta_pool_scale', 'delta_w_br_pool', 'delta_w_out', 'delta_norm_mix_post', 'delta_norm_ffn_pre', 'delta_w_ff1', 'delta_w_ff2', 'delta_norm_ffn_post', 'new_m_norm_mix_pre', 'new_m_w_in', 'new_m_gate_b', 'new_m_q_norm', 'new_m_w_uq', 'new_m_kv_norm', 'new_m_w_ukv', 'new_m_w_br_mla', 'new_m_sg_ln_g', 'new_m_sg_ln_b', 'new_m_sg_w', 'new_m_sg_b', 'new_m_w_br_sg', 'new_m_conv_w', 'new_m_w_br_conv', 'new_m_pool_w', 'new_m_pool_scale', 'new_m_w_br_pool', 'new_m_w_out', 'new_m_norm_mix_post', 'new_m_norm_ffn_pre', 'new_m_w_ff1', 'new_m_w_ff2', 'new_m_norm_ffn_post', 'new_v_norm_mix_pre', 'new_v_w_in', 'new_v_gate_b', 'new_v_q_norm', 'new_v_w_uq', 'new_v_kv_norm', 'new_v_w_ukv', 'new_v_w_br_mla', 'new_v_sg_ln_g', 'new_v_sg_ln_b', 'new_v_sg_w', 'new_v_sg_b', 'new_v_w_br_sg', 'new_v_conv_w', 'new_v_w_br_conv', 'new_v_pool_w', 'new_v_pool_scale', 'new_v_w_br_pool', 'new_v_w_out', 'new_v_norm_mix_post', 'new_v_norm_ffn_pre', 'new_v_w_ff1', 'new_v_w_ff2', 'new_v_norm_ffn_post']
TWIN_LEAF_KINDS = {'loss': 'loss', 'grad_x': 'grad_x', 'grad_norm_mix_pre': 'grad_w', 'grad_w_in': 'grad_w', 'grad_gate_b': 'grad_w', 'grad_q_norm': 'grad_w', 'grad_w_uq': 'grad_w', 'grad_kv_norm': 'grad_w', 'grad_w_ukv': 'grad_w', 'grad_w_br_mla': 'grad_w', 'grad_sg_ln_g': 'grad_w', 'grad_sg_ln_b': 'grad_w', 'grad_sg_w': 'grad_w', 'grad_sg_b': 'grad_w', 'grad_w_br_sg': 'grad_w', 'grad_conv_w': 'grad_w', 'grad_w_br_conv': 'grad_w', 'grad_pool_w': 'grad_w', 'grad_pool_scale': 'grad_w', 'grad_w_br_pool': 'grad_w', 'grad_w_out': 'grad_w', 'grad_norm_mix_post': 'grad_w', 'grad_norm_ffn_pre': 'grad_w', 'grad_w_ff1': 'grad_w', 'grad_w_ff2': 'grad_w', 'grad_norm_ffn_post': 'grad_w', 'delta_norm_mix_pre': 'delta_w', 'delta_w_in': 'delta_w', 'delta_gate_b': 'delta_w', 'delta_q_norm': 'delta_w', 'delta_w_uq': 'delta_w', 'delta_kv_norm': 'delta_w', 'delta_w_ukv': 'delta_w', 'delta_w_br_mla': 'delta_w', 'delta_sg_ln_g': 'delta_w', 'delta_sg_ln_b': 'delta_w', 'delta_sg_w': 'delta_w', 'delta_sg_b': 'delta_w', 'delta_w_br_sg': 'delta_w', 'delta_conv_w': 'delta_w', 'delta_w_br_conv': 'delta_w', 'delta_pool_w': 'delta_w', 'delta_pool_scale': 'delta_w', 'delta_w_br_pool': 'delta_w', 'delta_w_out': 'delta_w', 'delta_norm_mix_post': 'delta_w', 'delta_norm_ffn_pre': 'delta_w', 'delta_w_ff1': 'delta_w', 'delta_w_ff2': 'delta_w', 'delta_norm_ffn_post': 'delta_w', 'new_m_norm_mix_pre': 'new_m', 'new_m_w_in': 'new_m', 'new_m_gate_b': 'new_m', 'new_m_q_norm': 'new_m', 'new_m_w_uq': 'new_m', 'new_m_kv_norm': 'new_m', 'new_m_w_ukv': 'new_m', 'new_m_w_br_mla': 'new_m', 'new_m_sg_ln_g': 'new_m', 'new_m_sg_ln_b': 'new_m', 'new_m_sg_w': 'new_m', 'new_m_sg_b': 'new_m', 'new_m_w_br_sg': 'new_m', 'new_m_conv_w': 'new_m', 'new_m_w_br_conv': 'new_m', 'new_m_pool_w': 'new_m', 'new_m_pool_scale': 'new_m', 'new_m_w_br_pool': 'new_m', 'new_m_w_out': 'new_m', 'new_m_norm_mix_post': 'new_m', 'new_m_norm_ffn_pre': 'new_m', 'new_m_w_ff1': 'new_m', 'new_m_w_ff2': 'new_m', 'new_m_norm_ffn_post': 'new_m', 'new_v_norm_mix_pre': 'new_v', 'new_v_w_in': 'new_v', 'new_v_gate_b': 'new_v', 'new_v_q_norm': 'new_v', 'new_v_w_uq': 'new_v', 'new_v_kv_norm': 'new_v', 'new_v_w_ukv': 'new_v', 'new_v_w_br_mla': 'new_v', 'new_v_sg_ln_g': 'new_v', 'new_v_sg_ln_b': 'new_v', 'new_v_sg_w': 'new_v', 'new_v_sg_b': 'new_v', 'new_v_w_br_sg': 'new_v', 'new_v_conv_w': 'new_v', 'new_v_w_br_conv': 'new_v', 'new_v_pool_w': 'new_v', 'new_v_pool_scale': 'new_v', 'new_v_w_br_pool': 'new_v', 'new_v_w_out': 'new_v', 'new_v_norm_mix_post': 'new_v', 'new_v_norm_ffn_pre': 'new_v', 'new_v_w_ff1': 'new_v', 'new_v_w_ff2': 'new_v', 'new_v_norm_ffn_post': 'new_v'}


def _forward(args):
    return _fwd_reference(*[args[k] for k in FWD_PARAMS])


def _output_shape():
    def fwd():
        inp = _fwd_setup_inputs(0)
        return _fwd_reference(*[inp[k] for k in FWD_PARAMS])
    out = _jax.eval_shape(fwd)
    return out.shape, out.dtype

N_MICROBATCH = 1
ADAM_LR = 0.001
ADAM_B1 = 0.9
ADAM_B2 = 0.999
ADAM_EPS = 1e-08
ADAM_WD = 0.01
ADAM_STEP = 10
PER_EXAMPLE_BATCH_AXIS = {'x': 0, 'positions': 0, 'loss_target': 0}
SHARED_INPUTS = []
_WEIGHT_DTYPES = {'norm_mix_pre': _jnp.float32, 'w_in': _jnp.float32, 'gate_b': _jnp.float32, 'q_norm': _jnp.float32, 'w_uq': _jnp.float32, 'kv_norm': _jnp.float32, 'w_ukv': _jnp.float32, 'w_br_mla': _jnp.float32, 'sg_ln_g': _jnp.float32, 'sg_ln_b': _jnp.float32, 'sg_w': _jnp.float32, 'sg_b': _jnp.float32, 'w_br_sg': _jnp.float32, 'conv_w': _jnp.float32, 'w_br_conv': _jnp.float32, 'pool_w': _jnp.float32, 'pool_scale': _jnp.float32, 'w_br_pool': _jnp.float32, 'w_out': _jnp.float32, 'norm_mix_post': _jnp.float32, 'norm_ffn_pre': _jnp.float32, 'w_ff1': _jnp.float32, 'w_ff2': _jnp.float32, 'norm_ffn_post': _jnp.float32}
MOMENT_SCALE = {'norm_mix_pre': 1.017750e+01, 'w_in': 4.106718e+00, 'gate_b': 3.679189e+00, 'q_norm': 1.089079e+00, 'w_uq': 8.263536e-01, 'kv_norm': 3.041621e+01, 'w_ukv': 1.476644e+01, 'w_br_mla': 1.089021e+01, 'sg_ln_g': 1.177870e+00, 'sg_ln_b': 1.096255e+00, 'sg_w': 6.537579e-01, 'sg_b': 1.180728e+00, 'w_br_sg': 1.362063e+01, 'conv_w': 2.703133e+00, 'w_br_conv': 1.802544e+00, 'pool_w': 4.941431e+00, 'pool_scale': 6.072851e+00, 'w_br_pool': 2.645902e+00, 'w_out': 1.753530e+01, 'norm_mix_post': 6.750469e+01, 'norm_ffn_pre': 1.149187e+01, 'w_ff1': 5.692975e+00, 'w_ff2': 2.976638e+01, 'norm_ffn_post': 7.336781e+01}


def _to_microbatches(a, axis):
    t = _jnp.moveaxis(a, axis, 0)
    t = t.reshape((N_MICROBATCH, t.shape[0] // N_MICROBATCH) + t.shape[1:])
    return _jnp.moveaxis(t, 1, axis + 1)


def setup_inputs(seed: int = 0) -> dict:
    inp = _fwd_setup_inputs(seed)
    key = _jax.random.fold_in(_jax.random.key(seed), 7919)
    shape, _ = _output_shape()
    out = dict(inp)
    out["loss_target"] = _jax.random.normal(_jax.random.fold_in(key, 0), shape, _jnp.float32)
    for i, name in enumerate(TWIN_WEIGHTS):
        w = inp[name].astype(_jnp.float32)
        if MOMENT_SCALE is None:
            s = _jnp.sqrt(_jnp.mean(_jnp.square(w)) + 1e-30)
        else:
            s = MOMENT_SCALE[name]
        km, kv = _jax.random.split(_jax.random.fold_in(key, i + 1))
        out[name] = w
        out["m_" + name] = s * _jax.random.normal(km, w.shape, _jnp.float32)
        out["v_" + name] = (s * s) * _jax.random.uniform(kv, w.shape, _jnp.float32, 0.5, 1.5)
    if N_MICROBATCH > 1:
        for name, axis in PER_EXAMPLE_BATCH_AXIS.items():
            out[name] = _to_microbatches(out[name], axis)
    return {'x': out['x'], 'positions': out['positions'], 'norm_mix_pre': out['norm_mix_pre'], 'w_in': out['w_in'], 'gate_b': out['gate_b'], 'q_norm': out['q_norm'], 'w_uq': out['w_uq'], 'kv_norm': out['kv_norm'], 'w_ukv': out['w_ukv'], 'w_br_mla': out['w_br_mla'], 'sg_ln_g': out['sg_ln_g'], 'sg_ln_b': out['sg_ln_b'], 'sg_w': out['sg_w'], 'sg_b': out['sg_b'], 'w_br_sg': out['w_br_sg'], 'conv_w': out['conv_w'], 'w_br_conv': out['w_br_conv'], 'pool_w': out['pool_w'], 'pool_scale': out['pool_scale'], 'w_br_pool': out['w_br_pool'], 'w_out': out['w_out'], 'norm_mix_post': out['norm_mix_post'], 'norm_ffn_pre': out['norm_ffn_pre'], 'w_ff1': out['w_ff1'], 'w_ff2': out['w_ff2'], 'norm_ffn_post': out['norm_ffn_post'], 'loss_target': out['loss_target'], 'm_norm_mix_pre': out['m_norm_mix_pre'], 'm_w_in': out['m_w_in'], 'm_gate_b': out['m_gate_b'], 'm_q_norm': out['m_q_norm'], 'm_w_uq': out['m_w_uq'], 'm_kv_norm': out['m_kv_norm'], 'm_w_ukv': out['m_w_ukv'], 'm_w_br_mla': out['m_w_br_mla'], 'm_sg_ln_g': out['m_sg_ln_g'], 'm_sg_ln_b': out['m_sg_ln_b'], 'm_sg_w': out['m_sg_w'], 'm_sg_b': out['m_sg_b'], 'm_w_br_sg': out['m_w_br_sg'], 'm_conv_w': out['m_conv_w'], 'm_w_br_conv': out['m_w_br_conv'], 'm_pool_w': out['m_pool_w'], 'm_pool_scale': out['m_pool_scale'], 'm_w_br_pool': out['m_w_br_pool'], 'm_w_out': out['m_w_out'], 'm_norm_mix_post': out['m_norm_mix_post'], 'm_norm_ffn_pre': out['m_norm_ffn_pre'], 'm_w_ff1': out['m_w_ff1'], 'm_w_ff2': out['m_w_ff2'], 'm_norm_ffn_post': out['m_norm_ffn_post'], 'v_norm_mix_pre': out['v_norm_mix_pre'], 'v_w_in': out['v_w_in'], 'v_gate_b': out['v_gate_b'], 'v_q_norm': out['v_q_norm'], 'v_w_uq': out['v_w_uq'], 'v_kv_norm': out['v_kv_norm'], 'v_w_ukv': out['v_w_ukv'], 'v_w_br_mla': out['v_w_br_mla'], 'v_sg_ln_g': out['v_sg_ln_g'], 'v_sg_ln_b': out['v_sg_ln_b'], 'v_sg_w': out['v_sg_w'], 'v_sg_b': out['v_sg_b'], 'v_w_br_sg': out['v_w_br_sg'], 'v_conv_w': out['v_conv_w'], 'v_w_br_conv': out['v_w_br_conv'], 'v_pool_w': out['v_pool_w'], 'v_pool_scale': out['v_pool_scale'], 'v_w_br_pool': out['v_w_br_pool'], 'v_w_out': out['v_w_out'], 'v_norm_mix_post': out['v_norm_mix_post'], 'v_norm_ffn_pre': out['v_norm_ffn_pre'], 'v_w_ff1': out['v_w_ff1'], 'v_w_ff2': out['v_w_ff2'], 'v_norm_ffn_post': out['v_norm_ffn_post']}


def _loss(weights, diff, rest, loss_target):
    with _jax.named_scope("forward"):
        args = {**rest, TWIN_DIFF_INPUT: diff, **{k: w.astype(_WEIGHT_DTYPES[k]) for k, w in weights.items()}}
        y = _forward(args)
    with _jax.named_scope("loss_head"):
        err = _jnp.square(y.astype(_jnp.float32) - loss_target)
        return 0.5 * _jnp.sum(_jnp.mean(err, axis=-1)) if err.ndim else 0.5 * err


def _adamw(w, g, m, v):
    m = ADAM_B1 * m + (1.0 - ADAM_B1) * g
    v = ADAM_B2 * v + (1.0 - ADAM_B2) * _jnp.square(g)
    m_hat = m / (1.0 - ADAM_B1 ** ADAM_STEP)
    v_hat = v / (1.0 - ADAM_B2 ** ADAM_STEP)
    delta = -ADAM_LR * (m_hat / (_jnp.sqrt(v_hat) + ADAM_EPS) + ADAM_WD * w)
    return delta, m, v


def reference(x, positions, norm_mix_pre, w_in, gate_b, q_norm, w_uq, kv_norm, w_ukv, w_br_mla, sg_ln_g, sg_ln_b, sg_w, sg_b, w_br_sg, conv_w, w_br_conv, pool_w, pool_scale, w_br_pool, w_out, norm_mix_post, norm_ffn_pre, w_ff1, w_ff2, norm_ffn_post, loss_target, m_norm_mix_pre, m_w_in, m_gate_b, m_q_norm, m_w_uq, m_kv_norm, m_w_ukv, m_w_br_mla, m_sg_ln_g, m_sg_ln_b, m_sg_w, m_sg_b, m_w_br_sg, m_conv_w, m_w_br_conv, m_pool_w, m_pool_scale, m_w_br_pool, m_w_out, m_norm_mix_post, m_norm_ffn_pre, m_w_ff1, m_w_ff2, m_norm_ffn_post, v_norm_mix_pre, v_w_in, v_gate_b, v_q_norm, v_w_uq, v_kv_norm, v_w_ukv, v_w_br_mla, v_sg_ln_g, v_sg_ln_b, v_sg_w, v_sg_b, v_w_br_sg, v_conv_w, v_w_br_conv, v_pool_w, v_pool_scale, v_w_br_pool, v_w_out, v_norm_mix_post, v_norm_ffn_pre, v_w_ff1, v_w_ff2, v_norm_ffn_post):
    given = dict(x=x, positions=positions, norm_mix_pre=norm_mix_pre, w_in=w_in, gate_b=gate_b, q_norm=q_norm, w_uq=w_uq, kv_norm=kv_norm, w_ukv=w_ukv, w_br_mla=w_br_mla, sg_ln_g=sg_ln_g, sg_ln_b=sg_ln_b, sg_w=sg_w, sg_b=sg_b, w_br_sg=w_br_sg, conv_w=conv_w, w_br_conv=w_br_conv, pool_w=pool_w, pool_scale=pool_scale, w_br_pool=w_br_pool, w_out=w_out, norm_mix_post=norm_mix_post, norm_ffn_pre=norm_ffn_pre, w_ff1=w_ff1, w_ff2=w_ff2, norm_ffn_post=norm_ffn_post, loss_target=loss_target, m_norm_mix_pre=m_norm_mix_pre, m_w_in=m_w_in, m_gate_b=m_gate_b, m_q_norm=m_q_norm, m_w_uq=m_w_uq, m_kv_norm=m_kv_norm, m_w_ukv=m_w_ukv, m_w_br_mla=m_w_br_mla, m_sg_ln_g=m_sg_ln_g, m_sg_ln_b=m_sg_ln_b, m_sg_w=m_sg_w, m_sg_b=m_sg_b, m_w_br_sg=m_w_br_sg, m_conv_w=m_conv_w, m_w_br_conv=m_w_br_conv, m_pool_w=m_pool_w, m_pool_scale=m_pool_scale, m_w_br_pool=m_w_br_pool, m_w_out=m_w_out, m_norm_mix_post=m_norm_mix_post, m_norm_ffn_pre=m_norm_ffn_pre, m_w_ff1=m_w_ff1, m_w_ff2=m_w_ff2, m_norm_ffn_post=m_norm_ffn_post, v_norm_mix_pre=v_norm_mix_pre, v_w_in=v_w_in, v_gate_b=v_gate_b, v_q_norm=v_q_norm, v_w_uq=v_w_uq, v_kv_norm=v_kv_norm, v_w_ukv=v_w_ukv, v_w_br_mla=v_w_br_mla, v_sg_ln_g=v_sg_ln_g, v_sg_ln_b=v_sg_ln_b, v_sg_w=v_sg_w, v_sg_b=v_sg_b, v_w_br_sg=v_w_br_sg, v_conv_w=v_conv_w, v_w_br_conv=v_w_br_conv, v_pool_w=v_pool_w, v_pool_scale=v_pool_scale, v_w_br_pool=v_w_br_pool, v_w_out=v_w_out, v_norm_mix_post=v_norm_mix_post, v_norm_ffn_pre=v_norm_ffn_pre, v_w_ff1=v_w_ff1, v_w_ff2=v_w_ff2, v_norm_ffn_post=v_norm_ffn_post)
    weights = {n: given[n] for n in TWIN_WEIGHTS}
    shared = {n: given[n] for n in SHARED_INPUTS}
    per_example = {n: given[n] for n in ['x', 'positions']}
    grad_fn = _jax.value_and_grad(_loss, argnums=(0, 1))

    def one_microbatch(ex, loss_target):
        ex = dict(ex)
        diff = ex.pop(TWIN_DIFF_INPUT)
        return grad_fn(weights, diff, {**shared, **ex}, loss_target)

    if N_MICROBATCH == 1:
        loss, (grad_w, grad_x) = one_microbatch(per_example, given["loss_target"])
    else:
        def body(carry, xs):
            loss_sum, grad_sum = carry
            l_k, (gw_k, gx_k) = one_microbatch(xs[0], xs[1])
            with _jax.named_scope("update"):
                return (loss_sum + l_k, _jax.tree.map(_jnp.add, grad_sum, gw_k)), gx_k

        init = (_jnp.zeros((), _jnp.float32), _jax.tree.map(_jnp.zeros_like, weights))
        (loss, grad_w), grad_x = _jax.lax.scan(body, init, (per_example, given["loss_target"]))
    with _jax.named_scope("update"):
        delta_w, new_m, new_v = {}, {}, {}
        for n in TWIN_WEIGHTS:
            delta_w[n], new_m[n], new_v[n] = _adamw(weights[n], grad_w[n], given["m_" + n], given["v_" + n])
    return (loss, grad_x, *[grad_w[n] for n in TWIN_WEIGHTS], *[delta_w[n] for n in TWIN_WEIGHTS],
            *[new_m[n] for n in TWIN_WEIGHTS], *[new_v[n] for n in TWIN_WEIGHTS])
```

```python
import functools
import math

import jax
import jax.numpy as jnp
from jax import lax
from jax.experimental import pallas as pl
from jax.experimental.pallas import tpu as pltpu

F32 = jnp.float32
MXU_DTYPE = jnp.bfloat16
MESH = pl.DeviceIdType.MESH

D_MODEL = 1024
D_FF = 4096
N_HEADS = 4
QK_NOPE = 64
QK_ROPE = 32
V_HEAD = 64
HEAD_PAD = 128
Q_RANK = 256
KV_RANK = 128
SG_CHUNK = 128
SG_GROUPS = 4
BR_WIDTH = 256
N_BRANCH = 4
POOL_WINDOWS = (2, 4, 8, 16)
HALO = 16
ROPE_BASE = 10000.0
EPS = 1e-6
ATTN_SCALE = (QK_NOPE + QK_ROPE) ** -0.5
N_PROJ = N_BRANCH * D_MODEL + 6 * BR_WIDTH + Q_RANK + KV_RANK + 2 * HEAD_PAD
COL_G, COL_M1, COL_M2, COL_B = 0, 4096, 4864, 5632

ADAM_LR, ADAM_B1, ADAM_B2, ADAM_EPS, ADAM_WD, ADAM_STEP = 0.001, 0.9, 0.999, 1e-08, 0.01, 10

VMEM_LIMIT = 56 * 1024 * 1024

WEIGHTS = ['norm_mix_pre', 'w_in', 'gate_b', 'q_norm', 'w_uq', 'kv_norm', 'w_ukv', 'w_br_mla', 'sg_ln_g', 'sg_ln_b',
           'sg_w', 'sg_b', 'w_br_sg', 'conv_w', 'w_br_conv', 'pool_w', 'pool_scale', 'w_br_pool', 'w_out',
           'norm_mix_post', 'norm_ffn_pre', 'w_ff1', 'w_ff2', 'norm_ffn_post']
COL_SHARDED = ['w_in', 'w_uq', 'w_ukv', 'w_br_mla', 'w_br_sg', 'w_br_conv', 'w_br_pool', 'w_ff1']
ROW_SHARDED = ['w_out', 'w_ff2']
MATMUL_SHARDED = ['w_in', 'w_uq', 'w_ukv', 'w_br_mla', 'w_br_sg', 'w_br_conv', 'w_br_pool', 'w_out', 'w_ff1', 'w_ff2']
BIG_SHARDED = ['w_in', 'w_ff1', 'w_ff2', 'w_out']
SMALL_SHARDED = ['w_uq', 'w_ukv', 'w_br_mla', 'w_br_sg', 'w_br_conv', 'w_br_pool']
SHARDED = MATMUL_SHARDED + ['conv_w']
REPLICATED = [n for n in WEIGHTS if n not in SHARDED]
PACK_COLS = 1024


def _params(sem, vmem=VMEM_LIMIT):
    return pltpu.CompilerParams(dimension_semantics=sem, vmem_limit_bytes=vmem)


def _mxu(a):
    return a.astype(MXU_DTYPE)


def _dot(a, b):
    return jnp.dot(_mxu(a), _mxu(b), preferred_element_type=F32)


def _dot_nt(a, b):
    return lax.dot_general(_mxu(a), _mxu(b), (((1,), (1,)), ((), ())), preferred_element_type=F32)


def _dot_tn(a, b):
    return lax.dot_general(_mxu(a), _mxu(b), (((0,), (0,)), ((), ())), preferred_element_type=F32)


def _rms(x, g):
    r = lax.rsqrt(jnp.mean(x * x, axis=-1, keepdims=True) + EPS)
    return x * r * g


def _rms_bwd(x, g, dy):
    r = lax.rsqrt(jnp.mean(x * x, axis=-1, keepdims=True) + EPS)
    xh = x * r
    gdy = dy * g
    dx = r * (gdy - xh * jnp.mean(gdy * xh, axis=-1, keepdims=True))
    return dx, jnp.sum(dy * xh, axis=0, keepdims=True)


_GELU_C = math.sqrt(2.0 / math.pi)


def _gelu(x):
    t = jnp.tanh(_GELU_C * (x + 0.044715 * (x * x * x)))
    return x * (0.5 * (1.0 + t)), t


def _gelu_grad(x, t):
    return 0.5 * (1.0 + t) + 0.5 * x * (1.0 - t * t) * (_GELU_C * (1.0 + 3.0 * 0.044715 * x * x))


def _sigmoid(x):
    return 1.0 / (1.0 + jnp.exp(-x))


def _full(shape):
    return pl.BlockSpec(shape, lambda *_: (0,) * len(shape))


def _rows(ts, width, col=0):
    return pl.BlockSpec((ts, width), lambda i: (i, col))


def _tile(n, pref):
    return min(n, pref)


def _mm(name, a, w, *, tm, tn, prologue=None, rows=()):
    m, k = a.shape
    n = w.shape[1]
    tm, tn = _tile(m, tm), _tile(n, tn)

    def body(a_ref, *rest):
        row_refs, w_ref, o_ref = rest[:len(rows)], rest[len(rows)], rest[len(rows) + 1]
        av = a_ref[...]
        if prologue is not None:
            av = prologue(av, *[r[...] for r in row_refs])
        o_ref[...] = _dot(av, w_ref[...])

    return pl.pallas_call(
        body, name=name, grid=(m // tm, n // tn),
        in_specs=[pl.BlockSpec((tm, k), lambda i, j: (i, 0))] + [pl.BlockSpec((1, k), lambda i, j: (0, 0)) for _ in rows]
        + [pl.BlockSpec((k, tn), lambda i, j: (0, j))],
        out_specs=pl.BlockSpec((tm, tn), lambda i, j: (i, j)),
        out_shape=jax.ShapeDtypeStruct((m, n), F32),
        compiler_params=_params(("parallel", "parallel")),
    )(a, *rows, w)


def _mm_tn(name, a, b, *, tm, tn, prologue=None, rows=()):
    m, k = a.shape
    n = b.shape[1]
    tm, tn = _tile(m, tm), _tile(n, tn)

    def body(a_ref, *rest):
        row_refs, b_ref, o_ref = rest[:len(rows)], rest[len(rows)], rest[len(rows) + 1]

        @pl.when(pl.program_id(1) == 0)
        def _():
            o_ref[...] = jnp.zeros_like(o_ref)

        av = a_ref[...]
        if prologue is not None:
            av = prologue(av, *[r[...] for r in row_refs])
        o_ref[...] += _dot_tn(av, b_ref[...])

    return pl.pallas_call(
        body, name=name, grid=(n // tn, m // tm),
        in_specs=[pl.BlockSpec((tm, k), lambda j, i: (i, 0))] + [pl.BlockSpec((1, k), lambda j, i: (0, 0)) for _ in rows]
        + [pl.BlockSpec((tm, tn), lambda j, i: (i, j))],
        out_specs=pl.BlockSpec((k, tn), lambda j, i: (0, j)),
        out_shape=jax.ShapeDtypeStruct((k, n), F32),
        compiler_params=_params(("parallel", "arbitrary")),
    )(a, *rows, b)


def _relu_sq(a):
    r = jnp.maximum(a, 0.0)
    return r * r


def _qkv_prep(proj, q_norm, kv_norm, wq, wkv, cq_tab, s_tab, cr_tab):
    s = proj.shape[0]
    ts = _tile(s, 512)
    hw = N_HEADS * HEAD_PAD

    def body(cq_ref, ckv_ref, kr_ref, krs_ref, gq_ref, gkv_ref, wq_ref, wkv_ref, ct_ref, st_ref, crt_ref,
             q_ref, k_ref, v_ref):
        ct, st, crt = ct_ref[...], st_ref[...], crt_ref[...]
        qn = _rms(cq_ref[...], gq_ref[...])
        qab = _dot(qn, wq_ref[...])
        kvn = _rms(ckv_ref[...], gkv_ref[...])
        kav = _dot(kvn, wkv_ref[...])
        k_rope = kr_ref[...] * crt + krs_ref[...] * st
        for h in range(N_HEADS):
            lo = h * HEAD_PAD
            q_ref[h] = (qab[:, lo:lo + HEAD_PAD] * ct + qab[:, hw + lo:hw + lo + HEAD_PAD] * st).astype(q_ref.dtype)
            k_ref[h] = (kav[:, lo:lo + HEAD_PAD] + k_rope).astype(k_ref.dtype)
            v_ref[h] = kav[:, hw + lo:hw + lo + HEAD_PAD].astype(v_ref.dtype)

    head_spec = pl.BlockSpec((N_HEADS, ts, HEAD_PAD), lambda i: (0, i, 0))
    head_shape = jax.ShapeDtypeStruct((N_HEADS, s, HEAD_PAD), MXU_DTYPE)
    return pl.pallas_call(
        body, name="qkv_prep", grid=(s // ts,),
        in_specs=[_rows(ts, Q_RANK, COL_B // Q_RANK), _rows(ts, KV_RANK, (COL_B + Q_RANK) // KV_RANK),
                  _rows(ts, HEAD_PAD, (COL_B + Q_RANK + KV_RANK) // HEAD_PAD),
                  _rows(ts, HEAD_PAD, (COL_B + Q_RANK + KV_RANK + HEAD_PAD) // HEAD_PAD),
                  _full((1, Q_RANK)), _full((1, KV_RANK)), _full((Q_RANK, 2 * hw)), _full((KV_RANK, 2 * hw)),
                  _rows(ts, HEAD_PAD), _rows(ts, HEAD_PAD), _rows(ts, HEAD_PAD)],
        out_specs=[head_spec, head_spec, head_spec],
        out_shape=[head_shape, head_shape, head_shape],
        compiler_params=_params(("parallel",)),
    )(proj, proj, proj, proj, q_norm, kv_norm, wq, wkv, cq_tab, s_tab, cr_tab)


def _diagonal_mask(t):
    return lax.broadcasted_iota(jnp.int32, (t, t), 1) <= lax.broadcasted_iota(jnp.int32, (t, t), 0)


def _attn_fwd(q, k, v):
    s = q.shape[1]
    t = _tile(s, 512)

    def body(q_ref, k_ref, v_ref, o_ref, lse_ref):
        i = pl.program_id(1)
        qv = q_ref[0]

        def step(j, carry, on_diagonal):
            m, l, acc = carry
            kj = k_ref[0, pl.ds(pl.multiple_of(j * t, t), t), :]
            vj = v_ref[0, pl.ds(pl.multiple_of(j * t, t), t), :]
            sc = _dot_nt(qv, kj) * ATTN_SCALE
            if on_diagonal:
                sc = jnp.where(_diagonal_mask(t), sc, -jnp.inf)
            m_new = jnp.maximum(m, jnp.max(sc, axis=1, keepdims=True))
            p = jnp.exp(sc - m_new)
            alpha = jnp.exp(m - m_new)
            return m_new, alpha * l + jnp.sum(p, axis=1, keepdims=True), alpha * acc + _dot(p, vj)

        init = (jnp.full((t, 1), -jnp.inf, F32), jnp.zeros((t, 1), F32), jnp.zeros((t, HEAD_PAD), F32))
        below = lax.fori_loop(0, i, functools.partial(step, on_diagonal=False), init)
        m, l, acc = step(i, below, True)
        o_ref[...] = acc / l
        lse_ref[0] = m + jnp.log(l)

    return pl.pallas_call(
        body, name="attn_fwd", grid=(N_HEADS, s // t),
        in_specs=[pl.BlockSpec((1, t, HEAD_PAD), lambda h, i: (h, i, 0)),
                  pl.BlockSpec((1, s, HEAD_PAD), lambda h, i: (h, 0, 0)),
                  pl.BlockSpec((1, s, HEAD_PAD), lambda h, i: (h, 0, 0))],
        out_specs=[pl.BlockSpec((t, HEAD_PAD), lambda h, i: (i, h)), pl.BlockSpec((1, t, 1), lambda h, i: (h, i, 0))],
        out_shape=[jax.ShapeDtypeStruct((s, N_HEADS * HEAD_PAD), F32), jax.ShapeDtypeStruct((N_HEADS, s, 1), F32)],
        compiler_params=_params(("parallel", "parallel")),
    )(q, k, v)


def _attn_bwd(q, k, v, do, lse, delta):
    s = q.shape[1]
    t = _tile(s, 512)
    nq = s // t

    def body(q_ref, do_ref, lse_ref, dl_ref, k_ref, v_ref, dq_ref, dk_ref, dv_ref):
        j = pl.program_id(1)

        @pl.when(j == 0)
        def _():
            dq_ref[...] = jnp.zeros_like(dq_ref)

        kj, vj = k_ref[0], v_ref[0]

        def step(i, carry, on_diagonal):
            dk, dv = carry
            rows = pl.ds(pl.multiple_of(i * t, t), t)
            qi, doi = q_ref[0, rows, :], do_ref[rows, :]
            sc = _dot_nt(qi, kj) * ATTN_SCALE
            if on_diagonal:
                sc = jnp.where(_diagonal_mask(t), sc, -jnp.inf)
            p = jnp.exp(sc - lse_ref[0, rows, :])
            dv = dv + _dot_tn(p, doi)
            dp = _dot_nt(doi, vj)
            ds = p * (dp - dl_ref[0, rows, :]) * ATTN_SCALE
            dk = dk + _dot_tn(ds, qi)
            dq_ref[0, rows, :] += _dot(ds, kj)
            return dk, dv

        zero = jnp.zeros((t, HEAD_PAD), F32)
        dk, dv = lax.fori_loop(j + 1, nq, functools.partial(step, on_diagonal=False), step(j, (zero, zero), True))
        dk_ref[0] = dk
        dv_ref[0] = dv

    whole = lambda w: pl.BlockSpec((1, s, w), lambda h, j: (h, 0, 0))
    tile = pl.BlockSpec((1, t, HEAD_PAD), lambda h, j: (h, j, 0))
    head_shape = jax.ShapeDtypeStruct((N_HEADS, s, HEAD_PAD), F32)
    return pl.pallas_call(
        body, name="attn_bwd", grid=(N_HEADS, nq),
        in_specs=[whole(HEAD_PAD), pl.BlockSpec((s, HEAD_PAD), lambda h, j: (0, h)), whole(1), whole(1), tile, tile],
        out_specs=[whole(HEAD_PAD), tile, tile],
        out_shape=[head_shape, head_shape, head_shape],
        compiler_params=_params(("parallel", "arbitrary")),
    )(q, do, lse, delta, k, v)


def _qkv_bwd(dq, dk, dv, proj, q_norm, kv_norm, wq, wkv, cq_tab, s_tab, cr_tab):
    s = proj.shape[0]
    ts = _tile(s, 256)
    hw = N_HEADS * HEAD_PAD

    def body(dq_ref, dk_ref, dv_ref, cq_ref, ckv_ref, gq_ref, gkv_ref, wq_ref, wkv_ref, ct_ref, st_ref, crt_ref,
             dpb_ref, dwq_ref, dwkv_ref, dgq_ref, dgkv_ref):
        @pl.when(pl.program_id(0) == 0)
        def _():
            for r in (dwq_ref, dwkv_ref, dgq_ref, dgkv_ref):
                r[...] = jnp.zeros_like(r)

        ct, st, crt = ct_ref[...], st_ref[...], crt_ref[...]
        dqs = [dq_ref[h] for h in range(N_HEADS)]
        dks = [dk_ref[h] for h in range(N_HEADS)]
        dqab = jnp.concatenate([d * ct for d in dqs] + [d * st for d in dqs], axis=1)
        dkav = jnp.concatenate(dks + [dv_ref[h] for h in range(N_HEADS)], axis=1)
        dk_sum = dks[0] + dks[1] + dks[2] + dks[3]
        cq, ckv, gq, gkv = cq_ref[...], ckv_ref[...], gq_ref[...], gkv_ref[...]
        dwq_ref[...] += _dot_tn(_rms(cq, gq), dqab)
        dwkv_ref[...] += _dot_tn(_rms(ckv, gkv), dkav)
        dcq, dgq = _rms_bwd(cq, gq, _dot_nt(dqab, wq_ref[...]))
        dckv, dgkv = _rms_bwd(ckv, gkv, _dot_nt(dkav, wkv_ref[...]))
        dgq_ref[...] += dgq
        dgkv_ref[...] += dgkv
        dpb_ref[...] = jnp.concatenate([dcq, dckv, dk_sum * crt, dk_sum * st], axis=1).astype(dpb_ref.dtype)

    head_spec = pl.BlockSpec((N_HEADS, ts, HEAD_PAD), lambda i: (0, i, 0))
    wb = Q_RANK + KV_RANK + 2 * HEAD_PAD
    return pl.pallas_call(
        body, name="qkv_bwd", grid=(s // ts,),
        in_specs=[head_spec, head_spec, head_spec,
                  _rows(ts, Q_RANK, COL_B // Q_RANK), _rows(ts, KV_RANK, (COL_B + Q_RANK) // KV_RANK),
                  _full((1, Q_RANK)), _full((1, KV_RANK)), _full((Q_RANK, 2 * hw)), _full((KV_RANK, 2 * hw)),
                  _rows(ts, HEAD_PAD), _rows(ts, HEAD_PAD), _rows(ts, HEAD_PAD)],
        out_specs=[_rows(ts, wb), _full((Q_RANK, 2 * hw)), _full((KV_RANK, 2 * hw)), _full((1, Q_RANK)), _full((1, KV_RANK))],
        out_shape=[jax.ShapeDtypeStruct((s, wb), MXU_DTYPE), jax.ShapeDtypeStruct((Q_RANK, 2 * hw), F32),
                   jax.ShapeDtypeStruct((KV_RANK, 2 * hw), F32), jax.ShapeDtypeStruct((1, Q_RANK), F32),
                   jax.ShapeDtypeStruct((1, KV_RANK), F32)],
        compiler_params=_params(("arbitrary",)),
    )(dq, dk, dv, proj, proj, q_norm, kv_norm, wq, wkv, cq_tab, s_tab, cr_tab)


def _lane_group(width):
    return lax.broadcasted_iota(jnp.int32, (1, width), 1) // (width // 4)


def _shift_down(a, k):
    return pltpu.roll(a, k, 0)


def _shift_up(a, k):
    return pltpu.roll(a, a.shape[0] - k, 0)


def _window_sums(xh, shift):
    s2 = xh + shift(xh, 1)
    s4 = s2 + shift(s2, 2)
    s8 = s4 + shift(s4, 4)
    s16 = s8 + shift(s8, 8)
    grp = _lane_group(xh.shape[1])
    return jnp.where(grp == 0, s2, jnp.where(grp == 1, s4, jnp.where(grp == 2, s8, s16)))


def _pool_count(i, ts):
    grp = _lane_group(BR_WIDTH)
    win = jnp.where(grp == 0, 2.0, jnp.where(grp == 1, 4.0, jnp.where(grp == 2, 8.0, 16.0)))
    t = (i * ts + lax.broadcasted_iota(jnp.int32, (ts, 1), 0)).astype(F32)
    return jnp.minimum(t + 1.0, win)


def _mix_forward(i, ts, r):
    f = {}
    f['gates'] = _sigmoid(r['gate'][...] + r['gate_b'][...])
    sgu, sgv = r['sgu'][...], r['sgv'][...]
    f['sgu'], f['sgv'] = sgu, sgv
    u_act, f['tu'] = _gelu(sgu)
    vg, f['tv'] = _gelu(sgv)
    mu = jnp.mean(vg, axis=-1, keepdims=True)
    xc = vg - mu
    f['ln_r'] = lax.rsqrt(jnp.mean(xc * xc, axis=-1, keepdims=True) + EPS)
    f['ln_xh'] = xc * f['ln_r']
    vln = f['ln_xh'] * r['ln_g'][...] + r['ln_b'][...]
    tril = lax.broadcasted_iota(jnp.int32, (SG_CHUNK, SG_CHUNK), 1) <= lax.broadcasted_iota(jnp.int32, (SG_CHUNK, SG_CHUNK), 0)
    f['wm'] = [_mxu(jnp.where(tril, r['sg_w'][g], 0.0)) for g in range(SG_GROUPS)]
    f['tril'] = tril
    grp = _lane_group(BR_WIDTH)
    bias = r['sg_bias'][...]
    parts = []
    for ci in range(ts // SG_CHUNK):
        vc = vln[ci * SG_CHUNK:(ci + 1) * SG_CHUNK]
        sc = bias
        for g in range(SG_GROUPS):
            sc = sc + jnp.where(grp == g, _dot(f['wm'][g], vc), 0.0)
        parts.append(sc)
    f['vln'] = vln
    f['sg_s'] = parts[0] if len(parts) == 1 else jnp.concatenate(parts, axis=0)
    f['u_act'] = u_act
    out_b = u_act * f['sg_s']
    first = (i > 0).astype(F32)
    cvx, cvc, cvb = r['cvx'][...], r['cvc'][...], r['cvb'][...]
    f['cvx'], f['cvc'], f['cvb'] = cvx, cvc, cvb
    zh = jnp.concatenate([r['hx'][...] * r['hc'][...] * first, cvc * cvx], axis=0)
    f['z1'] = _shift_down(zh, 1)[HALO:]
    f['z2'] = _shift_down(zh, 2)[HALO:]
    f['z0'] = zh[HALO:]
    f['yv'] = r['conv_w'][0:1, :] * f['z2'] + r['conv_w'][1:2, :] * f['z1'] + r['conv_w'][2:3, :] * f['z0']
    out_c = cvb * f['yv']
    p = r['pool'][...]
    ph = jnp.concatenate([r['hp'][...] * first, p], axis=0)
    f['cnt'] = _pool_count(i, ts)
    f['pooled'] = _window_sums(ph, _shift_down)[HALO:] / f['cnt'] - p
    f['mixed'] = _dot(f['pooled'], r['wbd'][...])
    out_d = f['mixed'] * r['pool_scale'][...]
    f['outs'] = [r['o'][...], out_b, out_c, out_d]
    f['ys'] = [_dot(f['outs'][b], r['w_br'][b][...]) for b in range(N_BRANCH)]
    merged = f['gates'][:, 0:D_MODEL] * f['ys'][0]
    for b in range(1, N_BRANCH):
        merged = merged + f['gates'][:, b * D_MODEL:(b + 1) * D_MODEL] * f['ys'][b]
    f['merged'] = merged
    f['mo'] = _dot(merged, r['w_out'][...])
    return f


_MIX_TILE_INPUTS = ['gate', 'sgu', 'sgv', 'cvb', 'cvx', 'cvc', 'pool', 'hx', 'hc', 'hp', 'o']
_MIX_WEIGHTS = ['gate_b', 'ln_g', 'ln_b', 'sg_w', 'sg_bias', 'conv_w', 'wbd', 'pool_scale', 'w_br0', 'w_br1', 'w_br2',
                'w_br3', 'w_out', 'g_post']


def _mix_specs(s, ts):
    c0 = COL_M1 // BR_WIDTH
    prev = lambda col: pl.BlockSpec((HALO, BR_WIDTH), lambda i: (jnp.maximum(i * (ts // HALO) - 1, 0), col))
    tiles = [_rows(ts, N_BRANCH * D_MODEL, 0), _rows(ts, BR_WIDTH, c0), _rows(ts, BR_WIDTH, c0 + 1), _rows(ts, BR_WIDTH, c0 + 2),
             _rows(ts, BR_WIDTH, c0 + 3), _rows(ts, BR_WIDTH, c0 + 4), _rows(ts, BR_WIDTH, c0 + 5),
             prev(c0 + 3), prev(c0 + 4), prev(c0 + 5), _rows(ts, N_HEADS * HEAD_PAD)]
    weights = [_full((1, N_BRANCH * D_MODEL)), _full((1, BR_WIDTH)), _full((1, BR_WIDTH)),
               _full((SG_GROUPS, SG_CHUNK, SG_CHUNK)), _full((SG_CHUNK, BR_WIDTH)), _full((8, BR_WIDTH)),
               _full((BR_WIDTH, BR_WIDTH)), _full((1, BR_WIDTH)), _full((N_HEADS * HEAD_PAD, D_MODEL)),
               _full((BR_WIDTH, D_MODEL)), _full((BR_WIDTH, D_MODEL)), _full((BR_WIDTH, D_MODEL)),
               _full((D_MODEL, D_MODEL)), _full((1, D_MODEL))]
    return tiles, weights


def _mix_refs(refs):
    names = _MIX_TILE_INPUTS + _MIX_WEIGHTS
    r = dict(zip(names, refs[:len(names)]))
    r['w_br'] = [r['w_br0'], r['w_br1'], r['w_br2'], r['w_br3']]
    return r, refs[len(names):]


def _mix_operands(proj, o, lw):
    return ([proj] * 10 + [o] + [lw[n] for n in ['gate_b', 'sg_ln_g', 'sg_ln_b', 'sg_w', 'sg_bias', 'conv_w8', 'wbd',
                                                 'pool_scale', 'w_br_mla_p', 'w_br_sg', 'w_br_conv', 'w_br_pool', 'w_out',
                                                 'norm_mix_post']])


def _mix_fwd(x0, proj, o, lw):
    s = x0.shape[0]
    ts = _tile(s, 256)
    tiles, weights = _mix_specs(s, ts)

    def body(*refs):
        r, (x0_ref, x1_ref) = _mix_refs(refs)
        f = _mix_forward(pl.program_id(0), ts, r)
        x1_ref[...] = x0_ref[...] + _rms(f['mo'], r['g_post'][...])

    return pl.pallas_call(
        body, name="mix_fwd", grid=(s // ts,),
        in_specs=tiles + weights + [_rows(ts, D_MODEL)],
        out_specs=_rows(ts, D_MODEL),
        out_shape=jax.ShapeDtypeStruct((s, D_MODEL), F32),
        compiler_params=_params(("parallel",)),
    )(*_mix_operands(proj, o, lw), x0)


def _mix_bwd(dx1, proj, o, lw):
    s = dx1.shape[0]
    ts = _tile(s, 128)
    tiles, weights = _mix_specs(s, ts)
    hw = N_HEADS * HEAD_PAD

    def body(*refs):
        r, rest = _mix_refs(refs)
        (dx1_ref, dg_ref, dm1_ref, dyv_ref, up_ref, do_ref, delta_ref,
         dgate_b_ref, dln_g_ref, dln_b_ref, dsgw_ref, dsgb_ref, dconv_ref, dwbd_ref, dps_ref,
         dwbr0_ref, dwbr1_ref, dwbr2_ref, dwbr3_ref, dwout_ref, dgpost_ref, dbias_acc) = rest
        i = pl.program_id(0)
        acc_refs = [dgate_b_ref, dln_g_ref, dln_b_ref, dsgw_ref, dsgb_ref, dconv_ref, dwbd_ref, dps_ref,
                    dwbr0_ref, dwbr1_ref, dwbr2_ref, dwbr3_ref, dwout_ref, dgpost_ref, dbias_acc]

        @pl.when(i == 0)
        def _():
            for a in acc_refs:
                a[...] = jnp.zeros_like(a)

        f = _mix_forward(i, ts, r)
        dmo, dgpost = _rms_bwd(f['mo'], r['g_post'][...], dx1_ref[...])
        dgpost_ref[...] += dgpost
        dwout_ref[...] += _dot_tn(f['merged'], dmo)
        dmerged = _dot_nt(dmo, r['w_out'][...])
        dwbr = [dwbr0_ref, dwbr1_ref, dwbr2_ref, dwbr3_ref]
        douts = []
        for b in range(N_BRANCH):
            gb = f['gates'][:, b * D_MODEL:(b + 1) * D_MODEL]
            dgate = dmerged * f['ys'][b] * gb * (1.0 - gb)
            dg_ref[:, b * D_MODEL:(b + 1) * D_MODEL] = dgate.astype(dg_ref.dtype)
            dgate_b_ref[:, b * D_MODEL:(b + 1) * D_MODEL] += jnp.sum(dgate, axis=0, keepdims=True)
            dy = dmerged * gb
            dwbr[b][...] += _dot_tn(f['outs'][b], dy)
            douts.append(_dot_nt(dy, r['w_br'][b][...]))
        do = douts[0]
        do_ref[...] = do.astype(do_ref.dtype)
        prod = do * f['outs'][0]
        for h in range(N_HEADS):
            delta_ref[h] = jnp.sum(prod[:, h * HEAD_PAD:(h + 1) * HEAD_PAD], axis=1, keepdims=True)
        grp = _lane_group(BR_WIDTH)
        ds = douts[1] * f['u_act']
        dsgu = douts[1] * f['sg_s'] * _gelu_grad(f['sgu'], f['tu'])
        dvln_parts = []
        for ci in range(ts // SG_CHUNK):
            rows = slice(ci * SG_CHUNK, (ci + 1) * SG_CHUNK)
            ds_c, vln_c = ds[rows], f['vln'][rows]
            dvln_c = jnp.zeros((SG_CHUNK, BR_WIDTH), F32)
            for g in range(SG_GROUPS):
                dvln_c = dvln_c + jnp.where(grp == g, _dot_tn(f['wm'][g], ds_c), 0.0)
                dsgw_ref[g] += jnp.where(f['tril'], _dot_nt(jnp.where(grp == g, ds_c, 0.0), vln_c), 0.0)
            dbias_acc[...] += ds_c
            dvln_parts.append(dvln_c)
        dvln = dvln_parts[0] if len(dvln_parts) == 1 else jnp.concatenate(dvln_parts, axis=0)
        dln_g_ref[...] += jnp.sum(dvln * f['ln_xh'], axis=0, keepdims=True)
        dln_b_ref[...] += jnp.sum(dvln, axis=0, keepdims=True)
        dxh = dvln * r['ln_g'][...]
        dvg = f['ln_r'] * (dxh - jnp.mean(dxh, axis=-1, keepdims=True)
                           - f['ln_xh'] * jnp.mean(dxh * f['ln_xh'], axis=-1, keepdims=True))
        dsgv = dvg * _gelu_grad(f['sgv'], f['tv'])
        dcvb = douts[2] * f['yv']
        dyv = douts[2] * f['cvb']
        dyv_ref[...] = dyv
        for kk, zk in enumerate((f['z2'], f['z1'], f['z0'])):
            dconv_ref[kk:kk + 1, :] += jnp.sum(dyv * zk, axis=0, keepdims=True)
        dps_ref[...] += jnp.sum(douts[3] * f['mixed'], axis=0, keepdims=True)
        dmixed = douts[3] * r['pool_scale'][...]
        dwbd_ref[...] += _dot_tn(f['pooled'], dmixed)
        up_ref[...] = _dot_nt(dmixed, r['wbd'][...]) / f['cnt']
        dm1_ref[...] = jnp.concatenate([dsgu, dsgv, dcvb], axis=1).astype(dm1_ref.dtype)

        @pl.when(i == pl.num_programs(0) - 1)
        def _():
            lane = lax.broadcasted_iota(jnp.int32, (1, SG_CHUNK), 1)
            db = dbias_acc[...]
            out = jnp.zeros((SG_CHUNK, SG_CHUNK), F32)
            for g in range(SG_GROUPS):
                out = out + jnp.where(lane == g, jnp.sum(jnp.where(grp == g, db, 0.0), axis=1, keepdims=True), 0.0)
            dsgb_ref[...] = out

    acc = lambda shape: (_full(shape), jax.ShapeDtypeStruct(shape, F32))
    accs = [acc((1, N_BRANCH * D_MODEL)), acc((1, BR_WIDTH)), acc((1, BR_WIDTH)), acc((SG_GROUPS, SG_CHUNK, SG_CHUNK)),
            acc((SG_CHUNK, SG_CHUNK)), acc((8, BR_WIDTH)), acc((BR_WIDTH, BR_WIDTH)), acc((1, BR_WIDTH)),
            acc((hw, D_MODEL)), acc((BR_WIDTH, D_MODEL)), acc((BR_WIDTH, D_MODEL)), acc((BR_WIDTH, D_MODEL)),
            acc((D_MODEL, D_MODEL)), acc((1, D_MODEL))]
    tile_outs = [(_rows(ts, N_BRANCH * D_MODEL), jax.ShapeDtypeStruct((s, N_BRANCH * D_MODEL), MXU_DTYPE)),
                 (_rows(ts, 3 * BR_WIDTH), jax.ShapeDtypeStruct((s, 3 * BR_WIDTH), MXU_DTYPE)),
                 (_rows(ts, BR_WIDTH), jax.ShapeDtypeStruct((s, BR_WIDTH), F32)),
                 (_rows(ts, BR_WIDTH), jax.ShapeDtypeStruct((s, BR_WIDTH), F32)),
                 (_rows(ts, hw), jax.ShapeDtypeStruct((s, hw), MXU_DTYPE)),
                 (pl.BlockSpec((N_HEADS, ts, 1), lambda i: (0, i, 0)), jax.ShapeDtypeStruct((N_HEADS, s, 1), F32))]
    outs = tile_outs + accs
    return pl.pallas_call(
        body, name="mix_bwd", grid=(s // ts,),
        in_specs=tiles + weights + [_rows(ts, D_MODEL)],
        out_specs=[o_[0] for o_ in outs], out_shape=[o_[1] for o_ in outs],
        scratch_shapes=[pltpu.VMEM((SG_CHUNK, BR_WIDTH), F32)],
        compiler_params=_params(("arbitrary",), 60 * 1024 * 1024),
    )(*_mix_operands(proj, o, lw), dx1)


def _shift_bwd(dyv, upool, proj, conv_w8):
    s = dyv.shape[0]
    ts = _tile(s, 512)
    nb = s // HALO
    c0 = COL_M1 // BR_WIDTH

    def body(dyv_ref, dyvn_ref, up_ref, upn_ref, cvx_ref, cvc_ref, cw_ref, out_ref):
        i = pl.program_id(0)
        last = (i < pl.num_programs(0) - 1).astype(F32)
        dh = jnp.concatenate([dyv_ref[...], dyvn_ref[...] * last], axis=0)
        dz = (cw_ref[2:3, :] * dh + cw_ref[1:2, :] * _shift_up(dh, 1) + cw_ref[0:1, :] * _shift_up(dh, 2))[:ts]
        up = up_ref[...]
        uh = jnp.concatenate([up, upn_ref[...] * last], axis=0)
        dpool = _window_sums(uh, _shift_up)[:ts] - up * _pool_count(i, ts)
        out_ref[...] = jnp.concatenate([dz * cvc_ref[...], dz * cvx_ref[...], dpool], axis=1).astype(out_ref.dtype)

    nxt = pl.BlockSpec((HALO, BR_WIDTH), lambda i: (jnp.minimum((i + 1) * (ts // HALO), nb - 1), 0))
    return pl.pallas_call(
        body, name="shift_bwd", grid=(s // ts,),
        in_specs=[_rows(ts, BR_WIDTH), nxt, _rows(ts, BR_WIDTH), nxt, _rows(ts, BR_WIDTH, c0 + 3), _rows(ts, BR_WIDTH, c0 + 4),
                  _full((8, BR_WIDTH))],
        out_specs=_rows(ts, 3 * BR_WIDTH),
        out_shape=jax.ShapeDtypeStruct((s, 3 * BR_WIDTH), MXU_DTYPE),
        compiler_params=_params(("parallel",)),
    )(dyv, dyv, upool, upool, proj, proj, conv_w8)


def _ffn2(a, w2, x1, g):
    s = a.shape[0]
    ts = _tile(s, 256)

    def body(a_ref, w_ref, x1_ref, g_ref, x2_ref, f_ref):
        f = _dot(_relu_sq(a_ref[...]), w_ref[...])
        f_ref[...] = f
        x2_ref[...] = x1_ref[...] + _rms(f, g_ref[...])

    return pl.pallas_call(
        body, name="ffn2", grid=(s // ts,),
        in_specs=[_rows(ts, D_FF), _full((D_FF, D_MODEL)), _rows(ts, D_MODEL), _full((1, D_MODEL))],
        out_specs=[_rows(ts, D_MODEL), _rows(ts, D_MODEL)],
        out_shape=[jax.ShapeDtypeStruct((s, D_MODEL), F32)] * 2,
        compiler_params=_params(("parallel",)),
    )(a, w2, x1, g)


def _ffn2_bwd(dx2, f, g, a, w2t):
    s = a.shape[0]
    ts = _tile(s, 256)

    def body(dx2_ref, f_ref, g_ref, a_ref, w_ref, df_ref, da_ref, dg_ref):
        @pl.when(pl.program_id(0) == 0)
        def _():
            dg_ref[...] = jnp.zeros_like(dg_ref)

        df, dg = _rms_bwd(f_ref[...], g_ref[...], dx2_ref[...])
        dg_ref[...] += dg
        df_ref[...] = df.astype(df_ref.dtype)
        da_ref[...] = (_dot(df, w_ref[...]) * (2.0 * jnp.maximum(a_ref[...], 0.0))).astype(da_ref.dtype)

    return pl.pallas_call(
        body, name="ffn2_bwd", grid=(s // ts,),
        in_specs=[_rows(ts, D_MODEL), _rows(ts, D_MODEL), _full((1, D_MODEL)), _rows(ts, D_FF), _full((D_MODEL, D_FF))],
        out_specs=[_rows(ts, D_MODEL), _rows(ts, D_FF), _full((1, D_MODEL))],
        out_shape=[jax.ShapeDtypeStruct((s, D_MODEL), MXU_DTYPE), jax.ShapeDtypeStruct((s, D_FF), MXU_DTYPE),
                   jax.ShapeDtypeStruct((1, D_MODEL), F32)],
        compiler_params=_params(("arbitrary",)),
    )(dx2, f, g, a, w2t)


def _norm_in_bwd(name, pieces, x, g, dres):
    s = x.shape[0]
    ts = _tile(s, 256)
    n = len(pieces)

    def body(*refs):
        d_refs, w_refs = refs[:n], refs[n:2 * n]
        x_ref, g_ref, dres_ref, dx_ref, dg_ref = refs[2 * n:]

        @pl.when(pl.program_id(0) == 0)
        def _():
            dg_ref[...] = jnp.zeros_like(dg_ref)

        dh = _dot(d_refs[0][...], w_refs[0][...])
        for p in range(1, n):
            dh = dh + _dot(d_refs[p][...], w_refs[p][...])
        dx, dg = _rms_bwd(x_ref[...], g_ref[...], dh)
        dg_ref[...] += dg
        dx_ref[...] = dres_ref[...] + dx

    return pl.pallas_call(
        body, name=name, grid=(s // ts,),
        in_specs=[_rows(ts, d.shape[1]) for d, _ in pieces] + [_full(w.shape) for _, w in pieces]
        + [_rows(ts, D_MODEL), _full((1, D_MODEL)), _rows(ts, D_MODEL)],
        out_specs=[_rows(ts, D_MODEL), _full((1, D_MODEL))],
        out_shape=[jax.ShapeDtypeStruct((s, D_MODEL), F32), jax.ShapeDtypeStruct((1, D_MODEL), F32)],
        compiler_params=_params(("arbitrary",)),
    )(*[d for d, _ in pieces], *[w for _, w in pieces], x, g, dres)


def _loss_and_grad(y, target):
    s = y.shape[0]
    ts = _tile(s, 512)

    def body(y_ref, t_ref, dy_ref, loss_ref):
        @pl.when(pl.program_id(0) == 0)
        def _():
            loss_ref[...] = jnp.zeros_like(loss_ref)

        err = y_ref[...] - t_ref[...]
        dy_ref[...] = err * (1.0 / D_MODEL)
        loss_ref[...] += 0.5 * jnp.sum(jnp.mean(err * err, axis=-1, keepdims=True), axis=0, keepdims=True)

    dy, loss = pl.pallas_call(
        body, name="loss", grid=(s // ts,),
        in_specs=[_rows(ts, D_MODEL), _rows(ts, D_MODEL)],
        out_specs=[_rows(ts, D_MODEL), _full((8, 128))],
        out_shape=[jax.ShapeDtypeStruct((s, D_MODEL), F32), jax.ShapeDtypeStruct((8, 128), F32)],
        compiler_params=_params(("arbitrary",)),
    )(y, target)
    return loss[0, 0], dy


_W_IN_SPLITS = [256, 384, 416, 672, 928, 1184, 1440, 1696, 1952]


def _rope_swap(w):
    half = QK_ROPE // 2
    return jnp.concatenate([-w[..., half:], w[..., :half]], axis=-1)


def _rope_unswap(d):
    half = QK_ROPE // 2
    return jnp.concatenate([d[..., half:], -d[..., :half]], axis=-1)


def _zeros_like_cols(w, n):
    return jnp.zeros(w.shape[:-1] + (n,), w.dtype)


def _derive_weights(w):
    md = MXU_DTYPE
    nl = w['w_in'].shape[0]
    c_q, c_kv, k_r, sg_u, sg_v, cv_x, cv_b, cv_c, pool, gate = jnp.split(w['w_in'].astype(md), _W_IN_SPLITS, axis=-1)
    pad_rope = lambda r: jnp.concatenate([_zeros_like_cols(r, QK_NOPE), r, _zeros_like_cols(r, HEAD_PAD - QK_NOPE - QK_ROPE)], -1)
    w_in_p = jnp.concatenate([gate, sg_u, sg_v, cv_b, cv_x, cv_c, pool, c_q, c_kv, pad_rope(k_r), pad_rope(_rope_swap(k_r))], -1)
    wq = w['w_uq'].astype(md).reshape(nl, Q_RANK, N_HEADS, QK_NOPE + QK_ROPE)
    nope, rope_w = wq[..., :QK_NOPE], wq[..., QK_NOPE:]
    wq_a = jnp.concatenate([nope, rope_w, _zeros_like_cols(nope, 32)], -1).reshape(nl, Q_RANK, N_HEADS * HEAD_PAD)
    wq_b = pad_rope(_rope_swap(rope_w)).reshape(nl, Q_RANK, N_HEADS * HEAD_PAD)
    wkv = w['w_ukv'].astype(md).reshape(nl, KV_RANK, N_HEADS, QK_NOPE + V_HEAD)
    pad_half = lambda r: jnp.concatenate([r, _zeros_like_cols(r, HEAD_PAD - r.shape[-1])], -1).reshape(nl, KV_RANK, N_HEADS * HEAD_PAD)
    w_br_mla = w['w_br_mla'].astype(md).reshape(nl, N_HEADS, V_HEAD, D_MODEL)
    w_br_mla_p = jnp.concatenate([w_br_mla, jnp.zeros_like(w_br_mla)], axis=2).reshape(nl, N_HEADS * HEAD_PAD, D_MODEL)
    eye = jnp.eye(4, dtype=md)
    wbd = (w['pool_w'].astype(md)[:, :, :, None, :] * eye[None, :, None, :, None]).reshape(nl, BR_WIDTH, BR_WIDTH)
    row = lambda a: a.astype(F32)[:, None, :]
    w_in_pt = jnp.swapaxes(w_in_p, 1, 2)
    return dict(
        w_in_p=w_in_p, wt_g=w_in_pt[:, COL_G:COL_M1], wt_m1=w_in_pt[:, COL_M1:COL_M2], wt_m2=w_in_pt[:, COL_M2:COL_B],
        wt_b=w_in_pt[:, COL_B:],
        wq=jnp.concatenate([wq_a, wq_b], -1), wkv=jnp.concatenate([pad_half(wkv[..., :QK_NOPE]), pad_half(wkv[..., QK_NOPE:])], -1),
        w_br_mla_p=w_br_mla_p, w_br_sg=w['w_br_sg'].astype(md), w_br_conv=w['w_br_conv'].astype(md),
        w_br_pool=w['w_br_pool'].astype(md), wbd=wbd, w_out=w['w_out'].astype(md),
        w_ff1=w['w_ff1'].astype(md), w_ff1t=jnp.swapaxes(w['w_ff1'].astype(md), 1, 2),
        w_ff2=w['w_ff2'].astype(md), w_ff2t=jnp.swapaxes(w['w_ff2'].astype(md), 1, 2),
        norm_mix_pre=row(w['norm_mix_pre']), gate_b=row(w['gate_b']), q_norm=row(w['q_norm']), kv_norm=row(w['kv_norm']),
        sg_ln_g=row(w['sg_ln_g']), sg_ln_b=row(w['sg_ln_b']), sg_w=w['sg_w'].astype(F32),
        sg_bias=jnp.repeat(jnp.swapaxes(w['sg_b'].astype(F32), 1, 2), BR_WIDTH // SG_GROUPS, axis=2),
        conv_w8=jnp.pad(w['conv_w'].astype(F32), ((0, 0), (0, 5), (0, 0))), pool_scale=row(w['pool_scale']),
        norm_mix_post=row(w['norm_mix_post']), norm_ffn_pre=row(w['norm_ffn_pre']), norm_ffn_post=row(w['norm_ffn_post']),
    )


def _rope_tables(positions):
    inv_freq = ROPE_BASE ** (-jnp.arange(0, QK_ROPE, 2, dtype=F32) / QK_ROPE)
    ang = positions.astype(F32)[:, None] * inv_freq
    cos, sin = jnp.cos(ang), jnp.sin(ang)
    n = positions.shape[0]
    ones, z64, z32 = jnp.ones((n, QK_NOPE), F32), jnp.zeros((n, QK_NOPE), F32), jnp.zeros((n, 32), F32)
    return (jnp.concatenate([ones, cos, cos, z32], 1), jnp.concatenate([z64, sin, sin, z32], 1),
            jnp.concatenate([z64, cos, cos, z32], 1))


def _reference_layout_grads(g):
    gate, dm1, dm2, dpb = g['dw_in_pieces']
    nl = gate.shape[0]
    sg_u, sg_v, cv_b = jnp.split(dm1, 3, axis=-1)
    cv_x, cv_c, pool = jnp.split(dm2, 3, axis=-1)
    c_q, c_kv, kr, krs = jnp.split(dpb, [Q_RANK, Q_RANK + KV_RANK, Q_RANK + KV_RANK + HEAD_PAD], axis=-1)
    rope_cols = slice(QK_NOPE, QK_NOPE + QK_ROPE)
    k_r = kr[..., rope_cols] + _rope_unswap(krs[..., rope_cols])
    w_in = jnp.concatenate([c_q, c_kv, k_r, sg_u, sg_v, cv_x, cv_b, cv_c, pool, gate], -1)
    hw = N_HEADS * HEAD_PAD
    dqa = g['dwq'][..., :hw].reshape(nl, Q_RANK, N_HEADS, HEAD_PAD)
    dqb = g['dwq'][..., hw:].reshape(nl, Q_RANK, N_HEADS, HEAD_PAD)
    w_uq = jnp.concatenate([dqa[..., :QK_NOPE], dqa[..., rope_cols] + _rope_unswap(dqb[..., rope_cols])], -1)
    dka = g['dwkv'][..., :hw].reshape(nl, KV_RANK, N_HEADS, HEAD_PAD)
    dva = g['dwkv'][..., hw:].reshape(nl, KV_RANK, N_HEADS, HEAD_PAD)
    w_ukv = jnp.concatenate([dka[..., :QK_NOPE], dva[..., :V_HEAD]], -1)
    w_br_mla = g['dw_br_mla_p'].reshape(nl, N_HEADS, HEAD_PAD, D_MODEL)[:, :, :V_HEAD]
    dwbd = g['dwbd'].reshape(nl, 4, 64, 4, 64)
    pool_w = jnp.stack([dwbd[:, k, :, k, :] for k in range(4)], axis=1)
    sq = lambda a: a[:, 0, :]
    return dict(
        norm_mix_pre=sq(g['dg_pre']), w_in=w_in, gate_b=sq(g['dgate_b']), q_norm=sq(g['dq_norm']),
        w_uq=w_uq.reshape(nl, Q_RANK, -1), kv_norm=sq(g['dkv_norm']), w_ukv=w_ukv.reshape(nl, KV_RANK, -1),
        w_br_mla=w_br_mla.reshape(nl, N_HEADS * V_HEAD, D_MODEL), sg_ln_g=sq(g['dln_g']), sg_ln_b=sq(g['dln_b']),
        sg_w=g['dsg_w'], sg_b=jnp.swapaxes(g['dsg_b'][:, :, :SG_GROUPS], 1, 2), w_br_sg=g['dw_br_sg'],
        conv_w=g['dconv_w'][:, :3], w_br_conv=g['dw_br_conv'], pool_w=pool_w, pool_scale=sq(g['dpool_scale']),
        w_br_pool=g['dw_br_pool'], w_out=g['dw_out'], norm_mix_post=sq(g['dg_post']), norm_ffn_pre=sq(g['dg_fpre']),
        w_ff1=g['dw_ff1'], w_ff2=g['dw_ff2'], norm_ffn_post=sq(g['dg_fpost']))


def _layer_forward(x0, lw, tabs):
    proj = _mm("in_proj", x0, lw['w_in_p'], tm=512, tn=896, prologue=_rms, rows=(lw['norm_mix_pre'],))
    q, k, v = _qkv_prep(proj, lw['q_norm'], lw['kv_norm'], lw['wq'], lw['wkv'], *tabs)
    o, lse = _attn_fwd(q, k, v)
    x1 = _mix_fwd(x0, proj, o, lw)
    a = _mm("ffn1", x1, lw['w_ff1'], tm=512, tn=1024, prologue=_rms, rows=(lw['norm_ffn_pre'],))
    x2, f = _ffn2(a, lw['w_ff2'], x1, lw['norm_ffn_post'])
    return x2, dict(x0=x0, proj=proj, q=q, k=k, v=v, o=o, lse=lse, x1=x1, a=a, f=f)


def _layer_backward(dx2, lw, sv, tabs):
    g = {}
    df, da, g['dg_fpost'] = _ffn2_bwd(dx2, sv['f'], lw['norm_ffn_post'], sv['a'], lw['w_ff2t'])
    g['dw_ff2'] = _mm_tn("dw_ff2", sv['a'], df, tm=256, tn=512, prologue=_relu_sq)
    dx1, g['dg_fpre'] = _norm_in_bwd("ffn1_bwd", [(da, lw['w_ff1t'])], sv['x1'], lw['norm_ffn_pre'], dx2)
    g['dw_ff1'] = _mm_tn("dw_ff1", sv['x1'], da, tm=512, tn=1024, prologue=_rms, rows=(lw['norm_ffn_pre'],))
    (dgate, dm1, dyv, upool, do, delta, g['dgate_b'], g['dln_g'], g['dln_b'], g['dsg_w'], g['dsg_b'], g['dconv_w'],
     g['dwbd'], g['dpool_scale'], g['dw_br_mla_p'], g['dw_br_sg'], g['dw_br_conv'], g['dw_br_pool'], g['dw_out'],
     g['dg_post']) = _mix_bwd(dx1, sv['proj'], sv['o'], lw)
    dm2 = _shift_bwd(dyv, upool, sv['proj'], lw['conv_w8'])
    dq, dk, dv = _attn_bwd(sv['q'], sv['k'], sv['v'], do, sv['lse'], delta)
    dpb, g['dwq'], g['dwkv'], g['dq_norm'], g['dkv_norm'] = _qkv_bwd(
        dq, dk, dv, sv['proj'], lw['q_norm'], lw['kv_norm'], lw['wq'], lw['wkv'], *tabs)
    pieces = [(dgate, lw['wt_g']), (dm1, lw['wt_m1']), (dm2, lw['wt_m2']), (dpb, lw['wt_b'])]
    dx0, g['dg_pre'] = _norm_in_bwd("in_proj_bwd", pieces, sv['x0'], lw['norm_mix_pre'], dx1)
    g['dw_in_pieces'] = [_mm_tn("dw_in_%d" % n, sv['x0'], d, tm=512, tn=1024, prologue=_rms, rows=(lw['norm_mix_pre'],))
                         for n, (d, _) in enumerate(pieces)]
    return dx0, g


def _local_step(x, positions, target, layers, after_layer_backward):
    tabs = _rope_tables(positions)
    derived = [{n: a[0] for n, a in _derive_weights(w).items()} for w in layers]
    saved = []
    for lw in derived:
        x, sv = _layer_forward(x, lw, tabs)
        saved.append(sv)
    loss, dx = _loss_and_grad(x, target)
    for l in reversed(range(len(layers))):
        dx, g = _layer_backward(dx, derived[l], saved[l], tabs)
        lead = lambda a: [b[None] for b in a] if isinstance(a, list) else a[None]
        after_layer_backward(l, _reference_layout_grads({n: lead(a) for n, a in g.items()}))
    return loss, dx


HBM = pl.BlockSpec(memory_space=pl.ANY)


def _chip_peers():
    x, y = lax.axis_index("x"), lax.axis_index("y")
    return [(1 - x, y), (x, 1 - y), (1 - x, 1 - y)]


def _gather_chips(name, arrs):
    n = len(arrs)

    def body(*refs):
        ins, outs = refs[:n], refs[n:2 * n]
        send_sems, recv_sems, local_sems = refs[2 * n:]
        x, y, c = lax.axis_index("x"), lax.axis_index("y"), lax.axis_index("c")
        me = 2 * x + y
        local_copies, sends = [], []
        for k in range(n):
            local = pltpu.make_async_copy(ins[k], outs[k].at[me], local_sems.at[k])
            local.start()
            local_copies.append(local)
            for j, (px, py) in enumerate(_chip_peers()):
                cp = pltpu.make_async_remote_copy(src_ref=ins[k], dst_ref=outs[k].at[me], send_sem=send_sems.at[3 * k + j],
                                                  recv_sem=recv_sems.at[3 * k + j], device_id=(px, py, c), device_id_type=MESH)
                cp.start()
                sends.append(cp)
        for k in range(n):
            for j, (px, py) in enumerate(_chip_peers()):
                pltpu.make_async_remote_copy(src_ref=ins[k], dst_ref=outs[k].at[2 * px + py], send_sem=send_sems.at[3 * k + j],
                                             recv_sem=recv_sems.at[3 * k + j], device_id=(px, py, c),
                                             device_id_type=MESH).wait_recv()
        for cp in sends:
            cp.wait_send()
        for local in local_copies:
            local.wait()

    return pl.pallas_call(
        body, name=name,
        in_specs=[HBM] * n, out_specs=[HBM] * n,
        out_shape=[jax.ShapeDtypeStruct((4,) + a.shape, a.dtype) for a in arrs],
        scratch_shapes=[pltpu.SemaphoreType.DMA((3 * n,)), pltpu.SemaphoreType.DMA((3 * n,)), pltpu.SemaphoreType.DMA((n,))],
    )(*arrs)


def _half_rows(ref, which):
    h = ref.shape[1] // 2
    return ref.at[:, pl.ds(pl.multiple_of(which * h, 8), h), :]


def _sibling_halves(name, arrs):
    n = len(arrs)

    def body(*refs):
        ins, mine, theirs = refs[:n], refs[n:2 * n], refs[2 * n:3 * n]
        send_sems, recv_sems, local_sems = refs[3 * n:]
        x, y, c = lax.axis_index("x"), lax.axis_index("y"), lax.axis_index("c")
        copies = []
        for k in range(n):
            local = pltpu.make_async_copy(_half_rows(ins[k], c), mine[k], local_sems.at[k])
            local.start()
            cp = pltpu.make_async_remote_copy(src_ref=_half_rows(ins[k], 1 - c), dst_ref=theirs[k], send_sem=send_sems.at[k],
                                              recv_sem=recv_sems.at[k], device_id=(x, y, 1 - c), device_id_type=MESH)
            cp.start()
            copies += [cp, local]
        for cp in copies:
            cp.wait()

    halves = [jax.ShapeDtypeStruct((a.shape[0], a.shape[1] // 2, a.shape[2]), a.dtype) for a in arrs]
    out = pl.pallas_call(
        body, name=name, in_specs=[HBM] * n, out_specs=[HBM] * (2 * n), out_shape=halves + halves,
        scratch_shapes=[pltpu.SemaphoreType.DMA((n,)), pltpu.SemaphoreType.DMA((n,)), pltpu.SemaphoreType.DMA((n,))],
    )(*arrs)
    return out[:n], out[n:]


def _scatter_chips(name, arrs):
    n = len(arrs)

    def body(*refs):
        ins, outs = refs[:n], refs[n:2 * n]
        send_sems, recv_sems, local_sems = refs[2 * n:]
        x, y, c = lax.axis_index("x"), lax.axis_index("y"), lax.axis_index("c")
        me = 2 * x + y
        local_copies, sends = [], []
        for k in range(n):
            local = pltpu.make_async_copy(ins[k].at[me], outs[k].at[me], local_sems.at[k])
            local.start()
            local_copies.append(local)
            for j, (px, py) in enumerate(_chip_peers()):
                cp = pltpu.make_async_remote_copy(src_ref=ins[k].at[2 * px + py], dst_ref=outs[k].at[me],
                                                  send_sem=send_sems.at[3 * k + j], recv_sem=recv_sems.at[3 * k + j],
                                                  device_id=(px, py, c), device_id_type=MESH)
                cp.start()
                sends.append(cp)
        for k in range(n):
            for j, (px, py) in enumerate(_chip_peers()):
                pltpu.make_async_remote_copy(src_ref=ins[k].at[me], dst_ref=outs[k].at[2 * px + py],
                                             send_sem=send_sems.at[3 * k + j], recv_sem=recv_sems.at[3 * k + j],
                                             device_id=(px, py, c), device_id_type=MESH).wait_recv()
        for cp in sends:
            cp.wait_send()
        for local in local_copies:
            local.wait()

    return pl.pallas_call(
        body, name=name, in_specs=[HBM] * n, out_specs=[HBM] * n,
        out_shape=[jax.ShapeDtypeStruct(a.shape, a.dtype) for a in arrs],
        scratch_shapes=[pltpu.SemaphoreType.DMA((3 * n,)), pltpu.SemaphoreType.DMA((3 * n,)), pltpu.SemaphoreType.DMA((n,))],
    )(*arrs)


def _join_siblings(name, halves):
    n = len(halves)

    def body(*refs):
        ins, outs = refs[:n], refs[n:2 * n]
        send_sems, recv_sems, local_sems = refs[2 * n:]
        x, y, c = lax.axis_index("x"), lax.axis_index("y"), lax.axis_index("c")
        copies = []
        for k in range(n):
            rows = ins[k].shape[0]
            mine = outs[k].at[pl.ds(pl.multiple_of(c * rows, 8), rows), :]
            theirs = outs[k].at[pl.ds(pl.multiple_of((1 - c) * rows, 8), rows), :]
            local = pltpu.make_async_copy(ins[k], mine, local_sems.at[k])
            local.start()
            cp = pltpu.make_async_remote_copy(src_ref=ins[k], dst_ref=mine, send_sem=send_sems.at[k], recv_sem=recv_sems.at[k],
                                              device_id=(x, y, 1 - c), device_id_type=MESH)
            cp.start()
            arrival = pltpu.make_async_remote_copy(src_ref=ins[k], dst_ref=theirs, send_sem=send_sems.at[k],
                                                   recv_sem=recv_sems.at[k], device_id=(x, y, 1 - c), device_id_type=MESH)
            copies.append((local, cp, arrival))
        for local, cp, arrival in copies:
            arrival.wait_recv()
            cp.wait_send()
            local.wait()

    return pl.pallas_call(
        body, name=name, in_specs=[HBM] * n, out_specs=[HBM] * n,
        out_shape=[jax.ShapeDtypeStruct((2 * h.shape[0], h.shape[1]), h.dtype) for h in halves],
        scratch_shapes=[pltpu.SemaphoreType.DMA((n,)), pltpu.SemaphoreType.DMA((n,)), pltpu.SemaphoreType.DMA((n,))],
    )(*halves)


def _gather_all(name, a):
    def body(a_ref, out_ref, send_sems, recv_sems, local_sem):
        x, y, c = lax.axis_index("x"), lax.axis_index("y"), lax.axis_index("c")
        me = 4 * x + 2 * y + c
        flips = [(fx, fy, fc) for fx in (0, 1) for fy in (0, 1) for fc in (0, 1)][1:]
        peers = [(x ^ fx, y ^ fy, c ^ fc) for fx, fy, fc in flips]
        local = pltpu.make_async_copy(a_ref, out_ref.at[me], local_sem)
        local.start()
        sends = []
        for j, peer in enumerate(peers):
            cp = pltpu.make_async_remote_copy(src_ref=a_ref, dst_ref=out_ref.at[me], send_sem=send_sems.at[j],
                                              recv_sem=recv_sems.at[j], device_id=peer, device_id_type=MESH)
            cp.start()
            sends.append(cp)
        for j, (px, py, pc) in enumerate(peers):
            pltpu.make_async_remote_copy(src_ref=a_ref, dst_ref=out_ref.at[4 * px + 2 * py + pc], send_sem=send_sems.at[j],
                                         recv_sem=recv_sems.at[j], device_id=(px, py, pc), device_id_type=MESH).wait_recv()
        for cp in sends:
            cp.wait_send()
        local.wait()

    return pl.pallas_call(
        body, name=name, in_specs=[HBM], out_specs=HBM, out_shape=jax.ShapeDtypeStruct((8,) + a.shape, a.dtype),
        scratch_shapes=[pltpu.SemaphoreType.DMA((7,)), pltpu.SemaphoreType.DMA((7,)), pltpu.SemaphoreType.DMA],
    )(a)


def _rowwise_call(name, fn, slots, out_shapes, steps):
    n_in, n_out = [len(s) for s in slots], [len(o) for o in out_shapes]

    def spec(shape):
        if len(shape) == 3:
            return pl.BlockSpec((shape[0], shape[1] // steps, shape[2]), lambda i: (0, i, 0))
        return pl.BlockSpec((shape[0] // steps, shape[1]), lambda i: (i, 0))

    def body(*refs):
        ins, outs = refs[:sum(n_in)], refs[sum(n_in):]
        a = b = 0
        for k in range(len(slots)):
            for o_ref, val in zip(outs[b:b + n_out[k]], fn(*[r[...] for r in ins[a:a + n_in[k]]])):
                o_ref[...] = val
            a, b = a + n_in[k], b + n_out[k]

    flat_in = [arr for s in slots for arr in s]
    flat_out = [shp for o in out_shapes for shp in o]
    out = pl.pallas_call(body, name=name, grid=(steps,), in_specs=[spec(a.shape) for a in flat_in],
                         out_specs=[spec(s) for s in flat_out], out_shape=[jax.ShapeDtypeStruct(s, F32) for s in flat_out],
                         compiler_params=_params(("parallel",)))(*flat_in)
    grouped, b = [], 0
    for k in range(len(slots)):
        grouped.append(out[b:b + n_out[k]])
        b += n_out[k]
    return grouped


def _sum_in_order(a):
    acc = a[0]
    for k in range(1, a.shape[0]):
        acc = acc + a[k]
    return (acc,)


def _adamw_math(w, g, m, v):
    m_new = ADAM_B1 * m + (1.0 - ADAM_B1) * g
    v_new = ADAM_B2 * v + (1.0 - ADAM_B2) * (g * g)
    m_hat = m_new / (1.0 - ADAM_B1 ** ADAM_STEP)
    v_hat = v_new / (1.0 - ADAM_B2 ** ADAM_STEP)
    return -ADAM_LR * (m_hat / (jnp.sqrt(v_hat) + ADAM_EPS) + ADAM_WD * w), m_new, v_new


def _pack(arrs, rows_per_layer, dtype):
    nl = arrs[0].shape[0]
    flat = jnp.concatenate([a.astype(dtype).reshape(nl, -1) for a in arrs], axis=1)
    flat = jnp.pad(flat, ((0, 0), (0, rows_per_layer * PACK_COLS - flat.shape[1])))
    return flat.reshape(nl * rows_per_layer, PACK_COLS)


def _unpack(packed, shapes, rows_per_layer):
    nl = shapes[0][0]
    flat = packed.reshape(packed.shape[:-2] + (nl, rows_per_layer * PACK_COLS))
    out, off = [], 0
    for shp in shapes:
        size = math.prod(shp[1:])
        out.append(flat[..., off:off + size].reshape(packed.shape[:-2] + tuple(shp)))
        off += size
    return out


def _rows_needed(shapes, multiple):
    per_layer = sum(math.prod(s[1:]) for s in shapes)
    rows = -(-per_layer // PACK_COLS)
    return -(-rows // multiple) * multiple


def _full_weights(w):
    nl = w['w_in'].shape[0]
    conv_flat = w['conv_w'].reshape(-1, 128)
    conv_rows = conv_flat.shape[0]
    conv_flat = jnp.pad(conv_flat, ((0, -conv_rows % 8), (0, 0)))
    gathered = _gather_chips("gather_weights", [w[n].astype(MXU_DTYPE) for n in MATMUL_SHARDED] + [conv_flat])
    full = dict(w)
    for n, part in zip(MATMUL_SHARDED, gathered):
        if n in ROW_SHARDED:
            full[n] = jnp.swapaxes(part, 0, 1).reshape(nl, 4 * part.shape[2], part.shape[3])
        else:
            full[n] = jnp.transpose(part, (1, 2, 0, 3)).reshape(nl, part.shape[2], 4 * part.shape[3])
    conv = gathered[-1][:, :conv_rows].reshape((4,) + w['conv_w'].shape)
    full['conv_w'] = jnp.transpose(conv, (1, 2, 0, 3)).reshape(nl, 3, -1)
    return full


def _chip_major(n, g):
    nl = g.shape[0]
    if n in ROW_SHARDED:
        return jnp.swapaxes(g.reshape(nl, 4, g.shape[1] // 4, g.shape[2]), 0, 1)
    return jnp.transpose(g.reshape(nl, g.shape[1], 4, g.shape[2] // 4), (2, 0, 1, 3))


def kernel(x, positions, norm_mix_pre, w_in, gate_b, q_norm, w_uq, kv_norm, w_ukv, w_br_mla, sg_ln_g, sg_ln_b, sg_w, sg_b, w_br_sg, conv_w, w_br_conv, pool_w, pool_scale, w_br_pool, w_out, norm_mix_post, norm_ffn_pre, w_ff1, w_ff2, norm_ffn_post, loss_target, m_norm_mix_pre, m_w_in, m_gate_b, m_q_norm, m_w_uq, m_kv_norm, m_w_ukv, m_w_br_mla, m_sg_ln_g, m_sg_ln_b, m_sg_w, m_sg_b, m_w_br_sg, m_conv_w, m_w_br_conv, m_pool_w, m_pool_scale, m_w_br_pool, m_w_out, m_norm_mix_post, m_norm_ffn_pre, m_w_ff1, m_w_ff2, m_norm_ffn_post, v_norm_mix_pre, v_w_in, v_gate_b, v_q_norm, v_w_uq, v_kv_norm, v_w_ukv, v_w_br_mla, v_sg_ln_g, v_sg_ln_b, v_sg_w, v_sg_b, v_w_br_sg, v_conv_w, v_w_br_conv, v_pool_w, v_pool_scale, v_w_br_pool, v_w_out, v_norm_mix_post, v_norm_ffn_pre, v_w_ff1, v_w_ff2, v_norm_ffn_post):
    given = dict(locals())
    w = {n: given[n] for n in WEIGHTS}
    mom = {n: given['m_' + n] for n in WEIGHTS}
    var = {n: given['v_' + n] for n in WEIGHTS}
    nl = w['w_in'].shape[0]
    full = _full_weights(w)
    layers = [{n: full[n][l:l + 1] for n in WEIGHTS} for l in range(nl)]
    small_shapes = [(1,) + w[n].shape[1:] for n in SMALL_SHARDED]
    small_rows = _rows_needed(small_shapes, 256)
    reduced, local_small = {}, {}

    def reduce_layer(l, g):
        arrs = [_chip_major(n, g[n])[:, 0] for n in BIG_SHARDED]
        arrs.append(jnp.stack([_pack([_chip_major(n, g[n])[chip] for n in SMALL_SHARDED], small_rows, F32) for chip in range(4)]))
        mine, theirs = _sibling_halves("grads_to_sibling", arrs)
        chip_sum = _rowwise_call("add_sibling", lambda a, b: (a + b,), [[m_, t_] for m_, t_ in zip(mine, theirs)],
                                 [[m_.shape] for m_ in mine], 8)
        arrived = _scatter_chips("grads_to_chips", [s_[0] for s_ in chip_sum])
        halves = _rowwise_call("sum_chips", _sum_in_order, [[a_] for a_ in arrived], [[a_.shape[1:]] for a_ in arrived], 8)
        reduced[l] = _join_siblings("join_halves", [h_[0] for h_ in halves])
        local_small[l] = [g[n] for n in REPLICATED + ['conv_w']]

    loss, dx = _local_step(x[0], positions[0], loss_target[0], layers, reduce_layer)
    loss = lax.psum(loss, ("x", "y", "c"))

    grad, delta, new_m, new_v = {}, {}, {}, {}
    slots = []
    for k, n in enumerate(BIG_SHARDED):
        grad[n] = jnp.stack([reduced[l][k] for l in range(nl)])
        slots.append([a.reshape(-1, a.shape[-1]) for a in (w[n], grad[n], mom[n], var[n])])
    small_pack = lambda d: _pack([d[n] for n in SMALL_SHARDED], small_rows, F32)
    g_small = jnp.stack([reduced[l][-1] for l in range(nl)])
    slots.append([small_pack(w), g_small.reshape(-1, PACK_COLS), small_pack(mom), small_pack(var)])
    updated = _rowwise_call("adamw_sharded", _adamw_math, slots, [[s_[0].shape] * 3 for s_ in slots], 32)
    for k, n in enumerate(BIG_SHARDED):
        delta[n], new_m[n], new_v[n] = [a.reshape(w[n].shape) for a in updated[k]]
    sm_shapes = [w[n].shape for n in SMALL_SHARDED]
    for d, packed in zip((grad, delta, new_m, new_v), [g_small.reshape(-1, PACK_COLS)] + list(updated[-1])):
        d.update(zip(SMALL_SHARDED, _unpack(packed, sm_shapes, small_rows)))

    names = REPLICATED + ['conv_w']
    local = [jnp.concatenate([local_small[l][k] for l in range(nl)]) for k in range(len(names))]
    rows = _rows_needed([a.shape for a in local], 32)
    everyone = _gather_all("gather_small_grads", _pack(local, rows, F32))
    (summed,), = _rowwise_call("sum_devices", _sum_in_order, [[everyone]], [[everyone.shape[1:]]], 4)
    g_rep = _unpack(summed, [a.shape for a in local], rows)
    chip = 2 * lax.axis_index("x") + lax.axis_index("y")
    g_rep[-1] = lax.dynamic_slice_in_dim(g_rep[-1], chip * w['conv_w'].shape[2], w['conv_w'].shape[2], axis=2)
    rep_pack = lambda arrs: _pack(arrs, rows, F32)
    (rep_out,) = _rowwise_call("adamw_replicated", _adamw_math,
                               [[rep_pack([w[n] for n in names]), rep_pack(g_rep), rep_pack([mom[n] for n in names]),
                                 rep_pack([var[n] for n in names])]], [[(nl * rows, PACK_COLS)] * 3], 4)
    grad.update(zip(names, g_rep))
    for d, packed in zip((delta, new_m, new_v), rep_out):
        d.update(zip(names, _unpack(packed, [w[n].shape for n in names], rows)))

    return (loss, dx[None], *[grad[n] for n in WEIGHTS], *[delta[n] for n in WEIGHTS], *[new_m[n] for n in WEIGHTS],
            *[new_v[n] for n in WEIGHTS])
```

```python
import functools
import math

import jax
import jax.numpy as jnp
from jax import lax
from jax.experimental import pallas as pl
from jax.experimental.pallas import tpu as pltpu

F32 = jnp.float32
MXU_DTYPE = jnp.bfloat16
WIRE_DTYPE = jnp.bfloat16
MESH = pl.DeviceIdType.MESH

D_MODEL = 1024
D_FF = 4096
N_HEADS = 4
QK_NOPE = 64
QK_ROPE = 32
V_HEAD = 64
HEAD_PAD = 128
Q_RANK = 256
KV_RANK = 128
SG_CHUNK = 128
SG_GROUPS = 4
BR_WIDTH = 256
N_BRANCH = 4
POOL_WINDOWS = (2, 4, 8, 16)
HALO = 16
ROPE_BASE = 10000.0
EPS = 1e-6
ATTN_SCALE = (QK_NOPE + QK_ROPE) ** -0.5
N_PROJ = N_BRANCH * D_MODEL + 6 * BR_WIDTH + Q_RANK + KV_RANK + 2 * HEAD_PAD
COL_G, COL_M1, COL_M2, COL_B = 0, 4096, 4864, 5632

ADAM_LR, ADAM_B1, ADAM_B2, ADAM_EPS, ADAM_WD, ADAM_STEP = 0.001, 0.9, 0.999, 1e-08, 0.01, 10

VMEM_LIMIT = 56 * 1024 * 1024

WEIGHTS = ['norm_mix_pre', 'w_in', 'gate_b', 'q_norm', 'w_uq', 'kv_norm', 'w_ukv', 'w_br_mla', 'sg_ln_g', 'sg_ln_b',
           'sg_w', 'sg_b', 'w_br_sg', 'conv_w', 'w_br_conv', 'pool_w', 'pool_scale', 'w_br_pool', 'w_out',
           'norm_mix_post', 'norm_ffn_pre', 'w_ff1', 'w_ff2', 'norm_ffn_post']
COL_SHARDED = ['w_in', 'w_uq', 'w_ukv', 'w_br_mla', 'w_br_sg', 'w_br_conv', 'w_br_pool', 'w_ff1']
ROW_SHARDED = ['w_out', 'w_ff2']
MATMUL_SHARDED = ['w_in', 'w_uq', 'w_ukv', 'w_br_mla', 'w_br_sg', 'w_br_conv', 'w_br_pool', 'w_out', 'w_ff1', 'w_ff2']
BIG_SHARDED = ['w_in', 'w_ff1', 'w_ff2', 'w_out']
SMALL_SHARDED = ['w_uq', 'w_ukv', 'w_br_mla', 'w_br_sg', 'w_br_conv', 'w_br_pool']
SHARDED = MATMUL_SHARDED + ['conv_w']
REPLICATED = [n for n in WEIGHTS if n not in SHARDED]
PACK_COLS = 1024


def _params(sem, vmem=VMEM_LIMIT):
    return pltpu.CompilerParams(dimension_semantics=sem, vmem_limit_bytes=vmem)


def _mxu(a):
    return a.astype(MXU_DTYPE)


def _dot(a, b):
    return jnp.dot(_mxu(a), _mxu(b), preferred_element_type=F32)


def _dot_nt(a, b):
    return lax.dot_general(_mxu(a), _mxu(b), (((1,), (1,)), ((), ())), preferred_element_type=F32)


def _dot_tn(a, b):
    return lax.dot_general(_mxu(a), _mxu(b), (((0,), (0,)), ((), ())), preferred_element_type=F32)


def _rms(x, g):
    r = lax.rsqrt(jnp.mean(x * x, axis=-1, keepdims=True) + EPS)
    return x * r * g


def _rms_bwd(x, g, dy):
    r = lax.rsqrt(jnp.mean(x * x, axis=-1, keepdims=True) + EPS)
    xh = x * r
    gdy = dy * g
    dx = r * (gdy - xh * jnp.mean(gdy * xh, axis=-1, keepdims=True))
    return dx, jnp.sum(dy * xh, axis=0, keepdims=True)


_GELU_C = math.sqrt(2.0 / math.pi)


def _gelu(x):
    t = jnp.tanh(_GELU_C * (x + 0.044715 * (x * x * x)))
    return x * (0.5 * (1.0 + t)), t


def _gelu_grad(x, t):
    return 0.5 * (1.0 + t) + 0.5 * x * (1.0 - t * t) * (_GELU_C * (1.0 + 3.0 * 0.044715 * x * x))


def _sigmoid(x):
    return 1.0 / (1.0 + jnp.exp(-x))


def _full(shape):
    return pl.BlockSpec(shape, lambda *_: (0,) * len(shape))


def _rows(ts, width, col=0):
    return pl.BlockSpec((ts, width), lambda i: (i, col))


def _tile(n, pref):
    return min(n, pref)


def _mm(name, a, w, *, tm, tn, prologue=None, rows=()):
    m, k = a.shape
    n = w.shape[1]
    tm, tn = _tile(m, tm), _tile(n, tn)

    def body(a_ref, *rest):
        row_refs, w_ref, o_ref = rest[:len(rows)], rest[len(rows)], rest[len(rows) + 1]
        av = a_ref[...]
        if prologue is not None:
            av = prologue(av, *[r[...] for r in row_refs])
        o_ref[...] = _dot(av, w_ref[...])

    return pl.pallas_call(
        body, name=name, grid=(m // tm, n // tn),
        in_specs=[pl.BlockSpec((tm, k), lambda i, j: (i, 0))] + [pl.BlockSpec((1, k), lambda i, j: (0, 0)) for _ in rows]
        + [pl.BlockSpec((k, tn), lambda i, j: (0, j))],
        out_specs=pl.BlockSpec((tm, tn), lambda i, j: (i, j)),
        out_shape=jax.ShapeDtypeStruct((m, n), F32),
        compiler_params=_params(("parallel", "parallel")),
    )(a, *rows, w)


def _mm_tn(name, a, b, *, tm, tn, prologue=None, rows=()):
    m, k = a.shape
    n = b.shape[1]
    tm, tn = _tile(m, tm), _tile(n, tn)

    def body(a_ref, *rest):
        row_refs, b_ref, o_ref = rest[:len(rows)], rest[len(rows)], rest[len(rows) + 1]

        @pl.when(pl.program_id(1) == 0)
        def _():
            o_ref[...] = jnp.zeros_like(o_ref)

        av = a_ref[...]
        if prologue is not None:
            av = prologue(av, *[r[...] for r in row_refs])
        o_ref[...] += _dot_tn(av, b_ref[...])

    return pl.pallas_call(
        body, name=name, grid=(n // tn, m // tm),
        in_specs=[pl.BlockSpec((tm, k), lambda j, i: (i, 0))] + [pl.BlockSpec((1, k), lambda j, i: (0, 0)) for _ in rows]
        + [pl.BlockSpec((tm, tn), lambda j, i: (i, j))],
        out_specs=pl.BlockSpec((k, tn), lambda j, i: (0, j)),
        out_shape=jax.ShapeDtypeStruct((k, n), F32),
        compiler_params=_params(("parallel", "arbitrary")),
    )(a, *rows, b)


def _relu_sq(a):
    r = jnp.maximum(a, 0.0)
    return r * r


def _qkv_prep(proj, q_norm, kv_norm, wq, wkv, cq_tab, s_tab, cr_tab):
    s = proj.shape[0]
    ts = _tile(s, 512)
    hw = N_HEADS * HEAD_PAD

    def body(cq_ref, ckv_ref, kr_ref, krs_ref, gq_ref, gkv_ref, wq_ref, wkv_ref, ct_ref, st_ref, crt_ref,
             q_ref, k_ref, v_ref):
        ct, st, crt = ct_ref[...], st_ref[...], crt_ref[...]
        qn = _rms(cq_ref[...], gq_ref[...])
        qab = _dot(qn, wq_ref[...])
        kvn = _rms(ckv_ref[...], gkv_ref[...])
        kav = _dot(kvn, wkv_ref[...])
        k_rope = kr_ref[...] * crt + krs_ref[...] * st
        for h in range(N_HEADS):
            lo = h * HEAD_PAD
            q_ref[h] = (qab[:, lo:lo + HEAD_PAD] * ct + qab[:, hw + lo:hw + lo + HEAD_PAD] * st).astype(q_ref.dtype)
            k_ref[h] = (kav[:, lo:lo + HEAD_PAD] + k_rope).astype(k_ref.dtype)
            v_ref[h] = kav[:, hw + lo:hw + lo + HEAD_PAD].astype(v_ref.dtype)

    head_spec = pl.BlockSpec((N_HEADS, ts, HEAD_PAD), lambda i: (0, i, 0))
    head_shape = jax.ShapeDtypeStruct((N_HEADS, s, HEAD_PAD), MXU_DTYPE)
    return pl.pallas_call(
        body, name="qkv_prep", grid=(s // ts,),
        in_specs=[_rows(ts, Q_RANK, COL_B // Q_RANK), _rows(ts, KV_RANK, (COL_B + Q_RANK) // KV_RANK),
                  _rows(ts, HEAD_PAD, (COL_B + Q_RANK + KV_RANK) // HEAD_PAD),
                  _rows(ts, HEAD_PAD, (COL_B + Q_RANK + KV_RANK + HEAD_PAD) // HEAD_PAD),
                  _full((1, Q_RANK)), _full((1, KV_RANK)), _full((Q_RANK, 2 * hw)), _full((KV_RANK, 2 * hw)),
                  _rows(ts, HEAD_PAD), _rows(ts, HEAD_PAD), _rows(ts, HEAD_PAD)],
        out_specs=[head_spec, head_spec, head_spec],
        out_shape=[head_shape, head_shape, head_shape],
        compiler_params=_params(("parallel",)),
    )(proj, proj, proj, proj, q_norm, kv_norm, wq, wkv, cq_tab, s_tab, cr_tab)


def _diagonal_mask(t):
    return lax.broadcasted_iota(jnp.int32, (t, t), 1) <= lax.broadcasted_iota(jnp.int32, (t, t), 0)


def _attn_fwd(q, k, v):
    s = q.shape[1]
    t = _tile(s, 512)

    def body(q_ref, k_ref, v_ref, o_ref, lse_ref):
        i = pl.program_id(1)
        qv = q_ref[0]

        def step(j, carry, on_diagonal):
            m, l, acc = carry
            kj = k_ref[0, pl.ds(pl.multiple_of(j * t, t), t), :]
            vj = v_ref[0, pl.ds(pl.multiple_of(j * t, t), t), :]
            sc = _dot_nt(qv, kj) * ATTN_SCALE
            if on_diagonal:
                sc = jnp.where(_diagonal_mask(t), sc, -jnp.inf)
            m_new = jnp.maximum(m, jnp.max(sc, axis=1, keepdims=True))
            p = jnp.exp(sc - m_new)
            alpha = jnp.exp(m - m_new)
            return m_new, alpha * l + jnp.sum(p, axis=1, keepdims=True), alpha * acc + _dot(p, vj)

        init = (jnp.full((t, 1), -jnp.inf, F32), jnp.zeros((t, 1), F32), jnp.zeros((t, HEAD_PAD), F32))
        below = lax.fori_loop(0, i, functools.partial(step, on_diagonal=False), init)
        m, l, acc = step(i, below, True)
        o_ref[...] = acc / l
        lse_ref[0] = m + jnp.log(l)

    return pl.pallas_call(
        body, name="attn_fwd", grid=(N_HEADS, s // t),
        in_specs=[pl.BlockSpec((1, t, HEAD_PAD), lambda h, i: (h, i, 0)),
                  pl.BlockSpec((1, s, HEAD_PAD), lambda h, i: (h, 0, 0)),
                  pl.BlockSpec((1, s, HEAD_PAD), lambda h, i: (h, 0, 0))],
        out_specs=[pl.BlockSpec((t, HEAD_PAD), lambda h, i: (i, h)), pl.BlockSpec((1, t, 1), lambda h, i: (h, i, 0))],
        out_shape=[jax.ShapeDtypeStruct((s, N_HEADS * HEAD_PAD), F32), jax.ShapeDtypeStruct((N_HEADS, s, 1), F32)],
        compiler_params=_params(("parallel", "parallel")),
    )(q, k, v)


def _attn_bwd(q, k, v, do, lse, delta):
    s = q.shape[1]
    t = _tile(s, 512)
    nq = s // t

    def body(q_ref, do_ref, lse_ref, dl_ref, k_ref, v_ref, dq_ref, dk_ref, dv_ref):
        j = pl.program_id(1)

        @pl.when(j == 0)
        def _():
            dq_ref[...] = jnp.zeros_like(dq_ref)

        kj, vj = k_ref[0], v_ref[0]

        def step(i, carry, on_diagonal):
            dk, dv = carry
            rows = pl.ds(pl.multiple_of(i * t, t), t)
            qi, doi = q_ref[0, rows, :], do_ref[rows, :]
            sc = _dot_nt(qi, kj) * ATTN_SCALE
            if on_diagonal:
                sc = jnp.where(_diagonal_mask(t), sc, -jnp.inf)
            p = jnp.exp(sc - lse_ref[0, rows, :])
            dv = dv + _dot_tn(p, doi)
            dp = _dot_nt(doi, vj)
            ds = p * (dp - dl_ref[0, rows, :]) * ATTN_SCALE
            dk = dk + _dot_tn(ds, qi)
            dq_ref[0, rows, :] += _dot(ds, kj)
            return dk, dv

        zero = jnp.zeros((t, HEAD_PAD), F32)
        dk, dv = lax.fori_loop(j + 1, nq, functools.partial(step, on_diagonal=False), step(j, (zero, zero), True))
        dk_ref[0] = dk
        dv_ref[0] = dv

    whole = lambda w: pl.BlockSpec((1, s, w), lambda h, j: (h, 0, 0))
    tile = pl.BlockSpec((1, t, HEAD_PAD), lambda h, j: (h, j, 0))
    head_shape = jax.ShapeDtypeStruct((N_HEADS, s, HEAD_PAD), F32)
    return pl.pallas_call(
        body, name="attn_bwd", grid=(N_HEADS, nq),
        in_specs=[whole(HEAD_PAD), pl.BlockSpec((s, HEAD_PAD), lambda h, j: (0, h)), whole(1), whole(1), tile, tile],
        out_specs=[whole(HEAD_PAD), tile, tile],
        out_shape=[head_shape, head_shape, head_shape],
        compiler_params=_params(("parallel", "arbitrary")),
    )(q, do, lse, delta, k, v)


def _qkv_bwd(dq, dk, dv, proj, q_norm, kv_norm, wq, wkv, cq_tab, s_tab, cr_tab):
    s = proj.shape[0]
    ts = _tile(s, 256)
    hw = N_HEADS * HEAD_PAD

    def body(dq_ref, dk_ref, dv_ref, cq_ref, ckv_ref, gq_ref, gkv_ref, wq_ref, wkv_ref, ct_ref, st_ref, crt_ref,
             dpb_ref, dwq_ref, dwkv_ref, dgq_ref, dgkv_ref):
        @pl.when(pl.program_id(0) == 0)
        def _():
            for r in (dwq_ref, dwkv_ref, dgq_ref, dgkv_ref):
                r[...] = jnp.zeros_like(r)

        ct, st, crt = ct_ref[...], st_ref[...], crt_ref[...]
        dqs = [dq_ref[h] for h in range(N_HEADS)]
        dks = [dk_ref[h] for h in range(N_HEADS)]
        dqab = jnp.concatenate([d * ct for d in dqs] + [d * st for d in dqs], axis=1)
        dkav = jnp.concatenate(dks + [dv_ref[h] for h in range(N_HEADS)], axis=1)
        dk_sum = dks[0] + dks[1] + dks[2] + dks[3]
        cq, ckv, gq, gkv = cq_ref[...], ckv_ref[...], gq_ref[...], gkv_ref[...]
        dwq_ref[...] += _dot_tn(_rms(cq, gq), dqab)
        dwkv_ref[...] += _dot_tn(_rms(ckv, gkv), dkav)
        dcq, dgq = _rms_bwd(cq, gq, _dot_nt(dqab, wq_ref[...]))
        dckv, dgkv = _rms_bwd(ckv, gkv, _dot_nt(dkav, wkv_ref[...]))
        dgq_ref[...] += dgq
        dgkv_ref[...] += dgkv
        dpb_ref[...] = jnp.concatenate([dcq, dckv, dk_sum * crt, dk_sum * st], axis=1).astype(dpb_ref.dtype)

    head_spec = pl.BlockSpec((N_HEADS, ts, HEAD_PAD), lambda i: (0, i, 0))
    wb = Q_RANK + KV_RANK + 2 * HEAD_PAD
    return pl.pallas_call(
        body, name="qkv_bwd", grid=(s // ts,),
        in_specs=[head_spec, head_spec, head_spec,
                  _rows(ts, Q_RANK, COL_B // Q_RANK), _rows(ts, KV_RANK, (COL_B + Q_RANK) // KV_RANK),
                  _full((1, Q_RANK)), _full((1, KV_RANK)), _full((Q_RANK, 2 * hw)), _full((KV_RANK, 2 * hw)),
                  _rows(ts, HEAD_PAD), _rows(ts, HEAD_PAD), _rows(ts, HEAD_PAD)],
        out_specs=[_rows(ts, wb), _full((Q_RANK, 2 * hw)), _full((KV_RANK, 2 * hw)), _full((1, Q_RANK)), _full((1, KV_RANK))],
        out_shape=[jax.ShapeDtypeStruct((s, wb), MXU_DTYPE), jax.ShapeDtypeStruct((Q_RANK, 2 * hw), F32),
                   jax.ShapeDtypeStruct((KV_RANK, 2 * hw), F32), jax.ShapeDtypeStruct((1, Q_RANK), F32),
                   jax.ShapeDtypeStruct((1, KV_RANK), F32)],
        compiler_params=_params(("arbitrary",)),
    )(dq, dk, dv, proj, proj, q_norm, kv_norm, wq, wkv, cq_tab, s_tab, cr_tab)


def _lane_group(width):
    return lax.broadcasted_iota(jnp.int32, (1, width), 1) // (width // 4)


def _shift_down(a, k):
    return pltpu.roll(a, k, 0)


def _shift_up(a, k):
    return pltpu.roll(a, a.shape[0] - k, 0)


def _window_sums(xh, shift):
    s2 = xh + shift(xh, 1)
    s4 = s2 + shift(s2, 2)
    s8 = s4 + shift(s4, 4)
    s16 = s8 + shift(s8, 8)
    grp = _lane_group(xh.shape[1])
    return jnp.where(grp == 0, s2, jnp.where(grp == 1, s4, jnp.where(grp == 2, s8, s16)))


def _pool_count(i, ts):
    grp = _lane_group(BR_WIDTH)
    win = jnp.where(grp == 0, 2.0, jnp.where(grp == 1, 4.0, jnp.where(grp == 2, 8.0, 16.0)))
    t = (i * ts + lax.broadcasted_iota(jnp.int32, (ts, 1), 0)).astype(F32)
    return jnp.minimum(t + 1.0, win)


def _mix_forward(i, ts, r):
    f = {}
    f['gates'] = _sigmoid(r['gate'][...] + r['gate_b'][...])
    sgu, sgv = r['sgu'][...], r['sgv'][...]
    f['sgu'], f['sgv'] = sgu, sgv
    u_act, f['tu'] = _gelu(sgu)
    vg, f['tv'] = _gelu(sgv)
    mu = jnp.mean(vg, axis=-1, keepdims=True)
    xc = vg - mu
    f['ln_r'] = lax.rsqrt(jnp.mean(xc * xc, axis=-1, keepdims=True) + EPS)
    f['ln_xh'] = xc * f['ln_r']
    vln = f['ln_xh'] * r['ln_g'][...] + r['ln_b'][...]
    tril = lax.broadcasted_iota(jnp.int32, (SG_CHUNK, SG_CHUNK), 1) <= lax.broadcasted_iota(jnp.int32, (SG_CHUNK, SG_CHUNK), 0)
    f['wm'] = [_mxu(jnp.where(tril, r['sg_w'][g], 0.0)) for g in range(SG_GROUPS)]
    f['tril'] = tril
    grp = _lane_group(BR_WIDTH)
    bias = r['sg_bias'][...]
    parts = []
    for ci in range(ts // SG_CHUNK):
        vc = vln[ci * SG_CHUNK:(ci + 1) * SG_CHUNK]
        sc = bias
        for g in range(SG_GROUPS):
            sc = sc + jnp.where(grp == g, _dot(f['wm'][g], vc), 0.0)
        parts.append(sc)
    f['vln'] = vln
    f['sg_s'] = parts[0] if len(parts) == 1 else jnp.concatenate(parts, axis=0)
    f['u_act'] = u_act
    out_b = u_act * f['sg_s']
    first = (i > 0).astype(F32)
    cvx, cvc, cvb = r['cvx'][...], r['cvc'][...], r['cvb'][...]
    f['cvx'], f['cvc'], f['cvb'] = cvx, cvc, cvb
    zh = jnp.concatenate([r['hx'][...] * r['hc'][...] * first, cvc * cvx], axis=0)
    f['z1'] = _shift_down(zh, 1)[HALO:]
    f['z2'] = _shift_down(zh, 2)[HALO:]
    f['z0'] = zh[HALO:]
    f['yv'] = r['conv_w'][0:1, :] * f['z2'] + r['conv_w'][1:2, :] * f['z1'] + r['conv_w'][2:3, :] * f['z0']
    out_c = cvb * f['yv']
    p = r['pool'][...]
    ph = jnp.concatenate([r['hp'][...] * first, p], axis=0)
    f['cnt'] = _pool_count(i, ts)
    f['pooled'] = _window_sums(ph, _shift_down)[HALO:] / f['cnt'] - p
    f['mixed'] = _dot(f['pooled'], r['wbd'][...])
    out_d = f['mixed'] * r['pool_scale'][...]
    f['outs'] = [r['o'][...], out_b, out_c, out_d]
    f['ys'] = [_dot(f['outs'][b], r['w_br'][b][...]) for b in range(N_BRANCH)]
    merged = f['gates'][:, 0:D_MODEL] * f['ys'][0]
    for b in range(1, N_BRANCH):
        merged = merged + f['gates'][:, b * D_MODEL:(b + 1) * D_MODEL] * f['ys'][b]
    f['merged'] = merged
    f['mo'] = _dot(merged, r['w_out'][...])
    return f


_MIX_TILE_INPUTS = ['gate', 'sgu', 'sgv', 'cvb', 'cvx', 'cvc', 'pool', 'hx', 'hc', 'hp', 'o']
_MIX_WEIGHTS = ['gate_b', 'ln_g', 'ln_b', 'sg_w', 'sg_bias', 'conv_w', 'wbd', 'pool_scale', 'w_br0', 'w_br1', 'w_br2',
                'w_br3', 'w_out', 'g_post']


def _mix_specs(s, ts):
    c0 = COL_M1 // BR_WIDTH
    prev = lambda col: pl.BlockSpec((HALO, BR_WIDTH), lambda i: (jnp.maximum(i * (ts // HALO) - 1, 0), col))
    tiles = [_rows(ts, N_BRANCH * D_MODEL, 0), _rows(ts, BR_WIDTH, c0), _rows(ts, BR_WIDTH, c0 + 1), _rows(ts, BR_WIDTH, c0 + 2),
             _rows(ts, BR_WIDTH, c0 + 3), _rows(ts, BR_WIDTH, c0 + 4), _rows(ts, BR_WIDTH, c0 + 5),
             prev(c0 + 3), prev(c0 + 4), prev(c0 + 5), _rows(ts, N_HEADS * HEAD_PAD)]
    weights = [_full((1, N_BRANCH * D_MODEL)), _full((1, BR_WIDTH)), _full((1, BR_WIDTH)),
               _full((SG_GROUPS, SG_CHUNK, SG_CHUNK)), _full((SG_CHUNK, BR_WIDTH)), _full((8, BR_WIDTH)),
               _full((BR_WIDTH, BR_WIDTH)), _full((1, BR_WIDTH)), _full((N_HEADS * HEAD_PAD, D_MODEL)),
               _full((BR_WIDTH, D_MODEL)), _full((BR_WIDTH, D_MODEL)), _full((BR_WIDTH, D_MODEL)),
               _full((D_MODEL, D_MODEL)), _full((1, D_MODEL))]
    return tiles, weights


def _mix_refs(refs):
    names = _MIX_TILE_INPUTS + _MIX_WEIGHTS
    r = dict(zip(names, refs[:len(names)]))
    r['w_br'] = [r['w_br0'], r['w_br1'], r['w_br2'], r['w_br3']]
    return r, refs[len(names):]


def _mix_operands(proj, o, lw):
    return ([proj] * 10 + [o] + [lw[n] for n in ['gate_b', 'sg_ln_g', 'sg_ln_b', 'sg_w', 'sg_bias', 'conv_w8', 'wbd',
                                                 'pool_scale', 'w_br_mla_p', 'w_br_sg', 'w_br_conv', 'w_br_pool', 'w_out',
                                                 'norm_mix_post']])


def _mix_fwd(x0, proj, o, lw):
    s = x0.shape[0]
    ts = _tile(s, 256)
    tiles, weights = _mix_specs(s, ts)

    def body(*refs):
        r, (x0_ref, x1_ref) = _mix_refs(refs)
        f = _mix_forward(pl.program_id(0), ts, r)
        x1_ref[...] = x0_ref[...] + _rms(f['mo'], r['g_post'][...])

    return pl.pallas_call(
        body, name="mix_fwd", grid=(s // ts,),
        in_specs=tiles + weights + [_rows(ts, D_MODEL)],
        out_specs=_rows(ts, D_MODEL),
        out_shape=jax.ShapeDtypeStruct((s, D_MODEL), F32),
        compiler_params=_params(("parallel",)),
    )(*_mix_operands(proj, o, lw), x0)


def _mix_bwd(dx1, proj, o, lw):
    s = dx1.shape[0]
    ts = _tile(s, 128)
    tiles, weights = _mix_specs(s, ts)
    hw = N_HEADS * HEAD_PAD

    def body(*refs):
        r, rest = _mix_refs(refs)
        (dx1_ref, dg_ref, dm1_ref, dyv_ref, up_ref, do_ref, delta_ref,
         dgate_b_ref, dln_g_ref, dln_b_ref, dsgw_ref, dsgb_ref, dconv_ref, dwbd_ref, dps_ref,
         dwbr0_ref, dwbr1_ref, dwbr2_ref, dwbr3_ref, dwout_ref, dgpost_ref, dbias_acc) = rest
        i = pl.program_id(0)
        acc_refs = [dgate_b_ref, dln_g_ref, dln_b_ref, dsgw_ref, dsgb_ref, dconv_ref, dwbd_ref, dps_ref,
                    dwbr0_ref, dwbr1_ref, dwbr2_ref, dwbr3_ref, dwout_ref, dgpost_ref, dbias_acc]

        @pl.when(i == 0)
        def _():
            for a in acc_refs:
                a[...] = jnp.zeros_like(a)

        f = _mix_forward(i, ts, r)
        dmo, dgpost = _rms_bwd(f['mo'], r['g_post'][...], dx1_ref[...])
        dgpost_ref[...] += dgpost
        dwout_ref[...] += _dot_tn(f['merged'], dmo)
        dmerged = _dot_nt(dmo, r['w_out'][...])
        dwbr = [dwbr0_ref, dwbr1_ref, dwbr2_ref, dwbr3_ref]
        douts = []
        for b in range(N_BRANCH):
            gb = f['gates'][:, b * D_MODEL:(b + 1) * D_MODEL]
            dgate = dmerged * f['ys'][b] * gb * (1.0 - gb)
            dg_ref[:, b * D_MODEL:(b + 1) * D_MODEL] = dgate.astype(dg_ref.dtype)
            dgate_b_ref[:, b * D_MODEL:(b + 1) * D_MODEL] += jnp.sum(dgate, axis=0, keepdims=True)
            dy = dmerged * gb
            dwbr[b][...] += _dot_tn(f['outs'][b], dy)
            douts.append(_dot_nt(dy, r['w_br'][b][...]))
        do = douts[0]
        do_ref[...] = do.astype(do_ref.dtype)
        prod = do * f['outs'][0]
        for h in range(N_HEADS):
            delta_ref[h] = jnp.sum(prod[:, h * HEAD_PAD:(h + 1) * HEAD_PAD], axis=1, keepdims=True)
        grp = _lane_group(BR_WIDTH)
        ds = douts[1] * f['u_act']
        dsgu = douts[1] * f['sg_s'] * _gelu_grad(f['sgu'], f['tu'])
        dvln_parts = []
        for ci in range(ts // SG_CHUNK):
            rows = slice(ci * SG_CHUNK, (ci + 1) * SG_CHUNK)
            ds_c, vln_c = ds[rows], f['vln'][rows]
            dvln_c = jnp.zeros((SG_CHUNK, BR_WIDTH), F32)
            for g in range(SG_GROUPS):
                dvln_c = dvln_c + jnp.where(grp == g, _dot_tn(f['wm'][g], ds_c), 0.0)
                dsgw_ref[g] += jnp.where(f['tril'], _dot_nt(jnp.where(grp == g, ds_c, 0.0), vln_c), 0.0)
            dbias_acc[...] += ds_c
            dvln_parts.append(dvln_c)
        dvln = dvln_parts[0] if len(dvln_parts) == 1 else jnp.concatenate(dvln_parts, axis=0)
        dln_g_ref[...] += jnp.sum(dvln * f['ln_xh'], axis=0, keepdims=True)
        dln_b_ref[...] += jnp.sum(dvln, axis=0, keepdims=True)
        dxh = dvln * r['ln_g'][...]
        dvg = f['ln_r'] * (dxh - jnp.mean(dxh, axis=-1, keepdims=True)
                           - f['ln_xh'] * jnp.mean(dxh * f['ln_xh'], axis=-1, keepdims=True))
        dsgv = dvg * _gelu_grad(f['sgv'], f['tv'])
        dcvb = douts[2] * f['yv']
        dyv = douts[2] * f['cvb']
        dyv_ref[...] = dyv
        for kk, zk in enumerate((f['z2'], f['z1'], f['z0'])):
            dconv_ref[kk:kk + 1, :] += jnp.sum(dyv * zk, axis=0, keepdims=True)
        dps_ref[...] += jnp.sum(douts[3] * f['mixed'], axis=0, keepdims=True)
        dmixed = douts[3] * r['pool_scale'][...]
        dwbd_ref[...] += _dot_tn(f['pooled'], dmixed)
        up_ref[...] = _dot_nt(dmixed, r['wbd'][...]) / f['cnt']
        dm1_ref[...] = jnp.concatenate([dsgu, dsgv, dcvb], axis=1).astype(dm1_ref.dtype)

        @pl.when(i == pl.num_programs(0) - 1)
        def _():
            lane = lax.broadcasted_iota(jnp.int32, (1, SG_CHUNK), 1)
            db = dbias_acc[...]
            out = jnp.zeros((SG_CHUNK, SG_CHUNK), F32)
            for g in range(SG_GROUPS):
                out = out + jnp.where(lane == g, jnp.sum(jnp.where(grp == g, db, 0.0), axis=1, keepdims=True), 0.0)
            dsgb_ref[...] = out

    acc = lambda shape: (_full(shape), jax.ShapeDtypeStruct(shape, F32))
    accs = [acc((1, N_BRANCH * D_MODEL)), acc((1, BR_WIDTH)), acc((1, BR_WIDTH)), acc((SG_GROUPS, SG_CHUNK, SG_CHUNK)),
            acc((SG_CHUNK, SG_CHUNK)), acc((8, BR_WIDTH)), acc((BR_WIDTH, BR_WIDTH)), acc((1, BR_WIDTH)),
            acc((hw, D_MODEL)), acc((BR_WIDTH, D_MODEL)), acc((BR_WIDTH, D_MODEL)), acc((BR_WIDTH, D_MODEL)),
            acc((D_MODEL, D_MODEL)), acc((1, D_MODEL))]
    tile_outs = [(_rows(ts, N_BRANCH * D_MODEL), jax.ShapeDtypeStruct((s, N_BRANCH * D_MODEL), MXU_DTYPE)),
                 (_rows(ts, 3 * BR_WIDTH), jax.ShapeDtypeStruct((s, 3 * BR_WIDTH), MXU_DTYPE)),
                 (_rows(ts, BR_WIDTH), jax.ShapeDtypeStruct((s, BR_WIDTH), F32)),
                 (_rows(ts, BR_WIDTH), jax.ShapeDtypeStruct((s, BR_WIDTH), F32)),
                 (_rows(ts, hw), jax.ShapeDtypeStruct((s, hw), MXU_DTYPE)),
                 (pl.BlockSpec((N_HEADS, ts, 1), lambda i: (0, i, 0)), jax.ShapeDtypeStruct((N_HEADS, s, 1), F32))]
    outs = tile_outs + accs
    return pl.pallas_call(
        body, name="mix_bwd", grid=(s // ts,),
        in_specs=tiles + weights + [_rows(ts, D_MODEL)],
        out_specs=[o_[0] for o_ in outs], out_shape=[o_[1] for o_ in outs],
        scratch_shapes=[pltpu.VMEM((SG_CHUNK, BR_WIDTH), F32)],
        compiler_params=_params(("arbitrary",), 60 * 1024 * 1024),
    )(*_mix_operands(proj, o, lw), dx1)


def _shift_bwd(dyv, upool, proj, conv_w8):
    s = dyv.shape[0]
    ts = _tile(s, 512)
    nb = s // HALO
    c0 = COL_M1 // BR_WIDTH

    def body(dyv_ref, dyvn_ref, up_ref, upn_ref, cvx_ref, cvc_ref, cw_ref, out_ref):
        i = pl.program_id(0)
        last = (i < pl.num_programs(0) - 1).astype(F32)
        dh = jnp.concatenate([dyv_ref[...], dyvn_ref[...] * last], axis=0)
        dz = (cw_ref[2:3, :] * dh + cw_ref[1:2, :] * _shift_up(dh, 1) + cw_ref[0:1, :] * _shift_up(dh, 2))[:ts]
        up = up_ref[...]
        uh = jnp.concatenate([up, upn_ref[...] * last], axis=0)
        dpool = _window_sums(uh, _shift_up)[:ts] - up * _pool_count(i, ts)
        out_ref[...] = jnp.concatenate([dz * cvc_ref[...], dz * cvx_ref[...], dpool], axis=1).astype(out_ref.dtype)

    nxt = pl.BlockSpec((HALO, BR_WIDTH), lambda i: (jnp.minimum((i + 1) * (ts // HALO), nb - 1), 0))
    return pl.pallas_call(
        body, name="shift_bwd", grid=(s // ts,),
        in_specs=[_rows(ts, BR_WIDTH), nxt, _rows(ts, BR_WIDTH), nxt, _rows(ts, BR_WIDTH, c0 + 3), _rows(ts, BR_WIDTH, c0 + 4),
                  _full((8, BR_WIDTH))],
        out_specs=_rows(ts, 3 * BR_WIDTH),
        out_shape=jax.ShapeDtypeStruct((s, 3 * BR_WIDTH), MXU_DTYPE),
        compiler_params=_params(("parallel",)),
    )(dyv, dyv, upool, upool, proj, proj, conv_w8)


def _ffn2(a, w2, x1, g):
    s = a.shape[0]
    ts = _tile(s, 256)

    def body(a_ref, w_ref, x1_ref, g_ref, x2_ref, f_ref):
        f = _dot(_relu_sq(a_ref[...]), w_ref[...])
        f_ref[...] = f
        x2_ref[...] = x1_ref[...] + _rms(f, g_ref[...])

    return pl.pallas_call(
        body, name="ffn2", grid=(s // ts,),
        in_specs=[_rows(ts, D_FF), _full((D_FF, D_MODEL)), _rows(ts, D_MODEL), _full((1, D_MODEL))],
        out_specs=[_rows(ts, D_MODEL), _rows(ts, D_MODEL)],
        out_shape=[jax.ShapeDtypeStruct((s, D_MODEL), F32)] * 2,
        compiler_params=_params(("parallel",)),
    )(a, w2, x1, g)


def _ffn2_bwd(dx2, f, g, a, w2t):
    s = a.shape[0]
    ts = _tile(s, 256)

    def body(dx2_ref, f_ref, g_ref, a_ref, w_ref, df_ref, da_ref, dg_ref):
        @pl.when(pl.program_id(0) == 0)
        def _():
            dg_ref[...] = jnp.zeros_like(dg_ref)

        df, dg = _rms_bwd(f_ref[...], g_ref[...], dx2_ref[...])
        dg_ref[...] += dg
        df_ref[...] = df.astype(df_ref.dtype)
        da_ref[...] = (_dot(df, w_ref[...]) * (2.0 * jnp.maximum(a_ref[...], 0.0))).astype(da_ref.dtype)

    return pl.pallas_call(
        body, name="ffn2_bwd", grid=(s // ts,),
        in_specs=[_rows(ts, D_MODEL), _rows(ts, D_MODEL), _full((1, D_MODEL)), _rows(ts, D_FF), _full((D_MODEL, D_FF))],
        out_specs=[_rows(ts, D_MODEL), _rows(ts, D_FF), _full((1, D_MODEL))],
        out_shape=[jax.ShapeDtypeStruct((s, D_MODEL), MXU_DTYPE), jax.ShapeDtypeStruct((s, D_FF), MXU_DTYPE),
                   jax.ShapeDtypeStruct((1, D_MODEL), F32)],
        compiler_params=_params(("arbitrary",)),
    )(dx2, f, g, a, w2t)


def _norm_in_bwd(name, pieces, x, g, dres):
    s = x.shape[0]
    ts = _tile(s, 256)
    n = len(pieces)

    def body(*refs):
        d_refs, w_refs = refs[:n], refs[n:2 * n]
        x_ref, g_ref, dres_ref, dx_ref, dg_ref = refs[2 * n:]

        @pl.when(pl.program_id(0) == 0)
        def _():
            dg_ref[...] = jnp.zeros_like(dg_ref)

        dh = _dot(d_refs[0][...], w_refs[0][...])
        for p in range(1, n):
            dh = dh + _dot(d_refs[p][...], w_refs[p][...])
        dx, dg = _rms_bwd(x_ref[...], g_ref[...], dh)
        dg_ref[...] += dg
        dx_ref[...] = dres_ref[...] + dx

    return pl.pallas_call(
        body, name=name, grid=(s // ts,),
        in_specs=[_rows(ts, d.shape[1]) for d, _ in pieces] + [_full(w.shape) for _, w in pieces]
        + [_rows(ts, D_MODEL), _full((1, D_MODEL)), _rows(ts, D_MODEL)],
        out_specs=[_rows(ts, D_MODEL), _full((1, D_MODEL))],
        out_shape=[jax.ShapeDtypeStruct((s, D_MODEL), F32), jax.ShapeDtypeStruct((1, D_MODEL), F32)],
        compiler_params=_params(("arbitrary",)),
    )(*[d for d, _ in pieces], *[w for _, w in pieces], x, g, dres)


def _loss_and_grad(y, target):
    s = y.shape[0]
    ts = _tile(s, 512)

    def body(y_ref, t_ref, dy_ref, loss_ref):
        @pl.when(pl.program_id(0) == 0)
        def _():
            loss_ref[...] = jnp.zeros_like(loss_ref)

        err = y_ref[...] - t_ref[...]
        dy_ref[...] = err * (1.0 / D_MODEL)
        loss_ref[...] += 0.5 * jnp.sum(jnp.mean(err * err, axis=-1, keepdims=True), axis=0, keepdims=True)

    dy, loss = pl.pallas_call(
        body, name="loss", grid=(s // ts,),
        in_specs=[_rows(ts, D_MODEL), _rows(ts, D_MODEL)],
        out_specs=[_rows(ts, D_MODEL), _full((8, 128))],
        out_shape=[jax.ShapeDtypeStruct((s, D_MODEL), F32), jax.ShapeDtypeStruct((8, 128), F32)],
        compiler_params=_params(("arbitrary",)),
    )(y, target)
    return loss[0, 0], dy


_W_IN_SPLITS = [256, 384, 416, 672, 928, 1184, 1440, 1696, 1952]


def _rope_swap(w):
    half = QK_ROPE // 2
    return jnp.concatenate([-w[..., half:], w[..., :half]], axis=-1)


def _rope_unswap(d):
    half = QK_ROPE // 2
    return jnp.concatenate([d[..., half:], -d[..., :half]], axis=-1)


def _zeros_like_cols(w, n):
    return jnp.zeros(w.shape[:-1] + (n,), w.dtype)


def _derive_weights(w):
    md = MXU_DTYPE
    nl = w['w_in'].shape[0]
    c_q, c_kv, k_r, sg_u, sg_v, cv_x, cv_b, cv_c, pool, gate = jnp.split(w['w_in'].astype(md), _W_IN_SPLITS, axis=-1)
    pad_rope = lambda r: jnp.concatenate([_zeros_like_cols(r, QK_NOPE), r, _zeros_like_cols(r, HEAD_PAD - QK_NOPE - QK_ROPE)], -1)
    w_in_p = jnp.concatenate([gate, sg_u, sg_v, cv_b, cv_x, cv_c, pool, c_q, c_kv, pad_rope(k_r), pad_rope(_rope_swap(k_r))], -1)
    wq = w['w_uq'].astype(md).reshape(nl, Q_RANK, N_HEADS, QK_NOPE + QK_ROPE)
    nope, rope_w = wq[..., :QK_NOPE], wq[..., QK_NOPE:]
    wq_a = jnp.concatenate([nope, rope_w, _zeros_like_cols(nope, 32)], -1).reshape(nl, Q_RANK, N_HEADS * HEAD_PAD)
    wq_b = pad_rope(_rope_swap(rope_w)).reshape(nl, Q_RANK, N_HEADS * HEAD_PAD)
    wkv = w['w_ukv'].astype(md).reshape(nl, KV_RANK, N_HEADS, QK_NOPE + V_HEAD)
    pad_half = lambda r: jnp.concatenate([r, _zeros_like_cols(r, HEAD_PAD - r.shape[-1])], -1).reshape(nl, KV_RANK, N_HEADS * HEAD_PAD)
    w_br_mla = w['w_br_mla'].astype(md).reshape(nl, N_HEADS, V_HEAD, D_MODEL)
    w_br_mla_p = jnp.concatenate([w_br_mla, jnp.zeros_like(w_br_mla)], axis=2).reshape(nl, N_HEADS * HEAD_PAD, D_MODEL)
    eye = jnp.eye(4, dtype=md)
    wbd = (w['pool_w'].astype(md)[:, :, :, None, :] * eye[None, :, None, :, None]).reshape(nl, BR_WIDTH, BR_WIDTH)
    row = lambda a: a.astype(F32)[:, None, :]
    w_in_pt = jnp.swapaxes(w_in_p, 1, 2)
    return dict(
        w_in_p=w_in_p, wt_g=w_in_pt[:, COL_G:COL_M1], wt_m1=w_in_pt[:, COL_M1:COL_M2], wt_m2=w_in_pt[:, COL_M2:COL_B],
        wt_b=w_in_pt[:, COL_B:],
        wq=jnp.concatenate([wq_a, wq_b], -1), wkv=jnp.concatenate([pad_half(wkv[..., :QK_NOPE]), pad_half(wkv[..., QK_NOPE:])], -1),
        w_br_mla_p=w_br_mla_p, w_br_sg=w['w_br_sg'].astype(md), w_br_conv=w['w_br_conv'].astype(md),
        w_br_pool=w['w_br_pool'].astype(md), wbd=wbd, w_out=w['w_out'].astype(md),
        w_ff1=w['w_ff1'].astype(md), w_ff1t=jnp.swapaxes(w['w_ff1'].astype(md), 1, 2),
        w_ff2=w['w_ff2'].astype(md), w_ff2t=jnp.swapaxes(w['w_ff2'].astype(md), 1, 2),
        norm_mix_pre=row(w['norm_mix_pre']), gate_b=row(w['gate_b']), q_norm=row(w['q_norm']), kv_norm=row(w['kv_norm']),
        sg_ln_g=row(w['sg_ln_g']), sg_ln_b=row(w['sg_ln_b']), sg_w=w['sg_w'].astype(F32),
        sg_bias=jnp.repeat(jnp.swapaxes(w['sg_b'].astype(F32), 1, 2), BR_WIDTH // SG_GROUPS, axis=2),
        conv_w8=jnp.pad(w['conv_w'].astype(F32), ((0, 0), (0, 5), (0, 0))), pool_scale=row(w['pool_scale']),
        norm_mix_post=row(w['norm_mix_post']), norm_ffn_pre=row(w['norm_ffn_pre']), norm_ffn_post=row(w['norm_ffn_post']),
    )


def _rope_tables(positions):
    inv_freq = ROPE_BASE ** (-jnp.arange(0, QK_ROPE, 2, dtype=F32) / QK_ROPE)
    ang = positions.astype(F32)[:, None] * inv_freq
    cos, sin = jnp.cos(ang), jnp.sin(ang)
    n = positions.shape[0]
    ones, z64, z32 = jnp.ones((n, QK_NOPE), F32), jnp.zeros((n, QK_NOPE), F32), jnp.zeros((n, 32), F32)
    return (jnp.concatenate([ones, cos, cos, z32], 1), jnp.concatenate([z64, sin, sin, z32], 1),
            jnp.concatenate([z64, cos, cos, z32], 1))


def _reference_layout_grads(g):
    gate, dm1, dm2, dpb = g['dw_in_pieces']
    nl = gate.shape[0]
    sg_u, sg_v, cv_b = jnp.split(dm1, 3, axis=-1)
    cv_x, cv_c, pool = jnp.split(dm2, 3, axis=-1)
    c_q, c_kv, kr, krs = jnp.split(dpb, [Q_RANK, Q_RANK + KV_RANK, Q_RANK + KV_RANK + HEAD_PAD], axis=-1)
    rope_cols = slice(QK_NOPE, QK_NOPE + QK_ROPE)
    k_r = kr[..., rope_cols] + _rope_unswap(krs[..., rope_cols])
    w_in = jnp.concatenate([c_q, c_kv, k_r, sg_u, sg_v, cv_x, cv_b, cv_c, pool, gate], -1)
    hw = N_HEADS * HEAD_PAD
    dqa = g['dwq'][..., :hw].reshape(nl, Q_RANK, N_HEADS, HEAD_PAD)
    dqb = g['dwq'][..., hw:].reshape(nl, Q_RANK, N_HEADS, HEAD_PAD)
    w_uq = jnp.concatenate([dqa[..., :QK_NOPE], dqa[..., rope_cols] + _rope_unswap(dqb[..., rope_cols])], -1)
    dka = g['dwkv'][..., :hw].reshape(nl, KV_RANK, N_HEADS, HEAD_PAD)
    dva = g['dwkv'][..., hw:].reshape(nl, KV_RANK, N_HEADS, HEAD_PAD)
    w_ukv = jnp.concatenate([dka[..., :QK_NOPE], dva[..., :V_HEAD]], -1)
    w_br_mla = g['dw_br_mla_p'].reshape(nl, N_HEADS, HEAD_PAD, D_MODEL)[:, :, :V_HEAD]
    dwbd = g['dwbd'].reshape(nl, 4, 64, 4, 64)
    pool_w = jnp.stack([dwbd[:, k, :, k, :] for k in range(4)], axis=1)
    sq = lambda a: a[:, 0, :]
    return dict(
        norm_mix_pre=sq(g['dg_pre']), w_in=w_in, gate_b=sq(g['dgate_b']), q_norm=sq(g['dq_norm']),
        w_uq=w_uq.reshape(nl, Q_RANK, -1), kv_norm=sq(g['dkv_norm']), w_ukv=w_ukv.reshape(nl, KV_RANK, -1),
        w_br_mla=w_br_mla.reshape(nl, N_HEADS * V_HEAD, D_MODEL), sg_ln_g=sq(g['dln_g']), sg_ln_b=sq(g['dln_b']),
        sg_w=g['dsg_w'], sg_b=jnp.swapaxes(g['dsg_b'][:, :, :SG_GROUPS], 1, 2), w_br_sg=g['dw_br_sg'],
        conv_w=g['dconv_w'][:, :3], w_br_conv=g['dw_br_conv'], pool_w=pool_w, pool_scale=sq(g['dpool_scale']),
        w_br_pool=g['dw_br_pool'], w_out=g['dw_out'], norm_mix_post=sq(g['dg_post']), norm_ffn_pre=sq(g['dg_fpre']),
        w_ff1=g['dw_ff1'], w_ff2=g['dw_ff2'], norm_ffn_post=sq(g['dg_fpost']))


def _layer_forward(x0, lw, tabs):
    proj = _mm("in_proj", x0, lw['w_in_p'], tm=512, tn=896, prologue=_rms, rows=(lw['norm_mix_pre'],))
    q, k, v = _qkv_prep(proj, lw['q_norm'], lw['kv_norm'], lw['wq'], lw['wkv'], *tabs)
    o, lse = _attn_fwd(q, k, v)
    x1 = _mix_fwd(x0, proj, o, lw)
    a = _mm("ffn1", x1, lw['w_ff1'], tm=512, tn=1024, prologue=_rms, rows=(lw['norm_ffn_pre'],))
    x2, f = _ffn2(a, lw['w_ff2'], x1, lw['norm_ffn_post'])
    return x2, dict(x0=x0, proj=proj, q=q, k=k, v=v, o=o, lse=lse, x1=x1, a=a, f=f)


def _layer_backward(dx2, lw, sv, tabs):
    g = {}
    df, da, g['dg_fpost'] = _ffn2_bwd(dx2, sv['f'], lw['norm_ffn_post'], sv['a'], lw['w_ff2t'])
    g['dw_ff2'] = _mm_tn("dw_ff2", sv['a'], df, tm=256, tn=512, prologue=_relu_sq)
    dx1, g['dg_fpre'] = _norm_in_bwd("ffn1_bwd", [(da, lw['w_ff1t'])], sv['x1'], lw['norm_ffn_pre'], dx2)
    g['dw_ff1'] = _mm_tn("dw_ff1", sv['x1'], da, tm=512, tn=1024, prologue=_rms, rows=(lw['norm_ffn_pre'],))
    (dgate, dm1, dyv, upool, do, delta, g['dgate_b'], g['dln_g'], g['dln_b'], g['dsg_w'], g['dsg_b'], g['dconv_w'],
     g['dwbd'], g['dpool_scale'], g['dw_br_mla_p'], g['dw_br_sg'], g['dw_br_conv'], g['dw_br_pool'], g['dw_out'],
     g['dg_post']) = _mix_bwd(dx1, sv['proj'], sv['o'], lw)
    dm2 = _shift_bwd(dyv, upool, sv['proj'], lw['conv_w8'])
    dq, dk, dv = _attn_bwd(sv['q'], sv['k'], sv['v'], do, sv['lse'], delta)
    dpb, g['dwq'], g['dwkv'], g['dq_norm'], g['dkv_norm'] = _qkv_bwd(
        dq, dk, dv, sv['proj'], lw['q_norm'], lw['kv_norm'], lw['wq'], lw['wkv'], *tabs)
    pieces = [(dgate, lw['wt_g']), (dm1, lw['wt_m1']), (dm2, lw['wt_m2']), (dpb, lw['wt_b'])]
    dx0, g['dg_pre'] = _norm_in_bwd("in_proj_bwd", pieces, sv['x0'], lw['norm_mix_pre'], dx1)
    g['dw_in_pieces'] = [_mm_tn("dw_in_%d" % n, sv['x0'], d, tm=512, tn=1024, prologue=_rms, rows=(lw['norm_mix_pre'],))
                         for n, (d, _) in enumerate(pieces)]
    return dx0, g


def _local_step(x, positions, target, layers, after_layer_backward):
    tabs = _rope_tables(positions)
    derived = [{n: a[0] for n, a in _derive_weights(w).items()} for w in layers]
    saved = []
    for lw in derived:
        x, sv = _layer_forward(x, lw, tabs)
        saved.append(sv)
    loss, dx = _loss_and_grad(x, target)
    for l in reversed(range(len(layers))):
        dx, g = _layer_backward(dx, derived[l], saved[l], tabs)
        lead = lambda a: [b[None] for b in a] if isinstance(a, list) else a[None]
        after_layer_backward(l, _reference_layout_grads({n: lead(a) for n, a in g.items()}))
    return loss, dx


HBM = pl.BlockSpec(memory_space=pl.ANY)


def _chip_peers():
    x, y = lax.axis_index("x"), lax.axis_index("y")
    return [(1 - x, y), (x, 1 - y), (1 - x, 1 - y)]


def _relative_peers():
    x, y = lax.axis_index("x"), lax.axis_index("y")
    return {1: (x, 1 - y), 2: (1 - x, y), 3: (1 - x, 1 - y)}


def _gather_chips(name, split, whole):
    ns, nw = len(split), len(whole)
    n = ns + nw

    def body(*refs):
        ins, outs = refs[:n], refs[n:2 * n]
        ici_send, ici_recv, d2d_send, d2d_recv, local_sems = refs[2 * n:]
        x, y, c = lax.axis_index("x"), lax.axis_index("y"), lax.axis_index("c")
        peers = _relative_peers()

        def run(half):
            def rows(ref, which):
                h = ref.shape[-2] // 2
                return ref.at[(slice(None),) * (len(ref.shape) - 2) + (slice(which * h, (which + 1) * h), slice(None))]

            started = []
            for k in range(n):
                local = pltpu.make_async_copy(ins[k], outs[k].at[0], local_sems.at[k])
                local.start()
                started.append(local.wait)
                for r, (px, py) in peers.items():
                    src = rows(ins[k], half) if k < ns else ins[k]
                    dst = rows(outs[k].at[r], half) if k < ns else outs[k].at[r]
                    cp = pltpu.make_async_remote_copy(src_ref=src, dst_ref=dst, send_sem=ici_send.at[3 * k + r - 1],
                                                      recv_sem=ici_recv.at[3 * k + r - 1], device_id=(px, py, c),
                                                      device_id_type=MESH)
                    cp.start()
                    started.append(cp.wait_send)
            for k in range(n):
                for r, (px, py) in peers.items():
                    landed = rows(outs[k].at[r], half) if k < ns else outs[k].at[r]
                    pltpu.make_async_remote_copy(src_ref=landed, dst_ref=landed, send_sem=ici_send.at[3 * k + r - 1],
                                                 recv_sem=ici_recv.at[3 * k + r - 1], device_id=(px, py, c),
                                                 device_id_type=MESH).wait_recv()
                    if k < ns:
                        fwd = pltpu.make_async_remote_copy(src_ref=landed, dst_ref=landed, send_sem=d2d_send.at[3 * k + r - 1],
                                                           recv_sem=d2d_recv.at[3 * k + r - 1], device_id=(x, y, 1 - c),
                                                           device_id_type=MESH)
                        fwd.start()
                        started.append(fwd.wait_send)
            for k in range(ns):
                for r in peers:
                    other = rows(outs[k].at[r], 1 - half)
                    pltpu.make_async_remote_copy(src_ref=other, dst_ref=other, send_sem=d2d_send.at[3 * k + r - 1],
                                                 recv_sem=d2d_recv.at[3 * k + r - 1], device_id=(x, y, 1 - c),
                                                 device_id_type=MESH).wait_recv()
            for wait in started:
                wait()

        for half in (0, 1):
            pl.when(c == half)(functools.partial(run, half))

    arrs = list(split) + list(whole)
    return pl.pallas_call(
        body, name=name,
        in_specs=[HBM] * n, out_specs=[HBM] * n,
        out_shape=[jax.ShapeDtypeStruct((4,) + a.shape, a.dtype) for a in arrs],
        scratch_shapes=[pltpu.SemaphoreType.DMA((3 * n,)), pltpu.SemaphoreType.DMA((3 * n,)), pltpu.SemaphoreType.DMA((3 * ns,)),
                        pltpu.SemaphoreType.DMA((3 * ns,)), pltpu.SemaphoreType.DMA((n,))],
    )(*arrs)


def _absolute_chip_order(relative):
    me = 2 * lax.axis_index("x") + lax.axis_index("y")
    return jnp.stack([lax.dynamic_index_in_dim(relative, jnp.bitwise_xor(me, chip), 0, keepdims=False) for chip in range(4)])


def _half_rows(ref, which):
    h = ref.shape[1] // 2
    return ref.at[:, which * h:(which + 1) * h, :]


def _sibling_halves(name, arrs):
    n = len(arrs)

    def body(*refs):
        ins, mine, theirs = refs[:n], refs[n:2 * n], refs[2 * n:3 * n]
        send_sems, recv_sems, local_sems = refs[3 * n:]
        x, y, c = lax.axis_index("x"), lax.axis_index("y"), lax.axis_index("c")

        def exchange(my_half):
            copies = []
            for k in range(n):
                local = pltpu.make_async_copy(_half_rows(ins[k], my_half), mine[k], local_sems.at[k])
                local.start()
                cp = pltpu.make_async_remote_copy(src_ref=_half_rows(ins[k], 1 - my_half), dst_ref=theirs[k],
                                                  send_sem=send_sems.at[k], recv_sem=recv_sems.at[k],
                                                  device_id=(x, y, 1 - c), device_id_type=MESH)
                cp.start()
                copies += [cp, local]
            for cp in copies:
                cp.wait()

        for half in (0, 1):
            pl.when(c == half)(functools.partial(exchange, half))

    halves = [jax.ShapeDtypeStruct((a.shape[0], a.shape[1] // 2, a.shape[2]), a.dtype) for a in arrs]
    out = pl.pallas_call(
        body, name=name, in_specs=[HBM] * n, out_specs=[HBM] * (2 * n), out_shape=halves + halves,
        scratch_shapes=[pltpu.SemaphoreType.DMA((n,)), pltpu.SemaphoreType.DMA((n,)), pltpu.SemaphoreType.DMA((n,))],
    )(*arrs)
    return out[:n], out[n:]


def _scatter_chips(name, arrs):
    n = len(arrs)

    def body(*refs):
        ins, outs = refs[:n], refs[n:2 * n]
        send_sems, recv_sems, local_sems = refs[2 * n:]
        x, y, c = lax.axis_index("x"), lax.axis_index("y"), lax.axis_index("c")
        me = 2 * x + y
        local_copies, sends = [], []
        for k in range(n):
            local = pltpu.make_async_copy(ins[k].at[me], outs[k].at[me], local_sems.at[k])
            local.start()
            local_copies.append(local)
            for j, (px, py) in enumerate(_chip_peers()):
                cp = pltpu.make_async_remote_copy(src_ref=ins[k].at[2 * px + py], dst_ref=outs[k].at[me],
                                                  send_sem=send_sems.at[3 * k + j], recv_sem=recv_sems.at[3 * k + j],
                                                  device_id=(px, py, c), device_id_type=MESH)
                cp.start()
                sends.append(cp)
        for k in range(n):
            for j, (px, py) in enumerate(_chip_peers()):
                pltpu.make_async_remote_copy(src_ref=ins[k].at[me], dst_ref=outs[k].at[2 * px + py],
                                             send_sem=send_sems.at[3 * k + j], recv_sem=recv_sems.at[3 * k + j],
                                             device_id=(px, py, c), device_id_type=MESH).wait_recv()
        for cp in sends:
            cp.wait_send()
        for local in local_copies:
            local.wait()

    return pl.pallas_call(
        body, name=name, in_specs=[HBM] * n, out_specs=[HBM] * n,
        out_shape=[jax.ShapeDtypeStruct(a.shape, a.dtype) for a in arrs],
        scratch_shapes=[pltpu.SemaphoreType.DMA((3 * n,)), pltpu.SemaphoreType.DMA((3 * n,)), pltpu.SemaphoreType.DMA((n,))],
    )(*arrs)


def _join_siblings(name, halves):
    n = len(halves)

    def body(*refs):
        ins, outs = refs[:n], refs[n:2 * n]
        send_sems, recv_sems, local_sems = refs[2 * n:]
        x, y, c = lax.axis_index("x"), lax.axis_index("y"), lax.axis_index("c")

        def exchange(my_half):
            copies = []
            for k in range(n):
                rows = ins[k].shape[0]
                mine = outs[k].at[my_half * rows:(my_half + 1) * rows, :]
                theirs = outs[k].at[(1 - my_half) * rows:(2 - my_half) * rows, :]
                local = pltpu.make_async_copy(ins[k], mine, local_sems.at[k])
                local.start()
                cp = pltpu.make_async_remote_copy(src_ref=ins[k], dst_ref=mine, send_sem=send_sems.at[k],
                                                  recv_sem=recv_sems.at[k], device_id=(x, y, 1 - c), device_id_type=MESH)
                cp.start()
                arrival = pltpu.make_async_remote_copy(src_ref=ins[k], dst_ref=theirs, send_sem=send_sems.at[k],
                                                       recv_sem=recv_sems.at[k], device_id=(x, y, 1 - c), device_id_type=MESH)
                copies.append((local, cp, arrival))
            for local, cp, arrival in copies:
                arrival.wait_recv()
                cp.wait_send()
                local.wait()

        for half in (0, 1):
            pl.when(c == half)(functools.partial(exchange, half))

    return pl.pallas_call(
        body, name=name, in_specs=[HBM] * n, out_specs=[HBM] * n,
        out_shape=[jax.ShapeDtypeStruct((2 * h.shape[0], h.shape[1]), h.dtype) for h in halves],
        scratch_shapes=[pltpu.SemaphoreType.DMA((n,)), pltpu.SemaphoreType.DMA((n,)), pltpu.SemaphoreType.DMA((n,))],
    )(*halves)


def _gather_all(name, a):
    def body(a_ref, out_ref, send_sems, recv_sems, local_sem):
        x, y, c = lax.axis_index("x"), lax.axis_index("y"), lax.axis_index("c")
        me = 4 * x + 2 * y + c
        flips = [(fx, fy, fc) for fx in (0, 1) for fy in (0, 1) for fc in (0, 1)][1:]
        peers = [(x ^ fx, y ^ fy, c ^ fc) for fx, fy, fc in flips]
        local = pltpu.make_async_copy(a_ref, out_ref.at[me], local_sem)
        local.start()
        sends = []
        for j, peer in enumerate(peers):
            cp = pltpu.make_async_remote_copy(src_ref=a_ref, dst_ref=out_ref.at[me], send_sem=send_sems.at[j],
                                              recv_sem=recv_sems.at[j], device_id=peer, device_id_type=MESH)
            cp.start()
            sends.append(cp)
        for j, (px, py, pc) in enumerate(peers):
            pltpu.make_async_remote_copy(src_ref=a_ref, dst_ref=out_ref.at[4 * px + 2 * py + pc], send_sem=send_sems.at[j],
                                         recv_sem=recv_sems.at[j], device_id=(px, py, pc), device_id_type=MESH).wait_recv()
        for cp in sends:
            cp.wait_send()
        local.wait()

    return pl.pallas_call(
        body, name=name, in_specs=[HBM], out_specs=HBM, out_shape=jax.ShapeDtypeStruct((8,) + a.shape, a.dtype),
        scratch_shapes=[pltpu.SemaphoreType.DMA((7,)), pltpu.SemaphoreType.DMA((7,)), pltpu.SemaphoreType.DMA],
    )(a)


def _rowwise_call(name, fn, slots, out_shapes, steps, out_dtype=F32):
    n_in, n_out = [len(s) for s in slots], [len(o) for o in out_shapes]

    def spec(shape):
        if len(shape) == 3:
            return pl.BlockSpec((shape[0], shape[1] // steps, shape[2]), lambda i: (0, i, 0))
        return pl.BlockSpec((shape[0] // steps, shape[1]), lambda i: (i, 0))

    def body(*refs):
        ins, outs = refs[:sum(n_in)], refs[sum(n_in):]
        a = b = 0
        for k in range(len(slots)):
            for o_ref, val in zip(outs[b:b + n_out[k]], fn(*[r[...] for r in ins[a:a + n_in[k]]])):
                o_ref[...] = val.astype(out_dtype)
            a, b = a + n_in[k], b + n_out[k]

    flat_in = [arr for s in slots for arr in s]
    flat_out = [shp for o in out_shapes for shp in o]
    out = pl.pallas_call(body, name=name, grid=(steps,), in_specs=[spec(a.shape) for a in flat_in],
                         out_specs=[spec(s) for s in flat_out], out_shape=[jax.ShapeDtypeStruct(s, out_dtype) for s in flat_out],
                         compiler_params=_params(("parallel",)))(*flat_in)
    grouped, b = [], 0
    for k in range(len(slots)):
        grouped.append(out[b:b + n_out[k]])
        b += n_out[k]
    return grouped


def _sum_in_order(a):
    acc = a[0].astype(F32)
    for k in range(1, a.shape[0]):
        acc = acc + a[k].astype(F32)
    return (acc,)


def _adamw_math(w, g, m, v):
    m_new = ADAM_B1 * m + (1.0 - ADAM_B1) * g
    v_new = ADAM_B2 * v + (1.0 - ADAM_B2) * (g * g)
    m_hat = m_new / (1.0 - ADAM_B1 ** ADAM_STEP)
    v_hat = v_new / (1.0 - ADAM_B2 ** ADAM_STEP)
    return -ADAM_LR * (m_hat / (jnp.sqrt(v_hat) + ADAM_EPS) + ADAM_WD * w), m_new, v_new


def _pack(arrs, rows_per_layer, dtype):
    nl = arrs[0].shape[0]
    flat = jnp.concatenate([a.astype(dtype).reshape(nl, -1) for a in arrs], axis=1)
    flat = jnp.pad(flat, ((0, 0), (0, rows_per_layer * PACK_COLS - flat.shape[1])))
    return flat.reshape(nl * rows_per_layer, PACK_COLS)


def _unpack(packed, shapes, rows_per_layer):
    nl = shapes[0][0]
    flat = packed.reshape(packed.shape[:-2] + (nl, rows_per_layer * PACK_COLS))
    out, off = [], 0
    for shp in shapes:
        size = math.prod(shp[1:])
        out.append(flat[..., off:off + size].reshape(packed.shape[:-2] + tuple(shp)))
        off += size
    return out


def _rows_needed(shapes, multiple):
    per_layer = sum(math.prod(s[1:]) for s in shapes)
    rows = -(-per_layer // PACK_COLS)
    return -(-rows // multiple) * multiple


def _full_weights(w):
    nl = w['w_in'].shape[0]
    conv_flat = w['conv_w'].reshape(-1, 128)
    conv_rows = conv_flat.shape[0]
    conv_flat = jnp.pad(conv_flat, ((0, -conv_rows % 8), (0, 0)))
    gathered = _gather_chips("gather_weights", [w[n].astype(MXU_DTYPE) for n in MATMUL_SHARDED], [conv_flat])
    gathered = [_absolute_chip_order(g) for g in gathered]
    full = dict(w)
    for n, part in zip(MATMUL_SHARDED, gathered):
        if n in ROW_SHARDED:
            full[n] = jnp.swapaxes(part, 0, 1).reshape(nl, 4 * part.shape[2], part.shape[3])
        else:
            full[n] = jnp.transpose(part, (1, 2, 0, 3)).reshape(nl, part.shape[2], 4 * part.shape[3])
    conv = gathered[-1][:, :conv_rows].reshape((4,) + w['conv_w'].shape)
    full['conv_w'] = jnp.transpose(conv, (1, 2, 0, 3)).reshape(nl, 3, -1)
    return full


def _chip_major(n, g):
    nl = g.shape[0]
    if n in ROW_SHARDED:
        return jnp.swapaxes(g.reshape(nl, 4, g.shape[1] // 4, g.shape[2]), 0, 1)
    return jnp.transpose(g.reshape(nl, g.shape[1], 4, g.shape[2] // 4), (2, 0, 1, 3))


def kernel(x, positions, norm_mix_pre, w_in, gate_b, q_norm, w_uq, kv_norm, w_ukv, w_br_mla, sg_ln_g, sg_ln_b, sg_w, sg_b, w_br_sg, conv_w, w_br_conv, pool_w, pool_scale, w_br_pool, w_out, norm_mix_post, norm_ffn_pre, w_ff1, w_ff2, norm_ffn_post, loss_target, m_norm_mix_pre, m_w_in, m_gate_b, m_q_norm, m_w_uq, m_kv_norm, m_w_ukv, m_w_br_mla, m_sg_ln_g, m_sg_ln_b, m_sg_w, m_sg_b, m_w_br_sg, m_conv_w, m_w_br_conv, m_pool_w, m_pool_scale, m_w_br_pool, m_w_out, m_norm_mix_post, m_norm_ffn_pre, m_w_ff1, m_w_ff2, m_norm_ffn_post, v_norm_mix_pre, v_w_in, v_gate_b, v_q_norm, v_w_uq, v_kv_norm, v_w_ukv, v_w_br_mla, v_sg_ln_g, v_sg_ln_b, v_sg_w, v_sg_b, v_w_br_sg, v_conv_w, v_w_br_conv, v_pool_w, v_pool_scale, v_w_br_pool, v_w_out, v_norm_mix_post, v_norm_ffn_pre, v_w_ff1, v_w_ff2, v_norm_ffn_post):
    given = dict(locals())
    w = {n: given[n] for n in WEIGHTS}
    mom = {n: given['m_' + n] for n in WEIGHTS}
    var = {n: given['v_' + n] for n in WEIGHTS}
    nl = w['w_in'].shape[0]
    full = _full_weights(w)
    layers = [{n: full[n][l:l + 1] for n in WEIGHTS} for l in range(nl)]
    small_shapes = [(1,) + w[n].shape[1:] for n in SMALL_SHARDED]
    small_rows = _rows_needed(small_shapes, 256)
    reduced, local_small = {}, {}

    def reduce_layer(l, g):
        arrs = [_chip_major(n, g[n])[:, 0] for n in BIG_SHARDED]
        arrs.append(jnp.stack([_pack([_chip_major(n, g[n])[chip] for n in SMALL_SHARDED], small_rows, F32) for chip in range(4)]))
        mine, theirs = _sibling_halves("grads_to_sibling", arrs)
        chip_sum = _rowwise_call("add_sibling", lambda a, b: (a + b,), [[m_, t_] for m_, t_ in zip(mine, theirs)],
                                 [[m_.shape] for m_ in mine], 8, out_dtype=WIRE_DTYPE)
        arrived = _scatter_chips("grads_to_chips", [s_[0] for s_ in chip_sum])
        halves = _rowwise_call("sum_chips", _sum_in_order, [[a_] for a_ in arrived], [[a_.shape[1:]] for a_ in arrived], 8)
        reduced[l] = _join_siblings("join_halves", [h_[0] for h_ in halves])
        local_small[l] = [g[n] for n in REPLICATED + ['conv_w']]

    loss, dx = _local_step(x[0], positions[0], loss_target[0], layers, reduce_layer)
    loss = lax.psum(loss, ("x", "y", "c"))

    grad, delta, new_m, new_v = {}, {}, {}, {}
    slots = []
    for k, n in enumerate(BIG_SHARDED):
        grad[n] = jnp.stack([reduced[l][k] for l in range(nl)])
        slots.append([a.reshape(-1, a.shape[-1]) for a in (w[n], grad[n], mom[n], var[n])])
    small_pack = lambda d: _pack([d[n] for n in SMALL_SHARDED], small_rows, F32)
    g_small = jnp.stack([reduced[l][-1] for l in range(nl)])
    slots.append([small_pack(w), g_small.reshape(-1, PACK_COLS), small_pack(mom), small_pack(var)])
    updated = _rowwise_call("adamw_sharded", _adamw_math, slots, [[s_[0].shape] * 3 for s_ in slots], 32)
    for k, n in enumerate(BIG_SHARDED):
        delta[n], new_m[n], new_v[n] = [a.reshape(w[n].shape) for a in updated[k]]
    sm_shapes = [w[n].shape for n in SMALL_SHARDED]
    for d, packed in zip((grad, delta, new_m, new_v), [g_small.reshape(-1, PACK_COLS)] + list(updated[-1])):
        d.update(zip(SMALL_SHARDED, _unpack(packed, sm_shapes, small_rows)))

    names = REPLICATED + ['conv_w']
    local = [jnp.concatenate([local_small[l][k] for l in range(nl)]) for k in range(len(names))]
    rows = _rows_needed([a.shape for a in local], 32)
    everyone = _gather_all("gather_small_grads", _pack(local, rows, F32))
    (summed,), = _rowwise_call("sum_devices", _sum_in_order, [[everyone]], [[everyone.shape[1:]]], 4)
    g_rep = _unpack(summed, [a.shape for a in local], rows)
    chip = 2 * lax.axis_index("x") + lax.axis_index("y")
    g_rep[-1] = lax.dynamic_slice_in_dim(g_rep[-1], chip * w['conv_w'].shape[2], w['conv_w'].shape[2], axis=2)
    rep_pack = lambda arrs: _pack(arrs, rows, F32)
    (rep_out,) = _rowwise_call("adamw_replicated", _adamw_math,
                               [[rep_pack([w[n] for n in names]), rep_pack(g_rep), rep_pack([mom[n] for n in names]),
                                 rep_pack([var[n] for n in names])]], [[(nl * rows, PACK_COLS)] * 3], 4)
    grad.update(zip(names, g_rep))
    for d, packed in zip((delta, new_m, new_v), rep_out):
        d.update(zip(names, _unpack(packed, [w[n].shape for n in names], rows)))

    return (loss, dx[None], *[grad[n] for n in WEIGHTS], *[delta[n] for n in WEIGHTS], *[new_m[n] for n in WEIGHTS],
            *[new_v[n] for n in WEIGHTS])
```

```python
import functools
import math

import jax
import jax.numpy as jnp
from jax import lax
from jax.experimental import pallas as pl
from jax.experimental.pallas import tpu as pltpu

F32 = jnp.float32
MXU_DTYPE = jnp.bfloat16
WIRE_DTYPE = jnp.bfloat16
MESH = pl.DeviceIdType.MESH

D_MODEL = 1024
D_FF = 4096
N_HEADS = 4
QK_NOPE = 64
QK_ROPE = 32
V_HEAD = 64
HEAD_PAD = 128
Q_RANK = 256
KV_RANK = 128
SG_CHUNK = 128
SG_GROUPS = 4
BR_WIDTH = 256
N_BRANCH = 4
POOL_WINDOWS = (2, 4, 8, 16)
HALO = 16
ROPE_BASE = 10000.0
EPS = 1e-6
ATTN_SCALE = (QK_NOPE + QK_ROPE) ** -0.5
N_PROJ = N_BRANCH * D_MODEL + 6 * BR_WIDTH + Q_RANK + KV_RANK + 2 * HEAD_PAD
COL_G, COL_M1, COL_M2, COL_B = 0, 4096, 4864, 5632

ADAM_LR, ADAM_B1, ADAM_B2, ADAM_EPS, ADAM_WD, ADAM_STEP = 0.001, 0.9, 0.999, 1e-08, 0.01, 10

VMEM_LIMIT = 56 * 1024 * 1024

WEIGHTS = ['norm_mix_pre', 'w_in', 'gate_b', 'q_norm', 'w_uq', 'kv_norm', 'w_ukv', 'w_br_mla', 'sg_ln_g', 'sg_ln_b',
           'sg_w', 'sg_b', 'w_br_sg', 'conv_w', 'w_br_conv', 'pool_w', 'pool_scale', 'w_br_pool', 'w_out',
           'norm_mix_post', 'norm_ffn_pre', 'w_ff1', 'w_ff2', 'norm_ffn_post']
COL_SHARDED = ['w_in', 'w_uq', 'w_ukv', 'w_br_mla', 'w_br_sg', 'w_br_conv', 'w_br_pool', 'w_ff1']
ROW_SHARDED = ['w_out', 'w_ff2']
MATMUL_SHARDED = ['w_in', 'w_uq', 'w_ukv', 'w_br_mla', 'w_br_sg', 'w_br_conv', 'w_br_pool', 'w_out', 'w_ff1', 'w_ff2']
BIG_SHARDED = ['w_in', 'w_ff1', 'w_ff2', 'w_out']
SMALL_SHARDED = ['w_uq', 'w_ukv', 'w_br_mla', 'w_br_sg', 'w_br_conv', 'w_br_pool']
SHARDED = MATMUL_SHARDED + ['conv_w']
REPLICATED = [n for n in WEIGHTS if n not in SHARDED]
PACK_COLS = 1024


def _params(sem, vmem=VMEM_LIMIT):
    return pltpu.CompilerParams(dimension_semantics=sem, vmem_limit_bytes=vmem)


def _mxu(a):
    return a.astype(MXU_DTYPE)


def _dot(a, b):
    return jnp.dot(_mxu(a), _mxu(b), preferred_element_type=F32)


def _dot_nt(a, b):
    return lax.dot_general(_mxu(a), _mxu(b), (((1,), (1,)), ((), ())), preferred_element_type=F32)


def _dot_tn(a, b):
    return lax.dot_general(_mxu(a), _mxu(b), (((0,), (0,)), ((), ())), preferred_element_type=F32)


def _rms(x, g):
    r = lax.rsqrt(jnp.mean(x * x, axis=-1, keepdims=True) + EPS)
    return x * r * g


def _rms_bwd(x, g, dy):
    r = lax.rsqrt(jnp.mean(x * x, axis=-1, keepdims=True) + EPS)
    xh = x * r
    gdy = dy * g
    dx = r * (gdy - xh * jnp.mean(gdy * xh, axis=-1, keepdims=True))
    return dx, jnp.sum(dy * xh, axis=0, keepdims=True)


_GELU_C = math.sqrt(2.0 / math.pi)


def _gelu(x):
    t = jnp.tanh(_GELU_C * (x + 0.044715 * (x * x * x)))
    return x * (0.5 * (1.0 + t)), t


def _gelu_grad(x, t):
    return 0.5 * (1.0 + t) + 0.5 * x * (1.0 - t * t) * (_GELU_C * (1.0 + 3.0 * 0.044715 * x * x))


def _sigmoid(x):
    return 1.0 / (1.0 + jnp.exp(-x))


def _full(shape):
    return pl.BlockSpec(shape, lambda *_: (0,) * len(shape))


def _rows(ts, width, col=0):
    return pl.BlockSpec((ts, width), lambda i: (i, col))


def _tile(n, pref):
    return min(n, pref)


def _mm(name, a, w, *, tm, tn, prologue=None, rows=()):
    m, k = a.shape
    n = w.shape[1]
    tm, tn = _tile(m, tm), _tile(n, tn)

    def body(a_ref, *rest):
        row_refs, w_ref, o_ref = rest[:len(rows)], rest[len(rows)], rest[len(rows) + 1]
        av = a_ref[...]
        if prologue is not None:
            av = prologue(av, *[r[...] for r in row_refs])
        o_ref[...] = _dot(av, w_ref[...])

    return pl.pallas_call(
        body, name=name, grid=(m // tm, n // tn),
        in_specs=[pl.BlockSpec((tm, k), lambda i, j: (i, 0))] + [pl.BlockSpec((1, k), lambda i, j: (0, 0)) for _ in rows]
        + [pl.BlockSpec((k, tn), lambda i, j: (0, j))],
        out_specs=pl.BlockSpec((tm, tn), lambda i, j: (i, j)),
        out_shape=jax.ShapeDtypeStruct((m, n), F32),
        compiler_params=_params(("parallel", "parallel")),
    )(a, *rows, w)


def _mm_tn(name, a, b, *, tm, tn, prologue=None, rows=()):
    m, k = a.shape
    n = b.shape[1]
    tm, tn = _tile(m, tm), _tile(n, tn)

    def body(a_ref, *rest):
        row_refs, b_ref, o_ref = rest[:len(rows)], rest[len(rows)], rest[len(rows) + 1]

        @pl.when(pl.program_id(1) == 0)
        def _():
            o_ref[...] = jnp.zeros_like(o_ref)

        av = a_ref[...]
        if prologue is not None:
            av = prologue(av, *[r[...] for r in row_refs])
        o_ref[...] += _dot_tn(av, b_ref[...])

    return pl.pallas_call(
        body, name=name, grid=(n // tn, m // tm),
        in_specs=[pl.BlockSpec((tm, k), lambda j, i: (i, 0))] + [pl.BlockSpec((1, k), lambda j, i: (0, 0)) for _ in rows]
        + [pl.BlockSpec((tm, tn), lambda j, i: (i, j))],
        out_specs=pl.BlockSpec((k, tn), lambda j, i: (0, j)),
        out_shape=jax.ShapeDtypeStruct((k, n), F32),
        compiler_params=_params(("parallel", "arbitrary")),
    )(a, *rows, b)


def _relu_sq(a):
    r = jnp.maximum(a, 0.0)
    return r * r


def _qkv_prep(proj, q_norm, kv_norm, wq, wkv, cq_tab, s_tab, cr_tab):
    s = proj.shape[0]
    ts = _tile(s, 512)
    hw = N_HEADS * HEAD_PAD

    def body(cq_ref, ckv_ref, kr_ref, krs_ref, gq_ref, gkv_ref, wq_ref, wkv_ref, ct_ref, st_ref, crt_ref,
             q_ref, k_ref, v_ref):
        ct, st, crt = ct_ref[...], st_ref[...], crt_ref[...]
        qn = _rms(cq_ref[...], gq_ref[...])
        qab = _dot(qn, wq_ref[...])
        kvn = _rms(ckv_ref[...], gkv_ref[...])
        kav = _dot(kvn, wkv_ref[...])
        k_rope = kr_ref[...] * crt + krs_ref[...] * st
        for h in range(N_HEADS):
            lo = h * HEAD_PAD
            q_ref[h] = (qab[:, lo:lo + HEAD_PAD] * ct + qab[:, hw + lo:hw + lo + HEAD_PAD] * st).astype(q_ref.dtype)
            k_ref[h] = (kav[:, lo:lo + HEAD_PAD] + k_rope).astype(k_ref.dtype)
            v_ref[h] = kav[:, hw + lo:hw + lo + HEAD_PAD].astype(v_ref.dtype)

    head_spec = pl.BlockSpec((N_HEADS, ts, HEAD_PAD), lambda i: (0, i, 0))
    head_shape = jax.ShapeDtypeStruct((N_HEADS, s, HEAD_PAD), MXU_DTYPE)
    return pl.pallas_call(
        body, name="qkv_prep", grid=(s // ts,),
        in_specs=[_rows(ts, Q_RANK, COL_B // Q_RANK), _rows(ts, KV_RANK, (COL_B + Q_RANK) // KV_RANK),
                  _rows(ts, HEAD_PAD, (COL_B + Q_RANK + KV_RANK) // HEAD_PAD),
                  _rows(ts, HEAD_PAD, (COL_B + Q_RANK + KV_RANK + HEAD_PAD) // HEAD_PAD),
                  _full((1, Q_RANK)), _full((1, KV_RANK)), _full((Q_RANK, 2 * hw)), _full((KV_RANK, 2 * hw)),
                  _rows(ts, HEAD_PAD), _rows(ts, HEAD_PAD), _rows(ts, HEAD_PAD)],
        out_specs=[head_spec, head_spec, head_spec],
        out_shape=[head_shape, head_shape, head_shape],
        compiler_params=_params(("parallel",)),
    )(proj, proj, proj, proj, q_norm, kv_norm, wq, wkv, cq_tab, s_tab, cr_tab)


def _diagonal_mask(t):
    return lax.broadcasted_iota(jnp.int32, (t, t), 1) <= lax.broadcasted_iota(jnp.int32, (t, t), 0)


def _attn_fwd(q, k, v):
    s = q.shape[1]
    t = _tile(s, 512)

    def body(q_ref, k_ref, v_ref, o_ref, lse_ref):
        i = pl.program_id(1)
        qv = q_ref[0]

        def step(j, carry, on_diagonal):
            m, l, acc = carry
            kj = k_ref[0, pl.ds(pl.multiple_of(j * t, t), t), :]
            vj = v_ref[0, pl.ds(pl.multiple_of(j * t, t), t), :]
            sc = _dot_nt(qv, kj) * ATTN_SCALE
            if on_diagonal:
                sc = jnp.where(_diagonal_mask(t), sc, -jnp.inf)
            m_new = jnp.maximum(m, jnp.max(sc, axis=1, keepdims=True))
            p = jnp.exp(sc - m_new)
            alpha = jnp.exp(m - m_new)
            return m_new, alpha * l + jnp.sum(p, axis=1, keepdims=True), alpha * acc + _dot(p, vj)

        init = (jnp.full((t, 1), -jnp.inf, F32), jnp.zeros((t, 1), F32), jnp.zeros((t, HEAD_PAD), F32))
        below = lax.fori_loop(0, i, functools.partial(step, on_diagonal=False), init)
        m, l, acc = step(i, below, True)
        o_ref[...] = acc / l
        lse_ref[0] = m + jnp.log(l)

    return pl.pallas_call(
        body, name="attn_fwd", grid=(N_HEADS, s // t),
        in_specs=[pl.BlockSpec((1, t, HEAD_PAD), lambda h, i: (h, i, 0)),
                  pl.BlockSpec((1, s, HEAD_PAD), lambda h, i: (h, 0, 0)),
                  pl.BlockSpec((1, s, HEAD_PAD), lambda h, i: (h, 0, 0))],
        out_specs=[pl.BlockSpec((t, HEAD_PAD), lambda h, i: (i, h)), pl.BlockSpec((1, t, 1), lambda h, i: (h, i, 0))],
        out_shape=[jax.ShapeDtypeStruct((s, N_HEADS * HEAD_PAD), F32), jax.ShapeDtypeStruct((N_HEADS, s, 1), F32)],
        compiler_params=_params(("parallel", "parallel")),
    )(q, k, v)


def _attn_bwd(q, k, v, do, lse, delta):
    s = q.shape[1]
    t = _tile(s, 512)
    nq = s // t

    def body(q_ref, do_ref, lse_ref, dl_ref, k_ref, v_ref, dq_ref, dk_ref, dv_ref):
        j = pl.program_id(1)

        @pl.when(j == 0)
        def _():
            dq_ref[...] = jnp.zeros_like(dq_ref)

        kj, vj = k_ref[0], v_ref[0]

        def step(i, carry, on_diagonal):
            dk, dv = carry
            rows = pl.ds(pl.multiple_of(i * t, t), t)
            qi, doi = q_ref[0, rows, :], do_ref[rows, :]
            sc = _dot_nt(qi, kj) * ATTN_SCALE
            if on_diagonal:
                sc = jnp.where(_diagonal_mask(t), sc, -jnp.inf)
            p = jnp.exp(sc - lse_ref[0, rows, :])
            dv = dv + _dot_tn(p, doi)
            dp = _dot_nt(doi, vj)
            ds = p * (dp - dl_ref[0, rows, :]) * ATTN_SCALE
            dk = dk + _dot_tn(ds, qi)
            dq_ref[0, rows, :] += _dot(ds, kj)
            return dk, dv

        zero = jnp.zeros((t, HEAD_PAD), F32)
        dk, dv = lax.fori_loop(j + 1, nq, functools.partial(step, on_diagonal=False), step(j, (zero, zero), True))
        dk_ref[0] = dk
        dv_ref[0] = dv

    whole = lambda w: pl.BlockSpec((1, s, w), lambda h, j: (h, 0, 0))
    tile = pl.BlockSpec((1, t, HEAD_PAD), lambda h, j: (h, j, 0))
    head_shape = jax.ShapeDtypeStruct((N_HEADS, s, HEAD_PAD), F32)
    return pl.pallas_call(
        body, name="attn_bwd", grid=(N_HEADS, nq),
        in_specs=[whole(HEAD_PAD), pl.BlockSpec((s, HEAD_PAD), lambda h, j: (0, h)), whole(1), whole(1), tile, tile],
        out_specs=[whole(HEAD_PAD), tile, tile],
        out_shape=[head_shape, head_shape, head_shape],
        compiler_params=_params(("parallel", "arbitrary")),
    )(q, do, lse, delta, k, v)


def _qkv_bwd(dq, dk, dv, proj, q_norm, kv_norm, wq, wkv, cq_tab, s_tab, cr_tab):
    s = proj.shape[0]
    ts = _tile(s, 256)
    hw = N_HEADS * HEAD_PAD

    def body(dq_ref, dk_ref, dv_ref, cq_ref, ckv_ref, gq_ref, gkv_ref, wq_ref, wkv_ref, ct_ref, st_ref, crt_ref,
             dpb_ref, dwq_ref, dwkv_ref, dgq_ref, dgkv_ref):
        @pl.when(pl.program_id(0) == 0)
        def _():
            for r in (dwq_ref, dwkv_ref, dgq_ref, dgkv_ref):
                r[...] = jnp.zeros_like(r)

        ct, st, crt = ct_ref[...], st_ref[...], crt_ref[...]
        dqs = [dq_ref[h] for h in range(N_HEADS)]
        dks = [dk_ref[h] for h in range(N_HEADS)]
        dqab = jnp.concatenate([d * ct for d in dqs] + [d * st for d in dqs], axis=1)
        dkav = jnp.concatenate(dks + [dv_ref[h] for h in range(N_HEADS)], axis=1)
        dk_sum = dks[0] + dks[1] + dks[2] + dks[3]
        cq, ckv, gq, gkv = cq_ref[...], ckv_ref[...], gq_ref[...], gkv_ref[...]
        dwq_ref[...] += _dot_tn(_rms(cq, gq), dqab)
        dwkv_ref[...] += _dot_tn(_rms(ckv, gkv), dkav)
        dcq, dgq = _rms_bwd(cq, gq, _dot_nt(dqab, wq_ref[...]))
        dckv, dgkv = _rms_bwd(ckv, gkv, _dot_nt(dkav, wkv_ref[...]))
        dgq_ref[...] += dgq
        dgkv_ref[...] += dgkv
        dpb_ref[...] = jnp.concatenate([dcq, dckv, dk_sum * crt, dk_sum * st], axis=1).astype(dpb_ref.dtype)

    head_spec = pl.BlockSpec((N_HEADS, ts, HEAD_PAD), lambda i: (0, i, 0))
    wb = Q_RANK + KV_RANK + 2 * HEAD_PAD
    return pl.pallas_call(
        body, name="qkv_bwd", grid=(s // ts,),
        in_specs=[head_spec, head_spec, head_spec,
                  _rows(ts, Q_RANK, COL_B // Q_RANK), _rows(ts, KV_RANK, (COL_B + Q_RANK) // KV_RANK),
                  _full((1, Q_RANK)), _full((1, KV_RANK)), _full((Q_RANK, 2 * hw)), _full((KV_RANK, 2 * hw)),
                  _rows(ts, HEAD_PAD), _rows(ts, HEAD_PAD), _rows(ts, HEAD_PAD)],
        out_specs=[_rows(ts, wb), _full((Q_RANK, 2 * hw)), _full((KV_RANK, 2 * hw)), _full((1, Q_RANK)), _full((1, KV_RANK))],
        out_shape=[jax.ShapeDtypeStruct((s, wb), MXU_DTYPE), jax.ShapeDtypeStruct((Q_RANK, 2 * hw), F32),
                   jax.ShapeDtypeStruct((KV_RANK, 2 * hw), F32), jax.ShapeDtypeStruct((1, Q_RANK), F32),
                   jax.ShapeDtypeStruct((1, KV_RANK), F32)],
        compiler_params=_params(("arbitrary",)),
    )(dq, dk, dv, proj, proj, q_norm, kv_norm, wq, wkv, cq_tab, s_tab, cr_tab)


def _lane_group(width):
    return lax.broadcasted_iota(jnp.int32, (1, width), 1) // (width // 4)


def _shift_down(a, k):
    return pltpu.roll(a, k, 0)


def _shift_up(a, k):
    return pltpu.roll(a, a.shape[0] - k, 0)


def _window_sums(xh, shift):
    s2 = xh + shift(xh, 1)
    s4 = s2 + shift(s2, 2)
    s8 = s4 + shift(s4, 4)
    s16 = s8 + shift(s8, 8)
    grp = _lane_group(xh.shape[1])
    return jnp.where(grp == 0, s2, jnp.where(grp == 1, s4, jnp.where(grp == 2, s8, s16)))


def _pool_count(i, ts):
    grp = _lane_group(BR_WIDTH)
    win = jnp.where(grp == 0, 2.0, jnp.where(grp == 1, 4.0, jnp.where(grp == 2, 8.0, 16.0)))
    t = (i * ts + lax.broadcasted_iota(jnp.int32, (ts, 1), 0)).astype(F32)
    return jnp.minimum(t + 1.0, win)


def _mix_forward(i, ts, r):
    f = {}
    f['gates'] = _sigmoid(r['gate'][...] + r['gate_b'][...])
    sgu, sgv = r['sgu'][...], r['sgv'][...]
    f['sgu'], f['sgv'] = sgu, sgv
    u_act, f['tu'] = _gelu(sgu)
    vg, f['tv'] = _gelu(sgv)
    mu = jnp.mean(vg, axis=-1, keepdims=True)
    xc = vg - mu
    f['ln_r'] = lax.rsqrt(jnp.mean(xc * xc, axis=-1, keepdims=True) + EPS)
    f['ln_xh'] = xc * f['ln_r']
    vln = f['ln_xh'] * r['ln_g'][...] + r['ln_b'][...]
    tril = lax.broadcasted_iota(jnp.int32, (SG_CHUNK, SG_CHUNK), 1) <= lax.broadcasted_iota(jnp.int32, (SG_CHUNK, SG_CHUNK), 0)
    f['wm'] = [_mxu(jnp.where(tril, r['sg_w'][g], 0.0)) for g in range(SG_GROUPS)]
    f['tril'] = tril
    grp = _lane_group(BR_WIDTH)
    bias = r['sg_bias'][...]
    parts = []
    for ci in range(ts // SG_CHUNK):
        vc = vln[ci * SG_CHUNK:(ci + 1) * SG_CHUNK]
        sc = bias
        for g in range(SG_GROUPS):
            sc = sc + jnp.where(grp == g, _dot(f['wm'][g], vc), 0.0)
        parts.append(sc)
    f['vln'] = vln
    f['sg_s'] = parts[0] if len(parts) == 1 else jnp.concatenate(parts, axis=0)
    f['u_act'] = u_act
    out_b = u_act * f['sg_s']
    first = (i > 0).astype(F32)
    cvx, cvc, cvb = r['cvx'][...], r['cvc'][...], r['cvb'][...]
    f['cvx'], f['cvc'], f['cvb'] = cvx, cvc, cvb
    zh = jnp.concatenate([r['hx'][...] * r['hc'][...] * first, cvc * cvx], axis=0)
    f['z1'] = _shift_down(zh, 1)[HALO:]
    f['z2'] = _shift_down(zh, 2)[HALO:]
    f['z0'] = zh[HALO:]
    f['yv'] = r['conv_w'][0:1, :] * f['z2'] + r['conv_w'][1:2, :] * f['z1'] + r['conv_w'][2:3, :] * f['z0']
    out_c = cvb * f['yv']
    p = r['pool'][...]
    ph = jnp.concatenate([r['hp'][...] * first, p], axis=0)
    f['cnt'] = _pool_count(i, ts)
    f['pooled'] = _window_sums(ph, _shift_down)[HALO:] / f['cnt'] - p
    f['mixed'] = _dot(f['pooled'], r['wbd'][...])
    out_d = f['mixed'] * r['pool_scale'][...]
    f['outs'] = [r['o'][...], out_b, out_c, out_d]
    f['ys'] = [_dot(f['outs'][b], r['w_br'][b][...]) for b in range(N_BRANCH)]
    merged = f['gates'][:, 0:D_MODEL] * f['ys'][0]
    for b in range(1, N_BRANCH):
        merged = merged + f['gates'][:, b * D_MODEL:(b + 1) * D_MODEL] * f['ys'][b]
    f['merged'] = merged
    f['mo'] = _dot(merged, r['w_out'][...])
    return f


_MIX_TILE_INPUTS = ['gate', 'sgu', 'sgv', 'cvb', 'cvx', 'cvc', 'pool', 'hx', 'hc', 'hp', 'o']
_MIX_WEIGHTS = ['gate_b', 'ln_g', 'ln_b', 'sg_w', 'sg_bias', 'conv_w', 'wbd', 'pool_scale', 'w_br0', 'w_br1', 'w_br2',
                'w_br3', 'w_out', 'g_post']


def _mix_specs(s, ts):
    c0 = COL_M1 // BR_WIDTH
    prev = lambda col: pl.BlockSpec((HALO, BR_WIDTH), lambda i: (jnp.maximum(i * (ts // HALO) - 1, 0), col))
    tiles = [_rows(ts, N_BRANCH * D_MODEL, 0), _rows(ts, BR_WIDTH, c0), _rows(ts, BR_WIDTH, c0 + 1), _rows(ts, BR_WIDTH, c0 + 2),
             _rows(ts, BR_WIDTH, c0 + 3), _rows(ts, BR_WIDTH, c0 + 4), _rows(ts, BR_WIDTH, c0 + 5),
             prev(c0 + 3), prev(c0 + 4), prev(c0 + 5), _rows(ts, N_HEADS * HEAD_PAD)]
    weights = [_full((1, N_BRANCH * D_MODEL)), _full((1, BR_WIDTH)), _full((1, BR_WIDTH)),
               _full((SG_GROUPS, SG_CHUNK, SG_CHUNK)), _full((SG_CHUNK, BR_WIDTH)), _full((8, BR_WIDTH)),
               _full((BR_WIDTH, BR_WIDTH)), _full((1, BR_WIDTH)), _full((N_HEADS * HEAD_PAD, D_MODEL)),
               _full((BR_WIDTH, D_MODEL)), _full((BR_WIDTH, D_MODEL)), _full((BR_WIDTH, D_MODEL)),
               _full((D_MODEL, D_MODEL)), _full((1, D_MODEL))]
    return tiles, weights


def _mix_refs(refs):
    names = _MIX_TILE_INPUTS + _MIX_WEIGHTS
    r = dict(zip(names, refs[:len(names)]))
    r['w_br'] = [r['w_br0'], r['w_br1'], r['w_br2'], r['w_br3']]
    return r, refs[len(names):]


def _mix_operands(proj, o, lw):
    return ([proj] * 10 + [o] + [lw[n] for n in ['gate_b', 'sg_ln_g', 'sg_ln_b', 'sg_w', 'sg_bias', 'conv_w8', 'wbd',
                                                 'pool_scale', 'w_br_mla_p', 'w_br_sg', 'w_br_conv', 'w_br_pool', 'w_out',
                                                 'norm_mix_post']])


def _mix_fwd(x0, proj, o, lw):
    s = x0.shape[0]
    ts = _tile(s, 256)
    tiles, weights = _mix_specs(s, ts)

    def body(*refs):
        r, (x0_ref, x1_ref) = _mix_refs(refs)
        f = _mix_forward(pl.program_id(0), ts, r)
        x1_ref[...] = x0_ref[...] + _rms(f['mo'], r['g_post'][...])

    return pl.pallas_call(
        body, name="mix_fwd", grid=(s // ts,),
        in_specs=tiles + weights + [_rows(ts, D_MODEL)],
        out_specs=_rows(ts, D_MODEL),
        out_shape=jax.ShapeDtypeStruct((s, D_MODEL), F32),
        compiler_params=_params(("parallel",)),
    )(*_mix_operands(proj, o, lw), x0)


def _mix_bwd(dx1, proj, o, lw):
    s = dx1.shape[0]
    ts = _tile(s, 128)
    tiles, weights = _mix_specs(s, ts)
    hw = N_HEADS * HEAD_PAD

    def body(*refs):
        r, rest = _mix_refs(refs)
        (dx1_ref, dg_ref, dm1_ref, dyv_ref, up_ref, do_ref, delta_ref,
         dgate_b_ref, dln_g_ref, dln_b_ref, dsgw_ref, dsgb_ref, dconv_ref, dwbd_ref, dps_ref,
         dwbr0_ref, dwbr1_ref, dwbr2_ref, dwbr3_ref, dwout_ref, dgpost_ref, dbias_acc) = rest
        i = pl.program_id(0)
        acc_refs = [dgate_b_ref, dln_g_ref, dln_b_ref, dsgw_ref, dsgb_ref, dconv_ref, dwbd_ref, dps_ref,
                    dwbr0_ref, dwbr1_ref, dwbr2_ref, dwbr3_ref, dwout_ref, dgpost_ref, dbias_acc]

        @pl.when(i == 0)
        def _():
            for a in acc_refs:
                a[...] = jnp.zeros_like(a)

        f = _mix_forward(i, ts, r)
        dmo, dgpost = _rms_bwd(f['mo'], r['g_post'][...], dx1_ref[...])
        dgpost_ref[...] += dgpost
        dwout_ref[...] += _dot_tn(f['merged'], dmo)
        dmerged = _dot_nt(dmo, r['w_out'][...])
        dwbr = [dwbr0_ref, dwbr1_ref, dwbr2_ref, dwbr3_ref]
        douts = []
        for b in range(N_BRANCH):
            gb = f['gates'][:, b * D_MODEL:(b + 1) * D_MODEL]
            dgate = dmerged * f['ys'][b] * gb * (1.0 - gb)
            dg_ref[:, b * D_MODEL:(b + 1) * D_MODEL] = dgate.astype(dg_ref.dtype)
            dgate_b_ref[:, b * D_MODEL:(b + 1) * D_MODEL] += jnp.sum(dgate, axis=0, keepdims=True)
            dy = dmerged * gb
            dwbr[b][...] += _dot_tn(f['outs'][b], dy)
            douts.append(_dot_nt(dy, r['w_br'][b][...]))
        do = douts[0]
        do_ref[...] = do.astype(do_ref.dtype)
        prod = do * f['outs'][0]
        for h in range(N_HEADS):
            delta_ref[h] = jnp.sum(prod[:, h * HEAD_PAD:(h + 1) * HEAD_PAD], axis=1, keepdims=True)
        grp = _lane_group(BR_WIDTH)
        ds = douts[1] * f['u_act']
        dsgu = douts[1] * f['sg_s'] * _gelu_grad(f['sgu'], f['tu'])
        dvln_parts = []
        for ci in range(ts // SG_CHUNK):
            rows = slice(ci * SG_CHUNK, (ci + 1) * SG_CHUNK)
            ds_c, vln_c = ds[rows], f['vln'][rows]
            dvln_c = jnp.zeros((SG_CHUNK, BR_WIDTH), F32)
            for g in range(SG_GROUPS):
                dvln_c = dvln_c + jnp.where(grp == g, _dot_tn(f['wm'][g], ds_c), 0.0)
                dsgw_ref[g] += jnp.where(f['tril'], _dot_nt(jnp.where(grp == g, ds_c, 0.0), vln_c), 0.0)
            dbias_acc[...] += ds_c
            dvln_parts.append(dvln_c)
        dvln = dvln_parts[0] if len(dvln_parts) == 1 else jnp.concatenate(dvln_parts, axis=0)
        dln_g_ref[...] += jnp.sum(dvln * f['ln_xh'], axis=0, keepdims=True)
        dln_b_ref[...] += jnp.sum(dvln, axis=0, keepdims=True)
        dxh = dvln * r['ln_g'][...]
        dvg = f['ln_r'] * (dxh - jnp.mean(dxh, axis=-1, keepdims=True)
                           - f['ln_xh'] * jnp.mean(dxh * f['ln_xh'], axis=-1, keepdims=True))
        dsgv = dvg * _gelu_grad(f['sgv'], f['tv'])
        dcvb = douts[2] * f['yv']
        dyv = douts[2] * f['cvb']
        dyv_ref[...] = dyv
        for kk, zk in enumerate((f['z2'], f['z1'], f['z0'])):
            dconv_ref[kk:kk + 1, :] += jnp.sum(dyv * zk, axis=0, keepdims=True)
        dps_ref[...] += jnp.sum(douts[3] * f['mixed'], axis=0, keepdims=True)
        dmixed = douts[3] * r['pool_scale'][...]
        dwbd_ref[...] += _dot_tn(f['pooled'], dmixed)
        up_ref[...] = _dot_nt(dmixed, r['wbd'][...]) / f['cnt']
        dm1_ref[...] = jnp.concatenate([dsgu, dsgv, dcvb], axis=1).astype(dm1_ref.dtype)

        @pl.when(i == pl.num_programs(0) - 1)
        def _():
            lane = lax.broadcasted_iota(jnp.int32, (1, SG_CHUNK), 1)
            db = dbias_acc[...]
            out = jnp.zeros((SG_CHUNK, SG_CHUNK), F32)
            for g in range(SG_GROUPS):
                out = out + jnp.where(lane == g, jnp.sum(jnp.where(grp == g, db, 0.0), axis=1, keepdims=True), 0.0)
            dsgb_ref[...] = out

    acc = lambda shape: (_full(shape), jax.ShapeDtypeStruct(shape, F32))
    accs = [acc((1, N_BRANCH * D_MODEL)), acc((1, BR_WIDTH)), acc((1, BR_WIDTH)), acc((SG_GROUPS, SG_CHUNK, SG_CHUNK)),
            acc((SG_CHUNK, SG_CHUNK)), acc((8, BR_WIDTH)), acc((BR_WIDTH, BR_WIDTH)), acc((1, BR_WIDTH)),
            acc((hw, D_MODEL)), acc((BR_WIDTH, D_MODEL)), acc((BR_WIDTH, D_MODEL)), acc((BR_WIDTH, D_MODEL)),
            acc((D_MODEL, D_MODEL)), acc((1, D_MODEL))]
    tile_outs = [(_rows(ts, N_BRANCH * D_MODEL), jax.ShapeDtypeStruct((s, N_BRANCH * D_MODEL), MXU_DTYPE)),
                 (_rows(ts, 3 * BR_WIDTH), jax.ShapeDtypeStruct((s, 3 * BR_WIDTH), MXU_DTYPE)),
                 (_rows(ts, BR_WIDTH), jax.ShapeDtypeStruct((s, BR_WIDTH), F32)),
                 (_rows(ts, BR_WIDTH), jax.ShapeDtypeStruct((s, BR_WIDTH), F32)),
                 (_rows(ts, hw), jax.ShapeDtypeStruct((s, hw), MXU_DTYPE)),
                 (pl.BlockSpec((N_HEADS, ts, 1), lambda i: (0, i, 0)), jax.ShapeDtypeStruct((N_HEADS, s, 1), F32))]
    outs = tile_outs + accs
    return pl.pallas_call(
        body, name="mix_bwd", grid=(s // ts,),
        in_specs=tiles + weights + [_rows(ts, D_MODEL)],
        out_specs=[o_[0] for o_ in outs], out_shape=[o_[1] for o_ in outs],
        scratch_shapes=[pltpu.VMEM((SG_CHUNK, BR_WIDTH), F32)],
        compiler_params=_params(("arbitrary",), 60 * 1024 * 1024),
    )(*_mix_operands(proj, o, lw), dx1)


def _shift_bwd(dyv, upool, proj, conv_w8):
    s = dyv.shape[0]
    ts = _tile(s, 512)
    nb = s // HALO
    c0 = COL_M1 // BR_WIDTH

    def body(dyv_ref, dyvn_ref, up_ref, upn_ref, cvx_ref, cvc_ref, cw_ref, out_ref):
        i = pl.program_id(0)
        last = (i < pl.num_programs(0) - 1).astype(F32)
        dh = jnp.concatenate([dyv_ref[...], dyvn_ref[...] * last], axis=0)
        dz = (cw_ref[2:3, :] * dh + cw_ref[1:2, :] * _shift_up(dh, 1) + cw_ref[0:1, :] * _shift_up(dh, 2))[:ts]
        up = up_ref[...]
        uh = jnp.concatenate([up, upn_ref[...] * last], axis=0)
        dpool = _window_sums(uh, _shift_up)[:ts] - up * _pool_count(i, ts)
        out_ref[...] = jnp.concatenate([dz * cvc_ref[...], dz * cvx_ref[...], dpool], axis=1).astype(out_ref.dtype)

    nxt = pl.BlockSpec((HALO, BR_WIDTH), lambda i: (jnp.minimum((i + 1) * (ts // HALO), nb - 1), 0))
    return pl.pallas_call(
        body, name="shift_bwd", grid=(s // ts,),
        in_specs=[_rows(ts, BR_WIDTH), nxt, _rows(ts, BR_WIDTH), nxt, _rows(ts, BR_WIDTH, c0 + 3), _rows(ts, BR_WIDTH, c0 + 4),
                  _full((8, BR_WIDTH))],
        out_specs=_rows(ts, 3 * BR_WIDTH),
        out_shape=jax.ShapeDtypeStruct((s, 3 * BR_WIDTH), MXU_DTYPE),
        compiler_params=_params(("parallel",)),
    )(dyv, dyv, upool, upool, proj, proj, conv_w8)


def _ffn2(a, w2, x1, g):
    s = a.shape[0]
    ts = _tile(s, 256)

    def body(a_ref, w_ref, x1_ref, g_ref, x2_ref, f_ref):
        f = _dot(_relu_sq(a_ref[...]), w_ref[...])
        f_ref[...] = f
        x2_ref[...] = x1_ref[...] + _rms(f, g_ref[...])

    return pl.pallas_call(
        body, name="ffn2", grid=(s // ts,),
        in_specs=[_rows(ts, D_FF), _full((D_FF, D_MODEL)), _rows(ts, D_MODEL), _full((1, D_MODEL))],
        out_specs=[_rows(ts, D_MODEL), _rows(ts, D_MODEL)],
        out_shape=[jax.ShapeDtypeStruct((s, D_MODEL), F32)] * 2,
        compiler_params=_params(("parallel",)),
    )(a, w2, x1, g)


def _ffn2_bwd(dx2, f, g, a, w2t):
    s = a.shape[0]
    ts = _tile(s, 256)

    def body(dx2_ref, f_ref, g_ref, a_ref, w_ref, df_ref, da_ref, dg_ref):
        @pl.when(pl.program_id(0) == 0)
        def _():
            dg_ref[...] = jnp.zeros_like(dg_ref)

        df, dg = _rms_bwd(f_ref[...], g_ref[...], dx2_ref[...])
        dg_ref[...] += dg
        df_ref[...] = df.astype(df_ref.dtype)
        da_ref[...] = (_dot(df, w_ref[...]) * (2.0 * jnp.maximum(a_ref[...], 0.0))).astype(da_ref.dtype)

    return pl.pallas_call(
        body, name="ffn2_bwd", grid=(s // ts,),
        in_specs=[_rows(ts, D_MODEL), _rows(ts, D_MODEL), _full((1, D_MODEL)), _rows(ts, D_FF), _full((D_MODEL, D_FF))],
        out_specs=[_rows(ts, D_MODEL), _rows(ts, D_FF), _full((1, D_MODEL))],
        out_shape=[jax.ShapeDtypeStruct((s, D_MODEL), MXU_DTYPE), jax.ShapeDtypeStruct((s, D_FF), MXU_DTYPE),
                   jax.ShapeDtypeStruct((1, D_MODEL), F32)],
        compiler_params=_params(("arbitrary",)),
    )(dx2, f, g, a, w2t)


def _norm_in_bwd(name, pieces, x, g, dres):
    s = x.shape[0]
    ts = _tile(s, 256)
    n = len(pieces)

    def body(*refs):
        d_refs, w_refs = refs[:n], refs[n:2 * n]
        x_ref, g_ref, dres_ref, dx_ref, dg_ref = refs[2 * n:]

        @pl.when(pl.program_id(0) == 0)
        def _():
            dg_ref[...] = jnp.zeros_like(dg_ref)

        dh = _dot(d_refs[0][...], w_refs[0][...])
        for p in range(1, n):
            dh = dh + _dot(d_refs[p][...], w_refs[p][...])
        dx, dg = _rms_bwd(x_ref[...], g_ref[...], dh)
        dg_ref[...] += dg
        dx_ref[...] = dres_ref[...] + dx

    return pl.pallas_call(
        body, name=name, grid=(s // ts,),
        in_specs=[_rows(ts, d.shape[1]) for d, _ in pieces] + [_full(w.shape) for _, w in pieces]
        + [_rows(ts, D_MODEL), _full((1, D_MODEL)), _rows(ts, D_MODEL)],
        out_specs=[_rows(ts, D_MODEL), _full((1, D_MODEL))],
        out_shape=[jax.ShapeDtypeStruct((s, D_MODEL), F32), jax.ShapeDtypeStruct((1, D_MODEL), F32)],
        compiler_params=_params(("arbitrary",)),
    )(*[d for d, _ in pieces], *[w for _, w in pieces], x, g, dres)


def _loss_and_grad(y, target):
    s = y.shape[0]
    ts = _tile(s, 512)

    def body(y_ref, t_ref, dy_ref, loss_ref):
        @pl.when(pl.program_id(0) == 0)
        def _():
            loss_ref[...] = jnp.zeros_like(loss_ref)

        err = y_ref[...] - t_ref[...]
        dy_ref[...] = err * (1.0 / D_MODEL)
        loss_ref[...] += 0.5 * jnp.sum(jnp.mean(err * err, axis=-1, keepdims=True), axis=0, keepdims=True)

    dy, loss = pl.pallas_call(
        body, name="loss", grid=(s // ts,),
        in_specs=[_rows(ts, D_MODEL), _rows(ts, D_MODEL)],
        out_specs=[_rows(ts, D_MODEL), _full((8, 128))],
        out_shape=[jax.ShapeDtypeStruct((s, D_MODEL), F32), jax.ShapeDtypeStruct((8, 128), F32)],
        compiler_params=_params(("arbitrary",)),
    )(y, target)
    return loss[0, 0], dy


_W_IN_SPLITS = [256, 384, 416, 672, 928, 1184, 1440, 1696, 1952]


def _rope_swap(w):
    half = QK_ROPE // 2
    return jnp.concatenate([-w[..., half:], w[..., :half]], axis=-1)


def _rope_unswap(d):
    half = QK_ROPE // 2
    return jnp.concatenate([d[..., half:], -d[..., :half]], axis=-1)


def _zeros_like_cols(w, n):
    return jnp.zeros(w.shape[:-1] + (n,), w.dtype)


def _derive_weights(w):
    md = MXU_DTYPE
    nl = w['w_in'].shape[0]
    c_q, c_kv, k_r, sg_u, sg_v, cv_x, cv_b, cv_c, pool, gate = jnp.split(w['w_in'].astype(md), _W_IN_SPLITS, axis=-1)
    pad_rope = lambda r: jnp.concatenate([_zeros_like_cols(r, QK_NOPE), r, _zeros_like_cols(r, HEAD_PAD - QK_NOPE - QK_ROPE)], -1)
    w_in_p = jnp.concatenate([gate, sg_u, sg_v, cv_b, cv_x, cv_c, pool, c_q, c_kv, pad_rope(k_r), pad_rope(_rope_swap(k_r))], -1)
    wq = w['w_uq'].astype(md).reshape(nl, Q_RANK, N_HEADS, QK_NOPE + QK_ROPE)
    nope, rope_w = wq[..., :QK_NOPE], wq[..., QK_NOPE:]
    wq_a = jnp.concatenate([nope, rope_w, _zeros_like_cols(nope, 32)], -1).reshape(nl, Q_RANK, N_HEADS * HEAD_PAD)
    wq_b = pad_rope(_rope_swap(rope_w)).reshape(nl, Q_RANK, N_HEADS * HEAD_PAD)
    wkv = w['w_ukv'].astype(md).reshape(nl, KV_RANK, N_HEADS, QK_NOPE + V_HEAD)
    pad_half = lambda r: jnp.concatenate([r, _zeros_like_cols(r, HEAD_PAD - r.shape[-1])], -1).reshape(nl, KV_RANK, N_HEADS * HEAD_PAD)
    w_br_mla = w['w_br_mla'].astype(md).reshape(nl, N_HEADS, V_HEAD, D_MODEL)
    w_br_mla_p = jnp.concatenate([w_br_mla, jnp.zeros_like(w_br_mla)], axis=2).reshape(nl, N_HEADS * HEAD_PAD, D_MODEL)
    eye = jnp.eye(4, dtype=md)
    wbd = (w['pool_w'].astype(md)[:, :, :, None, :] * eye[None, :, None, :, None]).reshape(nl, BR_WIDTH, BR_WIDTH)
    row = lambda a: a.astype(F32)[:, None, :]
    w_in_pt = jnp.swapaxes(w_in_p, 1, 2)
    return dict(
        w_in_p=w_in_p, wt_g=w_in_pt[:, COL_G:COL_M1], wt_m1=w_in_pt[:, COL_M1:COL_M2], wt_m2=w_in_pt[:, COL_M2:COL_B],
        wt_b=w_in_pt[:, COL_B:],
        wq=jnp.concatenate([wq_a, wq_b], -1), wkv=jnp.concatenate([pad_half(wkv[..., :QK_NOPE]), pad_half(wkv[..., QK_NOPE:])], -1),
        w_br_mla_p=w_br_mla_p, w_br_sg=w['w_br_sg'].astype(md), w_br_conv=w['w_br_conv'].astype(md),
        w_br_pool=w['w_br_pool'].astype(md), wbd=wbd, w_out=w['w_out'].astype(md),
        w_ff1=w['w_ff1'].astype(md), w_ff1t=jnp.swapaxes(w['w_ff1'].astype(md), 1, 2),
        w_ff2=w['w_ff2'].astype(md), w_ff2t=jnp.swapaxes(w['w_ff2'].astype(md), 1, 2),
        norm_mix_pre=row(w['norm_mix_pre']), gate_b=row(w['gate_b']), q_norm=row(w['q_norm']), kv_norm=row(w['kv_norm']),
        sg_ln_g=row(w['sg_ln_g']), sg_ln_b=row(w['sg_ln_b']), sg_w=w['sg_w'].astype(F32),
        sg_bias=jnp.repeat(jnp.swapaxes(w['sg_b'].astype(F32), 1, 2), BR_WIDTH // SG_GROUPS, axis=2),
        conv_w8=jnp.pad(w['conv_w'].astype(F32), ((0, 0), (0, 5), (0, 0))), pool_scale=row(w['pool_scale']),
        norm_mix_post=row(w['norm_mix_post']), norm_ffn_pre=row(w['norm_ffn_pre']), norm_ffn_post=row(w['norm_ffn_post']),
    )


def _rope_tables(positions):
    inv_freq = ROPE_BASE ** (-jnp.arange(0, QK_ROPE, 2, dtype=F32) / QK_ROPE)
    ang = positions.astype(F32)[:, None] * inv_freq
    cos, sin = jnp.cos(ang), jnp.sin(ang)
    n = positions.shape[0]
    ones, z64, z32 = jnp.ones((n, QK_NOPE), F32), jnp.zeros((n, QK_NOPE), F32), jnp.zeros((n, 32), F32)
    return (jnp.concatenate([ones, cos, cos, z32], 1), jnp.concatenate([z64, sin, sin, z32], 1),
            jnp.concatenate([z64, cos, cos, z32], 1))


def _reference_layout_grads(g):
    gate, dm1, dm2, dpb = g['dw_in_pieces']
    nl = gate.shape[0]
    sg_u, sg_v, cv_b = jnp.split(dm1, 3, axis=-1)
    cv_x, cv_c, pool = jnp.split(dm2, 3, axis=-1)
    c_q, c_kv, kr, krs = jnp.split(dpb, [Q_RANK, Q_RANK + KV_RANK, Q_RANK + KV_RANK + HEAD_PAD], axis=-1)
    rope_cols = slice(QK_NOPE, QK_NOPE + QK_ROPE)
    k_r = kr[..., rope_cols] + _rope_unswap(krs[..., rope_cols])
    w_in = jnp.concatenate([c_q, c_kv, k_r, sg_u, sg_v, cv_x, cv_b, cv_c, pool, gate], -1)
    hw = N_HEADS * HEAD_PAD
    dqa = g['dwq'][..., :hw].reshape(nl, Q_RANK, N_HEADS, HEAD_PAD)
    dqb = g['dwq'][..., hw:].reshape(nl, Q_RANK, N_HEADS, HEAD_PAD)
    w_uq = jnp.concatenate([dqa[..., :QK_NOPE], dqa[..., rope_cols] + _rope_unswap(dqb[..., rope_cols])], -1)
    dka = g['dwkv'][..., :hw].reshape(nl, KV_RANK, N_HEADS, HEAD_PAD)
    dva = g['dwkv'][..., hw:].reshape(nl, KV_RANK, N_HEADS, HEAD_PAD)
    w_ukv = jnp.concatenate([dka[..., :QK_NOPE], dva[..., :V_HEAD]], -1)
    w_br_mla = g['dw_br_mla_p'].reshape(nl, N_HEADS, HEAD_PAD, D_MODEL)[:, :, :V_HEAD]
    dwbd = g['dwbd'].reshape(nl, 4, 64, 4, 64)
    pool_w = jnp.stack([dwbd[:, k, :, k, :] for k in range(4)], axis=1)
    sq = lambda a: a[:, 0, :]
    return dict(
        norm_mix_pre=sq(g['dg_pre']), w_in=w_in, gate_b=sq(g['dgate_b']), q_norm=sq(g['dq_norm']),
        w_uq=w_uq.reshape(nl, Q_RANK, -1), kv_norm=sq(g['dkv_norm']), w_ukv=w_ukv.reshape(nl, KV_RANK, -1),
        w_br_mla=w_br_mla.reshape(nl, N_HEADS * V_HEAD, D_MODEL), sg_ln_g=sq(g['dln_g']), sg_ln_b=sq(g['dln_b']),
        sg_w=g['dsg_w'], sg_b=jnp.swapaxes(g['dsg_b'][:, :, :SG_GROUPS], 1, 2), w_br_sg=g['dw_br_sg'],
        conv_w=g['dconv_w'][:, :3], w_br_conv=g['dw_br_conv'], pool_w=pool_w, pool_scale=sq(g['dpool_scale']),
        w_br_pool=g['dw_br_pool'], w_out=g['dw_out'], norm_mix_post=sq(g['dg_post']), norm_ffn_pre=sq(g['dg_fpre']),
        w_ff1=g['dw_ff1'], w_ff2=g['dw_ff2'], norm_ffn_post=sq(g['dg_fpost']))


def _layer_forward(x0, lw, tabs):
    proj = _mm("in_proj", x0, lw['w_in_p'], tm=512, tn=896, prologue=_rms, rows=(lw['norm_mix_pre'],))
    q, k, v = _qkv_prep(proj, lw['q_norm'], lw['kv_norm'], lw['wq'], lw['wkv'], *tabs)
    o, lse = _attn_fwd(q, k, v)
    x1 = _mix_fwd(x0, proj, o, lw)
    a = _mm("ffn1", x1, lw['w_ff1'], tm=512, tn=1024, prologue=_rms, rows=(lw['norm_ffn_pre'],))
    x2, f = _ffn2(a, lw['w_ff2'], x1, lw['norm_ffn_post'])
    return x2, dict(x0=x0, proj=proj, q=q, k=k, v=v, o=o, lse=lse, x1=x1, a=a, f=f)


def _layer_backward(dx2, lw, sv, tabs):
    g = {}
    df, da, g['dg_fpost'] = _ffn2_bwd(dx2, sv['f'], lw['norm_ffn_post'], sv['a'], lw['w_ff2t'])
    g['dw_ff2'] = _mm_tn("dw_ff2", sv['a'], df, tm=256, tn=512, prologue=_relu_sq)
    dx1, g['dg_fpre'] = _norm_in_bwd("ffn1_bwd", [(da, lw['w_ff1t'])], sv['x1'], lw['norm_ffn_pre'], dx2)
    g['dw_ff1'] = _mm_tn("dw_ff1", sv['x1'], da, tm=512, tn=1024, prologue=_rms, rows=(lw['norm_ffn_pre'],))
    (dgate, dm1, dyv, upool, do, delta, g['dgate_b'], g['dln_g'], g['dln_b'], g['dsg_w'], g['dsg_b'], g['dconv_w'],
     g['dwbd'], g['dpool_scale'], g['dw_br_mla_p'], g['dw_br_sg'], g['dw_br_conv'], g['dw_br_pool'], g['dw_out'],
     g['dg_post']) = _mix_bwd(dx1, sv['proj'], sv['o'], lw)
    dm2 = _shift_bwd(dyv, upool, sv['proj'], lw['conv_w8'])
    dq, dk, dv = _attn_bwd(sv['q'], sv['k'], sv['v'], do, sv['lse'], delta)
    dpb, g['dwq'], g['dwkv'], g['dq_norm'], g['dkv_norm'] = _qkv_bwd(
        dq, dk, dv, sv['proj'], lw['q_norm'], lw['kv_norm'], lw['wq'], lw['wkv'], *tabs)
    pieces = [(dgate, lw['wt_g']), (dm1, lw['wt_m1']), (dm2, lw['wt_m2']), (dpb, lw['wt_b'])]
    dx0, g['dg_pre'] = _norm_in_bwd("in_proj_bwd", pieces, sv['x0'], lw['norm_mix_pre'], dx1)
    g['dw_in_pieces'] = [_mm_tn("dw_in_%d" % n, sv['x0'], d, tm=512, tn=1024, prologue=_rms, rows=(lw['norm_mix_pre'],))
                         for n, (d, _) in enumerate(pieces)]
    return dx0, g


def _local_step(x, positions, target, layers, after_layer_backward):
    tabs = _rope_tables(positions)
    derived = [{n: a[0] for n, a in _derive_weights(w).items()} for w in layers]
    saved = []
    for lw in derived:
        x, sv = _layer_forward(x, lw, tabs)
        saved.append(sv)
    loss, dx = _loss_and_grad(x, target)
    for l in reversed(range(len(layers))):
        dx, g = _layer_backward(dx, derived[l], saved[l], tabs)
        lead = lambda a: [b[None] for b in a] if isinstance(a, list) else a[None]
        after_layer_backward(l, _reference_layout_grads({n: lead(a) for n, a in g.items()}))
    return loss, dx


HBM = pl.BlockSpec(memory_space=pl.ANY)


def _relative_peers():
    x, y = lax.axis_index("x"), lax.axis_index("y")
    return {1: (x, 1 - y), 2: (1 - x, y), 3: (1 - x, 1 - y)}


def _gather_chips(name, split, whole):
    ns, nw = len(split), len(whole)
    n = ns + nw

    def body(*refs):
        ins, outs = refs[:n], refs[n:2 * n]
        ici_send, ici_recv, d2d_send, d2d_recv, own_send, own_recv = refs[2 * n:]
        x, y, c = lax.axis_index("x"), lax.axis_index("y"), lax.axis_index("c")
        peers = _relative_peers()

        def run(half):
            def rows(ref, which):
                h = ref.shape[-2] // 2
                return ref.at[(slice(None),) * (len(ref.shape) - 2) + (slice(which * h, (which + 1) * h), slice(None))]

            started = []
            for k in range(n):
                own = pltpu.make_async_remote_copy(src_ref=ins[k], dst_ref=outs[k].at[0], send_sem=own_send.at[k],
                                                   recv_sem=own_recv.at[k], device_id=(x, y, 1 - c), device_id_type=MESH)
                own.start()
                started.append(own.wait)
                for r, (px, py) in peers.items():
                    src = rows(ins[k], half) if k < ns else ins[k]
                    dst = rows(outs[k].at[r], half) if k < ns else outs[k].at[r]
                    cp = pltpu.make_async_remote_copy(src_ref=src, dst_ref=dst, send_sem=ici_send.at[3 * k + r - 1],
                                                      recv_sem=ici_recv.at[3 * k + r - 1], device_id=(px, py, c),
                                                      device_id_type=MESH)
                    cp.start()
                    started.append(cp.wait_send)
            for k in range(n):
                for r, (px, py) in peers.items():
                    landed = rows(outs[k].at[r], half) if k < ns else outs[k].at[r]
                    pltpu.make_async_remote_copy(src_ref=landed, dst_ref=landed, send_sem=ici_send.at[3 * k + r - 1],
                                                 recv_sem=ici_recv.at[3 * k + r - 1], device_id=(px, py, c),
                                                 device_id_type=MESH).wait_recv()
                    if k < ns:
                        fwd = pltpu.make_async_remote_copy(src_ref=landed, dst_ref=landed, send_sem=d2d_send.at[3 * k + r - 1],
                                                           recv_sem=d2d_recv.at[3 * k + r - 1], device_id=(x, y, 1 - c),
                                                           device_id_type=MESH)
                        fwd.start()
                        started.append(fwd.wait_send)
            for k in range(ns):
                for r in peers:
                    other = rows(outs[k].at[r], 1 - half)
                    pltpu.make_async_remote_copy(src_ref=other, dst_ref=other, send_sem=d2d_send.at[3 * k + r - 1],
                                                 recv_sem=d2d_recv.at[3 * k + r - 1], device_id=(x, y, 1 - c),
                                                 device_id_type=MESH).wait_recv()
            for wait in started:
                wait()

        for half in (0, 1):
            pl.when(c == half)(functools.partial(run, half))

    arrs = list(split) + list(whole)
    return pl.pallas_call(
        body, name=name,
        in_specs=[HBM] * n, out_specs=[HBM] * n,
        out_shape=[jax.ShapeDtypeStruct((4,) + a.shape, a.dtype) for a in arrs],
        scratch_shapes=[pltpu.SemaphoreType.DMA((3 * n,)), pltpu.SemaphoreType.DMA((3 * n,)), pltpu.SemaphoreType.DMA((3 * ns,)),
                        pltpu.SemaphoreType.DMA((3 * ns,)), pltpu.SemaphoreType.DMA((n,)), pltpu.SemaphoreType.DMA((n,))],
    )(*arrs)


def _absolute_chip_order(relative):
    me = 2 * lax.axis_index("x") + lax.axis_index("y")
    return jnp.stack([lax.dynamic_index_in_dim(relative, jnp.bitwise_xor(me, chip), 0, keepdims=False) for chip in range(4)])


REDUCE_STEPS = 8


def _sibling_halves(name, arrs):
    n = len(arrs)

    def body(*refs):
        ins, theirs = refs[:n], refs[n:2 * n]
        send_sems, recv_sems = refs[2 * n:]
        x, y, c = lax.axis_index("x"), lax.axis_index("y"), lax.axis_index("c")

        def exchange(my_half):
            copies = []
            for k in range(n):
                h = ins[k].shape[1] // 2
                cp = pltpu.make_async_remote_copy(src_ref=ins[k].at[:, (1 - my_half) * h:(2 - my_half) * h, :], dst_ref=theirs[k],
                                                  send_sem=send_sems.at[k], recv_sem=recv_sems.at[k],
                                                  device_id=(x, y, 1 - c), device_id_type=MESH)
                cp.start()
                copies.append(cp)
            for cp in copies:
                cp.wait()

        for half in (0, 1):
            pl.when(c == half)(functools.partial(exchange, half))

    return pl.pallas_call(
        body, name=name, in_specs=[HBM] * n, out_specs=[HBM] * n,
        out_shape=[jax.ShapeDtypeStruct((a.shape[0], a.shape[1] // 2, a.shape[2]), a.dtype) for a in arrs],
        scratch_shapes=[pltpu.SemaphoreType.DMA((n,)), pltpu.SemaphoreType.DMA((n,))],
    )(*arrs)


def _add_sibling(name, arrs, theirs):
    n, steps = len(arrs), REDUCE_STEPS

    def body(*refs):
        for mine_ref, theirs_ref, out_ref in zip(refs[:n], refs[n:2 * n], refs[2 * n:]):
            out_ref[...] = (mine_ref[...] + theirs_ref[...]).astype(out_ref.dtype)

    block = lambda t: (4, t.shape[1] // steps, t.shape[2])
    return pl.pallas_call(
        body, name=name, grid=(steps,),
        in_specs=[pl.BlockSpec(block(t), lambda i: (0, lax.axis_index("c") * steps + i, 0)) for t in theirs]
        + [pl.BlockSpec(block(t), lambda i: (0, i, 0)) for t in theirs],
        out_specs=[pl.BlockSpec(block(t), lambda i: (0, i, 0)) for t in theirs],
        out_shape=[jax.ShapeDtypeStruct(t.shape, WIRE_DTYPE) for t in theirs],
        compiler_params=_params(("parallel",)))(*arrs, *theirs)


def _scatter_chips(name, arrs):
    n = len(arrs)

    def body(*refs):
        ins, outs = refs[:n], refs[n:2 * n]
        send_sems, recv_sems = refs[2 * n:]
        c = lax.axis_index("c")
        peers = _relative_peers()
        sends = []
        for k in range(n):
            for r, (px, py) in peers.items():
                cp = pltpu.make_async_remote_copy(src_ref=ins[k].at[2 * px + py], dst_ref=outs[k].at[r - 1],
                                                  send_sem=send_sems.at[3 * k + r - 1], recv_sem=recv_sems.at[3 * k + r - 1],
                                                  device_id=(px, py, c), device_id_type=MESH)
                cp.start()
                sends.append(cp)
        for k in range(n):
            for r, (px, py) in peers.items():
                pltpu.make_async_remote_copy(src_ref=ins[k].at[0], dst_ref=outs[k].at[r - 1],
                                             send_sem=send_sems.at[3 * k + r - 1], recv_sem=recv_sems.at[3 * k + r - 1],
                                             device_id=(px, py, c), device_id_type=MESH).wait_recv()
        for cp in sends:
            cp.wait_send()

    return pl.pallas_call(
        body, name=name, in_specs=[HBM] * n, out_specs=[HBM] * n,
        out_shape=[jax.ShapeDtypeStruct((3,) + a.shape[1:], a.dtype) for a in arrs],
        scratch_shapes=[pltpu.SemaphoreType.DMA((3 * n,)), pltpu.SemaphoreType.DMA((3 * n,))],
    )(*arrs)


def _sum_chips(name, chip_sums, arrived):
    n, steps = len(chip_sums), REDUCE_STEPS

    def body(*refs):
        for own_ref, arrived_ref, out_ref in zip(refs[:n], refs[n:2 * n], refs[2 * n:]):
            acc = own_ref[...].astype(F32)
            for r in range(3):
                acc = acc + arrived_ref[r].astype(F32)
            out_ref[...] = acc

    rows = lambda s: s.shape[1] // steps
    chip = lambda: 2 * lax.axis_index("x") + lax.axis_index("y")
    return pl.pallas_call(
        body, name=name, grid=(steps,),
        in_specs=[pl.BlockSpec((None, rows(s), s.shape[2]), lambda i: (chip(), i, 0)) for s in chip_sums]
        + [pl.BlockSpec((3, rows(s), s.shape[2]), lambda i: (0, i, 0)) for s in chip_sums],
        out_specs=[pl.BlockSpec((rows(s), s.shape[2]), lambda i: (lax.axis_index("c") * steps + i, 0)) for s in chip_sums],
        out_shape=[jax.ShapeDtypeStruct((2 * s.shape[1], s.shape[2]), F32) for s in chip_sums],
        compiler_params=_params(("parallel",)))(*chip_sums, *arrived)


def _join_siblings(name, bufs):
    n = len(bufs)

    def body(*refs):
        outs = refs[n:2 * n]
        send_sems, recv_sems = refs[2 * n:]
        x, y, c = lax.axis_index("x"), lax.axis_index("y"), lax.axis_index("c")

        def exchange(my_half):
            copies = []
            for k in range(n):
                h = outs[k].shape[0] // 2
                mine = outs[k].at[my_half * h:(my_half + 1) * h, :]
                theirs = outs[k].at[(1 - my_half) * h:(2 - my_half) * h, :]
                cp = pltpu.make_async_remote_copy(src_ref=mine, dst_ref=mine, send_sem=send_sems.at[k],
                                                  recv_sem=recv_sems.at[k], device_id=(x, y, 1 - c), device_id_type=MESH)
                cp.start()
                arrival = pltpu.make_async_remote_copy(src_ref=theirs, dst_ref=theirs, send_sem=send_sems.at[k],
                                                       recv_sem=recv_sems.at[k], device_id=(x, y, 1 - c), device_id_type=MESH)
                copies.append((cp, arrival))
            for cp, arrival in copies:
                arrival.wait_recv()
                cp.wait_send()

        for half in (0, 1):
            pl.when(c == half)(functools.partial(exchange, half))

    return pl.pallas_call(
        body, name=name, in_specs=[HBM] * n, out_specs=[HBM] * n,
        out_shape=[jax.ShapeDtypeStruct(b.shape, b.dtype) for b in bufs], input_output_aliases={k: k for k in range(n)},
        scratch_shapes=[pltpu.SemaphoreType.DMA((n,)), pltpu.SemaphoreType.DMA((n,))],
    )(*bufs)


def _gather_all(name, a):
    def body(a_ref, out_ref, staging, send_sems, recv_sems, local_sem):
        x, y, c = lax.axis_index("x"), lax.axis_index("y"), lax.axis_index("c")
        me = 4 * x + 2 * y + c
        flips = [(fx, fy, fc) for fx in (0, 1) for fy in (0, 1) for fc in (0, 1)][1:]
        peers = [(x ^ fx, y ^ fy, c ^ fc) for fx, fy, fc in flips]
        load = pltpu.make_async_copy(a_ref, staging, local_sem)
        load.start()
        load.wait()
        local = pltpu.make_async_copy(staging, out_ref.at[me], local_sem)
        local.start()
        sends = []
        for j, peer in enumerate(peers):
            cp = pltpu.make_async_remote_copy(src_ref=a_ref, dst_ref=out_ref.at[me], send_sem=send_sems.at[j],
                                              recv_sem=recv_sems.at[j], device_id=peer, device_id_type=MESH)
            cp.start()
            sends.append(cp)
        for j, (px, py, pc) in enumerate(peers):
            pltpu.make_async_remote_copy(src_ref=a_ref, dst_ref=out_ref.at[4 * px + 2 * py + pc], send_sem=send_sems.at[j],
                                         recv_sem=recv_sems.at[j], device_id=(px, py, pc), device_id_type=MESH).wait_recv()
        for cp in sends:
            cp.wait_send()
        local.wait()

    return pl.pallas_call(
        body, name=name, in_specs=[HBM], out_specs=HBM, out_shape=jax.ShapeDtypeStruct((8,) + a.shape, a.dtype),
        scratch_shapes=[pltpu.VMEM(a.shape, a.dtype), pltpu.SemaphoreType.DMA((7,)), pltpu.SemaphoreType.DMA((7,)),
                        pltpu.SemaphoreType.DMA],
    )(a)


def _rowwise_call(name, fn, slots, out_shapes, steps):
    n_in, n_out = [len(s) for s in slots], [len(o) for o in out_shapes]

    def spec(shape):
        if len(shape) == 3:
            return pl.BlockSpec((shape[0], shape[1] // steps, shape[2]), lambda i: (0, i, 0))
        return pl.BlockSpec((shape[0] // steps, shape[1]), lambda i: (i, 0))

    def body(*refs):
        ins, outs = refs[:sum(n_in)], refs[sum(n_in):]
        a = b = 0
        for k in range(len(slots)):
            for o_ref, val in zip(outs[b:b + n_out[k]], fn(*[r[...] for r in ins[a:a + n_in[k]]])):
                o_ref[...] = val
            a, b = a + n_in[k], b + n_out[k]

    flat_in = [arr for s in slots for arr in s]
    flat_out = [shp for o in out_shapes for shp in o]
    out = pl.pallas_call(body, name=name, grid=(steps,), in_specs=[spec(a.shape) for a in flat_in],
                         out_specs=[spec(s) for s in flat_out], out_shape=[jax.ShapeDtypeStruct(s, F32) for s in flat_out],
                         compiler_params=_params(("parallel",)))(*flat_in)
    grouped, b = [], 0
    for k in range(len(slots)):
        grouped.append(out[b:b + n_out[k]])
        b += n_out[k]
    return grouped


def _sum_in_order(a):
    acc = a[0].astype(F32)
    for k in range(1, a.shape[0]):
        acc = acc + a[k].astype(F32)
    return (acc,)


def _adamw_math(w, g, m, v):
    m_new = ADAM_B1 * m + (1.0 - ADAM_B1) * g
    v_new = ADAM_B2 * v + (1.0 - ADAM_B2) * (g * g)
    m_hat = m_new / (1.0 - ADAM_B1 ** ADAM_STEP)
    v_hat = v_new / (1.0 - ADAM_B2 ** ADAM_STEP)
    return -ADAM_LR * (m_hat / (jnp.sqrt(v_hat) + ADAM_EPS) + ADAM_WD * w), m_new, v_new


def _pack(arrs, rows_per_layer, dtype):
    nl = arrs[0].shape[0]
    flat = jnp.concatenate([a.astype(dtype).reshape(nl, -1) for a in arrs], axis=1)
    flat = jnp.pad(flat, ((0, 0), (0, rows_per_layer * PACK_COLS - flat.shape[1])))
    return flat.reshape(nl * rows_per_layer, PACK_COLS)


def _unpack(packed, shapes, rows_per_layer):
    nl = shapes[0][0]
    flat = packed.reshape(packed.shape[:-2] + (nl, rows_per_layer * PACK_COLS))
    out, off = [], 0
    for shp in shapes:
        size = math.prod(shp[1:])
        out.append(flat[..., off:off + size].reshape(packed.shape[:-2] + tuple(shp)))
        off += size
    return out


def _rows_needed(shapes, multiple):
    per_layer = sum(math.prod(s[1:]) for s in shapes)
    rows = -(-per_layer // PACK_COLS)
    return -(-rows // multiple) * multiple


def _full_weights(w):
    nl = w['w_in'].shape[0]
    conv_flat = w['conv_w'].reshape(-1, 128)
    conv_rows = conv_flat.shape[0]
    conv_flat = jnp.pad(conv_flat, ((0, -conv_rows % 8), (0, 0)))
    gathered = _gather_chips("gather_weights", [w[n].astype(MXU_DTYPE) for n in MATMUL_SHARDED], [conv_flat])
    gathered = [_absolute_chip_order(g) for g in gathered]
    full = dict(w)
    for n, part in zip(MATMUL_SHARDED, gathered):
        if n in ROW_SHARDED:
            full[n] = jnp.swapaxes(part, 0, 1).reshape(nl, 4 * part.shape[2], part.shape[3])
        else:
            full[n] = jnp.transpose(part, (1, 2, 0, 3)).reshape(nl, part.shape[2], 4 * part.shape[3])
    conv = gathered[-1][:, :conv_rows].reshape((4,) + w['conv_w'].shape)
    full['conv_w'] = jnp.transpose(conv, (1, 2, 0, 3)).reshape(nl, 3, -1)
    return full


def _chip_major(n, g):
    nl = g.shape[0]
    if n in ROW_SHARDED:
        return jnp.swapaxes(g.reshape(nl, 4, g.shape[1] // 4, g.shape[2]), 0, 1)
    return jnp.transpose(g.reshape(nl, g.shape[1], 4, g.shape[2] // 4), (2, 0, 1, 3))


def kernel(x, positions, norm_mix_pre, w_in, gate_b, q_norm, w_uq, kv_norm, w_ukv, w_br_mla, sg_ln_g, sg_ln_b, sg_w, sg_b, w_br_sg, conv_w, w_br_conv, pool_w, pool_scale, w_br_pool, w_out, norm_mix_post, norm_ffn_pre, w_ff1, w_ff2, norm_ffn_post, loss_target, m_norm_mix_pre, m_w_in, m_gate_b, m_q_norm, m_w_uq, m_kv_norm, m_w_ukv, m_w_br_mla, m_sg_ln_g, m_sg_ln_b, m_sg_w, m_sg_b, m_w_br_sg, m_conv_w, m_w_br_conv, m_pool_w, m_pool_scale, m_w_br_pool, m_w_out, m_norm_mix_post, m_norm_ffn_pre, m_w_ff1, m_w_ff2, m_norm_ffn_post, v_norm_mix_pre, v_w_in, v_gate_b, v_q_norm, v_w_uq, v_kv_norm, v_w_ukv, v_w_br_mla, v_sg_ln_g, v_sg_ln_b, v_sg_w, v_sg_b, v_w_br_sg, v_conv_w, v_w_br_conv, v_pool_w, v_pool_scale, v_w_br_pool, v_w_out, v_norm_mix_post, v_norm_ffn_pre, v_w_ff1, v_w_ff2, v_norm_ffn_post):
    given = dict(locals())
    w = {n: given[n] for n in WEIGHTS}
    mom = {n: given['m_' + n] for n in WEIGHTS}
    var = {n: given['v_' + n] for n in WEIGHTS}
    nl = w['w_in'].shape[0]
    full = _full_weights(w)
    layers = [{n: full[n][l:l + 1] for n in WEIGHTS} for l in range(nl)]
    small_shapes = [(1,) + w[n].shape[1:] for n in SMALL_SHARDED]
    small_rows = _rows_needed(small_shapes, 256)
    reduced, local_small = {}, {}

    def reduce_layer(l, g):
        arrs = [_chip_major(n, g[n])[:, 0] for n in BIG_SHARDED]
        arrs.append(jnp.stack([_pack([_chip_major(n, g[n])[chip] for n in SMALL_SHARDED], small_rows, F32) for chip in range(4)]))
        chip_sums = _add_sibling("add_sibling", arrs, _sibling_halves("grads_to_sibling", arrs))
        halves = _sum_chips("sum_chips", chip_sums, _scatter_chips("grads_to_chips", chip_sums))
        reduced[l] = _join_siblings("join_halves", halves)
        local_small[l] = [g[n] for n in REPLICATED + ['conv_w']]

    loss, dx = _local_step(x[0], positions[0], loss_target[0], layers, reduce_layer)
    loss = lax.psum(loss, ("x", "y", "c"))

    grad, delta, new_m, new_v = {}, {}, {}, {}
    slots = []
    for k, n in enumerate(BIG_SHARDED):
        grad[n] = jnp.stack([reduced[l][k] for l in range(nl)])
        slots.append([a.reshape(-1, a.shape[-1]) for a in (w[n], grad[n], mom[n], var[n])])
    small_pack = lambda d: _pack([d[n] for n in SMALL_SHARDED], small_rows, F32)
    g_small = jnp.stack([reduced[l][-1] for l in range(nl)])
    slots.append([small_pack(w), g_small.reshape(-1, PACK_COLS), small_pack(mom), small_pack(var)])
    updated = _rowwise_call("adamw_sharded", _adamw_math, slots, [[s_[0].shape] * 3 for s_ in slots], 32)
    for k, n in enumerate(BIG_SHARDED):
        delta[n], new_m[n], new_v[n] = [a.reshape(w[n].shape) for a in updated[k]]
    sm_shapes = [w[n].shape for n in SMALL_SHARDED]
    for d, packed in zip((grad, delta, new_m, new_v), [g_small.reshape(-1, PACK_COLS)] + list(updated[-1])):
        d.update(zip(SMALL_SHARDED, _unpack(packed, sm_shapes, small_rows)))

    names = REPLICATED + ['conv_w']
    local = [jnp.concatenate([local_small[l][k] for l in range(nl)]) for k in range(len(names))]
    rows = _rows_needed([a.shape for a in local], 32)
    everyone = _gather_all("gather_small_grads", _pack(local, rows, F32))
    (summed,), = _rowwise_call("sum_devices", _sum_in_order, [[everyone]], [[everyone.shape[1:]]], 4)
    g_rep = _unpack(summed, [a.shape for a in local], rows)
    chip = 2 * lax.axis_index("x") + lax.axis_index("y")
    g_rep[-1] = lax.dynamic_slice_in_dim(g_rep[-1], chip * w['conv_w'].shape[2], w['conv_w'].shape[2], axis=2)
    rep_pack = lambda arrs: _pack(arrs, rows, F32)
    (rep_out,) = _rowwise_call("adamw_replicated", _adamw_math,
                               [[rep_pack([w[n] for n in names]), rep_pack(g_rep), rep_pack([mom[n] for n in names]),
                                 rep_pack([var[n] for n in names])]], [[(nl * rows, PACK_COLS)] * 3], 4)
    grad.update(zip(names, g_rep))
    for d, packed in zip((delta, new_m, new_v), rep_out):
        d.update(zip(names, _unpack(packed, [w[n].shape for n in names], rows)))

    return (loss, dx[None], *[grad[n] for n in WEIGHTS], *[delta[n] for n in WEIGHTS], *[new_m[n] for n in WEIGHTS],
            *[new_v[n] for n in WEIGHTS])
```

```python
import functools
import math

import jax
import jax.numpy as jnp
from jax import lax
from jax.experimental import pallas as pl
from jax.experimental.pallas import tpu as pltpu

F32 = jnp.float32
MXU_DTYPE = jnp.bfloat16
WIRE_DTYPE = jnp.bfloat16
ACT_DTYPE = jnp.bfloat16
MESH = pl.DeviceIdType.MESH

D_MODEL = 1024
D_FF = 4096
N_HEADS = 4
QK_NOPE = 64
QK_ROPE = 32
V_HEAD = 64
HEAD_PAD = 128
Q_RANK = 256
KV_RANK = 128
SG_CHUNK = 128
SG_GROUPS = 4
BR_WIDTH = 256
N_BRANCH = 4
POOL_WINDOWS = (2, 4, 8, 16)
HALO = 16
ROPE_BASE = 10000.0
EPS = 1e-6
ATTN_SCALE = (QK_NOPE + QK_ROPE) ** -0.5
N_PROJ = N_BRANCH * D_MODEL + 6 * BR_WIDTH + Q_RANK + KV_RANK + 2 * HEAD_PAD
COL_G, COL_M1, COL_M2, COL_B = 0, 4096, 4864, 5632

ADAM_LR, ADAM_B1, ADAM_B2, ADAM_EPS, ADAM_WD, ADAM_STEP = 0.001, 0.9, 0.999, 1e-08, 0.01, 10

VMEM_LIMIT = 56 * 1024 * 1024

WEIGHTS = ['norm_mix_pre', 'w_in', 'gate_b', 'q_norm', 'w_uq', 'kv_norm', 'w_ukv', 'w_br_mla', 'sg_ln_g', 'sg_ln_b',
           'sg_w', 'sg_b', 'w_br_sg', 'conv_w', 'w_br_conv', 'pool_w', 'pool_scale', 'w_br_pool', 'w_out',
           'norm_mix_post', 'norm_ffn_pre', 'w_ff1', 'w_ff2', 'norm_ffn_post']
COL_SHARDED = ['w_in', 'w_uq', 'w_ukv', 'w_br_mla', 'w_br_sg', 'w_br_conv', 'w_br_pool', 'w_ff1']
ROW_SHARDED = ['w_out', 'w_ff2']
MATMUL_SHARDED = ['w_in', 'w_uq', 'w_ukv', 'w_br_mla', 'w_br_sg', 'w_br_conv', 'w_br_pool', 'w_out', 'w_ff1', 'w_ff2']
BIG_SHARDED = ['w_in', 'w_ff1', 'w_ff2', 'w_out']
SMALL_SHARDED = ['w_uq', 'w_ukv', 'w_br_mla', 'w_br_sg', 'w_br_conv', 'w_br_pool']
SHARDED = MATMUL_SHARDED + ['conv_w']
REPLICATED = [n for n in WEIGHTS if n not in SHARDED]
PACK_COLS = 1024


def _params(sem, vmem=VMEM_LIMIT):
    return pltpu.CompilerParams(dimension_semantics=sem, vmem_limit_bytes=vmem)


def _mxu(a):
    return a.astype(MXU_DTYPE)


def _dot(a, b):
    return jnp.dot(_mxu(a), _mxu(b), preferred_element_type=F32)


def _dot_nt(a, b):
    return lax.dot_general(_mxu(a), _mxu(b), (((1,), (1,)), ((), ())), preferred_element_type=F32)


def _dot_tn(a, b):
    return lax.dot_general(_mxu(a), _mxu(b), (((0,), (0,)), ((), ())), preferred_element_type=F32)


def _rms(x, g):
    r = lax.rsqrt(jnp.mean(x * x, axis=-1, keepdims=True) + EPS)
    return x * r * g


def _rms_bwd(x, g, dy):
    r = lax.rsqrt(jnp.mean(x * x, axis=-1, keepdims=True) + EPS)
    xh = x * r
    gdy = dy * g
    dx = r * (gdy - xh * jnp.mean(gdy * xh, axis=-1, keepdims=True))
    return dx, jnp.sum(dy * xh, axis=0, keepdims=True)


_GELU_C = math.sqrt(2.0 / math.pi)


def _gelu(x):
    t = jnp.tanh(_GELU_C * (x + 0.044715 * (x * x * x)))
    return x * (0.5 * (1.0 + t)), t


def _gelu_grad(x, t):
    return 0.5 * (1.0 + t) + 0.5 * x * (1.0 - t * t) * (_GELU_C * (1.0 + 3.0 * 0.044715 * x * x))


def _sigmoid(x):
    return 1.0 / (1.0 + jnp.exp(-x))


def _full(shape):
    return pl.BlockSpec(shape, lambda *_: (0,) * len(shape))


def _rows(ts, width, col=0):
    return pl.BlockSpec((ts, width), lambda i: (i, col))


def _tile(n, pref):
    return min(n, pref)


def _mm(name, a, w, *, tm, tn, prologue=None, rows=()):
    m, k = a.shape
    n = w.shape[1]
    tm, tn = _tile(m, tm), _tile(n, tn)

    def body(a_ref, *rest):
        row_refs, w_ref, o_ref = rest[:len(rows)], rest[len(rows)], rest[len(rows) + 1]
        av = a_ref[...]
        if prologue is not None:
            av = prologue(av, *[r[...] for r in row_refs])
        o_ref[...] = _dot(av, w_ref[...]).astype(o_ref.dtype)

    return pl.pallas_call(
        body, name=name, grid=(m // tm, n // tn),
        in_specs=[pl.BlockSpec((tm, k), lambda i, j: (i, 0))] + [pl.BlockSpec((1, k), lambda i, j: (0, 0)) for _ in rows]
        + [pl.BlockSpec((k, tn), lambda i, j: (0, j))],
        out_specs=pl.BlockSpec((tm, tn), lambda i, j: (i, j)),
        out_shape=jax.ShapeDtypeStruct((m, n), ACT_DTYPE),
        compiler_params=_params(("parallel", "parallel")),
    )(a, *rows, w)


def _mm_tn(name, a, b, *, tm, tn, prologue=None, rows=()):
    m, k = a.shape
    n = b.shape[1]
    tm, tn = _tile(m, tm), _tile(n, tn)

    def body(a_ref, *rest):
        row_refs, b_ref, o_ref = rest[:len(rows)], rest[len(rows)], rest[len(rows) + 1]

        @pl.when(pl.program_id(1) == 0)
        def _():
            o_ref[...] = jnp.zeros_like(o_ref)

        av = a_ref[...]
        if prologue is not None:
            av = prologue(av, *[r[...] for r in row_refs])
        o_ref[...] += _dot_tn(av, b_ref[...])

    return pl.pallas_call(
        body, name=name, grid=(n // tn, m // tm),
        in_specs=[pl.BlockSpec((tm, k), lambda j, i: (i, 0))] + [pl.BlockSpec((1, k), lambda j, i: (0, 0)) for _ in rows]
        + [pl.BlockSpec((tm, tn), lambda j, i: (i, j))],
        out_specs=pl.BlockSpec((k, tn), lambda j, i: (0, j)),
        out_shape=jax.ShapeDtypeStruct((k, n), F32),
        compiler_params=_params(("parallel", "arbitrary")),
    )(a, *rows, b)


def _relu_sq(a):
    r = jnp.maximum(a.astype(F32), 0.0)
    return r * r


def _qkv_prep(proj, q_norm, kv_norm, wq, wkv, cq_tab, s_tab, cr_tab):
    s = proj.shape[0]
    ts = _tile(s, 512)
    hw = N_HEADS * HEAD_PAD

    def body(cq_ref, ckv_ref, kr_ref, krs_ref, gq_ref, gkv_ref, wq_ref, wkv_ref, ct_ref, st_ref, crt_ref,
             q_ref, k_ref, v_ref):
        ct, st, crt = ct_ref[...], st_ref[...], crt_ref[...]
        qn = _rms(cq_ref[...].astype(F32), gq_ref[...])
        qab = _dot(qn, wq_ref[...])
        kvn = _rms(ckv_ref[...].astype(F32), gkv_ref[...])
        kav = _dot(kvn, wkv_ref[...])
        k_rope = kr_ref[...].astype(F32) * crt + krs_ref[...].astype(F32) * st
        for h in range(N_HEADS):
            lo = h * HEAD_PAD
            q_ref[h] = (qab[:, lo:lo + HEAD_PAD] * ct + qab[:, hw + lo:hw + lo + HEAD_PAD] * st).astype(q_ref.dtype)
            k_ref[h] = (kav[:, lo:lo + HEAD_PAD] + k_rope).astype(k_ref.dtype)
            v_ref[h] = kav[:, hw + lo:hw + lo + HEAD_PAD].astype(v_ref.dtype)

    head_spec = pl.BlockSpec((N_HEADS, ts, HEAD_PAD), lambda i: (0, i, 0))
    head_shape = jax.ShapeDtypeStruct((N_HEADS, s, HEAD_PAD), MXU_DTYPE)
    return pl.pallas_call(
        body, name="qkv_prep", grid=(s // ts,),
        in_specs=[_rows(ts, Q_RANK, COL_B // Q_RANK), _rows(ts, KV_RANK, (COL_B + Q_RANK) // KV_RANK),
                  _rows(ts, HEAD_PAD, (COL_B + Q_RANK + KV_RANK) // HEAD_PAD),
                  _rows(ts, HEAD_PAD, (COL_B + Q_RANK + KV_RANK + HEAD_PAD) // HEAD_PAD),
                  _full((1, Q_RANK)), _full((1, KV_RANK)), _full((Q_RANK, 2 * hw)), _full((KV_RANK, 2 * hw)),
                  _rows(ts, HEAD_PAD), _rows(ts, HEAD_PAD), _rows(ts, HEAD_PAD)],
        out_specs=[head_spec, head_spec, head_spec],
        out_shape=[head_shape, head_shape, head_shape],
        compiler_params=_params(("parallel",)),
    )(proj, proj, proj, proj, q_norm, kv_norm, wq, wkv, cq_tab, s_tab, cr_tab)


def _diagonal_mask(t):
    return lax.broadcasted_iota(jnp.int32, (t, t), 1) <= lax.broadcasted_iota(jnp.int32, (t, t), 0)


def _attn_fwd(q, k, v):
    s = q.shape[1]
    t = _tile(s, 512)

    def body(q_ref, k_ref, v_ref, o_ref, lse_ref):
        i = pl.program_id(1)
        qv = q_ref[0]

        def step(j, carry, on_diagonal):
            m, l, acc = carry
            kj = k_ref[0, pl.ds(pl.multiple_of(j * t, t), t), :]
            vj = v_ref[0, pl.ds(pl.multiple_of(j * t, t), t), :]
            sc = _dot_nt(qv, kj) * ATTN_SCALE
            if on_diagonal:
                sc = jnp.where(_diagonal_mask(t), sc, -jnp.inf)
            m_new = jnp.maximum(m, jnp.max(sc, axis=1, keepdims=True))
            p = jnp.exp(sc - m_new)
            alpha = jnp.exp(m - m_new)
            return m_new, alpha * l + jnp.sum(p, axis=1, keepdims=True), alpha * acc + _dot(p, vj)

        init = (jnp.full((t, 1), -jnp.inf, F32), jnp.zeros((t, 1), F32), jnp.zeros((t, HEAD_PAD), F32))
        below = lax.fori_loop(0, i, functools.partial(step, on_diagonal=False), init)
        m, l, acc = step(i, below, True)
        o_ref[...] = acc / l
        lse_ref[0] = m + jnp.log(l)

    return pl.pallas_call(
        body, name="attn_fwd", grid=(N_HEADS, s // t),
        in_specs=[pl.BlockSpec((1, t, HEAD_PAD), lambda h, i: (h, i, 0)),
                  pl.BlockSpec((1, s, HEAD_PAD), lambda h, i: (h, 0, 0)),
                  pl.BlockSpec((1, s, HEAD_PAD), lambda h, i: (h, 0, 0))],
        out_specs=[pl.BlockSpec((t, HEAD_PAD), lambda h, i: (i, h)), pl.BlockSpec((1, t, 1), lambda h, i: (h, i, 0))],
        out_shape=[jax.ShapeDtypeStruct((s, N_HEADS * HEAD_PAD), F32), jax.ShapeDtypeStruct((N_HEADS, s, 1), F32)],
        compiler_params=_params(("parallel", "parallel")),
    )(q, k, v)


def _attn_bwd(q, k, v, do, lse, delta):
    s = q.shape[1]
    t = _tile(s, 512)
    nq = s // t

    def body(q_ref, do_ref, lse_ref, dl_ref, k_ref, v_ref, dq_ref, dk_ref, dv_ref):
        j = pl.program_id(1)

        @pl.when(j == 0)
        def _():
            dq_ref[...] = jnp.zeros_like(dq_ref)

        kj, vj = k_ref[0], v_ref[0]

        def step(i, carry, on_diagonal):
            dk, dv = carry
            rows = pl.ds(pl.multiple_of(i * t, t), t)
            qi, doi = q_ref[0, rows, :], do_ref[rows, :]
            sc = _dot_nt(qi, kj) * ATTN_SCALE
            if on_diagonal:
                sc = jnp.where(_diagonal_mask(t), sc, -jnp.inf)
            p = jnp.exp(sc - lse_ref[0, rows, :])
            dv = dv + _dot_tn(p, doi)
            dp = _dot_nt(doi, vj)
            ds = p * (dp - dl_ref[0, rows, :]) * ATTN_SCALE
            dk = dk + _dot_tn(ds, qi)
            dq_ref[0, rows, :] += _dot(ds, kj)
            return dk, dv

        zero = jnp.zeros((t, HEAD_PAD), F32)
        dk, dv = lax.fori_loop(j + 1, nq, functools.partial(step, on_diagonal=False), step(j, (zero, zero), True))
        dk_ref[0] = dk
        dv_ref[0] = dv

    whole = lambda w: pl.BlockSpec((1, s, w), lambda h, j: (h, 0, 0))
    tile = pl.BlockSpec((1, t, HEAD_PAD), lambda h, j: (h, j, 0))
    head_shape = jax.ShapeDtypeStruct((N_HEADS, s, HEAD_PAD), F32)
    return pl.pallas_call(
        body, name="attn_bwd", grid=(N_HEADS, nq),
        in_specs=[whole(HEAD_PAD), pl.BlockSpec((s, HEAD_PAD), lambda h, j: (0, h)), whole(1), whole(1), tile, tile],
        out_specs=[whole(HEAD_PAD), tile, tile],
        out_shape=[head_shape, head_shape, head_shape],
        compiler_params=_params(("parallel", "arbitrary")),
    )(q, do, lse, delta, k, v)


def _qkv_bwd(dq, dk, dv, proj, q_norm, kv_norm, wq, wkv, cq_tab, s_tab, cr_tab):
    s = proj.shape[0]
    ts = _tile(s, 256)
    hw = N_HEADS * HEAD_PAD

    def body(dq_ref, dk_ref, dv_ref, cq_ref, ckv_ref, gq_ref, gkv_ref, wq_ref, wkv_ref, ct_ref, st_ref, crt_ref,
             dpb_ref, dwq_ref, dwkv_ref, dgq_ref, dgkv_ref):
        @pl.when(pl.program_id(0) == 0)
        def _():
            for r in (dwq_ref, dwkv_ref, dgq_ref, dgkv_ref):
                r[...] = jnp.zeros_like(r)

        ct, st, crt = ct_ref[...], st_ref[...], crt_ref[...]
        dqs = [dq_ref[h] for h in range(N_HEADS)]
        dks = [dk_ref[h] for h in range(N_HEADS)]
        dqab = jnp.concatenate([d * ct for d in dqs] + [d * st for d in dqs], axis=1)
        dkav = jnp.concatenate(dks + [dv_ref[h] for h in range(N_HEADS)], axis=1)
        dk_sum = dks[0] + dks[1] + dks[2] + dks[3]
        cq, ckv, gq, gkv = cq_ref[...].astype(F32), ckv_ref[...].astype(F32), gq_ref[...], gkv_ref[...]
        dwq_ref[...] += _dot_tn(_rms(cq, gq), dqab)
        dwkv_ref[...] += _dot_tn(_rms(ckv, gkv), dkav)
        dcq, dgq = _rms_bwd(cq, gq, _dot_nt(dqab, wq_ref[...]))
        dckv, dgkv = _rms_bwd(ckv, gkv, _dot_nt(dkav, wkv_ref[...]))
        dgq_ref[...] += dgq
        dgkv_ref[...] += dgkv
        dpb_ref[...] = jnp.concatenate([dcq, dckv, dk_sum * crt, dk_sum * st], axis=1).astype(dpb_ref.dtype)

    head_spec = pl.BlockSpec((N_HEADS, ts, HEAD_PAD), lambda i: (0, i, 0))
    wb = Q_RANK + KV_RANK + 2 * HEAD_PAD
    return pl.pallas_call(
        body, name="qkv_bwd", grid=(s // ts,),
        in_specs=[head_spec, head_spec, head_spec,
                  _rows(ts, Q_RANK, COL_B // Q_RANK), _rows(ts, KV_RANK, (COL_B + Q_RANK) // KV_RANK),
                  _full((1, Q_RANK)), _full((1, KV_RANK)), _full((Q_RANK, 2 * hw)), _full((KV_RANK, 2 * hw)),
                  _rows(ts, HEAD_PAD), _rows(ts, HEAD_PAD), _rows(ts, HEAD_PAD)],
        out_specs=[_rows(ts, wb), _full((Q_RANK, 2 * hw)), _full((KV_RANK, 2 * hw)), _full((1, Q_RANK)), _full((1, KV_RANK))],
        out_shape=[jax.ShapeDtypeStruct((s, wb), MXU_DTYPE), jax.ShapeDtypeStruct((Q_RANK, 2 * hw), F32),
                   jax.ShapeDtypeStruct((KV_RANK, 2 * hw), F32), jax.ShapeDtypeStruct((1, Q_RANK), F32),
                   jax.ShapeDtypeStruct((1, KV_RANK), F32)],
        compiler_params=_params(("arbitrary",)),
    )(dq, dk, dv, proj, proj, q_norm, kv_norm, wq, wkv, cq_tab, s_tab, cr_tab)


def _lane_group(width):
    return lax.broadcasted_iota(jnp.int32, (1, width), 1) // (width // 4)


def _shift_down(a, k):
    return pltpu.roll(a, k, 0)


def _shift_up(a, k):
    return pltpu.roll(a, a.shape[0] - k, 0)


def _window_sums(xh, shift):
    s2 = xh + shift(xh, 1)
    s4 = s2 + shift(s2, 2)
    s8 = s4 + shift(s4, 4)
    s16 = s8 + shift(s8, 8)
    grp = _lane_group(xh.shape[1])
    return jnp.where(grp == 0, s2, jnp.where(grp == 1, s4, jnp.where(grp == 2, s8, s16)))


def _pool_count(i, ts):
    grp = _lane_group(BR_WIDTH)
    win = jnp.where(grp == 0, 2.0, jnp.where(grp == 1, 4.0, jnp.where(grp == 2, 8.0, 16.0)))
    t = (i * ts + lax.broadcasted_iota(jnp.int32, (ts, 1), 0)).astype(F32)
    return jnp.minimum(t + 1.0, win)


def _mix_forward(i, ts, r):
    f = {}
    act = lambda name: r[name][...].astype(F32)
    f['gates'] = _sigmoid(act('gate') + r['gate_b'][...])
    sgu, sgv = act('sgu'), act('sgv')
    f['sgu'], f['sgv'] = sgu, sgv
    u_act, f['tu'] = _gelu(sgu)
    vg, f['tv'] = _gelu(sgv)
    mu = jnp.mean(vg, axis=-1, keepdims=True)
    xc = vg - mu
    f['ln_r'] = lax.rsqrt(jnp.mean(xc * xc, axis=-1, keepdims=True) + EPS)
    f['ln_xh'] = xc * f['ln_r']
    vln = f['ln_xh'] * r['ln_g'][...] + r['ln_b'][...]
    tril = lax.broadcasted_iota(jnp.int32, (SG_CHUNK, SG_CHUNK), 1) <= lax.broadcasted_iota(jnp.int32, (SG_CHUNK, SG_CHUNK), 0)
    f['wm'] = [_mxu(jnp.where(tril, r['sg_w'][g], 0.0)) for g in range(SG_GROUPS)]
    f['tril'] = tril
    grp = _lane_group(BR_WIDTH)
    bias = r['sg_bias'][...]
    parts = []
    for ci in range(ts // SG_CHUNK):
        vc = vln[ci * SG_CHUNK:(ci + 1) * SG_CHUNK]
        sc = bias
        for g in range(SG_GROUPS):
            sc = sc + jnp.where(grp == g, _dot(f['wm'][g], vc), 0.0)
        parts.append(sc)
    f['vln'] = vln
    f['sg_s'] = parts[0] if len(parts) == 1 else jnp.concatenate(parts, axis=0)
    f['u_act'] = u_act
    out_b = u_act * f['sg_s']
    first = (i > 0).astype(F32)
    cvx, cvc, cvb = act('cvx'), act('cvc'), act('cvb')
    f['cvx'], f['cvc'], f['cvb'] = cvx, cvc, cvb
    zh = jnp.concatenate([act('hx') * act('hc') * first, cvc * cvx], axis=0)
    f['z1'] = _shift_down(zh, 1)[HALO:]
    f['z2'] = _shift_down(zh, 2)[HALO:]
    f['z0'] = zh[HALO:]
    f['yv'] = r['conv_w'][0:1, :] * f['z2'] + r['conv_w'][1:2, :] * f['z1'] + r['conv_w'][2:3, :] * f['z0']
    out_c = cvb * f['yv']
    p = act('pool')
    ph = jnp.concatenate([act('hp') * first, p], axis=0)
    f['cnt'] = _pool_count(i, ts)
    f['pooled'] = _window_sums(ph, _shift_down)[HALO:] / f['cnt'] - p
    f['mixed'] = _dot(f['pooled'], r['wbd'][...])
    out_d = f['mixed'] * r['pool_scale'][...]
    f['outs'] = [r['o'][...], out_b, out_c, out_d]
    f['ys'] = [_dot(f['outs'][b], r['w_br'][b][...]) for b in range(N_BRANCH)]
    merged = f['gates'][:, 0:D_MODEL] * f['ys'][0]
    for b in range(1, N_BRANCH):
        merged = merged + f['gates'][:, b * D_MODEL:(b + 1) * D_MODEL] * f['ys'][b]
    f['merged'] = merged
    f['mo'] = _dot(merged, r['w_out'][...])
    return f


_MIX_TILE_INPUTS = ['gate', 'sgu', 'sgv', 'cvb', 'cvx', 'cvc', 'pool', 'hx', 'hc', 'hp', 'o']
_MIX_WEIGHTS = ['gate_b', 'ln_g', 'ln_b', 'sg_w', 'sg_bias', 'conv_w', 'wbd', 'pool_scale', 'w_br0', 'w_br1', 'w_br2',
                'w_br3', 'w_out', 'g_post']


def _mix_specs(s, ts):
    c0 = COL_M1 // BR_WIDTH
    prev = lambda col: pl.BlockSpec((HALO, BR_WIDTH), lambda i: (jnp.maximum(i * (ts // HALO) - 1, 0), col))
    tiles = [_rows(ts, N_BRANCH * D_MODEL, 0), _rows(ts, BR_WIDTH, c0), _rows(ts, BR_WIDTH, c0 + 1), _rows(ts, BR_WIDTH, c0 + 2),
             _rows(ts, BR_WIDTH, c0 + 3), _rows(ts, BR_WIDTH, c0 + 4), _rows(ts, BR_WIDTH, c0 + 5),
             prev(c0 + 3), prev(c0 + 4), prev(c0 + 5), _rows(ts, N_HEADS * HEAD_PAD)]
    weights = [_full((1, N_BRANCH * D_MODEL)), _full((1, BR_WIDTH)), _full((1, BR_WIDTH)),
               _full((SG_GROUPS, SG_CHUNK, SG_CHUNK)), _full((SG_CHUNK, BR_WIDTH)), _full((8, BR_WIDTH)),
               _full((BR_WIDTH, BR_WIDTH)), _full((1, BR_WIDTH)), _full((N_HEADS * HEAD_PAD, D_MODEL)),
               _full((BR_WIDTH, D_MODEL)), _full((BR_WIDTH, D_MODEL)), _full((BR_WIDTH, D_MODEL)),
               _full((D_MODEL, D_MODEL)), _full((1, D_MODEL))]
    return tiles, weights


def _mix_refs(refs):
    names = _MIX_TILE_INPUTS + _MIX_WEIGHTS
    r = dict(zip(names, refs[:len(names)]))
    r['w_br'] = [r['w_br0'], r['w_br1'], r['w_br2'], r['w_br3']]
    return r, refs[len(names):]


def _mix_operands(proj, o, lw):
    return ([proj] * 10 + [o] + [lw[n] for n in ['gate_b', 'sg_ln_g', 'sg_ln_b', 'sg_w', 'sg_bias', 'conv_w8', 'wbd',
                                                 'pool_scale', 'w_br_mla_p', 'w_br_sg', 'w_br_conv', 'w_br_pool', 'w_out',
                                                 'norm_mix_post']])


def _mix_fwd(x0, proj, o, lw):
    s = x0.shape[0]
    ts = _tile(s, 256)
    tiles, weights = _mix_specs(s, ts)

    def body(*refs):
        r, (x0_ref, x1_ref) = _mix_refs(refs)
        f = _mix_forward(pl.program_id(0), ts, r)
        x1_ref[...] = x0_ref[...] + _rms(f['mo'], r['g_post'][...])

    return pl.pallas_call(
        body, name="mix_fwd", grid=(s // ts,),
        in_specs=tiles + weights + [_rows(ts, D_MODEL)],
        out_specs=_rows(ts, D_MODEL),
        out_shape=jax.ShapeDtypeStruct((s, D_MODEL), F32),
        compiler_params=_params(("parallel",)),
    )(*_mix_operands(proj, o, lw), x0)


def _mix_bwd(dx1, proj, o, lw):
    s = dx1.shape[0]
    ts = _tile(s, 128)
    tiles, weights = _mix_specs(s, ts)
    hw = N_HEADS * HEAD_PAD

    def body(*refs):
        r, rest = _mix_refs(refs)
        (dx1_ref, dg_ref, dm1_ref, dyv_ref, up_ref, do_ref, delta_ref,
         dgate_b_ref, dln_g_ref, dln_b_ref, dsgw_ref, dsgb_ref, dconv_ref, dwbd_ref, dps_ref,
         dwbr0_ref, dwbr1_ref, dwbr2_ref, dwbr3_ref, dwout_ref, dgpost_ref, dbias_acc) = rest
        i = pl.program_id(0)
        acc_refs = [dgate_b_ref, dln_g_ref, dln_b_ref, dsgw_ref, dsgb_ref, dconv_ref, dwbd_ref, dps_ref,
                    dwbr0_ref, dwbr1_ref, dwbr2_ref, dwbr3_ref, dwout_ref, dgpost_ref, dbias_acc]

        @pl.when(i == 0)
        def _():
            for a in acc_refs:
                a[...] = jnp.zeros_like(a)

        f = _mix_forward(i, ts, r)
        dmo, dgpost = _rms_bwd(f['mo'], r['g_post'][...], dx1_ref[...])
        dgpost_ref[...] += dgpost
        dwout_ref[...] += _dot_tn(f['merged'], dmo)
        dmerged = _dot_nt(dmo, r['w_out'][...])
        dwbr = [dwbr0_ref, dwbr1_ref, dwbr2_ref, dwbr3_ref]
        douts = []
        for b in range(N_BRANCH):
            gb = f['gates'][:, b * D_MODEL:(b + 1) * D_MODEL]
            dgate = dmerged * f['ys'][b] * gb * (1.0 - gb)
            dg_ref[:, b * D_MODEL:(b + 1) * D_MODEL] = dgate.astype(dg_ref.dtype)
            dgate_b_ref[:, b * D_MODEL:(b + 1) * D_MODEL] += jnp.sum(dgate, axis=0, keepdims=True)
            dy = dmerged * gb
            dwbr[b][...] += _dot_tn(f['outs'][b], dy)
            douts.append(_dot_nt(dy, r['w_br'][b][...]))
        do = douts[0]
        do_ref[...] = do.astype(do_ref.dtype)
        prod = do * f['outs'][0]
        for h in range(N_HEADS):
            delta_ref[h] = jnp.sum(prod[:, h * HEAD_PAD:(h + 1) * HEAD_PAD], axis=1, keepdims=True)
        grp = _lane_group(BR_WIDTH)
        ds = douts[1] * f['u_act']
        dsgu = douts[1] * f['sg_s'] * _gelu_grad(f['sgu'], f['tu'])
        dvln_parts = []
        for ci in range(ts // SG_CHUNK):
            rows = slice(ci * SG_CHUNK, (ci + 1) * SG_CHUNK)
            ds_c, vln_c = ds[rows], f['vln'][rows]
            dvln_c = jnp.zeros((SG_CHUNK, BR_WIDTH), F32)
            for g in range(SG_GROUPS):
                dvln_c = dvln_c + jnp.where(grp == g, _dot_tn(f['wm'][g], ds_c), 0.0)
                dsgw_ref[g] += jnp.where(f['tril'], _dot_nt(jnp.where(grp == g, ds_c, 0.0), vln_c), 0.0)
            dbias_acc[...] += ds_c
            dvln_parts.append(dvln_c)
        dvln = dvln_parts[0] if len(dvln_parts) == 1 else jnp.concatenate(dvln_parts, axis=0)
        dln_g_ref[...] += jnp.sum(dvln * f['ln_xh'], axis=0, keepdims=True)
        dln_b_ref[...] += jnp.sum(dvln, axis=0, keepdims=True)
        dxh = dvln * r['ln_g'][...]
        dvg = f['ln_r'] * (dxh - jnp.mean(dxh, axis=-1, keepdims=True)
                           - f['ln_xh'] * jnp.mean(dxh * f['ln_xh'], axis=-1, keepdims=True))
        dsgv = dvg * _gelu_grad(f['sgv'], f['tv'])
        dcvb = douts[2] * f['yv']
        dyv = douts[2] * f['cvb']
        dyv_ref[...] = dyv
        for kk, zk in enumerate((f['z2'], f['z1'], f['z0'])):
            dconv_ref[kk:kk + 1, :] += jnp.sum(dyv * zk, axis=0, keepdims=True)
        dps_ref[...] += jnp.sum(douts[3] * f['mixed'], axis=0, keepdims=True)
        dmixed = douts[3] * r['pool_scale'][...]
        dwbd_ref[...] += _dot_tn(f['pooled'], dmixed)
        up_ref[...] = _dot_nt(dmixed, r['wbd'][...]) / f['cnt']
        dm1_ref[...] = jnp.concatenate([dsgu, dsgv, dcvb], axis=1).astype(dm1_ref.dtype)

        @pl.when(i == pl.num_programs(0) - 1)
        def _():
            lane = lax.broadcasted_iota(jnp.int32, (1, SG_CHUNK), 1)
            db = dbias_acc[...]
            out = jnp.zeros((SG_CHUNK, SG_CHUNK), F32)
            for g in range(SG_GROUPS):
                out = out + jnp.where(lane == g, jnp.sum(jnp.where(grp == g, db, 0.0), axis=1, keepdims=True), 0.0)
            dsgb_ref[...] = out

    acc = lambda shape: (_full(shape), jax.ShapeDtypeStruct(shape, F32))
    accs = [acc((1, N_BRANCH * D_MODEL)), acc((1, BR_WIDTH)), acc((1, BR_WIDTH)), acc((SG_GROUPS, SG_CHUNK, SG_CHUNK)),
            acc((SG_CHUNK, SG_CHUNK)), acc((8, BR_WIDTH)), acc((BR_WIDTH, BR_WIDTH)), acc((1, BR_WIDTH)),
            acc((hw, D_MODEL)), acc((BR_WIDTH, D_MODEL)), acc((BR_WIDTH, D_MODEL)), acc((BR_WIDTH, D_MODEL)),
            acc((D_MODEL, D_MODEL)), acc((1, D_MODEL))]
    tile_outs = [(_rows(ts, N_BRANCH * D_MODEL), jax.ShapeDtypeStruct((s, N_BRANCH * D_MODEL), MXU_DTYPE)),
                 (_rows(ts, 3 * BR_WIDTH), jax.ShapeDtypeStruct((s, 3 * BR_WIDTH), MXU_DTYPE)),
                 (_rows(ts, BR_WIDTH), jax.ShapeDtypeStruct((s, BR_WIDTH), F32)),
                 (_rows(ts, BR_WIDTH), jax.ShapeDtypeStruct((s, BR_WIDTH), F32)),
                 (_rows(ts, hw), jax.ShapeDtypeStruct((s, hw), MXU_DTYPE)),
                 (pl.BlockSpec((N_HEADS, ts, 1), lambda i: (0, i, 0)), jax.ShapeDtypeStruct((N_HEADS, s, 1), F32))]
    outs = tile_outs + accs
    return pl.pallas_call(
        body, name="mix_bwd", grid=(s // ts,),
        in_specs=tiles + weights + [_rows(ts, D_MODEL)],
        out_specs=[o_[0] for o_ in outs], out_shape=[o_[1] for o_ in outs],
        scratch_shapes=[pltpu.VMEM((SG_CHUNK, BR_WIDTH), F32)],
        compiler_params=_params(("arbitrary",), 60 * 1024 * 1024),
    )(*_mix_operands(proj, o, lw), dx1)


def _shift_bwd(dyv, upool, proj, conv_w8):
    s = dyv.shape[0]
    ts = _tile(s, 512)
    nb = s // HALO
    c0 = COL_M1 // BR_WIDTH

    def body(dyv_ref, dyvn_ref, up_ref, upn_ref, cvx_ref, cvc_ref, cw_ref, out_ref):
        i = pl.program_id(0)
        last = (i < pl.num_programs(0) - 1).astype(F32)
        dh = jnp.concatenate([dyv_ref[...], dyvn_ref[...] * last], axis=0)
        dz = (cw_ref[2:3, :] * dh + cw_ref[1:2, :] * _shift_up(dh, 1) + cw_ref[0:1, :] * _shift_up(dh, 2))[:ts]
        up = up_ref[...]
        uh = jnp.concatenate([up, upn_ref[...] * last], axis=0)
        dpool = _window_sums(uh, _shift_up)[:ts] - up * _pool_count(i, ts)
        out_ref[...] = jnp.concatenate([dz * cvc_ref[...].astype(F32), dz * cvx_ref[...].astype(F32), dpool],
                                       axis=1).astype(out_ref.dtype)

    nxt = pl.BlockSpec((HALO, BR_WIDTH), lambda i: (jnp.minimum((i + 1) * (ts // HALO), nb - 1), 0))
    return pl.pallas_call(
        body, name="shift_bwd", grid=(s // ts,),
        in_specs=[_rows(ts, BR_WIDTH), nxt, _rows(ts, BR_WIDTH), nxt, _rows(ts, BR_WIDTH, c0 + 3), _rows(ts, BR_WIDTH, c0 + 4),
                  _full((8, BR_WIDTH))],
        out_specs=_rows(ts, 3 * BR_WIDTH),
        out_shape=jax.ShapeDtypeStruct((s, 3 * BR_WIDTH), MXU_DTYPE),
        compiler_params=_params(("parallel",)),
    )(dyv, dyv, upool, upool, proj, proj, conv_w8)


def _ffn2(a, w2, x1, g):
    s = a.shape[0]
    ts = _tile(s, 512)

    def body(a_ref, w_ref, x1_ref, g_ref, x2_ref, f_ref):
        f = _dot(_relu_sq(a_ref[...]), w_ref[...])
        f_ref[...] = f
        x2_ref[...] = x1_ref[...] + _rms(f, g_ref[...])

    return pl.pallas_call(
        body, name="ffn2", grid=(s // ts,),
        in_specs=[_rows(ts, D_FF), _full((D_FF, D_MODEL)), _rows(ts, D_MODEL), _full((1, D_MODEL))],
        out_specs=[_rows(ts, D_MODEL), _rows(ts, D_MODEL)],
        out_shape=[jax.ShapeDtypeStruct((s, D_MODEL), F32)] * 2,
        compiler_params=_params(("parallel",)),
    )(a, w2, x1, g)


def _ffn2_bwd(dx2, f, g, a, w2t):
    s = a.shape[0]
    ts = _tile(s, 256)

    def body(dx2_ref, f_ref, g_ref, a_ref, w_ref, df_ref, da_ref, dg_ref):
        @pl.when(pl.program_id(0) == 0)
        def _():
            dg_ref[...] = jnp.zeros_like(dg_ref)

        df, dg = _rms_bwd(f_ref[...], g_ref[...], dx2_ref[...])
        dg_ref[...] += dg
        df_ref[...] = df.astype(df_ref.dtype)
        da_ref[...] = (_dot(df, w_ref[...]) * (2.0 * jnp.maximum(a_ref[...].astype(F32), 0.0))).astype(da_ref.dtype)

    return pl.pallas_call(
        body, name="ffn2_bwd", grid=(s // ts,),
        in_specs=[_rows(ts, D_MODEL), _rows(ts, D_MODEL), _full((1, D_MODEL)), _rows(ts, D_FF), _full((D_MODEL, D_FF))],
        out_specs=[_rows(ts, D_MODEL), _rows(ts, D_FF), _full((1, D_MODEL))],
        out_shape=[jax.ShapeDtypeStruct((s, D_MODEL), MXU_DTYPE), jax.ShapeDtypeStruct((s, D_FF), MXU_DTYPE),
                   jax.ShapeDtypeStruct((1, D_MODEL), F32)],
        compiler_params=_params(("arbitrary",)),
    )(dx2, f, g, a, w2t)


def _norm_in_bwd(name, pieces, x, g, dres):
    s = x.shape[0]
    ts = _tile(s, 256)
    n = len(pieces)

    def body(*refs):
        d_refs, w_refs = refs[:n], refs[n:2 * n]
        x_ref, g_ref, dres_ref, dx_ref, dg_ref = refs[2 * n:]

        @pl.when(pl.program_id(0) == 0)
        def _():
            dg_ref[...] = jnp.zeros_like(dg_ref)

        dh = _dot(d_refs[0][...], w_refs[0][...])
        for p in range(1, n):
            dh = dh + _dot(d_refs[p][...], w_refs[p][...])
        dx, dg = _rms_bwd(x_ref[...], g_ref[...], dh)
        dg_ref[...] += dg
        dx_ref[...] = dres_ref[...] + dx

    return pl.pallas_call(
        body, name=name, grid=(s // ts,),
        in_specs=[_rows(ts, d.shape[1]) for d, _ in pieces] + [_full(w.shape) for _, w in pieces]
        + [_rows(ts, D_MODEL), _full((1, D_MODEL)), _rows(ts, D_MODEL)],
        out_specs=[_rows(ts, D_MODEL), _full((1, D_MODEL))],
        out_shape=[jax.ShapeDtypeStruct((s, D_MODEL), F32), jax.ShapeDtypeStruct((1, D_MODEL), F32)],
        compiler_params=_params(("arbitrary",)),
    )(*[d for d, _ in pieces], *[w for _, w in pieces], x, g, dres)


def _loss_and_grad(y, target):
    s = y.shape[0]
    ts = _tile(s, 512)

    def body(y_ref, t_ref, dy_ref, loss_ref):
        @pl.when(pl.program_id(0) == 0)
        def _():
            loss_ref[...] = jnp.zeros_like(loss_ref)

        err = y_ref[...] - t_ref[...]
        dy_ref[...] = err * (1.0 / D_MODEL)
        loss_ref[...] += 0.5 * jnp.sum(jnp.mean(err * err, axis=-1, keepdims=True), axis=0, keepdims=True)

    dy, loss = pl.pallas_call(
        body, name="loss", grid=(s // ts,),
        in_specs=[_rows(ts, D_MODEL), _rows(ts, D_MODEL)],
        out_specs=[_rows(ts, D_MODEL), _full((8, 128))],
        out_shape=[jax.ShapeDtypeStruct((s, D_MODEL), F32), jax.ShapeDtypeStruct((8, 128), F32)],
        compiler_params=_params(("arbitrary",)),
    )(y, target)
    return loss[0, 0], dy


_W_IN_SPLITS = [256, 384, 416, 672, 928, 1184, 1440, 1696, 1952]


def _rope_swap(w):
    half = QK_ROPE // 2
    return jnp.concatenate([-w[..., half:], w[..., :half]], axis=-1)


def _rope_unswap(d):
    half = QK_ROPE // 2
    return jnp.concatenate([d[..., half:], -d[..., :half]], axis=-1)


def _zeros_like_cols(w, n):
    return jnp.zeros(w.shape[:-1] + (n,), w.dtype)


def _derive_weights(w):
    md = MXU_DTYPE
    nl = w['w_in'].shape[0]
    c_q, c_kv, k_r, sg_u, sg_v, cv_x, cv_b, cv_c, pool, gate = jnp.split(w['w_in'].astype(md), _W_IN_SPLITS, axis=-1)
    pad_rope = lambda r: jnp.concatenate([_zeros_like_cols(r, QK_NOPE), r, _zeros_like_cols(r, HEAD_PAD - QK_NOPE - QK_ROPE)], -1)
    w_in_p = jnp.concatenate([gate, sg_u, sg_v, cv_b, cv_x, cv_c, pool, c_q, c_kv, pad_rope(k_r), pad_rope(_rope_swap(k_r))], -1)
    wq = w['w_uq'].astype(md).reshape(nl, Q_RANK, N_HEADS, QK_NOPE + QK_ROPE)
    nope, rope_w = wq[..., :QK_NOPE], wq[..., QK_NOPE:]
    wq_a = jnp.concatenate([nope, rope_w, _zeros_like_cols(nope, 32)], -1).reshape(nl, Q_RANK, N_HEADS * HEAD_PAD)
    wq_b = pad_rope(_rope_swap(rope_w)).reshape(nl, Q_RANK, N_HEADS * HEAD_PAD)
    wkv = w['w_ukv'].astype(md).reshape(nl, KV_RANK, N_HEADS, QK_NOPE + V_HEAD)
    pad_half = lambda r: jnp.concatenate([r, _zeros_like_cols(r, HEAD_PAD - r.shape[-1])], -1).reshape(nl, KV_RANK, N_HEADS * HEAD_PAD)
    w_br_mla = w['w_br_mla'].astype(md).reshape(nl, N_HEADS, V_HEAD, D_MODEL)
    w_br_mla_p = jnp.concatenate([w_br_mla, jnp.zeros_like(w_br_mla)], axis=2).reshape(nl, N_HEADS * HEAD_PAD, D_MODEL)
    eye = jnp.eye(4, dtype=md)
    wbd = (w['pool_w'].astype(md)[:, :, :, None, :] * eye[None, :, None, :, None]).reshape(nl, BR_WIDTH, BR_WIDTH)
    row = lambda a: a.astype(F32)[:, None, :]
    w_in_pt = jnp.swapaxes(w_in_p, 1, 2)
    return dict(
        w_in_p=w_in_p, wt_g=w_in_pt[:, COL_G:COL_M1], wt_m1=w_in_pt[:, COL_M1:COL_M2], wt_m2=w_in_pt[:, COL_M2:COL_B],
        wt_b=w_in_pt[:, COL_B:],
        wq=jnp.concatenate([wq_a, wq_b], -1), wkv=jnp.concatenate([pad_half(wkv[..., :QK_NOPE]), pad_half(wkv[..., QK_NOPE:])], -1),
        w_br_mla_p=w_br_mla_p, w_br_sg=w['w_br_sg'].astype(md), w_br_conv=w['w_br_conv'].astype(md),
        w_br_pool=w['w_br_pool'].astype(md), wbd=wbd, w_out=w['w_out'].astype(md),
        w_ff1=w['w_ff1'].astype(md), w_ff1t=jnp.swapaxes(w['w_ff1'].astype(md), 1, 2),
        w_ff2=w['w_ff2'].astype(md), w_ff2t=jnp.swapaxes(w['w_ff2'].astype(md), 1, 2),
        norm_mix_pre=row(w['norm_mix_pre']), gate_b=row(w['gate_b']), q_norm=row(w['q_norm']), kv_norm=row(w['kv_norm']),
        sg_ln_g=row(w['sg_ln_g']), sg_ln_b=row(w['sg_ln_b']), sg_w=w['sg_w'].astype(F32),
        sg_bias=jnp.repeat(jnp.swapaxes(w['sg_b'].astype(F32), 1, 2), BR_WIDTH // SG_GROUPS, axis=2),
        conv_w8=jnp.pad(w['conv_w'].astype(F32), ((0, 0), (0, 5), (0, 0))), pool_scale=row(w['pool_scale']),
        norm_mix_post=row(w['norm_mix_post']), norm_ffn_pre=row(w['norm_ffn_pre']), norm_ffn_post=row(w['norm_ffn_post']),
    )


def _rope_tables(positions):
    inv_freq = ROPE_BASE ** (-jnp.arange(0, QK_ROPE, 2, dtype=F32) / QK_ROPE)
    ang = positions.astype(F32)[:, None] * inv_freq
    cos, sin = jnp.cos(ang), jnp.sin(ang)
    n = positions.shape[0]
    ones, z64, z32 = jnp.ones((n, QK_NOPE), F32), jnp.zeros((n, QK_NOPE), F32), jnp.zeros((n, 32), F32)
    return (jnp.concatenate([ones, cos, cos, z32], 1), jnp.concatenate([z64, sin, sin, z32], 1),
            jnp.concatenate([z64, cos, cos, z32], 1))


def _reference_layout_grads(g):
    gate, dm1, dm2, dpb = g['dw_in_pieces']
    nl = gate.shape[0]
    sg_u, sg_v, cv_b = jnp.split(dm1, 3, axis=-1)
    cv_x, cv_c, pool = jnp.split(dm2, 3, axis=-1)
    c_q, c_kv, kr, krs = jnp.split(dpb, [Q_RANK, Q_RANK + KV_RANK, Q_RANK + KV_RANK + HEAD_PAD], axis=-1)
    rope_cols = slice(QK_NOPE, QK_NOPE + QK_ROPE)
    k_r = kr[..., rope_cols] + _rope_unswap(krs[..., rope_cols])
    w_in = jnp.concatenate([c_q, c_kv, k_r, sg_u, sg_v, cv_x, cv_b, cv_c, pool, gate], -1)
    hw = N_HEADS * HEAD_PAD
    dqa = g['dwq'][..., :hw].reshape(nl, Q_RANK, N_HEADS, HEAD_PAD)
    dqb = g['dwq'][..., hw:].reshape(nl, Q_RANK, N_HEADS, HEAD_PAD)
    w_uq = jnp.concatenate([dqa[..., :QK_NOPE], dqa[..., rope_cols] + _rope_unswap(dqb[..., rope_cols])], -1)
    dka = g['dwkv'][..., :hw].reshape(nl, KV_RANK, N_HEADS, HEAD_PAD)
    dva = g['dwkv'][..., hw:].reshape(nl, KV_RANK, N_HEADS, HEAD_PAD)
    w_ukv = jnp.concatenate([dka[..., :QK_NOPE], dva[..., :V_HEAD]], -1)
    w_br_mla = g['dw_br_mla_p'].reshape(nl, N_HEADS, HEAD_PAD, D_MODEL)[:, :, :V_HEAD]
    dwbd = g['dwbd'].reshape(nl, 4, 64, 4, 64)
    pool_w = jnp.stack([dwbd[:, k, :, k, :] for k in range(4)], axis=1)
    sq = lambda a: a[:, 0, :]
    return dict(
        norm_mix_pre=sq(g['dg_pre']), w_in=w_in, gate_b=sq(g['dgate_b']), q_norm=sq(g['dq_norm']),
        w_uq=w_uq.reshape(nl, Q_RANK, -1), kv_norm=sq(g['dkv_norm']), w_ukv=w_ukv.reshape(nl, KV_RANK, -1),
        w_br_mla=w_br_mla.reshape(nl, N_HEADS * V_HEAD, D_MODEL), sg_ln_g=sq(g['dln_g']), sg_ln_b=sq(g['dln_b']),
        sg_w=g['dsg_w'], sg_b=jnp.swapaxes(g['dsg_b'][:, :, :SG_GROUPS], 1, 2), w_br_sg=g['dw_br_sg'],
        conv_w=g['dconv_w'][:, :3], w_br_conv=g['dw_br_conv'], pool_w=pool_w, pool_scale=sq(g['dpool_scale']),
        w_br_pool=g['dw_br_pool'], w_out=g['dw_out'], norm_mix_post=sq(g['dg_post']), norm_ffn_pre=sq(g['dg_fpre']),
        w_ff1=g['dw_ff1'], w_ff2=g['dw_ff2'], norm_ffn_post=sq(g['dg_fpost']))


def _layer_forward(x0, lw, tabs):
    proj = _mm("in_proj", x0, lw['w_in_p'], tm=512, tn=896, prologue=_rms, rows=(lw['norm_mix_pre'],))
    q, k, v = _qkv_prep(proj, lw['q_norm'], lw['kv_norm'], lw['wq'], lw['wkv'], *tabs)
    o, lse = _attn_fwd(q, k, v)
    x1 = _mix_fwd(x0, proj, o, lw)
    a = _mm("ffn1", x1, lw['w_ff1'], tm=512, tn=1024, prologue=_rms, rows=(lw['norm_ffn_pre'],))
    x2, f = _ffn2(a, lw['w_ff2'], x1, lw['norm_ffn_post'])
    return x2, dict(x0=x0, proj=proj, q=q, k=k, v=v, o=o, lse=lse, x1=x1, a=a, f=f)


def _layer_backward(dx2, lw, sv, tabs):
    g = {}
    df, da, g['dg_fpost'] = _ffn2_bwd(dx2, sv['f'], lw['norm_ffn_post'], sv['a'], lw['w_ff2t'])
    g['dw_ff2'] = _mm_tn("dw_ff2", sv['a'], df, tm=256, tn=512, prologue=_relu_sq)
    dx1, g['dg_fpre'] = _norm_in_bwd("ffn1_bwd", [(da, lw['w_ff1t'])], sv['x1'], lw['norm_ffn_pre'], dx2)
    g['dw_ff1'] = _mm_tn("dw_ff1", sv['x1'], da, tm=512, tn=1024, prologue=_rms, rows=(lw['norm_ffn_pre'],))
    (dgate, dm1, dyv, upool, do, delta, g['dgate_b'], g['dln_g'], g['dln_b'], g['dsg_w'], g['dsg_b'], g['dconv_w'],
     g['dwbd'], g['dpool_scale'], g['dw_br_mla_p'], g['dw_br_sg'], g['dw_br_conv'], g['dw_br_pool'], g['dw_out'],
     g['dg_post']) = _mix_bwd(dx1, sv['proj'], sv['o'], lw)
    dm2 = _shift_bwd(dyv, upool, sv['proj'], lw['conv_w8'])
    dq, dk, dv = _attn_bwd(sv['q'], sv['k'], sv['v'], do, sv['lse'], delta)
    dpb, g['dwq'], g['dwkv'], g['dq_norm'], g['dkv_norm'] = _qkv_bwd(
        dq, dk, dv, sv['proj'], lw['q_norm'], lw['kv_norm'], lw['wq'], lw['wkv'], *tabs)
    pieces = [(dgate, lw['wt_g']), (dm1, lw['wt_m1']), (dm2, lw['wt_m2']), (dpb, lw['wt_b'])]
    dx0, g['dg_pre'] = _norm_in_bwd("in_proj_bwd", pieces, sv['x0'], lw['norm_mix_pre'], dx1)
    g['dw_in_pieces'] = [_mm_tn("dw_in_%d" % n, sv['x0'], d, tm=512, tn=1024, prologue=_rms, rows=(lw['norm_mix_pre'],))
                         for n, (d, _) in enumerate(pieces)]
    return dx0, g


def _local_step(x, positions, target, layers, after_layer_backward):
    tabs = _rope_tables(positions)
    derived = [{n: a[0] for n, a in _derive_weights(w).items()} for w in layers]
    saved = []
    for lw in derived:
        x, sv = _layer_forward(x, lw, tabs)
        saved.append(sv)
    loss, dx = _loss_and_grad(x, target)
    for l in reversed(range(len(layers))):
        dx, g = _layer_backward(dx, derived[l], saved[l], tabs)
        lead = lambda a: [b[None] for b in a] if isinstance(a, list) else a[None]
        after_layer_backward(l, _reference_layout_grads({n: lead(a) for n, a in g.items()}))
    return loss, dx


HBM = pl.BlockSpec(memory_space=pl.ANY)


def _relative_peers():
    x, y = lax.axis_index("x"), lax.axis_index("y")
    return {1: (x, 1 - y), 2: (1 - x, y), 3: (1 - x, 1 - y)}


def _gather_chips(name, split, whole):
    ns, nw = len(split), len(whole)
    n = ns + nw

    def body(*refs):
        ins, outs = refs[:n], refs[n:2 * n]
        ici_send, ici_recv, d2d_send, d2d_recv, own_send, own_recv = refs[2 * n:]
        x, y, c = lax.axis_index("x"), lax.axis_index("y"), lax.axis_index("c")
        peers = _relative_peers()

        def run(half):
            def rows(ref, which):
                h = ref.shape[-2] // 2
                return ref.at[(slice(None),) * (len(ref.shape) - 2) + (slice(which * h, (which + 1) * h), slice(None))]

            started = []
            for k in range(n):
                own = pltpu.make_async_remote_copy(src_ref=ins[k], dst_ref=outs[k].at[0], send_sem=own_send.at[k],
                                                   recv_sem=own_recv.at[k], device_id=(x, y, 1 - c), device_id_type=MESH)
                own.start()
                started.append(own.wait)
                for r, (px, py) in peers.items():
                    src = rows(ins[k], half) if k < ns else ins[k]
                    dst = rows(outs[k].at[r], half) if k < ns else outs[k].at[r]
                    cp = pltpu.make_async_remote_copy(src_ref=src, dst_ref=dst, send_sem=ici_send.at[3 * k + r - 1],
                                                      recv_sem=ici_recv.at[3 * k + r - 1], device_id=(px, py, c),
                                                      device_id_type=MESH)
                    cp.start()
                    started.append(cp.wait_send)
            for k in range(n):
                for r, (px, py) in peers.items():
                    landed = rows(outs[k].at[r], half) if k < ns else outs[k].at[r]
                    pltpu.make_async_remote_copy(src_ref=landed, dst_ref=landed, send_sem=ici_send.at[3 * k + r - 1],
                                                 recv_sem=ici_recv.at[3 * k + r - 1], device_id=(px, py, c),
                                                 device_id_type=MESH).wait_recv()
                    if k < ns:
                        fwd = pltpu.make_async_remote_copy(src_ref=landed, dst_ref=landed, send_sem=d2d_send.at[3 * k + r - 1],
                                                           recv_sem=d2d_recv.at[3 * k + r - 1], device_id=(x, y, 1 - c),
                                                           device_id_type=MESH)
                        fwd.start()
                        started.append(fwd.wait_send)
            for k in range(ns):
                for r in peers:
                    other = rows(outs[k].at[r], 1 - half)
                    pltpu.make_async_remote_copy(src_ref=other, dst_ref=other, send_sem=d2d_send.at[3 * k + r - 1],
                                                 recv_sem=d2d_recv.at[3 * k + r - 1], device_id=(x, y, 1 - c),
                                                 device_id_type=MESH).wait_recv()
            for wait in started:
                wait()

        for half in (0, 1):
            pl.when(c == half)(functools.partial(run, half))

    arrs = list(split) + list(whole)
    return pl.pallas_call(
        body, name=name,
        in_specs=[HBM] * n, out_specs=[HBM] * n,
        out_shape=[jax.ShapeDtypeStruct((4,) + a.shape, a.dtype) for a in arrs],
        scratch_shapes=[pltpu.SemaphoreType.DMA((3 * n,)), pltpu.SemaphoreType.DMA((3 * n,)), pltpu.SemaphoreType.DMA((3 * ns,)),
                        pltpu.SemaphoreType.DMA((3 * ns,)), pltpu.SemaphoreType.DMA((n,)), pltpu.SemaphoreType.DMA((n,))],
    )(*arrs)


def _absolute_chip_order(relative):
    me = 2 * lax.axis_index("x") + lax.axis_index("y")
    return jnp.stack([lax.dynamic_index_in_dim(relative, jnp.bitwise_xor(me, chip), 0, keepdims=False) for chip in range(4)])


REDUCE_STEPS = 8


def _sibling_halves(name, arrs):
    n = len(arrs)

    def body(*refs):
        ins, theirs = refs[:n], refs[n:2 * n]
        send_sems, recv_sems = refs[2 * n:]
        x, y, c = lax.axis_index("x"), lax.axis_index("y"), lax.axis_index("c")

        def exchange(my_half):
            copies = []
            for k in range(n):
                h = ins[k].shape[1] // 2
                cp = pltpu.make_async_remote_copy(src_ref=ins[k].at[:, (1 - my_half) * h:(2 - my_half) * h, :], dst_ref=theirs[k],
                                                  send_sem=send_sems.at[k], recv_sem=recv_sems.at[k],
                                                  device_id=(x, y, 1 - c), device_id_type=MESH)
                cp.start()
                copies.append(cp)
            for cp in copies:
                cp.wait()

        for half in (0, 1):
            pl.when(c == half)(functools.partial(exchange, half))

    return pl.pallas_call(
        body, name=name, in_specs=[HBM] * n, out_specs=[HBM] * n,
        out_shape=[jax.ShapeDtypeStruct((a.shape[0], a.shape[1] // 2, a.shape[2]), a.dtype) for a in arrs],
        scratch_shapes=[pltpu.SemaphoreType.DMA((n,)), pltpu.SemaphoreType.DMA((n,))],
    )(*arrs)


def _add_sibling(name, arrs, theirs):
    n, steps = len(arrs), REDUCE_STEPS

    def body(*refs):
        for mine_ref, theirs_ref, out_ref in zip(refs[:n], refs[n:2 * n], refs[2 * n:]):
            out_ref[...] = (mine_ref[...] + theirs_ref[...]).astype(out_ref.dtype)

    block = lambda t: (4, t.shape[1] // steps, t.shape[2])
    return pl.pallas_call(
        body, name=name, grid=(steps,),
        in_specs=[pl.BlockSpec(block(t), lambda i: (0, lax.axis_index("c") * steps + i, 0)) for t in theirs]
        + [pl.BlockSpec(block(t), lambda i: (0, i, 0)) for t in theirs],
        out_specs=[pl.BlockSpec(block(t), lambda i: (0, i, 0)) for t in theirs],
        out_shape=[jax.ShapeDtypeStruct(t.shape, WIRE_DTYPE) for t in theirs],
        compiler_params=_params(("parallel",)))(*arrs, *theirs)


def _scatter_chips(name, arrs):
    n = len(arrs)

    def body(*refs):
        ins, outs = refs[:n], refs[n:2 * n]
        send_sems, recv_sems = refs[2 * n:]
        c = lax.axis_index("c")
        peers = _relative_peers()
        sends = []
        for k in range(n):
            for r, (px, py) in peers.items():
                cp = pltpu.make_async_remote_copy(src_ref=ins[k].at[2 * px + py], dst_ref=outs[k].at[r - 1],
                                                  send_sem=send_sems.at[3 * k + r - 1], recv_sem=recv_sems.at[3 * k + r - 1],
                                                  device_id=(px, py, c), device_id_type=MESH)
                cp.start()
                sends.append(cp)
        for k in range(n):
            for r, (px, py) in peers.items():
                pltpu.make_async_remote_copy(src_ref=ins[k].at[0], dst_ref=outs[k].at[r - 1],
                                             send_sem=send_sems.at[3 * k + r - 1], recv_sem=recv_sems.at[3 * k + r - 1],
                                             device_id=(px, py, c), device_id_type=MESH).wait_recv()
        for cp in sends:
            cp.wait_send()

    return pl.pallas_call(
        body, name=name, in_specs=[HBM] * n, out_specs=[HBM] * n,
        out_shape=[jax.ShapeDtypeStruct((3,) + a.shape[1:], a.dtype) for a in arrs],
        scratch_shapes=[pltpu.SemaphoreType.DMA((3 * n,)), pltpu.SemaphoreType.DMA((3 * n,))],
    )(*arrs)


def _sum_chips(name, chip_sums, arrived):
    n, steps = len(chip_sums), REDUCE_STEPS

    def body(*refs):
        for own_ref, arrived_ref, out_ref in zip(refs[:n], refs[n:2 * n], refs[2 * n:]):
            acc = own_ref[...].astype(F32)
            for r in range(3):
                acc = acc + arrived_ref[r].astype(F32)
            out_ref[...] = acc

    rows = lambda s: s.shape[1] // steps
    chip = lambda: 2 * lax.axis_index("x") + lax.axis_index("y")
    return pl.pallas_call(
        body, name=name, grid=(steps,),
        in_specs=[pl.BlockSpec((None, rows(s), s.shape[2]), lambda i: (chip(), i, 0)) for s in chip_sums]
        + [pl.BlockSpec((3, rows(s), s.shape[2]), lambda i: (0, i, 0)) for s in chip_sums],
        out_specs=[pl.BlockSpec((rows(s), s.shape[2]), lambda i: (lax.axis_index("c") * steps + i, 0)) for s in chip_sums],
        out_shape=[jax.ShapeDtypeStruct((2 * s.shape[1], s.shape[2]), F32) for s in chip_sums],
        compiler_params=_params(("parallel",)))(*chip_sums, *arrived)


def _join_siblings(name, bufs):
    n = len(bufs)

    def body(*refs):
        outs = refs[n:2 * n]
        send_sems, recv_sems = refs[2 * n:]
        x, y, c = lax.axis_index("x"), lax.axis_index("y"), lax.axis_index("c")

        def exchange(my_half):
            copies = []
            for k in range(n):
                h = outs[k].shape[0] // 2
                mine = outs[k].at[my_half * h:(my_half + 1) * h, :]
                theirs = outs[k].at[(1 - my_half) * h:(2 - my_half) * h, :]
                cp = pltpu.make_async_remote_copy(src_ref=mine, dst_ref=mine, send_sem=send_sems.at[k],
                                                  recv_sem=recv_sems.at[k], device_id=(x, y, 1 - c), device_id_type=MESH)
                cp.start()
                arrival = pltpu.make_async_remote_copy(src_ref=theirs, dst_ref=theirs, send_sem=send_sems.at[k],
                                                       recv_sem=recv_sems.at[k], device_id=(x, y, 1 - c), device_id_type=MESH)
                copies.append((cp, arrival))
            for cp, arrival in copies:
                arrival.wait_recv()
                cp.wait_send()

        for half in (0, 1):
            pl.when(c == half)(functools.partial(exchange, half))

    return pl.pallas_call(
        body, name=name, in_specs=[HBM] * n, out_specs=[HBM] * n,
        out_shape=[jax.ShapeDtypeStruct(b.shape, b.dtype) for b in bufs], input_output_aliases={k: k for k in range(n)},
        scratch_shapes=[pltpu.SemaphoreType.DMA((n,)), pltpu.SemaphoreType.DMA((n,))],
    )(*bufs)


def _gather_all(name, a):
    def body(a_ref, out_ref, staging, send_sems, recv_sems, local_sem):
        x, y, c = lax.axis_index("x"), lax.axis_index("y"), lax.axis_index("c")
        me = 4 * x + 2 * y + c
        flips = [(fx, fy, fc) for fx in (0, 1) for fy in (0, 1) for fc in (0, 1)][1:]
        peers = [(x ^ fx, y ^ fy, c ^ fc) for fx, fy, fc in flips]
        load = pltpu.make_async_copy(a_ref, staging, local_sem)
        load.start()
        load.wait()
        local = pltpu.make_async_copy(staging, out_ref.at[me], local_sem)
        local.start()
        sends = []
        for j, peer in enumerate(peers):
            cp = pltpu.make_async_remote_copy(src_ref=a_ref, dst_ref=out_ref.at[me], send_sem=send_sems.at[j],
                                              recv_sem=recv_sems.at[j], device_id=peer, device_id_type=MESH)
            cp.start()
            sends.append(cp)
        for j, (px, py, pc) in enumerate(peers):
            pltpu.make_async_remote_copy(src_ref=a_ref, dst_ref=out_ref.at[4 * px + 2 * py + pc], send_sem=send_sems.at[j],
                                         recv_sem=recv_sems.at[j], device_id=(px, py, pc), device_id_type=MESH).wait_recv()
        for cp in sends:
            cp.wait_send()
        local.wait()

    return pl.pallas_call(
        body, name=name, in_specs=[HBM], out_specs=HBM, out_shape=jax.ShapeDtypeStruct((8,) + a.shape, a.dtype),
        scratch_shapes=[pltpu.VMEM(a.shape, a.dtype), pltpu.SemaphoreType.DMA((7,)), pltpu.SemaphoreType.DMA((7,)),
                        pltpu.SemaphoreType.DMA],
    )(a)


def _rowwise_call(name, fn, slots, out_shapes, steps):
    n_in, n_out = [len(s) for s in slots], [len(o) for o in out_shapes]

    def spec(shape):
        if len(shape) == 3:
            return pl.BlockSpec((shape[0], shape[1] // steps, shape[2]), lambda i: (0, i, 0))
        return pl.BlockSpec((shape[0] // steps, shape[1]), lambda i: (i, 0))

    def body(*refs):
        ins, outs = refs[:sum(n_in)], refs[sum(n_in):]
        a = b = 0
        for k in range(len(slots)):
            for o_ref, val in zip(outs[b:b + n_out[k]], fn(*[r[...] for r in ins[a:a + n_in[k]]])):
                o_ref[...] = val
            a, b = a + n_in[k], b + n_out[k]

    flat_in = [arr for s in slots for arr in s]
    flat_out = [shp for o in out_shapes for shp in o]
    out = pl.pallas_call(body, name=name, grid=(steps,), in_specs=[spec(a.shape) for a in flat_in],
                         out_specs=[spec(s) for s in flat_out], out_shape=[jax.ShapeDtypeStruct(s, F32) for s in flat_out],
                         compiler_params=_params(("parallel",)))(*flat_in)
    grouped, b = [], 0
    for k in range(len(slots)):
        grouped.append(out[b:b + n_out[k]])
        b += n_out[k]
    return grouped


def _sum_in_order(a):
    acc = a[0].astype(F32)
    for k in range(1, a.shape[0]):
        acc = acc + a[k].astype(F32)
    return (acc,)


def _adamw_math(w, g, m, v):
    m_new = ADAM_B1 * m + (1.0 - ADAM_B1) * g
    v_new = ADAM_B2 * v + (1.0 - ADAM_B2) * (g * g)
    m_hat = m_new / (1.0 - ADAM_B1 ** ADAM_STEP)
    v_hat = v_new / (1.0 - ADAM_B2 ** ADAM_STEP)
    return -ADAM_LR * (m_hat / (jnp.sqrt(v_hat) + ADAM_EPS) + ADAM_WD * w), m_new, v_new


SMALL_PACK_COLS = 256
SMALL_PACK_ROWS = 2048


def _pack_small(parts):
    wide = [jnp.pad(p, ((0, 0), (0, 0), (0, SMALL_PACK_COLS - p.shape[2]))) for p in parts]
    rows = jnp.concatenate(wide, axis=1)
    return jnp.pad(rows, ((0, 0), (0, SMALL_PACK_ROWS - rows.shape[1]), (0, 0)))


def _unpack_small(packed, shapes):
    out, row = [], 0
    for a, b in shapes:
        out.append(packed[:, row:row + a, :b])
        row += a
    return out


def _pack(arrs, rows_per_layer, dtype):
    nl = arrs[0].shape[0]
    flat = jnp.concatenate([a.astype(dtype).reshape(nl, -1) for a in arrs], axis=1)
    flat = jnp.pad(flat, ((0, 0), (0, rows_per_layer * PACK_COLS - flat.shape[1])))
    return flat.reshape(nl * rows_per_layer, PACK_COLS)


def _unpack(packed, shapes, rows_per_layer):
    nl = shapes[0][0]
    flat = packed.reshape(packed.shape[:-2] + (nl, rows_per_layer * PACK_COLS))
    out, off = [], 0
    for shp in shapes:
        size = math.prod(shp[1:])
        out.append(flat[..., off:off + size].reshape(packed.shape[:-2] + tuple(shp)))
        off += size
    return out


def _rows_needed(shapes, multiple):
    per_layer = sum(math.prod(s[1:]) for s in shapes)
    rows = -(-per_layer // PACK_COLS)
    return -(-rows // multiple) * multiple


def _full_weights(w):
    nl = w['w_in'].shape[0]
    conv_flat = w['conv_w'].reshape(-1, 128)
    conv_rows = conv_flat.shape[0]
    conv_flat = jnp.pad(conv_flat, ((0, -conv_rows % 8), (0, 0)))
    gathered = _gather_chips("gather_weights", [w[n].astype(MXU_DTYPE) for n in MATMUL_SHARDED], [conv_flat])
    gathered = [_absolute_chip_order(g) for g in gathered]
    full = dict(w)
    for n, part in zip(MATMUL_SHARDED, gathered):
        if n in ROW_SHARDED:
            full[n] = jnp.swapaxes(part, 0, 1).reshape(nl, 4 * part.shape[2], part.shape[3])
        else:
            full[n] = jnp.transpose(part, (1, 2, 0, 3)).reshape(nl, part.shape[2], 4 * part.shape[3])
    conv = gathered[-1][:, :conv_rows].reshape((4,) + w['conv_w'].shape)
    full['conv_w'] = jnp.transpose(conv, (1, 2, 0, 3)).reshape(nl, 3, -1)
    return full


def _chip_major(n, g):
    nl = g.shape[0]
    if n in ROW_SHARDED:
        return jnp.swapaxes(g.reshape(nl, 4, g.shape[1] // 4, g.shape[2]), 0, 1)
    return jnp.transpose(g.reshape(nl, g.shape[1], 4, g.shape[2] // 4), (2, 0, 1, 3))


def kernel(x, positions, norm_mix_pre, w_in, gate_b, q_norm, w_uq, kv_norm, w_ukv, w_br_mla, sg_ln_g, sg_ln_b, sg_w, sg_b, w_br_sg, conv_w, w_br_conv, pool_w, pool_scale, w_br_pool, w_out, norm_mix_post, norm_ffn_pre, w_ff1, w_ff2, norm_ffn_post, loss_target, m_norm_mix_pre, m_w_in, m_gate_b, m_q_norm, m_w_uq, m_kv_norm, m_w_ukv, m_w_br_mla, m_sg_ln_g, m_sg_ln_b, m_sg_w, m_sg_b, m_w_br_sg, m_conv_w, m_w_br_conv, m_pool_w, m_pool_scale, m_w_br_pool, m_w_out, m_norm_mix_post, m_norm_ffn_pre, m_w_ff1, m_w_ff2, m_norm_ffn_post, v_norm_mix_pre, v_w_in, v_gate_b, v_q_norm, v_w_uq, v_kv_norm, v_w_ukv, v_w_br_mla, v_sg_ln_g, v_sg_ln_b, v_sg_w, v_sg_b, v_w_br_sg, v_conv_w, v_w_br_conv, v_pool_w, v_pool_scale, v_w_br_pool, v_w_out, v_norm_mix_post, v_norm_ffn_pre, v_w_ff1, v_w_ff2, v_norm_ffn_post):
    given = dict(locals())
    w = {n: given[n] for n in WEIGHTS}
    mom = {n: given['m_' + n] for n in WEIGHTS}
    var = {n: given['v_' + n] for n in WEIGHTS}
    nl = w['w_in'].shape[0]
    full = _full_weights(w)
    layers = [{n: full[n][l:l + 1] for n in WEIGHTS} for l in range(nl)]
    reduced, local_small = {}, {}

    def reduce_layer(l, g):
        arrs = [_chip_major(n, g[n])[:, 0] for n in BIG_SHARDED]
        arrs.append(_pack_small([_chip_major(n, g[n])[:, 0] for n in SMALL_SHARDED]))
        chip_sums = _add_sibling("add_sibling", arrs, _sibling_halves("grads_to_sibling", arrs))
        halves = _sum_chips("sum_chips", chip_sums, _scatter_chips("grads_to_chips", chip_sums))
        reduced[l] = _join_siblings("join_halves", halves)
        local_small[l] = [g[n] for n in REPLICATED + ['conv_w']]

    loss, dx = _local_step(x[0], positions[0], loss_target[0], layers, reduce_layer)
    loss = lax.psum(loss, ("x", "y", "c"))

    grad, delta, new_m, new_v = {}, {}, {}, {}
    for k, n in enumerate(BIG_SHARDED):
        grad[n] = jnp.stack([reduced[l][k] for l in range(nl)])
    slots = [[w[n], grad[n], mom[n], var[n]] for n in BIG_SHARDED]
    small_pack = lambda d: _pack_small([d[n] for n in SMALL_SHARDED])
    g_small = jnp.stack([reduced[l][-1] for l in range(nl)])
    slots.append([small_pack(w), g_small, small_pack(mom), small_pack(var)])
    updated = _rowwise_call("adamw_sharded", _adamw_math, slots, [[s_[0].shape] * 3 for s_ in slots], 32)
    for k, n in enumerate(BIG_SHARDED):
        delta[n], new_m[n], new_v[n] = updated[k]
    for d, packed in zip((grad, delta, new_m, new_v), [g_small] + list(updated[-1])):
        d.update(zip(SMALL_SHARDED, _unpack_small(packed, [w[n].shape[1:] for n in SMALL_SHARDED])))

    names = REPLICATED + ['conv_w']
    local = [jnp.concatenate([local_small[l][k] for l in range(nl)]) for k in range(len(names))]
    rows = _rows_needed([a.shape for a in local], 32)
    everyone = _gather_all("gather_small_grads", _pack(local, rows, F32))
    (summed,), = _rowwise_call("sum_devices", _sum_in_order, [[everyone]], [[everyone.shape[1:]]], 4)
    g_rep = _unpack(summed, [a.shape for a in local], rows)
    chip = 2 * lax.axis_index("x") + lax.axis_index("y")
    g_rep[-1] = lax.dynamic_slice_in_dim(g_rep[-1], chip * w['conv_w'].shape[2], w['conv_w'].shape[2], axis=2)
    rep_pack = lambda arrs: _pack(arrs, rows, F32)
    (rep_out,) = _rowwise_call("adamw_replicated", _adamw_math,
                               [[rep_pack([w[n] for n in names]), rep_pack(g_rep), rep_pack([mom[n] for n in names]),
                                 rep_pack([var[n] for n in names])]], [[(nl * rows, PACK_COLS)] * 3], 4)
    grad.update(zip(names, g_rep))
    for d, packed in zip((delta, new_m, new_v), rep_out):
        d.update(zip(names, _unpack(packed, [w[n].shape for n in names], rows)))

    return (loss, dx[None], *[grad[n] for n in WEIGHTS], *[delta[n] for n in WEIGHTS], *[new_m[n] for n in WEIGHTS],
            *[new_v[n] for n in WEIGHTS])
```

```python
import functools
import math

import jax
import jax.numpy as jnp
from jax import lax
from jax.experimental import pallas as pl
from jax.experimental.pallas import tpu as pltpu

F32 = jnp.float32
MXU_DTYPE = jnp.bfloat16
WIRE_DTYPE = jnp.bfloat16
ACT_DTYPE = jnp.bfloat16
MESH = pl.DeviceIdType.MESH

D_MODEL = 1024
D_FF = 4096
N_HEADS = 4
QK_NOPE = 64
QK_ROPE = 32
V_HEAD = 64
HEAD_PAD = 128
Q_RANK = 256
KV_RANK = 128
SG_CHUNK = 128
SG_GROUPS = 4
BR_WIDTH = 256
N_BRANCH = 4
POOL_WINDOWS = (2, 4, 8, 16)
HALO = 16
ROPE_BASE = 10000.0
EPS = 1e-6
ATTN_SCALE = (QK_NOPE + QK_ROPE) ** -0.5
N_PROJ = N_BRANCH * D_MODEL + 6 * BR_WIDTH + Q_RANK + KV_RANK + 2 * HEAD_PAD
COL_G, COL_M1, COL_M2, COL_B = 0, 4096, 4864, 5632

ADAM_LR, ADAM_B1, ADAM_B2, ADAM_EPS, ADAM_WD, ADAM_STEP = 0.001, 0.9, 0.999, 1e-08, 0.01, 10

VMEM_LIMIT = 56 * 1024 * 1024

WEIGHTS = ['norm_mix_pre', 'w_in', 'gate_b', 'q_norm', 'w_uq', 'kv_norm', 'w_ukv', 'w_br_mla', 'sg_ln_g', 'sg_ln_b',
           'sg_w', 'sg_b', 'w_br_sg', 'conv_w', 'w_br_conv', 'pool_w', 'pool_scale', 'w_br_pool', 'w_out',
           'norm_mix_post', 'norm_ffn_pre', 'w_ff1', 'w_ff2', 'norm_ffn_post']
COL_SHARDED = ['w_in', 'w_uq', 'w_ukv', 'w_br_mla', 'w_br_sg', 'w_br_conv', 'w_br_pool', 'w_ff1']
ROW_SHARDED = ['w_out', 'w_ff2']
MATMUL_SHARDED = ['w_in', 'w_uq', 'w_ukv', 'w_br_mla', 'w_br_sg', 'w_br_conv', 'w_br_pool', 'w_out', 'w_ff1', 'w_ff2']
BIG_SHARDED = ['w_in', 'w_ff1', 'w_ff2', 'w_out']
SMALL_SHARDED = ['w_uq', 'w_ukv', 'w_br_mla', 'w_br_sg', 'w_br_conv', 'w_br_pool']
SHARDED = MATMUL_SHARDED + ['conv_w']
REPLICATED = [n for n in WEIGHTS if n not in SHARDED]
PACK_COLS = 1024


def _params(sem, vmem=VMEM_LIMIT):
    return pltpu.CompilerParams(dimension_semantics=sem, vmem_limit_bytes=vmem)


def _mxu(a):
    return a.astype(MXU_DTYPE)


def _dot(a, b):
    return jnp.dot(_mxu(a), _mxu(b), preferred_element_type=F32)


def _dot_nt(a, b):
    return lax.dot_general(_mxu(a), _mxu(b), (((1,), (1,)), ((), ())), preferred_element_type=F32)


def _dot_tn(a, b):
    return lax.dot_general(_mxu(a), _mxu(b), (((0,), (0,)), ((), ())), preferred_element_type=F32)


def _rms(x, g):
    r = lax.rsqrt(jnp.mean(x * x, axis=-1, keepdims=True) + EPS)
    return x * r * g


def _rms_bwd(x, g, dy):
    r = lax.rsqrt(jnp.mean(x * x, axis=-1, keepdims=True) + EPS)
    xh = x * r
    gdy = dy * g
    dx = r * (gdy - xh * jnp.mean(gdy * xh, axis=-1, keepdims=True))
    return dx, jnp.sum(dy * xh, axis=0, keepdims=True)


_GELU_C = math.sqrt(2.0 / math.pi)


def _gelu(x):
    t = jnp.tanh(_GELU_C * (x + 0.044715 * (x * x * x)))
    return x * (0.5 * (1.0 + t)), t


def _gelu_grad(x, t):
    return 0.5 * (1.0 + t) + 0.5 * x * (1.0 - t * t) * (_GELU_C * (1.0 + 3.0 * 0.044715 * x * x))


def _sigmoid(x):
    return 1.0 / (1.0 + jnp.exp(-x))


def _full(shape):
    return pl.BlockSpec(shape, lambda *_: (0,) * len(shape))


def _resident(shape):
    return pl.BlockSpec(shape, lambda *_: (0,) * len(shape), pipeline_mode=pl.Buffered(1))


def _rows(ts, width, col=0):
    return pl.BlockSpec((ts, width), lambda i: (i, col))


def _tile(n, pref):
    return min(n, pref)


def _mm(name, a, w, *, tm, tn, prologue=None, rows=()):
    m, k = a.shape
    n = w.shape[1]
    tm, tn = _tile(m, tm), _tile(n, tn)

    def body(a_ref, *rest):
        row_refs, w_ref, o_ref = rest[:len(rows)], rest[len(rows)], rest[len(rows) + 1]
        av = a_ref[...]
        if prologue is not None:
            av = prologue(av, *[r[...] for r in row_refs])
        o_ref[...] = _dot(av, w_ref[...]).astype(o_ref.dtype)

    return pl.pallas_call(
        body, name=name, grid=(m // tm, n // tn),
        in_specs=[pl.BlockSpec((tm, k), lambda i, j: (i, 0))] + [pl.BlockSpec((1, k), lambda i, j: (0, 0)) for _ in rows]
        + [pl.BlockSpec((k, tn), lambda i, j: (0, j))],
        out_specs=pl.BlockSpec((tm, tn), lambda i, j: (i, j)),
        out_shape=jax.ShapeDtypeStruct((m, n), ACT_DTYPE),
        compiler_params=_params(("parallel", "parallel")),
    )(a, *rows, w)


def _mm_tn(name, a, b, *, tm, tn, prologue=None, rows=()):
    m, k = a.shape
    n = b.shape[1]
    tm, tn = _tile(m, tm), _tile(n, tn)

    def body(a_ref, *rest):
        row_refs, b_ref, o_ref = rest[:len(rows)], rest[len(rows)], rest[len(rows) + 1]

        @pl.when(pl.program_id(1) == 0)
        def _():
            o_ref[...] = jnp.zeros_like(o_ref)

        av = a_ref[...]
        if prologue is not None:
            av = prologue(av, *[r[...] for r in row_refs])
        o_ref[...] += _dot_tn(av, b_ref[...])

    return pl.pallas_call(
        body, name=name, grid=(n // tn, m // tm),
        in_specs=[pl.BlockSpec((tm, k), lambda j, i: (i, 0))] + [pl.BlockSpec((1, k), lambda j, i: (0, 0)) for _ in rows]
        + [pl.BlockSpec((tm, tn), lambda j, i: (i, j))],
        out_specs=pl.BlockSpec((k, tn), lambda j, i: (0, j)),
        out_shape=jax.ShapeDtypeStruct((k, n), F32),
        compiler_params=_params(("parallel", "arbitrary")),
    )(a, *rows, b)


def _relu_sq(a):
    r = jnp.maximum(a.astype(F32), 0.0)
    return r * r


def _qkv_prep(proj, q_norm, kv_norm, wq, wkv, cq_tab, s_tab, cr_tab):
    s = proj.shape[0]
    ts = _tile(s, 512)
    hw = N_HEADS * HEAD_PAD

    def body(cq_ref, ckv_ref, kr_ref, krs_ref, gq_ref, gkv_ref, wq_ref, wkv_ref, ct_ref, st_ref, crt_ref,
             q_ref, k_ref, v_ref):
        ct, st, crt = ct_ref[...], st_ref[...], crt_ref[...]
        qn = _rms(cq_ref[...].astype(F32), gq_ref[...])
        qab = _dot(qn, wq_ref[...])
        kvn = _rms(ckv_ref[...].astype(F32), gkv_ref[...])
        kav = _dot(kvn, wkv_ref[...])
        k_rope = kr_ref[...].astype(F32) * crt + krs_ref[...].astype(F32) * st
        for h in range(N_HEADS):
            lo = h * HEAD_PAD
            q_ref[h] = (qab[:, lo:lo + HEAD_PAD] * ct + qab[:, hw + lo:hw + lo + HEAD_PAD] * st).astype(q_ref.dtype)
            k_ref[h] = (kav[:, lo:lo + HEAD_PAD] + k_rope).astype(k_ref.dtype)
            v_ref[h] = kav[:, hw + lo:hw + lo + HEAD_PAD].astype(v_ref.dtype)

    head_spec = pl.BlockSpec((N_HEADS, ts, HEAD_PAD), lambda i: (0, i, 0))
    head_shape = jax.ShapeDtypeStruct((N_HEADS, s, HEAD_PAD), MXU_DTYPE)
    return pl.pallas_call(
        body, name="qkv_prep", grid=(s // ts,),
        in_specs=[_rows(ts, Q_RANK, COL_B // Q_RANK), _rows(ts, KV_RANK, (COL_B + Q_RANK) // KV_RANK),
                  _rows(ts, HEAD_PAD, (COL_B + Q_RANK + KV_RANK) // HEAD_PAD),
                  _rows(ts, HEAD_PAD, (COL_B + Q_RANK + KV_RANK + HEAD_PAD) // HEAD_PAD),
                  _full((1, Q_RANK)), _full((1, KV_RANK)), _full((Q_RANK, 2 * hw)), _full((KV_RANK, 2 * hw)),
                  _rows(ts, HEAD_PAD), _rows(ts, HEAD_PAD), _rows(ts, HEAD_PAD)],
        out_specs=[head_spec, head_spec, head_spec],
        out_shape=[head_shape, head_shape, head_shape],
        compiler_params=_params(("parallel",)),
    )(proj, proj, proj, proj, q_norm, kv_norm, wq, wkv, cq_tab, s_tab, cr_tab)


def _diagonal_mask(t):
    return lax.broadcasted_iota(jnp.int32, (t, t), 1) <= lax.broadcasted_iota(jnp.int32, (t, t), 0)


def _attn_fwd(q, k, v):
    s = q.shape[1]
    t = _tile(s, 512)

    def body(q_ref, k_ref, v_ref, o_ref, lse_ref):
        i = pl.program_id(1)
        qv = q_ref[0]

        def step(j, carry, on_diagonal):
            m, l, acc = carry
            kj = k_ref[0, pl.ds(pl.multiple_of(j * t, t), t), :]
            vj = v_ref[0, pl.ds(pl.multiple_of(j * t, t), t), :]
            sc = _dot_nt(qv, kj) * ATTN_SCALE
            if on_diagonal:
                sc = jnp.where(_diagonal_mask(t), sc, -jnp.inf)
            m_new = jnp.maximum(m, jnp.max(sc, axis=1, keepdims=True))
            p = jnp.exp(sc - m_new)
            alpha = jnp.exp(m - m_new)
            return m_new, alpha * l + jnp.sum(p, axis=1, keepdims=True), alpha * acc + _dot(p, vj)

        init = (jnp.full((t, 1), -jnp.inf, F32), jnp.zeros((t, 1), F32), jnp.zeros((t, HEAD_PAD), F32))
        below = lax.fori_loop(0, i, functools.partial(step, on_diagonal=False), init)
        m, l, acc = step(i, below, True)
        o_ref[...] = acc / l
        lse_ref[0] = m + jnp.log(l)

    return pl.pallas_call(
        body, name="attn_fwd", grid=(N_HEADS, s // t),
        in_specs=[pl.BlockSpec((1, t, HEAD_PAD), lambda h, i: (h, i, 0)),
                  pl.BlockSpec((1, s, HEAD_PAD), lambda h, i: (h, 0, 0)),
                  pl.BlockSpec((1, s, HEAD_PAD), lambda h, i: (h, 0, 0))],
        out_specs=[pl.BlockSpec((t, HEAD_PAD), lambda h, i: (i, h)), pl.BlockSpec((1, t, 1), lambda h, i: (h, i, 0))],
        out_shape=[jax.ShapeDtypeStruct((s, N_HEADS * HEAD_PAD), F32), jax.ShapeDtypeStruct((N_HEADS, s, 1), F32)],
        compiler_params=_params(("parallel", "parallel")),
    )(q, k, v)


def _attn_bwd(q, k, v, do, lse, delta):
    s = q.shape[1]
    t = _tile(s, 512)
    nq = s // t

    def body(q_ref, do_ref, lse_ref, dl_ref, k_ref, v_ref, dq_ref, dk_ref, dv_ref):
        j = pl.program_id(1)

        @pl.when(j == 0)
        def _():
            dq_ref[...] = jnp.zeros_like(dq_ref)

        kj, vj = k_ref[0], v_ref[0]

        def step(i, carry, on_diagonal):
            dk, dv = carry
            rows = pl.ds(pl.multiple_of(i * t, t), t)
            qi, doi = q_ref[0, rows, :], do_ref[rows, :]
            sc = _dot_nt(qi, kj) * ATTN_SCALE
            if on_diagonal:
                sc = jnp.where(_diagonal_mask(t), sc, -jnp.inf)
            p = jnp.exp(sc - lse_ref[0, rows, :])
            dv = dv + _dot_tn(p, doi)
            dp = _dot_nt(doi, vj)
            ds = p * (dp - dl_ref[0, rows, :]) * ATTN_SCALE
            dk = dk + _dot_tn(ds, qi)
            dq_ref[0, rows, :] += _dot(ds, kj)
            return dk, dv

        zero = jnp.zeros((t, HEAD_PAD), F32)
        dk, dv = lax.fori_loop(j + 1, nq, functools.partial(step, on_diagonal=False), step(j, (zero, zero), True))
        dk_ref[0] = dk
        dv_ref[0] = dv

    whole = lambda w: pl.BlockSpec((1, s, w), lambda h, j: (h, 0, 0))
    tile = pl.BlockSpec((1, t, HEAD_PAD), lambda h, j: (h, j, 0))
    head_shape = jax.ShapeDtypeStruct((N_HEADS, s, HEAD_PAD), F32)
    return pl.pallas_call(
        body, name="attn_bwd", grid=(N_HEADS, nq),
        in_specs=[whole(HEAD_PAD), pl.BlockSpec((s, HEAD_PAD), lambda h, j: (0, h)), whole(1), whole(1), tile, tile],
        out_specs=[whole(HEAD_PAD), tile, tile],
        out_shape=[head_shape, head_shape, head_shape],
        compiler_params=_params(("parallel", "arbitrary")),
    )(q, do, lse, delta, k, v)


def _qkv_bwd(dq, dk, dv, proj, q_norm, kv_norm, wq, wkv, cq_tab, s_tab, cr_tab):
    s = proj.shape[0]
    ts = _tile(s, 512)
    hw = N_HEADS * HEAD_PAD

    def body(dq_ref, dk_ref, dv_ref, cq_ref, ckv_ref, gq_ref, gkv_ref, wq_ref, wkv_ref, ct_ref, st_ref, crt_ref,
             dpb_ref, dwq_ref, dwkv_ref, dgq_ref, dgkv_ref):
        @pl.when(pl.program_id(0) == 0)
        def _():
            for r in (dwq_ref, dwkv_ref, dgq_ref, dgkv_ref):
                r[...] = jnp.zeros_like(r)

        ct, st, crt = ct_ref[...], st_ref[...], crt_ref[...]
        dqs = [dq_ref[h] for h in range(N_HEADS)]
        dks = [dk_ref[h] for h in range(N_HEADS)]
        dqab = jnp.concatenate([d * ct for d in dqs] + [d * st for d in dqs], axis=1)
        dkav = jnp.concatenate(dks + [dv_ref[h] for h in range(N_HEADS)], axis=1)
        dk_sum = dks[0] + dks[1] + dks[2] + dks[3]
        cq, ckv, gq, gkv = cq_ref[...].astype(F32), ckv_ref[...].astype(F32), gq_ref[...], gkv_ref[...]
        dwq_ref[...] += _dot_tn(_rms(cq, gq), dqab)
        dwkv_ref[...] += _dot_tn(_rms(ckv, gkv), dkav)
        dcq, dgq = _rms_bwd(cq, gq, _dot_nt(dqab, wq_ref[...]))
        dckv, dgkv = _rms_bwd(ckv, gkv, _dot_nt(dkav, wkv_ref[...]))
        dgq_ref[...] += dgq
        dgkv_ref[...] += dgkv
        dpb_ref[...] = jnp.concatenate([dcq, dckv, dk_sum * crt, dk_sum * st], axis=1).astype(dpb_ref.dtype)

    head_spec = pl.BlockSpec((N_HEADS, ts, HEAD_PAD), lambda i: (0, i, 0))
    wb = Q_RANK + KV_RANK + 2 * HEAD_PAD
    return pl.pallas_call(
        body, name="qkv_bwd", grid=(s // ts,),
        in_specs=[head_spec, head_spec, head_spec,
                  _rows(ts, Q_RANK, COL_B // Q_RANK), _rows(ts, KV_RANK, (COL_B + Q_RANK) // KV_RANK),
                  _full((1, Q_RANK)), _full((1, KV_RANK)), _full((Q_RANK, 2 * hw)), _full((KV_RANK, 2 * hw)),
                  _rows(ts, HEAD_PAD), _rows(ts, HEAD_PAD), _rows(ts, HEAD_PAD)],
        out_specs=[_rows(ts, wb), _full((Q_RANK, 2 * hw)), _full((KV_RANK, 2 * hw)), _full((1, Q_RANK)), _full((1, KV_RANK))],
        out_shape=[jax.ShapeDtypeStruct((s, wb), MXU_DTYPE), jax.ShapeDtypeStruct((Q_RANK, 2 * hw), F32),
                   jax.ShapeDtypeStruct((KV_RANK, 2 * hw), F32), jax.ShapeDtypeStruct((1, Q_RANK), F32),
                   jax.ShapeDtypeStruct((1, KV_RANK), F32)],
        compiler_params=_params(("arbitrary",)),
    )(dq, dk, dv, proj, proj, q_norm, kv_norm, wq, wkv, cq_tab, s_tab, cr_tab)


def _lane_group(width):
    return lax.broadcasted_iota(jnp.int32, (1, width), 1) // (width // 4)


def _shift_down(a, k):
    return pltpu.roll(a, k, 0)


def _shift_up(a, k):
    return pltpu.roll(a, a.shape[0] - k, 0)


def _window_sums(xh, shift):
    s2 = xh + shift(xh, 1)
    s4 = s2 + shift(s2, 2)
    s8 = s4 + shift(s4, 4)
    s16 = s8 + shift(s8, 8)
    grp = _lane_group(xh.shape[1])
    return jnp.where(grp == 0, s2, jnp.where(grp == 1, s4, jnp.where(grp == 2, s8, s16)))


def _pool_count(i, ts):
    grp = _lane_group(BR_WIDTH)
    win = jnp.where(grp == 0, 2.0, jnp.where(grp == 1, 4.0, jnp.where(grp == 2, 8.0, 16.0)))
    t = (i * ts + lax.broadcasted_iota(jnp.int32, (ts, 1), 0)).astype(F32)
    return jnp.minimum(t + 1.0, win)


def _mix_forward(i, ts, r):
    f = {}
    act = lambda name: r[name][...].astype(F32)
    f['gates'] = _sigmoid(act('gate') + r['gate_b'][...])
    sgu, sgv = act('sgu'), act('sgv')
    f['sgu'], f['sgv'] = sgu, sgv
    u_act, f['tu'] = _gelu(sgu)
    vg, f['tv'] = _gelu(sgv)
    mu = jnp.mean(vg, axis=-1, keepdims=True)
    xc = vg - mu
    f['ln_r'] = lax.rsqrt(jnp.mean(xc * xc, axis=-1, keepdims=True) + EPS)
    f['ln_xh'] = xc * f['ln_r']
    vln = f['ln_xh'] * r['ln_g'][...] + r['ln_b'][...]
    tril = lax.broadcasted_iota(jnp.int32, (SG_CHUNK, SG_CHUNK), 1) <= lax.broadcasted_iota(jnp.int32, (SG_CHUNK, SG_CHUNK), 0)
    f['wm'] = [_mxu(jnp.where(tril, r['sg_w'][g], 0.0)) for g in range(SG_GROUPS)]
    f['tril'] = tril
    grp = _lane_group(BR_WIDTH)
    bias = r['sg_bias'][...]
    parts = []
    for ci in range(ts // SG_CHUNK):
        vc = vln[ci * SG_CHUNK:(ci + 1) * SG_CHUNK]
        sc = bias
        for g in range(SG_GROUPS):
            sc = sc + jnp.where(grp == g, _dot(f['wm'][g], vc), 0.0)
        parts.append(sc)
    f['vln'] = vln
    f['sg_s'] = parts[0] if len(parts) == 1 else jnp.concatenate(parts, axis=0)
    f['u_act'] = u_act
    out_b = u_act * f['sg_s']
    first = (i > 0).astype(F32)
    cvx, cvc, cvb = act('cvx'), act('cvc'), act('cvb')
    f['cvx'], f['cvc'], f['cvb'] = cvx, cvc, cvb
    zh = jnp.concatenate([act('hx') * act('hc') * first, cvc * cvx], axis=0)
    f['z1'] = _shift_down(zh, 1)[HALO:]
    f['z2'] = _shift_down(zh, 2)[HALO:]
    f['z0'] = zh[HALO:]
    f['yv'] = r['conv_w'][0:1, :] * f['z2'] + r['conv_w'][1:2, :] * f['z1'] + r['conv_w'][2:3, :] * f['z0']
    out_c = cvb * f['yv']
    p = act('pool')
    ph = jnp.concatenate([act('hp') * first, p], axis=0)
    f['cnt'] = _pool_count(i, ts)
    f['pooled'] = _window_sums(ph, _shift_down)[HALO:] / f['cnt'] - p
    f['mixed'] = _dot(f['pooled'], r['wbd'][...])
    out_d = f['mixed'] * r['pool_scale'][...]
    f['outs'] = [r['o'][...], out_b, out_c, out_d]
    f['ys'] = [_dot(f['outs'][b], r['w_br'][b][...]) for b in range(N_BRANCH)]
    merged = f['gates'][:, 0:D_MODEL] * f['ys'][0]
    for b in range(1, N_BRANCH):
        merged = merged + f['gates'][:, b * D_MODEL:(b + 1) * D_MODEL] * f['ys'][b]
    f['merged'] = merged
    f['mo'] = _dot(merged, r['w_out'][...])
    return f


_MIX_TILE_INPUTS = ['gate', 'sgu', 'sgv', 'cvb', 'cvx', 'cvc', 'pool', 'hx', 'hc', 'hp', 'o']
_MIX_WEIGHTS = ['gate_b', 'ln_g', 'ln_b', 'sg_w', 'sg_bias', 'conv_w', 'wbd', 'pool_scale', 'w_br0', 'w_br1', 'w_br2',
                'w_br3', 'w_out', 'g_post']


def _mix_specs(s, ts):
    c0 = COL_M1 // BR_WIDTH
    prev = lambda col: pl.BlockSpec((HALO, BR_WIDTH), lambda i: (jnp.maximum(i * (ts // HALO) - 1, 0), col))
    tiles = [_rows(ts, N_BRANCH * D_MODEL, 0), _rows(ts, BR_WIDTH, c0), _rows(ts, BR_WIDTH, c0 + 1), _rows(ts, BR_WIDTH, c0 + 2),
             _rows(ts, BR_WIDTH, c0 + 3), _rows(ts, BR_WIDTH, c0 + 4), _rows(ts, BR_WIDTH, c0 + 5),
             prev(c0 + 3), prev(c0 + 4), prev(c0 + 5), _rows(ts, N_HEADS * HEAD_PAD)]
    weights = [_full((1, N_BRANCH * D_MODEL)), _full((1, BR_WIDTH)), _full((1, BR_WIDTH)),
               _full((SG_GROUPS, SG_CHUNK, SG_CHUNK)), _full((SG_CHUNK, BR_WIDTH)), _full((8, BR_WIDTH)),
               _resident((BR_WIDTH, BR_WIDTH)), _full((1, BR_WIDTH)), _resident((N_HEADS * HEAD_PAD, D_MODEL)),
               _resident((BR_WIDTH, D_MODEL)), _resident((BR_WIDTH, D_MODEL)), _resident((BR_WIDTH, D_MODEL)),
               _resident((D_MODEL, D_MODEL)), _full((1, D_MODEL))]
    return tiles, weights


def _mix_refs(refs):
    names = _MIX_TILE_INPUTS + _MIX_WEIGHTS
    r = dict(zip(names, refs[:len(names)]))
    r['w_br'] = [r['w_br0'], r['w_br1'], r['w_br2'], r['w_br3']]
    return r, refs[len(names):]


def _mix_operands(proj, o, lw):
    return ([proj] * 10 + [o] + [lw[n] for n in ['gate_b', 'sg_ln_g', 'sg_ln_b', 'sg_w', 'sg_bias', 'conv_w8', 'wbd',
                                                 'pool_scale', 'w_br_mla_p', 'w_br_sg', 'w_br_conv', 'w_br_pool', 'w_out',
                                                 'norm_mix_post']])


def _mix_fwd(x0, proj, o, lw):
    s = x0.shape[0]
    ts = _tile(s, 512)
    tiles, weights = _mix_specs(s, ts)

    def body(*refs):
        r, (x0_ref, x1_ref) = _mix_refs(refs)
        f = _mix_forward(pl.program_id(0), ts, r)
        x1_ref[...] = x0_ref[...] + _rms(f['mo'], r['g_post'][...])

    return pl.pallas_call(
        body, name="mix_fwd", grid=(s // ts,),
        in_specs=tiles + weights + [_rows(ts, D_MODEL)],
        out_specs=_rows(ts, D_MODEL),
        out_shape=jax.ShapeDtypeStruct((s, D_MODEL), F32),
        compiler_params=_params(("parallel",)),
    )(*_mix_operands(proj, o, lw), x0)


def _mix_bwd(dx1, proj, o, lw):
    s = dx1.shape[0]
    ts = _tile(s, 256)
    tiles, weights = _mix_specs(s, ts)
    hw = N_HEADS * HEAD_PAD

    def body(*refs):
        r, rest = _mix_refs(refs)
        (dx1_ref, dg_ref, dm1_ref, dyv_ref, up_ref, do_ref, delta_ref,
         dgate_b_ref, dln_g_ref, dln_b_ref, dsgw_ref, dsgb_ref, dconv_ref, dwbd_ref, dps_ref,
         dwbr0_ref, dwbr1_ref, dwbr2_ref, dwbr3_ref, dwout_ref, dgpost_ref, dbias_acc) = rest
        i = pl.program_id(0)
        acc_refs = [dgate_b_ref, dln_g_ref, dln_b_ref, dsgw_ref, dsgb_ref, dconv_ref, dwbd_ref, dps_ref,
                    dwbr0_ref, dwbr1_ref, dwbr2_ref, dwbr3_ref, dwout_ref, dgpost_ref, dbias_acc]

        @pl.when(i == 0)
        def _():
            for a in acc_refs:
                a[...] = jnp.zeros_like(a)

        f = _mix_forward(i, ts, r)
        dmo, dgpost = _rms_bwd(f['mo'], r['g_post'][...], dx1_ref[...])
        dgpost_ref[...] += dgpost
        dwout_ref[...] += _dot_tn(f['merged'], dmo)
        dmerged = _dot_nt(dmo, r['w_out'][...])
        dwbr = [dwbr0_ref, dwbr1_ref, dwbr2_ref, dwbr3_ref]
        douts = []
        for b in range(N_BRANCH):
            gb = f['gates'][:, b * D_MODEL:(b + 1) * D_MODEL]
            dgate = dmerged * f['ys'][b] * gb * (1.0 - gb)
            dg_ref[:, b * D_MODEL:(b + 1) * D_MODEL] = dgate.astype(dg_ref.dtype)
            dgate_b_ref[:, b * D_MODEL:(b + 1) * D_MODEL] += jnp.sum(dgate, axis=0, keepdims=True)
            dy = dmerged * gb
            dwbr[b][...] += _dot_tn(f['outs'][b], dy)
            douts.append(_dot_nt(dy, r['w_br'][b][...]))
        do = douts[0]
        do_ref[...] = do.astype(do_ref.dtype)
        prod = do * f['outs'][0]
        for h in range(N_HEADS):
            delta_ref[h] = jnp.sum(prod[:, h * HEAD_PAD:(h + 1) * HEAD_PAD], axis=1, keepdims=True)
        grp = _lane_group(BR_WIDTH)
        ds = douts[1] * f['u_act']
        dsgu = douts[1] * f['sg_s'] * _gelu_grad(f['sgu'], f['tu'])
        dvln_parts = []
        for ci in range(ts // SG_CHUNK):
            rows = slice(ci * SG_CHUNK, (ci + 1) * SG_CHUNK)
            ds_c, vln_c = ds[rows], f['vln'][rows]
            dvln_c = jnp.zeros((SG_CHUNK, BR_WIDTH), F32)
            for g in range(SG_GROUPS):
                dvln_c = dvln_c + jnp.where(grp == g, _dot_tn(f['wm'][g], ds_c), 0.0)
                dsgw_ref[g] += jnp.where(f['tril'], _dot_nt(jnp.where(grp == g, ds_c, 0.0), vln_c), 0.0)
            dbias_acc[...] += ds_c
            dvln_parts.append(dvln_c)
        dvln = dvln_parts[0] if len(dvln_parts) == 1 else jnp.concatenate(dvln_parts, axis=0)
        dln_g_ref[...] += jnp.sum(dvln * f['ln_xh'], axis=0, keepdims=True)
        dln_b_ref[...] += jnp.sum(dvln, axis=0, keepdims=True)
        dxh = dvln * r['ln_g'][...]
        dvg = f['ln_r'] * (dxh - jnp.mean(dxh, axis=-1, keepdims=True)
                           - f['ln_xh'] * jnp.mean(dxh * f['ln_xh'], axis=-1, keepdims=True))
        dsgv = dvg * _gelu_grad(f['sgv'], f['tv'])
        dcvb = douts[2] * f['yv']
        dyv = douts[2] * f['cvb']
        dyv_ref[...] = dyv
        for kk, zk in enumerate((f['z2'], f['z1'], f['z0'])):
            dconv_ref[kk:kk + 1, :] += jnp.sum(dyv * zk, axis=0, keepdims=True)
        dps_ref[...] += jnp.sum(douts[3] * f['mixed'], axis=0, keepdims=True)
        dmixed = douts[3] * r['pool_scale'][...]
        dwbd_ref[...] += _dot_tn(f['pooled'], dmixed)
        up_ref[...] = _dot_nt(dmixed, r['wbd'][...]) / f['cnt']
        dm1_ref[...] = jnp.concatenate([dsgu, dsgv, dcvb], axis=1).astype(dm1_ref.dtype)

        @pl.when(i == pl.num_programs(0) - 1)
        def _():
            lane = lax.broadcasted_iota(jnp.int32, (1, SG_CHUNK), 1)
            db = dbias_acc[...]
            out = jnp.zeros((SG_CHUNK, SG_CHUNK), F32)
            for g in range(SG_GROUPS):
                out = out + jnp.where(lane == g, jnp.sum(jnp.where(grp == g, db, 0.0), axis=1, keepdims=True), 0.0)
            dsgb_ref[...] = out

    acc = lambda shape: (_full(shape), jax.ShapeDtypeStruct(shape, F32))
    accs = [acc((1, N_BRANCH * D_MODEL)), acc((1, BR_WIDTH)), acc((1, BR_WIDTH)), acc((SG_GROUPS, SG_CHUNK, SG_CHUNK)),
            acc((SG_CHUNK, SG_CHUNK)), acc((8, BR_WIDTH)), acc((BR_WIDTH, BR_WIDTH)), acc((1, BR_WIDTH)),
            acc((hw, D_MODEL)), acc((BR_WIDTH, D_MODEL)), acc((BR_WIDTH, D_MODEL)), acc((BR_WIDTH, D_MODEL)),
            acc((D_MODEL, D_MODEL)), acc((1, D_MODEL))]
    tile_outs = [(_rows(ts, N_BRANCH * D_MODEL), jax.ShapeDtypeStruct((s, N_BRANCH * D_MODEL), MXU_DTYPE)),
                 (_rows(ts, 3 * BR_WIDTH), jax.ShapeDtypeStruct((s, 3 * BR_WIDTH), MXU_DTYPE)),
                 (_rows(ts, BR_WIDTH), jax.ShapeDtypeStruct((s, BR_WIDTH), F32)),
                 (_rows(ts, BR_WIDTH), jax.ShapeDtypeStruct((s, BR_WIDTH), F32)),
                 (_rows(ts, hw), jax.ShapeDtypeStruct((s, hw), MXU_DTYPE)),
                 (pl.BlockSpec((N_HEADS, ts, 1), lambda i: (0, i, 0)), jax.ShapeDtypeStruct((N_HEADS, s, 1), F32))]
    outs = tile_outs + accs
    return pl.pallas_call(
        body, name="mix_bwd", grid=(s // ts,),
        in_specs=tiles + weights + [_rows(ts, D_MODEL)],
        out_specs=[o_[0] for o_ in outs], out_shape=[o_[1] for o_ in outs],
        scratch_shapes=[pltpu.VMEM((SG_CHUNK, BR_WIDTH), F32)],
        compiler_params=_params(("arbitrary",), 60 * 1024 * 1024),
    )(*_mix_operands(proj, o, lw), dx1)


def _shift_bwd(dyv, upool, proj, conv_w8):
    s = dyv.shape[0]
    ts = _tile(s, 512)
    nb = s // HALO
    c0 = COL_M1 // BR_WIDTH

    def body(dyv_ref, dyvn_ref, up_ref, upn_ref, cvx_ref, cvc_ref, cw_ref, out_ref):
        i = pl.program_id(0)
        last = (i < pl.num_programs(0) - 1).astype(F32)
        dh = jnp.concatenate([dyv_ref[...], dyvn_ref[...] * last], axis=0)
        dz = (cw_ref[2:3, :] * dh + cw_ref[1:2, :] * _shift_up(dh, 1) + cw_ref[0:1, :] * _shift_up(dh, 2))[:ts]
        up = up_ref[...]
        uh = jnp.concatenate([up, upn_ref[...] * last], axis=0)
        dpool = _window_sums(uh, _shift_up)[:ts] - up * _pool_count(i, ts)
        out_ref[...] = jnp.concatenate([dz * cvc_ref[...].astype(F32), dz * cvx_ref[...].astype(F32), dpool],
                                       axis=1).astype(out_ref.dtype)

    nxt = pl.BlockSpec((HALO, BR_WIDTH), lambda i: (jnp.minimum((i + 1) * (ts // HALO), nb - 1), 0))
    return pl.pallas_call(
        body, name="shift_bwd", grid=(s // ts,),
        in_specs=[_rows(ts, BR_WIDTH), nxt, _rows(ts, BR_WIDTH), nxt, _rows(ts, BR_WIDTH, c0 + 3), _rows(ts, BR_WIDTH, c0 + 4),
                  _full((8, BR_WIDTH))],
        out_specs=_rows(ts, 3 * BR_WIDTH),
        out_shape=jax.ShapeDtypeStruct((s, 3 * BR_WIDTH), MXU_DTYPE),
        compiler_params=_params(("parallel",)),
    )(dyv, dyv, upool, upool, proj, proj, conv_w8)


def _ffn2(a, w2, x1, g):
    s = a.shape[0]
    ts = _tile(s, 512)

    def body(a_ref, w_ref, x1_ref, g_ref, x2_ref, f_ref):
        f = _dot(_relu_sq(a_ref[...]), w_ref[...])
        f_ref[...] = f
        x2_ref[...] = x1_ref[...] + _rms(f, g_ref[...])

    return pl.pallas_call(
        body, name="ffn2", grid=(s // ts,),
        in_specs=[_rows(ts, D_FF), _resident((D_FF, D_MODEL)), _rows(ts, D_MODEL), _full((1, D_MODEL))],
        out_specs=[_rows(ts, D_MODEL), _rows(ts, D_MODEL)],
        out_shape=[jax.ShapeDtypeStruct((s, D_MODEL), F32)] * 2,
        compiler_params=_params(("parallel",)),
    )(a, w2, x1, g)


def _ffn2_bwd(dx2, f, g, a, w2t):
    s = a.shape[0]
    ts = _tile(s, 512)

    def body(dx2_ref, f_ref, g_ref, a_ref, w_ref, df_ref, da_ref, dg_ref):
        @pl.when(pl.program_id(0) == 0)
        def _():
            dg_ref[...] = jnp.zeros_like(dg_ref)

        df, dg = _rms_bwd(f_ref[...], g_ref[...], dx2_ref[...])
        dg_ref[...] += dg
        df_ref[...] = df.astype(df_ref.dtype)
        da_ref[...] = (_dot(df, w_ref[...]) * (2.0 * jnp.maximum(a_ref[...].astype(F32), 0.0))).astype(da_ref.dtype)

    return pl.pallas_call(
        body, name="ffn2_bwd", grid=(s // ts,),
        in_specs=[_rows(ts, D_MODEL), _rows(ts, D_MODEL), _full((1, D_MODEL)), _rows(ts, D_FF), _resident((D_MODEL, D_FF))],
        out_specs=[_rows(ts, D_MODEL), _rows(ts, D_FF), _full((1, D_MODEL))],
        out_shape=[jax.ShapeDtypeStruct((s, D_MODEL), MXU_DTYPE), jax.ShapeDtypeStruct((s, D_FF), MXU_DTYPE),
                   jax.ShapeDtypeStruct((1, D_MODEL), F32)],
        compiler_params=_params(("arbitrary",)),
    )(dx2, f, g, a, w2t)


def _norm_in_bwd(name, pieces, x, g, dres):
    s = x.shape[0]
    ts = _tile(s, 512)
    n = len(pieces)

    def body(*refs):
        d_refs, w_refs = refs[:n], refs[n:2 * n]
        x_ref, g_ref, dres_ref, dx_ref, dg_ref = refs[2 * n:]

        @pl.when(pl.program_id(0) == 0)
        def _():
            dg_ref[...] = jnp.zeros_like(dg_ref)

        dh = _dot(d_refs[0][...], w_refs[0][...])
        for p in range(1, n):
            dh = dh + _dot(d_refs[p][...], w_refs[p][...])
        dx, dg = _rms_bwd(x_ref[...], g_ref[...], dh)
        dg_ref[...] += dg
        dx_ref[...] = dres_ref[...] + dx

    return pl.pallas_call(
        body, name=name, grid=(s // ts,),
        in_specs=[_rows(ts, d.shape[1]) for d, _ in pieces] + [_resident(w.shape) for _, w in pieces]
        + [_rows(ts, D_MODEL), _full((1, D_MODEL)), _rows(ts, D_MODEL)],
        out_specs=[_rows(ts, D_MODEL), _full((1, D_MODEL))],
        out_shape=[jax.ShapeDtypeStruct((s, D_MODEL), F32), jax.ShapeDtypeStruct((1, D_MODEL), F32)],
        compiler_params=_params(("arbitrary",)),
    )(*[d for d, _ in pieces], *[w for _, w in pieces], x, g, dres)


def _loss_and_grad(y, target):
    s = y.shape[0]
    ts = _tile(s, 512)

    def body(y_ref, t_ref, dy_ref, loss_ref):
        @pl.when(pl.program_id(0) == 0)
        def _():
            loss_ref[...] = jnp.zeros_like(loss_ref)

        err = y_ref[...] - t_ref[...]
        dy_ref[...] = err * (1.0 / D_MODEL)
        loss_ref[...] += 0.5 * jnp.sum(jnp.mean(err * err, axis=-1, keepdims=True), axis=0, keepdims=True)

    dy, loss = pl.pallas_call(
        body, name="loss", grid=(s // ts,),
        in_specs=[_rows(ts, D_MODEL), _rows(ts, D_MODEL)],
        out_specs=[_rows(ts, D_MODEL), _full((8, 128))],
        out_shape=[jax.ShapeDtypeStruct((s, D_MODEL), F32), jax.ShapeDtypeStruct((8, 128), F32)],
        compiler_params=_params(("arbitrary",)),
    )(y, target)
    return loss[0, 0], dy


_W_IN_SPLITS = [256, 384, 416, 672, 928, 1184, 1440, 1696, 1952]


def _rope_swap(w):
    half = QK_ROPE // 2
    return jnp.concatenate([-w[..., half:], w[..., :half]], axis=-1)


def _rope_unswap(d):
    half = QK_ROPE // 2
    return jnp.concatenate([d[..., half:], -d[..., :half]], axis=-1)


def _zeros_like_cols(w, n):
    return jnp.zeros(w.shape[:-1] + (n,), w.dtype)


def _derive_weights(w):
    md = MXU_DTYPE
    nl = w['w_in'].shape[0]
    c_q, c_kv, k_r, sg_u, sg_v, cv_x, cv_b, cv_c, pool, gate = jnp.split(w['w_in'].astype(md), _W_IN_SPLITS, axis=-1)
    pad_rope = lambda r: jnp.concatenate([_zeros_like_cols(r, QK_NOPE), r, _zeros_like_cols(r, HEAD_PAD - QK_NOPE - QK_ROPE)], -1)
    w_in_p = jnp.concatenate([gate, sg_u, sg_v, cv_b, cv_x, cv_c, pool, c_q, c_kv, pad_rope(k_r), pad_rope(_rope_swap(k_r))], -1)
    wq = w['w_uq'].astype(md).reshape(nl, Q_RANK, N_HEADS, QK_NOPE + QK_ROPE)
    nope, rope_w = wq[..., :QK_NOPE], wq[..., QK_NOPE:]
    wq_a = jnp.concatenate([nope, rope_w, _zeros_like_cols(nope, 32)], -1).reshape(nl, Q_RANK, N_HEADS * HEAD_PAD)
    wq_b = pad_rope(_rope_swap(rope_w)).reshape(nl, Q_RANK, N_HEADS * HEAD_PAD)
    wkv = w['w_ukv'].astype(md).reshape(nl, KV_RANK, N_HEADS, QK_NOPE + V_HEAD)
    pad_half = lambda r: jnp.concatenate([r, _zeros_like_cols(r, HEAD_PAD - r.shape[-1])], -1).reshape(nl, KV_RANK, N_HEADS * HEAD_PAD)
    w_br_mla = w['w_br_mla'].astype(md).reshape(nl, N_HEADS, V_HEAD, D_MODEL)
    w_br_mla_p = jnp.concatenate([w_br_mla, jnp.zeros_like(w_br_mla)], axis=2).reshape(nl, N_HEADS * HEAD_PAD, D_MODEL)
    eye = jnp.eye(4, dtype=md)
    wbd = (w['pool_w'].astype(md)[:, :, :, None, :] * eye[None, :, None, :, None]).reshape(nl, BR_WIDTH, BR_WIDTH)
    row = lambda a: a.astype(F32)[:, None, :]
    w_in_pt = jnp.swapaxes(w_in_p, 1, 2)
    return dict(
        w_in_p=w_in_p, wt_g=w_in_pt[:, COL_G:COL_M1], wt_m1=w_in_pt[:, COL_M1:COL_M2], wt_m2=w_in_pt[:, COL_M2:COL_B],
        wt_b=w_in_pt[:, COL_B:],
        wq=jnp.concatenate([wq_a, wq_b], -1), wkv=jnp.concatenate([pad_half(wkv[..., :QK_NOPE]), pad_half(wkv[..., QK_NOPE:])], -1),
        w_br_mla_p=w_br_mla_p, w_br_sg=w['w_br_sg'].astype(md), w_br_conv=w['w_br_conv'].astype(md),
        w_br_pool=w['w_br_pool'].astype(md), wbd=wbd, w_out=w['w_out'].astype(md),
        w_ff1=w['w_ff1'].astype(md), w_ff1t=jnp.swapaxes(w['w_ff1'].astype(md), 1, 2),
        w_ff2=w['w_ff2'].astype(md), w_ff2t=jnp.swapaxes(w['w_ff2'].astype(md), 1, 2),
        norm_mix_pre=row(w['norm_mix_pre']), gate_b=row(w['gate_b']), q_norm=row(w['q_norm']), kv_norm=row(w['kv_norm']),
        sg_ln_g=row(w['sg_ln_g']), sg_ln_b=row(w['sg_ln_b']), sg_w=w['sg_w'].astype(F32),
        sg_bias=jnp.repeat(jnp.swapaxes(w['sg_b'].astype(F32), 1, 2), BR_WIDTH // SG_GROUPS, axis=2),
        conv_w8=jnp.pad(w['conv_w'].astype(F32), ((0, 0), (0, 5), (0, 0))), pool_scale=row(w['pool_scale']),
        norm_mix_post=row(w['norm_mix_post']), norm_ffn_pre=row(w['norm_ffn_pre']), norm_ffn_post=row(w['norm_ffn_post']),
    )


def _rope_tables(positions):
    inv_freq = ROPE_BASE ** (-jnp.arange(0, QK_ROPE, 2, dtype=F32) / QK_ROPE)
    ang = positions.astype(F32)[:, None] * inv_freq
    cos, sin = jnp.cos(ang), jnp.sin(ang)
    n = positions.shape[0]
    ones, z64, z32 = jnp.ones((n, QK_NOPE), F32), jnp.zeros((n, QK_NOPE), F32), jnp.zeros((n, 32), F32)
    return (jnp.concatenate([ones, cos, cos, z32], 1), jnp.concatenate([z64, sin, sin, z32], 1),
            jnp.concatenate([z64, cos, cos, z32], 1))


def _reference_layout_grads(g):
    gate, dm1, dm2, dpb = g['dw_in_pieces']
    nl = gate.shape[0]
    sg_u, sg_v, cv_b = jnp.split(dm1, 3, axis=-1)
    cv_x, cv_c, pool = jnp.split(dm2, 3, axis=-1)
    c_q, c_kv, kr, krs = jnp.split(dpb, [Q_RANK, Q_RANK + KV_RANK, Q_RANK + KV_RANK + HEAD_PAD], axis=-1)
    rope_cols = slice(QK_NOPE, QK_NOPE + QK_ROPE)
    k_r = kr[..., rope_cols] + _rope_unswap(krs[..., rope_cols])
    w_in = jnp.concatenate([c_q, c_kv, k_r, sg_u, sg_v, cv_x, cv_b, cv_c, pool, gate], -1)
    hw = N_HEADS * HEAD_PAD
    dqa = g['dwq'][..., :hw].reshape(nl, Q_RANK, N_HEADS, HEAD_PAD)
    dqb = g['dwq'][..., hw:].reshape(nl, Q_RANK, N_HEADS, HEAD_PAD)
    w_uq = jnp.concatenate([dqa[..., :QK_NOPE], dqa[..., rope_cols] + _rope_unswap(dqb[..., rope_cols])], -1)
    dka = g['dwkv'][..., :hw].reshape(nl, KV_RANK, N_HEADS, HEAD_PAD)
    dva = g['dwkv'][..., hw:].reshape(nl, KV_RANK, N_HEADS, HEAD_PAD)
    w_ukv = jnp.concatenate([dka[..., :QK_NOPE], dva[..., :V_HEAD]], -1)
    w_br_mla = g['dw_br_mla_p'].reshape(nl, N_HEADS, HEAD_PAD, D_MODEL)[:, :, :V_HEAD]
    dwbd = g['dwbd'].reshape(nl, 4, 64, 4, 64)
    pool_w = jnp.stack([dwbd[:, k, :, k, :] for k in range(4)], axis=1)
    sq = lambda a: a[:, 0, :]
    return dict(
        norm_mix_pre=sq(g['dg_pre']), w_in=w_in, gate_b=sq(g['dgate_b']), q_norm=sq(g['dq_norm']),
        w_uq=w_uq.reshape(nl, Q_RANK, -1), kv_norm=sq(g['dkv_norm']), w_ukv=w_ukv.reshape(nl, KV_RANK, -1),
        w_br_mla=w_br_mla.reshape(nl, N_HEADS * V_HEAD, D_MODEL), sg_ln_g=sq(g['dln_g']), sg_ln_b=sq(g['dln_b']),
        sg_w=g['dsg_w'], sg_b=jnp.swapaxes(g['dsg_b'][:, :, :SG_GROUPS], 1, 2), w_br_sg=g['dw_br_sg'],
        conv_w=g['dconv_w'][:, :3], w_br_conv=g['dw_br_conv'], pool_w=pool_w, pool_scale=sq(g['dpool_scale']),
        w_br_pool=g['dw_br_pool'], w_out=g['dw_out'], norm_mix_post=sq(g['dg_post']), norm_ffn_pre=sq(g['dg_fpre']),
        w_ff1=g['dw_ff1'], w_ff2=g['dw_ff2'], norm_ffn_post=sq(g['dg_fpost']))


def _layer_forward(x0, lw, tabs):
    proj = _mm("in_proj", x0, lw['w_in_p'], tm=1024, tn=896, prologue=_rms, rows=(lw['norm_mix_pre'],))
    q, k, v = _qkv_prep(proj, lw['q_norm'], lw['kv_norm'], lw['wq'], lw['wkv'], *tabs)
    o, lse = _attn_fwd(q, k, v)
    x1 = _mix_fwd(x0, proj, o, lw)
    a = _mm("ffn1", x1, lw['w_ff1'], tm=1024, tn=1024, prologue=_rms, rows=(lw['norm_ffn_pre'],))
    x2, f = _ffn2(a, lw['w_ff2'], x1, lw['norm_ffn_post'])
    return x2, dict(x0=x0, proj=proj, q=q, k=k, v=v, o=o, lse=lse, x1=x1, a=a, f=f)


def _layer_backward(dx2, lw, sv, tabs):
    g = {}
    df, da, g['dg_fpost'] = _ffn2_bwd(dx2, sv['f'], lw['norm_ffn_post'], sv['a'], lw['w_ff2t'])
    g['dw_ff2'] = _mm_tn("dw_ff2", sv['a'], df, tm=512, tn=512, prologue=_relu_sq)
    dx1, g['dg_fpre'] = _norm_in_bwd("ffn1_bwd", [(da, lw['w_ff1t'])], sv['x1'], lw['norm_ffn_pre'], dx2)
    g['dw_ff1'] = _mm_tn("dw_ff1", sv['x1'], da, tm=512, tn=1024, prologue=_rms, rows=(lw['norm_ffn_pre'],))
    (dgate, dm1, dyv, upool, do, delta, g['dgate_b'], g['dln_g'], g['dln_b'], g['dsg_w'], g['dsg_b'], g['dconv_w'],
     g['dwbd'], g['dpool_scale'], g['dw_br_mla_p'], g['dw_br_sg'], g['dw_br_conv'], g['dw_br_pool'], g['dw_out'],
     g['dg_post']) = _mix_bwd(dx1, sv['proj'], sv['o'], lw)
    dm2 = _shift_bwd(dyv, upool, sv['proj'], lw['conv_w8'])
    dq, dk, dv = _attn_bwd(sv['q'], sv['k'], sv['v'], do, sv['lse'], delta)
    dpb, g['dwq'], g['dwkv'], g['dq_norm'], g['dkv_norm'] = _qkv_bwd(
        dq, dk, dv, sv['proj'], lw['q_norm'], lw['kv_norm'], lw['wq'], lw['wkv'], *tabs)
    pieces = [(dgate, lw['wt_g']), (dm1, lw['wt_m1']), (dm2, lw['wt_m2']), (dpb, lw['wt_b'])]
    dx0, g['dg_pre'] = _norm_in_bwd("in_proj_bwd", pieces, sv['x0'], lw['norm_mix_pre'], dx1)
    g['dw_in_pieces'] = [_mm_tn("dw_in_%d" % n, sv['x0'], d, tm=512, tn=1024, prologue=_rms, rows=(lw['norm_mix_pre'],))
                         for n, (d, _) in enumerate(pieces)]
    return dx0, g


def _local_step(x, positions, target, layers, after_layer_backward):
    tabs = _rope_tables(positions)
    derived = [{n: a[0] for n, a in _derive_weights(w).items()} for w in layers]
    saved = []
    for lw in derived:
        x, sv = _layer_forward(x, lw, tabs)
        saved.append(sv)
    loss, dx = _loss_and_grad(x, target)
    for l in reversed(range(len(layers))):
        dx, g = _layer_backward(dx, derived[l], saved[l], tabs)
        lead = lambda a: [b[None] for b in a] if isinstance(a, list) else a[None]
        after_layer_backward(l, _reference_layout_grads({n: lead(a) for n, a in g.items()}))
    return loss, dx


HBM = pl.BlockSpec(memory_space=pl.ANY)


def _relative_peers():
    x, y = lax.axis_index("x"), lax.axis_index("y")
    return {1: (x, 1 - y), 2: (1 - x, y), 3: (1 - x, 1 - y)}


def _gather_chips(name, split, whole):
    ns, nw = len(split), len(whole)
    n = ns + nw

    def body(*refs):
        ins, outs = refs[:n], refs[n:2 * n]
        ici_send, ici_recv, d2d_send, d2d_recv, own_send, own_recv = refs[2 * n:]
        x, y, c = lax.axis_index("x"), lax.axis_index("y"), lax.axis_index("c")
        peers = _relative_peers()

        def run(half):
            def rows(ref, which):
                h = ref.shape[-2] // 2
                return ref.at[(slice(None),) * (len(ref.shape) - 2) + (slice(which * h, (which + 1) * h), slice(None))]

            started = []
            for k in range(n):
                own = pltpu.make_async_remote_copy(src_ref=ins[k], dst_ref=outs[k].at[0], send_sem=own_send.at[k],
                                                   recv_sem=own_recv.at[k], device_id=(x, y, 1 - c), device_id_type=MESH)
                own.start()
                started.append(own.wait)
                for r, (px, py) in peers.items():
                    src = rows(ins[k], half) if k < ns else ins[k]
                    dst = rows(outs[k].at[r], half) if k < ns else outs[k].at[r]
                    cp = pltpu.make_async_remote_copy(src_ref=src, dst_ref=dst, send_sem=ici_send.at[3 * k + r - 1],
                                                      recv_sem=ici_recv.at[3 * k + r - 1], device_id=(px, py, c),
                                                      device_id_type=MESH)
                    cp.start()
                    started.append(cp.wait_send)
            for k in range(n):
                for r, (px, py) in peers.items():
                    landed = rows(outs[k].at[r], half) if k < ns else outs[k].at[r]
                    pltpu.make_async_remote_copy(src_ref=landed, dst_ref=landed, send_sem=ici_send.at[3 * k + r - 1],
                                                 recv_sem=ici_recv.at[3 * k + r - 1], device_id=(px, py, c),
                                                 device_id_type=MESH).wait_recv()
                    if k < ns:
                        fwd = pltpu.make_async_remote_copy(src_ref=landed, dst_ref=landed, send_sem=d2d_send.at[3 * k + r - 1],
                                                           recv_sem=d2d_recv.at[3 * k + r - 1], device_id=(x, y, 1 - c),
                                                           device_id_type=MESH)
                        fwd.start()
                        started.append(fwd.wait_send)
            for k in range(ns):
                for r in peers:
                    other = rows(outs[k].at[r], 1 - half)
                    pltpu.make_async_remote_copy(src_ref=other, dst_ref=other, send_sem=d2d_send.at[3 * k + r - 1],
                                                 recv_sem=d2d_recv.at[3 * k + r - 1], device_id=(x, y, 1 - c),
                                                 device_id_type=MESH).wait_recv()
            for wait in started:
                wait()

        for half in (0, 1):
            pl.when(c == half)(functools.partial(run, half))

    arrs = list(split) + list(whole)
    return pl.pallas_call(
        body, name=name,
        in_specs=[HBM] * n, out_specs=[HBM] * n,
        out_shape=[jax.ShapeDtypeStruct((4,) + a.shape, a.dtype) for a in arrs],
        scratch_shapes=[pltpu.SemaphoreType.DMA((3 * n,)), pltpu.SemaphoreType.DMA((3 * n,)), pltpu.SemaphoreType.DMA((3 * ns,)),
                        pltpu.SemaphoreType.DMA((3 * ns,)), pltpu.SemaphoreType.DMA((n,)), pltpu.SemaphoreType.DMA((n,))],
    )(*arrs)


def _absolute_chip_order(relative):
    me = 2 * lax.axis_index("x") + lax.axis_index("y")
    return jnp.stack([lax.dynamic_index_in_dim(relative, jnp.bitwise_xor(me, chip), 0, keepdims=False) for chip in range(4)])


REDUCE_STEPS = 8


def _sibling_halves(name, arrs):
    n = len(arrs)

    def body(*refs):
        ins, theirs = refs[:n], refs[n:2 * n]
        send_sems, recv_sems = refs[2 * n:]
        x, y, c = lax.axis_index("x"), lax.axis_index("y"), lax.axis_index("c")

        def exchange(my_half):
            copies = []
            for k in range(n):
                h = ins[k].shape[1] // 2
                cp = pltpu.make_async_remote_copy(src_ref=ins[k].at[:, (1 - my_half) * h:(2 - my_half) * h, :], dst_ref=theirs[k],
                                                  send_sem=send_sems.at[k], recv_sem=recv_sems.at[k],
                                                  device_id=(x, y, 1 - c), device_id_type=MESH)
                cp.start()
                copies.append(cp)
            for cp in copies:
                cp.wait()

        for half in (0, 1):
            pl.when(c == half)(functools.partial(exchange, half))

    return pl.pallas_call(
        body, name=name, in_specs=[HBM] * n, out_specs=[HBM] * n,
        out_shape=[jax.ShapeDtypeStruct((a.shape[0], a.shape[1] // 2, a.shape[2]), a.dtype) for a in arrs],
        scratch_shapes=[pltpu.SemaphoreType.DMA((n,)), pltpu.SemaphoreType.DMA((n,))],
    )(*arrs)


def _add_sibling(name, arrs, theirs):
    n, steps = len(arrs), REDUCE_STEPS

    def body(*refs):
        for mine_ref, theirs_ref, out_ref in zip(refs[:n], refs[n:2 * n], refs[2 * n:]):
            out_ref[...] = (mine_ref[...] + theirs_ref[...]).astype(out_ref.dtype)

    block = lambda t: (4, t.shape[1] // steps, t.shape[2])
    return pl.pallas_call(
        body, name=name, grid=(steps,),
        in_specs=[pl.BlockSpec(block(t), lambda i: (0, lax.axis_index("c") * steps + i, 0)) for t in theirs]
        + [pl.BlockSpec(block(t), lambda i: (0, i, 0)) for t in theirs],
        out_specs=[pl.BlockSpec(block(t), lambda i: (0, i, 0)) for t in theirs],
        out_shape=[jax.ShapeDtypeStruct(t.shape, WIRE_DTYPE) for t in theirs],
        compiler_params=_params(("parallel",)))(*arrs, *theirs)


def _scatter_chips(name, arrs):
    n = len(arrs)

    def body(*refs):
        ins, outs = refs[:n], refs[n:2 * n]
        send_sems, recv_sems = refs[2 * n:]
        c = lax.axis_index("c")
        peers = _relative_peers()
        sends = []
        for k in range(n):
            for r, (px, py) in peers.items():
                cp = pltpu.make_async_remote_copy(src_ref=ins[k].at[2 * px + py], dst_ref=outs[k].at[r - 1],
                                                  send_sem=send_sems.at[3 * k + r - 1], recv_sem=recv_sems.at[3 * k + r - 1],
                                                  device_id=(px, py, c), device_id_type=MESH)
                cp.start()
                sends.append(cp)
        for k in range(n):
            for r, (px, py) in peers.items():
                pltpu.make_async_remote_copy(src_ref=ins[k].at[0], dst_ref=outs[k].at[r - 1],
                                             send_sem=send_sems.at[3 * k + r - 1], recv_sem=recv_sems.at[3 * k + r - 1],
                                             device_id=(px, py, c), device_id_type=MESH).wait_recv()
        for cp in sends:
            cp.wait_send()

    return pl.pallas_call(
        body, name=name, in_specs=[HBM] * n, out_specs=[HBM] * n,
        out_shape=[jax.ShapeDtypeStruct((3,) + a.shape[1:], a.dtype) for a in arrs],
        scratch_shapes=[pltpu.SemaphoreType.DMA((3 * n,)), pltpu.SemaphoreType.DMA((3 * n,))],
    )(*arrs)


def _sum_chips(name, chip_sums, arrived):
    n, steps = len(chip_sums), REDUCE_STEPS

    def body(*refs):
        for own_ref, arrived_ref, out_ref in zip(refs[:n], refs[n:2 * n], refs[2 * n:]):
            acc = own_ref[...].astype(F32)
            for r in range(3):
                acc = acc + arrived_ref[r].astype(F32)
            out_ref[...] = acc

    rows = lambda s: s.shape[1] // steps
    chip = lambda: 2 * lax.axis_index("x") + lax.axis_index("y")
    return pl.pallas_call(
        body, name=name, grid=(steps,),
        in_specs=[pl.BlockSpec((None, rows(s), s.shape[2]), lambda i: (chip(), i, 0)) for s in chip_sums]
        + [pl.BlockSpec((3, rows(s), s.shape[2]), lambda i: (0, i, 0)) for s in chip_sums],
        out_specs=[pl.BlockSpec((rows(s), s.shape[2]), lambda i: (lax.axis_index("c") * steps + i, 0)) for s in chip_sums],
        out_shape=[jax.ShapeDtypeStruct((2 * s.shape[1], s.shape[2]), F32) for s in chip_sums],
        compiler_params=_params(("parallel",)))(*chip_sums, *arrived)


def _join_siblings(name, bufs):
    n = len(bufs)

    def body(*refs):
        outs = refs[n:2 * n]
        send_sems, recv_sems = refs[2 * n:]
        x, y, c = lax.axis_index("x"), lax.axis_index("y"), lax.axis_index("c")

        def exchange(my_half):
            copies = []
            for k in range(n):
                h = outs[k].shape[0] // 2
                mine = outs[k].at[my_half * h:(my_half + 1) * h, :]
                theirs = outs[k].at[(1 - my_half) * h:(2 - my_half) * h, :]
                cp = pltpu.make_async_remote_copy(src_ref=mine, dst_ref=mine, send_sem=send_sems.at[k],
                                                  recv_sem=recv_sems.at[k], device_id=(x, y, 1 - c), device_id_type=MESH)
                cp.start()
                arrival = pltpu.make_async_remote_copy(src_ref=theirs, dst_ref=theirs, send_sem=send_sems.at[k],
                                                       recv_sem=recv_sems.at[k], device_id=(x, y, 1 - c), device_id_type=MESH)
                copies.append((cp, arrival))
            for cp, arrival in copies:
                arrival.wait_recv()
                cp.wait_send()

        for half in (0, 1):
            pl.when(c == half)(functools.partial(exchange, half))

    return pl.pallas_call(
        body, name=name, in_specs=[HBM] * n, out_specs=[HBM] * n,
        out_shape=[jax.ShapeDtypeStruct(b.shape, b.dtype) for b in bufs], input_output_aliases={k: k for k in range(n)},
        scratch_shapes=[pltpu.SemaphoreType.DMA((n,)), pltpu.SemaphoreType.DMA((n,))],
    )(*bufs)


def _gather_all(name, a):
    def body(a_ref, out_ref, staging, send_sems, recv_sems, local_sem):
        x, y, c = lax.axis_index("x"), lax.axis_index("y"), lax.axis_index("c")
        me = 4 * x + 2 * y + c
        flips = [(fx, fy, fc) for fx in (0, 1) for fy in (0, 1) for fc in (0, 1)][1:]
        peers = [(x ^ fx, y ^ fy, c ^ fc) for fx, fy, fc in flips]
        load = pltpu.make_async_copy(a_ref, staging, local_sem)
        load.start()
        load.wait()
        local = pltpu.make_async_copy(staging, out_ref.at[me], local_sem)
        local.start()
        sends = []
        for j, peer in enumerate(peers):
            cp = pltpu.make_async_remote_copy(src_ref=a_ref, dst_ref=out_ref.at[me], send_sem=send_sems.at[j],
                                              recv_sem=recv_sems.at[j], device_id=peer, device_id_type=MESH)
            cp.start()
            sends.append(cp)
        for j, (px, py, pc) in enumerate(peers):
            pltpu.make_async_remote_copy(src_ref=a_ref, dst_ref=out_ref.at[4 * px + 2 * py + pc], send_sem=send_sems.at[j],
                                         recv_sem=recv_sems.at[j], device_id=(px, py, pc), device_id_type=MESH).wait_recv()
        for cp in sends:
            cp.wait_send()
        local.wait()

    return pl.pallas_call(
        body, name=name, in_specs=[HBM], out_specs=HBM, out_shape=jax.ShapeDtypeStruct((8,) + a.shape, a.dtype),
        scratch_shapes=[pltpu.VMEM(a.shape, a.dtype), pltpu.SemaphoreType.DMA((7,)), pltpu.SemaphoreType.DMA((7,)),
                        pltpu.SemaphoreType.DMA],
    )(a)


def _rowwise_call(name, fn, slots, out_shapes, steps):
    n_in, n_out = [len(s) for s in slots], [len(o) for o in out_shapes]

    def spec(shape):
        if len(shape) == 3:
            return pl.BlockSpec((shape[0], shape[1] // steps, shape[2]), lambda i: (0, i, 0))
        return pl.BlockSpec((shape[0] // steps, shape[1]), lambda i: (i, 0))

    def body(*refs):
        ins, outs = refs[:sum(n_in)], refs[sum(n_in):]
        a = b = 0
        for k in range(len(slots)):
            for o_ref, val in zip(outs[b:b + n_out[k]], fn(*[r[...] for r in ins[a:a + n_in[k]]])):
                o_ref[...] = val
            a, b = a + n_in[k], b + n_out[k]

    flat_in = [arr for s in slots for arr in s]
    flat_out = [shp for o in out_shapes for shp in o]
    out = pl.pallas_call(body, name=name, grid=(steps,), in_specs=[spec(a.shape) for a in flat_in],
                         out_specs=[spec(s) for s in flat_out], out_shape=[jax.ShapeDtypeStruct(s, F32) for s in flat_out],
                         compiler_params=_params(("parallel",)))(*flat_in)
    grouped, b = [], 0
    for k in range(len(slots)):
        grouped.append(out[b:b + n_out[k]])
        b += n_out[k]
    return grouped


def _sum_in_order(a):
    acc = a[0].astype(F32)
    for k in range(1, a.shape[0]):
        acc = acc + a[k].astype(F32)
    return (acc,)


def _adamw_math(w, g, m, v):
    m_new = ADAM_B1 * m + (1.0 - ADAM_B1) * g
    v_new = ADAM_B2 * v + (1.0 - ADAM_B2) * (g * g)
    m_hat = m_new / (1.0 - ADAM_B1 ** ADAM_STEP)
    v_hat = v_new / (1.0 - ADAM_B2 ** ADAM_STEP)
    return -ADAM_LR * (m_hat / (jnp.sqrt(v_hat) + ADAM_EPS) + ADAM_WD * w), m_new, v_new


SMALL_PACK_COLS = 256
SMALL_PACK_ROWS = 2048


def _pack_small(parts):
    wide = [jnp.pad(p, ((0, 0), (0, 0), (0, SMALL_PACK_COLS - p.shape[2]))) for p in parts]
    rows = jnp.concatenate(wide, axis=1)
    return jnp.pad(rows, ((0, 0), (0, SMALL_PACK_ROWS - rows.shape[1]), (0, 0)))


def _unpack_small(packed, shapes):
    out, row = [], 0
    for a, b in shapes:
        out.append(packed[:, row:row + a, :b])
        row += a
    return out


def _pack(arrs, rows_per_layer, dtype):
    nl = arrs[0].shape[0]
    flat = jnp.concatenate([a.astype(dtype).reshape(nl, -1) for a in arrs], axis=1)
    flat = jnp.pad(flat, ((0, 0), (0, rows_per_layer * PACK_COLS - flat.shape[1])))
    return flat.reshape(nl * rows_per_layer, PACK_COLS)


def _unpack(packed, shapes, rows_per_layer):
    nl = shapes[0][0]
    flat = packed.reshape(packed.shape[:-2] + (nl, rows_per_layer * PACK_COLS))
    out, off = [], 0
    for shp in shapes:
        size = math.prod(shp[1:])
        out.append(flat[..., off:off + size].reshape(packed.shape[:-2] + tuple(shp)))
        off += size
    return out


def _rows_needed(shapes, multiple):
    per_layer = sum(math.prod(s[1:]) for s in shapes)
    rows = -(-per_layer // PACK_COLS)
    return -(-rows // multiple) * multiple


def _full_weights(w):
    nl = w['w_in'].shape[0]
    conv_flat = w['conv_w'].reshape(-1, 128)
    conv_rows = conv_flat.shape[0]
    conv_flat = jnp.pad(conv_flat, ((0, -conv_rows % 8), (0, 0)))
    gathered = _gather_chips("gather_weights", [w[n].astype(MXU_DTYPE) for n in MATMUL_SHARDED], [conv_flat])
    gathered = [_absolute_chip_order(g) for g in gathered]
    full = dict(w)
    for n, part in zip(MATMUL_SHARDED, gathered):
        if n in ROW_SHARDED:
            full[n] = jnp.swapaxes(part, 0, 1).reshape(nl, 4 * part.shape[2], part.shape[3])
        else:
            full[n] = jnp.transpose(part, (1, 2, 0, 3)).reshape(nl, part.shape[2], 4 * part.shape[3])
    conv = gathered[-1][:, :conv_rows].reshape((4,) + w['conv_w'].shape)
    full['conv_w'] = jnp.transpose(conv, (1, 2, 0, 3)).reshape(nl, 3, -1)
    return full


def _chip_major(n, g):
    nl = g.shape[0]
    if n in ROW_SHARDED:
        return jnp.swapaxes(g.reshape(nl, 4, g.shape[1] // 4, g.shape[2]), 0, 1)
    return jnp.transpose(g.reshape(nl, g.shape[1], 4, g.shape[2] // 4), (2, 0, 1, 3))


def kernel(x, positions, norm_mix_pre, w_in, gate_b, q_norm, w_uq, kv_norm, w_ukv, w_br_mla, sg_ln_g, sg_ln_b, sg_w, sg_b, w_br_sg, conv_w, w_br_conv, pool_w, pool_scale, w_br_pool, w_out, norm_mix_post, norm_ffn_pre, w_ff1, w_ff2, norm_ffn_post, loss_target, m_norm_mix_pre, m_w_in, m_gate_b, m_q_norm, m_w_uq, m_kv_norm, m_w_ukv, m_w_br_mla, m_sg_ln_g, m_sg_ln_b, m_sg_w, m_sg_b, m_w_br_sg, m_conv_w, m_w_br_conv, m_pool_w, m_pool_scale, m_w_br_pool, m_w_out, m_norm_mix_post, m_norm_ffn_pre, m_w_ff1, m_w_ff2, m_norm_ffn_post, v_norm_mix_pre, v_w_in, v_gate_b, v_q_norm, v_w_uq, v_kv_norm, v_w_ukv, v_w_br_mla, v_sg_ln_g, v_sg_ln_b, v_sg_w, v_sg_b, v_w_br_sg, v_conv_w, v_w_br_conv, v_pool_w, v_pool_scale, v_w_br_pool, v_w_out, v_norm_mix_post, v_norm_ffn_pre, v_w_ff1, v_w_ff2, v_norm_ffn_post):
    given = dict(locals())
    w = {n: given[n] for n in WEIGHTS}
    mom = {n: given['m_' + n] for n in WEIGHTS}
    var = {n: given['v_' + n] for n in WEIGHTS}
    nl = w['w_in'].shape[0]
    full = _full_weights(w)
    layers = [{n: full[n][l:l + 1] for n in WEIGHTS} for l in range(nl)]
    reduced, local_small = {}, {}

    def reduce_layer(l, g):
        arrs = [_chip_major(n, g[n])[:, 0] for n in BIG_SHARDED]
        arrs.append(_pack_small([_chip_major(n, g[n])[:, 0] for n in SMALL_SHARDED]))
        chip_sums = _add_sibling("add_sibling", arrs, _sibling_halves("grads_to_sibling", arrs))
        halves = _sum_chips("sum_chips", chip_sums, _scatter_chips("grads_to_chips", chip_sums))
        reduced[l] = _join_siblings("join_halves", halves)
        local_small[l] = [g[n] for n in REPLICATED + ['conv_w']]

    loss, dx = _local_step(x[0], positions[0], loss_target[0], layers, reduce_layer)
    loss = lax.psum(loss, ("x", "y", "c"))

    grad, delta, new_m, new_v = {}, {}, {}, {}
    for k, n in enumerate(BIG_SHARDED):
        grad[n] = jnp.stack([reduced[l][k] for l in range(nl)])
    slots = [[w[n], grad[n], mom[n], var[n]] for n in BIG_SHARDED]
    small_pack = lambda d: _pack_small([d[n] for n in SMALL_SHARDED])
    g_small = jnp.stack([reduced[l][-1] for l in range(nl)])
    slots.append([small_pack(w), g_small, small_pack(mom), small_pack(var)])
    updated = _rowwise_call("adamw_sharded", _adamw_math, slots, [[s_[0].shape] * 3 for s_ in slots], 32)
    for k, n in enumerate(BIG_SHARDED):
        delta[n], new_m[n], new_v[n] = updated[k]
    for d, packed in zip((grad, delta, new_m, new_v), [g_small] + list(updated[-1])):
        d.update(zip(SMALL_SHARDED, _unpack_small(packed, [w[n].shape[1:] for n in SMALL_SHARDED])))

    names = REPLICATED + ['conv_w']
    local = [jnp.concatenate([local_small[l][k] for l in range(nl)]) for k in range(len(names))]
    rows = _rows_needed([a.shape for a in local], 32)
    everyone = _gather_all("gather_small_grads", _pack(local, rows, F32))
    (summed,), = _rowwise_call("sum_devices", _sum_in_order, [[everyone]], [[everyone.shape[1:]]], 4)
    g_rep = _unpack(summed, [a.shape for a in local], rows)
    chip = 2 * lax.axis_index("x") + lax.axis_index("y")
    g_rep[-1] = lax.dynamic_slice_in_dim(g_rep[-1], chip * w['conv_w'].shape[2], w['conv_w'].shape[2], axis=2)
    rep_pack = lambda arrs: _pack(arrs, rows, F32)
    (rep_out,) = _rowwise_call("adamw_replicated", _adamw_math,
                               [[rep_pack([w[n] for n in names]), rep_pack(g_rep), rep_pack([mom[n] for n in names]),
                                 rep_pack([var[n] for n in names])]], [[(nl * rows, PACK_COLS)] * 3], 4)
    grad.update(zip(names, g_rep))
    for d, packed in zip((delta, new_m, new_v), rep_out):
        d.update(zip(names, _unpack(packed, [w[n].shape for n in names], rows)))

    return (loss, dx[None], *[grad[n] for n in WEIGHTS], *[delta[n] for n in WEIGHTS], *[new_m[n] for n in WEIGHTS],
            *[new_v[n] for n in WEIGHTS])
```

```python
import functools
import math
from typing import Any, Callable, NamedTuple, Sequence

import jax
import jax.numpy as jnp
from jax import lax
from jax.experimental import pallas as pl
from jax.experimental.pallas import tpu as pltpu

F32 = jnp.float32
MXU_DTYPE = jnp.bfloat16
WIRE_DTYPE = jnp.bfloat16
ACT_DTYPE = jnp.bfloat16
MESH = pl.DeviceIdType.MESH

D_MODEL = 1024
D_FF = 4096
N_HEADS = 4
QK_NOPE = 64
QK_ROPE = 32
V_HEAD = 64
HEAD_PAD = 128
Q_RANK = 256
KV_RANK = 128
SG_CHUNK = 128
SG_GROUPS = 4
BR_WIDTH = 256
N_BRANCH = 4
POOL_WINDOWS = (2, 4, 8, 16)
HALO = 16
ROPE_BASE = 10000.0
EPS = 1e-6
ATTN_SCALE = (QK_NOPE + QK_ROPE) ** -0.5
N_PROJ = N_BRANCH * D_MODEL + 6 * BR_WIDTH + Q_RANK + KV_RANK + 2 * HEAD_PAD
COL_G, COL_M1, COL_M2, COL_B = 0, 4096, 4864, 5632

ADAM_LR, ADAM_B1, ADAM_B2, ADAM_EPS, ADAM_WD, ADAM_STEP = 0.001, 0.9, 0.999, 1e-08, 0.01, 10

VMEM_LIMIT = 56 * 1024 * 1024

WEIGHTS = ['norm_mix_pre', 'w_in', 'gate_b', 'q_norm', 'w_uq', 'kv_norm', 'w_ukv', 'w_br_mla', 'sg_ln_g', 'sg_ln_b',
           'sg_w', 'sg_b', 'w_br_sg', 'conv_w', 'w_br_conv', 'pool_w', 'pool_scale', 'w_br_pool', 'w_out',
           'norm_mix_post', 'norm_ffn_pre', 'w_ff1', 'w_ff2', 'norm_ffn_post']
COL_SHARDED = ['w_in', 'w_uq', 'w_ukv', 'w_br_mla', 'w_br_sg', 'w_br_conv', 'w_br_pool', 'w_ff1']
ROW_SHARDED = ['w_out', 'w_ff2']
MATMUL_SHARDED = ['w_in', 'w_uq', 'w_ukv', 'w_br_mla', 'w_br_sg', 'w_br_conv', 'w_br_pool', 'w_out', 'w_ff1', 'w_ff2']
BIG_SHARDED = ['w_in', 'w_ff1', 'w_ff2', 'w_out']
SMALL_SHARDED = ['w_uq', 'w_ukv', 'w_br_mla', 'w_br_sg', 'w_br_conv', 'w_br_pool']
SHARDED = MATMUL_SHARDED + ['conv_w']
REPLICATED = [n for n in WEIGHTS if n not in SHARDED]
PACK_COLS = 1024


def _params(sem, vmem=VMEM_LIMIT):
    return pltpu.CompilerParams(dimension_semantics=sem, vmem_limit_bytes=vmem)


def _mxu(a):
    return a.astype(MXU_DTYPE)


def _dot(a, b):
    return jnp.dot(_mxu(a), _mxu(b), preferred_element_type=F32)


def _dot_nt(a, b):
    return lax.dot_general(_mxu(a), _mxu(b), (((1,), (1,)), ((), ())), preferred_element_type=F32)


def _dot_tn(a, b):
    return lax.dot_general(_mxu(a), _mxu(b), (((0,), (0,)), ((), ())), preferred_element_type=F32)


def _rms(x, g):
    r = lax.rsqrt(jnp.mean(x * x, axis=-1, keepdims=True) + EPS)
    return x * r * g


def _rms_bwd(x, g, dy):
    r = lax.rsqrt(jnp.mean(x * x, axis=-1, keepdims=True) + EPS)
    xh = x * r
    gdy = dy * g
    dx = r * (gdy - xh * jnp.mean(gdy * xh, axis=-1, keepdims=True))
    return dx, jnp.sum(dy * xh, axis=0, keepdims=True)


_GELU_C = math.sqrt(2.0 / math.pi)


def _gelu(x):
    t = jnp.tanh(_GELU_C * (x + 0.044715 * (x * x * x)))
    return x * (0.5 * (1.0 + t)), t


def _gelu_grad(x, t):
    return 0.5 * (1.0 + t) + 0.5 * x * (1.0 - t * t) * (_GELU_C * (1.0 + 3.0 * 0.044715 * x * x))


def _sigmoid(x):
    return 1.0 / (1.0 + jnp.exp(-x))


def _full(shape):
    return pl.BlockSpec(shape, lambda *_: (0,) * len(shape))


def _resident(shape):
    return pl.BlockSpec(shape, lambda *_: (0,) * len(shape), pipeline_mode=pl.Buffered(1))


def _rows(ts, width, col=0):
    return pl.BlockSpec((ts, width), lambda i: (i, col))


def _tile(n, pref):
    return min(n, pref)


def _mm(name, a, w, *, tm, tn, prologue=None, rows=()):
    m, k = a.shape
    n = w.shape[1]
    tm, tn = _tile(m, tm), _tile(n, tn)

    def body(a_ref, *rest):
        row_refs, w_ref, o_ref = rest[:len(rows)], rest[len(rows)], rest[len(rows) + 1]
        av = a_ref[...]
        if prologue is not None:
            av = prologue(av, *[r[...] for r in row_refs])
        o_ref[...] = _dot(av, w_ref[...]).astype(o_ref.dtype)

    return pl.pallas_call(
        body, name=name, grid=(m // tm, n // tn),
        in_specs=[pl.BlockSpec((tm, k), lambda i, j: (i, 0))] + [pl.BlockSpec((1, k), lambda i, j: (0, 0)) for _ in rows]
        + [pl.BlockSpec((k, tn), lambda i, j: (0, j))],
        out_specs=pl.BlockSpec((tm, tn), lambda i, j: (i, j)),
        out_shape=jax.ShapeDtypeStruct((m, n), ACT_DTYPE),
        compiler_params=_params(("parallel", "parallel")),
    )(a, *rows, w)


def _mm_tn(name, a, b, *, tm, tn, prologue=None, rows=()):
    m, k = a.shape
    n = b.shape[1]
    tm, tn = _tile(m, tm), _tile(n, tn)

    def body(a_ref, *rest):
        row_refs, b_ref, o_ref = rest[:len(rows)], rest[len(rows)], rest[len(rows) + 1]

        @pl.when(pl.program_id(1) == 0)
        def _():
            o_ref[...] = jnp.zeros_like(o_ref)

        av = a_ref[...]
        if prologue is not None:
            av = prologue(av, *[r[...] for r in row_refs])
        o_ref[...] += _dot_tn(av, b_ref[...])

    return pl.pallas_call(
        body, name=name, grid=(n // tn, m // tm),
        in_specs=[pl.BlockSpec((tm, k), lambda j, i: (i, 0))] + [pl.BlockSpec((1, k), lambda j, i: (0, 0)) for _ in rows]
        + [pl.BlockSpec((tm, tn), lambda j, i: (i, j))],
        out_specs=pl.BlockSpec((k, tn), lambda j, i: (0, j)),
        out_shape=jax.ShapeDtypeStruct((k, n), F32),
        compiler_params=_params(("parallel", "arbitrary")),
    )(a, *rows, b)


def _relu_sq(a):
    r = jnp.maximum(a.astype(F32), 0.0)
    return r * r


HBM = pl.BlockSpec(memory_space=pl.ANY)


class _Exchange(NamedTuple):
    operands: Sequence[Any]
    out_shape: Sequence[Any]
    scratch_shapes: Sequence[Any]
    phases: Callable


def _run_exchange(name, ex):
    n_in, n_out = len(ex.operands), len(ex.out_shape)

    def body(*refs):
        for phase in ex.phases(refs[:n_in], refs[n_in:n_in + n_out], refs[n_in + n_out:]):
            phase()

    return pl.pallas_call(body, name=name, in_specs=[HBM] * n_in, out_specs=[HBM] * n_out, out_shape=list(ex.out_shape),
                          scratch_shapes=list(ex.scratch_shapes))(*ex.operands)


def _call_carrying(body, carried, *, name, grid, in_specs, out_specs, out_shape, operands, semantics):
    if carried is None:
        return pl.pallas_call(body, name=name, grid=grid, in_specs=in_specs, out_specs=out_specs, out_shape=out_shape,
                              compiler_params=_params(semantics))(*operands), None
    n_main_in, n_main_out = len(operands), len(out_shape)
    n_in, n_out = len(carried.operands), len(carried.out_shape)
    steps = math.prod(grid)

    def wrapped(*refs):
        main_in, refs = refs[:n_main_in], refs[n_main_in:]
        ex_in, refs = refs[:n_in], refs[n_in:]
        main_out, refs = refs[:n_main_out], refs[n_main_out:]
        ex_out, sems = refs[:n_out], refs[n_out:]
        step = pl.program_id(0)
        for axis in range(1, len(grid)):
            step = step * grid[axis] + pl.program_id(axis)
        start, middle, finish = carried.phases(ex_in, ex_out, sems)
        pl.when(step == 0)(start)
        pl.when(step == (steps - 1) // 2)(middle)
        body(*main_in, *main_out)
        pl.when(step == steps - 1)(finish)

    out = pl.pallas_call(
        wrapped, name=name + "_carrying", grid=grid, in_specs=list(in_specs) + [HBM] * n_in,
        out_specs=list(out_specs) + [HBM] * n_out, out_shape=list(out_shape) + list(carried.out_shape),
        scratch_shapes=list(carried.scratch_shapes), compiler_params=_params(("arbitrary",) * len(grid)))(*operands, *carried.operands)
    return out[:n_main_out], out[n_main_out:]


def _qkv_prep(proj, q_norm, kv_norm, wq, wkv, cq_tab, s_tab, cr_tab):
    s = proj.shape[0]
    ts = _tile(s, 512)
    hw = N_HEADS * HEAD_PAD

    def body(cq_ref, ckv_ref, kr_ref, krs_ref, gq_ref, gkv_ref, wq_ref, wkv_ref, ct_ref, st_ref, crt_ref,
             q_ref, k_ref, v_ref):
        ct, st, crt = ct_ref[...], st_ref[...], crt_ref[...]
        qn = _rms(cq_ref[...].astype(F32), gq_ref[...])
        qab = _dot(qn, wq_ref[...])
        kvn = _rms(ckv_ref[...].astype(F32), gkv_ref[...])
        kav = _dot(kvn, wkv_ref[...])
        k_rope = kr_ref[...].astype(F32) * crt + krs_ref[...].astype(F32) * st
        for h in range(N_HEADS):
            lo = h * HEAD_PAD
            q_ref[h] = (qab[:, lo:lo + HEAD_PAD] * ct + qab[:, hw + lo:hw + lo + HEAD_PAD] * st).astype(q_ref.dtype)
            k_ref[h] = (kav[:, lo:lo + HEAD_PAD] + k_rope).astype(k_ref.dtype)
            v_ref[h] = kav[:, hw + lo:hw + lo + HEAD_PAD].astype(v_ref.dtype)

    head_spec = pl.BlockSpec((N_HEADS, ts, HEAD_PAD), lambda i: (0, i, 0))
    head_shape = jax.ShapeDtypeStruct((N_HEADS, s, HEAD_PAD), MXU_DTYPE)
    return pl.pallas_call(
        body, name="qkv_prep", grid=(s // ts,),
        in_specs=[_rows(ts, Q_RANK, COL_B // Q_RANK), _rows(ts, KV_RANK, (COL_B + Q_RANK) // KV_RANK),
                  _rows(ts, HEAD_PAD, (COL_B + Q_RANK + KV_RANK) // HEAD_PAD),
                  _rows(ts, HEAD_PAD, (COL_B + Q_RANK + KV_RANK + HEAD_PAD) // HEAD_PAD),
                  _full((1, Q_RANK)), _full((1, KV_RANK)), _full((Q_RANK, 2 * hw)), _full((KV_RANK, 2 * hw)),
                  _rows(ts, HEAD_PAD), _rows(ts, HEAD_PAD), _rows(ts, HEAD_PAD)],
        out_specs=[head_spec, head_spec, head_spec],
        out_shape=[head_shape, head_shape, head_shape],
        compiler_params=_params(("parallel",)),
    )(proj, proj, proj, proj, q_norm, kv_norm, wq, wkv, cq_tab, s_tab, cr_tab)


def _diagonal_mask(t):
    return lax.broadcasted_iota(jnp.int32, (t, t), 1) <= lax.broadcasted_iota(jnp.int32, (t, t), 0)


def _attn_fwd(q, k, v, carried=None):
    s = q.shape[1]
    t = _tile(s, 512)

    def body(q_ref, k_ref, v_ref, o_ref, lse_ref):
        i = pl.program_id(1)
        qv = q_ref[0]

        def step(j, carry, on_diagonal):
            m, l, acc = carry
            kj = k_ref[0, pl.ds(pl.multiple_of(j * t, t), t), :]
            vj = v_ref[0, pl.ds(pl.multiple_of(j * t, t), t), :]
            sc = _dot_nt(qv, kj) * ATTN_SCALE
            if on_diagonal:
                sc = jnp.where(_diagonal_mask(t), sc, -jnp.inf)
            m_new = jnp.maximum(m, jnp.max(sc, axis=1, keepdims=True))
            p = jnp.exp(sc - m_new)
            alpha = jnp.exp(m - m_new)
            return m_new, alpha * l + jnp.sum(p, axis=1, keepdims=True), alpha * acc + _dot(p, vj)

        init = (jnp.full((t, 1), -jnp.inf, F32), jnp.zeros((t, 1), F32), jnp.zeros((t, HEAD_PAD), F32))
        below = lax.fori_loop(0, i, functools.partial(step, on_diagonal=False), init)
        m, l, acc = step(i, below, True)
        o_ref[...] = acc / l
        lse_ref[0] = m + jnp.log(l)

    return _call_carrying(
        body, carried, name="attn_fwd", grid=(N_HEADS, s // t),
        in_specs=[pl.BlockSpec((1, t, HEAD_PAD), lambda h, i: (h, i, 0)),
                  pl.BlockSpec((1, s, HEAD_PAD), lambda h, i: (h, 0, 0)),
                  pl.BlockSpec((1, s, HEAD_PAD), lambda h, i: (h, 0, 0))],
        out_specs=[pl.BlockSpec((t, HEAD_PAD), lambda h, i: (i, h)), pl.BlockSpec((1, t, 1), lambda h, i: (h, i, 0))],
        out_shape=[jax.ShapeDtypeStruct((s, N_HEADS * HEAD_PAD), F32), jax.ShapeDtypeStruct((N_HEADS, s, 1), F32)],
        operands=(q, k, v), semantics=("parallel", "parallel"))


def _attn_bwd(q, k, v, do, lse, delta, carried=None):
    s = q.shape[1]
    t = _tile(s, 512)
    nq = s // t

    def body(q_ref, do_ref, lse_ref, dl_ref, k_ref, v_ref, dq_ref, dk_ref, dv_ref):
        j = pl.program_id(1)

        @pl.when(j == 0)
        def _():
            dq_ref[...] = jnp.zeros_like(dq_ref)

        kj, vj = k_ref[0], v_ref[0]

        def step(i, carry, on_diagonal):
            dk, dv = carry
            rows = pl.ds(pl.multiple_of(i * t, t), t)
            qi, doi = q_ref[0, rows, :], do_ref[rows, :]
            sc = _dot_nt(qi, kj) * ATTN_SCALE
            if on_diagonal:
                sc = jnp.where(_diagonal_mask(t), sc, -jnp.inf)
            p = jnp.exp(sc - lse_ref[0, rows, :])
            dv = dv + _dot_tn(p, doi)
            dp = _dot_nt(doi, vj)
            ds = p * (dp - dl_ref[0, rows, :]) * ATTN_SCALE
            dk = dk + _dot_tn(ds, qi)
            dq_ref[0, rows, :] += _dot(ds, kj)
            return dk, dv

        zero = jnp.zeros((t, HEAD_PAD), F32)
        dk, dv = lax.fori_loop(j + 1, nq, functools.partial(step, on_diagonal=False), step(j, (zero, zero), True))
        dk_ref[0] = dk
        dv_ref[0] = dv

    whole = lambda w: pl.BlockSpec((1, s, w), lambda h, j: (h, 0, 0))
    tile = pl.BlockSpec((1, t, HEAD_PAD), lambda h, j: (h, j, 0))
    head_shape = jax.ShapeDtypeStruct((N_HEADS, s, HEAD_PAD), F32)
    return _call_carrying(
        body, carried, name="attn_bwd", grid=(N_HEADS, nq),
        in_specs=[whole(HEAD_PAD), pl.BlockSpec((s, HEAD_PAD), lambda h, j: (0, h)), whole(1), whole(1), tile, tile],
        out_specs=[whole(HEAD_PAD), tile, tile],
        out_shape=[head_shape, head_shape, head_shape],
        operands=(q, do, lse, delta, k, v), semantics=("parallel", "arbitrary"))


def _qkv_bwd(dq, dk, dv, proj, q_norm, kv_norm, wq, wkv, cq_tab, s_tab, cr_tab):
    s = proj.shape[0]
    ts = _tile(s, 512)
    hw = N_HEADS * HEAD_PAD

    def body(dq_ref, dk_ref, dv_ref, cq_ref, ckv_ref, gq_ref, gkv_ref, wq_ref, wkv_ref, ct_ref, st_ref, crt_ref,
             dpb_ref, dwq_ref, dwkv_ref, dgq_ref, dgkv_ref):
        @pl.when(pl.program_id(0) == 0)
        def _():
            for r in (dwq_ref, dwkv_ref, dgq_ref, dgkv_ref):
                r[...] = jnp.zeros_like(r)

        ct, st, crt = ct_ref[...], st_ref[...], crt_ref[...]
        dqs = [dq_ref[h] for h in range(N_HEADS)]
        dks = [dk_ref[h] for h in range(N_HEADS)]
        dqab = jnp.concatenate([d * ct for d in dqs] + [d * st for d in dqs], axis=1)
        dkav = jnp.concatenate(dks + [dv_ref[h] for h in range(N_HEADS)], axis=1)
        dk_sum = dks[0] + dks[1] + dks[2] + dks[3]
        cq, ckv, gq, gkv = cq_ref[...].astype(F32), ckv_ref[...].astype(F32), gq_ref[...], gkv_ref[...]
        dwq_ref[...] += _dot_tn(_rms(cq, gq), dqab)
        dwkv_ref[...] += _dot_tn(_rms(ckv, gkv), dkav)
        dcq, dgq = _rms_bwd(cq, gq, _dot_nt(dqab, wq_ref[...]))
        dckv, dgkv = _rms_bwd(ckv, gkv, _dot_nt(dkav, wkv_ref[...]))
        dgq_ref[...] += dgq
        dgkv_ref[...] += dgkv
        dpb_ref[...] = jnp.concatenate([dcq, dckv, dk_sum * crt, dk_sum * st], axis=1).astype(dpb_ref.dtype)

    head_spec = pl.BlockSpec((N_HEADS, ts, HEAD_PAD), lambda i: (0, i, 0))
    wb = Q_RANK + KV_RANK + 2 * HEAD_PAD
    return pl.pallas_call(
        body, name="qkv_bwd", grid=(s // ts,),
        in_specs=[head_spec, head_spec, head_spec,
                  _rows(ts, Q_RANK, COL_B // Q_RANK), _rows(ts, KV_RANK, (COL_B + Q_RANK) // KV_RANK),
                  _full((1, Q_RANK)), _full((1, KV_RANK)), _full((Q_RANK, 2 * hw)), _full((KV_RANK, 2 * hw)),
                  _rows(ts, HEAD_PAD), _rows(ts, HEAD_PAD), _rows(ts, HEAD_PAD)],
        out_specs=[_rows(ts, wb), _full((Q_RANK, 2 * hw)), _full((KV_RANK, 2 * hw)), _full((1, Q_RANK)), _full((1, KV_RANK))],
        out_shape=[jax.ShapeDtypeStruct((s, wb), MXU_DTYPE), jax.ShapeDtypeStruct((Q_RANK, 2 * hw), F32),
                   jax.ShapeDtypeStruct((KV_RANK, 2 * hw), F32), jax.ShapeDtypeStruct((1, Q_RANK), F32),
                   jax.ShapeDtypeStruct((1, KV_RANK), F32)],
        compiler_params=_params(("arbitrary",)),
    )(dq, dk, dv, proj, proj, q_norm, kv_norm, wq, wkv, cq_tab, s_tab, cr_tab)


def _lane_group(width):
    return lax.broadcasted_iota(jnp.int32, (1, width), 1) // (width // 4)


def _shift_down(a, k):
    return pltpu.roll(a, k, 0)


def _shift_up(a, k):
    return pltpu.roll(a, a.shape[0] - k, 0)


def _window_sums(xh, shift):
    s2 = xh + shift(xh, 1)
    s4 = s2 + shift(s2, 2)
    s8 = s4 + shift(s4, 4)
    s16 = s8 + shift(s8, 8)
    grp = _lane_group(xh.shape[1])
    return jnp.where(grp == 0, s2, jnp.where(grp == 1, s4, jnp.where(grp == 2, s8, s16)))


def _pool_count(i, ts):
    grp = _lane_group(BR_WIDTH)
    win = jnp.where(grp == 0, 2.0, jnp.where(grp == 1, 4.0, jnp.where(grp == 2, 8.0, 16.0)))
    t = (i * ts + lax.broadcasted_iota(jnp.int32, (ts, 1), 0)).astype(F32)
    return jnp.minimum(t + 1.0, win)


def _mix_forward(i, ts, r):
    f = {}
    act = lambda name: r[name][...].astype(F32)
    f['gates'] = _sigmoid(act('gate') + r['gate_b'][...])
    sgu, sgv = act('sgu'), act('sgv')
    f['sgu'], f['sgv'] = sgu, sgv
    u_act, f['tu'] = _gelu(sgu)
    vg, f['tv'] = _gelu(sgv)
    mu = jnp.mean(vg, axis=-1, keepdims=True)
    xc = vg - mu
    f['ln_r'] = lax.rsqrt(jnp.mean(xc * xc, axis=-1, keepdims=True) + EPS)
    f['ln_xh'] = xc * f['ln_r']
    vln = f['ln_xh'] * r['ln_g'][...] + r['ln_b'][...]
    tril = lax.broadcasted_iota(jnp.int32, (SG_CHUNK, SG_CHUNK), 1) <= lax.broadcasted_iota(jnp.int32, (SG_CHUNK, SG_CHUNK), 0)
    f['wm'] = [_mxu(jnp.where(tril, r['sg_w'][g], 0.0)) for g in range(SG_GROUPS)]
    f['tril'] = tril
    grp = _lane_group(BR_WIDTH)
    bias = r['sg_bias'][...]
    parts = []
    for ci in range(ts // SG_CHUNK):
        vc = vln[ci * SG_CHUNK:(ci + 1) * SG_CHUNK]
        sc = bias
        for g in range(SG_GROUPS):
            sc = sc + jnp.where(grp == g, _dot(f['wm'][g], vc), 0.0)
        parts.append(sc)
    f['vln'] = vln
    f['sg_s'] = parts[0] if len(parts) == 1 else jnp.concatenate(parts, axis=0)
    f['u_act'] = u_act
    out_b = u_act * f['sg_s']
    first = (i > 0).astype(F32)
    cvx, cvc, cvb = act('cvx'), act('cvc'), act('cvb')
    f['cvx'], f['cvc'], f['cvb'] = cvx, cvc, cvb
    zh = jnp.concatenate([act('hx') * act('hc') * first, cvc * cvx], axis=0)
    f['z1'] = _shift_down(zh, 1)[HALO:]
    f['z2'] = _shift_down(zh, 2)[HALO:]
    f['z0'] = zh[HALO:]
    f['yv'] = r['conv_w'][0:1, :] * f['z2'] + r['conv_w'][1:2, :] * f['z1'] + r['conv_w'][2:3, :] * f['z0']
    out_c = cvb * f['yv']
    p = act('pool')
    ph = jnp.concatenate([act('hp') * first, p], axis=0)
    f['cnt'] = _pool_count(i, ts)
    f['pooled'] = _window_sums(ph, _shift_down)[HALO:] / f['cnt'] - p
    f['mixed'] = _dot(f['pooled'], r['wbd'][...])
    out_d = f['mixed'] * r['pool_scale'][...]
    f['outs'] = [r['o'][...], out_b, out_c, out_d]
    f['ys'] = [_dot(f['outs'][b], r['w_br'][b][...]) for b in range(N_BRANCH)]
    merged = f['gates'][:, 0:D_MODEL] * f['ys'][0]
    for b in range(1, N_BRANCH):
        merged = merged + f['gates'][:, b * D_MODEL:(b + 1) * D_MODEL] * f['ys'][b]
    f['merged'] = merged
    f['mo'] = _dot(merged, r['w_out'][...])
    return f


_MIX_TILE_INPUTS = ['gate', 'sgu', 'sgv', 'cvb', 'cvx', 'cvc', 'pool', 'hx', 'hc', 'hp', 'o']
_MIX_WEIGHTS = ['gate_b', 'ln_g', 'ln_b', 'sg_w', 'sg_bias', 'conv_w', 'wbd', 'pool_scale', 'w_br0', 'w_br1', 'w_br2',
                'w_br3', 'w_out', 'g_post']


def _mix_specs(s, ts):
    c0 = COL_M1 // BR_WIDTH
    prev = lambda col: pl.BlockSpec((HALO, BR_WIDTH), lambda i: (jnp.maximum(i * (ts // HALO) - 1, 0), col))
    tiles = [_rows(ts, N_BRANCH * D_MODEL, 0), _rows(ts, BR_WIDTH, c0), _rows(ts, BR_WIDTH, c0 + 1), _rows(ts, BR_WIDTH, c0 + 2),
             _rows(ts, BR_WIDTH, c0 + 3), _rows(ts, BR_WIDTH, c0 + 4), _rows(ts, BR_WIDTH, c0 + 5),
             prev(c0 + 3), prev(c0 + 4), prev(c0 + 5), _rows(ts, N_HEADS * HEAD_PAD)]
    weights = [_full((1, N_BRANCH * D_MODEL)), _full((1, BR_WIDTH)), _full((1, BR_WIDTH)),
               _full((SG_GROUPS, SG_CHUNK, SG_CHUNK)), _full((SG_CHUNK, BR_WIDTH)), _full((8, BR_WIDTH)),
               _resident((BR_WIDTH, BR_WIDTH)), _full((1, BR_WIDTH)), _resident((N_HEADS * HEAD_PAD, D_MODEL)),
               _resident((BR_WIDTH, D_MODEL)), _resident((BR_WIDTH, D_MODEL)), _resident((BR_WIDTH, D_MODEL)),
               _resident((D_MODEL, D_MODEL)), _full((1, D_MODEL))]
    return tiles, weights


def _mix_refs(refs):
    names = _MIX_TILE_INPUTS + _MIX_WEIGHTS
    r = dict(zip(names, refs[:len(names)]))
    r['w_br'] = [r['w_br0'], r['w_br1'], r['w_br2'], r['w_br3']]
    return r, refs[len(names):]


def _mix_operands(proj, o, lw):
    return ([proj] * 10 + [o] + [lw[n] for n in ['gate_b', 'sg_ln_g', 'sg_ln_b', 'sg_w', 'sg_bias', 'conv_w8', 'wbd',
                                                 'pool_scale', 'w_br_mla_p', 'w_br_sg', 'w_br_conv', 'w_br_pool', 'w_out',
                                                 'norm_mix_post']])


def _mix_fwd(x0, proj, o, lw):
    s = x0.shape[0]
    ts = _tile(s, 512)
    tiles, weights = _mix_specs(s, ts)

    def body(*refs):
        r, (x0_ref, x1_ref) = _mix_refs(refs)
        f = _mix_forward(pl.program_id(0), ts, r)
        x1_ref[...] = x0_ref[...] + _rms(f['mo'], r['g_post'][...])

    return pl.pallas_call(
        body, name="mix_fwd", grid=(s // ts,),
        in_specs=tiles + weights + [_rows(ts, D_MODEL)],
        out_specs=_rows(ts, D_MODEL),
        out_shape=jax.ShapeDtypeStruct((s, D_MODEL), F32),
        compiler_params=_params(("parallel",)),
    )(*_mix_operands(proj, o, lw), x0)


def _mix_bwd(dx1, proj, o, lw):
    s = dx1.shape[0]
    ts = _tile(s, 256)
    tiles, weights = _mix_specs(s, ts)
    hw = N_HEADS * HEAD_PAD

    def body(*refs):
        r, rest = _mix_refs(refs)
        (dx1_ref, dg_ref, dm1_ref, dyv_ref, up_ref, do_ref, delta_ref,
         dgate_b_ref, dln_g_ref, dln_b_ref, dsgw_ref, dsgb_ref, dconv_ref, dwbd_ref, dps_ref,
         dwbr0_ref, dwbr1_ref, dwbr2_ref, dwbr3_ref, dwout_ref, dgpost_ref, dbias_acc) = rest
        i = pl.program_id(0)
        acc_refs = [dgate_b_ref, dln_g_ref, dln_b_ref, dsgw_ref, dsgb_ref, dconv_ref, dwbd_ref, dps_ref,
                    dwbr0_ref, dwbr1_ref, dwbr2_ref, dwbr3_ref, dwout_ref, dgpost_ref, dbias_acc]

        @pl.when(i == 0)
        def _():
            for a in acc_refs:
                a[...] = jnp.zeros_like(a)

        f = _mix_forward(i, ts, r)
        dmo, dgpost = _rms_bwd(f['mo'], r['g_post'][...], dx1_ref[...])
        dgpost_ref[...] += dgpost
        dwout_ref[...] += _dot_tn(f['merged'], dmo)
        dmerged = _dot_nt(dmo, r['w_out'][...])
        dwbr = [dwbr0_ref, dwbr1_ref, dwbr2_ref, dwbr3_ref]
        douts = []
        for b in range(N_BRANCH):
            gb = f['gates'][:, b * D_MODEL:(b + 1) * D_MODEL]
            dgate = dmerged * f['ys'][b] * gb * (1.0 - gb)
            dg_ref[:, b * D_MODEL:(b + 1) * D_MODEL] = dgate.astype(dg_ref.dtype)
            dgate_b_ref[:, b * D_MODEL:(b + 1) * D_MODEL] += jnp.sum(dgate, axis=0, keepdims=True)
            dy = dmerged * gb
            dwbr[b][...] += _dot_tn(f['outs'][b], dy)
            douts.append(_dot_nt(dy, r['w_br'][b][...]))
        do = douts[0]
        do_ref[...] = do.astype(do_ref.dtype)
        prod = do * f['outs'][0]
        for h in range(N_HEADS):
            delta_ref[h] = jnp.sum(prod[:, h * HEAD_PAD:(h + 1) * HEAD_PAD], axis=1, keepdims=True)
        grp = _lane_group(BR_WIDTH)
        ds = douts[1] * f['u_act']
        dsgu = douts[1] * f['sg_s'] * _gelu_grad(f['sgu'], f['tu'])
        dvln_parts = []
        for ci in range(ts // SG_CHUNK):
            rows = slice(ci * SG_CHUNK, (ci + 1) * SG_CHUNK)
            ds_c, vln_c = ds[rows], f['vln'][rows]
            dvln_c = jnp.zeros((SG_CHUNK, BR_WIDTH), F32)
            for g in range(SG_GROUPS):
                dvln_c = dvln_c + jnp.where(grp == g, _dot_tn(f['wm'][g], ds_c), 0.0)
                dsgw_ref[g] += jnp.where(f['tril'], _dot_nt(jnp.where(grp == g, ds_c, 0.0), vln_c), 0.0)
            dbias_acc[...] += ds_c
            dvln_parts.append(dvln_c)
        dvln = dvln_parts[0] if len(dvln_parts) == 1 else jnp.concatenate(dvln_parts, axis=0)
        dln_g_ref[...] += jnp.sum(dvln * f['ln_xh'], axis=0, keepdims=True)
        dln_b_ref[...] += jnp.sum(dvln, axis=0, keepdims=True)
        dxh = dvln * r['ln_g'][...]
        dvg = f['ln_r'] * (dxh - jnp.mean(dxh, axis=-1, keepdims=True)
                           - f['ln_xh'] * jnp.mean(dxh * f['ln_xh'], axis=-1, keepdims=True))
        dsgv = dvg * _gelu_grad(f['sgv'], f['tv'])
        dcvb = douts[2] * f['yv']
        dyv = douts[2] * f['cvb']
        dyv_ref[...] = dyv
        for kk, zk in enumerate((f['z2'], f['z1'], f['z0'])):
            dconv_ref[kk:kk + 1, :] += jnp.sum(dyv * zk, axis=0, keepdims=True)
        dps_ref[...] += jnp.sum(douts[3] * f['mixed'], axis=0, keepdims=True)
        dmixed = douts[3] * r['pool_scale'][...]
        dwbd_ref[...] += _dot_tn(f['pooled'], dmixed)
        up_ref[...] = _dot_nt(dmixed, r['wbd'][...]) / f['cnt']
        dm1_ref[...] = jnp.concatenate([dsgu, dsgv, dcvb], axis=1).astype(dm1_ref.dtype)

        @pl.when(i == pl.num_programs(0) - 1)
        def _():
            lane = lax.broadcasted_iota(jnp.int32, (1, SG_CHUNK), 1)
            db = dbias_acc[...]
            out = jnp.zeros((SG_CHUNK, SG_CHUNK), F32)
            for g in range(SG_GROUPS):
                out = out + jnp.where(lane == g, jnp.sum(jnp.where(grp == g, db, 0.0), axis=1, keepdims=True), 0.0)
            dsgb_ref[...] = out

    acc = lambda shape: (_full(shape), jax.ShapeDtypeStruct(shape, F32))
    accs = [acc((1, N_BRANCH * D_MODEL)), acc((1, BR_WIDTH)), acc((1, BR_WIDTH)), acc((SG_GROUPS, SG_CHUNK, SG_CHUNK)),
            acc((SG_CHUNK, SG_CHUNK)), acc((8, BR_WIDTH)), acc((BR_WIDTH, BR_WIDTH)), acc((1, BR_WIDTH)),
            acc((hw, D_MODEL)), acc((BR_WIDTH, D_MODEL)), acc((BR_WIDTH, D_MODEL)), acc((BR_WIDTH, D_MODEL)),
            acc((D_MODEL, D_MODEL)), acc((1, D_MODEL))]
    tile_outs = [(_rows(ts, N_BRANCH * D_MODEL), jax.ShapeDtypeStruct((s, N_BRANCH * D_MODEL), MXU_DTYPE)),
                 (_rows(ts, 3 * BR_WIDTH), jax.ShapeDtypeStruct((s, 3 * BR_WIDTH), MXU_DTYPE)),
                 (_rows(ts, BR_WIDTH), jax.ShapeDtypeStruct((s, BR_WIDTH), F32)),
                 (_rows(ts, BR_WIDTH), jax.ShapeDtypeStruct((s, BR_WIDTH), F32)),
                 (_rows(ts, hw), jax.ShapeDtypeStruct((s, hw), MXU_DTYPE)),
                 (pl.BlockSpec((N_HEADS, ts, 1), lambda i: (0, i, 0)), jax.ShapeDtypeStruct((N_HEADS, s, 1), F32))]
    outs = tile_outs + accs
    return pl.pallas_call(
        body, name="mix_bwd", grid=(s // ts,),
        in_specs=tiles + weights + [_rows(ts, D_MODEL)],
        out_specs=[o_[0] for o_ in outs], out_shape=[o_[1] for o_ in outs],
        scratch_shapes=[pltpu.VMEM((SG_CHUNK, BR_WIDTH), F32)],
        compiler_params=_params(("arbitrary",), 60 * 1024 * 1024),
    )(*_mix_operands(proj, o, lw), dx1)


def _shift_bwd(dyv, upool, proj, conv_w8):
    s = dyv.shape[0]
    ts = _tile(s, 512)
    nb = s // HALO
    c0 = COL_M1 // BR_WIDTH

    def body(dyv_ref, dyvn_ref, up_ref, upn_ref, cvx_ref, cvc_ref, cw_ref, out_ref):
        i = pl.program_id(0)
        last = (i < pl.num_programs(0) - 1).astype(F32)
        dh = jnp.concatenate([dyv_ref[...], dyvn_ref[...] * last], axis=0)
        dz = (cw_ref[2:3, :] * dh + cw_ref[1:2, :] * _shift_up(dh, 1) + cw_ref[0:1, :] * _shift_up(dh, 2))[:ts]
        up = up_ref[...]
        uh = jnp.concatenate([up, upn_ref[...] * last], axis=0)
        dpool = _window_sums(uh, _shift_up)[:ts] - up * _pool_count(i, ts)
        out_ref[...] = jnp.concatenate([dz * cvc_ref[...].astype(F32), dz * cvx_ref[...].astype(F32), dpool],
                                       axis=1).astype(out_ref.dtype)

    nxt = pl.BlockSpec((HALO, BR_WIDTH), lambda i: (jnp.minimum((i + 1) * (ts // HALO), nb - 1), 0))
    return pl.pallas_call(
        body, name="shift_bwd", grid=(s // ts,),
        in_specs=[_rows(ts, BR_WIDTH), nxt, _rows(ts, BR_WIDTH), nxt, _rows(ts, BR_WIDTH, c0 + 3), _rows(ts, BR_WIDTH, c0 + 4),
                  _full((8, BR_WIDTH))],
        out_specs=_rows(ts, 3 * BR_WIDTH),
        out_shape=jax.ShapeDtypeStruct((s, 3 * BR_WIDTH), MXU_DTYPE),
        compiler_params=_params(("parallel",)),
    )(dyv, dyv, upool, upool, proj, proj, conv_w8)


def _ffn2(a, w2, x1, g):
    s = a.shape[0]
    ts = _tile(s, 512)

    def body(a_ref, w_ref, x1_ref, g_ref, x2_ref, f_ref):
        f = _dot(_relu_sq(a_ref[...]), w_ref[...])
        f_ref[...] = f
        x2_ref[...] = x1_ref[...] + _rms(f, g_ref[...])

    return pl.pallas_call(
        body, name="ffn2", grid=(s // ts,),
        in_specs=[_rows(ts, D_FF), _resident((D_FF, D_MODEL)), _rows(ts, D_MODEL), _full((1, D_MODEL))],
        out_specs=[_rows(ts, D_MODEL), _rows(ts, D_MODEL)],
        out_shape=[jax.ShapeDtypeStruct((s, D_MODEL), F32)] * 2,
        compiler_params=_params(("parallel",)),
    )(a, w2, x1, g)


def _ffn2_bwd(dx2, f, g, a, w2t):
    s = a.shape[0]
    ts = _tile(s, 512)

    def body(dx2_ref, f_ref, g_ref, a_ref, w_ref, df_ref, da_ref, dg_ref):
        @pl.when(pl.program_id(0) == 0)
        def _():
            dg_ref[...] = jnp.zeros_like(dg_ref)

        df, dg = _rms_bwd(f_ref[...], g_ref[...], dx2_ref[...])
        dg_ref[...] += dg
        df_ref[...] = df.astype(df_ref.dtype)
        da_ref[...] = (_dot(df, w_ref[...]) * (2.0 * jnp.maximum(a_ref[...].astype(F32), 0.0))).astype(da_ref.dtype)

    return pl.pallas_call(
        body, name="ffn2_bwd", grid=(s // ts,),
        in_specs=[_rows(ts, D_MODEL), _rows(ts, D_MODEL), _full((1, D_MODEL)), _rows(ts, D_FF), _resident((D_MODEL, D_FF))],
        out_specs=[_rows(ts, D_MODEL), _rows(ts, D_FF), _full((1, D_MODEL))],
        out_shape=[jax.ShapeDtypeStruct((s, D_MODEL), MXU_DTYPE), jax.ShapeDtypeStruct((s, D_FF), MXU_DTYPE),
                   jax.ShapeDtypeStruct((1, D_MODEL), F32)],
        compiler_params=_params(("arbitrary",)),
    )(dx2, f, g, a, w2t)


def _norm_in_bwd(name, pieces, x, g, dres):
    s = x.shape[0]
    ts = _tile(s, 512)
    n = len(pieces)

    def body(*refs):
        d_refs, w_refs = refs[:n], refs[n:2 * n]
        x_ref, g_ref, dres_ref, dx_ref, dg_ref = refs[2 * n:]

        @pl.when(pl.program_id(0) == 0)
        def _():
            dg_ref[...] = jnp.zeros_like(dg_ref)

        dh = _dot(d_refs[0][...], w_refs[0][...])
        for p in range(1, n):
            dh = dh + _dot(d_refs[p][...], w_refs[p][...])
        dx, dg = _rms_bwd(x_ref[...], g_ref[...], dh)
        dg_ref[...] += dg
        dx_ref[...] = dres_ref[...] + dx

    return pl.pallas_call(
        body, name=name, grid=(s // ts,),
        in_specs=[_rows(ts, d.shape[1]) for d, _ in pieces] + [_resident(w.shape) for _, w in pieces]
        + [_rows(ts, D_MODEL), _full((1, D_MODEL)), _rows(ts, D_MODEL)],
        out_specs=[_rows(ts, D_MODEL), _full((1, D_MODEL))],
        out_shape=[jax.ShapeDtypeStruct((s, D_MODEL), F32), jax.ShapeDtypeStruct((1, D_MODEL), F32)],
        compiler_params=_params(("arbitrary",)),
    )(*[d for d, _ in pieces], *[w for _, w in pieces], x, g, dres)


def _loss_and_grad(y, target):
    s = y.shape[0]
    ts = _tile(s, 512)

    def body(y_ref, t_ref, dy_ref, loss_ref):
        @pl.when(pl.program_id(0) == 0)
        def _():
            loss_ref[...] = jnp.zeros_like(loss_ref)

        err = y_ref[...] - t_ref[...]
        dy_ref[...] = err * (1.0 / D_MODEL)
        loss_ref[...] += 0.5 * jnp.sum(jnp.mean(err * err, axis=-1, keepdims=True), axis=0, keepdims=True)

    dy, loss = pl.pallas_call(
        body, name="loss", grid=(s // ts,),
        in_specs=[_rows(ts, D_MODEL), _rows(ts, D_MODEL)],
        out_specs=[_rows(ts, D_MODEL), _full((8, 128))],
        out_shape=[jax.ShapeDtypeStruct((s, D_MODEL), F32), jax.ShapeDtypeStruct((8, 128), F32)],
        compiler_params=_params(("arbitrary",)),
    )(y, target)
    return loss[0, 0], dy


_W_IN_SPLITS = [256, 384, 416, 672, 928, 1184, 1440, 1696, 1952]


def _rope_swap(w):
    half = QK_ROPE // 2
    return jnp.concatenate([-w[..., half:], w[..., :half]], axis=-1)


def _rope_unswap(d):
    half = QK_ROPE // 2
    return jnp.concatenate([d[..., half:], -d[..., :half]], axis=-1)


def _zeros_like_cols(w, n):
    return jnp.zeros(w.shape[:-1] + (n,), w.dtype)


def _derive_weights(w):
    md = MXU_DTYPE
    nl = w['w_in'].shape[0]
    c_q, c_kv, k_r, sg_u, sg_v, cv_x, cv_b, cv_c, pool, gate = jnp.split(w['w_in'].astype(md), _W_IN_SPLITS, axis=-1)
    pad_rope = lambda r: jnp.concatenate([_zeros_like_cols(r, QK_NOPE), r, _zeros_like_cols(r, HEAD_PAD - QK_NOPE - QK_ROPE)], -1)
    w_in_p = jnp.concatenate([gate, sg_u, sg_v, cv_b, cv_x, cv_c, pool, c_q, c_kv, pad_rope(k_r), pad_rope(_rope_swap(k_r))], -1)
    wq = w['w_uq'].astype(md).reshape(nl, Q_RANK, N_HEADS, QK_NOPE + QK_ROPE)
    nope, rope_w = wq[..., :QK_NOPE], wq[..., QK_NOPE:]
    wq_a = jnp.concatenate([nope, rope_w, _zeros_like_cols(nope, 32)], -1).reshape(nl, Q_RANK, N_HEADS * HEAD_PAD)
    wq_b = pad_rope(_rope_swap(rope_w)).reshape(nl, Q_RANK, N_HEADS * HEAD_PAD)
    wkv = w['w_ukv'].astype(md).reshape(nl, KV_RANK, N_HEADS, QK_NOPE + V_HEAD)
    pad_half = lambda r: jnp.concatenate([r, _zeros_like_cols(r, HEAD_PAD - r.shape[-1])], -1).reshape(nl, KV_RANK, N_HEADS * HEAD_PAD)
    w_br_mla = w['w_br_mla'].astype(md).reshape(nl, N_HEADS, V_HEAD, D_MODEL)
    w_br_mla_p = jnp.concatenate([w_br_mla, jnp.zeros_like(w_br_mla)], axis=2).reshape(nl, N_HEADS * HEAD_PAD, D_MODEL)
    eye = jnp.eye(4, dtype=md)
    wbd = (w['pool_w'].astype(md)[:, :, :, None, :] * eye[None, :, None, :, None]).reshape(nl, BR_WIDTH, BR_WIDTH)
    row = lambda a: a.astype(F32)[:, None, :]
    w_in_pt = jnp.swapaxes(w_in_p, 1, 2)
    return dict(
        w_in_p=w_in_p, wt_g=w_in_pt[:, COL_G:COL_M1], wt_m1=w_in_pt[:, COL_M1:COL_M2], wt_m2=w_in_pt[:, COL_M2:COL_B],
        wt_b=w_in_pt[:, COL_B:],
        wq=jnp.concatenate([wq_a, wq_b], -1), wkv=jnp.concatenate([pad_half(wkv[..., :QK_NOPE]), pad_half(wkv[..., QK_NOPE:])], -1),
        w_br_mla_p=w_br_mla_p, w_br_sg=w['w_br_sg'].astype(md), w_br_conv=w['w_br_conv'].astype(md),
        w_br_pool=w['w_br_pool'].astype(md), wbd=wbd, w_out=w['w_out'].astype(md),
        w_ff1=w['w_ff1'].astype(md), w_ff1t=jnp.swapaxes(w['w_ff1'].astype(md), 1, 2),
        w_ff2=w['w_ff2'].astype(md), w_ff2t=jnp.swapaxes(w['w_ff2'].astype(md), 1, 2),
        norm_mix_pre=row(w['norm_mix_pre']), gate_b=row(w['gate_b']), q_norm=row(w['q_norm']), kv_norm=row(w['kv_norm']),
        sg_ln_g=row(w['sg_ln_g']), sg_ln_b=row(w['sg_ln_b']), sg_w=w['sg_w'].astype(F32),
        sg_bias=jnp.repeat(jnp.swapaxes(w['sg_b'].astype(F32), 1, 2), BR_WIDTH // SG_GROUPS, axis=2),
        conv_w8=jnp.pad(w['conv_w'].astype(F32), ((0, 0), (0, 5), (0, 0))), pool_scale=row(w['pool_scale']),
        norm_mix_post=row(w['norm_mix_post']), norm_ffn_pre=row(w['norm_ffn_pre']), norm_ffn_post=row(w['norm_ffn_post']),
    )


def _rope_tables(positions):
    inv_freq = ROPE_BASE ** (-jnp.arange(0, QK_ROPE, 2, dtype=F32) / QK_ROPE)
    ang = positions.astype(F32)[:, None] * inv_freq
    cos, sin = jnp.cos(ang), jnp.sin(ang)
    n = positions.shape[0]
    ones, z64, z32 = jnp.ones((n, QK_NOPE), F32), jnp.zeros((n, QK_NOPE), F32), jnp.zeros((n, 32), F32)
    return (jnp.concatenate([ones, cos, cos, z32], 1), jnp.concatenate([z64, sin, sin, z32], 1),
            jnp.concatenate([z64, cos, cos, z32], 1))


def _reference_layout_grads(g):
    gate, dm1, dm2, dpb = g['dw_in_pieces']
    nl = gate.shape[0]
    sg_u, sg_v, cv_b = jnp.split(dm1, 3, axis=-1)
    cv_x, cv_c, pool = jnp.split(dm2, 3, axis=-1)
    c_q, c_kv, kr, krs = jnp.split(dpb, [Q_RANK, Q_RANK + KV_RANK, Q_RANK + KV_RANK + HEAD_PAD], axis=-1)
    rope_cols = slice(QK_NOPE, QK_NOPE + QK_ROPE)
    k_r = kr[..., rope_cols] + _rope_unswap(krs[..., rope_cols])
    w_in = jnp.concatenate([c_q, c_kv, k_r, sg_u, sg_v, cv_x, cv_b, cv_c, pool, gate], -1)
    hw = N_HEADS * HEAD_PAD
    dqa = g['dwq'][..., :hw].reshape(nl, Q_RANK, N_HEADS, HEAD_PAD)
    dqb = g['dwq'][..., hw:].reshape(nl, Q_RANK, N_HEADS, HEAD_PAD)
    w_uq = jnp.concatenate([dqa[..., :QK_NOPE], dqa[..., rope_cols] + _rope_unswap(dqb[..., rope_cols])], -1)
    dka = g['dwkv'][..., :hw].reshape(nl, KV_RANK, N_HEADS, HEAD_PAD)
    dva = g['dwkv'][..., hw:].reshape(nl, KV_RANK, N_HEADS, HEAD_PAD)
    w_ukv = jnp.concatenate([dka[..., :QK_NOPE], dva[..., :V_HEAD]], -1)
    w_br_mla = g['dw_br_mla_p'].reshape(nl, N_HEADS, HEAD_PAD, D_MODEL)[:, :, :V_HEAD]
    dwbd = g['dwbd'].reshape(nl, 4, 64, 4, 64)
    pool_w = jnp.stack([dwbd[:, k, :, k, :] for k in range(4)], axis=1)
    sq = lambda a: a[:, 0, :]
    return dict(
        norm_mix_pre=sq(g['dg_pre']), w_in=w_in, gate_b=sq(g['dgate_b']), q_norm=sq(g['dq_norm']),
        w_uq=w_uq.reshape(nl, Q_RANK, -1), kv_norm=sq(g['dkv_norm']), w_ukv=w_ukv.reshape(nl, KV_RANK, -1),
        w_br_mla=w_br_mla.reshape(nl, N_HEADS * V_HEAD, D_MODEL), sg_ln_g=sq(g['dln_g']), sg_ln_b=sq(g['dln_b']),
        sg_w=g['dsg_w'], sg_b=jnp.swapaxes(g['dsg_b'][:, :, :SG_GROUPS], 1, 2), w_br_sg=g['dw_br_sg'],
        conv_w=g['dconv_w'][:, :3], w_br_conv=g['dw_br_conv'], pool_w=pool_w, pool_scale=sq(g['dpool_scale']),
        w_br_pool=g['dw_br_pool'], w_out=g['dw_out'], norm_mix_post=sq(g['dg_post']), norm_ffn_pre=sq(g['dg_fpre']),
        w_ff1=g['dw_ff1'], w_ff2=g['dw_ff2'], norm_ffn_post=sq(g['dg_fpost']))


def _layer_forward(x0, lw, tabs, carried):
    proj = _mm("in_proj", x0, lw['w_in_p'], tm=1024, tn=896, prologue=_rms, rows=(lw['norm_mix_pre'],))
    q, k, v = _qkv_prep(proj, lw['q_norm'], lw['kv_norm'], lw['wq'], lw['wkv'], *tabs)
    (o, lse), carried_out = _attn_fwd(q, k, v, carried)
    x1 = _mix_fwd(x0, proj, o, lw)
    a = _mm("ffn1", x1, lw['w_ff1'], tm=1024, tn=1024, prologue=_rms, rows=(lw['norm_ffn_pre'],))
    x2, f = _ffn2(a, lw['w_ff2'], x1, lw['norm_ffn_post'])
    return x2, dict(x0=x0, proj=proj, q=q, k=k, v=v, o=o, lse=lse, x1=x1, a=a, f=f), carried_out


def _layer_backward(dx2, lw, sv, tabs, carried):
    g = {}
    df, da, g['dg_fpost'] = _ffn2_bwd(dx2, sv['f'], lw['norm_ffn_post'], sv['a'], lw['w_ff2t'])
    g['dw_ff2'] = _mm_tn("dw_ff2", sv['a'], df, tm=512, tn=512, prologue=_relu_sq)
    dx1, g['dg_fpre'] = _norm_in_bwd("ffn1_bwd", [(da, lw['w_ff1t'])], sv['x1'], lw['norm_ffn_pre'], dx2)
    g['dw_ff1'] = _mm_tn("dw_ff1", sv['x1'], da, tm=512, tn=1024, prologue=_rms, rows=(lw['norm_ffn_pre'],))
    (dgate, dm1, dyv, upool, do, delta, g['dgate_b'], g['dln_g'], g['dln_b'], g['dsg_w'], g['dsg_b'], g['dconv_w'],
     g['dwbd'], g['dpool_scale'], g['dw_br_mla_p'], g['dw_br_sg'], g['dw_br_conv'], g['dw_br_pool'], g['dw_out'],
     g['dg_post']) = _mix_bwd(dx1, sv['proj'], sv['o'], lw)
    dm2 = _shift_bwd(dyv, upool, sv['proj'], lw['conv_w8'])
    (dq, dk, dv), carried_out = _attn_bwd(sv['q'], sv['k'], sv['v'], do, sv['lse'], delta, carried)
    dpb, g['dwq'], g['dwkv'], g['dq_norm'], g['dkv_norm'] = _qkv_bwd(
        dq, dk, dv, sv['proj'], lw['q_norm'], lw['kv_norm'], lw['wq'], lw['wkv'], *tabs)
    pieces = [(dgate, lw['wt_g']), (dm1, lw['wt_m1']), (dm2, lw['wt_m2']), (dpb, lw['wt_b'])]
    dx0, g['dg_pre'] = _norm_in_bwd("in_proj_bwd", pieces, sv['x0'], lw['norm_mix_pre'], dx1)
    g['dw_in_pieces'] = [_mm_tn("dw_in_%d" % n, sv['x0'], d, tm=512, tn=1024, prologue=_rms, rows=(lw['norm_mix_pre'],))
                         for n, (d, _) in enumerate(pieces)]
    return dx0, g, carried_out


class _StepPlan(NamedTuple):
    n_layers: int
    weights_exchange: Callable
    weights_from: Callable
    grads_ready: Callable
    exchange_done: Callable


def _local_step(x, positions, target, plan):
    tabs = _rope_tables(positions)
    derive = lambda w: {n: a[0] for n, a in _derive_weights(w).items()}
    first = plan.weights_exchange(0)
    weights = plan.weights_from(0, None if first is None else _run_exchange("gather_weights", first))
    derived, saved = [], []
    for l in range(plan.n_layers):
        derived.append(derive(weights))
        coming = plan.weights_exchange(l + 1) if l + 1 < plan.n_layers else None
        x, sv, arrived = _layer_forward(x, derived[l], tabs, coming)
        saved.append(sv)
        if l + 1 < plan.n_layers:
            weights = plan.weights_from(l + 1, arrived)
    loss, dx = _loss_and_grad(x, target)
    pending = None
    for l in reversed(range(plan.n_layers)):
        dx, g, arrived = _layer_backward(dx, derived[l], saved[l], tabs, None if pending is None else pending[1])
        if pending is not None:
            plan.exchange_done(pending[0], arrived)
        lead = lambda a: [b[None] for b in a] if isinstance(a, list) else a[None]
        going = plan.grads_ready(l, _reference_layout_grads({n: lead(a) for n, a in g.items()}))
        pending = None if going is None else (l, going)
    if pending is not None:
        plan.exchange_done(pending[0], _run_exchange("grads_to_chips", pending[1]))
    return loss, dx


def _relative_peers():
    x, y = lax.axis_index("x"), lax.axis_index("y")
    return {1: (x, 1 - y), 2: (1 - x, y), 3: (1 - x, 1 - y)}


def _for_my_core(fn):
    def run():
        for half in (0, 1):
            pl.when(lax.axis_index("c") == half)(functools.partial(fn, half))
    return run


def _gather_exchange(split, whole):
    ns, nw = len(split), len(whole)
    n = ns + nw

    def phases(ins, outs, sems):
        ici_send, ici_recv, d2d_send, d2d_recv, own_send, own_recv = sems
        x, y, c = lax.axis_index("x"), lax.axis_index("y"), lax.axis_index("c")
        peers = _relative_peers()

        def rows(ref, which):
            h = ref.shape[-2] // 2
            return ref.at[(slice(None),) * (len(ref.shape) - 2) + (slice(which * h, (which + 1) * h), slice(None))]

        def own(k):
            return pltpu.make_async_remote_copy(src_ref=ins[k], dst_ref=outs[k].at[0], send_sem=own_send.at[k],
                                                recv_sem=own_recv.at[k], device_id=(x, y, 1 - c), device_id_type=MESH)

        def over_ici(k, r, half):
            src = rows(ins[k], half) if k < ns else ins[k]
            dst = rows(outs[k].at[r], half) if k < ns else outs[k].at[r]
            return pltpu.make_async_remote_copy(src_ref=src, dst_ref=dst, send_sem=ici_send.at[3 * k + r - 1],
                                                recv_sem=ici_recv.at[3 * k + r - 1], device_id=(*peers[r], c), device_id_type=MESH)

        def to_sibling(k, r, half):
            landed = rows(outs[k].at[r], half)
            return pltpu.make_async_remote_copy(src_ref=landed, dst_ref=landed, send_sem=d2d_send.at[3 * k + r - 1],
                                                recv_sem=d2d_recv.at[3 * k + r - 1], device_id=(x, y, 1 - c), device_id_type=MESH)

        def start(half):
            for k in range(n):
                own(k).start()
                for r in peers:
                    over_ici(k, r, half).start()

        def middle(half):
            for k in range(n):
                for r in peers:
                    over_ici(k, r, half).wait_recv()
                    if k < ns:
                        to_sibling(k, r, half).start()

        def finish(half):
            for k in range(n):
                own(k).wait()
                for r in peers:
                    if k < ns:
                        to_sibling(k, r, 1 - half).wait_recv()
                        to_sibling(k, r, half).wait_send()
                    over_ici(k, r, half).wait_send()

        return _for_my_core(start), _for_my_core(middle), _for_my_core(finish)

    arrs = list(split) + list(whole)
    return _Exchange(
        operands=arrs, out_shape=[jax.ShapeDtypeStruct((4,) + a.shape, a.dtype) for a in arrs],
        scratch_shapes=[pltpu.SemaphoreType.DMA((3 * n,)), pltpu.SemaphoreType.DMA((3 * n,)), pltpu.SemaphoreType.DMA((3 * ns,)),
                        pltpu.SemaphoreType.DMA((3 * ns,)), pltpu.SemaphoreType.DMA((n,)), pltpu.SemaphoreType.DMA((n,))],
        phases=phases)


def _absolute_chip_order(relative):
    me = 2 * lax.axis_index("x") + lax.axis_index("y")
    return jnp.stack([lax.dynamic_index_in_dim(relative, jnp.bitwise_xor(me, chip), 0, keepdims=False) for chip in range(4)])


REDUCE_STEPS = 8


def _sibling_halves(name, arrs):
    n = len(arrs)

    def body(*refs):
        ins, theirs = refs[:n], refs[n:2 * n]
        send_sems, recv_sems = refs[2 * n:]
        x, y, c = lax.axis_index("x"), lax.axis_index("y"), lax.axis_index("c")

        def exchange(my_half):
            copies = []
            for k in range(n):
                h = ins[k].shape[1] // 2
                cp = pltpu.make_async_remote_copy(src_ref=ins[k].at[:, (1 - my_half) * h:(2 - my_half) * h, :], dst_ref=theirs[k],
                                                  send_sem=send_sems.at[k], recv_sem=recv_sems.at[k],
                                                  device_id=(x, y, 1 - c), device_id_type=MESH)
                cp.start()
                copies.append(cp)
            for cp in copies:
                cp.wait()

        for half in (0, 1):
            pl.when(c == half)(functools.partial(exchange, half))

    return pl.pallas_call(
        body, name=name, in_specs=[HBM] * n, out_specs=[HBM] * n,
        out_shape=[jax.ShapeDtypeStruct((a.shape[0], a.shape[1] // 2, a.shape[2]), a.dtype) for a in arrs],
        scratch_shapes=[pltpu.SemaphoreType.DMA((n,)), pltpu.SemaphoreType.DMA((n,))],
    )(*arrs)


def _add_sibling(name, arrs, theirs):
    n, steps = len(arrs), REDUCE_STEPS

    def body(*refs):
        for mine_ref, theirs_ref, out_ref in zip(refs[:n], refs[n:2 * n], refs[2 * n:]):
            out_ref[...] = (mine_ref[...] + theirs_ref[...]).astype(out_ref.dtype)

    block = lambda t: (4, t.shape[1] // steps, t.shape[2])
    return pl.pallas_call(
        body, name=name, grid=(steps,),
        in_specs=[pl.BlockSpec(block(t), lambda i: (0, lax.axis_index("c") * steps + i, 0)) for t in theirs]
        + [pl.BlockSpec(block(t), lambda i: (0, i, 0)) for t in theirs],
        out_specs=[pl.BlockSpec(block(t), lambda i: (0, i, 0)) for t in theirs],
        out_shape=[jax.ShapeDtypeStruct(t.shape, WIRE_DTYPE) for t in theirs],
        compiler_params=_params(("parallel",)))(*arrs, *theirs)


def _scatter_exchange(arrs):
    n = len(arrs)

    def phases(ins, outs, sems):
        send_sems, recv_sems = sems
        c = lax.axis_index("c")
        peers = _relative_peers()

        def copy(k, r):
            px, py = peers[r]
            return pltpu.make_async_remote_copy(src_ref=ins[k].at[2 * px + py], dst_ref=outs[k].at[r - 1],
                                                send_sem=send_sems.at[3 * k + r - 1], recv_sem=recv_sems.at[3 * k + r - 1],
                                                device_id=(px, py, c), device_id_type=MESH)

        def start():
            for k in range(n):
                for r in peers:
                    copy(k, r).start()

        def finish():
            for k in range(n):
                for r in peers:
                    copy(k, r).wait()

        return start, lambda: None, finish

    return _Exchange(operands=list(arrs), out_shape=[jax.ShapeDtypeStruct((3,) + a.shape[1:], a.dtype) for a in arrs],
                     scratch_shapes=[pltpu.SemaphoreType.DMA((3 * n,)), pltpu.SemaphoreType.DMA((3 * n,))], phases=phases)


def _sum_chips(name, chip_sums, arrived):
    n, steps = len(chip_sums), REDUCE_STEPS

    def body(*refs):
        for own_ref, arrived_ref, out_ref in zip(refs[:n], refs[n:2 * n], refs[2 * n:]):
            acc = own_ref[...].astype(F32)
            for r in range(3):
                acc = acc + arrived_ref[r].astype(F32)
            out_ref[...] = acc

    rows = lambda s: s.shape[1] // steps
    chip = lambda: 2 * lax.axis_index("x") + lax.axis_index("y")
    return pl.pallas_call(
        body, name=name, grid=(steps,),
        in_specs=[pl.BlockSpec((None, rows(s), s.shape[2]), lambda i: (chip(), i, 0)) for s in chip_sums]
        + [pl.BlockSpec((3, rows(s), s.shape[2]), lambda i: (0, i, 0)) for s in chip_sums],
        out_specs=[pl.BlockSpec((rows(s), s.shape[2]), lambda i: (lax.axis_index("c") * steps + i, 0)) for s in chip_sums],
        out_shape=[jax.ShapeDtypeStruct((2 * s.shape[1], s.shape[2]), F32) for s in chip_sums],
        compiler_params=_params(("parallel",)))(*chip_sums, *arrived)


def _join_siblings(name, bufs):
    n = len(bufs)

    def body(*refs):
        outs = refs[n:2 * n]
        send_sems, recv_sems = refs[2 * n:]
        x, y, c = lax.axis_index("x"), lax.axis_index("y"), lax.axis_index("c")

        def exchange(my_half):
            copies = []
            for k in range(n):
                h = outs[k].shape[0] // 2
                mine = outs[k].at[my_half * h:(my_half + 1) * h, :]
                theirs = outs[k].at[(1 - my_half) * h:(2 - my_half) * h, :]
                cp = pltpu.make_async_remote_copy(src_ref=mine, dst_ref=mine, send_sem=send_sems.at[k],
                                                  recv_sem=recv_sems.at[k], device_id=(x, y, 1 - c), device_id_type=MESH)
                cp.start()
                arrival = pltpu.make_async_remote_copy(src_ref=theirs, dst_ref=theirs, send_sem=send_sems.at[k],
                                                       recv_sem=recv_sems.at[k], device_id=(x, y, 1 - c), device_id_type=MESH)
                copies.append((cp, arrival))
            for cp, arrival in copies:
                arrival.wait_recv()
                cp.wait_send()

        for half in (0, 1):
            pl.when(c == half)(functools.partial(exchange, half))

    return pl.pallas_call(
        body, name=name, in_specs=[HBM] * n, out_specs=[HBM] * n,
        out_shape=[jax.ShapeDtypeStruct(b.shape, b.dtype) for b in bufs], input_output_aliases={k: k for k in range(n)},
        scratch_shapes=[pltpu.SemaphoreType.DMA((n,)), pltpu.SemaphoreType.DMA((n,))],
    )(*bufs)


def _gather_all(name, a):
    def body(a_ref, out_ref, staging, send_sems, recv_sems, local_sem):
        x, y, c = lax.axis_index("x"), lax.axis_index("y"), lax.axis_index("c")
        me = 4 * x + 2 * y + c
        flips = [(fx, fy, fc) for fx in (0, 1) for fy in (0, 1) for fc in (0, 1)][1:]
        peers = [(x ^ fx, y ^ fy, c ^ fc) for fx, fy, fc in flips]
        load = pltpu.make_async_copy(a_ref, staging, local_sem)
        load.start()
        load.wait()
        local = pltpu.make_async_copy(staging, out_ref.at[me], local_sem)
        local.start()
        sends = []
        for j, peer in enumerate(peers):
            cp = pltpu.make_async_remote_copy(src_ref=a_ref, dst_ref=out_ref.at[me], send_sem=send_sems.at[j],
                                              recv_sem=recv_sems.at[j], device_id=peer, device_id_type=MESH)
            cp.start()
            sends.append(cp)
        for j, (px, py, pc) in enumerate(peers):
            pltpu.make_async_remote_copy(src_ref=a_ref, dst_ref=out_ref.at[4 * px + 2 * py + pc], send_sem=send_sems.at[j],
                                         recv_sem=recv_sems.at[j], device_id=(px, py, pc), device_id_type=MESH).wait_recv()
        for cp in sends:
            cp.wait_send()
        local.wait()

    return pl.pallas_call(
        body, name=name, in_specs=[HBM], out_specs=HBM, out_shape=jax.ShapeDtypeStruct((8,) + a.shape, a.dtype),
        scratch_shapes=[pltpu.VMEM(a.shape, a.dtype), pltpu.SemaphoreType.DMA((7,)), pltpu.SemaphoreType.DMA((7,)),
                        pltpu.SemaphoreType.DMA],
    )(a)


def _rowwise_call(name, fn, slots, out_shapes, steps):
    n_in, n_out = [len(s) for s in slots], [len(o) for o in out_shapes]

    def spec(shape):
        if len(shape) == 3:
            return pl.BlockSpec((shape[0], shape[1] // steps, shape[2]), lambda i: (0, i, 0))
        return pl.BlockSpec((shape[0] // steps, shape[1]), lambda i: (i, 0))

    def body(*refs):
        ins, outs = refs[:sum(n_in)], refs[sum(n_in):]
        a = b = 0
        for k in range(len(slots)):
            for o_ref, val in zip(outs[b:b + n_out[k]], fn(*[r[...] for r in ins[a:a + n_in[k]]])):
                o_ref[...] = val
            a, b = a + n_in[k], b + n_out[k]

    flat_in = [arr for s in slots for arr in s]
    flat_out = [shp for o in out_shapes for shp in o]
    out = pl.pallas_call(body, name=name, grid=(steps,), in_specs=[spec(a.shape) for a in flat_in],
                         out_specs=[spec(s) for s in flat_out], out_shape=[jax.ShapeDtypeStruct(s, F32) for s in flat_out],
                         compiler_params=_params(("parallel",)))(*flat_in)
    grouped, b = [], 0
    for k in range(len(slots)):
        grouped.append(out[b:b + n_out[k]])
        b += n_out[k]
    return grouped


def _sum_in_order(a):
    acc = a[0].astype(F32)
    for k in range(1, a.shape[0]):
        acc = acc + a[k].astype(F32)
    return (acc,)


def _adamw_math(w, g, m, v):
    m_new = ADAM_B1 * m + (1.0 - ADAM_B1) * g
    v_new = ADAM_B2 * v + (1.0 - ADAM_B2) * (g * g)
    m_hat = m_new / (1.0 - ADAM_B1 ** ADAM_STEP)
    v_hat = v_new / (1.0 - ADAM_B2 ** ADAM_STEP)
    return -ADAM_LR * (m_hat / (jnp.sqrt(v_hat) + ADAM_EPS) + ADAM_WD * w), m_new, v_new


SMALL_PACK_COLS = 256
SMALL_PACK_ROWS = 2048


def _pack_small(parts):
    wide = [jnp.pad(p, ((0, 0), (0, 0), (0, SMALL_PACK_COLS - p.shape[2]))) for p in parts]
    rows = jnp.concatenate(wide, axis=1)
    return jnp.pad(rows, ((0, 0), (0, SMALL_PACK_ROWS - rows.shape[1]), (0, 0)))


def _unpack_small(packed, shapes):
    out, row = [], 0
    for a, b in shapes:
        out.append(packed[:, row:row + a, :b])
        row += a
    return out


def _pack(arrs, rows_per_layer, dtype):
    nl = arrs[0].shape[0]
    flat = jnp.concatenate([a.astype(dtype).reshape(nl, -1) for a in arrs], axis=1)
    flat = jnp.pad(flat, ((0, 0), (0, rows_per_layer * PACK_COLS - flat.shape[1])))
    return flat.reshape(nl * rows_per_layer, PACK_COLS)


def _unpack(packed, shapes, rows_per_layer):
    nl = shapes[0][0]
    flat = packed.reshape(packed.shape[:-2] + (nl, rows_per_layer * PACK_COLS))
    out, off = [], 0
    for shp in shapes:
        size = math.prod(shp[1:])
        out.append(flat[..., off:off + size].reshape(packed.shape[:-2] + tuple(shp)))
        off += size
    return out


def _rows_needed(shapes, multiple):
    per_layer = sum(math.prod(s[1:]) for s in shapes)
    rows = -(-per_layer // PACK_COLS)
    return -(-rows // multiple) * multiple


CONV_TILE = (8, 128)


def _layer_shard_exchange(w, l):
    conv = w['conv_w'][l].reshape(-1)
    conv = jnp.pad(conv, (0, math.prod(CONV_TILE) - conv.shape[0])).reshape(CONV_TILE)
    return _gather_exchange([w[n][l].astype(MXU_DTYPE) for n in MATMUL_SHARDED], [conv])


def _layer_full_weights(w, l, gathered):
    gathered = [_absolute_chip_order(g) for g in gathered]
    full = {n: w[n][l:l + 1] for n in WEIGHTS}
    for n, part in zip(MATMUL_SHARDED, gathered):
        if n in ROW_SHARDED:
            full[n] = part.reshape(1, 4 * part.shape[1], part.shape[2])
        else:
            full[n] = jnp.swapaxes(part, 0, 1).reshape(1, part.shape[1], 4 * part.shape[2])
    rows, cols = w['conv_w'].shape[1:]
    conv = gathered[-1].reshape(4, -1)[:, :rows * cols].reshape(4, rows, cols)
    full['conv_w'] = jnp.swapaxes(conv, 0, 1).reshape(1, rows, 4 * cols)
    return full


def _chip_major(n, g):
    nl = g.shape[0]
    if n in ROW_SHARDED:
        return jnp.swapaxes(g.reshape(nl, 4, g.shape[1] // 4, g.shape[2]), 0, 1)
    return jnp.transpose(g.reshape(nl, g.shape[1], 4, g.shape[2] // 4), (2, 0, 1, 3))


def kernel(x, positions, norm_mix_pre, w_in, gate_b, q_norm, w_uq, kv_norm, w_ukv, w_br_mla, sg_ln_g, sg_ln_b, sg_w, sg_b, w_br_sg, conv_w, w_br_conv, pool_w, pool_scale, w_br_pool, w_out, norm_mix_post, norm_ffn_pre, w_ff1, w_ff2, norm_ffn_post, loss_target, m_norm_mix_pre, m_w_in, m_gate_b, m_q_norm, m_w_uq, m_kv_norm, m_w_ukv, m_w_br_mla, m_sg_ln_g, m_sg_ln_b, m_sg_w, m_sg_b, m_w_br_sg, m_conv_w, m_w_br_conv, m_pool_w, m_pool_scale, m_w_br_pool, m_w_out, m_norm_mix_post, m_norm_ffn_pre, m_w_ff1, m_w_ff2, m_norm_ffn_post, v_norm_mix_pre, v_w_in, v_gate_b, v_q_norm, v_w_uq, v_kv_norm, v_w_ukv, v_w_br_mla, v_sg_ln_g, v_sg_ln_b, v_sg_w, v_sg_b, v_w_br_sg, v_conv_w, v_w_br_conv, v_pool_w, v_pool_scale, v_w_br_pool, v_w_out, v_norm_mix_post, v_norm_ffn_pre, v_w_ff1, v_w_ff2, v_norm_ffn_post):
    given = dict(locals())
    w = {n: given[n] for n in WEIGHTS}
    mom = {n: given['m_' + n] for n in WEIGHTS}
    var = {n: given['v_' + n] for n in WEIGHTS}
    nl = w['w_in'].shape[0]
    chip_sums, reduced, local_small = {}, {}, {}

    def grads_ready(l, g):
        arrs = [_chip_major(n, g[n])[:, 0] for n in BIG_SHARDED]
        arrs.append(_pack_small([_chip_major(n, g[n])[:, 0] for n in SMALL_SHARDED]))
        chip_sums[l] = _add_sibling("add_sibling", arrs, _sibling_halves("grads_to_sibling", arrs))
        local_small[l] = [g[n] for n in REPLICATED + ['conv_w']]
        return _scatter_exchange(chip_sums[l])

    def exchange_done(l, arrived):
        reduced[l] = _join_siblings("join_halves", _sum_chips("sum_chips", chip_sums[l], arrived))

    plan = _StepPlan(n_layers=nl, weights_exchange=functools.partial(_layer_shard_exchange, w),
                     weights_from=functools.partial(_layer_full_weights, w), grads_ready=grads_ready,
                     exchange_done=exchange_done)
    loss, dx = _local_step(x[0], positions[0], loss_target[0], plan)
    loss = lax.psum(loss, ("x", "y", "c"))

    grad, delta, new_m, new_v = {}, {}, {}, {}
    for k, n in enumerate(BIG_SHARDED):
        grad[n] = jnp.stack([reduced[l][k] for l in range(nl)])
    slots = [[w[n], grad[n], mom[n], var[n]] for n in BIG_SHARDED]
    small_pack = lambda d: _pack_small([d[n] for n in SMALL_SHARDED])
    g_small = jnp.stack([reduced[l][-1] for l in range(nl)])
    slots.append([small_pack(w), g_small, small_pack(mom), small_pack(var)])
    updated = _rowwise_call("adamw_sharded", _adamw_math, slots, [[s_[0].shape] * 3 for s_ in slots], 32)
    for k, n in enumerate(BIG_SHARDED):
        delta[n], new_m[n], new_v[n] = updated[k]
    for d, packed in zip((grad, delta, new_m, new_v), [g_small] + list(updated[-1])):
        d.update(zip(SMALL_SHARDED, _unpack_small(packed, [w[n].shape[1:] for n in SMALL_SHARDED])))

    names = REPLICATED + ['conv_w']
    local = [jnp.concatenate([local_small[l][k] for l in range(nl)]) for k in range(len(names))]
    rows = _rows_needed([a.shape for a in local], 32)
    everyone = _gather_all("gather_small_grads", _pack(local, rows, F32))
    (summed,), = _rowwise_call("sum_devices", _sum_in_order, [[everyone]], [[everyone.shape[1:]]], 4)
    g_rep = _unpack(summed, [a.shape for a in local], rows)
    chip = 2 * lax.axis_index("x") + lax.axis_index("y")
    g_rep[-1] = lax.dynamic_slice_in_dim(g_rep[-1], chip * w['conv_w'].shape[2], w['conv_w'].shape[2], axis=2)
    rep_pack = lambda arrs: _pack(arrs, rows, F32)
    (rep_out,) = _rowwise_call("adamw_replicated", _adamw_math,
                               [[rep_pack([w[n] for n in names]), rep_pack(g_rep), rep_pack([mom[n] for n in names]),
                                 rep_pack([var[n] for n in names])]], [[(nl * rows, PACK_COLS)] * 3], 4)
    grad.update(zip(names, g_rep))
    for d, packed in zip((delta, new_m, new_v), rep_out):
        d.update(zip(names, _unpack(packed, [w[n].shape for n in names], rows)))

    return (loss, dx[None], *[grad[n] for n in WEIGHTS], *[delta[n] for n in WEIGHTS], *[new_m[n] for n in WEIGHTS],
            *[new_v[n] for n in WEIGHTS])
```

```python
import functools
import math
from typing import Any, Callable, NamedTuple, Sequence

import jax
import jax.numpy as jnp
from jax import lax
from jax.experimental import pallas as pl
from jax.experimental.pallas import tpu as pltpu

F32 = jnp.float32
MXU_DTYPE = jnp.bfloat16
WIRE_DTYPE = jnp.bfloat16
ACT_DTYPE = jnp.bfloat16
MESH = pl.DeviceIdType.MESH

D_MODEL = 1024
D_FF = 4096
N_HEADS = 4
QK_NOPE = 64
QK_ROPE = 32
V_HEAD = 64
HEAD_PAD = 128
Q_RANK = 256
KV_RANK = 128
SG_CHUNK = 128
SG_GROUPS = 4
BR_WIDTH = 256
N_BRANCH = 4
POOL_WINDOWS = (2, 4, 8, 16)
HALO = 16
ROPE_BASE = 10000.0
EPS = 1e-6
ATTN_SCALE = (QK_NOPE + QK_ROPE) ** -0.5
LOG2_E = math.log2(math.e)
FWD_HEADS_PER_STEP = 4
HEADS_PER_STEP = 2
N_PROJ = N_BRANCH * D_MODEL + 6 * BR_WIDTH + Q_RANK + KV_RANK + 2 * HEAD_PAD
COL_G, COL_M1, COL_M2, COL_B = 0, 4096, 4864, 5632

ADAM_LR, ADAM_B1, ADAM_B2, ADAM_EPS, ADAM_WD, ADAM_STEP = 0.001, 0.9, 0.999, 1e-08, 0.01, 10

VMEM_LIMIT = 56 * 1024 * 1024

WEIGHTS = ['norm_mix_pre', 'w_in', 'gate_b', 'q_norm', 'w_uq', 'kv_norm', 'w_ukv', 'w_br_mla', 'sg_ln_g', 'sg_ln_b',
           'sg_w', 'sg_b', 'w_br_sg', 'conv_w', 'w_br_conv', 'pool_w', 'pool_scale', 'w_br_pool', 'w_out',
           'norm_mix_post', 'norm_ffn_pre', 'w_ff1', 'w_ff2', 'norm_ffn_post']
COL_SHARDED = ['w_in', 'w_uq', 'w_ukv', 'w_br_mla', 'w_br_sg', 'w_br_conv', 'w_br_pool', 'w_ff1']
ROW_SHARDED = ['w_out', 'w_ff2']
MATMUL_SHARDED = ['w_in', 'w_uq', 'w_ukv', 'w_br_mla', 'w_br_sg', 'w_br_conv', 'w_br_pool', 'w_out', 'w_ff1', 'w_ff2']
BIG_SHARDED = ['w_in', 'w_ff1', 'w_ff2', 'w_out']
SMALL_SHARDED = ['w_uq', 'w_ukv', 'w_br_mla', 'w_br_sg', 'w_br_conv', 'w_br_pool']
SHARDED = MATMUL_SHARDED + ['conv_w']
REPLICATED = [n for n in WEIGHTS if n not in SHARDED]
PACK_COLS = 1024


def _params(sem, vmem=VMEM_LIMIT):
    return pltpu.CompilerParams(dimension_semantics=sem, vmem_limit_bytes=vmem)


def _mxu(a):
    return a.astype(MXU_DTYPE)


def _dot(a, b):
    return jnp.dot(_mxu(a), _mxu(b), preferred_element_type=F32)


def _dot_nt(a, b):
    return lax.dot_general(_mxu(a), _mxu(b), (((1,), (1,)), ((), ())), preferred_element_type=F32)


def _dot_tn(a, b):
    return lax.dot_general(_mxu(a), _mxu(b), (((0,), (0,)), ((), ())), preferred_element_type=F32)


def _rms(x, g):
    r = lax.rsqrt(jnp.mean(x * x, axis=-1, keepdims=True) + EPS)
    return x * r * g


def _rms_bwd(x, g, dy):
    r = lax.rsqrt(jnp.mean(x * x, axis=-1, keepdims=True) + EPS)
    xh = x * r
    gdy = dy * g
    dx = r * (gdy - xh * jnp.mean(gdy * xh, axis=-1, keepdims=True))
    return dx, jnp.sum(dy * xh, axis=0, keepdims=True)


_GELU_C = math.sqrt(2.0 / math.pi)


def _gelu(x):
    t = jnp.tanh(_GELU_C * (x + 0.044715 * (x * x * x)))
    return x * (0.5 * (1.0 + t)), t


def _gelu_grad(x, t):
    return 0.5 * (1.0 + t) + 0.5 * x * (1.0 - t * t) * (_GELU_C * (1.0 + 3.0 * 0.044715 * x * x))


def _sigmoid(x):
    return 1.0 / (1.0 + jnp.exp(-x))


def _full(shape):
    return pl.BlockSpec(shape, lambda *_: (0,) * len(shape))


def _resident(shape):
    return pl.BlockSpec(shape, lambda *_: (0,) * len(shape), pipeline_mode=pl.Buffered(1))


def _rows(ts, width, col=0):
    return pl.BlockSpec((ts, width), lambda i: (i, col))


def _tile(n, pref):
    return min(n, pref)


def _mm(name, a, w, *, tm, tn, prologue=None, rows=()):
    m, k = a.shape
    n = w.shape[1]
    tm, tn = _tile(m, tm), _tile(n, tn)

    def body(a_ref, *rest):
        row_refs, w_ref, o_ref = rest[:len(rows)], rest[len(rows)], rest[len(rows) + 1]
        av = a_ref[...]
        if prologue is not None:
            av = prologue(av, *[r[...] for r in row_refs])
        o_ref[...] = _dot(av, w_ref[...]).astype(o_ref.dtype)

    return pl.pallas_call(
        body, name=name, grid=(m // tm, n // tn),
        in_specs=[pl.BlockSpec((tm, k), lambda i, j: (i, 0))] + [pl.BlockSpec((1, k), lambda i, j: (0, 0)) for _ in rows]
        + [pl.BlockSpec((k, tn), lambda i, j: (0, j))],
        out_specs=pl.BlockSpec((tm, tn), lambda i, j: (i, j)),
        out_shape=jax.ShapeDtypeStruct((m, n), ACT_DTYPE),
        compiler_params=_params(("parallel", "parallel")),
    )(a, *rows, w)


def _mm_tn(name, a, b, *, tm, tn, prologue=None, rows=()):
    m, k = a.shape
    n = b.shape[1]
    tm, tn = _tile(m, tm), _tile(n, tn)

    def body(a_ref, *rest):
        row_refs, b_ref, o_ref = rest[:len(rows)], rest[len(rows)], rest[len(rows) + 1]

        @pl.when(pl.program_id(1) == 0)
        def _():
            o_ref[...] = jnp.zeros_like(o_ref)

        av = a_ref[...]
        if prologue is not None:
            av = prologue(av, *[r[...] for r in row_refs])
        o_ref[...] += _dot_tn(av, b_ref[...])

    return pl.pallas_call(
        body, name=name, grid=(n // tn, m // tm),
        in_specs=[pl.BlockSpec((tm, k), lambda j, i: (i, 0))] + [pl.BlockSpec((1, k), lambda j, i: (0, 0)) for _ in rows]
        + [pl.BlockSpec((tm, tn), lambda j, i: (i, j))],
        out_specs=pl.BlockSpec((k, tn), lambda j, i: (0, j)),
        out_shape=jax.ShapeDtypeStruct((k, n), F32),
        compiler_params=_params(("parallel", "arbitrary")),
    )(a, *rows, b)


def _relu_sq(a):
    r = jnp.maximum(a.astype(F32), 0.0)
    return r * r


HBM = pl.BlockSpec(memory_space=pl.ANY)


class _Exchange(NamedTuple):
    operands: Sequence[Any]
    out_shape: Sequence[Any]
    scratch_shapes: Sequence[Any]
    phases: Callable


def _run_exchange(name, ex):
    n_in, n_out = len(ex.operands), len(ex.out_shape)

    def body(*refs):
        for phase in ex.phases(refs[:n_in], refs[n_in:n_in + n_out], refs[n_in + n_out:]):
            phase()

    return pl.pallas_call(body, name=name, in_specs=[HBM] * n_in, out_specs=[HBM] * n_out, out_shape=list(ex.out_shape),
                          scratch_shapes=list(ex.scratch_shapes))(*ex.operands)


def _call_carrying(body, carried, *, name, grid, in_specs, out_specs, out_shape, operands, semantics):
    if carried is None:
        return pl.pallas_call(body, name=name, grid=grid, in_specs=in_specs, out_specs=out_specs, out_shape=out_shape,
                              compiler_params=_params(semantics))(*operands), None
    n_main_in, n_main_out = len(operands), len(out_shape)
    n_in, n_out = len(carried.operands), len(carried.out_shape)
    steps = math.prod(grid)

    def wrapped(*refs):
        main_in, refs = refs[:n_main_in], refs[n_main_in:]
        ex_in, refs = refs[:n_in], refs[n_in:]
        main_out, refs = refs[:n_main_out], refs[n_main_out:]
        ex_out, sems = refs[:n_out], refs[n_out:]
        step = pl.program_id(0)
        for axis in range(1, len(grid)):
            step = step * grid[axis] + pl.program_id(axis)
        start, middle, finish = carried.phases(ex_in, ex_out, sems)
        pl.when(step == 0)(start)
        pl.when(step == (steps - 1) // 2)(middle)
        body(*main_in, *main_out)
        pl.when(step == steps - 1)(finish)

    out = pl.pallas_call(
        wrapped, name=name + "_carrying", grid=grid, in_specs=list(in_specs) + [HBM] * n_in,
        out_specs=list(out_specs) + [HBM] * n_out, out_shape=list(out_shape) + list(carried.out_shape),
        scratch_shapes=list(carried.scratch_shapes), compiler_params=_params(("arbitrary",) * len(grid)))(*operands, *carried.operands)
    return out[:n_main_out], out[n_main_out:]


def _qkv_prep(proj, q_norm, kv_norm, wq, wkv, cq_tab, s_tab, cr_tab):
    s = proj.shape[0]
    ts = _tile(s, 512)
    hw = N_HEADS * HEAD_PAD

    def body(cq_ref, ckv_ref, kr_ref, krs_ref, gq_ref, gkv_ref, wq_ref, wkv_ref, ct_ref, st_ref, crt_ref,
             q_ref, k_ref, v_ref):
        ct, st, crt = ct_ref[...], st_ref[...], crt_ref[...]
        qn = _rms(cq_ref[...].astype(F32), gq_ref[...])
        qab = _dot(qn, wq_ref[...])
        kvn = _rms(ckv_ref[...].astype(F32), gkv_ref[...])
        kav = _dot(kvn, wkv_ref[...])
        k_rope = kr_ref[...].astype(F32) * crt + krs_ref[...].astype(F32) * st
        ones_lane = (lax.broadcasted_iota(jnp.int32, (1, HEAD_PAD), 1) == V_HEAD).astype(F32)
        for h in range(N_HEADS):
            lo = h * HEAD_PAD
            q_ref[h] = (qab[:, lo:lo + HEAD_PAD] * ct + qab[:, hw + lo:hw + lo + HEAD_PAD] * st).astype(q_ref.dtype)
            k_ref[h] = (kav[:, lo:lo + HEAD_PAD] + k_rope).astype(k_ref.dtype)
            v_ref[h] = (kav[:, hw + lo:hw + lo + HEAD_PAD] + ones_lane).astype(v_ref.dtype)

    head_spec = pl.BlockSpec((N_HEADS, ts, HEAD_PAD), lambda i: (0, i, 0))
    head_shape = jax.ShapeDtypeStruct((N_HEADS, s, HEAD_PAD), MXU_DTYPE)
    return pl.pallas_call(
        body, name="qkv_prep", grid=(s // ts,),
        in_specs=[_rows(ts, Q_RANK, COL_B // Q_RANK), _rows(ts, KV_RANK, (COL_B + Q_RANK) // KV_RANK),
                  _rows(ts, HEAD_PAD, (COL_B + Q_RANK + KV_RANK) // HEAD_PAD),
                  _rows(ts, HEAD_PAD, (COL_B + Q_RANK + KV_RANK + HEAD_PAD) // HEAD_PAD),
                  _full((1, Q_RANK)), _full((1, KV_RANK)), _full((Q_RANK, 2 * hw)), _full((KV_RANK, 2 * hw)),
                  _rows(ts, HEAD_PAD), _rows(ts, HEAD_PAD), _rows(ts, HEAD_PAD)],
        out_specs=[head_spec, head_spec, head_spec],
        out_shape=[head_shape, head_shape, head_shape],
        compiler_params=_params(("parallel",)),
    )(proj, proj, proj, proj, q_norm, kv_norm, wq, wkv, cq_tab, s_tab, cr_tab)


def _diagonal_mask(t):
    return lax.broadcasted_iota(jnp.int32, (t, t), 1) <= lax.broadcasted_iota(jnp.int32, (t, t), 0)


def _attn_fwd(q, k, v, carried=None):
    s = q.shape[1]
    t = _tile(s, 512)

    def body(q_ref, k_ref, v_ref, o_ref, lse_ref):
        i = pl.program_id(1)
        lane = lax.broadcasted_iota(jnp.int32, (1, HEAD_PAD), 1)

        def step(j, carry, on_diagonal):
            rows = pl.ds(pl.multiple_of(j * t, t), t)
            out = []
            for h in range(FWD_HEADS_PER_STEP):
                m, acc = carry[h]
                sc = _dot_nt(q_ref[h], k_ref[h, rows, :]) * (ATTN_SCALE * LOG2_E)
                if on_diagonal:
                    sc = jnp.where(_diagonal_mask(t), sc, -jnp.inf)
                m_new = jnp.maximum(m, jnp.max(sc, axis=1, keepdims=True))
                out.append((m_new, jnp.exp2(m - m_new) * acc + _dot(jnp.exp2(sc - m_new), v_ref[h, rows, :])))
            return tuple(out)

        init = ((jnp.full((t, 1), -jnp.inf, F32), jnp.zeros((t, HEAD_PAD), F32)),) * FWD_HEADS_PER_STEP
        below = lax.fori_loop(0, i, functools.partial(step, on_diagonal=False), init)
        for h, (m, acc) in enumerate(step(i, below, True)):
            l = jnp.sum(jnp.where(lane == V_HEAD, acc, 0.0), axis=1, keepdims=True)
            o_ref[:, h * HEAD_PAD:(h + 1) * HEAD_PAD] = jnp.where(lane < V_HEAD, acc / l, 0.0)
            lse_ref[h] = m + jnp.log2(l)

    group = FWD_HEADS_PER_STEP
    return _call_carrying(
        body, carried, name="attn_fwd", grid=(N_HEADS // group, s // t),
        in_specs=[pl.BlockSpec((group, t, HEAD_PAD), lambda h, i: (h, i, 0)),
                  pl.BlockSpec((group, s, HEAD_PAD), lambda h, i: (h, 0, 0)),
                  pl.BlockSpec((group, s, HEAD_PAD), lambda h, i: (h, 0, 0))],
        out_specs=[pl.BlockSpec((t, group * HEAD_PAD), lambda h, i: (i, h)), pl.BlockSpec((group, t, 1), lambda h, i: (h, i, 0))],
        out_shape=[jax.ShapeDtypeStruct((s, N_HEADS * HEAD_PAD), F32), jax.ShapeDtypeStruct((N_HEADS, s, 1), F32)],
        operands=(q, k, v), semantics=("parallel", "parallel"))


def _attn_bwd(q, k, v, do, lse, delta, carried=None):
    s = q.shape[1]
    t = _tile(s, 512)
    nq = s // t

    def body(q_ref, do_ref, lse_ref, dl_ref, k_ref, v_ref, dq_ref, dk_ref, dv_ref):
        j = pl.program_id(1)

        @pl.when(j == 0)
        def _():
            dq_ref[...] = jnp.zeros_like(dq_ref)

        def step(i, carry, on_diagonal):
            rows = pl.ds(pl.multiple_of(i * t, t), t)
            dq = [dq_ref[h, rows, :] for h in range(HEADS_PER_STEP)]
            out = []
            for h in range(HEADS_PER_STEP):
                dk, dv = carry[h]
                qi, doi = q_ref[h, rows, :], do_ref[rows, h * HEAD_PAD:(h + 1) * HEAD_PAD]
                sc = _dot_nt(qi, k_ref[h]) * (ATTN_SCALE * LOG2_E)
                if on_diagonal:
                    sc = jnp.where(_diagonal_mask(t), sc, -jnp.inf)
                p = jnp.exp2(sc - lse_ref[h, rows, :])
                dv = dv + _dot_tn(p, doi)
                ds = p * (_dot_nt(doi, v_ref[h]) - dl_ref[h, rows, :])
                dk = dk + _dot_tn(ds, qi)
                dq[h] = dq[h] + _dot(ds, k_ref[h]) * ATTN_SCALE
                out.append((dk, dv))
            for h in range(HEADS_PER_STEP):
                dq_ref[h, rows, :] = dq[h]
            return tuple(out)

        zero = ((jnp.zeros((t, HEAD_PAD), F32),) * 2,) * HEADS_PER_STEP
        sums = lax.fori_loop(j + 1, nq, functools.partial(step, on_diagonal=False), step(j, zero, True))
        for h, (dk, dv) in enumerate(sums):
            dk_ref[h] = dk * ATTN_SCALE
            dv_ref[h] = dv

    group = HEADS_PER_STEP
    whole = lambda w: pl.BlockSpec((group, s, w), lambda h, j: (h, 0, 0), pipeline_mode=pl.Buffered(1))
    tile = pl.BlockSpec((group, t, HEAD_PAD), lambda h, j: (h, j, 0))
    head_shape = jax.ShapeDtypeStruct((N_HEADS, s, HEAD_PAD), F32)
    return _call_carrying(
        body, carried, name="attn_bwd", grid=(N_HEADS // group, nq),
        in_specs=[whole(HEAD_PAD), pl.BlockSpec((s, group * HEAD_PAD), lambda h, j: (0, h), pipeline_mode=pl.Buffered(1)),
                  whole(1), whole(1), tile, tile],
        out_specs=[whole(HEAD_PAD), tile, tile],
        out_shape=[head_shape, head_shape, head_shape],
        operands=(q, do, lse, delta, k, v), semantics=("parallel", "arbitrary"))


def _qkv_bwd(dq, dk, dv, proj, q_norm, kv_norm, wq, wkv, cq_tab, s_tab, cr_tab):
    s = proj.shape[0]
    ts = _tile(s, 512)
    hw = N_HEADS * HEAD_PAD

    def body(dq_ref, dk_ref, dv_ref, cq_ref, ckv_ref, gq_ref, gkv_ref, wq_ref, wkv_ref, ct_ref, st_ref, crt_ref,
             dpb_ref, dwq_ref, dwkv_ref, dgq_ref, dgkv_ref):
        @pl.when(pl.program_id(0) == 0)
        def _():
            for r in (dwq_ref, dwkv_ref, dgq_ref, dgkv_ref):
                r[...] = jnp.zeros_like(r)

        ct, st, crt = ct_ref[...], st_ref[...], crt_ref[...]
        dqs = [dq_ref[h] for h in range(N_HEADS)]
        dks = [dk_ref[h] for h in range(N_HEADS)]
        dqab = jnp.concatenate([d * ct for d in dqs] + [d * st for d in dqs], axis=1)
        dkav = jnp.concatenate(dks + [dv_ref[h] for h in range(N_HEADS)], axis=1)
        dk_sum = dks[0] + dks[1] + dks[2] + dks[3]
        cq, ckv, gq, gkv = cq_ref[...].astype(F32), ckv_ref[...].astype(F32), gq_ref[...], gkv_ref[...]
        dwq_ref[...] += _dot_tn(_rms(cq, gq), dqab)
        dwkv_ref[...] += _dot_tn(_rms(ckv, gkv), dkav)
        dcq, dgq = _rms_bwd(cq, gq, _dot_nt(dqab, wq_ref[...]))
        dckv, dgkv = _rms_bwd(ckv, gkv, _dot_nt(dkav, wkv_ref[...]))
        dgq_ref[...] += dgq
        dgkv_ref[...] += dgkv
        dpb_ref[...] = jnp.concatenate([dcq, dckv, dk_sum * crt, dk_sum * st], axis=1).astype(dpb_ref.dtype)

    head_spec = pl.BlockSpec((N_HEADS, ts, HEAD_PAD), lambda i: (0, i, 0))
    wb = Q_RANK + KV_RANK + 2 * HEAD_PAD
    return pl.pallas_call(
        body, name="qkv_bwd", grid=(s // ts,),
        in_specs=[head_spec, head_spec, head_spec,
                  _rows(ts, Q_RANK, COL_B // Q_RANK), _rows(ts, KV_RANK, (COL_B + Q_RANK) // KV_RANK),
                  _full((1, Q_RANK)), _full((1, KV_RANK)), _full((Q_RANK, 2 * hw)), _full((KV_RANK, 2 * hw)),
                  _rows(ts, HEAD_PAD), _rows(ts, HEAD_PAD), _rows(ts, HEAD_PAD)],
        out_specs=[_rows(ts, wb), _full((Q_RANK, 2 * hw)), _full((KV_RANK, 2 * hw)), _full((1, Q_RANK)), _full((1, KV_RANK))],
        out_shape=[jax.ShapeDtypeStruct((s, wb), MXU_DTYPE), jax.ShapeDtypeStruct((Q_RANK, 2 * hw), F32),
                   jax.ShapeDtypeStruct((KV_RANK, 2 * hw), F32), jax.ShapeDtypeStruct((1, Q_RANK), F32),
                   jax.ShapeDtypeStruct((1, KV_RANK), F32)],
        compiler_params=_params(("arbitrary",)),
    )(dq, dk, dv, proj, proj, q_norm, kv_norm, wq, wkv, cq_tab, s_tab, cr_tab)


def _lane_group(width):
    return lax.broadcasted_iota(jnp.int32, (1, width), 1) // (width // 4)


def _shift_down(a, k):
    return pltpu.roll(a, k, 0)


def _shift_up(a, k):
    return pltpu.roll(a, a.shape[0] - k, 0)


def _window_sums(xh, shift):
    s2 = xh + shift(xh, 1)
    s4 = s2 + shift(s2, 2)
    s8 = s4 + shift(s4, 4)
    s16 = s8 + shift(s8, 8)
    grp = _lane_group(xh.shape[1])
    return jnp.where(grp == 0, s2, jnp.where(grp == 1, s4, jnp.where(grp == 2, s8, s16)))


def _pool_count(i, ts):
    grp = _lane_group(BR_WIDTH)
    win = jnp.where(grp == 0, 2.0, jnp.where(grp == 1, 4.0, jnp.where(grp == 2, 8.0, 16.0)))
    t = (i * ts + lax.broadcasted_iota(jnp.int32, (ts, 1), 0)).astype(F32)
    return jnp.minimum(t + 1.0, win)


def _mix_forward(i, ts, r):
    f = {}
    act = lambda name: r[name][...].astype(F32)
    f['gates'] = _sigmoid(act('gate') + r['gate_b'][...])
    sgu, sgv = act('sgu'), act('sgv')
    f['sgu'], f['sgv'] = sgu, sgv
    u_act, f['tu'] = _gelu(sgu)
    vg, f['tv'] = _gelu(sgv)
    mu = jnp.mean(vg, axis=-1, keepdims=True)
    xc = vg - mu
    f['ln_r'] = lax.rsqrt(jnp.mean(xc * xc, axis=-1, keepdims=True) + EPS)
    f['ln_xh'] = xc * f['ln_r']
    vln = f['ln_xh'] * r['ln_g'][...] + r['ln_b'][...]
    tril = lax.broadcasted_iota(jnp.int32, (SG_CHUNK, SG_CHUNK), 1) <= lax.broadcasted_iota(jnp.int32, (SG_CHUNK, SG_CHUNK), 0)
    f['wm'] = [_mxu(jnp.where(tril, r['sg_w'][g], 0.0)) for g in range(SG_GROUPS)]
    f['tril'] = tril
    grp = _lane_group(BR_WIDTH)
    bias = r['sg_bias'][...]
    parts = []
    for ci in range(ts // SG_CHUNK):
        vc = vln[ci * SG_CHUNK:(ci + 1) * SG_CHUNK]
        sc = bias
        for g in range(SG_GROUPS):
            sc = sc + jnp.where(grp == g, _dot(f['wm'][g], vc), 0.0)
        parts.append(sc)
    f['vln'] = vln
    f['sg_s'] = parts[0] if len(parts) == 1 else jnp.concatenate(parts, axis=0)
    f['u_act'] = u_act
    out_b = u_act * f['sg_s']
    first = (i > 0).astype(F32)
    cvx, cvc, cvb = act('cvx'), act('cvc'), act('cvb')
    f['cvx'], f['cvc'], f['cvb'] = cvx, cvc, cvb
    zh = jnp.concatenate([act('hx') * act('hc') * first, cvc * cvx], axis=0)
    f['z1'] = _shift_down(zh, 1)[HALO:]
    f['z2'] = _shift_down(zh, 2)[HALO:]
    f['z0'] = zh[HALO:]
    f['yv'] = r['conv_w'][0:1, :] * f['z2'] + r['conv_w'][1:2, :] * f['z1'] + r['conv_w'][2:3, :] * f['z0']
    out_c = cvb * f['yv']
    p = act('pool')
    ph = jnp.concatenate([act('hp') * first, p], axis=0)
    f['cnt'] = _pool_count(i, ts)
    f['pooled'] = _window_sums(ph, _shift_down)[HALO:] / f['cnt'] - p
    f['mixed'] = _dot(f['pooled'], r['wbd'][...])
    out_d = f['mixed'] * r['pool_scale'][...]
    f['outs'] = [r['o'][...], out_b, out_c, out_d]
    f['ys'] = [_dot(f['outs'][b], r['w_br'][b][...]) for b in range(N_BRANCH)]
    merged = f['gates'][:, 0:D_MODEL] * f['ys'][0]
    for b in range(1, N_BRANCH):
        merged = merged + f['gates'][:, b * D_MODEL:(b + 1) * D_MODEL] * f['ys'][b]
    f['merged'] = merged
    f['mo'] = _dot(merged, r['w_out'][...])
    return f


_MIX_TILE_INPUTS = ['gate', 'sgu', 'sgv', 'cvb', 'cvx', 'cvc', 'pool', 'hx', 'hc', 'hp', 'o']
_MIX_WEIGHTS = ['gate_b', 'ln_g', 'ln_b', 'sg_w', 'sg_bias', 'conv_w', 'wbd', 'pool_scale', 'w_br0', 'w_br1', 'w_br2',
                'w_br3', 'w_out', 'g_post']


def _mix_specs(s, ts):
    c0 = COL_M1 // BR_WIDTH
    prev = lambda col: pl.BlockSpec((HALO, BR_WIDTH), lambda i: (jnp.maximum(i * (ts // HALO) - 1, 0), col))
    tiles = [_rows(ts, N_BRANCH * D_MODEL, 0), _rows(ts, BR_WIDTH, c0), _rows(ts, BR_WIDTH, c0 + 1), _rows(ts, BR_WIDTH, c0 + 2),
             _rows(ts, BR_WIDTH, c0 + 3), _rows(ts, BR_WIDTH, c0 + 4), _rows(ts, BR_WIDTH, c0 + 5),
             prev(c0 + 3), prev(c0 + 4), prev(c0 + 5), _rows(ts, N_HEADS * HEAD_PAD)]
    weights = [_full((1, N_BRANCH * D_MODEL)), _full((1, BR_WIDTH)), _full((1, BR_WIDTH)),
               _full((SG_GROUPS, SG_CHUNK, SG_CHUNK)), _full((SG_CHUNK, BR_WIDTH)), _full((8, BR_WIDTH)),
               _resident((BR_WIDTH, BR_WIDTH)), _full((1, BR_WIDTH)), _resident((N_HEADS * HEAD_PAD, D_MODEL)),
               _resident((BR_WIDTH, D_MODEL)), _resident((BR_WIDTH, D_MODEL)), _resident((BR_WIDTH, D_MODEL)),
               _resident((D_MODEL, D_MODEL)), _full((1, D_MODEL))]
    return tiles, weights


def _mix_refs(refs):
    names = _MIX_TILE_INPUTS + _MIX_WEIGHTS
    r = dict(zip(names, refs[:len(names)]))
    r['w_br'] = [r['w_br0'], r['w_br1'], r['w_br2'], r['w_br3']]
    return r, refs[len(names):]


def _mix_operands(proj, o, lw):
    return ([proj] * 10 + [o] + [lw[n] for n in ['gate_b', 'sg_ln_g', 'sg_ln_b', 'sg_w', 'sg_bias', 'conv_w8', 'wbd',
                                                 'pool_scale', 'w_br_mla_p', 'w_br_sg', 'w_br_conv', 'w_br_pool', 'w_out',
                                                 'norm_mix_post']])


def _mix_fwd(x0, proj, o, lw):
    s = x0.shape[0]
    ts = _tile(s, 512)
    tiles, weights = _mix_specs(s, ts)

    def body(*refs):
        r, (x0_ref, x1_ref) = _mix_refs(refs)
        f = _mix_forward(pl.program_id(0), ts, r)
        x1_ref[...] = x0_ref[...] + _rms(f['mo'], r['g_post'][...])

    return pl.pallas_call(
        body, name="mix_fwd", grid=(s // ts,),
        in_specs=tiles + weights + [_rows(ts, D_MODEL)],
        out_specs=_rows(ts, D_MODEL),
        out_shape=jax.ShapeDtypeStruct((s, D_MODEL), F32),
        compiler_params=_params(("parallel",)),
    )(*_mix_operands(proj, o, lw), x0)


def _mix_bwd(dx1, proj, o, lw):
    s = dx1.shape[0]
    ts = _tile(s, 256)
    tiles, weights = _mix_specs(s, ts)
    hw = N_HEADS * HEAD_PAD

    def body(*refs):
        r, rest = _mix_refs(refs)
        (dx1_ref, dg_ref, dm1_ref, dyv_ref, up_ref, do_ref, delta_ref,
         dgate_b_ref, dln_g_ref, dln_b_ref, dsgw_ref, dsgb_ref, dconv_ref, dwbd_ref, dps_ref,
         dwbr0_ref, dwbr1_ref, dwbr2_ref, dwbr3_ref, dwout_ref, dgpost_ref, dbias_acc) = rest
        i = pl.program_id(0)
        acc_refs = [dgate_b_ref, dln_g_ref, dln_b_ref, dsgw_ref, dsgb_ref, dconv_ref, dwbd_ref, dps_ref,
                    dwbr0_ref, dwbr1_ref, dwbr2_ref, dwbr3_ref, dwout_ref, dgpost_ref, dbias_acc]

        @pl.when(i == 0)
        def _():
            for a in acc_refs:
                a[...] = jnp.zeros_like(a)

        f = _mix_forward(i, ts, r)
        dmo, dgpost = _rms_bwd(f['mo'], r['g_post'][...], dx1_ref[...])
        dgpost_ref[...] += dgpost
        dwout_ref[...] += _dot_tn(f['merged'], dmo)
        dmerged = _dot_nt(dmo, r['w_out'][...])
        dwbr = [dwbr0_ref, dwbr1_ref, dwbr2_ref, dwbr3_ref]
        douts = []
        for b in range(N_BRANCH):
            gb = f['gates'][:, b * D_MODEL:(b + 1) * D_MODEL]
            dgate = dmerged * f['ys'][b] * gb * (1.0 - gb)
            dg_ref[:, b * D_MODEL:(b + 1) * D_MODEL] = dgate.astype(dg_ref.dtype)
            dgate_b_ref[:, b * D_MODEL:(b + 1) * D_MODEL] += jnp.sum(dgate, axis=0, keepdims=True)
            dy = dmerged * gb
            dwbr[b][...] += _dot_tn(f['outs'][b], dy)
            douts.append(_dot_nt(dy, r['w_br'][b][...]))
        do = douts[0]
        do_ref[...] = do.astype(do_ref.dtype)
        prod = do * f['outs'][0]
        for h in range(N_HEADS):
            delta_ref[h] = jnp.sum(prod[:, h * HEAD_PAD:(h + 1) * HEAD_PAD], axis=1, keepdims=True)
        grp = _lane_group(BR_WIDTH)
        ds = douts[1] * f['u_act']
        dsgu = douts[1] * f['sg_s'] * _gelu_grad(f['sgu'], f['tu'])
        dvln_parts = []
        for ci in range(ts // SG_CHUNK):
            rows = slice(ci * SG_CHUNK, (ci + 1) * SG_CHUNK)
            ds_c, vln_c = ds[rows], f['vln'][rows]
            dvln_c = jnp.zeros((SG_CHUNK, BR_WIDTH), F32)
            for g in range(SG_GROUPS):
                dvln_c = dvln_c + jnp.where(grp == g, _dot_tn(f['wm'][g], ds_c), 0.0)
                dsgw_ref[g] += jnp.where(f['tril'], _dot_nt(jnp.where(grp == g, ds_c, 0.0), vln_c), 0.0)
            dbias_acc[...] += ds_c
            dvln_parts.append(dvln_c)
        dvln = dvln_parts[0] if len(dvln_parts) == 1 else jnp.concatenate(dvln_parts, axis=0)
        dln_g_ref[...] += jnp.sum(dvln * f['ln_xh'], axis=0, keepdims=True)
        dln_b_ref[...] += jnp.sum(dvln, axis=0, keepdims=True)
        dxh = dvln * r['ln_g'][...]
        dvg = f['ln_r'] * (dxh - jnp.mean(dxh, axis=-1, keepdims=True)
                           - f['ln_xh'] * jnp.mean(dxh * f['ln_xh'], axis=-1, keepdims=True))
        dsgv = dvg * _gelu_grad(f['sgv'], f['tv'])
        dcvb = douts[2] * f['yv']
        dyv = douts[2] * f['cvb']
        dyv_ref[...] = dyv
        for kk, zk in enumerate((f['z2'], f['z1'], f['z0'])):
            dconv_ref[kk:kk + 1, :] += jnp.sum(dyv * zk, axis=0, keepdims=True)
        dps_ref[...] += jnp.sum(douts[3] * f['mixed'], axis=0, keepdims=True)
        dmixed = douts[3] * r['pool_scale'][...]
        dwbd_ref[...] += _dot_tn(f['pooled'], dmixed)
        up_ref[...] = _dot_nt(dmixed, r['wbd'][...]) / f['cnt']
        dm1_ref[...] = jnp.concatenate([dsgu, dsgv, dcvb], axis=1).astype(dm1_ref.dtype)

        @pl.when(i == pl.num_programs(0) - 1)
        def _():
            lane = lax.broadcasted_iota(jnp.int32, (1, SG_CHUNK), 1)
            db = dbias_acc[...]
            out = jnp.zeros((SG_CHUNK, SG_CHUNK), F32)
            for g in range(SG_GROUPS):
                out = out + jnp.where(lane == g, jnp.sum(jnp.where(grp == g, db, 0.0), axis=1, keepdims=True), 0.0)
            dsgb_ref[...] = out

    acc = lambda shape: (_full(shape), jax.ShapeDtypeStruct(shape, F32))
    accs = [acc((1, N_BRANCH * D_MODEL)), acc((1, BR_WIDTH)), acc((1, BR_WIDTH)), acc((SG_GROUPS, SG_CHUNK, SG_CHUNK)),
            acc((SG_CHUNK, SG_CHUNK)), acc((8, BR_WIDTH)), acc((BR_WIDTH, BR_WIDTH)), acc((1, BR_WIDTH)),
            acc((hw, D_MODEL)), acc((BR_WIDTH, D_MODEL)), acc((BR_WIDTH, D_MODEL)), acc((BR_WIDTH, D_MODEL)),
            acc((D_MODEL, D_MODEL)), acc((1, D_MODEL))]
    tile_outs = [(_rows(ts, N_BRANCH * D_MODEL), jax.ShapeDtypeStruct((s, N_BRANCH * D_MODEL), MXU_DTYPE)),
                 (_rows(ts, 3 * BR_WIDTH), jax.ShapeDtypeStruct((s, 3 * BR_WIDTH), MXU_DTYPE)),
                 (_rows(ts, BR_WIDTH), jax.ShapeDtypeStruct((s, BR_WIDTH), F32)),
                 (_rows(ts, BR_WIDTH), jax.ShapeDtypeStruct((s, BR_WIDTH), F32)),
                 (_rows(ts, hw), jax.ShapeDtypeStruct((s, hw), MXU_DTYPE)),
                 (pl.BlockSpec((N_HEADS, ts, 1), lambda i: (0, i, 0)), jax.ShapeDtypeStruct((N_HEADS, s, 1), F32))]
    outs = tile_outs + accs
    return pl.pallas_call(
        body, name="mix_bwd", grid=(s // ts,),
        in_specs=tiles + weights + [_rows(ts, D_MODEL)],
        out_specs=[o_[0] for o_ in outs], out_shape=[o_[1] for o_ in outs],
        scratch_shapes=[pltpu.VMEM((SG_CHUNK, BR_WIDTH), F32)],
        compiler_params=_params(("arbitrary",), 60 * 1024 * 1024),
    )(*_mix_operands(proj, o, lw), dx1)


def _shift_bwd(dyv, upool, proj, conv_w8):
    s = dyv.shape[0]
    ts = _tile(s, 512)
    nb = s // HALO
    c0 = COL_M1 // BR_WIDTH

    def body(dyv_ref, dyvn_ref, up_ref, upn_ref, cvx_ref, cvc_ref, cw_ref, out_ref):
        i = pl.program_id(0)
        last = (i < pl.num_programs(0) - 1).astype(F32)
        dh = jnp.concatenate([dyv_ref[...], dyvn_ref[...] * last], axis=0)
        dz = (cw_ref[2:3, :] * dh + cw_ref[1:2, :] * _shift_up(dh, 1) + cw_ref[0:1, :] * _shift_up(dh, 2))[:ts]
        up = up_ref[...]
        uh = jnp.concatenate([up, upn_ref[...] * last], axis=0)
        dpool = _window_sums(uh, _shift_up)[:ts] - up * _pool_count(i, ts)
        out_ref[...] = jnp.concatenate([dz * cvc_ref[...].astype(F32), dz * cvx_ref[...].astype(F32), dpool],
                                       axis=1).astype(out_ref.dtype)

    nxt = pl.BlockSpec((HALO, BR_WIDTH), lambda i: (jnp.minimum((i + 1) * (ts // HALO), nb - 1), 0))
    return pl.pallas_call(
        body, name="shift_bwd", grid=(s // ts,),
        in_specs=[_rows(ts, BR_WIDTH), nxt, _rows(ts, BR_WIDTH), nxt, _rows(ts, BR_WIDTH, c0 + 3), _rows(ts, BR_WIDTH, c0 + 4),
                  _full((8, BR_WIDTH))],
        out_specs=_rows(ts, 3 * BR_WIDTH),
        out_shape=jax.ShapeDtypeStruct((s, 3 * BR_WIDTH), MXU_DTYPE),
        compiler_params=_params(("parallel",)),
    )(dyv, dyv, upool, upool, proj, proj, conv_w8)


def _ffn2(a, w2, x1, g):
    s = a.shape[0]
    ts = _tile(s, 512)

    def body(a_ref, w_ref, x1_ref, g_ref, x2_ref, f_ref):
        f = _dot(_relu_sq(a_ref[...]), w_ref[...])
        f_ref[...] = f
        x2_ref[...] = x1_ref[...] + _rms(f, g_ref[...])

    return pl.pallas_call(
        body, name="ffn2", grid=(s // ts,),
        in_specs=[_rows(ts, D_FF), _resident((D_FF, D_MODEL)), _rows(ts, D_MODEL), _full((1, D_MODEL))],
        out_specs=[_rows(ts, D_MODEL), _rows(ts, D_MODEL)],
        out_shape=[jax.ShapeDtypeStruct((s, D_MODEL), F32)] * 2,
        compiler_params=_params(("parallel",)),
    )(a, w2, x1, g)


def _ffn2_bwd(dx2, f, g, a, w2t):
    s = a.shape[0]
    ts = _tile(s, 512)

    def body(dx2_ref, f_ref, g_ref, a_ref, w_ref, df_ref, da_ref, dg_ref):
        @pl.when(pl.program_id(0) == 0)
        def _():
            dg_ref[...] = jnp.zeros_like(dg_ref)

        df, dg = _rms_bwd(f_ref[...], g_ref[...], dx2_ref[...])
        dg_ref[...] += dg
        df_ref[...] = df.astype(df_ref.dtype)
        da_ref[...] = (_dot(df, w_ref[...]) * (2.0 * jnp.maximum(a_ref[...].astype(F32), 0.0))).astype(da_ref.dtype)

    return pl.pallas_call(
        body, name="ffn2_bwd", grid=(s // ts,),
        in_specs=[_rows(ts, D_MODEL), _rows(ts, D_MODEL), _full((1, D_MODEL)), _rows(ts, D_FF), _resident((D_MODEL, D_FF))],
        out_specs=[_rows(ts, D_MODEL), _rows(ts, D_FF), _full((1, D_MODEL))],
        out_shape=[jax.ShapeDtypeStruct((s, D_MODEL), MXU_DTYPE), jax.ShapeDtypeStruct((s, D_FF), MXU_DTYPE),
                   jax.ShapeDtypeStruct((1, D_MODEL), F32)],
        compiler_params=_params(("arbitrary",)),
    )(dx2, f, g, a, w2t)


def _norm_in_bwd(name, pieces, x, g, dres):
    s = x.shape[0]
    ts = _tile(s, 512)
    n = len(pieces)

    def body(*refs):
        d_refs, w_refs = refs[:n], refs[n:2 * n]
        x_ref, g_ref, dres_ref, dx_ref, dg_ref = refs[2 * n:]

        @pl.when(pl.program_id(0) == 0)
        def _():
            dg_ref[...] = jnp.zeros_like(dg_ref)

        dh = _dot(d_refs[0][...], w_refs[0][...])
        for p in range(1, n):
            dh = dh + _dot(d_refs[p][...], w_refs[p][...])
        dx, dg = _rms_bwd(x_ref[...], g_ref[...], dh)
        dg_ref[...] += dg
        dx_ref[...] = dres_ref[...] + dx

    return pl.pallas_call(
        body, name=name, grid=(s // ts,),
        in_specs=[_rows(ts, d.shape[1]) for d, _ in pieces] + [_resident(w.shape) for _, w in pieces]
        + [_rows(ts, D_MODEL), _full((1, D_MODEL)), _rows(ts, D_MODEL)],
        out_specs=[_rows(ts, D_MODEL), _full((1, D_MODEL))],
        out_shape=[jax.ShapeDtypeStruct((s, D_MODEL), F32), jax.ShapeDtypeStruct((1, D_MODEL), F32)],
        compiler_params=_params(("arbitrary",)),
    )(*[d for d, _ in pieces], *[w for _, w in pieces], x, g, dres)


def _loss_and_grad(y, target):
    s = y.shape[0]
    ts = _tile(s, 512)

    def body(y_ref, t_ref, dy_ref, loss_ref):
        @pl.when(pl.program_id(0) == 0)
        def _():
            loss_ref[...] = jnp.zeros_like(loss_ref)

        err = y_ref[...] - t_ref[...]
        dy_ref[...] = err * (1.0 / D_MODEL)
        loss_ref[...] += 0.5 * jnp.sum(jnp.mean(err * err, axis=-1, keepdims=True), axis=0, keepdims=True)

    dy, loss = pl.pallas_call(
        body, name="loss", grid=(s // ts,),
        in_specs=[_rows(ts, D_MODEL), _rows(ts, D_MODEL)],
        out_specs=[_rows(ts, D_MODEL), _full((8, 128))],
        out_shape=[jax.ShapeDtypeStruct((s, D_MODEL), F32), jax.ShapeDtypeStruct((8, 128), F32)],
        compiler_params=_params(("arbitrary",)),
    )(y, target)
    return loss[0, 0], dy


_W_IN_SPLITS = [256, 384, 416, 672, 928, 1184, 1440, 1696, 1952]


def _rope_swap(w):
    half = QK_ROPE // 2
    return jnp.concatenate([-w[..., half:], w[..., :half]], axis=-1)


def _rope_unswap(d):
    half = QK_ROPE // 2
    return jnp.concatenate([d[..., half:], -d[..., :half]], axis=-1)


def _zeros_like_cols(w, n):
    return jnp.zeros(w.shape[:-1] + (n,), w.dtype)


def _derive_weights(w):
    md = MXU_DTYPE
    nl = w['w_in'].shape[0]
    c_q, c_kv, k_r, sg_u, sg_v, cv_x, cv_b, cv_c, pool, gate = jnp.split(w['w_in'].astype(md), _W_IN_SPLITS, axis=-1)
    pad_rope = lambda r: jnp.concatenate([_zeros_like_cols(r, QK_NOPE), r, _zeros_like_cols(r, HEAD_PAD - QK_NOPE - QK_ROPE)], -1)
    w_in_p = jnp.concatenate([gate, sg_u, sg_v, cv_b, cv_x, cv_c, pool, c_q, c_kv, pad_rope(k_r), pad_rope(_rope_swap(k_r))], -1)
    wq = w['w_uq'].astype(md).reshape(nl, Q_RANK, N_HEADS, QK_NOPE + QK_ROPE)
    nope, rope_w = wq[..., :QK_NOPE], wq[..., QK_NOPE:]
    wq_a = jnp.concatenate([nope, rope_w, _zeros_like_cols(nope, 32)], -1).reshape(nl, Q_RANK, N_HEADS * HEAD_PAD)
    wq_b = pad_rope(_rope_swap(rope_w)).reshape(nl, Q_RANK, N_HEADS * HEAD_PAD)
    wkv = w['w_ukv'].astype(md).reshape(nl, KV_RANK, N_HEADS, QK_NOPE + V_HEAD)
    pad_half = lambda r: jnp.concatenate([r, _zeros_like_cols(r, HEAD_PAD - r.shape[-1])], -1).reshape(nl, KV_RANK, N_HEADS * HEAD_PAD)
    w_br_mla = w['w_br_mla'].astype(md).reshape(nl, N_HEADS, V_HEAD, D_MODEL)
    w_br_mla_p = jnp.concatenate([w_br_mla, jnp.zeros_like(w_br_mla)], axis=2).reshape(nl, N_HEADS * HEAD_PAD, D_MODEL)
    eye = jnp.eye(4, dtype=md)
    wbd = (w['pool_w'].astype(md)[:, :, :, None, :] * eye[None, :, None, :, None]).reshape(nl, BR_WIDTH, BR_WIDTH)
    row = lambda a: a.astype(F32)[:, None, :]
    w_in_pt = jnp.swapaxes(w_in_p, 1, 2)
    return dict(
        w_in_p=w_in_p, wt_g=w_in_pt[:, COL_G:COL_M1], wt_m1=w_in_pt[:, COL_M1:COL_M2], wt_m2=w_in_pt[:, COL_M2:COL_B],
        wt_b=w_in_pt[:, COL_B:],
        wq=jnp.concatenate([wq_a, wq_b], -1), wkv=jnp.concatenate([pad_half(wkv[..., :QK_NOPE]), pad_half(wkv[..., QK_NOPE:])], -1),
        w_br_mla_p=w_br_mla_p, w_br_sg=w['w_br_sg'].astype(md), w_br_conv=w['w_br_conv'].astype(md),
        w_br_pool=w['w_br_pool'].astype(md), wbd=wbd, w_out=w['w_out'].astype(md),
        w_ff1=w['w_ff1'].astype(md), w_ff1t=jnp.swapaxes(w['w_ff1'].astype(md), 1, 2),
        w_ff2=w['w_ff2'].astype(md), w_ff2t=jnp.swapaxes(w['w_ff2'].astype(md), 1, 2),
        norm_mix_pre=row(w['norm_mix_pre']), gate_b=row(w['gate_b']), q_norm=row(w['q_norm']), kv_norm=row(w['kv_norm']),
        sg_ln_g=row(w['sg_ln_g']), sg_ln_b=row(w['sg_ln_b']), sg_w=w['sg_w'].astype(F32),
        sg_bias=jnp.repeat(jnp.swapaxes(w['sg_b'].astype(F32), 1, 2), BR_WIDTH // SG_GROUPS, axis=2),
        conv_w8=jnp.pad(w['conv_w'].astype(F32), ((0, 0), (0, 5), (0, 0))), pool_scale=row(w['pool_scale']),
        norm_mix_post=row(w['norm_mix_post']), norm_ffn_pre=row(w['norm_ffn_pre']), norm_ffn_post=row(w['norm_ffn_post']),
    )


def _rope_tables(positions):
    inv_freq = ROPE_BASE ** (-jnp.arange(0, QK_ROPE, 2, dtype=F32) / QK_ROPE)
    ang = positions.astype(F32)[:, None] * inv_freq
    cos, sin = jnp.cos(ang), jnp.sin(ang)
    n = positions.shape[0]
    ones, z64, z32 = jnp.ones((n, QK_NOPE), F32), jnp.zeros((n, QK_NOPE), F32), jnp.zeros((n, 32), F32)
    return (jnp.concatenate([ones, cos, cos, z32], 1), jnp.concatenate([z64, sin, sin, z32], 1),
            jnp.concatenate([z64, cos, cos, z32], 1))


def _reference_layout_grads(g):
    gate, dm1, dm2, dpb = g['dw_in_pieces']
    nl = gate.shape[0]
    sg_u, sg_v, cv_b = jnp.split(dm1, 3, axis=-1)
    cv_x, cv_c, pool = jnp.split(dm2, 3, axis=-1)
    c_q, c_kv, kr, krs = jnp.split(dpb, [Q_RANK, Q_RANK + KV_RANK, Q_RANK + KV_RANK + HEAD_PAD], axis=-1)
    rope_cols = slice(QK_NOPE, QK_NOPE + QK_ROPE)
    k_r = kr[..., rope_cols] + _rope_unswap(krs[..., rope_cols])
    w_in = jnp.concatenate([c_q, c_kv, k_r, sg_u, sg_v, cv_x, cv_b, cv_c, pool, gate], -1)
    hw = N_HEADS * HEAD_PAD
    dqa = g['dwq'][..., :hw].reshape(nl, Q_RANK, N_HEADS, HEAD_PAD)
    dqb = g['dwq'][..., hw:].reshape(nl, Q_RANK, N_HEADS, HEAD_PAD)
    w_uq = jnp.concatenate([dqa[..., :QK_NOPE], dqa[..., rope_cols] + _rope_unswap(dqb[..., rope_cols])], -1)
    dka = g['dwkv'][..., :hw].reshape(nl, KV_RANK, N_HEADS, HEAD_PAD)
    dva = g['dwkv'][..., hw:].reshape(nl, KV_RANK, N_HEADS, HEAD_PAD)
    w_ukv = jnp.concatenate([dka[..., :QK_NOPE], dva[..., :V_HEAD]], -1)
    w_br_mla = g['dw_br_mla_p'].reshape(nl, N_HEADS, HEAD_PAD, D_MODEL)[:, :, :V_HEAD]
    dwbd = g['dwbd'].reshape(nl, 4, 64, 4, 64)
    pool_w = jnp.stack([dwbd[:, k, :, k, :] for k in range(4)], axis=1)
    sq = lambda a: a[:, 0, :]
    return dict(
        norm_mix_pre=sq(g['dg_pre']), w_in=w_in, gate_b=sq(g['dgate_b']), q_norm=sq(g['dq_norm']),
        w_uq=w_uq.reshape(nl, Q_RANK, -1), kv_norm=sq(g['dkv_norm']), w_ukv=w_ukv.reshape(nl, KV_RANK, -1),
        w_br_mla=w_br_mla.reshape(nl, N_HEADS * V_HEAD, D_MODEL), sg_ln_g=sq(g['dln_g']), sg_ln_b=sq(g['dln_b']),
        sg_w=g['dsg_w'], sg_b=jnp.swapaxes(g['dsg_b'][:, :, :SG_GROUPS], 1, 2), w_br_sg=g['dw_br_sg'],
        conv_w=g['dconv_w'][:, :3], w_br_conv=g['dw_br_conv'], pool_w=pool_w, pool_scale=sq(g['dpool_scale']),
        w_br_pool=g['dw_br_pool'], w_out=g['dw_out'], norm_mix_post=sq(g['dg_post']), norm_ffn_pre=sq(g['dg_fpre']),
        w_ff1=g['dw_ff1'], w_ff2=g['dw_ff2'], norm_ffn_post=sq(g['dg_fpost']))


def _layer_forward(x0, lw, tabs, carried):
    proj = _mm("in_proj", x0, lw['w_in_p'], tm=1024, tn=896, prologue=_rms, rows=(lw['norm_mix_pre'],))
    q, k, v = _qkv_prep(proj, lw['q_norm'], lw['kv_norm'], lw['wq'], lw['wkv'], *tabs)
    (o, lse), carried_out = _attn_fwd(q, k, v, carried)
    x1 = _mix_fwd(x0, proj, o, lw)
    a = _mm("ffn1", x1, lw['w_ff1'], tm=1024, tn=1024, prologue=_rms, rows=(lw['norm_ffn_pre'],))
    x2, f = _ffn2(a, lw['w_ff2'], x1, lw['norm_ffn_post'])
    return x2, dict(x0=x0, proj=proj, q=q, k=k, v=v, o=o, lse=lse, x1=x1, a=a, f=f), carried_out


def _layer_backward(dx2, lw, sv, tabs, carried):
    g = {}
    df, da, g['dg_fpost'] = _ffn2_bwd(dx2, sv['f'], lw['norm_ffn_post'], sv['a'], lw['w_ff2t'])
    g['dw_ff2'] = _mm_tn("dw_ff2", sv['a'], df, tm=512, tn=512, prologue=_relu_sq)
    dx1, g['dg_fpre'] = _norm_in_bwd("ffn1_bwd", [(da, lw['w_ff1t'])], sv['x1'], lw['norm_ffn_pre'], dx2)
    g['dw_ff1'] = _mm_tn("dw_ff1", sv['x1'], da, tm=512, tn=1024, prologue=_rms, rows=(lw['norm_ffn_pre'],))
    (dgate, dm1, dyv, upool, do, delta, g['dgate_b'], g['dln_g'], g['dln_b'], g['dsg_w'], g['dsg_b'], g['dconv_w'],
     g['dwbd'], g['dpool_scale'], g['dw_br_mla_p'], g['dw_br_sg'], g['dw_br_conv'], g['dw_br_pool'], g['dw_out'],
     g['dg_post']) = _mix_bwd(dx1, sv['proj'], sv['o'], lw)
    dm2 = _shift_bwd(dyv, upool, sv['proj'], lw['conv_w8'])
    (dq, dk, dv), carried_out = _attn_bwd(sv['q'], sv['k'], sv['v'], do, sv['lse'], delta, carried)
    dpb, g['dwq'], g['dwkv'], g['dq_norm'], g['dkv_norm'] = _qkv_bwd(
        dq, dk, dv, sv['proj'], lw['q_norm'], lw['kv_norm'], lw['wq'], lw['wkv'], *tabs)
    pieces = [(dgate, lw['wt_g']), (dm1, lw['wt_m1']), (dm2, lw['wt_m2']), (dpb, lw['wt_b'])]
    dx0, g['dg_pre'] = _norm_in_bwd("in_proj_bwd", pieces, sv['x0'], lw['norm_mix_pre'], dx1)
    g['dw_in_pieces'] = [_mm_tn("dw_in_%d" % n, sv['x0'], d, tm=512, tn=1024, prologue=_rms, rows=(lw['norm_mix_pre'],))
                         for n, (d, _) in enumerate(pieces)]
    return dx0, g, carried_out


class _StepPlan(NamedTuple):
    n_layers: int
    weights_exchange: Callable
    weights_from: Callable
    grads_ready: Callable
    exchange_done: Callable


def _local_step(x, positions, target, plan):
    tabs = _rope_tables(positions)
    derive = lambda w: {n: a[0] for n, a in _derive_weights(w).items()}
    first = plan.weights_exchange(0)
    weights = plan.weights_from(0, None if first is None else _run_exchange("gather_weights", first))
    derived, saved = [], []
    for l in range(plan.n_layers):
        derived.append(derive(weights))
        coming = plan.weights_exchange(l + 1) if l + 1 < plan.n_layers else None
        x, sv, arrived = _layer_forward(x, derived[l], tabs, coming)
        saved.append(sv)
        if l + 1 < plan.n_layers:
            weights = plan.weights_from(l + 1, arrived)
    loss, dx = _loss_and_grad(x, target)
    pending = None
    for l in reversed(range(plan.n_layers)):
        dx, g, arrived = _layer_backward(dx, derived[l], saved[l], tabs, None if pending is None else pending[1])
        if pending is not None:
            plan.exchange_done(pending[0], arrived)
        lead = lambda a: [b[None] for b in a] if isinstance(a, list) else a[None]
        going = plan.grads_ready(l, _reference_layout_grads({n: lead(a) for n, a in g.items()}))
        pending = None if going is None else (l, going)
    if pending is not None:
        plan.exchange_done(pending[0], _run_exchange("grads_to_chips", pending[1]))
    return loss, dx


def _relative_peers():
    x, y = lax.axis_index("x"), lax.axis_index("y")
    return {1: (x, 1 - y), 2: (1 - x, y), 3: (1 - x, 1 - y)}


def _for_my_core(fn):
    def run():
        for half in (0, 1):
            pl.when(lax.axis_index("c") == half)(functools.partial(fn, half))
    return run


def _gather_exchange(split, whole):
    ns, nw = len(split), len(whole)
    n = ns + nw

    def phases(ins, outs, sems):
        ici_send, ici_recv, d2d_send, d2d_recv, own_send, own_recv = sems
        x, y, c = lax.axis_index("x"), lax.axis_index("y"), lax.axis_index("c")
        peers = _relative_peers()

        def rows(ref, which):
            h = ref.shape[-2] // 2
            return ref.at[(slice(None),) * (len(ref.shape) - 2) + (slice(which * h, (which + 1) * h), slice(None))]

        def own(k):
            return pltpu.make_async_remote_copy(src_ref=ins[k], dst_ref=outs[k].at[0], send_sem=own_send.at[k],
                                                recv_sem=own_recv.at[k], device_id=(x, y, 1 - c), device_id_type=MESH)

        def over_ici(k, r, half):
            src = rows(ins[k], half) if k < ns else ins[k]
            dst = rows(outs[k].at[r], half) if k < ns else outs[k].at[r]
            return pltpu.make_async_remote_copy(src_ref=src, dst_ref=dst, send_sem=ici_send.at[3 * k + r - 1],
                                                recv_sem=ici_recv.at[3 * k + r - 1], device_id=(*peers[r], c), device_id_type=MESH)

        def to_sibling(k, r, half):
            landed = rows(outs[k].at[r], half)
            return pltpu.make_async_remote_copy(src_ref=landed, dst_ref=landed, send_sem=d2d_send.at[3 * k + r - 1],
                                                recv_sem=d2d_recv.at[3 * k + r - 1], device_id=(x, y, 1 - c), device_id_type=MESH)

        def start(half):
            for k in range(n):
                own(k).start()
                for r in peers:
                    over_ici(k, r, half).start()

        def middle(half):
            for k in range(n):
                for r in peers:
                    over_ici(k, r, half).wait_recv()
                    if k < ns:
                        to_sibling(k, r, half).start()

        def finish(half):
            for k in range(n):
                own(k).wait()
                for r in peers:
                    if k < ns:
                        to_sibling(k, r, 1 - half).wait_recv()
                        to_sibling(k, r, half).wait_send()
                    over_ici(k, r, half).wait_send()

        return _for_my_core(start), _for_my_core(middle), _for_my_core(finish)

    arrs = list(split) + list(whole)
    return _Exchange(
        operands=arrs, out_shape=[jax.ShapeDtypeStruct((4,) + a.shape, a.dtype) for a in arrs],
        scratch_shapes=[pltpu.SemaphoreType.DMA((3 * n,)), pltpu.SemaphoreType.DMA((3 * n,)), pltpu.SemaphoreType.DMA((3 * ns,)),
                        pltpu.SemaphoreType.DMA((3 * ns,)), pltpu.SemaphoreType.DMA((n,)), pltpu.SemaphoreType.DMA((n,))],
        phases=phases)


def _absolute_chip_order(relative):
    me = 2 * lax.axis_index("x") + lax.axis_index("y")
    return jnp.stack([lax.dynamic_index_in_dim(relative, jnp.bitwise_xor(me, chip), 0, keepdims=False) for chip in range(4)])


REDUCE_STEPS = 8


def _sibling_halves(name, arrs):
    n = len(arrs)

    def body(*refs):
        ins, theirs = refs[:n], refs[n:2 * n]
        send_sems, recv_sems = refs[2 * n:]
        x, y, c = lax.axis_index("x"), lax.axis_index("y"), lax.axis_index("c")

        def exchange(my_half):
            copies = []
            for k in range(n):
                h = ins[k].shape[1] // 2
                cp = pltpu.make_async_remote_copy(src_ref=ins[k].at[:, (1 - my_half) * h:(2 - my_half) * h, :], dst_ref=theirs[k],
                                                  send_sem=send_sems.at[k], recv_sem=recv_sems.at[k],
                                                  device_id=(x, y, 1 - c), device_id_type=MESH)
                cp.start()
                copies.append(cp)
            for cp in copies:
                cp.wait()

        for half in (0, 1):
            pl.when(c == half)(functools.partial(exchange, half))

    return pl.pallas_call(
        body, name=name, in_specs=[HBM] * n, out_specs=[HBM] * n,
        out_shape=[jax.ShapeDtypeStruct((a.shape[0], a.shape[1] // 2, a.shape[2]), a.dtype) for a in arrs],
        scratch_shapes=[pltpu.SemaphoreType.DMA((n,)), pltpu.SemaphoreType.DMA((n,))],
    )(*arrs)


def _add_sibling(name, arrs, theirs):
    n, steps = len(arrs), REDUCE_STEPS

    def body(*refs):
        for mine_ref, theirs_ref, out_ref in zip(refs[:n], refs[n:2 * n], refs[2 * n:]):
            out_ref[...] = (mine_ref[...] + theirs_ref[...]).astype(out_ref.dtype)

    block = lambda t: (4, t.shape[1] // steps, t.shape[2])
    return pl.pallas_call(
        body, name=name, grid=(steps,),
        in_specs=[pl.BlockSpec(block(t), lambda i: (0, lax.axis_index("c") * steps + i, 0)) for t in theirs]
        + [pl.BlockSpec(block(t), lambda i: (0, i, 0)) for t in theirs],
        out_specs=[pl.BlockSpec(block(t), lambda i: (0, i, 0)) for t in theirs],
        out_shape=[jax.ShapeDtypeStruct(t.shape, WIRE_DTYPE) for t in theirs],
        compiler_params=_params(("parallel",)))(*arrs, *theirs)


def _scatter_exchange(arrs):
    n = len(arrs)

    def phases(ins, outs, sems):
        send_sems, recv_sems = sems
        c = lax.axis_index("c")
        peers = _relative_peers()

        def copy(k, r):
            px, py = peers[r]
            return pltpu.make_async_remote_copy(src_ref=ins[k].at[2 * px + py], dst_ref=outs[k].at[r - 1],
                                                send_sem=send_sems.at[3 * k + r - 1], recv_sem=recv_sems.at[3 * k + r - 1],
                                                device_id=(px, py, c), device_id_type=MESH)

        def start():
            for k in range(n):
                for r in peers:
                    copy(k, r).start()

        def finish():
            for k in range(n):
                for r in peers:
                    copy(k, r).wait()

        return start, lambda: None, finish

    return _Exchange(operands=list(arrs), out_shape=[jax.ShapeDtypeStruct((3,) + a.shape[1:], a.dtype) for a in arrs],
                     scratch_shapes=[pltpu.SemaphoreType.DMA((3 * n,)), pltpu.SemaphoreType.DMA((3 * n,))], phases=phases)


def _sum_chips(name, chip_sums, arrived):
    n, steps = len(chip_sums), REDUCE_STEPS

    def body(*refs):
        for own_ref, arrived_ref, out_ref in zip(refs[:n], refs[n:2 * n], refs[2 * n:]):
            acc = own_ref[...].astype(F32)
            for r in range(3):
                acc = acc + arrived_ref[r].astype(F32)
            out_ref[...] = acc

    rows = lambda s: s.shape[1] // steps
    chip = lambda: 2 * lax.axis_index("x") + lax.axis_index("y")
    return pl.pallas_call(
        body, name=name, grid=(steps,),
        in_specs=[pl.BlockSpec((None, rows(s), s.shape[2]), lambda i: (chip(), i, 0)) for s in chip_sums]
        + [pl.BlockSpec((3, rows(s), s.shape[2]), lambda i: (0, i, 0)) for s in chip_sums],
        out_specs=[pl.BlockSpec((rows(s), s.shape[2]), lambda i: (lax.axis_index("c") * steps + i, 0)) for s in chip_sums],
        out_shape=[jax.ShapeDtypeStruct((2 * s.shape[1], s.shape[2]), F32) for s in chip_sums],
        compiler_params=_params(("parallel",)))(*chip_sums, *arrived)


def _join_siblings(name, bufs):
    n = len(bufs)

    def body(*refs):
        outs = refs[n:2 * n]
        send_sems, recv_sems = refs[2 * n:]
        x, y, c = lax.axis_index("x"), lax.axis_index("y"), lax.axis_index("c")

        def exchange(my_half):
            copies = []
            for k in range(n):
                h = outs[k].shape[0] // 2
                mine = outs[k].at[my_half * h:(my_half + 1) * h, :]
                theirs = outs[k].at[(1 - my_half) * h:(2 - my_half) * h, :]
                cp = pltpu.make_async_remote_copy(src_ref=mine, dst_ref=mine, send_sem=send_sems.at[k],
                                                  recv_sem=recv_sems.at[k], device_id=(x, y, 1 - c), device_id_type=MESH)
                cp.start()
                arrival = pltpu.make_async_remote_copy(src_ref=theirs, dst_ref=theirs, send_sem=send_sems.at[k],
                                                       recv_sem=recv_sems.at[k], device_id=(x, y, 1 - c), device_id_type=MESH)
                copies.append((cp, arrival))
            for cp, arrival in copies:
                arrival.wait_recv()
                cp.wait_send()

        for half in (0, 1):
            pl.when(c == half)(functools.partial(exchange, half))

    return pl.pallas_call(
        body, name=name, in_specs=[HBM] * n, out_specs=[HBM] * n,
        out_shape=[jax.ShapeDtypeStruct(b.shape, b.dtype) for b in bufs], input_output_aliases={k: k for k in range(n)},
        scratch_shapes=[pltpu.SemaphoreType.DMA((n,)), pltpu.SemaphoreType.DMA((n,))],
    )(*bufs)


def _gather_all(name, a):
    def body(a_ref, out_ref, staging, send_sems, recv_sems, local_sem):
        x, y, c = lax.axis_index("x"), lax.axis_index("y"), lax.axis_index("c")
        me = 4 * x + 2 * y + c
        flips = [(fx, fy, fc) for fx in (0, 1) for fy in (0, 1) for fc in (0, 1)][1:]
        peers = [(x ^ fx, y ^ fy, c ^ fc) for fx, fy, fc in flips]
        load = pltpu.make_async_copy(a_ref, staging, local_sem)
        load.start()
        load.wait()
        local = pltpu.make_async_copy(staging, out_ref.at[me], local_sem)
        local.start()
        sends = []
        for j, peer in enumerate(peers):
            cp = pltpu.make_async_remote_copy(src_ref=a_ref, dst_ref=out_ref.at[me], send_sem=send_sems.at[j],
                                              recv_sem=recv_sems.at[j], device_id=peer, device_id_type=MESH)
            cp.start()
            sends.append(cp)
        for j, (px, py, pc) in enumerate(peers):
            pltpu.make_async_remote_copy(src_ref=a_ref, dst_ref=out_ref.at[4 * px + 2 * py + pc], send_sem=send_sems.at[j],
                                         recv_sem=recv_sems.at[j], device_id=(px, py, pc), device_id_type=MESH).wait_recv()
        for cp in sends:
            cp.wait_send()
        local.wait()

    return pl.pallas_call(
        body, name=name, in_specs=[HBM], out_specs=HBM, out_shape=jax.ShapeDtypeStruct((8,) + a.shape, a.dtype),
        scratch_shapes=[pltpu.VMEM(a.shape, a.dtype), pltpu.SemaphoreType.DMA((7,)), pltpu.SemaphoreType.DMA((7,)),
                        pltpu.SemaphoreType.DMA],
    )(a)


def _rowwise_call(name, fn, slots, out_shapes, steps):
    n_in, n_out = [len(s) for s in slots], [len(o) for o in out_shapes]

    def spec(shape):
        if len(shape) == 3:
            return pl.BlockSpec((shape[0], shape[1] // steps, shape[2]), lambda i: (0, i, 0))
        return pl.BlockSpec((shape[0] // steps, shape[1]), lambda i: (i, 0))

    def body(*refs):
        ins, outs = refs[:sum(n_in)], refs[sum(n_in):]
        a = b = 0
        for k in range(len(slots)):
            for o_ref, val in zip(outs[b:b + n_out[k]], fn(*[r[...] for r in ins[a:a + n_in[k]]])):
                o_ref[...] = val
            a, b = a + n_in[k], b + n_out[k]

    flat_in = [arr for s in slots for arr in s]
    flat_out = [shp for o in out_shapes for shp in o]
    out = pl.pallas_call(body, name=name, grid=(steps,), in_specs=[spec(a.shape) for a in flat_in],
                         out_specs=[spec(s) for s in flat_out], out_shape=[jax.ShapeDtypeStruct(s, F32) for s in flat_out],
                         compiler_params=_params(("parallel",)))(*flat_in)
    grouped, b = [], 0
    for k in range(len(slots)):
        grouped.append(out[b:b + n_out[k]])
        b += n_out[k]
    return grouped


def _sum_in_order(a):
    acc = a[0].astype(F32)
    for k in range(1, a.shape[0]):
        acc = acc + a[k].astype(F32)
    return (acc,)


def _adamw_math(w, g, m, v):
    m_new = ADAM_B1 * m + (1.0 - ADAM_B1) * g
    v_new = ADAM_B2 * v + (1.0 - ADAM_B2) * (g * g)
    m_hat = m_new / (1.0 - ADAM_B1 ** ADAM_STEP)
    v_hat = v_new / (1.0 - ADAM_B2 ** ADAM_STEP)
    return -ADAM_LR * (m_hat / (jnp.sqrt(v_hat) + ADAM_EPS) + ADAM_WD * w), m_new, v_new


SMALL_PACK_COLS = 256
SMALL_PACK_ROWS = 2048


def _pack_small(parts):
    wide = [jnp.pad(p, ((0, 0), (0, 0), (0, SMALL_PACK_COLS - p.shape[2]))) for p in parts]
    rows = jnp.concatenate(wide, axis=1)
    return jnp.pad(rows, ((0, 0), (0, SMALL_PACK_ROWS - rows.shape[1]), (0, 0)))


def _unpack_small(packed, shapes):
    out, row = [], 0
    for a, b in shapes:
        out.append(packed[:, row:row + a, :b])
        row += a
    return out


def _pack(arrs, rows_per_layer, dtype):
    nl = arrs[0].shape[0]
    flat = jnp.concatenate([a.astype(dtype).reshape(nl, -1) for a in arrs], axis=1)
    flat = jnp.pad(flat, ((0, 0), (0, rows_per_layer * PACK_COLS - flat.shape[1])))
    return flat.reshape(nl * rows_per_layer, PACK_COLS)


def _unpack(packed, shapes, rows_per_layer):
    nl = shapes[0][0]
    flat = packed.reshape(packed.shape[:-2] + (nl, rows_per_layer * PACK_COLS))
    out, off = [], 0
    for shp in shapes:
        size = math.prod(shp[1:])
        out.append(flat[..., off:off + size].reshape(packed.shape[:-2] + tuple(shp)))
        off += size
    return out


def _rows_needed(shapes, multiple):
    per_layer = sum(math.prod(s[1:]) for s in shapes)
    rows = -(-per_layer // PACK_COLS)
    return -(-rows // multiple) * multiple


CONV_TILE = (8, 128)


def _layer_shard_exchange(w, l):
    conv = w['conv_w'][l].reshape(-1)
    conv = jnp.pad(conv, (0, math.prod(CONV_TILE) - conv.shape[0])).reshape(CONV_TILE)
    return _gather_exchange([w[n][l].astype(MXU_DTYPE) for n in MATMUL_SHARDED], [conv])


def _layer_full_weights(w, l, gathered):
    gathered = [_absolute_chip_order(g) for g in gathered]
    full = {n: w[n][l:l + 1] for n in WEIGHTS}
    for n, part in zip(MATMUL_SHARDED, gathered):
        if n in ROW_SHARDED:
            full[n] = part.reshape(1, 4 * part.shape[1], part.shape[2])
        else:
            full[n] = jnp.swapaxes(part, 0, 1).reshape(1, part.shape[1], 4 * part.shape[2])
    rows, cols = w['conv_w'].shape[1:]
    conv = gathered[-1].reshape(4, -1)[:, :rows * cols].reshape(4, rows, cols)
    full['conv_w'] = jnp.swapaxes(conv, 0, 1).reshape(1, rows, 4 * cols)
    return full


def _chip_major(n, g):
    nl = g.shape[0]
    if n in ROW_SHARDED:
        return jnp.swapaxes(g.reshape(nl, 4, g.shape[1] // 4, g.shape[2]), 0, 1)
    return jnp.transpose(g.reshape(nl, g.shape[1], 4, g.shape[2] // 4), (2, 0, 1, 3))


def kernel(x, positions, norm_mix_pre, w_in, gate_b, q_norm, w_uq, kv_norm, w_ukv, w_br_mla, sg_ln_g, sg_ln_b, sg_w, sg_b, w_br_sg, conv_w, w_br_conv, pool_w, pool_scale, w_br_pool, w_out, norm_mix_post, norm_ffn_pre, w_ff1, w_ff2, norm_ffn_post, loss_target, m_norm_mix_pre, m_w_in, m_gate_b, m_q_norm, m_w_uq, m_kv_norm, m_w_ukv, m_w_br_mla, m_sg_ln_g, m_sg_ln_b, m_sg_w, m_sg_b, m_w_br_sg, m_conv_w, m_w_br_conv, m_pool_w, m_pool_scale, m_w_br_pool, m_w_out, m_norm_mix_post, m_norm_ffn_pre, m_w_ff1, m_w_ff2, m_norm_ffn_post, v_norm_mix_pre, v_w_in, v_gate_b, v_q_norm, v_w_uq, v_kv_norm, v_w_ukv, v_w_br_mla, v_sg_ln_g, v_sg_ln_b, v_sg_w, v_sg_b, v_w_br_sg, v_conv_w, v_w_br_conv, v_pool_w, v_pool_scale, v_w_br_pool, v_w_out, v_norm_mix_post, v_norm_ffn_pre, v_w_ff1, v_w_ff2, v_norm_ffn_post):
    given = dict(locals())
    w = {n: given[n] for n in WEIGHTS}
    mom = {n: given['m_' + n] for n in WEIGHTS}
    var = {n: given['v_' + n] for n in WEIGHTS}
    nl = w['w_in'].shape[0]
    chip_sums, reduced, local_small = {}, {}, {}

    def grads_ready(l, g):
        arrs = [_chip_major(n, g[n])[:, 0] for n in BIG_SHARDED]
        arrs.append(_pack_small([_chip_major(n, g[n])[:, 0] for n in SMALL_SHARDED]))
        chip_sums[l] = _add_sibling("add_sibling", arrs, _sibling_halves("grads_to_sibling", arrs))
        local_small[l] = [g[n] for n in REPLICATED + ['conv_w']]
        return _scatter_exchange(chip_sums[l])

    def exchange_done(l, arrived):
        reduced[l] = _join_siblings("join_halves", _sum_chips("sum_chips", chip_sums[l], arrived))

    plan = _StepPlan(n_layers=nl, weights_exchange=functools.partial(_layer_shard_exchange, w),
                     weights_from=functools.partial(_layer_full_weights, w), grads_ready=grads_ready,
                     exchange_done=exchange_done)
    loss, dx = _local_step(x[0], positions[0], loss_target[0], plan)
    loss = lax.psum(loss, ("x", "y", "c"))

    grad, delta, new_m, new_v = {}, {}, {}, {}
    for k, n in enumerate(BIG_SHARDED):
        grad[n] = jnp.stack([reduced[l][k] for l in range(nl)])
    slots = [[w[n], grad[n], mom[n], var[n]] for n in BIG_SHARDED]
    small_pack = lambda d: _pack_small([d[n] for n in SMALL_SHARDED])
    g_small = jnp.stack([reduced[l][-1] for l in range(nl)])
    slots.append([small_pack(w), g_small, small_pack(mom), small_pack(var)])
    updated = _rowwise_call("adamw_sharded", _adamw_math, slots, [[s_[0].shape] * 3 for s_ in slots], 32)
    for k, n in enumerate(BIG_SHARDED):
        delta[n], new_m[n], new_v[n] = updated[k]
    for d, packed in zip((grad, delta, new_m, new_v), [g_small] + list(updated[-1])):
        d.update(zip(SMALL_SHARDED, _unpack_small(packed, [w[n].shape[1:] for n in SMALL_SHARDED])))

    names = REPLICATED + ['conv_w']
    local = [jnp.concatenate([local_small[l][k] for l in range(nl)]) for k in range(len(names))]
    rows = _rows_needed([a.shape for a in local], 32)
    everyone = _gather_all("gather_small_grads", _pack(local, rows, F32))
    (summed,), = _rowwise_call("sum_devices", _sum_in_order, [[everyone]], [[everyone.shape[1:]]], 4)
    g_rep = _unpack(summed, [a.shape for a in local], rows)
    chip = 2 * lax.axis_index("x") + lax.axis_index("y")
    g_rep[-1] = lax.dynamic_slice_in_dim(g_rep[-1], chip * w['conv_w'].shape[2], w['conv_w'].shape[2], axis=2)
    rep_pack = lambda arrs: _pack(arrs, rows, F32)
    (rep_out,) = _rowwise_call("adamw_replicated", _adamw_math,
                               [[rep_pack([w[n] for n in names]), rep_pack(g_rep), rep_pack([mom[n] for n in names]),
                                 rep_pack([var[n] for n in names])]], [[(nl * rows, PACK_COLS)] * 3], 4)
    grad.update(zip(names, g_rep))
    for d, packed in zip((delta, new_m, new_v), rep_out):
        d.update(zip(names, _unpack(packed, [w[n].shape for n in names], rows)))

    return (loss, dx[None], *[grad[n] for n in WEIGHTS], *[delta[n] for n in WEIGHTS], *[new_m[n] for n in WEIGHTS],
            *[new_v[n] for n in WEIGHTS])
```

```python
import functools
import math
from typing import Any, Callable, NamedTuple, Sequence

import jax
import jax.numpy as jnp
from jax import lax
from jax.experimental import pallas as pl
from jax.experimental.pallas import tpu as pltpu

F32 = jnp.float32
MXU_DTYPE = jnp.bfloat16
WIRE_DTYPE = jnp.bfloat16
ACT_DTYPE = jnp.bfloat16
MESH = pl.DeviceIdType.MESH

D_MODEL = 1024
D_FF = 4096
N_HEADS = 4
QK_NOPE = 64
QK_ROPE = 32
V_HEAD = 64
HEAD_PAD = 128
Q_RANK = 256
KV_RANK = 128
SG_CHUNK = 128
SG_GROUPS = 4
BR_WIDTH = 256
N_BRANCH = 4
POOL_WINDOWS = (2, 4, 8, 16)
HALO = 16
ROPE_BASE = 10000.0
EPS = 1e-6
ATTN_SCALE = (QK_NOPE + QK_ROPE) ** -0.5
LOG2_E = math.log2(math.e)
FWD_HEADS_PER_STEP = 4
HEADS_PER_STEP = 2
N_PROJ = N_BRANCH * D_MODEL + 6 * BR_WIDTH + Q_RANK + KV_RANK + 2 * HEAD_PAD
COL_G, COL_M1, COL_M2, COL_B = 0, 4096, 4864, 5632

ADAM_LR, ADAM_B1, ADAM_B2, ADAM_EPS, ADAM_WD, ADAM_STEP = 0.001, 0.9, 0.999, 1e-08, 0.01, 10

VMEM_LIMIT = 56 * 1024 * 1024

WEIGHTS = ['norm_mix_pre', 'w_in', 'gate_b', 'q_norm', 'w_uq', 'kv_norm', 'w_ukv', 'w_br_mla', 'sg_ln_g', 'sg_ln_b',
           'sg_w', 'sg_b', 'w_br_sg', 'conv_w', 'w_br_conv', 'pool_w', 'pool_scale', 'w_br_pool', 'w_out',
           'norm_mix_post', 'norm_ffn_pre', 'w_ff1', 'w_ff2', 'norm_ffn_post']
COL_SHARDED = ['w_in', 'w_uq', 'w_ukv', 'w_br_mla', 'w_br_sg', 'w_br_conv', 'w_br_pool', 'w_ff1']
ROW_SHARDED = ['w_out', 'w_ff2']
MATMUL_SHARDED = ['w_in', 'w_uq', 'w_ukv', 'w_br_mla', 'w_br_sg', 'w_br_conv', 'w_br_pool', 'w_out', 'w_ff1', 'w_ff2']
BIG_SHARDED = ['w_in', 'w_ff1', 'w_ff2', 'w_out']
SMALL_SHARDED = ['w_uq', 'w_ukv', 'w_br_mla', 'w_br_sg', 'w_br_conv', 'w_br_pool']
SHARDED = MATMUL_SHARDED + ['conv_w']
REPLICATED = [n for n in WEIGHTS if n not in SHARDED]
PACK_COLS = 1024


def _params(sem, vmem=VMEM_LIMIT):
    return pltpu.CompilerParams(dimension_semantics=sem, vmem_limit_bytes=vmem)


def _mxu(a):
    return a.astype(MXU_DTYPE)


def _dot(a, b):
    return jnp.dot(_mxu(a), _mxu(b), preferred_element_type=F32)


def _dot_nt(a, b):
    return lax.dot_general(_mxu(a), _mxu(b), (((1,), (1,)), ((), ())), preferred_element_type=F32)


def _dot_tn(a, b):
    return lax.dot_general(_mxu(a), _mxu(b), (((0,), (0,)), ((), ())), preferred_element_type=F32)


def _rms(x, g):
    r = lax.rsqrt(jnp.mean(x * x, axis=-1, keepdims=True) + EPS)
    return x * r * g


def _rms_bwd(x, g, dy):
    r = lax.rsqrt(jnp.mean(x * x, axis=-1, keepdims=True) + EPS)
    xh = x * r
    gdy = dy * g
    dx = r * (gdy - xh * jnp.mean(gdy * xh, axis=-1, keepdims=True))
    return dx, jnp.sum(dy * xh, axis=0, keepdims=True)


_GELU_C = math.sqrt(2.0 / math.pi)


def _gelu(x):
    t = jnp.tanh(_GELU_C * (x + 0.044715 * (x * x * x)))
    return x * (0.5 * (1.0 + t)), t


def _gelu_grad(x, t):
    return 0.5 * (1.0 + t) + 0.5 * x * (1.0 - t * t) * (_GELU_C * (1.0 + 3.0 * 0.044715 * x * x))


def _sigmoid(x):
    return 1.0 / (1.0 + jnp.exp(-x))


def _full(shape):
    return pl.BlockSpec(shape, lambda *_: (0,) * len(shape))


def _resident(shape):
    return pl.BlockSpec(shape, lambda *_: (0,) * len(shape), pipeline_mode=pl.Buffered(1))


def _rows(ts, width, col=0):
    return pl.BlockSpec((ts, width), lambda i: (i, col))


def _tile(n, pref):
    return min(n, pref)


def _mm(name, a, w, *, tm, tn, prologue=None, rows=()):
    m, k = a.shape
    n = w.shape[1]
    tm, tn = _tile(m, tm), _tile(n, tn)

    def body(a_ref, *rest):
        row_refs, w_ref, o_ref = rest[:len(rows)], rest[len(rows)], rest[len(rows) + 1]
        av = a_ref[...]
        if prologue is not None:
            av = prologue(av, *[r[...] for r in row_refs])
        o_ref[...] = _dot(av, w_ref[...]).astype(o_ref.dtype)

    return pl.pallas_call(
        body, name=name, grid=(m // tm, n // tn),
        in_specs=[pl.BlockSpec((tm, k), lambda i, j: (i, 0))] + [pl.BlockSpec((1, k), lambda i, j: (0, 0)) for _ in rows]
        + [pl.BlockSpec((k, tn), lambda i, j: (0, j))],
        out_specs=pl.BlockSpec((tm, tn), lambda i, j: (i, j)),
        out_shape=jax.ShapeDtypeStruct((m, n), ACT_DTYPE),
        compiler_params=_params(("parallel", "parallel")),
    )(a, *rows, w)


def _mm_tn(name, a, b, *, tm, tn, prologue=None, rows=()):
    m, k = a.shape
    n = b.shape[1]
    tm, tn = _tile(m, tm), _tile(n, tn)

    def body(a_ref, *rest):
        row_refs, b_ref, o_ref = rest[:len(rows)], rest[len(rows)], rest[len(rows) + 1]

        @pl.when(pl.program_id(1) == 0)
        def _():
            o_ref[...] = jnp.zeros_like(o_ref)

        av = a_ref[...]
        if prologue is not None:
            av = prologue(av, *[r[...] for r in row_refs])
        o_ref[...] += _dot_tn(av, b_ref[...])

    return pl.pallas_call(
        body, name=name, grid=(n // tn, m // tm),
        in_specs=[pl.BlockSpec((tm, k), lambda j, i: (i, 0))] + [pl.BlockSpec((1, k), lambda j, i: (0, 0)) for _ in rows]
        + [pl.BlockSpec((tm, tn), lambda j, i: (i, j))],
        out_specs=pl.BlockSpec((k, tn), lambda j, i: (0, j)),
        out_shape=jax.ShapeDtypeStruct((k, n), F32),
        compiler_params=_params(("parallel", "arbitrary")),
    )(a, *rows, b)


def _relu_sq(a):
    r = jnp.maximum(a.astype(F32), 0.0)
    return r * r


HBM = pl.BlockSpec(memory_space=pl.ANY)


class _Exchange(NamedTuple):
    operands: Sequence[Any]
    out_shape: Sequence[Any]
    scratch_shapes: Sequence[Any]
    phases: Callable


def _run_exchange(name, ex):
    n_in, n_out = len(ex.operands), len(ex.out_shape)

    def body(*refs):
        for phase in ex.phases(refs[:n_in], refs[n_in:n_in + n_out], refs[n_in + n_out:]):
            phase()

    return pl.pallas_call(body, name=name, in_specs=[HBM] * n_in, out_specs=[HBM] * n_out, out_shape=list(ex.out_shape),
                          scratch_shapes=list(ex.scratch_shapes))(*ex.operands)


def _call_carrying(body, carried, *, name, grid, in_specs, out_specs, out_shape, operands, semantics):
    if carried is None:
        return pl.pallas_call(body, name=name, grid=grid, in_specs=in_specs, out_specs=out_specs, out_shape=out_shape,
                              compiler_params=_params(semantics))(*operands), None
    n_main_in, n_main_out = len(operands), len(out_shape)
    n_in, n_out = len(carried.operands), len(carried.out_shape)
    steps = math.prod(grid)

    def wrapped(*refs):
        main_in, refs = refs[:n_main_in], refs[n_main_in:]
        ex_in, refs = refs[:n_in], refs[n_in:]
        main_out, refs = refs[:n_main_out], refs[n_main_out:]
        ex_out, sems = refs[:n_out], refs[n_out:]
        step = pl.program_id(0)
        for axis in range(1, len(grid)):
            step = step * grid[axis] + pl.program_id(axis)
        start, middle, finish = carried.phases(ex_in, ex_out, sems)
        pl.when(step == 0)(start)
        pl.when(step == (steps - 1) // 2)(middle)
        body(*main_in, *main_out)
        pl.when(step == steps - 1)(finish)

    out = pl.pallas_call(
        wrapped, name=name + "_carrying", grid=grid, in_specs=list(in_specs) + [HBM] * n_in,
        out_specs=list(out_specs) + [HBM] * n_out, out_shape=list(out_shape) + list(carried.out_shape),
        scratch_shapes=list(carried.scratch_shapes), compiler_params=_params(("arbitrary",) * len(grid)))(*operands, *carried.operands)
    return out[:n_main_out], out[n_main_out:]


def _qkv_prep(proj, q_norm, kv_norm, wq, wkv, cq_tab, s_tab, cr_tab):
    s = proj.shape[0]
    ts = _tile(s, 512)
    hw = N_HEADS * HEAD_PAD

    def body(cq_ref, ckv_ref, kr_ref, krs_ref, gq_ref, gkv_ref, wq_ref, wkv_ref, ct_ref, st_ref, crt_ref,
             q_ref, k_ref, v_ref):
        ct, st, crt = ct_ref[...], st_ref[...], crt_ref[...]
        qn = _rms(cq_ref[...].astype(F32), gq_ref[...])
        qab = _dot(qn, wq_ref[...])
        kvn = _rms(ckv_ref[...].astype(F32), gkv_ref[...])
        kav = _dot(kvn, wkv_ref[...])
        k_rope = kr_ref[...].astype(F32) * crt + krs_ref[...].astype(F32) * st
        ones_lane = (lax.broadcasted_iota(jnp.int32, (1, HEAD_PAD), 1) == V_HEAD).astype(F32)
        for h in range(N_HEADS):
            lo = h * HEAD_PAD
            q_ref[h] = (qab[:, lo:lo + HEAD_PAD] * ct + qab[:, hw + lo:hw + lo + HEAD_PAD] * st).astype(q_ref.dtype)
            k_ref[h] = (kav[:, lo:lo + HEAD_PAD] + k_rope).astype(k_ref.dtype)
            v_ref[h] = (kav[:, hw + lo:hw + lo + HEAD_PAD] + ones_lane).astype(v_ref.dtype)

    head_spec = pl.BlockSpec((N_HEADS, ts, HEAD_PAD), lambda i: (0, i, 0))
    head_shape = jax.ShapeDtypeStruct((N_HEADS, s, HEAD_PAD), MXU_DTYPE)
    return pl.pallas_call(
        body, name="qkv_prep", grid=(s // ts,),
        in_specs=[_rows(ts, Q_RANK, COL_B // Q_RANK), _rows(ts, KV_RANK, (COL_B + Q_RANK) // KV_RANK),
                  _rows(ts, HEAD_PAD, (COL_B + Q_RANK + KV_RANK) // HEAD_PAD),
                  _rows(ts, HEAD_PAD, (COL_B + Q_RANK + KV_RANK + HEAD_PAD) // HEAD_PAD),
                  _full((1, Q_RANK)), _full((1, KV_RANK)), _full((Q_RANK, 2 * hw)), _full((KV_RANK, 2 * hw)),
                  _rows(ts, HEAD_PAD), _rows(ts, HEAD_PAD), _rows(ts, HEAD_PAD)],
        out_specs=[head_spec, head_spec, head_spec],
        out_shape=[head_shape, head_shape, head_shape],
        compiler_params=_params(("parallel",)),
    )(proj, proj, proj, proj, q_norm, kv_norm, wq, wkv, cq_tab, s_tab, cr_tab)


def _diagonal_mask(t):
    return lax.broadcasted_iota(jnp.int32, (t, t), 1) <= lax.broadcasted_iota(jnp.int32, (t, t), 0)


def _attn_fwd(q, k, v, carried=None):
    s = q.shape[1]
    t = _tile(s, 512)

    def body(q_ref, k_ref, v_ref, o_ref, lse_ref):
        i = pl.program_id(1)
        lane = lax.broadcasted_iota(jnp.int32, (1, HEAD_PAD), 1)

        def step(j, carry, on_diagonal):
            rows = pl.ds(pl.multiple_of(j * t, t), t)
            out = []
            for h in range(FWD_HEADS_PER_STEP):
                m, acc = carry[h]
                sc = _dot_nt(q_ref[h], k_ref[h, rows, :]) * (ATTN_SCALE * LOG2_E)
                if on_diagonal:
                    sc = jnp.where(_diagonal_mask(t), sc, -jnp.inf)
                m_new = jnp.maximum(m, jnp.max(sc, axis=1, keepdims=True))
                out.append((m_new, jnp.exp2(m - m_new) * acc + _dot(jnp.exp2(sc - m_new), v_ref[h, rows, :])))
            return tuple(out)

        init = ((jnp.full((t, 1), -jnp.inf, F32), jnp.zeros((t, HEAD_PAD), F32)),) * FWD_HEADS_PER_STEP
        below = lax.fori_loop(0, i, functools.partial(step, on_diagonal=False), init)
        for h, (m, acc) in enumerate(step(i, below, True)):
            l = jnp.sum(jnp.where(lane == V_HEAD, acc, 0.0), axis=1, keepdims=True)
            o_ref[:, h * HEAD_PAD:(h + 1) * HEAD_PAD] = jnp.where(lane < V_HEAD, acc / l, 0.0)
            lse_ref[h] = m + jnp.log2(l)

    group = FWD_HEADS_PER_STEP
    return _call_carrying(
        body, carried, name="attn_fwd", grid=(N_HEADS // group, s // t),
        in_specs=[pl.BlockSpec((group, t, HEAD_PAD), lambda h, i: (h, i, 0)),
                  pl.BlockSpec((group, s, HEAD_PAD), lambda h, i: (h, 0, 0)),
                  pl.BlockSpec((group, s, HEAD_PAD), lambda h, i: (h, 0, 0))],
        out_specs=[pl.BlockSpec((t, group * HEAD_PAD), lambda h, i: (i, h)), pl.BlockSpec((group, t, 1), lambda h, i: (h, i, 0))],
        out_shape=[jax.ShapeDtypeStruct((s, N_HEADS * HEAD_PAD), F32), jax.ShapeDtypeStruct((N_HEADS, s, 1), F32)],
        operands=(q, k, v), semantics=("parallel", "parallel"))


def _attn_bwd(q, k, v, do, lse, delta, carried=None):
    s = q.shape[1]
    t = _tile(s, 512)
    nq = s // t

    def body(q_ref, do_ref, lse_ref, dl_ref, k_ref, v_ref, dq_ref, dk_ref, dv_ref):
        j = pl.program_id(1)

        @pl.when(j == 0)
        def _():
            dq_ref[...] = jnp.zeros_like(dq_ref)

        def step(i, carry, on_diagonal):
            rows = pl.ds(pl.multiple_of(i * t, t), t)
            dq = [dq_ref[h, rows, :] for h in range(HEADS_PER_STEP)]
            out = []
            for h in range(HEADS_PER_STEP):
                dk, dv = carry[h]
                qi, doi = q_ref[h, rows, :], do_ref[rows, h * HEAD_PAD:(h + 1) * HEAD_PAD]
                sc = _dot_nt(qi, k_ref[h]) * (ATTN_SCALE * LOG2_E)
                if on_diagonal:
                    sc = jnp.where(_diagonal_mask(t), sc, -jnp.inf)
                p = jnp.exp2(sc - lse_ref[h, rows, :])
                dv = dv + _dot_tn(p, doi)
                ds = p * (_dot_nt(doi, v_ref[h]) - dl_ref[h, rows, :])
                dk = dk + _dot_tn(ds, qi)
                dq[h] = dq[h] + _dot(ds, k_ref[h]) * ATTN_SCALE
                out.append((dk, dv))
            for h in range(HEADS_PER_STEP):
                dq_ref[h, rows, :] = dq[h]
            return tuple(out)

        zero = ((jnp.zeros((t, HEAD_PAD), F32),) * 2,) * HEADS_PER_STEP
        sums = lax.fori_loop(j + 1, nq, functools.partial(step, on_diagonal=False), step(j, zero, True))
        for h, (dk, dv) in enumerate(sums):
            dk_ref[h] = dk * ATTN_SCALE
            dv_ref[h] = dv

    group = HEADS_PER_STEP
    whole = lambda w: pl.BlockSpec((group, s, w), lambda h, j: (h, 0, 0), pipeline_mode=pl.Buffered(1))
    tile = pl.BlockSpec((group, t, HEAD_PAD), lambda h, j: (h, j, 0))
    head_shape = jax.ShapeDtypeStruct((N_HEADS, s, HEAD_PAD), F32)
    return _call_carrying(
        body, carried, name="attn_bwd", grid=(N_HEADS // group, nq),
        in_specs=[whole(HEAD_PAD), pl.BlockSpec((s, group * HEAD_PAD), lambda h, j: (0, h), pipeline_mode=pl.Buffered(1)),
                  whole(1), whole(1), tile, tile],
        out_specs=[whole(HEAD_PAD), tile, tile],
        out_shape=[head_shape, head_shape, head_shape],
        operands=(q, do, lse, delta, k, v), semantics=("parallel", "arbitrary"))


def _qkv_bwd(dq, dk, dv, proj, q_norm, kv_norm, wq, wkv, cq_tab, s_tab, cr_tab):
    s = proj.shape[0]
    ts = _tile(s, 512)
    hw = N_HEADS * HEAD_PAD

    def body(dq_ref, dk_ref, dv_ref, cq_ref, ckv_ref, gq_ref, gkv_ref, wq_ref, wkv_ref, ct_ref, st_ref, crt_ref,
             dpb_ref, dwq_ref, dwkv_ref, dgq_ref, dgkv_ref):
        @pl.when(pl.program_id(0) == 0)
        def _():
            for r in (dwq_ref, dwkv_ref, dgq_ref, dgkv_ref):
                r[...] = jnp.zeros_like(r)

        ct, st, crt = ct_ref[...], st_ref[...], crt_ref[...]
        dqs = [dq_ref[h] for h in range(N_HEADS)]
        dks = [dk_ref[h] for h in range(N_HEADS)]
        dqab = jnp.concatenate([d * ct for d in dqs] + [d * st for d in dqs], axis=1)
        dkav = jnp.concatenate(dks + [dv_ref[h] for h in range(N_HEADS)], axis=1)
        dk_sum = dks[0] + dks[1] + dks[2] + dks[3]
        cq, ckv, gq, gkv = cq_ref[...].astype(F32), ckv_ref[...].astype(F32), gq_ref[...], gkv_ref[...]
        dwq_ref[...] += _dot_tn(_rms(cq, gq), dqab)
        dwkv_ref[...] += _dot_tn(_rms(ckv, gkv), dkav)
        dcq, dgq = _rms_bwd(cq, gq, _dot_nt(dqab, wq_ref[...]))
        dckv, dgkv = _rms_bwd(ckv, gkv, _dot_nt(dkav, wkv_ref[...]))
        dgq_ref[...] += dgq
        dgkv_ref[...] += dgkv
        dpb_ref[...] = jnp.concatenate([dcq, dckv, dk_sum * crt, dk_sum * st], axis=1).astype(dpb_ref.dtype)

    head_spec = pl.BlockSpec((N_HEADS, ts, HEAD_PAD), lambda i: (0, i, 0))
    wb = Q_RANK + KV_RANK + 2 * HEAD_PAD
    return pl.pallas_call(
        body, name="qkv_bwd", grid=(s // ts,),
        in_specs=[head_spec, head_spec, head_spec,
                  _rows(ts, Q_RANK, COL_B // Q_RANK), _rows(ts, KV_RANK, (COL_B + Q_RANK) // KV_RANK),
                  _full((1, Q_RANK)), _full((1, KV_RANK)), _full((Q_RANK, 2 * hw)), _full((KV_RANK, 2 * hw)),
                  _rows(ts, HEAD_PAD), _rows(ts, HEAD_PAD), _rows(ts, HEAD_PAD)],
        out_specs=[_rows(ts, wb), _full((Q_RANK, 2 * hw)), _full((KV_RANK, 2 * hw)), _full((1, Q_RANK)), _full((1, KV_RANK))],
        out_shape=[jax.ShapeDtypeStruct((s, wb), MXU_DTYPE), jax.ShapeDtypeStruct((Q_RANK, 2 * hw), F32),
                   jax.ShapeDtypeStruct((KV_RANK, 2 * hw), F32), jax.ShapeDtypeStruct((1, Q_RANK), F32),
                   jax.ShapeDtypeStruct((1, KV_RANK), F32)],
        compiler_params=_params(("arbitrary",)),
    )(dq, dk, dv, proj, proj, q_norm, kv_norm, wq, wkv, cq_tab, s_tab, cr_tab)


def _lane_group(width):
    return lax.broadcasted_iota(jnp.int32, (1, width), 1) // (width // 4)


def _shift_down(a, k):
    return pltpu.roll(a, k, 0)


def _shift_up(a, k):
    return pltpu.roll(a, a.shape[0] - k, 0)


def _window_sums(xh, shift):
    s2 = xh + shift(xh, 1)
    s4 = s2 + shift(s2, 2)
    s8 = s4 + shift(s4, 4)
    s16 = s8 + shift(s8, 8)
    grp = _lane_group(xh.shape[1])
    return jnp.where(grp == 0, s2, jnp.where(grp == 1, s4, jnp.where(grp == 2, s8, s16)))


def _pool_count(i, ts):
    grp = _lane_group(BR_WIDTH)
    win = jnp.where(grp == 0, 2.0, jnp.where(grp == 1, 4.0, jnp.where(grp == 2, 8.0, 16.0)))
    t = (i * ts + lax.broadcasted_iota(jnp.int32, (ts, 1), 0)).astype(F32)
    return jnp.minimum(t + 1.0, win)


def _mix_forward(i, ts, r):
    f = {}
    act = lambda name: r[name][...].astype(F32)
    f['gates'] = _sigmoid(act('gate') + r['gate_b'][...])
    sgu, sgv = act('sgu'), act('sgv')
    f['sgu'], f['sgv'] = sgu, sgv
    u_act, f['tu'] = _gelu(sgu)
    vg, f['tv'] = _gelu(sgv)
    mu = jnp.mean(vg, axis=-1, keepdims=True)
    xc = vg - mu
    f['ln_r'] = lax.rsqrt(jnp.mean(xc * xc, axis=-1, keepdims=True) + EPS)
    f['ln_xh'] = xc * f['ln_r']
    vln = f['ln_xh'] * r['ln_g'][...] + r['ln_b'][...]
    tril = lax.broadcasted_iota(jnp.int32, (SG_CHUNK, SG_CHUNK), 1) <= lax.broadcasted_iota(jnp.int32, (SG_CHUNK, SG_CHUNK), 0)
    f['wm'] = [_mxu(jnp.where(tril, r['sg_w'][g], 0.0)) for g in range(SG_GROUPS)]
    f['tril'] = tril
    grp = _lane_group(BR_WIDTH)
    bias = r['sg_bias'][...]
    parts = []
    for ci in range(ts // SG_CHUNK):
        vc = vln[ci * SG_CHUNK:(ci + 1) * SG_CHUNK]
        sc = bias
        for g in range(SG_GROUPS):
            sc = sc + jnp.where(grp == g, _dot(f['wm'][g], vc), 0.0)
        parts.append(sc)
    f['vln'] = vln
    f['sg_s'] = parts[0] if len(parts) == 1 else jnp.concatenate(parts, axis=0)
    f['u_act'] = u_act
    out_b = u_act * f['sg_s']
    first = (i > 0).astype(F32)
    cvx, cvc, cvb = act('cvx'), act('cvc'), act('cvb')
    f['cvx'], f['cvc'], f['cvb'] = cvx, cvc, cvb
    zh = jnp.concatenate([act('hx') * act('hc') * first, cvc * cvx], axis=0)
    f['z1'] = _shift_down(zh, 1)[HALO:]
    f['z2'] = _shift_down(zh, 2)[HALO:]
    f['z0'] = zh[HALO:]
    f['yv'] = r['conv_w'][0:1, :] * f['z2'] + r['conv_w'][1:2, :] * f['z1'] + r['conv_w'][2:3, :] * f['z0']
    out_c = cvb * f['yv']
    p = act('pool')
    ph = jnp.concatenate([act('hp') * first, p], axis=0)
    f['cnt'] = _pool_count(i, ts)
    f['pooled'] = _window_sums(ph, _shift_down)[HALO:] / f['cnt'] - p
    f['mixed'] = _dot(f['pooled'], r['wbd'][...])
    out_d = f['mixed'] * r['pool_scale'][...]
    f['outs'] = [r['o'][...], out_b, out_c, out_d]
    f['ys'] = [_dot(f['outs'][b], r['w_br'][b][...]) for b in range(N_BRANCH)]
    merged = f['gates'][:, 0:D_MODEL] * f['ys'][0]
    for b in range(1, N_BRANCH):
        merged = merged + f['gates'][:, b * D_MODEL:(b + 1) * D_MODEL] * f['ys'][b]
    f['merged'] = merged
    f['mo'] = _dot(merged, r['w_out'][...])
    return f


_MIX_TILE_INPUTS = ['gate', 'sgu', 'sgv', 'cvb', 'cvx', 'cvc', 'pool', 'hx', 'hc', 'hp', 'o']
_MIX_WEIGHTS = ['gate_b', 'ln_g', 'ln_b', 'sg_w', 'sg_bias', 'conv_w', 'wbd', 'pool_scale', 'w_br0', 'w_br1', 'w_br2',
                'w_br3', 'w_out', 'g_post']


def _mix_specs(s, ts):
    c0 = COL_M1 // BR_WIDTH
    prev = lambda col: pl.BlockSpec((HALO, BR_WIDTH), lambda i: (jnp.maximum(i * (ts // HALO) - 1, 0), col))
    tiles = [_rows(ts, N_BRANCH * D_MODEL, 0), _rows(ts, BR_WIDTH, c0), _rows(ts, BR_WIDTH, c0 + 1), _rows(ts, BR_WIDTH, c0 + 2),
             _rows(ts, BR_WIDTH, c0 + 3), _rows(ts, BR_WIDTH, c0 + 4), _rows(ts, BR_WIDTH, c0 + 5),
             prev(c0 + 3), prev(c0 + 4), prev(c0 + 5), _rows(ts, N_HEADS * HEAD_PAD)]
    weights = [_full((1, N_BRANCH * D_MODEL)), _full((1, BR_WIDTH)), _full((1, BR_WIDTH)),
               _full((SG_GROUPS, SG_CHUNK, SG_CHUNK)), _full((SG_CHUNK, BR_WIDTH)), _full((8, BR_WIDTH)),
               _resident((BR_WIDTH, BR_WIDTH)), _full((1, BR_WIDTH)), _resident((N_HEADS * HEAD_PAD, D_MODEL)),
               _resident((BR_WIDTH, D_MODEL)), _resident((BR_WIDTH, D_MODEL)), _resident((BR_WIDTH, D_MODEL)),
               _resident((D_MODEL, D_MODEL)), _full((1, D_MODEL))]
    return tiles, weights


def _mix_refs(refs):
    names = _MIX_TILE_INPUTS + _MIX_WEIGHTS
    r = dict(zip(names, refs[:len(names)]))
    r['w_br'] = [r['w_br0'], r['w_br1'], r['w_br2'], r['w_br3']]
    return r, refs[len(names):]


def _mix_operands(proj, o, lw):
    return ([proj] * 10 + [o] + [lw[n] for n in ['gate_b', 'sg_ln_g', 'sg_ln_b', 'sg_w', 'sg_bias', 'conv_w8', 'wbd',
                                                 'pool_scale', 'w_br_mla_p', 'w_br_sg', 'w_br_conv', 'w_br_pool', 'w_out',
                                                 'norm_mix_post']])


def _mix_fwd(x0, proj, o, lw):
    s = x0.shape[0]
    ts = _tile(s, 512)
    tiles, weights = _mix_specs(s, ts)

    def body(*refs):
        r, (x0_ref, x1_ref) = _mix_refs(refs)
        f = _mix_forward(pl.program_id(0), ts, r)
        x1_ref[...] = x0_ref[...] + _rms(f['mo'], r['g_post'][...])

    return pl.pallas_call(
        body, name="mix_fwd", grid=(s // ts,),
        in_specs=tiles + weights + [_rows(ts, D_MODEL)],
        out_specs=_rows(ts, D_MODEL),
        out_shape=jax.ShapeDtypeStruct((s, D_MODEL), F32),
        compiler_params=_params(("parallel",)),
    )(*_mix_operands(proj, o, lw), x0)


def _mix_bwd(dx1, proj, o, lw):
    s = dx1.shape[0]
    ts = _tile(s, 256)
    tiles, weights = _mix_specs(s, ts)
    hw = N_HEADS * HEAD_PAD

    def body(*refs):
        r, rest = _mix_refs(refs)
        (dx1_ref, dg_ref, dm1_ref, dyv_ref, up_ref, do_ref, delta_ref,
         dgate_b_ref, dln_g_ref, dln_b_ref, dsgw_ref, dsgb_ref, dconv_ref, dwbd_ref, dps_ref,
         dwbr0_ref, dwbr1_ref, dwbr2_ref, dwbr3_ref, dwout_ref, dgpost_ref, dbias_acc) = rest
        i = pl.program_id(0)
        acc_refs = [dgate_b_ref, dln_g_ref, dln_b_ref, dsgw_ref, dsgb_ref, dconv_ref, dwbd_ref, dps_ref,
                    dwbr0_ref, dwbr1_ref, dwbr2_ref, dwbr3_ref, dwout_ref, dgpost_ref, dbias_acc]

        @pl.when(i == 0)
        def _():
            for a in acc_refs:
                a[...] = jnp.zeros_like(a)

        f = _mix_forward(i, ts, r)
        dmo, dgpost = _rms_bwd(f['mo'], r['g_post'][...], dx1_ref[...])
        dgpost_ref[...] += dgpost
        dwout_ref[...] += _dot_tn(f['merged'], dmo)
        dmerged = _dot_nt(dmo, r['w_out'][...])
        dwbr = [dwbr0_ref, dwbr1_ref, dwbr2_ref, dwbr3_ref]
        douts = []
        for b in range(N_BRANCH):
            gb = f['gates'][:, b * D_MODEL:(b + 1) * D_MODEL]
            dgate = dmerged * f['ys'][b] * gb * (1.0 - gb)
            dg_ref[:, b * D_MODEL:(b + 1) * D_MODEL] = dgate.astype(dg_ref.dtype)
            dgate_b_ref[:, b * D_MODEL:(b + 1) * D_MODEL] += jnp.sum(dgate, axis=0, keepdims=True)
            dy = dmerged * gb
            dwbr[b][...] += _dot_tn(f['outs'][b], dy)
            douts.append(_dot_nt(dy, r['w_br'][b][...]))
        do = douts[0]
        do_ref[...] = do.astype(do_ref.dtype)
        prod = do * f['outs'][0]
        for h in range(N_HEADS):
            delta_ref[h] = jnp.sum(prod[:, h * HEAD_PAD:(h + 1) * HEAD_PAD], axis=1, keepdims=True)
        grp = _lane_group(BR_WIDTH)
        ds = douts[1] * f['u_act']
        dsgu = douts[1] * f['sg_s'] * _gelu_grad(f['sgu'], f['tu'])
        dvln_parts = []
        for ci in range(ts // SG_CHUNK):
            rows = slice(ci * SG_CHUNK, (ci + 1) * SG_CHUNK)
            ds_c, vln_c = ds[rows], f['vln'][rows]
            dvln_c = jnp.zeros((SG_CHUNK, BR_WIDTH), F32)
            for g in range(SG_GROUPS):
                dvln_c = dvln_c + jnp.where(grp == g, _dot_tn(f['wm'][g], ds_c), 0.0)
                dsgw_ref[g] += jnp.where(f['tril'], _dot_nt(jnp.where(grp == g, ds_c, 0.0), vln_c), 0.0)
            dbias_acc[...] += ds_c
            dvln_parts.append(dvln_c)
        dvln = dvln_parts[0] if len(dvln_parts) == 1 else jnp.concatenate(dvln_parts, axis=0)
        dln_g_ref[...] += jnp.sum(dvln * f['ln_xh'], axis=0, keepdims=True)
        dln_b_ref[...] += jnp.sum(dvln, axis=0, keepdims=True)
        dxh = dvln * r['ln_g'][...]
        dvg = f['ln_r'] * (dxh - jnp.mean(dxh, axis=-1, keepdims=True)
                           - f['ln_xh'] * jnp.mean(dxh * f['ln_xh'], axis=-1, keepdims=True))
        dsgv = dvg * _gelu_grad(f['sgv'], f['tv'])
        dcvb = douts[2] * f['yv']
        dyv = douts[2] * f['cvb']
        dyv_ref[...] = dyv
        for kk, zk in enumerate((f['z2'], f['z1'], f['z0'])):
            dconv_ref[kk:kk + 1, :] += jnp.sum(dyv * zk, axis=0, keepdims=True)
        dps_ref[...] += jnp.sum(douts[3] * f['mixed'], axis=0, keepdims=True)
        dmixed = douts[3] * r['pool_scale'][...]
        dwbd_ref[...] += _dot_tn(f['pooled'], dmixed)
        up_ref[...] = _dot_nt(dmixed, r['wbd'][...]) / f['cnt']
        dm1_ref[...] = jnp.concatenate([dsgu, dsgv, dcvb], axis=1).astype(dm1_ref.dtype)

        @pl.when(i == pl.num_programs(0) - 1)
        def _():
            lane = lax.broadcasted_iota(jnp.int32, (1, SG_CHUNK), 1)
            db = dbias_acc[...]
            out = jnp.zeros((SG_CHUNK, SG_CHUNK), F32)
            for g in range(SG_GROUPS):
                out = out + jnp.where(lane == g, jnp.sum(jnp.where(grp == g, db, 0.0), axis=1, keepdims=True), 0.0)
            dsgb_ref[...] = out

    acc = lambda shape: (_full(shape), jax.ShapeDtypeStruct(shape, F32))
    accs = [acc((1, N_BRANCH * D_MODEL)), acc((1, BR_WIDTH)), acc((1, BR_WIDTH)), acc((SG_GROUPS, SG_CHUNK, SG_CHUNK)),
            acc((SG_CHUNK, SG_CHUNK)), acc((8, BR_WIDTH)), acc((BR_WIDTH, BR_WIDTH)), acc((1, BR_WIDTH)),
            acc((hw, D_MODEL)), acc((BR_WIDTH, D_MODEL)), acc((BR_WIDTH, D_MODEL)), acc((BR_WIDTH, D_MODEL)),
            acc((D_MODEL, D_MODEL)), acc((1, D_MODEL))]
    tile_outs = [(_rows(ts, N_BRANCH * D_MODEL), jax.ShapeDtypeStruct((s, N_BRANCH * D_MODEL), MXU_DTYPE)),
                 (_rows(ts, 3 * BR_WIDTH), jax.ShapeDtypeStruct((s, 3 * BR_WIDTH), MXU_DTYPE)),
                 (_rows(ts, BR_WIDTH), jax.ShapeDtypeStruct((s, BR_WIDTH), F32)),
                 (_rows(ts, BR_WIDTH), jax.ShapeDtypeStruct((s, BR_WIDTH), F32)),
                 (_rows(ts, hw), jax.ShapeDtypeStruct((s, hw), MXU_DTYPE)),
                 (pl.BlockSpec((N_HEADS, ts, 1), lambda i: (0, i, 0)), jax.ShapeDtypeStruct((N_HEADS, s, 1), F32))]
    outs = tile_outs + accs
    return pl.pallas_call(
        body, name="mix_bwd", grid=(s // ts,),
        in_specs=tiles + weights + [_rows(ts, D_MODEL)],
        out_specs=[o_[0] for o_ in outs], out_shape=[o_[1] for o_ in outs],
        scratch_shapes=[pltpu.VMEM((SG_CHUNK, BR_WIDTH), F32)],
        compiler_params=_params(("arbitrary",), 60 * 1024 * 1024),
    )(*_mix_operands(proj, o, lw), dx1)


def _shift_bwd(dyv, upool, proj, conv_w8):
    s = dyv.shape[0]
    ts = _tile(s, 512)
    nb = s // HALO
    c0 = COL_M1 // BR_WIDTH

    def body(dyv_ref, dyvn_ref, up_ref, upn_ref, cvx_ref, cvc_ref, cw_ref, out_ref):
        i = pl.program_id(0)
        last = (i < pl.num_programs(0) - 1).astype(F32)
        dh = jnp.concatenate([dyv_ref[...], dyvn_ref[...] * last], axis=0)
        dz = (cw_ref[2:3, :] * dh + cw_ref[1:2, :] * _shift_up(dh, 1) + cw_ref[0:1, :] * _shift_up(dh, 2))[:ts]
        up = up_ref[...]
        uh = jnp.concatenate([up, upn_ref[...] * last], axis=0)
        dpool = _window_sums(uh, _shift_up)[:ts] - up * _pool_count(i, ts)
        out_ref[...] = jnp.concatenate([dz * cvc_ref[...].astype(F32), dz * cvx_ref[...].astype(F32), dpool],
                                       axis=1).astype(out_ref.dtype)

    nxt = pl.BlockSpec((HALO, BR_WIDTH), lambda i: (jnp.minimum((i + 1) * (ts // HALO), nb - 1), 0))
    return pl.pallas_call(
        body, name="shift_bwd", grid=(s // ts,),
        in_specs=[_rows(ts, BR_WIDTH), nxt, _rows(ts, BR_WIDTH), nxt, _rows(ts, BR_WIDTH, c0 + 3), _rows(ts, BR_WIDTH, c0 + 4),
                  _full((8, BR_WIDTH))],
        out_specs=_rows(ts, 3 * BR_WIDTH),
        out_shape=jax.ShapeDtypeStruct((s, 3 * BR_WIDTH), MXU_DTYPE),
        compiler_params=_params(("parallel",)),
    )(dyv, dyv, upool, upool, proj, proj, conv_w8)


def _ffn2(a, w2, x1, g):
    s = a.shape[0]
    ts = _tile(s, 512)

    def body(a_ref, w_ref, x1_ref, g_ref, x2_ref, f_ref):
        f = _dot(_relu_sq(a_ref[...]), w_ref[...])
        f_ref[...] = f
        x2_ref[...] = x1_ref[...] + _rms(f, g_ref[...])

    return pl.pallas_call(
        body, name="ffn2", grid=(s // ts,),
        in_specs=[_rows(ts, D_FF), _resident((D_FF, D_MODEL)), _rows(ts, D_MODEL), _full((1, D_MODEL))],
        out_specs=[_rows(ts, D_MODEL), _rows(ts, D_MODEL)],
        out_shape=[jax.ShapeDtypeStruct((s, D_MODEL), F32)] * 2,
        compiler_params=_params(("parallel",)),
    )(a, w2, x1, g)


def _ffn2_bwd(dx2, f, g, a, w2t, carried=None):
    s = a.shape[0]
    ts = _tile(s, 512)

    def body(dx2_ref, f_ref, g_ref, a_ref, w_ref, df_ref, da_ref, dg_ref):
        @pl.when(pl.program_id(0) == 0)
        def _():
            dg_ref[...] = jnp.zeros_like(dg_ref)

        df, dg = _rms_bwd(f_ref[...], g_ref[...], dx2_ref[...])
        dg_ref[...] += dg
        df_ref[...] = df.astype(df_ref.dtype)
        da_ref[...] = (_dot(df, w_ref[...]) * (2.0 * jnp.maximum(a_ref[...].astype(F32), 0.0))).astype(da_ref.dtype)

    return _call_carrying(
        body, carried, name="ffn2_bwd", grid=(s // ts,),
        in_specs=[_rows(ts, D_MODEL), _rows(ts, D_MODEL), _full((1, D_MODEL)), _rows(ts, D_FF), _resident((D_MODEL, D_FF))],
        out_specs=[_rows(ts, D_MODEL), _rows(ts, D_FF), _full((1, D_MODEL))],
        out_shape=[jax.ShapeDtypeStruct((s, D_MODEL), MXU_DTYPE), jax.ShapeDtypeStruct((s, D_FF), MXU_DTYPE),
                   jax.ShapeDtypeStruct((1, D_MODEL), F32)],
        operands=(dx2, f, g, a, w2t), semantics=("arbitrary",))


def _norm_in_bwd(name, pieces, x, g, dres):
    s = x.shape[0]
    ts = _tile(s, 512)
    n = len(pieces)

    def body(*refs):
        d_refs, w_refs = refs[:n], refs[n:2 * n]
        x_ref, g_ref, dres_ref, dx_ref, dg_ref = refs[2 * n:]

        @pl.when(pl.program_id(0) == 0)
        def _():
            dg_ref[...] = jnp.zeros_like(dg_ref)

        dh = _dot(d_refs[0][...], w_refs[0][...])
        for p in range(1, n):
            dh = dh + _dot(d_refs[p][...], w_refs[p][...])
        dx, dg = _rms_bwd(x_ref[...], g_ref[...], dh)
        dg_ref[...] += dg
        dx_ref[...] = dres_ref[...] + dx

    return pl.pallas_call(
        body, name=name, grid=(s // ts,),
        in_specs=[_rows(ts, d.shape[1]) for d, _ in pieces] + [_resident(w.shape) for _, w in pieces]
        + [_rows(ts, D_MODEL), _full((1, D_MODEL)), _rows(ts, D_MODEL)],
        out_specs=[_rows(ts, D_MODEL), _full((1, D_MODEL))],
        out_shape=[jax.ShapeDtypeStruct((s, D_MODEL), F32), jax.ShapeDtypeStruct((1, D_MODEL), F32)],
        compiler_params=_params(("arbitrary",)),
    )(*[d for d, _ in pieces], *[w for _, w in pieces], x, g, dres)


def _loss_and_grad(y, target):
    s = y.shape[0]
    ts = _tile(s, 512)

    def body(y_ref, t_ref, dy_ref, loss_ref):
        @pl.when(pl.program_id(0) == 0)
        def _():
            loss_ref[...] = jnp.zeros_like(loss_ref)

        err = y_ref[...] - t_ref[...]
        dy_ref[...] = err * (1.0 / D_MODEL)
        loss_ref[...] += 0.5 * jnp.sum(jnp.mean(err * err, axis=-1, keepdims=True), axis=0, keepdims=True)

    dy, loss = pl.pallas_call(
        body, name="loss", grid=(s // ts,),
        in_specs=[_rows(ts, D_MODEL), _rows(ts, D_MODEL)],
        out_specs=[_rows(ts, D_MODEL), _full((8, 128))],
        out_shape=[jax.ShapeDtypeStruct((s, D_MODEL), F32), jax.ShapeDtypeStruct((8, 128), F32)],
        compiler_params=_params(("arbitrary",)),
    )(y, target)
    return loss[0, 0], dy


_W_IN_SPLITS = [256, 384, 416, 672, 928, 1184, 1440, 1696, 1952]


def _rope_swap(w):
    half = QK_ROPE // 2
    return jnp.concatenate([-w[..., half:], w[..., :half]], axis=-1)


def _rope_unswap(d):
    half = QK_ROPE // 2
    return jnp.concatenate([d[..., half:], -d[..., :half]], axis=-1)


def _zeros_like_cols(w, n):
    return jnp.zeros(w.shape[:-1] + (n,), w.dtype)


def _derive_weights(w):
    md = MXU_DTYPE
    nl = w['w_in'].shape[0]
    c_q, c_kv, k_r, sg_u, sg_v, cv_x, cv_b, cv_c, pool, gate = jnp.split(w['w_in'].astype(md), _W_IN_SPLITS, axis=-1)
    pad_rope = lambda r: jnp.concatenate([_zeros_like_cols(r, QK_NOPE), r, _zeros_like_cols(r, HEAD_PAD - QK_NOPE - QK_ROPE)], -1)
    w_in_p = jnp.concatenate([gate, sg_u, sg_v, cv_b, cv_x, cv_c, pool, c_q, c_kv, pad_rope(k_r), pad_rope(_rope_swap(k_r))], -1)
    wq = w['w_uq'].astype(md).reshape(nl, Q_RANK, N_HEADS, QK_NOPE + QK_ROPE)
    nope, rope_w = wq[..., :QK_NOPE], wq[..., QK_NOPE:]
    wq_a = jnp.concatenate([nope, rope_w, _zeros_like_cols(nope, 32)], -1).reshape(nl, Q_RANK, N_HEADS * HEAD_PAD)
    wq_b = pad_rope(_rope_swap(rope_w)).reshape(nl, Q_RANK, N_HEADS * HEAD_PAD)
    wkv = w['w_ukv'].astype(md).reshape(nl, KV_RANK, N_HEADS, QK_NOPE + V_HEAD)
    pad_half = lambda r: jnp.concatenate([r, _zeros_like_cols(r, HEAD_PAD - r.shape[-1])], -1).reshape(nl, KV_RANK, N_HEADS * HEAD_PAD)
    w_br_mla = w['w_br_mla'].astype(md).reshape(nl, N_HEADS, V_HEAD, D_MODEL)
    w_br_mla_p = jnp.concatenate([w_br_mla, jnp.zeros_like(w_br_mla)], axis=2).reshape(nl, N_HEADS * HEAD_PAD, D_MODEL)
    eye = jnp.eye(4, dtype=md)
    wbd = (w['pool_w'].astype(md)[:, :, :, None, :] * eye[None, :, None, :, None]).reshape(nl, BR_WIDTH, BR_WIDTH)
    row = lambda a: a.astype(F32)[:, None, :]
    w_in_pt = jnp.swapaxes(w_in_p, 1, 2)
    return dict(
        w_in_p=w_in_p, wt_g=w_in_pt[:, COL_G:COL_M1], wt_m1=w_in_pt[:, COL_M1:COL_M2], wt_m2=w_in_pt[:, COL_M2:COL_B],
        wt_b=w_in_pt[:, COL_B:],
        wq=jnp.concatenate([wq_a, wq_b], -1), wkv=jnp.concatenate([pad_half(wkv[..., :QK_NOPE]), pad_half(wkv[..., QK_NOPE:])], -1),
        w_br_mla_p=w_br_mla_p, w_br_sg=w['w_br_sg'].astype(md), w_br_conv=w['w_br_conv'].astype(md),
        w_br_pool=w['w_br_pool'].astype(md), wbd=wbd, w_out=w['w_out'].astype(md),
        w_ff1=w['w_ff1'].astype(md), w_ff1t=jnp.swapaxes(w['w_ff1'].astype(md), 1, 2),
        w_ff2=w['w_ff2'].astype(md), w_ff2t=jnp.swapaxes(w['w_ff2'].astype(md), 1, 2),
        norm_mix_pre=row(w['norm_mix_pre']), gate_b=row(w['gate_b']), q_norm=row(w['q_norm']), kv_norm=row(w['kv_norm']),
        sg_ln_g=row(w['sg_ln_g']), sg_ln_b=row(w['sg_ln_b']), sg_w=w['sg_w'].astype(F32),
        sg_bias=jnp.repeat(jnp.swapaxes(w['sg_b'].astype(F32), 1, 2), BR_WIDTH // SG_GROUPS, axis=2),
        conv_w8=jnp.pad(w['conv_w'].astype(F32), ((0, 0), (0, 5), (0, 0))), pool_scale=row(w['pool_scale']),
        norm_mix_post=row(w['norm_mix_post']), norm_ffn_pre=row(w['norm_ffn_pre']), norm_ffn_post=row(w['norm_ffn_post']),
    )


def _rope_tables(positions):
    inv_freq = ROPE_BASE ** (-jnp.arange(0, QK_ROPE, 2, dtype=F32) / QK_ROPE)
    ang = positions.astype(F32)[:, None] * inv_freq
    cos, sin = jnp.cos(ang), jnp.sin(ang)
    n = positions.shape[0]
    ones, z64, z32 = jnp.ones((n, QK_NOPE), F32), jnp.zeros((n, QK_NOPE), F32), jnp.zeros((n, 32), F32)
    return (jnp.concatenate([ones, cos, cos, z32], 1), jnp.concatenate([z64, sin, sin, z32], 1),
            jnp.concatenate([z64, cos, cos, z32], 1))


def _reference_layout_grads(g):
    gate, dm1, dm2, dpb = g['dw_in_pieces']
    nl = gate.shape[0]
    sg_u, sg_v, cv_b = jnp.split(dm1, 3, axis=-1)
    cv_x, cv_c, pool = jnp.split(dm2, 3, axis=-1)
    c_q, c_kv, kr, krs = jnp.split(dpb, [Q_RANK, Q_RANK + KV_RANK, Q_RANK + KV_RANK + HEAD_PAD], axis=-1)
    rope_cols = slice(QK_NOPE, QK_NOPE + QK_ROPE)
    k_r = kr[..., rope_cols] + _rope_unswap(krs[..., rope_cols])
    w_in = jnp.concatenate([c_q, c_kv, k_r, sg_u, sg_v, cv_x, cv_b, cv_c, pool, gate], -1)
    hw = N_HEADS * HEAD_PAD
    dqa = g['dwq'][..., :hw].reshape(nl, Q_RANK, N_HEADS, HEAD_PAD)
    dqb = g['dwq'][..., hw:].reshape(nl, Q_RANK, N_HEADS, HEAD_PAD)
    w_uq = jnp.concatenate([dqa[..., :QK_NOPE], dqa[..., rope_cols] + _rope_unswap(dqb[..., rope_cols])], -1)
    dka = g['dwkv'][..., :hw].reshape(nl, KV_RANK, N_HEADS, HEAD_PAD)
    dva = g['dwkv'][..., hw:].reshape(nl, KV_RANK, N_HEADS, HEAD_PAD)
    w_ukv = jnp.concatenate([dka[..., :QK_NOPE], dva[..., :V_HEAD]], -1)
    w_br_mla = g['dw_br_mla_p'].reshape(nl, N_HEADS, HEAD_PAD, D_MODEL)[:, :, :V_HEAD]
    dwbd = g['dwbd'].reshape(nl, 4, 64, 4, 64)
    pool_w = jnp.stack([dwbd[:, k, :, k, :] for k in range(4)], axis=1)
    sq = lambda a: a[:, 0, :]
    return dict(
        norm_mix_pre=sq(g['dg_pre']), w_in=w_in, gate_b=sq(g['dgate_b']), q_norm=sq(g['dq_norm']),
        w_uq=w_uq.reshape(nl, Q_RANK, -1), kv_norm=sq(g['dkv_norm']), w_ukv=w_ukv.reshape(nl, KV_RANK, -1),
        w_br_mla=w_br_mla.reshape(nl, N_HEADS * V_HEAD, D_MODEL), sg_ln_g=sq(g['dln_g']), sg_ln_b=sq(g['dln_b']),
        sg_w=g['dsg_w'], sg_b=jnp.swapaxes(g['dsg_b'][:, :, :SG_GROUPS], 1, 2), w_br_sg=g['dw_br_sg'],
        conv_w=g['dconv_w'][:, :3], w_br_conv=g['dw_br_conv'], pool_w=pool_w, pool_scale=sq(g['dpool_scale']),
        w_br_pool=g['dw_br_pool'], w_out=g['dw_out'], norm_mix_post=sq(g['dg_post']), norm_ffn_pre=sq(g['dg_fpre']),
        w_ff1=g['dw_ff1'], w_ff2=g['dw_ff2'], norm_ffn_post=sq(g['dg_fpost']))


def _layer_forward(x0, lw, tabs, carried):
    proj = _mm("in_proj", x0, lw['w_in_p'], tm=1024, tn=896, prologue=_rms, rows=(lw['norm_mix_pre'],))
    q, k, v = _qkv_prep(proj, lw['q_norm'], lw['kv_norm'], lw['wq'], lw['wkv'], *tabs)
    (o, lse), carried_out = _attn_fwd(q, k, v, carried)
    x1 = _mix_fwd(x0, proj, o, lw)
    a = _mm("ffn1", x1, lw['w_ff1'], tm=1024, tn=1024, prologue=_rms, rows=(lw['norm_ffn_pre'],))
    x2, f = _ffn2(a, lw['w_ff2'], x1, lw['norm_ffn_post'])
    return x2, dict(x0=x0, proj=proj, q=q, k=k, v=v, o=o, lse=lse, x1=x1, a=a, f=f), carried_out


def _layer_backward(dx2, lw, sv, tabs, early, late):
    g = {}
    (df, da, g['dg_fpost']), early_out = _ffn2_bwd(dx2, sv['f'], lw['norm_ffn_post'], sv['a'], lw['w_ff2t'], early)
    carried = late(early_out)
    g['dw_ff2'] = _mm_tn("dw_ff2", sv['a'], df, tm=512, tn=512, prologue=_relu_sq)
    dx1, g['dg_fpre'] = _norm_in_bwd("ffn1_bwd", [(da, lw['w_ff1t'])], sv['x1'], lw['norm_ffn_pre'], dx2)
    g['dw_ff1'] = _mm_tn("dw_ff1", sv['x1'], da, tm=512, tn=1024, prologue=_rms, rows=(lw['norm_ffn_pre'],))
    (dgate, dm1, dyv, upool, do, delta, g['dgate_b'], g['dln_g'], g['dln_b'], g['dsg_w'], g['dsg_b'], g['dconv_w'],
     g['dwbd'], g['dpool_scale'], g['dw_br_mla_p'], g['dw_br_sg'], g['dw_br_conv'], g['dw_br_pool'], g['dw_out'],
     g['dg_post']) = _mix_bwd(dx1, sv['proj'], sv['o'], lw)
    dm2 = _shift_bwd(dyv, upool, sv['proj'], lw['conv_w8'])
    (dq, dk, dv), carried_out = _attn_bwd(sv['q'], sv['k'], sv['v'], do, sv['lse'], delta, carried)
    dpb, g['dwq'], g['dwkv'], g['dq_norm'], g['dkv_norm'] = _qkv_bwd(
        dq, dk, dv, sv['proj'], lw['q_norm'], lw['kv_norm'], lw['wq'], lw['wkv'], *tabs)
    pieces = [(dgate, lw['wt_g']), (dm1, lw['wt_m1']), (dm2, lw['wt_m2']), (dpb, lw['wt_b'])]
    dx0, g['dg_pre'] = _norm_in_bwd("in_proj_bwd", pieces, sv['x0'], lw['norm_mix_pre'], dx1)
    g['dw_in_pieces'] = [_mm_tn("dw_in_%d" % n, sv['x0'], d, tm=512, tn=1024, prologue=_rms, rows=(lw['norm_mix_pre'],))
                         for n, (d, _) in enumerate(pieces)]
    return dx0, g, carried_out


class _StepPlan(NamedTuple):
    n_layers: int
    weights_exchange: Callable
    weights_from: Callable
    grads_ready: Callable
    first_done: Callable
    second_done: Callable


def _local_step(x, positions, target, plan):
    tabs = _rope_tables(positions)
    derive = lambda w: {n: a[0] for n, a in _derive_weights(w).items()}
    first = plan.weights_exchange(0)
    weights = plan.weights_from(0, None if first is None else _run_exchange("gather_weights", first))
    derived, saved = [], []
    for l in range(plan.n_layers):
        derived.append(derive(weights))
        coming = plan.weights_exchange(l + 1) if l + 1 < plan.n_layers else None
        x, sv, arrived = _layer_forward(x, derived[l], tabs, coming)
        saved.append(sv)
        if l + 1 < plan.n_layers:
            weights = plan.weights_from(l + 1, arrived)
    loss, dx = _loss_and_grad(x, target)
    pending = None
    for l in reversed(range(plan.n_layers)):
        if pending is None:
            dx, g, _ = _layer_backward(dx, derived[l], saved[l], tabs, None, lambda _: None)
        else:
            dx, g, arrived = _layer_backward(dx, derived[l], saved[l], tabs, pending[1],
                                             functools.partial(plan.first_done, pending[0]))
            plan.second_done(pending[0], arrived)
        lead = lambda a: [b[None] for b in a] if isinstance(a, list) else a[None]
        going = plan.grads_ready(l, _reference_layout_grads({n: lead(a) for n, a in g.items()}))
        pending = None if going is None else (l, going)
    if pending is not None:
        second = plan.first_done(pending[0], _run_exchange("grads_to_sibling", pending[1]))
        plan.second_done(pending[0], _run_exchange("grads_to_chips", second))
    return loss, dx


def _relative_peers():
    x, y = lax.axis_index("x"), lax.axis_index("y")
    return {1: (x, 1 - y), 2: (1 - x, y), 3: (1 - x, 1 - y)}


def _for_my_core(fn):
    def run():
        for half in (0, 1):
            pl.when(lax.axis_index("c") == half)(functools.partial(fn, half))
    return run


def _gather_exchange(split, whole):
    ns, nw = len(split), len(whole)
    n = ns + nw

    def phases(ins, outs, sems):
        ici_send, ici_recv, d2d_send, d2d_recv, own_send, own_recv = sems
        x, y, c = lax.axis_index("x"), lax.axis_index("y"), lax.axis_index("c")
        peers = _relative_peers()

        def rows(ref, which):
            h = ref.shape[-2] // 2
            return ref.at[(slice(None),) * (len(ref.shape) - 2) + (slice(which * h, (which + 1) * h), slice(None))]

        def own(k):
            return pltpu.make_async_remote_copy(src_ref=ins[k], dst_ref=outs[k].at[0], send_sem=own_send.at[k],
                                                recv_sem=own_recv.at[k], device_id=(x, y, 1 - c), device_id_type=MESH)

        def over_ici(k, r, half):
            src = rows(ins[k], half) if k < ns else ins[k]
            dst = rows(outs[k].at[r], half) if k < ns else outs[k].at[r]
            return pltpu.make_async_remote_copy(src_ref=src, dst_ref=dst, send_sem=ici_send.at[3 * k + r - 1],
                                                recv_sem=ici_recv.at[3 * k + r - 1], device_id=(*peers[r], c), device_id_type=MESH)

        def to_sibling(k, r, half):
            landed = rows(outs[k].at[r], half)
            return pltpu.make_async_remote_copy(src_ref=landed, dst_ref=landed, send_sem=d2d_send.at[3 * k + r - 1],
                                                recv_sem=d2d_recv.at[3 * k + r - 1], device_id=(x, y, 1 - c), device_id_type=MESH)

        def start(half):
            for k in range(n):
                own(k).start()
                for r in peers:
                    over_ici(k, r, half).start()

        def middle(half):
            for k in range(n):
                for r in peers:
                    over_ici(k, r, half).wait_recv()
                    if k < ns:
                        to_sibling(k, r, half).start()

        def finish(half):
            for k in range(n):
                own(k).wait()
                for r in peers:
                    if k < ns:
                        to_sibling(k, r, 1 - half).wait_recv()
                        to_sibling(k, r, half).wait_send()
                    over_ici(k, r, half).wait_send()

        return _for_my_core(start), _for_my_core(middle), _for_my_core(finish)

    arrs = list(split) + list(whole)
    return _Exchange(
        operands=arrs, out_shape=[jax.ShapeDtypeStruct((4,) + a.shape, a.dtype) for a in arrs],
        scratch_shapes=[pltpu.SemaphoreType.DMA((3 * n,)), pltpu.SemaphoreType.DMA((3 * n,)), pltpu.SemaphoreType.DMA((3 * ns,)),
                        pltpu.SemaphoreType.DMA((3 * ns,)), pltpu.SemaphoreType.DMA((n,)), pltpu.SemaphoreType.DMA((n,))],
        phases=phases)


def _absolute_chip_order(relative):
    me = 2 * lax.axis_index("x") + lax.axis_index("y")
    return jnp.stack([lax.dynamic_index_in_dim(relative, jnp.bitwise_xor(me, chip), 0, keepdims=False) for chip in range(4)])


REDUCE_STEPS = 8


def _sibling_exchange(arrs):
    n = len(arrs)

    def phases(ins, theirs, sems):
        send_sems, recv_sems = sems
        x, y, c = lax.axis_index("x"), lax.axis_index("y"), lax.axis_index("c")

        def copy(k, my_half):
            h = ins[k].shape[1] // 2
            return pltpu.make_async_remote_copy(src_ref=ins[k].at[:, (1 - my_half) * h:(2 - my_half) * h, :], dst_ref=theirs[k],
                                                send_sem=send_sems.at[k], recv_sem=recv_sems.at[k],
                                                device_id=(x, y, 1 - c), device_id_type=MESH)

        def start(my_half):
            for k in range(n):
                copy(k, my_half).start()

        def finish(my_half):
            for k in range(n):
                copy(k, my_half).wait()

        return _for_my_core(start), lambda: None, _for_my_core(finish)

    return _Exchange(operands=list(arrs),
                     out_shape=[jax.ShapeDtypeStruct((a.shape[0], a.shape[1] // 2, a.shape[2]), a.dtype) for a in arrs],
                     scratch_shapes=[pltpu.SemaphoreType.DMA((n,)), pltpu.SemaphoreType.DMA((n,))], phases=phases)


def _add_sibling(name, arrs, theirs):
    n, steps = len(arrs), REDUCE_STEPS

    def body(*refs):
        for mine_ref, theirs_ref, out_ref in zip(refs[:n], refs[n:2 * n], refs[2 * n:]):
            out_ref[...] = (mine_ref[...] + theirs_ref[...]).astype(out_ref.dtype)

    block = lambda t: (4, t.shape[1] // steps, t.shape[2])
    return pl.pallas_call(
        body, name=name, grid=(steps,),
        in_specs=[pl.BlockSpec(block(t), lambda i: (0, lax.axis_index("c") * steps + i, 0)) for t in theirs]
        + [pl.BlockSpec(block(t), lambda i: (0, i, 0)) for t in theirs],
        out_specs=[pl.BlockSpec(block(t), lambda i: (0, i, 0)) for t in theirs],
        out_shape=[jax.ShapeDtypeStruct(t.shape, WIRE_DTYPE) for t in theirs],
        compiler_params=_params(("parallel",)))(*arrs, *theirs)


def _scatter_exchange(arrs):
    n = len(arrs)

    def phases(ins, outs, sems):
        send_sems, recv_sems = sems
        c = lax.axis_index("c")
        peers = _relative_peers()

        def copy(k, r):
            px, py = peers[r]
            return pltpu.make_async_remote_copy(src_ref=ins[k].at[2 * px + py], dst_ref=outs[k].at[r - 1],
                                                send_sem=send_sems.at[3 * k + r - 1], recv_sem=recv_sems.at[3 * k + r - 1],
                                                device_id=(px, py, c), device_id_type=MESH)

        def start():
            for k in range(n):
                for r in peers:
                    copy(k, r).start()

        def finish():
            for k in range(n):
                for r in peers:
                    copy(k, r).wait()

        return start, lambda: None, finish

    return _Exchange(operands=list(arrs), out_shape=[jax.ShapeDtypeStruct((3,) + a.shape[1:], a.dtype) for a in arrs],
                     scratch_shapes=[pltpu.SemaphoreType.DMA((3 * n,)), pltpu.SemaphoreType.DMA((3 * n,))], phases=phases)


def _sum_chips(name, chip_sums, arrived):
    n, steps = len(chip_sums), REDUCE_STEPS

    def body(*refs):
        for own_ref, arrived_ref, out_ref in zip(refs[:n], refs[n:2 * n], refs[2 * n:]):
            acc = own_ref[...].astype(F32)
            for r in range(3):
                acc = acc + arrived_ref[r].astype(F32)
            out_ref[...] = acc

    rows = lambda s: s.shape[1] // steps
    chip = lambda: 2 * lax.axis_index("x") + lax.axis_index("y")
    return pl.pallas_call(
        body, name=name, grid=(steps,),
        in_specs=[pl.BlockSpec((None, rows(s), s.shape[2]), lambda i: (chip(), i, 0)) for s in chip_sums]
        + [pl.BlockSpec((3, rows(s), s.shape[2]), lambda i: (0, i, 0)) for s in chip_sums],
        out_specs=[pl.BlockSpec((rows(s), s.shape[2]), lambda i: (lax.axis_index("c") * steps + i, 0)) for s in chip_sums],
        out_shape=[jax.ShapeDtypeStruct((2 * s.shape[1], s.shape[2]), F32) for s in chip_sums],
        compiler_params=_params(("parallel",)))(*chip_sums, *arrived)


def _join_siblings(name, bufs):
    n = len(bufs)

    def body(*refs):
        outs = refs[n:2 * n]
        send_sems, recv_sems = refs[2 * n:]
        x, y, c = lax.axis_index("x"), lax.axis_index("y"), lax.axis_index("c")

        def exchange(my_half):
            copies = []
            for k in range(n):
                h = outs[k].shape[0] // 2
                mine = outs[k].at[my_half * h:(my_half + 1) * h, :]
                theirs = outs[k].at[(1 - my_half) * h:(2 - my_half) * h, :]
                cp = pltpu.make_async_remote_copy(src_ref=mine, dst_ref=mine, send_sem=send_sems.at[k],
                                                  recv_sem=recv_sems.at[k], device_id=(x, y, 1 - c), device_id_type=MESH)
                cp.start()
                arrival = pltpu.make_async_remote_copy(src_ref=theirs, dst_ref=theirs, send_sem=send_sems.at[k],
                                                       recv_sem=recv_sems.at[k], device_id=(x, y, 1 - c), device_id_type=MESH)
                copies.append((cp, arrival))
            for cp, arrival in copies:
                arrival.wait_recv()
                cp.wait_send()

        for half in (0, 1):
            pl.when(c == half)(functools.partial(exchange, half))

    return pl.pallas_call(
        body, name=name, in_specs=[HBM] * n, out_specs=[HBM] * n,
        out_shape=[jax.ShapeDtypeStruct(b.shape, b.dtype) for b in bufs], input_output_aliases={k: k for k in range(n)},
        scratch_shapes=[pltpu.SemaphoreType.DMA((n,)), pltpu.SemaphoreType.DMA((n,))],
    )(*bufs)


def _gather_all_exchange(a):
    def phases(ins, outs, scratch):
        (a_ref,), (out_ref,) = ins, outs
        staging, send_sems, recv_sems, local_sem = scratch
        x, y, c = lax.axis_index("x"), lax.axis_index("y"), lax.axis_index("c")
        me = 4 * x + 2 * y + c
        flips = [(fx, fy, fc) for fx in (0, 1) for fy in (0, 1) for fc in (0, 1)][1:]
        peers = [(x ^ fx, y ^ fy, c ^ fc) for fx, fy, fc in flips]

        def copy(j):
            px, py, pc = peers[j]
            return pltpu.make_async_remote_copy(src_ref=a_ref, dst_ref=out_ref.at[me], send_sem=send_sems.at[j],
                                                recv_sem=recv_sems.at[j], device_id=(px, py, pc), device_id_type=MESH)

        def arrival(j):
            px, py, pc = peers[j]
            return pltpu.make_async_remote_copy(src_ref=a_ref, dst_ref=out_ref.at[4 * px + 2 * py + pc], send_sem=send_sems.at[j],
                                                recv_sem=recv_sems.at[j], device_id=(px, py, pc), device_id_type=MESH)

        own = pltpu.make_async_copy(staging, out_ref.at[me], local_sem)

        def start():
            load = pltpu.make_async_copy(a_ref, staging, local_sem)
            load.start()
            load.wait()
            own.start()
            for j in range(7):
                copy(j).start()

        def finish():
            for j in range(7):
                arrival(j).wait_recv()
            for j in range(7):
                copy(j).wait_send()
            own.wait()

        return start, lambda: None, finish

    return _Exchange(operands=[a], out_shape=[jax.ShapeDtypeStruct((8,) + a.shape, a.dtype)],
                     scratch_shapes=[pltpu.VMEM(a.shape, a.dtype), pltpu.SemaphoreType.DMA((7,)), pltpu.SemaphoreType.DMA((7,)),
                                     pltpu.SemaphoreType.DMA], phases=phases)


def _rowwise_call(name, fn, slots, out_shapes, steps, carried=None):
    n_in, n_out = [len(s) for s in slots], [len(o) for o in out_shapes]

    def spec(shape):
        if len(shape) == 3:
            return pl.BlockSpec((shape[0], shape[1] // steps, shape[2]), lambda i: (0, i, 0))
        return pl.BlockSpec((shape[0] // steps, shape[1]), lambda i: (i, 0))

    def body(*refs):
        ins, outs = refs[:sum(n_in)], refs[sum(n_in):]
        a = b = 0
        for k in range(len(slots)):
            for o_ref, val in zip(outs[b:b + n_out[k]], fn(*[r[...] for r in ins[a:a + n_in[k]]])):
                o_ref[...] = val
            a, b = a + n_in[k], b + n_out[k]

    flat_in = [arr for s in slots for arr in s]
    flat_out = [shp for o in out_shapes for shp in o]
    out, carried_out = _call_carrying(
        body, carried, name=name, grid=(steps,), in_specs=[spec(a.shape) for a in flat_in],
        out_specs=[spec(s) for s in flat_out], out_shape=[jax.ShapeDtypeStruct(s, F32) for s in flat_out],
        operands=flat_in, semantics=("parallel",))
    grouped, b = [], 0
    for k in range(len(slots)):
        grouped.append(out[b:b + n_out[k]])
        b += n_out[k]
    return grouped if carried is None else (grouped, carried_out)


def _sum_in_order(a):
    acc = a[0].astype(F32)
    for k in range(1, a.shape[0]):
        acc = acc + a[k].astype(F32)
    return (acc,)


def _adamw_math(w, g, m, v):
    m_new = ADAM_B1 * m + (1.0 - ADAM_B1) * g
    v_new = ADAM_B2 * v + (1.0 - ADAM_B2) * (g * g)
    m_hat = m_new / (1.0 - ADAM_B1 ** ADAM_STEP)
    v_hat = v_new / (1.0 - ADAM_B2 ** ADAM_STEP)
    return -ADAM_LR * (m_hat / (jnp.sqrt(v_hat) + ADAM_EPS) + ADAM_WD * w), m_new, v_new


SMALL_PACK_COLS = 256
SMALL_PACK_ROWS = 2048


def _pack_small(parts):
    wide = [jnp.pad(p, ((0, 0), (0, 0), (0, SMALL_PACK_COLS - p.shape[2]))) for p in parts]
    rows = jnp.concatenate(wide, axis=1)
    return jnp.pad(rows, ((0, 0), (0, SMALL_PACK_ROWS - rows.shape[1]), (0, 0)))


def _unpack_small(packed, shapes):
    out, row = [], 0
    for a, b in shapes:
        out.append(packed[:, row:row + a, :b])
        row += a
    return out


def _pack(arrs, rows_per_layer, dtype):
    nl = arrs[0].shape[0]
    flat = jnp.concatenate([a.astype(dtype).reshape(nl, -1) for a in arrs], axis=1)
    flat = jnp.pad(flat, ((0, 0), (0, rows_per_layer * PACK_COLS - flat.shape[1])))
    return flat.reshape(nl * rows_per_layer, PACK_COLS)


def _unpack(packed, shapes, rows_per_layer):
    nl = shapes[0][0]
    flat = packed.reshape(packed.shape[:-2] + (nl, rows_per_layer * PACK_COLS))
    out, off = [], 0
    for shp in shapes:
        size = math.prod(shp[1:])
        out.append(flat[..., off:off + size].reshape(packed.shape[:-2] + tuple(shp)))
        off += size
    return out


def _rows_needed(shapes, multiple):
    per_layer = sum(math.prod(s[1:]) for s in shapes)
    rows = -(-per_layer // PACK_COLS)
    return -(-rows // multiple) * multiple


CONV_TILE = (8, 128)


def _layer_shard_exchange(w, l):
    conv = w['conv_w'][l].reshape(-1)
    conv = jnp.pad(conv, (0, math.prod(CONV_TILE) - conv.shape[0])).reshape(CONV_TILE)
    return _gather_exchange([w[n][l].astype(MXU_DTYPE) for n in MATMUL_SHARDED], [conv])


def _layer_full_weights(w, l, gathered):
    gathered = [_absolute_chip_order(g) for g in gathered]
    full = {n: w[n][l:l + 1] for n in WEIGHTS}
    for n, part in zip(MATMUL_SHARDED, gathered):
        if n in ROW_SHARDED:
            full[n] = part.reshape(1, 4 * part.shape[1], part.shape[2])
        else:
            full[n] = jnp.swapaxes(part, 0, 1).reshape(1, part.shape[1], 4 * part.shape[2])
    rows, cols = w['conv_w'].shape[1:]
    conv = gathered[-1].reshape(4, -1)[:, :rows * cols].reshape(4, rows, cols)
    full['conv_w'] = jnp.swapaxes(conv, 0, 1).reshape(1, rows, 4 * cols)
    return full


def _chip_major(n, g):
    nl = g.shape[0]
    if n in ROW_SHARDED:
        return jnp.swapaxes(g.reshape(nl, 4, g.shape[1] // 4, g.shape[2]), 0, 1)
    return jnp.transpose(g.reshape(nl, g.shape[1], 4, g.shape[2] // 4), (2, 0, 1, 3))


def kernel(x, positions, norm_mix_pre, w_in, gate_b, q_norm, w_uq, kv_norm, w_ukv, w_br_mla, sg_ln_g, sg_ln_b, sg_w, sg_b, w_br_sg, conv_w, w_br_conv, pool_w, pool_scale, w_br_pool, w_out, norm_mix_post, norm_ffn_pre, w_ff1, w_ff2, norm_ffn_post, loss_target, m_norm_mix_pre, m_w_in, m_gate_b, m_q_norm, m_w_uq, m_kv_norm, m_w_ukv, m_w_br_mla, m_sg_ln_g, m_sg_ln_b, m_sg_w, m_sg_b, m_w_br_sg, m_conv_w, m_w_br_conv, m_pool_w, m_pool_scale, m_w_br_pool, m_w_out, m_norm_mix_post, m_norm_ffn_pre, m_w_ff1, m_w_ff2, m_norm_ffn_post, v_norm_mix_pre, v_w_in, v_gate_b, v_q_norm, v_w_uq, v_kv_norm, v_w_ukv, v_w_br_mla, v_sg_ln_g, v_sg_ln_b, v_sg_w, v_sg_b, v_w_br_sg, v_conv_w, v_w_br_conv, v_pool_w, v_pool_scale, v_w_br_pool, v_w_out, v_norm_mix_post, v_norm_ffn_pre, v_w_ff1, v_w_ff2, v_norm_ffn_post):
    given = dict(locals())
    w = {n: given[n] for n in WEIGHTS}
    mom = {n: given['m_' + n] for n in WEIGHTS}
    var = {n: given['v_' + n] for n in WEIGHTS}
    nl = w['w_in'].shape[0]
    sharded, chip_sums, reduced, local_small = {}, {}, {}, {}

    def grads_ready(l, g):
        sharded[l] = [_chip_major(n, g[n])[:, 0] for n in BIG_SHARDED]
        sharded[l].append(_pack_small([_chip_major(n, g[n])[:, 0] for n in SMALL_SHARDED]))
        local_small[l] = [g[n] for n in REPLICATED + ['conv_w']]
        return _sibling_exchange(sharded[l])

    def sibling_done(l, theirs):
        chip_sums[l] = _add_sibling("add_sibling", sharded[l], theirs)
        return _scatter_exchange(chip_sums[l])

    def chips_done(l, arrived):
        reduced[l] = _join_siblings("join_halves", _sum_chips("sum_chips", chip_sums[l], arrived))

    plan = _StepPlan(n_layers=nl, weights_exchange=functools.partial(_layer_shard_exchange, w),
                     weights_from=functools.partial(_layer_full_weights, w), grads_ready=grads_ready,
                     first_done=sibling_done, second_done=chips_done)
    loss, dx = _local_step(x[0], positions[0], loss_target[0], plan)
    loss = lax.psum(loss, ("x", "y", "c"))

    grad, delta, new_m, new_v = {}, {}, {}, {}
    names = REPLICATED + ['conv_w']
    local = [jnp.concatenate([local_small[l][k] for l in range(nl)]) for k in range(len(names))]
    rows = _rows_needed([a.shape for a in local], 32)
    small_grads = _gather_all_exchange(_pack(local, rows, F32))
    for k, n in enumerate(BIG_SHARDED):
        grad[n] = jnp.stack([reduced[l][k] for l in range(nl)])
    slots = [[w[n], grad[n], mom[n], var[n]] for n in BIG_SHARDED]
    small_pack = lambda d: _pack_small([d[n] for n in SMALL_SHARDED])
    g_small = jnp.stack([reduced[l][-1] for l in range(nl)])
    slots.append([small_pack(w), g_small, small_pack(mom), small_pack(var)])
    updated, (everyone,) = _rowwise_call("adamw_sharded", _adamw_math, slots, [[s_[0].shape] * 3 for s_ in slots], 32,
                                         carried=small_grads)
    for k, n in enumerate(BIG_SHARDED):
        delta[n], new_m[n], new_v[n] = updated[k]
    for d, packed in zip((grad, delta, new_m, new_v), [g_small] + list(updated[-1])):
        d.update(zip(SMALL_SHARDED, _unpack_small(packed, [w[n].shape[1:] for n in SMALL_SHARDED])))

    (summed,), = _rowwise_call("sum_devices", _sum_in_order, [[everyone]], [[everyone.shape[1:]]], 4)
    g_rep = _unpack(summed, [a.shape for a in local], rows)
    chip = 2 * lax.axis_index("x") + lax.axis_index("y")
    g_rep[-1] = lax.dynamic_slice_in_dim(g_rep[-1], chip * w['conv_w'].shape[2], w['conv_w'].shape[2], axis=2)
    rep_pack = lambda arrs: _pack(arrs, rows, F32)
    (rep_out,) = _rowwise_call("adamw_replicated", _adamw_math,
                               [[rep_pack([w[n] for n in names]), rep_pack(g_rep), rep_pack([mom[n] for n in names]),
                                 rep_pack([var[n] for n in names])]], [[(nl * rows, PACK_COLS)] * 3], 4)
    grad.update(zip(names, g_rep))
    for d, packed in zip((delta, new_m, new_v), rep_out):
        d.update(zip(names, _unpack(packed, [w[n].shape for n in names], rows)))

    return (loss, dx[None], *[grad[n] for n in WEIGHTS], *[delta[n] for n in WEIGHTS], *[new_m[n] for n in WEIGHTS],
            *[new_v[n] for n in WEIGHTS])
```

```python
import functools
import math
from typing import Any, Callable, NamedTuple, Sequence

import jax
import jax.numpy as jnp
from jax import lax
from jax.experimental import pallas as pl
from jax.experimental.pallas import tpu as pltpu

F32 = jnp.float32
MXU_DTYPE = jnp.bfloat16
WIRE_DTYPE = jnp.bfloat16
ACT_DTYPE = jnp.bfloat16
MESH = pl.DeviceIdType.MESH

D_MODEL = 1024
D_FF = 4096
N_HEADS = 4
QK_NOPE = 64
QK_ROPE = 32
V_HEAD = 64
HEAD_PAD = 128
Q_RANK = 256
KV_RANK = 128
SG_CHUNK = 128
SG_GROUPS = 4
BR_WIDTH = 256
N_BRANCH = 4
POOL_WINDOWS = (2, 4, 8, 16)
HALO = 16
ROPE_BASE = 10000.0
EPS = 1e-6
ATTN_SCALE = (QK_NOPE + QK_ROPE) ** -0.5
LOG2_E = math.log2(math.e)
FWD_HEADS_PER_STEP = 4
HEADS_PER_STEP = 2
N_PROJ = N_BRANCH * D_MODEL + 6 * BR_WIDTH + Q_RANK + KV_RANK + 2 * HEAD_PAD
COL_G, COL_M1, COL_M2, COL_B = 0, 4096, 4864, 5632

ADAM_LR, ADAM_B1, ADAM_B2, ADAM_EPS, ADAM_WD, ADAM_STEP = 0.001, 0.9, 0.999, 1e-08, 0.01, 10

VMEM_LIMIT = 56 * 1024 * 1024

WEIGHTS = ['norm_mix_pre', 'w_in', 'gate_b', 'q_norm', 'w_uq', 'kv_norm', 'w_ukv', 'w_br_mla', 'sg_ln_g', 'sg_ln_b',
           'sg_w', 'sg_b', 'w_br_sg', 'conv_w', 'w_br_conv', 'pool_w', 'pool_scale', 'w_br_pool', 'w_out',
           'norm_mix_post', 'norm_ffn_pre', 'w_ff1', 'w_ff2', 'norm_ffn_post']
COL_SHARDED = ['w_in', 'w_uq', 'w_ukv', 'w_br_mla', 'w_br_sg', 'w_br_conv', 'w_br_pool', 'w_ff1']
ROW_SHARDED = ['w_out', 'w_ff2']
MATMUL_SHARDED = ['w_in', 'w_uq', 'w_ukv', 'w_br_mla', 'w_br_sg', 'w_br_conv', 'w_br_pool', 'w_out', 'w_ff1', 'w_ff2']
BIG_SHARDED = ['w_in', 'w_ff1', 'w_ff2', 'w_out']
SMALL_SHARDED = ['w_uq', 'w_ukv', 'w_br_mla', 'w_br_sg', 'w_br_conv', 'w_br_pool']
SHARDED = MATMUL_SHARDED + ['conv_w']
REPLICATED = [n for n in WEIGHTS if n not in SHARDED]
PACK_COLS = 1024


def _params(sem, vmem=VMEM_LIMIT):
    return pltpu.CompilerParams(dimension_semantics=sem, vmem_limit_bytes=vmem)


def _mxu(a):
    return a.astype(MXU_DTYPE)


def _dot(a, b):
    return jnp.dot(_mxu(a), _mxu(b), preferred_element_type=F32)


def _dot_nt(a, b):
    return lax.dot_general(_mxu(a), _mxu(b), (((1,), (1,)), ((), ())), preferred_element_type=F32)


def _dot_tn(a, b):
    return lax.dot_general(_mxu(a), _mxu(b), (((0,), (0,)), ((), ())), preferred_element_type=F32)


def _rms(x, g):
    r = lax.rsqrt(jnp.mean(x * x, axis=-1, keepdims=True) + EPS)
    return x * r * g


def _rms_bwd(x, g, dy):
    r = lax.rsqrt(jnp.mean(x * x, axis=-1, keepdims=True) + EPS)
    xh = x * r
    gdy = dy * g
    dx = r * (gdy - xh * jnp.mean(gdy * xh, axis=-1, keepdims=True))
    return dx, jnp.sum(dy * xh, axis=0, keepdims=True)


_GELU_C = math.sqrt(2.0 / math.pi)


def _gelu(x):
    t = jnp.tanh(_GELU_C * (x + 0.044715 * (x * x * x)))
    return x * (0.5 * (1.0 + t)), t


def _gelu_grad(x, t):
    return 0.5 * (1.0 + t) + 0.5 * x * (1.0 - t * t) * (_GELU_C * (1.0 + 3.0 * 0.044715 * x * x))


def _sigmoid(x):
    return 1.0 / (1.0 + jnp.exp(-x))


def _full(shape):
    return pl.BlockSpec(shape, lambda *_: (0,) * len(shape))


def _resident(shape):
    return pl.BlockSpec(shape, lambda *_: (0,) * len(shape), pipeline_mode=pl.Buffered(1))


def _rows(ts, width, col=0):
    return pl.BlockSpec((ts, width), lambda i: (i, col))


def _tile(n, pref):
    return min(n, pref)


def _mm(name, a, w, *, tm, tn, prologue=None, rows=()):
    m, k = a.shape
    n = w.shape[1]
    tm, tn = _tile(m, tm), _tile(n, tn)

    def body(a_ref, *rest):
        row_refs, w_ref, o_ref = rest[:len(rows)], rest[len(rows)], rest[len(rows) + 1]
        av = a_ref[...]
        if prologue is not None:
            av = prologue(av, *[r[...] for r in row_refs])
        o_ref[...] = _dot(av, w_ref[...]).astype(o_ref.dtype)

    return pl.pallas_call(
        body, name=name, grid=(m // tm, n // tn),
        in_specs=[pl.BlockSpec((tm, k), lambda i, j: (i, 0))] + [pl.BlockSpec((1, k), lambda i, j: (0, 0)) for _ in rows]
        + [pl.BlockSpec((k, tn), lambda i, j: (0, j))],
        out_specs=pl.BlockSpec((tm, tn), lambda i, j: (i, j)),
        out_shape=jax.ShapeDtypeStruct((m, n), ACT_DTYPE),
        compiler_params=_params(("parallel", "parallel")),
    )(a, *rows, w)


def _mm_tn(name, a, b, *, tm, tn, prologue=None, rows=()):
    m, k = a.shape
    n = b.shape[1]
    tm, tn = _tile(m, tm), _tile(n, tn)

    def body(a_ref, *rest):
        row_refs, b_ref, o_ref = rest[:len(rows)], rest[len(rows)], rest[len(rows) + 1]

        @pl.when(pl.program_id(1) == 0)
        def _():
            o_ref[...] = jnp.zeros_like(o_ref)

        av = a_ref[...]
        if prologue is not None:
            av = prologue(av, *[r[...] for r in row_refs])
        o_ref[...] += _dot_tn(av, b_ref[...])

    return pl.pallas_call(
        body, name=name, grid=(n // tn, m // tm),
        in_specs=[pl.BlockSpec((tm, k), lambda j, i: (i, 0))] + [pl.BlockSpec((1, k), lambda j, i: (0, 0)) for _ in rows]
        + [pl.BlockSpec((tm, tn), lambda j, i: (i, j))],
        out_specs=pl.BlockSpec((k, tn), lambda j, i: (0, j)),
        out_shape=jax.ShapeDtypeStruct((k, n), F32),
        compiler_params=_params(("parallel", "arbitrary")),
    )(a, *rows, b)


def _relu_sq(a):
    r = jnp.maximum(a.astype(F32), 0.0)
    return r * r


HBM = pl.BlockSpec(memory_space=pl.ANY)


class _Exchange(NamedTuple):
    operands: Sequence[Any]
    out_shape: Sequence[Any]
    scratch_shapes: Sequence[Any]
    phases: Callable


def _run_exchange(name, ex):
    n_in, n_out = len(ex.operands), len(ex.out_shape)

    def body(*refs):
        for phase in ex.phases(refs[:n_in], refs[n_in:n_in + n_out], refs[n_in + n_out:]):
            phase()

    return pl.pallas_call(body, name=name, in_specs=[HBM] * n_in, out_specs=[HBM] * n_out, out_shape=list(ex.out_shape),
                          scratch_shapes=list(ex.scratch_shapes))(*ex.operands)


def _call_carrying(body, carried, *, name, grid, in_specs, out_specs, out_shape, operands, semantics, middle_step=None):
    if carried is None:
        return pl.pallas_call(body, name=name, grid=grid, in_specs=in_specs, out_specs=out_specs, out_shape=out_shape,
                              compiler_params=_params(semantics))(*operands), None
    n_main_in, n_main_out = len(operands), len(out_shape)
    n_in, n_out = len(carried.operands), len(carried.out_shape)
    steps = math.prod(grid)

    def wrapped(*refs):
        main_in, refs = refs[:n_main_in], refs[n_main_in:]
        ex_in, refs = refs[:n_in], refs[n_in:]
        main_out, refs = refs[:n_main_out], refs[n_main_out:]
        ex_out, sems = refs[:n_out], refs[n_out:]
        step = pl.program_id(0)
        for axis in range(1, len(grid)):
            step = step * grid[axis] + pl.program_id(axis)
        start, middle, finish = carried.phases(ex_in, ex_out, sems)
        pl.when(step == 0)(start)
        pl.when(step == ((steps - 1) // 2 if middle_step is None else middle_step))(middle)
        body(*main_in, *main_out)
        pl.when(step == steps - 1)(finish)

    out = pl.pallas_call(
        wrapped, name=name + "_carrying", grid=grid, in_specs=list(in_specs) + [HBM] * n_in,
        out_specs=list(out_specs) + [HBM] * n_out, out_shape=list(out_shape) + list(carried.out_shape),
        scratch_shapes=list(carried.scratch_shapes), compiler_params=_params(("arbitrary",) * len(grid)))(*operands, *carried.operands)
    return out[:n_main_out], out[n_main_out:]


def _qkv_prep(proj, q_norm, kv_norm, wq, wkv, cq_tab, s_tab, cr_tab):
    s = proj.shape[0]
    ts = _tile(s, 512)
    hw = N_HEADS * HEAD_PAD

    def body(cq_ref, ckv_ref, kr_ref, krs_ref, gq_ref, gkv_ref, wq_ref, wkv_ref, ct_ref, st_ref, crt_ref,
             q_ref, k_ref, v_ref):
        ct, st, crt = ct_ref[...], st_ref[...], crt_ref[...]
        qn = _rms(cq_ref[...].astype(F32), gq_ref[...])
        qab = _dot(qn, wq_ref[...])
        kvn = _rms(ckv_ref[...].astype(F32), gkv_ref[...])
        kav = _dot(kvn, wkv_ref[...])
        k_rope = kr_ref[...].astype(F32) * crt + krs_ref[...].astype(F32) * st
        ones_lane = (lax.broadcasted_iota(jnp.int32, (1, HEAD_PAD), 1) == V_HEAD).astype(F32)
        for h in range(N_HEADS):
            lo = h * HEAD_PAD
            q_ref[h] = (qab[:, lo:lo + HEAD_PAD] * ct + qab[:, hw + lo:hw + lo + HEAD_PAD] * st).astype(q_ref.dtype)
            k_ref[h] = (kav[:, lo:lo + HEAD_PAD] + k_rope).astype(k_ref.dtype)
            v_ref[h] = (kav[:, hw + lo:hw + lo + HEAD_PAD] + ones_lane).astype(v_ref.dtype)

    head_spec = pl.BlockSpec((N_HEADS, ts, HEAD_PAD), lambda i: (0, i, 0))
    head_shape = jax.ShapeDtypeStruct((N_HEADS, s, HEAD_PAD), MXU_DTYPE)
    return pl.pallas_call(
        body, name="qkv_prep", grid=(s // ts,),
        in_specs=[_rows(ts, Q_RANK, COL_B // Q_RANK), _rows(ts, KV_RANK, (COL_B + Q_RANK) // KV_RANK),
                  _rows(ts, HEAD_PAD, (COL_B + Q_RANK + KV_RANK) // HEAD_PAD),
                  _rows(ts, HEAD_PAD, (COL_B + Q_RANK + KV_RANK + HEAD_PAD) // HEAD_PAD),
                  _full((1, Q_RANK)), _full((1, KV_RANK)), _full((Q_RANK, 2 * hw)), _full((KV_RANK, 2 * hw)),
                  _rows(ts, HEAD_PAD), _rows(ts, HEAD_PAD), _rows(ts, HEAD_PAD)],
        out_specs=[head_spec, head_spec, head_spec],
        out_shape=[head_shape, head_shape, head_shape],
        compiler_params=_params(("parallel",)),
    )(proj, proj, proj, proj, q_norm, kv_norm, wq, wkv, cq_tab, s_tab, cr_tab)


def _diagonal_mask(t):
    return lax.broadcasted_iota(jnp.int32, (t, t), 1) <= lax.broadcasted_iota(jnp.int32, (t, t), 0)


MIDDLE_WORK_SHARE = 0.65


def _causal_work_step(nq, groups):
    total = groups * nq * (nq + 1) // 2
    done = 0
    for step in range(groups * nq):
        if done >= MIDDLE_WORK_SHARE * total:
            return step
        done += step % nq + 1
    return groups * nq - 1


def _attn_fwd(q, k, v, carried=None):
    s = q.shape[1]
    t = _tile(s, 512)

    def body(q_ref, k_ref, v_ref, o_ref, lse_ref):
        i = pl.program_id(1)
        lane = lax.broadcasted_iota(jnp.int32, (1, HEAD_PAD), 1)

        def step(j, carry, on_diagonal):
            rows = pl.ds(pl.multiple_of(j * t, t), t)
            out = []
            for h in range(FWD_HEADS_PER_STEP):
                m, acc = carry[h]
                sc = _dot_nt(q_ref[h], k_ref[h, rows, :]) * (ATTN_SCALE * LOG2_E)
                if on_diagonal:
                    sc = jnp.where(_diagonal_mask(t), sc, -jnp.inf)
                m_new = jnp.maximum(m, jnp.max(sc, axis=1, keepdims=True))
                out.append((m_new, jnp.exp2(m - m_new) * acc + _dot(jnp.exp2(sc - m_new), v_ref[h, rows, :])))
            return tuple(out)

        init = ((jnp.full((t, 1), -jnp.inf, F32), jnp.zeros((t, HEAD_PAD), F32)),) * FWD_HEADS_PER_STEP
        below = lax.fori_loop(0, i, functools.partial(step, on_diagonal=False), init)
        for h, (m, acc) in enumerate(step(i, below, True)):
            l = jnp.sum(jnp.where(lane == V_HEAD, acc, 0.0), axis=1, keepdims=True)
            o_ref[:, h * HEAD_PAD:(h + 1) * HEAD_PAD] = jnp.where(lane < V_HEAD, acc / l, 0.0)
            lse_ref[h] = m + jnp.log2(l)

    group = FWD_HEADS_PER_STEP
    return _call_carrying(
        body, carried, name="attn_fwd", grid=(N_HEADS // group, s // t),
        in_specs=[pl.BlockSpec((group, t, HEAD_PAD), lambda h, i: (h, i, 0)),
                  pl.BlockSpec((group, s, HEAD_PAD), lambda h, i: (h, 0, 0)),
                  pl.BlockSpec((group, s, HEAD_PAD), lambda h, i: (h, 0, 0))],
        out_specs=[pl.BlockSpec((t, group * HEAD_PAD), lambda h, i: (i, h)), pl.BlockSpec((group, t, 1), lambda h, i: (h, i, 0))],
        out_shape=[jax.ShapeDtypeStruct((s, N_HEADS * HEAD_PAD), F32), jax.ShapeDtypeStruct((N_HEADS, s, 1), F32)],
        operands=(q, k, v), semantics=("parallel", "parallel"), middle_step=_causal_work_step(s // t, N_HEADS // group))


def _attn_bwd(q, k, v, do, lse, delta, carried=None):
    s = q.shape[1]
    t = _tile(s, 512)
    nq = s // t

    def body(q_ref, do_ref, lse_ref, dl_ref, k_ref, v_ref, dq_ref, dk_ref, dv_ref):
        j = pl.program_id(1)

        @pl.when(j == 0)
        def _():
            dq_ref[...] = jnp.zeros_like(dq_ref)

        def step(i, carry, on_diagonal):
            rows = pl.ds(pl.multiple_of(i * t, t), t)
            dq = [dq_ref[h, rows, :] for h in range(HEADS_PER_STEP)]
            out = []
            for h in range(HEADS_PER_STEP):
                dk, dv = carry[h]
                qi, doi = q_ref[h, rows, :], do_ref[rows, h * HEAD_PAD:(h + 1) * HEAD_PAD]
                sc = _dot_nt(qi, k_ref[h]) * (ATTN_SCALE * LOG2_E)
                if on_diagonal:
                    sc = jnp.where(_diagonal_mask(t), sc, -jnp.inf)
                p = jnp.exp2(sc - lse_ref[h, rows, :])
                dv = dv + _dot_tn(p, doi)
                ds = p * (_dot_nt(doi, v_ref[h]) - dl_ref[h, rows, :])
                dk = dk + _dot_tn(ds, qi)
                dq[h] = dq[h] + _dot(ds, k_ref[h]) * ATTN_SCALE
                out.append((dk, dv))
            for h in range(HEADS_PER_STEP):
                dq_ref[h, rows, :] = dq[h]
            return tuple(out)

        zero = ((jnp.zeros((t, HEAD_PAD), F32),) * 2,) * HEADS_PER_STEP
        sums = lax.fori_loop(j + 1, nq, functools.partial(step, on_diagonal=False), step(j, zero, True))
        for h, (dk, dv) in enumerate(sums):
            dk_ref[h] = dk * ATTN_SCALE
            dv_ref[h] = dv

    group = HEADS_PER_STEP
    whole = lambda w: pl.BlockSpec((group, s, w), lambda h, j: (h, 0, 0), pipeline_mode=pl.Buffered(1))
    tile = pl.BlockSpec((group, t, HEAD_PAD), lambda h, j: (h, j, 0))
    head_shape = jax.ShapeDtypeStruct((N_HEADS, s, HEAD_PAD), F32)
    return _call_carrying(
        body, carried, name="attn_bwd", grid=(N_HEADS // group, nq),
        in_specs=[whole(HEAD_PAD), pl.BlockSpec((s, group * HEAD_PAD), lambda h, j: (0, h), pipeline_mode=pl.Buffered(1)),
                  whole(1), whole(1), tile, tile],
        out_specs=[whole(HEAD_PAD), tile, tile],
        out_shape=[head_shape, head_shape, head_shape],
        operands=(q, do, lse, delta, k, v), semantics=("parallel", "arbitrary"))


def _qkv_bwd(dq, dk, dv, proj, q_norm, kv_norm, wq, wkv, cq_tab, s_tab, cr_tab):
    s = proj.shape[0]
    ts = _tile(s, 512)
    hw = N_HEADS * HEAD_PAD

    def body(dq_ref, dk_ref, dv_ref, cq_ref, ckv_ref, gq_ref, gkv_ref, wq_ref, wkv_ref, ct_ref, st_ref, crt_ref,
             dpb_ref, dwq_ref, dwkv_ref, dgq_ref, dgkv_ref):
        @pl.when(pl.program_id(0) == 0)
        def _():
            for r in (dwq_ref, dwkv_ref, dgq_ref, dgkv_ref):
                r[...] = jnp.zeros_like(r)

        ct, st, crt = ct_ref[...], st_ref[...], crt_ref[...]
        dqs = [dq_ref[h] for h in range(N_HEADS)]
        dks = [dk_ref[h] for h in range(N_HEADS)]
        dqab = jnp.concatenate([d * ct for d in dqs] + [d * st for d in dqs], axis=1)
        dkav = jnp.concatenate(dks + [dv_ref[h] for h in range(N_HEADS)], axis=1)
        dk_sum = dks[0] + dks[1] + dks[2] + dks[3]
        cq, ckv, gq, gkv = cq_ref[...].astype(F32), ckv_ref[...].astype(F32), gq_ref[...], gkv_ref[...]
        dwq_ref[...] += _dot_tn(_rms(cq, gq), dqab)
        dwkv_ref[...] += _dot_tn(_rms(ckv, gkv), dkav)
        dcq, dgq = _rms_bwd(cq, gq, _dot_nt(dqab, wq_ref[...]))
        dckv, dgkv = _rms_bwd(ckv, gkv, _dot_nt(dkav, wkv_ref[...]))
        dgq_ref[...] += dgq
        dgkv_ref[...] += dgkv
        dpb_ref[...] = jnp.concatenate([dcq, dckv, dk_sum * crt, dk_sum * st], axis=1).astype(dpb_ref.dtype)

    head_spec = pl.BlockSpec((N_HEADS, ts, HEAD_PAD), lambda i: (0, i, 0))
    wb = Q_RANK + KV_RANK + 2 * HEAD_PAD
    return pl.pallas_call(
        body, name="qkv_bwd", grid=(s // ts,),
        in_specs=[head_spec, head_spec, head_spec,
                  _rows(ts, Q_RANK, COL_B // Q_RANK), _rows(ts, KV_RANK, (COL_B + Q_RANK) // KV_RANK),
                  _full((1, Q_RANK)), _full((1, KV_RANK)), _full((Q_RANK, 2 * hw)), _full((KV_RANK, 2 * hw)),
                  _rows(ts, HEAD_PAD), _rows(ts, HEAD_PAD), _rows(ts, HEAD_PAD)],
        out_specs=[_rows(ts, wb), _full((Q_RANK, 2 * hw)), _full((KV_RANK, 2 * hw)), _full((1, Q_RANK)), _full((1, KV_RANK))],
        out_shape=[jax.ShapeDtypeStruct((s, wb), MXU_DTYPE), jax.ShapeDtypeStruct((Q_RANK, 2 * hw), F32),
                   jax.ShapeDtypeStruct((KV_RANK, 2 * hw), F32), jax.ShapeDtypeStruct((1, Q_RANK), F32),
                   jax.ShapeDtypeStruct((1, KV_RANK), F32)],
        compiler_params=_params(("arbitrary",)),
    )(dq, dk, dv, proj, proj, q_norm, kv_norm, wq, wkv, cq_tab, s_tab, cr_tab)


def _lane_group(width):
    return lax.broadcasted_iota(jnp.int32, (1, width), 1) // (width // 4)


def _shift_down(a, k):
    return pltpu.roll(a, k, 0)


def _shift_up(a, k):
    return pltpu.roll(a, a.shape[0] - k, 0)


def _window_sums(xh, shift):
    s2 = xh + shift(xh, 1)
    s4 = s2 + shift(s2, 2)
    s8 = s4 + shift(s4, 4)
    s16 = s8 + shift(s8, 8)
    grp = _lane_group(xh.shape[1])
    return jnp.where(grp == 0, s2, jnp.where(grp == 1, s4, jnp.where(grp == 2, s8, s16)))


def _pool_count(i, ts):
    grp = _lane_group(BR_WIDTH)
    win = jnp.where(grp == 0, 2.0, jnp.where(grp == 1, 4.0, jnp.where(grp == 2, 8.0, 16.0)))
    t = (i * ts + lax.broadcasted_iota(jnp.int32, (ts, 1), 0)).astype(F32)
    return jnp.minimum(t + 1.0, win)


def _mix_forward(i, ts, r):
    f = {}
    act = lambda name: r[name][...].astype(F32)
    f['gates'] = _sigmoid(act('gate') + r['gate_b'][...])
    sgu, sgv = act('sgu'), act('sgv')
    f['sgu'], f['sgv'] = sgu, sgv
    u_act, f['tu'] = _gelu(sgu)
    vg, f['tv'] = _gelu(sgv)
    mu = jnp.mean(vg, axis=-1, keepdims=True)
    xc = vg - mu
    f['ln_r'] = lax.rsqrt(jnp.mean(xc * xc, axis=-1, keepdims=True) + EPS)
    f['ln_xh'] = xc * f['ln_r']
    vln = f['ln_xh'] * r['ln_g'][...] + r['ln_b'][...]
    tril = lax.broadcasted_iota(jnp.int32, (SG_CHUNK, SG_CHUNK), 1) <= lax.broadcasted_iota(jnp.int32, (SG_CHUNK, SG_CHUNK), 0)
    f['wm'] = [_mxu(jnp.where(tril, r['sg_w'][g], 0.0)) for g in range(SG_GROUPS)]
    f['tril'] = tril
    grp = _lane_group(BR_WIDTH)
    bias = r['sg_bias'][...]
    parts = []
    for ci in range(ts // SG_CHUNK):
        vc = vln[ci * SG_CHUNK:(ci + 1) * SG_CHUNK]
        sc = bias
        for g in range(SG_GROUPS):
            sc = sc + jnp.where(grp == g, _dot(f['wm'][g], vc), 0.0)
        parts.append(sc)
    f['vln'] = vln
    f['sg_s'] = parts[0] if len(parts) == 1 else jnp.concatenate(parts, axis=0)
    f['u_act'] = u_act
    out_b = u_act * f['sg_s']
    first = (i > 0).astype(F32)
    cvx, cvc, cvb = act('cvx'), act('cvc'), act('cvb')
    f['cvx'], f['cvc'], f['cvb'] = cvx, cvc, cvb
    zh = jnp.concatenate([act('hx') * act('hc') * first, cvc * cvx], axis=0)
    f['z1'] = _shift_down(zh, 1)[HALO:]
    f['z2'] = _shift_down(zh, 2)[HALO:]
    f['z0'] = zh[HALO:]
    f['yv'] = r['conv_w'][0:1, :] * f['z2'] + r['conv_w'][1:2, :] * f['z1'] + r['conv_w'][2:3, :] * f['z0']
    out_c = cvb * f['yv']
    p = act('pool')
    ph = jnp.concatenate([act('hp') * first, p], axis=0)
    f['cnt'] = _pool_count(i, ts)
    f['pooled'] = _window_sums(ph, _shift_down)[HALO:] / f['cnt'] - p
    f['mixed'] = _dot(f['pooled'], r['wbd'][...])
    out_d = f['mixed'] * r['pool_scale'][...]
    f['outs'] = [r['o'][...], out_b, out_c, out_d]
    f['ys'] = [_dot(f['outs'][b], r['w_br'][b][...]) for b in range(N_BRANCH)]
    merged = f['gates'][:, 0:D_MODEL] * f['ys'][0]
    for b in range(1, N_BRANCH):
        merged = merged + f['gates'][:, b * D_MODEL:(b + 1) * D_MODEL] * f['ys'][b]
    f['merged'] = merged
    f['mo'] = _dot(merged, r['w_out'][...])
    return f


_MIX_TILE_INPUTS = ['gate', 'sgu', 'sgv', 'cvb', 'cvx', 'cvc', 'pool', 'hx', 'hc', 'hp', 'o']
_MIX_WEIGHTS = ['gate_b', 'ln_g', 'ln_b', 'sg_w', 'sg_bias', 'conv_w', 'wbd', 'pool_scale', 'w_br0', 'w_br1', 'w_br2',
                'w_br3', 'w_out', 'g_post']


def _mix_specs(s, ts):
    c0 = COL_M1 // BR_WIDTH
    prev = lambda col: pl.BlockSpec((HALO, BR_WIDTH), lambda i: (jnp.maximum(i * (ts // HALO) - 1, 0), col))
    tiles = [_rows(ts, N_BRANCH * D_MODEL, 0), _rows(ts, BR_WIDTH, c0), _rows(ts, BR_WIDTH, c0 + 1), _rows(ts, BR_WIDTH, c0 + 2),
             _rows(ts, BR_WIDTH, c0 + 3), _rows(ts, BR_WIDTH, c0 + 4), _rows(ts, BR_WIDTH, c0 + 5),
             prev(c0 + 3), prev(c0 + 4), prev(c0 + 5), _rows(ts, N_HEADS * HEAD_PAD)]
    weights = [_full((1, N_BRANCH * D_MODEL)), _full((1, BR_WIDTH)), _full((1, BR_WIDTH)),
               _full((SG_GROUPS, SG_CHUNK, SG_CHUNK)), _full((SG_CHUNK, BR_WIDTH)), _full((8, BR_WIDTH)),
               _resident((BR_WIDTH, BR_WIDTH)), _full((1, BR_WIDTH)), _resident((N_HEADS * HEAD_PAD, D_MODEL)),
               _resident((BR_WIDTH, D_MODEL)), _resident((BR_WIDTH, D_MODEL)), _resident((BR_WIDTH, D_MODEL)),
               _resident((D_MODEL, D_MODEL)), _full((1, D_MODEL))]
    return tiles, weights


def _mix_refs(refs):
    names = _MIX_TILE_INPUTS + _MIX_WEIGHTS
    r = dict(zip(names, refs[:len(names)]))
    r['w_br'] = [r['w_br0'], r['w_br1'], r['w_br2'], r['w_br3']]
    return r, refs[len(names):]


def _mix_operands(proj, o, lw):
    return ([proj] * 10 + [o] + [lw[n] for n in ['gate_b', 'sg_ln_g', 'sg_ln_b', 'sg_w', 'sg_bias', 'conv_w8', 'wbd',
                                                 'pool_scale', 'w_br_mla_p', 'w_br_sg', 'w_br_conv', 'w_br_pool', 'w_out',
                                                 'norm_mix_post']])


def _mix_fwd(x0, proj, o, lw):
    s = x0.shape[0]
    ts = _tile(s, 512)
    tiles, weights = _mix_specs(s, ts)

    def body(*refs):
        r, (x0_ref, x1_ref) = _mix_refs(refs)
        f = _mix_forward(pl.program_id(0), ts, r)
        x1_ref[...] = x0_ref[...] + _rms(f['mo'], r['g_post'][...])

    return pl.pallas_call(
        body, name="mix_fwd", grid=(s // ts,),
        in_specs=tiles + weights + [_rows(ts, D_MODEL)],
        out_specs=_rows(ts, D_MODEL),
        out_shape=jax.ShapeDtypeStruct((s, D_MODEL), F32),
        compiler_params=_params(("parallel",)),
    )(*_mix_operands(proj, o, lw), x0)


def _mix_bwd(dx1, proj, o, lw):
    s = dx1.shape[0]
    ts = _tile(s, 256)
    tiles, weights = _mix_specs(s, ts)
    hw = N_HEADS * HEAD_PAD

    def body(*refs):
        r, rest = _mix_refs(refs)
        (dx1_ref, dg_ref, dm1_ref, dyv_ref, up_ref, do_ref, delta_ref,
         dgate_b_ref, dln_g_ref, dln_b_ref, dsgw_ref, dsgb_ref, dconv_ref, dwbd_ref, dps_ref,
         dwbr0_ref, dwbr1_ref, dwbr2_ref, dwbr3_ref, dwout_ref, dgpost_ref, dbias_acc) = rest
        i = pl.program_id(0)
        acc_refs = [dgate_b_ref, dln_g_ref, dln_b_ref, dsgw_ref, dsgb_ref, dconv_ref, dwbd_ref, dps_ref,
                    dwbr0_ref, dwbr1_ref, dwbr2_ref, dwbr3_ref, dwout_ref, dgpost_ref, dbias_acc]

        @pl.when(i == 0)
        def _():
            for a in acc_refs:
                a[...] = jnp.zeros_like(a)

        f = _mix_forward(i, ts, r)
        dmo, dgpost = _rms_bwd(f['mo'], r['g_post'][...], dx1_ref[...])
        dgpost_ref[...] += dgpost
        dwout_ref[...] += _dot_tn(f['merged'], dmo)
        dmerged = _dot_nt(dmo, r['w_out'][...])
        dwbr = [dwbr0_ref, dwbr1_ref, dwbr2_ref, dwbr3_ref]
        douts = []
        for b in range(N_BRANCH):
            gb = f['gates'][:, b * D_MODEL:(b + 1) * D_MODEL]
            dgate = dmerged * f['ys'][b] * gb * (1.0 - gb)
            dg_ref[:, b * D_MODEL:(b + 1) * D_MODEL] = dgate.astype(dg_ref.dtype)
            dgate_b_ref[:, b * D_MODEL:(b + 1) * D_MODEL] += jnp.sum(dgate, axis=0, keepdims=True)
            dy = dmerged * gb
            dwbr[b][...] += _dot_tn(f['outs'][b], dy)
            douts.append(_dot_nt(dy, r['w_br'][b][...]))
        do = douts[0]
        do_ref[...] = do.astype(do_ref.dtype)
        prod = do * f['outs'][0]
        for h in range(N_HEADS):
            delta_ref[h] = jnp.sum(prod[:, h * HEAD_PAD:(h + 1) * HEAD_PAD], axis=1, keepdims=True)
        grp = _lane_group(BR_WIDTH)
        ds = douts[1] * f['u_act']
        dsgu = douts[1] * f['sg_s'] * _gelu_grad(f['sgu'], f['tu'])
        dvln_parts = []
        for ci in range(ts // SG_CHUNK):
            rows = slice(ci * SG_CHUNK, (ci + 1) * SG_CHUNK)
            ds_c, vln_c = ds[rows], f['vln'][rows]
            dvln_c = jnp.zeros((SG_CHUNK, BR_WIDTH), F32)
            for g in range(SG_GROUPS):
                dvln_c = dvln_c + jnp.where(grp == g, _dot_tn(f['wm'][g], ds_c), 0.0)
                dsgw_ref[g] += jnp.where(f['tril'], _dot_nt(jnp.where(grp == g, ds_c, 0.0), vln_c), 0.0)
            dbias_acc[...] += ds_c
            dvln_parts.append(dvln_c)
        dvln = dvln_parts[0] if len(dvln_parts) == 1 else jnp.concatenate(dvln_parts, axis=0)
        dln_g_ref[...] += jnp.sum(dvln * f['ln_xh'], axis=0, keepdims=True)
        dln_b_ref[...] += jnp.sum(dvln, axis=0, keepdims=True)
        dxh = dvln * r['ln_g'][...]
        dvg = f['ln_r'] * (dxh - jnp.mean(dxh, axis=-1, keepdims=True)
                           - f['ln_xh'] * jnp.mean(dxh * f['ln_xh'], axis=-1, keepdims=True))
        dsgv = dvg * _gelu_grad(f['sgv'], f['tv'])
        dcvb = douts[2] * f['yv']
        dyv = douts[2] * f['cvb']
        dyv_ref[...] = dyv
        for kk, zk in enumerate((f['z2'], f['z1'], f['z0'])):
            dconv_ref[kk:kk + 1, :] += jnp.sum(dyv * zk, axis=0, keepdims=True)
        dps_ref[...] += jnp.sum(douts[3] * f['mixed'], axis=0, keepdims=True)
        dmixed = douts[3] * r['pool_scale'][...]
        dwbd_ref[...] += _dot_tn(f['pooled'], dmixed)
        up_ref[...] = _dot_nt(dmixed, r['wbd'][...]) / f['cnt']
        dm1_ref[...] = jnp.concatenate([dsgu, dsgv, dcvb], axis=1).astype(dm1_ref.dtype)

        @pl.when(i == pl.num_programs(0) - 1)
        def _():
            lane = lax.broadcasted_iota(jnp.int32, (1, SG_CHUNK), 1)
            db = dbias_acc[...]
            out = jnp.zeros((SG_CHUNK, SG_CHUNK), F32)
            for g in range(SG_GROUPS):
                out = out + jnp.where(lane == g, jnp.sum(jnp.where(grp == g, db, 0.0), axis=1, keepdims=True), 0.0)
            dsgb_ref[...] = out

    acc = lambda shape: (_full(shape), jax.ShapeDtypeStruct(shape, F32))
    accs = [acc((1, N_BRANCH * D_MODEL)), acc((1, BR_WIDTH)), acc((1, BR_WIDTH)), acc((SG_GROUPS, SG_CHUNK, SG_CHUNK)),
            acc((SG_CHUNK, SG_CHUNK)), acc((8, BR_WIDTH)), acc((BR_WIDTH, BR_WIDTH)), acc((1, BR_WIDTH)),
            acc((hw, D_MODEL)), acc((BR_WIDTH, D_MODEL)), acc((BR_WIDTH, D_MODEL)), acc((BR_WIDTH, D_MODEL)),
            acc((D_MODEL, D_MODEL)), acc((1, D_MODEL))]
    tile_outs = [(_rows(ts, N_BRANCH * D_MODEL), jax.ShapeDtypeStruct((s, N_BRANCH * D_MODEL), MXU_DTYPE)),
                 (_rows(ts, 3 * BR_WIDTH), jax.ShapeDtypeStruct((s, 3 * BR_WIDTH), MXU_DTYPE)),
                 (_rows(ts, BR_WIDTH), jax.ShapeDtypeStruct((s, BR_WIDTH), F32)),
                 (_rows(ts, BR_WIDTH), jax.ShapeDtypeStruct((s, BR_WIDTH), F32)),
                 (_rows(ts, hw), jax.ShapeDtypeStruct((s, hw), MXU_DTYPE)),
                 (pl.BlockSpec((N_HEADS, ts, 1), lambda i: (0, i, 0)), jax.ShapeDtypeStruct((N_HEADS, s, 1), F32))]
    outs = tile_outs + accs
    return pl.pallas_call(
        body, name="mix_bwd", grid=(s // ts,),
        in_specs=tiles + weights + [_rows(ts, D_MODEL)],
        out_specs=[o_[0] for o_ in outs], out_shape=[o_[1] for o_ in outs],
        scratch_shapes=[pltpu.VMEM((SG_CHUNK, BR_WIDTH), F32)],
        compiler_params=_params(("arbitrary",), 60 * 1024 * 1024),
    )(*_mix_operands(proj, o, lw), dx1)


def _shift_bwd(dyv, upool, proj, conv_w8):
    s = dyv.shape[0]
    ts = _tile(s, 512)
    nb = s // HALO
    c0 = COL_M1 // BR_WIDTH

    def body(dyv_ref, dyvn_ref, up_ref, upn_ref, cvx_ref, cvc_ref, cw_ref, out_ref):
        i = pl.program_id(0)
        last = (i < pl.num_programs(0) - 1).astype(F32)
        dh = jnp.concatenate([dyv_ref[...], dyvn_ref[...] * last], axis=0)
        dz = (cw_ref[2:3, :] * dh + cw_ref[1:2, :] * _shift_up(dh, 1) + cw_ref[0:1, :] * _shift_up(dh, 2))[:ts]
        up = up_ref[...]
        uh = jnp.concatenate([up, upn_ref[...] * last], axis=0)
        dpool = _window_sums(uh, _shift_up)[:ts] - up * _pool_count(i, ts)
        out_ref[...] = jnp.concatenate([dz * cvc_ref[...].astype(F32), dz * cvx_ref[...].astype(F32), dpool],
                                       axis=1).astype(out_ref.dtype)

    nxt = pl.BlockSpec((HALO, BR_WIDTH), lambda i: (jnp.minimum((i + 1) * (ts // HALO), nb - 1), 0))
    return pl.pallas_call(
        body, name="shift_bwd", grid=(s // ts,),
        in_specs=[_rows(ts, BR_WIDTH), nxt, _rows(ts, BR_WIDTH), nxt, _rows(ts, BR_WIDTH, c0 + 3), _rows(ts, BR_WIDTH, c0 + 4),
                  _full((8, BR_WIDTH))],
        out_specs=_rows(ts, 3 * BR_WIDTH),
        out_shape=jax.ShapeDtypeStruct((s, 3 * BR_WIDTH), MXU_DTYPE),
        compiler_params=_params(("parallel",)),
    )(dyv, dyv, upool, upool, proj, proj, conv_w8)


def _ffn2(a, w2, x1, g):
    s = a.shape[0]
    ts = _tile(s, 512)

    def body(a_ref, w_ref, x1_ref, g_ref, x2_ref, f_ref):
        f = _dot(_relu_sq(a_ref[...]), w_ref[...])
        f_ref[...] = f
        x2_ref[...] = x1_ref[...] + _rms(f, g_ref[...])

    return pl.pallas_call(
        body, name="ffn2", grid=(s // ts,),
        in_specs=[_rows(ts, D_FF), _resident((D_FF, D_MODEL)), _rows(ts, D_MODEL), _full((1, D_MODEL))],
        out_specs=[_rows(ts, D_MODEL), _rows(ts, D_MODEL)],
        out_shape=[jax.ShapeDtypeStruct((s, D_MODEL), F32)] * 2,
        compiler_params=_params(("parallel",)),
    )(a, w2, x1, g)


def _ffn2_bwd(dx2, f, g, a, w2t, carried=None):
    s = a.shape[0]
    ts = _tile(s, 512)

    def body(dx2_ref, f_ref, g_ref, a_ref, w_ref, df_ref, da_ref, dg_ref):
        @pl.when(pl.program_id(0) == 0)
        def _():
            dg_ref[...] = jnp.zeros_like(dg_ref)

        df, dg = _rms_bwd(f_ref[...], g_ref[...], dx2_ref[...])
        dg_ref[...] += dg
        df_ref[...] = df.astype(df_ref.dtype)
        da_ref[...] = (_dot(df, w_ref[...]) * (2.0 * jnp.maximum(a_ref[...].astype(F32), 0.0))).astype(da_ref.dtype)

    return _call_carrying(
        body, carried, name="ffn2_bwd", grid=(s // ts,),
        in_specs=[_rows(ts, D_MODEL), _rows(ts, D_MODEL), _full((1, D_MODEL)), _rows(ts, D_FF), _resident((D_MODEL, D_FF))],
        out_specs=[_rows(ts, D_MODEL), _rows(ts, D_FF), _full((1, D_MODEL))],
        out_shape=[jax.ShapeDtypeStruct((s, D_MODEL), MXU_DTYPE), jax.ShapeDtypeStruct((s, D_FF), MXU_DTYPE),
                   jax.ShapeDtypeStruct((1, D_MODEL), F32)],
        operands=(dx2, f, g, a, w2t), semantics=("arbitrary",))


def _norm_in_bwd(name, pieces, x, g, dres):
    s = x.shape[0]
    ts = _tile(s, 512)
    n = len(pieces)

    def body(*refs):
        d_refs, w_refs = refs[:n], refs[n:2 * n]
        x_ref, g_ref, dres_ref, dx_ref, dg_ref = refs[2 * n:]

        @pl.when(pl.program_id(0) == 0)
        def _():
            dg_ref[...] = jnp.zeros_like(dg_ref)

        dh = _dot(d_refs[0][...], w_refs[0][...])
        for p in range(1, n):
            dh = dh + _dot(d_refs[p][...], w_refs[p][...])
        dx, dg = _rms_bwd(x_ref[...], g_ref[...], dh)
        dg_ref[...] += dg
        dx_ref[...] = dres_ref[...] + dx

    return pl.pallas_call(
        body, name=name, grid=(s // ts,),
        in_specs=[_rows(ts, d.shape[1]) for d, _ in pieces] + [_resident(w.shape) for _, w in pieces]
        + [_rows(ts, D_MODEL), _full((1, D_MODEL)), _rows(ts, D_MODEL)],
        out_specs=[_rows(ts, D_MODEL), _full((1, D_MODEL))],
        out_shape=[jax.ShapeDtypeStruct((s, D_MODEL), F32), jax.ShapeDtypeStruct((1, D_MODEL), F32)],
        compiler_params=_params(("arbitrary",)),
    )(*[d for d, _ in pieces], *[w for _, w in pieces], x, g, dres)


def _loss_and_grad(y, target):
    s = y.shape[0]
    ts = _tile(s, 512)

    def body(y_ref, t_ref, dy_ref, loss_ref):
        @pl.when(pl.program_id(0) == 0)
        def _():
            loss_ref[...] = jnp.zeros_like(loss_ref)

        err = y_ref[...] - t_ref[...]
        dy_ref[...] = err * (1.0 / D_MODEL)
        loss_ref[...] += 0.5 * jnp.sum(jnp.mean(err * err, axis=-1, keepdims=True), axis=0, keepdims=True)

    dy, loss = pl.pallas_call(
        body, name="loss", grid=(s // ts,),
        in_specs=[_rows(ts, D_MODEL), _rows(ts, D_MODEL)],
        out_specs=[_rows(ts, D_MODEL), _full((8, 128))],
        out_shape=[jax.ShapeDtypeStruct((s, D_MODEL), F32), jax.ShapeDtypeStruct((8, 128), F32)],
        compiler_params=_params(("arbitrary",)),
    )(y, target)
    return loss[0, 0], dy


_W_IN_SPLITS = [256, 384, 416, 672, 928, 1184, 1440, 1696, 1952]


def _rope_swap(w):
    half = QK_ROPE // 2
    return jnp.concatenate([-w[..., half:], w[..., :half]], axis=-1)


def _rope_unswap(d):
    half = QK_ROPE // 2
    return jnp.concatenate([d[..., half:], -d[..., :half]], axis=-1)


def _zeros_like_cols(w, n):
    return jnp.zeros(w.shape[:-1] + (n,), w.dtype)


def _derive_weights(w):
    md = MXU_DTYPE
    nl = w['w_in'].shape[0]
    c_q, c_kv, k_r, sg_u, sg_v, cv_x, cv_b, cv_c, pool, gate = jnp.split(w['w_in'].astype(md), _W_IN_SPLITS, axis=-1)
    pad_rope = lambda r: jnp.concatenate([_zeros_like_cols(r, QK_NOPE), r, _zeros_like_cols(r, HEAD_PAD - QK_NOPE - QK_ROPE)], -1)
    w_in_p = jnp.concatenate([gate, sg_u, sg_v, cv_b, cv_x, cv_c, pool, c_q, c_kv, pad_rope(k_r), pad_rope(_rope_swap(k_r))], -1)
    wq = w['w_uq'].astype(md).reshape(nl, Q_RANK, N_HEADS, QK_NOPE + QK_ROPE)
    nope, rope_w = wq[..., :QK_NOPE], wq[..., QK_NOPE:]
    wq_a = jnp.concatenate([nope, rope_w, _zeros_like_cols(nope, 32)], -1).reshape(nl, Q_RANK, N_HEADS * HEAD_PAD)
    wq_b = pad_rope(_rope_swap(rope_w)).reshape(nl, Q_RANK, N_HEADS * HEAD_PAD)
    wkv = w['w_ukv'].astype(md).reshape(nl, KV_RANK, N_HEADS, QK_NOPE + V_HEAD)
    pad_half = lambda r: jnp.concatenate([r, _zeros_like_cols(r, HEAD_PAD - r.shape[-1])], -1).reshape(nl, KV_RANK, N_HEADS * HEAD_PAD)
    w_br_mla = w['w_br_mla'].astype(md).reshape(nl, N_HEADS, V_HEAD, D_MODEL)
    w_br_mla_p = jnp.concatenate([w_br_mla, jnp.zeros_like(w_br_mla)], axis=2).reshape(nl, N_HEADS * HEAD_PAD, D_MODEL)
    eye = jnp.eye(4, dtype=md)
    wbd = (w['pool_w'].astype(md)[:, :, :, None, :] * eye[None, :, None, :, None]).reshape(nl, BR_WIDTH, BR_WIDTH)
    row = lambda a: a.astype(F32)[:, None, :]
    w_in_pt = jnp.swapaxes(w_in_p, 1, 2)
    return dict(
        w_in_p=w_in_p, wt_g=w_in_pt[:, COL_G:COL_M1], wt_m1=w_in_pt[:, COL_M1:COL_M2], wt_m2=w_in_pt[:, COL_M2:COL_B],
        wt_b=w_in_pt[:, COL_B:],
        wq=jnp.concatenate([wq_a, wq_b], -1), wkv=jnp.concatenate([pad_half(wkv[..., :QK_NOPE]), pad_half(wkv[..., QK_NOPE:])], -1),
        w_br_mla_p=w_br_mla_p, w_br_sg=w['w_br_sg'].astype(md), w_br_conv=w['w_br_conv'].astype(md),
        w_br_pool=w['w_br_pool'].astype(md), wbd=wbd, w_out=w['w_out'].astype(md),
        w_ff1=w['w_ff1'].astype(md), w_ff1t=jnp.swapaxes(w['w_ff1'].astype(md), 1, 2),
        w_ff2=w['w_ff2'].astype(md), w_ff2t=jnp.swapaxes(w['w_ff2'].astype(md), 1, 2),
        norm_mix_pre=row(w['norm_mix_pre']), gate_b=row(w['gate_b']), q_norm=row(w['q_norm']), kv_norm=row(w['kv_norm']),
        sg_ln_g=row(w['sg_ln_g']), sg_ln_b=row(w['sg_ln_b']), sg_w=w['sg_w'].astype(F32),
        sg_bias=jnp.repeat(jnp.swapaxes(w['sg_b'].astype(F32), 1, 2), BR_WIDTH // SG_GROUPS, axis=2),
        conv_w8=jnp.pad(w['conv_w'].astype(F32), ((0, 0), (0, 5), (0, 0))), pool_scale=row(w['pool_scale']),
        norm_mix_post=row(w['norm_mix_post']), norm_ffn_pre=row(w['norm_ffn_pre']), norm_ffn_post=row(w['norm_ffn_post']),
    )


def _rope_tables(positions):
    inv_freq = ROPE_BASE ** (-jnp.arange(0, QK_ROPE, 2, dtype=F32) / QK_ROPE)
    ang = positions.astype(F32)[:, None] * inv_freq
    cos, sin = jnp.cos(ang), jnp.sin(ang)
    n = positions.shape[0]
    ones, z64, z32 = jnp.ones((n, QK_NOPE), F32), jnp.zeros((n, QK_NOPE), F32), jnp.zeros((n, 32), F32)
    return (jnp.concatenate([ones, cos, cos, z32], 1), jnp.concatenate([z64, sin, sin, z32], 1),
            jnp.concatenate([z64, cos, cos, z32], 1))


def _reference_layout_grads(g):
    gate, dm1, dm2, dpb = g['dw_in_pieces']
    nl = gate.shape[0]
    sg_u, sg_v, cv_b = jnp.split(dm1, 3, axis=-1)
    cv_x, cv_c, pool = jnp.split(dm2, 3, axis=-1)
    c_q, c_kv, kr, krs = jnp.split(dpb, [Q_RANK, Q_RANK + KV_RANK, Q_RANK + KV_RANK + HEAD_PAD], axis=-1)
    rope_cols = slice(QK_NOPE, QK_NOPE + QK_ROPE)
    k_r = kr[..., rope_cols] + _rope_unswap(krs[..., rope_cols])
    w_in = jnp.concatenate([c_q, c_kv, k_r, sg_u, sg_v, cv_x, cv_b, cv_c, pool, gate], -1)
    hw = N_HEADS * HEAD_PAD
    dqa = g['dwq'][..., :hw].reshape(nl, Q_RANK, N_HEADS, HEAD_PAD)
    dqb = g['dwq'][..., hw:].reshape(nl, Q_RANK, N_HEADS, HEAD_PAD)
    w_uq = jnp.concatenate([dqa[..., :QK_NOPE], dqa[..., rope_cols] + _rope_unswap(dqb[..., rope_cols])], -1)
    dka = g['dwkv'][..., :hw].reshape(nl, KV_RANK, N_HEADS, HEAD_PAD)
    dva = g['dwkv'][..., hw:].reshape(nl, KV_RANK, N_HEADS, HEAD_PAD)
    w_ukv = jnp.concatenate([dka[..., :QK_NOPE], dva[..., :V_HEAD]], -1)
    w_br_mla = g['dw_br_mla_p'].reshape(nl, N_HEADS, HEAD_PAD, D_MODEL)[:, :, :V_HEAD]
    dwbd = g['dwbd'].reshape(nl, 4, 64, 4, 64)
    pool_w = jnp.stack([dwbd[:, k, :, k, :] for k in range(4)], axis=1)
    sq = lambda a: a[:, 0, :]
    return dict(
        norm_mix_pre=sq(g['dg_pre']), w_in=w_in, gate_b=sq(g['dgate_b']), q_norm=sq(g['dq_norm']),
        w_uq=w_uq.reshape(nl, Q_RANK, -1), kv_norm=sq(g['dkv_norm']), w_ukv=w_ukv.reshape(nl, KV_RANK, -1),
        w_br_mla=w_br_mla.reshape(nl, N_HEADS * V_HEAD, D_MODEL), sg_ln_g=sq(g['dln_g']), sg_ln_b=sq(g['dln_b']),
        sg_w=g['dsg_w'], sg_b=jnp.swapaxes(g['dsg_b'][:, :, :SG_GROUPS], 1, 2), w_br_sg=g['dw_br_sg'],
        conv_w=g['dconv_w'][:, :3], w_br_conv=g['dw_br_conv'], pool_w=pool_w, pool_scale=sq(g['dpool_scale']),
        w_br_pool=g['dw_br_pool'], w_out=g['dw_out'], norm_mix_post=sq(g['dg_post']), norm_ffn_pre=sq(g['dg_fpre']),
        w_ff1=g['dw_ff1'], w_ff2=g['dw_ff2'], norm_ffn_post=sq(g['dg_fpost']))


def _layer_forward(x0, lw, tabs, carried):
    proj = _mm("in_proj", x0, lw['w_in_p'], tm=1024, tn=896, prologue=_rms, rows=(lw['norm_mix_pre'],))
    q, k, v = _qkv_prep(proj, lw['q_norm'], lw['kv_norm'], lw['wq'], lw['wkv'], *tabs)
    (o, lse), carried_out = _attn_fwd(q, k, v, carried)
    x1 = _mix_fwd(x0, proj, o, lw)
    a = _mm("ffn1", x1, lw['w_ff1'], tm=1024, tn=1024, prologue=_rms, rows=(lw['norm_ffn_pre'],))
    x2, f = _ffn2(a, lw['w_ff2'], x1, lw['norm_ffn_post'])
    return x2, dict(x0=x0, proj=proj, q=q, k=k, v=v, o=o, lse=lse, x1=x1, a=a, f=f), carried_out


def _layer_backward(dx2, lw, sv, tabs, early, late):
    g = {}
    (df, da, g['dg_fpost']), early_out = _ffn2_bwd(dx2, sv['f'], lw['norm_ffn_post'], sv['a'], lw['w_ff2t'], early)
    carried = late(early_out)
    g['dw_ff2'] = _mm_tn("dw_ff2", sv['a'], df, tm=512, tn=512, prologue=_relu_sq)
    dx1, g['dg_fpre'] = _norm_in_bwd("ffn1_bwd", [(da, lw['w_ff1t'])], sv['x1'], lw['norm_ffn_pre'], dx2)
    g['dw_ff1'] = _mm_tn("dw_ff1", sv['x1'], da, tm=512, tn=1024, prologue=_rms, rows=(lw['norm_ffn_pre'],))
    (dgate, dm1, dyv, upool, do, delta, g['dgate_b'], g['dln_g'], g['dln_b'], g['dsg_w'], g['dsg_b'], g['dconv_w'],
     g['dwbd'], g['dpool_scale'], g['dw_br_mla_p'], g['dw_br_sg'], g['dw_br_conv'], g['dw_br_pool'], g['dw_out'],
     g['dg_post']) = _mix_bwd(dx1, sv['proj'], sv['o'], lw)
    dm2 = _shift_bwd(dyv, upool, sv['proj'], lw['conv_w8'])
    (dq, dk, dv), carried_out = _attn_bwd(sv['q'], sv['k'], sv['v'], do, sv['lse'], delta, carried)
    dpb, g['dwq'], g['dwkv'], g['dq_norm'], g['dkv_norm'] = _qkv_bwd(
        dq, dk, dv, sv['proj'], lw['q_norm'], lw['kv_norm'], lw['wq'], lw['wkv'], *tabs)
    pieces = [(dgate, lw['wt_g']), (dm1, lw['wt_m1']), (dm2, lw['wt_m2']), (dpb, lw['wt_b'])]
    dx0, g['dg_pre'] = _norm_in_bwd("in_proj_bwd", pieces, sv['x0'], lw['norm_mix_pre'], dx1)
    g['dw_in_pieces'] = [_mm_tn("dw_in_%d" % n, sv['x0'], d, tm=512, tn=1024, prologue=_rms, rows=(lw['norm_mix_pre'],))
                         for n, (d, _) in enumerate(pieces)]
    return dx0, g, carried_out


class _StepPlan(NamedTuple):
    n_layers: int
    weights_exchange: Callable
    weights_from: Callable
    grads_ready: Callable
    first_done: Callable
    second_done: Callable


def _local_step(x, positions, target, plan):
    tabs = _rope_tables(positions)
    derive = lambda w: {n: a[0] for n, a in _derive_weights(w).items()}
    first = plan.weights_exchange(0)
    weights = plan.weights_from(0, None if first is None else _run_exchange("gather_weights", first))
    derived, saved = [], []
    for l in range(plan.n_layers):
        derived.append(derive(weights))
        coming = plan.weights_exchange(l + 1) if l + 1 < plan.n_layers else None
        x, sv, arrived = _layer_forward(x, derived[l], tabs, coming)
        saved.append(sv)
        if l + 1 < plan.n_layers:
            weights = plan.weights_from(l + 1, arrived)
    loss, dx = _loss_and_grad(x, target)
    pending = None
    for l in reversed(range(plan.n_layers)):
        if pending is None:
            dx, g, _ = _layer_backward(dx, derived[l], saved[l], tabs, None, lambda _: None)
        else:
            dx, g, arrived = _layer_backward(dx, derived[l], saved[l], tabs, pending[1],
                                             functools.partial(plan.first_done, pending[0]))
            plan.second_done(pending[0], arrived)
        lead = lambda a: [b[None] for b in a] if isinstance(a, list) else a[None]
        going = plan.grads_ready(l, _reference_layout_grads({n: lead(a) for n, a in g.items()}))
        pending = None if going is None else (l, going)
    if pending is not None:
        second = plan.first_done(pending[0], _run_exchange("grads_to_sibling", pending[1]))
        plan.second_done(pending[0], _run_exchange("grads_to_chips", second))
    return loss, dx


def _relative_peers():
    x, y = lax.axis_index("x"), lax.axis_index("y")
    return {1: (x, 1 - y), 2: (1 - x, y), 3: (1 - x, 1 - y)}


def _for_my_core(fn):
    def run():
        for half in (0, 1):
            pl.when(lax.axis_index("c") == half)(functools.partial(fn, half))
    return run


def _gather_exchange(split, whole):
    ns, nw = len(split), len(whole)
    n = ns + nw

    def phases(ins, outs, sems):
        ici_send, ici_recv, d2d_send, d2d_recv, own_send, own_recv = sems
        x, y, c = lax.axis_index("x"), lax.axis_index("y"), lax.axis_index("c")
        peers = _relative_peers()

        def rows(ref, which):
            h = ref.shape[-2] // 2
            return ref.at[(slice(None),) * (len(ref.shape) - 2) + (slice(which * h, (which + 1) * h), slice(None))]

        def own(k):
            return pltpu.make_async_remote_copy(src_ref=ins[k], dst_ref=outs[k].at[0], send_sem=own_send.at[k],
                                                recv_sem=own_recv.at[k], device_id=(x, y, 1 - c), device_id_type=MESH)

        def over_ici(k, r, half):
            src = rows(ins[k], half) if k < ns else ins[k]
            dst = rows(outs[k].at[r], half) if k < ns else outs[k].at[r]
            return pltpu.make_async_remote_copy(src_ref=src, dst_ref=dst, send_sem=ici_send.at[3 * k + r - 1],
                                                recv_sem=ici_recv.at[3 * k + r - 1], device_id=(*peers[r], c), device_id_type=MESH)

        def to_sibling(k, r, half):
            landed = rows(outs[k].at[r], half)
            return pltpu.make_async_remote_copy(src_ref=landed, dst_ref=landed, send_sem=d2d_send.at[3 * k + r - 1],
                                                recv_sem=d2d_recv.at[3 * k + r - 1], device_id=(x, y, 1 - c), device_id_type=MESH)

        def start(half):
            for k in range(n):
                own(k).start()
                for r in peers:
                    over_ici(k, r, half).start()

        def middle(half):
            for k in range(n):
                for r in peers:
                    over_ici(k, r, half).wait_recv()
                    if k < ns:
                        to_sibling(k, r, half).start()

        def finish(half):
            for k in range(n):
                own(k).wait()
                for r in peers:
                    if k < ns:
                        to_sibling(k, r, 1 - half).wait_recv()
                        to_sibling(k, r, half).wait_send()
                    over_ici(k, r, half).wait_send()

        return _for_my_core(start), _for_my_core(middle), _for_my_core(finish)

    arrs = list(split) + list(whole)
    return _Exchange(
        operands=arrs, out_shape=[jax.ShapeDtypeStruct((4,) + a.shape, a.dtype) for a in arrs],
        scratch_shapes=[pltpu.SemaphoreType.DMA((3 * n,)), pltpu.SemaphoreType.DMA((3 * n,)), pltpu.SemaphoreType.DMA((3 * ns,)),
                        pltpu.SemaphoreType.DMA((3 * ns,)), pltpu.SemaphoreType.DMA((n,)), pltpu.SemaphoreType.DMA((n,))],
        phases=phases)


def _absolute_chip_order(relative):
    me = 2 * lax.axis_index("x") + lax.axis_index("y")
    return jnp.stack([lax.dynamic_index_in_dim(relative, jnp.bitwise_xor(me, chip), 0, keepdims=False) for chip in range(4)])


REDUCE_STEPS = 8


def _sibling_exchange(arrs):
    n = len(arrs)

    def phases(ins, theirs, sems):
        send_sems, recv_sems = sems
        x, y, c = lax.axis_index("x"), lax.axis_index("y"), lax.axis_index("c")

        def copy(k, my_half):
            h = ins[k].shape[1] // 2
            return pltpu.make_async_remote_copy(src_ref=ins[k].at[:, (1 - my_half) * h:(2 - my_half) * h, :], dst_ref=theirs[k],
                                                send_sem=send_sems.at[k], recv_sem=recv_sems.at[k],
                                                device_id=(x, y, 1 - c), device_id_type=MESH)

        def start(my_half):
            for k in range(n):
                copy(k, my_half).start()

        def finish(my_half):
            for k in range(n):
                copy(k, my_half).wait()

        return _for_my_core(start), lambda: None, _for_my_core(finish)

    return _Exchange(operands=list(arrs),
                     out_shape=[jax.ShapeDtypeStruct((a.shape[0], a.shape[1] // 2, a.shape[2]), a.dtype) for a in arrs],
                     scratch_shapes=[pltpu.SemaphoreType.DMA((n,)), pltpu.SemaphoreType.DMA((n,))], phases=phases)


def _add_sibling(name, arrs, theirs):
    n, steps = len(arrs), REDUCE_STEPS

    def body(*refs):
        for mine_ref, theirs_ref, out_ref in zip(refs[:n], refs[n:2 * n], refs[2 * n:]):
            out_ref[...] = (mine_ref[...] + theirs_ref[...]).astype(out_ref.dtype)

    block = lambda t: (4, t.shape[1] // steps, t.shape[2])
    return pl.pallas_call(
        body, name=name, grid=(steps,),
        in_specs=[pl.BlockSpec(block(t), lambda i: (0, lax.axis_index("c") * steps + i, 0)) for t in theirs]
        + [pl.BlockSpec(block(t), lambda i: (0, i, 0)) for t in theirs],
        out_specs=[pl.BlockSpec(block(t), lambda i: (0, i, 0)) for t in theirs],
        out_shape=[jax.ShapeDtypeStruct(t.shape, WIRE_DTYPE) for t in theirs],
        compiler_params=_params(("parallel",)))(*arrs, *theirs)


def _scatter_exchange(arrs):
    n = len(arrs)

    def phases(ins, outs, sems):
        send_sems, recv_sems = sems
        c = lax.axis_index("c")
        peers = _relative_peers()

        def copy(k, r):
            px, py = peers[r]
            return pltpu.make_async_remote_copy(src_ref=ins[k].at[2 * px + py], dst_ref=outs[k].at[r - 1],
                                                send_sem=send_sems.at[3 * k + r - 1], recv_sem=recv_sems.at[3 * k + r - 1],
                                                device_id=(px, py, c), device_id_type=MESH)

        def start():
            for k in range(n):
                for r in peers:
                    copy(k, r).start()

        def finish():
            for k in range(n):
                for r in peers:
                    copy(k, r).wait()

        return start, lambda: None, finish

    return _Exchange(operands=list(arrs), out_shape=[jax.ShapeDtypeStruct((3,) + a.shape[1:], a.dtype) for a in arrs],
                     scratch_shapes=[pltpu.SemaphoreType.DMA((3 * n,)), pltpu.SemaphoreType.DMA((3 * n,))], phases=phases)


def _sum_chips(name, chip_sums, arrived):
    n, steps = len(chip_sums), REDUCE_STEPS

    def body(*refs):
        for own_ref, arrived_ref, out_ref in zip(refs[:n], refs[n:2 * n], refs[2 * n:]):
            acc = own_ref[...].astype(F32)
            for r in range(3):
                acc = acc + arrived_ref[r].astype(F32)
            out_ref[...] = acc

    rows = lambda s: s.shape[1] // steps
    chip = lambda: 2 * lax.axis_index("x") + lax.axis_index("y")
    return pl.pallas_call(
        body, name=name, grid=(steps,),
        in_specs=[pl.BlockSpec((None, rows(s), s.shape[2]), lambda i: (chip(), i, 0)) for s in chip_sums]
        + [pl.BlockSpec((3, rows(s), s.shape[2]), lambda i: (0, i, 0)) for s in chip_sums],
        out_specs=[pl.BlockSpec((rows(s), s.shape[2]), lambda i: (lax.axis_index("c") * steps + i, 0)) for s in chip_sums],
        out_shape=[jax.ShapeDtypeStruct((2 * s.shape[1], s.shape[2]), F32) for s in chip_sums],
        compiler_params=_params(("parallel",)))(*chip_sums, *arrived)


def _join_siblings(name, bufs):
    n = len(bufs)

    def body(*refs):
        outs = refs[n:2 * n]
        send_sems, recv_sems = refs[2 * n:]
        x, y, c = lax.axis_index("x"), lax.axis_index("y"), lax.axis_index("c")

        def exchange(my_half):
            copies = []
            for k in range(n):
                h = outs[k].shape[0] // 2
                mine = outs[k].at[my_half * h:(my_half + 1) * h, :]
                theirs = outs[k].at[(1 - my_half) * h:(2 - my_half) * h, :]
                cp = pltpu.make_async_remote_copy(src_ref=mine, dst_ref=mine, send_sem=send_sems.at[k],
                                                  recv_sem=recv_sems.at[k], device_id=(x, y, 1 - c), device_id_type=MESH)
                cp.start()
                arrival = pltpu.make_async_remote_copy(src_ref=theirs, dst_ref=theirs, send_sem=send_sems.at[k],
                                                       recv_sem=recv_sems.at[k], device_id=(x, y, 1 - c), device_id_type=MESH)
                copies.append((cp, arrival))
            for cp, arrival in copies:
                arrival.wait_recv()
                cp.wait_send()

        for half in (0, 1):
            pl.when(c == half)(functools.partial(exchange, half))

    return pl.pallas_call(
        body, name=name, in_specs=[HBM] * n, out_specs=[HBM] * n,
        out_shape=[jax.ShapeDtypeStruct(b.shape, b.dtype) for b in bufs], input_output_aliases={k: k for k in range(n)},
        scratch_shapes=[pltpu.SemaphoreType.DMA((n,)), pltpu.SemaphoreType.DMA((n,))],
    )(*bufs)


def _gather_all_exchange(a):
    def phases(ins, outs, scratch):
        (a_ref,), (out_ref,) = ins, outs
        staging, send_sems, recv_sems, local_sem = scratch
        x, y, c = lax.axis_index("x"), lax.axis_index("y"), lax.axis_index("c")
        me = 4 * x + 2 * y + c
        flips = [(fx, fy, fc) for fx in (0, 1) for fy in (0, 1) for fc in (0, 1)][1:]
        peers = [(x ^ fx, y ^ fy, c ^ fc) for fx, fy, fc in flips]

        def copy(j):
            px, py, pc = peers[j]
            return pltpu.make_async_remote_copy(src_ref=a_ref, dst_ref=out_ref.at[me], send_sem=send_sems.at[j],
                                                recv_sem=recv_sems.at[j], device_id=(px, py, pc), device_id_type=MESH)

        def arrival(j):
            px, py, pc = peers[j]
            return pltpu.make_async_remote_copy(src_ref=a_ref, dst_ref=out_ref.at[4 * px + 2 * py + pc], send_sem=send_sems.at[j],
                                                recv_sem=recv_sems.at[j], device_id=(px, py, pc), device_id_type=MESH)

        own = pltpu.make_async_copy(staging, out_ref.at[me], local_sem)

        def start():
            load = pltpu.make_async_copy(a_ref, staging, local_sem)
            load.start()
            load.wait()
            own.start()
            for j in range(7):
                copy(j).start()

        def finish():
            for j in range(7):
                arrival(j).wait_recv()
            for j in range(7):
                copy(j).wait_send()
            own.wait()

        return start, lambda: None, finish

    return _Exchange(operands=[a], out_shape=[jax.ShapeDtypeStruct((8,) + a.shape, a.dtype)],
                     scratch_shapes=[pltpu.VMEM(a.shape, a.dtype), pltpu.SemaphoreType.DMA((7,)), pltpu.SemaphoreType.DMA((7,)),
                                     pltpu.SemaphoreType.DMA], phases=phases)


def _rowwise_call(name, fn, slots, out_shapes, steps, carried=None):
    n_in, n_out = [len(s) for s in slots], [len(o) for o in out_shapes]

    def spec(shape):
        if len(shape) == 3:
            return pl.BlockSpec((shape[0], shape[1] // steps, shape[2]), lambda i: (0, i, 0))
        return pl.BlockSpec((shape[0] // steps, shape[1]), lambda i: (i, 0))

    def body(*refs):
        ins, outs = refs[:sum(n_in)], refs[sum(n_in):]
        a = b = 0
        for k in range(len(slots)):
            for o_ref, val in zip(outs[b:b + n_out[k]], fn(*[r[...] for r in ins[a:a + n_in[k]]])):
                o_ref[...] = val
            a, b = a + n_in[k], b + n_out[k]

    flat_in = [arr for s in slots for arr in s]
    flat_out = [shp for o in out_shapes for shp in o]
    out, carried_out = _call_carrying(
        body, carried, name=name, grid=(steps,), in_specs=[spec(a.shape) for a in flat_in],
        out_specs=[spec(s) for s in flat_out], out_shape=[jax.ShapeDtypeStruct(s, F32) for s in flat_out],
        operands=flat_in, semantics=("parallel",))
    grouped, b = [], 0
    for k in range(len(slots)):
        grouped.append(out[b:b + n_out[k]])
        b += n_out[k]
    return grouped if carried is None else (grouped, carried_out)


def _sum_in_order(a):
    acc = a[0].astype(F32)
    for k in range(1, a.shape[0]):
        acc = acc + a[k].astype(F32)
    return (acc,)


def _adamw_math(w, g, m, v):
    m_new = ADAM_B1 * m + (1.0 - ADAM_B1) * g
    v_new = ADAM_B2 * v + (1.0 - ADAM_B2) * (g * g)
    m_hat = m_new / (1.0 - ADAM_B1 ** ADAM_STEP)
    v_hat = v_new / (1.0 - ADAM_B2 ** ADAM_STEP)
    return -ADAM_LR * (m_hat / (jnp.sqrt(v_hat) + ADAM_EPS) + ADAM_WD * w), m_new, v_new


SMALL_PACK_COLS = 256
SMALL_PACK_ROWS = 2048


def _pack_small(parts):
    wide = [jnp.pad(p, ((0, 0), (0, 0), (0, SMALL_PACK_COLS - p.shape[2]))) for p in parts]
    rows = jnp.concatenate(wide, axis=1)
    return jnp.pad(rows, ((0, 0), (0, SMALL_PACK_ROWS - rows.shape[1]), (0, 0)))


def _unpack_small(packed, shapes):
    out, row = [], 0
    for a, b in shapes:
        out.append(packed[:, row:row + a, :b])
        row += a
    return out


def _pack(arrs, rows_per_layer, dtype):
    nl = arrs[0].shape[0]
    flat = jnp.concatenate([a.astype(dtype).reshape(nl, -1) for a in arrs], axis=1)
    flat = jnp.pad(flat, ((0, 0), (0, rows_per_layer * PACK_COLS - flat.shape[1])))
    return flat.reshape(nl * rows_per_layer, PACK_COLS)


def _unpack(packed, shapes, rows_per_layer):
    nl = shapes[0][0]
    flat = packed.reshape(packed.shape[:-2] + (nl, rows_per_layer * PACK_COLS))
    out, off = [], 0
    for shp in shapes:
        size = math.prod(shp[1:])
        out.append(flat[..., off:off + size].reshape(packed.shape[:-2] + tuple(shp)))
        off += size
    return out


def _rows_needed(shapes, multiple):
    per_layer = sum(math.prod(s[1:]) for s in shapes)
    rows = -(-per_layer // PACK_COLS)
    return -(-rows // multiple) * multiple


CONV_TILE = (8, 128)


def _layer_shard_exchange(w, l):
    conv = w['conv_w'][l].reshape(-1)
    conv = jnp.pad(conv, (0, math.prod(CONV_TILE) - conv.shape[0])).reshape(CONV_TILE)
    return _gather_exchange([w[n][l].astype(MXU_DTYPE) for n in MATMUL_SHARDED], [conv])


def _layer_full_weights(w, l, gathered):
    gathered = [_absolute_chip_order(g) for g in gathered]
    full = {n: w[n][l:l + 1] for n in WEIGHTS}
    for n, part in zip(MATMUL_SHARDED, gathered):
        if n in ROW_SHARDED:
            full[n] = part.reshape(1, 4 * part.shape[1], part.shape[2])
        else:
            full[n] = jnp.swapaxes(part, 0, 1).reshape(1, part.shape[1], 4 * part.shape[2])
    rows, cols = w['conv_w'].shape[1:]
    conv = gathered[-1].reshape(4, -1)[:, :rows * cols].reshape(4, rows, cols)
    full['conv_w'] = jnp.swapaxes(conv, 0, 1).reshape(1, rows, 4 * cols)
    return full


def _chip_major(n, g):
    nl = g.shape[0]
    if n in ROW_SHARDED:
        return jnp.swapaxes(g.reshape(nl, 4, g.shape[1] // 4, g.shape[2]), 0, 1)
    return jnp.transpose(g.reshape(nl, g.shape[1], 4, g.shape[2] // 4), (2, 0, 1, 3))


def kernel(x, positions, norm_mix_pre, w_in, gate_b, q_norm, w_uq, kv_norm, w_ukv, w_br_mla, sg_ln_g, sg_ln_b, sg_w, sg_b, w_br_sg, conv_w, w_br_conv, pool_w, pool_scale, w_br_pool, w_out, norm_mix_post, norm_ffn_pre, w_ff1, w_ff2, norm_ffn_post, loss_target, m_norm_mix_pre, m_w_in, m_gate_b, m_q_norm, m_w_uq, m_kv_norm, m_w_ukv, m_w_br_mla, m_sg_ln_g, m_sg_ln_b, m_sg_w, m_sg_b, m_w_br_sg, m_conv_w, m_w_br_conv, m_pool_w, m_pool_scale, m_w_br_pool, m_w_out, m_norm_mix_post, m_norm_ffn_pre, m_w_ff1, m_w_ff2, m_norm_ffn_post, v_norm_mix_pre, v_w_in, v_gate_b, v_q_norm, v_w_uq, v_kv_norm, v_w_ukv, v_w_br_mla, v_sg_ln_g, v_sg_ln_b, v_sg_w, v_sg_b, v_w_br_sg, v_conv_w, v_w_br_conv, v_pool_w, v_pool_scale, v_w_br_pool, v_w_out, v_norm_mix_post, v_norm_ffn_pre, v_w_ff1, v_w_ff2, v_norm_ffn_post):
    given = dict(locals())
    w = {n: given[n] for n in WEIGHTS}
    mom = {n: given['m_' + n] for n in WEIGHTS}
    var = {n: given['v_' + n] for n in WEIGHTS}
    nl = w['w_in'].shape[0]
    sharded, chip_sums, reduced, local_small = {}, {}, {}, {}

    def grads_ready(l, g):
        sharded[l] = [_chip_major(n, g[n])[:, 0] for n in BIG_SHARDED]
        sharded[l].append(_pack_small([_chip_major(n, g[n])[:, 0] for n in SMALL_SHARDED]))
        local_small[l] = [g[n] for n in REPLICATED + ['conv_w']]
        return _sibling_exchange(sharded[l])

    def sibling_done(l, theirs):
        chip_sums[l] = _add_sibling("add_sibling", sharded[l], theirs)
        return _scatter_exchange(chip_sums[l])

    def chips_done(l, arrived):
        reduced[l] = _join_siblings("join_halves", _sum_chips("sum_chips", chip_sums[l], arrived))

    plan = _StepPlan(n_layers=nl, weights_exchange=functools.partial(_layer_shard_exchange, w),
                     weights_from=functools.partial(_layer_full_weights, w), grads_ready=grads_ready,
                     first_done=sibling_done, second_done=chips_done)
    loss, dx = _local_step(x[0], positions[0], loss_target[0], plan)
    loss = lax.psum(loss, ("x", "y", "c"))

    grad, delta, new_m, new_v = {}, {}, {}, {}
    names = REPLICATED + ['conv_w']
    local = [jnp.concatenate([local_small[l][k] for l in range(nl)]) for k in range(len(names))]
    rows = _rows_needed([a.shape for a in local], 32)
    small_grads = _gather_all_exchange(_pack(local, rows, F32))
    for k, n in enumerate(BIG_SHARDED):
        grad[n] = jnp.stack([reduced[l][k] for l in range(nl)])
    slots = [[w[n], grad[n], mom[n], var[n]] for n in BIG_SHARDED]
    small_pack = lambda d: _pack_small([d[n] for n in SMALL_SHARDED])
    g_small = jnp.stack([reduced[l][-1] for l in range(nl)])
    slots.append([small_pack(w), g_small, small_pack(mom), small_pack(var)])
    updated, (everyone,) = _rowwise_call("adamw_sharded", _adamw_math, slots, [[s_[0].shape] * 3 for s_ in slots], 32,
                                         carried=small_grads)
    for k, n in enumerate(BIG_SHARDED):
        delta[n], new_m[n], new_v[n] = updated[k]
    for d, packed in zip((grad, delta, new_m, new_v), [g_small] + list(updated[-1])):
        d.update(zip(SMALL_SHARDED, _unpack_small(packed, [w[n].shape[1:] for n in SMALL_SHARDED])))

    (summed,), = _rowwise_call("sum_devices", _sum_in_order, [[everyone]], [[everyone.shape[1:]]], 4)
    g_rep = _unpack(summed, [a.shape for a in local], rows)
    chip = 2 * lax.axis_index("x") + lax.axis_index("y")
    g_rep[-1] = lax.dynamic_slice_in_dim(g_rep[-1], chip * w['conv_w'].shape[2], w['conv_w'].shape[2], axis=2)
    rep_pack = lambda arrs: _pack(arrs, rows, F32)
    (rep_out,) = _rowwise_call("adamw_replicated", _adamw_math,
                               [[rep_pack([w[n] for n in names]), rep_pack(g_rep), rep_pack([mom[n] for n in names]),
                                 rep_pack([var[n] for n in names])]], [[(nl * rows, PACK_COLS)] * 3], 4)
    grad.update(zip(names, g_rep))
    for d, packed in zip((delta, new_m, new_v), rep_out):
        d.update(zip(names, _unpack(packed, [w[n].shape for n in names], rows)))

    return (loss, dx[None], *[grad[n] for n in WEIGHTS], *[delta[n] for n in WEIGHTS], *[new_m[n] for n in WEIGHTS],
            *[new_v[n] for n in WEIGHTS])
```

```python
import functools
import math
from typing import Any, Callable, NamedTuple, Sequence

import jax
import jax.numpy as jnp
from jax import lax
from jax.experimental import pallas as pl
from jax.experimental.pallas import tpu as pltpu

F32 = jnp.float32
MXU_DTYPE = jnp.bfloat16
WIRE_DTYPE = jnp.bfloat16
ACT_DTYPE = jnp.bfloat16
MESH = pl.DeviceIdType.MESH

D_MODEL = 1024
D_FF = 4096
N_HEADS = 4
QK_NOPE = 64
QK_ROPE = 32
V_HEAD = 64
HEAD_PAD = 128
Q_RANK = 256
KV_RANK = 128
SG_CHUNK = 128
SG_GROUPS = 4
BR_WIDTH = 256
N_BRANCH = 4
POOL_WINDOWS = (2, 4, 8, 16)
HALO = 16
ROPE_BASE = 10000.0
EPS = 1e-6
ATTN_SCALE = (QK_NOPE + QK_ROPE) ** -0.5
LOG2_E = math.log2(math.e)
FWD_HEADS_PER_STEP = 4
HEADS_PER_STEP = 2
N_PROJ = N_BRANCH * D_MODEL + 6 * BR_WIDTH + Q_RANK + KV_RANK + 2 * HEAD_PAD
COL_G, COL_M1, COL_M2, COL_B = 0, 4096, 4864, 5632

ADAM_LR, ADAM_B1, ADAM_B2, ADAM_EPS, ADAM_WD, ADAM_STEP = 0.001, 0.9, 0.999, 1e-08, 0.01, 10

VMEM_LIMIT = 56 * 1024 * 1024

WEIGHTS = ['norm_mix_pre', 'w_in', 'gate_b', 'q_norm', 'w_uq', 'kv_norm', 'w_ukv', 'w_br_mla', 'sg_ln_g', 'sg_ln_b',
           'sg_w', 'sg_b', 'w_br_sg', 'conv_w', 'w_br_conv', 'pool_w', 'pool_scale', 'w_br_pool', 'w_out',
           'norm_mix_post', 'norm_ffn_pre', 'w_ff1', 'w_ff2', 'norm_ffn_post']
COL_SHARDED = ['w_in', 'w_uq', 'w_ukv', 'w_br_mla', 'w_br_sg', 'w_br_conv', 'w_br_pool', 'w_ff1']
ROW_SHARDED = ['w_out', 'w_ff2']
MATMUL_SHARDED = ['w_in', 'w_uq', 'w_ukv', 'w_br_mla', 'w_br_sg', 'w_br_conv', 'w_br_pool', 'w_out', 'w_ff1', 'w_ff2']
BIG_SHARDED = ['w_in', 'w_ff1', 'w_ff2', 'w_out']
SMALL_SHARDED = ['w_uq', 'w_ukv', 'w_br_mla', 'w_br_sg', 'w_br_conv', 'w_br_pool']
SHARDED = MATMUL_SHARDED + ['conv_w']
REPLICATED = [n for n in WEIGHTS if n not in SHARDED]
PACK_COLS = 1024


def _params(sem, vmem=VMEM_LIMIT):
    return pltpu.CompilerParams(dimension_semantics=sem, vmem_limit_bytes=vmem)


def _mxu(a):
    return a.astype(MXU_DTYPE)


def _dot(a, b):
    return jnp.dot(_mxu(a), _mxu(b), preferred_element_type=F32)


def _dot_nt(a, b):
    return lax.dot_general(_mxu(a), _mxu(b), (((1,), (1,)), ((), ())), preferred_element_type=F32)


def _dot_tn(a, b):
    return lax.dot_general(_mxu(a), _mxu(b), (((0,), (0,)), ((), ())), preferred_element_type=F32)


def _rms(x, g):
    r = lax.rsqrt(jnp.mean(x * x, axis=-1, keepdims=True) + EPS)
    return x * r * g


def _rms_bwd(x, g, dy):
    r = lax.rsqrt(jnp.mean(x * x, axis=-1, keepdims=True) + EPS)
    xh = x * r
    gdy = dy * g
    dx = r * (gdy - xh * jnp.mean(gdy * xh, axis=-1, keepdims=True))
    return dx, jnp.sum(dy * xh, axis=0, keepdims=True)


_GELU_C = math.sqrt(2.0 / math.pi)


def _gelu(x):
    t = jnp.tanh(_GELU_C * (x + 0.044715 * (x * x * x)))
    return x * (0.5 * (1.0 + t)), t


def _gelu_grad(x, t):
    return 0.5 * (1.0 + t) + 0.5 * x * (1.0 - t * t) * (_GELU_C * (1.0 + 3.0 * 0.044715 * x * x))


def _sigmoid(x):
    return 1.0 / (1.0 + jnp.exp(-x))


def _full(shape):
    return pl.BlockSpec(shape, lambda *_: (0,) * len(shape))


def _resident(shape):
    return pl.BlockSpec(shape, lambda *_: (0,) * len(shape), pipeline_mode=pl.Buffered(1))


def _rows(ts, width, col=0):
    return pl.BlockSpec((ts, width), lambda i: (i, col))


def _tile(n, pref):
    return min(n, pref)


def _mm(name, a, w, *, tm, tn, prologue=None, rows=()):
    m, k = a.shape
    n = w.shape[1]
    tm, tn = _tile(m, tm), _tile(n, tn)

    def body(a_ref, *rest):
        row_refs, w_ref, o_ref = rest[:len(rows)], rest[len(rows)], rest[len(rows) + 1]
        av = a_ref[...]
        if prologue is not None:
            av = prologue(av, *[r[...] for r in row_refs])
        o_ref[...] = _dot(av, w_ref[...]).astype(o_ref.dtype)

    return pl.pallas_call(
        body, name=name, grid=(m // tm, n // tn),
        in_specs=[pl.BlockSpec((tm, k), lambda i, j: (i, 0))] + [pl.BlockSpec((1, k), lambda i, j: (0, 0)) for _ in rows]
        + [_resident((k, n)) if tn == n else pl.BlockSpec((k, tn), lambda i, j: (0, j))],
        out_specs=pl.BlockSpec((tm, tn), lambda i, j: (i, j)),
        out_shape=jax.ShapeDtypeStruct((m, n), ACT_DTYPE),
        compiler_params=_params(("parallel", "parallel")),
    )(a, *rows, w)


def _mm_tn(name, a, b, *, tm, tn, prologue=None, rows=()):
    m, k = a.shape
    n = b.shape[1]
    tm, tn = _tile(m, tm), _tile(n, tn)

    def body(a_ref, *rest):
        row_refs, b_ref, o_ref = rest[:len(rows)], rest[len(rows)], rest[len(rows) + 1]

        @pl.when(pl.program_id(1) == 0)
        def _():
            o_ref[...] = jnp.zeros_like(o_ref)

        av = a_ref[...]
        if prologue is not None:
            av = prologue(av, *[r[...] for r in row_refs])
        o_ref[...] += _dot_tn(av, b_ref[...])

    return pl.pallas_call(
        body, name=name, grid=(n // tn, m // tm),
        in_specs=[pl.BlockSpec((tm, k), lambda j, i: (i, 0))] + [pl.BlockSpec((1, k), lambda j, i: (0, 0)) for _ in rows]
        + [pl.BlockSpec((tm, tn), lambda j, i: (i, j))],
        out_specs=pl.BlockSpec((k, tn), lambda j, i: (0, j)),
        out_shape=jax.ShapeDtypeStruct((k, n), F32),
        compiler_params=_params(("parallel", "arbitrary")),
    )(a, *rows, b)


def _relu_sq(a):
    r = jnp.maximum(a.astype(F32), 0.0)
    return r * r


HBM = pl.BlockSpec(memory_space=pl.ANY)


class _Exchange(NamedTuple):
    operands: Sequence[Any]
    out_shape: Sequence[Any]
    scratch_shapes: Sequence[Any]
    phases: Callable


def _run_exchange(name, ex):
    n_in, n_out = len(ex.operands), len(ex.out_shape)

    def body(*refs):
        for phase in ex.phases(refs[:n_in], refs[n_in:n_in + n_out], refs[n_in + n_out:]):
            phase()

    return pl.pallas_call(body, name=name, in_specs=[HBM] * n_in, out_specs=[HBM] * n_out, out_shape=list(ex.out_shape),
                          scratch_shapes=list(ex.scratch_shapes))(*ex.operands)


def _call_carrying(body, carried, *, name, grid, in_specs, out_specs, out_shape, operands, semantics, middle_step=None):
    if carried is None:
        return pl.pallas_call(body, name=name, grid=grid, in_specs=in_specs, out_specs=out_specs, out_shape=out_shape,
                              compiler_params=_params(semantics))(*operands), None
    n_main_in, n_main_out = len(operands), len(out_shape)
    n_in, n_out = len(carried.operands), len(carried.out_shape)
    steps = math.prod(grid)

    def wrapped(*refs):
        main_in, refs = refs[:n_main_in], refs[n_main_in:]
        ex_in, refs = refs[:n_in], refs[n_in:]
        main_out, refs = refs[:n_main_out], refs[n_main_out:]
        ex_out, sems = refs[:n_out], refs[n_out:]
        step = pl.program_id(0)
        for axis in range(1, len(grid)):
            step = step * grid[axis] + pl.program_id(axis)
        start, middle, finish = carried.phases(ex_in, ex_out, sems)
        pl.when(step == 0)(start)
        pl.when(step == ((steps - 1) // 2 if middle_step is None else middle_step))(middle)
        body(*main_in, *main_out)
        pl.when(step == steps - 1)(finish)

    out = pl.pallas_call(
        wrapped, name=name + "_carrying", grid=grid, in_specs=list(in_specs) + [HBM] * n_in,
        out_specs=list(out_specs) + [HBM] * n_out, out_shape=list(out_shape) + list(carried.out_shape),
        scratch_shapes=list(carried.scratch_shapes), compiler_params=_params(("arbitrary",) * len(grid)))(*operands, *carried.operands)
    return out[:n_main_out], out[n_main_out:]


def _qkv_prep(proj, q_norm, kv_norm, wq, wkv, cq_tab, s_tab, cr_tab):
    s = proj.shape[0]
    ts = _tile(s, 512)
    hw = N_HEADS * HEAD_PAD

    def body(cq_ref, ckv_ref, kr_ref, krs_ref, gq_ref, gkv_ref, wq_ref, wkv_ref, ct_ref, st_ref, crt_ref,
             q_ref, k_ref, v_ref):
        ct, st, crt = ct_ref[...], st_ref[...], crt_ref[...]
        qn = _rms(cq_ref[...].astype(F32), gq_ref[...])
        qab = _dot(qn, wq_ref[...])
        kvn = _rms(ckv_ref[...].astype(F32), gkv_ref[...])
        kav = _dot(kvn, wkv_ref[...])
        k_rope = kr_ref[...].astype(F32) * crt + krs_ref[...].astype(F32) * st
        ones_lane = (lax.broadcasted_iota(jnp.int32, (1, HEAD_PAD), 1) == V_HEAD).astype(F32)
        for h in range(N_HEADS):
            lo = h * HEAD_PAD
            q_ref[h] = (qab[:, lo:lo + HEAD_PAD] * ct + qab[:, hw + lo:hw + lo + HEAD_PAD] * st).astype(q_ref.dtype)
            k_ref[h] = (kav[:, lo:lo + HEAD_PAD] + k_rope).astype(k_ref.dtype)
            v_ref[h] = (kav[:, hw + lo:hw + lo + HEAD_PAD] + ones_lane).astype(v_ref.dtype)

    head_spec = pl.BlockSpec((N_HEADS, ts, HEAD_PAD), lambda i: (0, i, 0))
    head_shape = jax.ShapeDtypeStruct((N_HEADS, s, HEAD_PAD), MXU_DTYPE)
    return pl.pallas_call(
        body, name="qkv_prep", grid=(s // ts,),
        in_specs=[_rows(ts, Q_RANK, COL_B // Q_RANK), _rows(ts, KV_RANK, (COL_B + Q_RANK) // KV_RANK),
                  _rows(ts, HEAD_PAD, (COL_B + Q_RANK + KV_RANK) // HEAD_PAD),
                  _rows(ts, HEAD_PAD, (COL_B + Q_RANK + KV_RANK + HEAD_PAD) // HEAD_PAD),
                  _full((1, Q_RANK)), _full((1, KV_RANK)), _full((Q_RANK, 2 * hw)), _full((KV_RANK, 2 * hw)),
                  _rows(ts, HEAD_PAD), _rows(ts, HEAD_PAD), _rows(ts, HEAD_PAD)],
        out_specs=[head_spec, head_spec, head_spec],
        out_shape=[head_shape, head_shape, head_shape],
        compiler_params=_params(("parallel",)),
    )(proj, proj, proj, proj, q_norm, kv_norm, wq, wkv, cq_tab, s_tab, cr_tab)


def _diagonal_mask(t, keys_on_rows=False):
    key_axis = 0 if keys_on_rows else 1
    return lax.broadcasted_iota(jnp.int32, (t, t), key_axis) <= lax.broadcasted_iota(jnp.int32, (t, t), 1 - key_axis)


MIDDLE_WORK_SHARE = 0.65


def _causal_work_step(nq, groups):
    total = groups * nq * (nq + 1) // 2
    done = 0
    for step in range(groups * nq):
        if done >= MIDDLE_WORK_SHARE * total:
            return step
        done += step % nq + 1
    return groups * nq - 1


def _attn_fwd(q, k, v, carried=None):
    s = q.shape[1]
    t = _tile(s, 512)

    def body(q_ref, k_ref, v_ref, o_ref, lse_ref):
        i = pl.program_id(1)
        lane = lax.broadcasted_iota(jnp.int32, (1, HEAD_PAD), 1)

        def step(j, carry, on_diagonal):
            rows = pl.ds(pl.multiple_of(j * t, t), t)
            out = []
            for h in range(FWD_HEADS_PER_STEP):
                m, acc = carry[h]
                sc = _dot_nt(q_ref[h], k_ref[h, rows, :]) * (ATTN_SCALE * LOG2_E)
                if on_diagonal:
                    sc = jnp.where(_diagonal_mask(t), sc, -jnp.inf)
                m_new = jnp.maximum(m, jnp.max(sc, axis=1, keepdims=True))
                out.append((m_new, jnp.exp2(m - m_new) * acc + _dot(jnp.exp2(sc - m_new), v_ref[h, rows, :])))
            return tuple(out)

        init = ((jnp.full((t, 1), -jnp.inf, F32), jnp.zeros((t, HEAD_PAD), F32)),) * FWD_HEADS_PER_STEP
        below = lax.fori_loop(0, i, functools.partial(step, on_diagonal=False), init)
        for h, (m, acc) in enumerate(step(i, below, True)):
            l = jnp.sum(jnp.where(lane == V_HEAD, acc, 0.0), axis=1, keepdims=True)
            o_ref[:, h * HEAD_PAD:(h + 1) * HEAD_PAD] = jnp.where(lane < V_HEAD, acc / l, 0.0)
            lse_ref[h] = m + jnp.log2(l)

    group = FWD_HEADS_PER_STEP
    return _call_carrying(
        body, carried, name="attn_fwd", grid=(N_HEADS // group, s // t),
        in_specs=[pl.BlockSpec((group, t, HEAD_PAD), lambda h, i: (h, i, 0)),
                  pl.BlockSpec((group, s, HEAD_PAD), lambda h, i: (h, 0, 0)),
                  pl.BlockSpec((group, s, HEAD_PAD), lambda h, i: (h, 0, 0))],
        out_specs=[pl.BlockSpec((t, group * HEAD_PAD), lambda h, i: (i, h)), pl.BlockSpec((group, t, 1), lambda h, i: (h, i, 0))],
        out_shape=[jax.ShapeDtypeStruct((s, N_HEADS * HEAD_PAD), F32), jax.ShapeDtypeStruct((N_HEADS, s, 1), F32)],
        operands=(q, k, v), semantics=("parallel", "parallel"), middle_step=_causal_work_step(s // t, N_HEADS // group))


def _attn_bwd(q, k, v, do, lse, delta, carried=None):
    s = q.shape[1]
    t = _tile(s, 512)
    nq = s // t

    def body(q_ref, do_ref, lse_ref, dl_ref, k_ref, v_ref, dq_ref, dk_ref, dv_ref):
        j = pl.program_id(1)

        @pl.when(j == 0)
        def _():
            dq_ref[...] = jnp.zeros_like(dq_ref)

        def step(i, carry, on_diagonal):
            rows = pl.ds(pl.multiple_of(i * t, t), t)
            out = []
            for h in range(HEADS_PER_STEP):
                dk, dv = carry[h]
                qi, doi = q_ref[h, rows, :], do_ref[rows, h * HEAD_PAD:(h + 1) * HEAD_PAD]
                sc = _dot_nt(k_ref[h], qi) * (ATTN_SCALE * LOG2_E)
                if on_diagonal:
                    sc = jnp.where(_diagonal_mask(t, keys_on_rows=True), sc, -jnp.inf)
                p = jnp.exp2(sc - lse_ref[h, i])
                dv = dv + _dot(p, doi)
                ds = p * (_dot_nt(v_ref[h], doi) - dl_ref[h, i])
                dk = dk + _dot(ds, qi)
                dq_ref[h, rows, :] += _dot_tn(ds, k_ref[h]) * ATTN_SCALE
                out.append((dk, dv))
            return tuple(out)

        zero = ((jnp.zeros((t, HEAD_PAD), F32),) * 2,) * HEADS_PER_STEP
        sums = lax.fori_loop(j + 1, nq, functools.partial(step, on_diagonal=False), step(j, zero, True))
        for h, (dk, dv) in enumerate(sums):
            dk_ref[h] = dk * ATTN_SCALE
            dv_ref[h] = dv

    group = HEADS_PER_STEP
    whole = lambda w: pl.BlockSpec((group, s, w), lambda h, j: (h, 0, 0), pipeline_mode=pl.Buffered(1))
    tile = pl.BlockSpec((group, t, HEAD_PAD), lambda h, j: (h, j, 0))
    per_query = pl.BlockSpec((group, nq, 1, t), lambda h, j: (h, 0, 0, 0))
    head_shape = jax.ShapeDtypeStruct((N_HEADS, s, HEAD_PAD), F32)
    return _call_carrying(
        body, carried, name="attn_bwd", grid=(N_HEADS // group, nq),
        in_specs=[whole(HEAD_PAD), pl.BlockSpec((s, group * HEAD_PAD), lambda h, j: (0, h), pipeline_mode=pl.Buffered(1)),
                  per_query, per_query, tile, tile],
        out_specs=[whole(HEAD_PAD), tile, tile],
        out_shape=[head_shape, head_shape, head_shape],
        operands=(q, do, lse.reshape(N_HEADS, nq, 1, t), delta.reshape(N_HEADS, nq, 1, t), k, v),
        semantics=("parallel", "arbitrary"))


def _qkv_bwd(dq, dk, dv, proj, q_norm, kv_norm, wq, wkv, cq_tab, s_tab, cr_tab):
    s = proj.shape[0]
    ts = _tile(s, 512)
    hw = N_HEADS * HEAD_PAD

    def body(dq_ref, dk_ref, dv_ref, cq_ref, ckv_ref, gq_ref, gkv_ref, wq_ref, wkv_ref, ct_ref, st_ref, crt_ref,
             dpb_ref, dwq_ref, dwkv_ref, dgq_ref, dgkv_ref):
        @pl.when(pl.program_id(0) == 0)
        def _():
            for r in (dwq_ref, dwkv_ref, dgq_ref, dgkv_ref):
                r[...] = jnp.zeros_like(r)

        ct, st, crt = ct_ref[...], st_ref[...], crt_ref[...]
        dqs = [dq_ref[h] for h in range(N_HEADS)]
        dks = [dk_ref[h] for h in range(N_HEADS)]
        dqab = jnp.concatenate([d * ct for d in dqs] + [d * st for d in dqs], axis=1)
        dkav = jnp.concatenate(dks + [dv_ref[h] for h in range(N_HEADS)], axis=1)
        dk_sum = dks[0] + dks[1] + dks[2] + dks[3]
        cq, ckv, gq, gkv = cq_ref[...].astype(F32), ckv_ref[...].astype(F32), gq_ref[...], gkv_ref[...]
        dwq_ref[...] += _dot_tn(_rms(cq, gq), dqab)
        dwkv_ref[...] += _dot_tn(_rms(ckv, gkv), dkav)
        dcq, dgq = _rms_bwd(cq, gq, _dot_nt(dqab, wq_ref[...]))
        dckv, dgkv = _rms_bwd(ckv, gkv, _dot_nt(dkav, wkv_ref[...]))
        dgq_ref[...] += dgq
        dgkv_ref[...] += dgkv
        dpb_ref[...] = jnp.concatenate([dcq, dckv, dk_sum * crt, dk_sum * st], axis=1).astype(dpb_ref.dtype)

    head_spec = pl.BlockSpec((N_HEADS, ts, HEAD_PAD), lambda i: (0, i, 0))
    wb = Q_RANK + KV_RANK + 2 * HEAD_PAD
    return pl.pallas_call(
        body, name="qkv_bwd", grid=(s // ts,),
        in_specs=[head_spec, head_spec, head_spec,
                  _rows(ts, Q_RANK, COL_B // Q_RANK), _rows(ts, KV_RANK, (COL_B + Q_RANK) // KV_RANK),
                  _full((1, Q_RANK)), _full((1, KV_RANK)), _full((Q_RANK, 2 * hw)), _full((KV_RANK, 2 * hw)),
                  _rows(ts, HEAD_PAD), _rows(ts, HEAD_PAD), _rows(ts, HEAD_PAD)],
        out_specs=[_rows(ts, wb), _full((Q_RANK, 2 * hw)), _full((KV_RANK, 2 * hw)), _full((1, Q_RANK)), _full((1, KV_RANK))],
        out_shape=[jax.ShapeDtypeStruct((s, wb), MXU_DTYPE), jax.ShapeDtypeStruct((Q_RANK, 2 * hw), F32),
                   jax.ShapeDtypeStruct((KV_RANK, 2 * hw), F32), jax.ShapeDtypeStruct((1, Q_RANK), F32),
                   jax.ShapeDtypeStruct((1, KV_RANK), F32)],
        compiler_params=_params(("arbitrary",)),
    )(dq, dk, dv, proj, proj, q_norm, kv_norm, wq, wkv, cq_tab, s_tab, cr_tab)


def _lane_group(width):
    return lax.broadcasted_iota(jnp.int32, (1, width), 1) // (width // 4)


def _shift_down(a, k):
    return pltpu.roll(a, k, 0)


def _shift_up(a, k):
    return pltpu.roll(a, a.shape[0] - k, 0)


def _window_sums(xh, shift):
    s2 = xh + shift(xh, 1)
    s4 = s2 + shift(s2, 2)
    s8 = s4 + shift(s4, 4)
    s16 = s8 + shift(s8, 8)
    grp = _lane_group(xh.shape[1])
    return jnp.where(grp == 0, s2, jnp.where(grp == 1, s4, jnp.where(grp == 2, s8, s16)))


def _pool_count(i, ts):
    grp = _lane_group(BR_WIDTH)
    win = jnp.where(grp == 0, 2.0, jnp.where(grp == 1, 4.0, jnp.where(grp == 2, 8.0, 16.0)))
    t = (i * ts + lax.broadcasted_iota(jnp.int32, (ts, 1), 0)).astype(F32)
    return jnp.minimum(t + 1.0, win)


def _mix_forward(i, ts, r):
    f = {}
    act = lambda name: r[name][...].astype(F32)
    f['gates'] = _sigmoid(act('gate') + r['gate_b'][...])
    sgu, sgv = act('sgu'), act('sgv')
    f['sgu'], f['sgv'] = sgu, sgv
    u_act, f['tu'] = _gelu(sgu)
    vg, f['tv'] = _gelu(sgv)
    mu = jnp.mean(vg, axis=-1, keepdims=True)
    xc = vg - mu
    f['ln_r'] = lax.rsqrt(jnp.mean(xc * xc, axis=-1, keepdims=True) + EPS)
    f['ln_xh'] = xc * f['ln_r']
    vln = f['ln_xh'] * r['ln_g'][...] + r['ln_b'][...]
    tril = lax.broadcasted_iota(jnp.int32, (SG_CHUNK, SG_CHUNK), 1) <= lax.broadcasted_iota(jnp.int32, (SG_CHUNK, SG_CHUNK), 0)
    f['wm'] = [_mxu(jnp.where(tril, r['sg_w'][g], 0.0)) for g in range(SG_GROUPS)]
    f['tril'] = tril
    grp = _lane_group(BR_WIDTH)
    bias = r['sg_bias'][...]
    parts = []
    for ci in range(ts // SG_CHUNK):
        vc = vln[ci * SG_CHUNK:(ci + 1) * SG_CHUNK]
        sc = bias
        for g in range(SG_GROUPS):
            sc = sc + jnp.where(grp == g, _dot(f['wm'][g], vc), 0.0)
        parts.append(sc)
    f['vln'] = vln
    f['sg_s'] = parts[0] if len(parts) == 1 else jnp.concatenate(parts, axis=0)
    f['u_act'] = u_act
    out_b = u_act * f['sg_s']
    first = (i > 0).astype(F32)
    cvx, cvc, cvb = act('cvx'), act('cvc'), act('cvb')
    f['cvx'], f['cvc'], f['cvb'] = cvx, cvc, cvb
    zh = jnp.concatenate([act('hx') * act('hc') * first, cvc * cvx], axis=0)
    f['z1'] = _shift_down(zh, 1)[HALO:]
    f['z2'] = _shift_down(zh, 2)[HALO:]
    f['z0'] = zh[HALO:]
    f['yv'] = r['conv_w'][0:1, :] * f['z2'] + r['conv_w'][1:2, :] * f['z1'] + r['conv_w'][2:3, :] * f['z0']
    out_c = cvb * f['yv']
    p = act('pool')
    ph = jnp.concatenate([act('hp') * first, p], axis=0)
    f['cnt'] = _pool_count(i, ts)
    f['pooled'] = _window_sums(ph, _shift_down)[HALO:] / f['cnt'] - p
    f['mixed'] = _dot(f['pooled'], r['wbd'][...])
    out_d = f['mixed'] * r['pool_scale'][...]
    f['outs'] = [r['o'][...], out_b, out_c, out_d]
    f['ys'] = [_dot(f['outs'][b], r['w_br'][b][...]) for b in range(N_BRANCH)]
    merged = f['gates'][:, 0:D_MODEL] * f['ys'][0]
    for b in range(1, N_BRANCH):
        merged = merged + f['gates'][:, b * D_MODEL:(b + 1) * D_MODEL] * f['ys'][b]
    f['merged'] = merged
    f['mo'] = _dot(merged, r['w_out'][...])
    return f


_MIX_TILE_INPUTS = ['gate', 'sgu', 'sgv', 'cvb', 'cvx', 'cvc', 'pool', 'hx', 'hc', 'hp', 'o']
_MIX_WEIGHTS = ['gate_b', 'ln_g', 'ln_b', 'sg_w', 'sg_bias', 'conv_w', 'wbd', 'pool_scale', 'w_br0', 'w_br1', 'w_br2',
                'w_br3', 'w_out', 'g_post']


def _mix_specs(s, ts):
    c0 = COL_M1 // BR_WIDTH
    prev = lambda col: pl.BlockSpec((HALO, BR_WIDTH), lambda i: (jnp.maximum(i * (ts // HALO) - 1, 0), col))
    tiles = [_rows(ts, N_BRANCH * D_MODEL, 0), _rows(ts, BR_WIDTH, c0), _rows(ts, BR_WIDTH, c0 + 1), _rows(ts, BR_WIDTH, c0 + 2),
             _rows(ts, BR_WIDTH, c0 + 3), _rows(ts, BR_WIDTH, c0 + 4), _rows(ts, BR_WIDTH, c0 + 5),
             prev(c0 + 3), prev(c0 + 4), prev(c0 + 5), _rows(ts, N_HEADS * HEAD_PAD)]
    weights = [_full((1, N_BRANCH * D_MODEL)), _full((1, BR_WIDTH)), _full((1, BR_WIDTH)),
               _full((SG_GROUPS, SG_CHUNK, SG_CHUNK)), _full((SG_CHUNK, BR_WIDTH)), _full((8, BR_WIDTH)),
               _resident((BR_WIDTH, BR_WIDTH)), _full((1, BR_WIDTH)), _resident((N_HEADS * HEAD_PAD, D_MODEL)),
               _resident((BR_WIDTH, D_MODEL)), _resident((BR_WIDTH, D_MODEL)), _resident((BR_WIDTH, D_MODEL)),
               _resident((D_MODEL, D_MODEL)), _full((1, D_MODEL))]
    return tiles, weights


def _mix_refs(refs):
    names = _MIX_TILE_INPUTS + _MIX_WEIGHTS
    r = dict(zip(names, refs[:len(names)]))
    r['w_br'] = [r['w_br0'], r['w_br1'], r['w_br2'], r['w_br3']]
    return r, refs[len(names):]


def _mix_operands(proj, o, lw):
    return ([proj] * 10 + [o] + [lw[n] for n in ['gate_b', 'sg_ln_g', 'sg_ln_b', 'sg_w', 'sg_bias', 'conv_w8', 'wbd',
                                                 'pool_scale', 'w_br_mla_p', 'w_br_sg', 'w_br_conv', 'w_br_pool', 'w_out',
                                                 'norm_mix_post']])


def _mix_fwd(x0, proj, o, lw):
    s = x0.shape[0]
    ts = _tile(s, 512)
    tiles, weights = _mix_specs(s, ts)

    def body(*refs):
        r, (x0_ref, x1_ref) = _mix_refs(refs)
        f = _mix_forward(pl.program_id(0), ts, r)
        x1_ref[...] = x0_ref[...] + _rms(f['mo'], r['g_post'][...])

    return pl.pallas_call(
        body, name="mix_fwd", grid=(s // ts,),
        in_specs=tiles + weights + [_rows(ts, D_MODEL)],
        out_specs=_rows(ts, D_MODEL),
        out_shape=jax.ShapeDtypeStruct((s, D_MODEL), F32),
        compiler_params=_params(("parallel",)),
    )(*_mix_operands(proj, o, lw), x0)


def _mix_bwd(dx1, proj, o, lw):
    s = dx1.shape[0]
    ts = _tile(s, 256)
    tiles, weights = _mix_specs(s, ts)
    hw = N_HEADS * HEAD_PAD

    def body(*refs):
        r, rest = _mix_refs(refs)
        (dx1_ref, dg_ref, dm1_ref, dyv_ref, up_ref, do_ref, delta_ref,
         dgate_b_ref, dln_g_ref, dln_b_ref, dsgw_ref, dsgb_ref, dconv_ref, dwbd_ref, dps_ref,
         dwbr0_ref, dwbr1_ref, dwbr2_ref, dwbr3_ref, dwout_ref, dgpost_ref, dbias_acc) = rest
        i = pl.program_id(0)
        acc_refs = [dgate_b_ref, dln_g_ref, dln_b_ref, dsgw_ref, dsgb_ref, dconv_ref, dwbd_ref, dps_ref,
                    dwbr0_ref, dwbr1_ref, dwbr2_ref, dwbr3_ref, dwout_ref, dgpost_ref, dbias_acc]

        @pl.when(i == 0)
        def _():
            for a in acc_refs:
                a[...] = jnp.zeros_like(a)

        f = _mix_forward(i, ts, r)
        dmo, dgpost = _rms_bwd(f['mo'], r['g_post'][...], dx1_ref[...])
        dgpost_ref[...] += dgpost
        dwout_ref[...] += _dot_tn(f['merged'], dmo)
        dmerged = _dot_nt(dmo, r['w_out'][...])
        dwbr = [dwbr0_ref, dwbr1_ref, dwbr2_ref, dwbr3_ref]
        douts = []
        for b in range(N_BRANCH):
            gb = f['gates'][:, b * D_MODEL:(b + 1) * D_MODEL]
            dgate = dmerged * f['ys'][b] * gb * (1.0 - gb)
            dg_ref[:, b * D_MODEL:(b + 1) * D_MODEL] = dgate.astype(dg_ref.dtype)
            dgate_b_ref[:, b * D_MODEL:(b + 1) * D_MODEL] += jnp.sum(dgate, axis=0, keepdims=True)
            dy = dmerged * gb
            dwbr[b][...] += _dot_tn(f['outs'][b], dy)
            douts.append(_dot_nt(dy, r['w_br'][b][...]))
        do = douts[0]
        do_ref[...] = do.astype(do_ref.dtype)
        prod = do * f['outs'][0]
        for h in range(N_HEADS):
            delta_ref[h] = jnp.sum(prod[:, h * HEAD_PAD:(h + 1) * HEAD_PAD], axis=1, keepdims=True)
        grp = _lane_group(BR_WIDTH)
        ds = douts[1] * f['u_act']
        dsgu = douts[1] * f['sg_s'] * _gelu_grad(f['sgu'], f['tu'])
        dvln_parts = []
        for ci in range(ts // SG_CHUNK):
            rows = slice(ci * SG_CHUNK, (ci + 1) * SG_CHUNK)
            ds_c, vln_c = ds[rows], f['vln'][rows]
            dvln_c = jnp.zeros((SG_CHUNK, BR_WIDTH), F32)
            for g in range(SG_GROUPS):
                dvln_c = dvln_c + jnp.where(grp == g, _dot_tn(f['wm'][g], ds_c), 0.0)
                dsgw_ref[g] += jnp.where(f['tril'], _dot_nt(jnp.where(grp == g, ds_c, 0.0), vln_c), 0.0)
            dbias_acc[...] += ds_c
            dvln_parts.append(dvln_c)
        dvln = dvln_parts[0] if len(dvln_parts) == 1 else jnp.concatenate(dvln_parts, axis=0)
        dln_g_ref[...] += jnp.sum(dvln * f['ln_xh'], axis=0, keepdims=True)
        dln_b_ref[...] += jnp.sum(dvln, axis=0, keepdims=True)
        dxh = dvln * r['ln_g'][...]
        dvg = f['ln_r'] * (dxh - jnp.mean(dxh, axis=-1, keepdims=True)
                           - f['ln_xh'] * jnp.mean(dxh * f['ln_xh'], axis=-1, keepdims=True))
        dsgv = dvg * _gelu_grad(f['sgv'], f['tv'])
        dcvb = douts[2] * f['yv']
        dyv = douts[2] * f['cvb']
        dyv_ref[...] = dyv
        for kk, zk in enumerate((f['z2'], f['z1'], f['z0'])):
            dconv_ref[kk:kk + 1, :] += jnp.sum(dyv * zk, axis=0, keepdims=True)
        dps_ref[...] += jnp.sum(douts[3] * f['mixed'], axis=0, keepdims=True)
        dmixed = douts[3] * r['pool_scale'][...]
        dwbd_ref[...] += _dot_tn(f['pooled'], dmixed)
        up_ref[...] = _dot_nt(dmixed, r['wbd'][...]) / f['cnt']
        dm1_ref[...] = jnp.concatenate([dsgu, dsgv, dcvb], axis=1).astype(dm1_ref.dtype)

        @pl.when(i == pl.num_programs(0) - 1)
        def _():
            lane = lax.broadcasted_iota(jnp.int32, (1, SG_CHUNK), 1)
            db = dbias_acc[...]
            out = jnp.zeros((SG_CHUNK, SG_CHUNK), F32)
            for g in range(SG_GROUPS):
                out = out + jnp.where(lane == g, jnp.sum(jnp.where(grp == g, db, 0.0), axis=1, keepdims=True), 0.0)
            dsgb_ref[...] = out

    acc = lambda shape: (_full(shape), jax.ShapeDtypeStruct(shape, F32))
    accs = [acc((1, N_BRANCH * D_MODEL)), acc((1, BR_WIDTH)), acc((1, BR_WIDTH)), acc((SG_GROUPS, SG_CHUNK, SG_CHUNK)),
            acc((SG_CHUNK, SG_CHUNK)), acc((8, BR_WIDTH)), acc((BR_WIDTH, BR_WIDTH)), acc((1, BR_WIDTH)),
            acc((hw, D_MODEL)), acc((BR_WIDTH, D_MODEL)), acc((BR_WIDTH, D_MODEL)), acc((BR_WIDTH, D_MODEL)),
            acc((D_MODEL, D_MODEL)), acc((1, D_MODEL))]
    tile_outs = [(_rows(ts, N_BRANCH * D_MODEL), jax.ShapeDtypeStruct((s, N_BRANCH * D_MODEL), MXU_DTYPE)),
                 (_rows(ts, 3 * BR_WIDTH), jax.ShapeDtypeStruct((s, 3 * BR_WIDTH), MXU_DTYPE)),
                 (_rows(ts, BR_WIDTH), jax.ShapeDtypeStruct((s, BR_WIDTH), F32)),
                 (_rows(ts, BR_WIDTH), jax.ShapeDtypeStruct((s, BR_WIDTH), F32)),
                 (_rows(ts, hw), jax.ShapeDtypeStruct((s, hw), MXU_DTYPE)),
                 (pl.BlockSpec((N_HEADS, ts, 1), lambda i: (0, i, 0)), jax.ShapeDtypeStruct((N_HEADS, s, 1), F32))]
    outs = tile_outs + accs
    return pl.pallas_call(
        body, name="mix_bwd", grid=(s // ts,),
        in_specs=tiles + weights + [_rows(ts, D_MODEL)],
        out_specs=[o_[0] for o_ in outs], out_shape=[o_[1] for o_ in outs],
        scratch_shapes=[pltpu.VMEM((SG_CHUNK, BR_WIDTH), F32)],
        compiler_params=_params(("arbitrary",), 60 * 1024 * 1024),
    )(*_mix_operands(proj, o, lw), dx1)


def _shift_bwd(dyv, upool, proj, conv_w8):
    s = dyv.shape[0]
    ts = _tile(s, 512)
    nb = s // HALO
    c0 = COL_M1 // BR_WIDTH

    def body(dyv_ref, dyvn_ref, up_ref, upn_ref, cvx_ref, cvc_ref, cw_ref, out_ref):
        i = pl.program_id(0)
        last = (i < pl.num_programs(0) - 1).astype(F32)
        dh = jnp.concatenate([dyv_ref[...], dyvn_ref[...] * last], axis=0)
        dz = (cw_ref[2:3, :] * dh + cw_ref[1:2, :] * _shift_up(dh, 1) + cw_ref[0:1, :] * _shift_up(dh, 2))[:ts]
        up = up_ref[...]
        uh = jnp.concatenate([up, upn_ref[...] * last], axis=0)
        dpool = _window_sums(uh, _shift_up)[:ts] - up * _pool_count(i, ts)
        out_ref[...] = jnp.concatenate([dz * cvc_ref[...].astype(F32), dz * cvx_ref[...].astype(F32), dpool],
                                       axis=1).astype(out_ref.dtype)

    nxt = pl.BlockSpec((HALO, BR_WIDTH), lambda i: (jnp.minimum((i + 1) * (ts // HALO), nb - 1), 0))
    return pl.pallas_call(
        body, name="shift_bwd", grid=(s // ts,),
        in_specs=[_rows(ts, BR_WIDTH), nxt, _rows(ts, BR_WIDTH), nxt, _rows(ts, BR_WIDTH, c0 + 3), _rows(ts, BR_WIDTH, c0 + 4),
                  _full((8, BR_WIDTH))],
        out_specs=_rows(ts, 3 * BR_WIDTH),
        out_shape=jax.ShapeDtypeStruct((s, 3 * BR_WIDTH), MXU_DTYPE),
        compiler_params=_params(("parallel",)),
    )(dyv, dyv, upool, upool, proj, proj, conv_w8)


def _ffn2(a, w2, x1, g):
    s = a.shape[0]
    ts = _tile(s, 512)

    def body(a_ref, w_ref, x1_ref, g_ref, x2_ref, f_ref):
        f = _dot(_relu_sq(a_ref[...]), w_ref[...])
        f_ref[...] = f
        x2_ref[...] = x1_ref[...] + _rms(f, g_ref[...])

    return pl.pallas_call(
        body, name="ffn2", grid=(s // ts,),
        in_specs=[_rows(ts, D_FF), _resident((D_FF, D_MODEL)), _rows(ts, D_MODEL), _full((1, D_MODEL))],
        out_specs=[_rows(ts, D_MODEL), _rows(ts, D_MODEL)],
        out_shape=[jax.ShapeDtypeStruct((s, D_MODEL), F32)] * 2,
        compiler_params=_params(("parallel",)),
    )(a, w2, x1, g)


def _ffn2_bwd(dx2, f, g, a, w2t, carried=None):
    s = a.shape[0]
    ts = _tile(s, 512)

    def body(dx2_ref, f_ref, g_ref, a_ref, w_ref, df_ref, da_ref, dg_ref):
        @pl.when(pl.program_id(0) == 0)
        def _():
            dg_ref[...] = jnp.zeros_like(dg_ref)

        df, dg = _rms_bwd(f_ref[...], g_ref[...], dx2_ref[...])
        dg_ref[...] += dg
        df_ref[...] = df.astype(df_ref.dtype)
        da_ref[...] = (_dot(df, w_ref[...]) * (2.0 * jnp.maximum(a_ref[...].astype(F32), 0.0))).astype(da_ref.dtype)

    return _call_carrying(
        body, carried, name="ffn2_bwd", grid=(s // ts,),
        in_specs=[_rows(ts, D_MODEL), _rows(ts, D_MODEL), _full((1, D_MODEL)), _rows(ts, D_FF), _resident((D_MODEL, D_FF))],
        out_specs=[_rows(ts, D_MODEL), _rows(ts, D_FF), _full((1, D_MODEL))],
        out_shape=[jax.ShapeDtypeStruct((s, D_MODEL), MXU_DTYPE), jax.ShapeDtypeStruct((s, D_FF), MXU_DTYPE),
                   jax.ShapeDtypeStruct((1, D_MODEL), F32)],
        operands=(dx2, f, g, a, w2t), semantics=("arbitrary",))


def _norm_in_bwd(name, pieces, x, g, dres):
    s = x.shape[0]
    ts = _tile(s, 512)
    n = len(pieces)

    def body(*refs):
        d_refs, w_refs = refs[:n], refs[n:2 * n]
        x_ref, g_ref, dres_ref, dx_ref, dg_ref = refs[2 * n:]

        @pl.when(pl.program_id(0) == 0)
        def _():
            dg_ref[...] = jnp.zeros_like(dg_ref)

        dh = _dot(d_refs[0][...], w_refs[0][...])
        for p in range(1, n):
            dh = dh + _dot(d_refs[p][...], w_refs[p][...])
        dx, dg = _rms_bwd(x_ref[...], g_ref[...], dh)
        dg_ref[...] += dg
        dx_ref[...] = dres_ref[...] + dx

    return pl.pallas_call(
        body, name=name, grid=(s // ts,),
        in_specs=[_rows(ts, d.shape[1]) for d, _ in pieces] + [_resident(w.shape) for _, w in pieces]
        + [_rows(ts, D_MODEL), _full((1, D_MODEL)), _rows(ts, D_MODEL)],
        out_specs=[_rows(ts, D_MODEL), _full((1, D_MODEL))],
        out_shape=[jax.ShapeDtypeStruct((s, D_MODEL), F32), jax.ShapeDtypeStruct((1, D_MODEL), F32)],
        compiler_params=_params(("arbitrary",)),
    )(*[d for d, _ in pieces], *[w for _, w in pieces], x, g, dres)


def _loss_and_grad(y, target):
    s = y.shape[0]
    ts = _tile(s, 512)

    def body(y_ref, t_ref, dy_ref, loss_ref):
        @pl.when(pl.program_id(0) == 0)
        def _():
            loss_ref[...] = jnp.zeros_like(loss_ref)

        err = y_ref[...] - t_ref[...]
        dy_ref[...] = err * (1.0 / D_MODEL)
        loss_ref[...] += 0.5 * jnp.sum(jnp.mean(err * err, axis=-1, keepdims=True), axis=0, keepdims=True)

    dy, loss = pl.pallas_call(
        body, name="loss", grid=(s // ts,),
        in_specs=[_rows(ts, D_MODEL), _rows(ts, D_MODEL)],
        out_specs=[_rows(ts, D_MODEL), _full((8, 128))],
        out_shape=[jax.ShapeDtypeStruct((s, D_MODEL), F32), jax.ShapeDtypeStruct((8, 128), F32)],
        compiler_params=_params(("arbitrary",)),
    )(y, target)
    return loss[0, 0], dy


_W_IN_SPLITS = [256, 384, 416, 672, 928, 1184, 1440, 1696, 1952]


def _rope_swap(w):
    half = QK_ROPE // 2
    return jnp.concatenate([-w[..., half:], w[..., :half]], axis=-1)


def _rope_unswap(d):
    half = QK_ROPE // 2
    return jnp.concatenate([d[..., half:], -d[..., :half]], axis=-1)


def _zeros_like_cols(w, n):
    return jnp.zeros(w.shape[:-1] + (n,), w.dtype)


def _derive_weights(w):
    md = MXU_DTYPE
    nl = w['w_in'].shape[0]
    c_q, c_kv, k_r, sg_u, sg_v, cv_x, cv_b, cv_c, pool, gate = jnp.split(w['w_in'].astype(md), _W_IN_SPLITS, axis=-1)
    pad_rope = lambda r: jnp.concatenate([_zeros_like_cols(r, QK_NOPE), r, _zeros_like_cols(r, HEAD_PAD - QK_NOPE - QK_ROPE)], -1)
    w_in_p = jnp.concatenate([gate, sg_u, sg_v, cv_b, cv_x, cv_c, pool, c_q, c_kv, pad_rope(k_r), pad_rope(_rope_swap(k_r))], -1)
    wq = w['w_uq'].astype(md).reshape(nl, Q_RANK, N_HEADS, QK_NOPE + QK_ROPE)
    nope, rope_w = wq[..., :QK_NOPE], wq[..., QK_NOPE:]
    wq_a = jnp.concatenate([nope, rope_w, _zeros_like_cols(nope, 32)], -1).reshape(nl, Q_RANK, N_HEADS * HEAD_PAD)
    wq_b = pad_rope(_rope_swap(rope_w)).reshape(nl, Q_RANK, N_HEADS * HEAD_PAD)
    wkv = w['w_ukv'].astype(md).reshape(nl, KV_RANK, N_HEADS, QK_NOPE + V_HEAD)
    pad_half = lambda r: jnp.concatenate([r, _zeros_like_cols(r, HEAD_PAD - r.shape[-1])], -1).reshape(nl, KV_RANK, N_HEADS * HEAD_PAD)
    w_br_mla = w['w_br_mla'].astype(md).reshape(nl, N_HEADS, V_HEAD, D_MODEL)
    w_br_mla_p = jnp.concatenate([w_br_mla, jnp.zeros_like(w_br_mla)], axis=2).reshape(nl, N_HEADS * HEAD_PAD, D_MODEL)
    eye = jnp.eye(4, dtype=md)
    wbd = (w['pool_w'].astype(md)[:, :, :, None, :] * eye[None, :, None, :, None]).reshape(nl, BR_WIDTH, BR_WIDTH)
    row = lambda a: a.astype(F32)[:, None, :]
    w_in_pt = jnp.swapaxes(w_in_p, 1, 2)
    return dict(
        w_in_p=w_in_p, wt_g=w_in_pt[:, COL_G:COL_M1], wt_m1=w_in_pt[:, COL_M1:COL_M2], wt_m2=w_in_pt[:, COL_M2:COL_B],
        wt_b=w_in_pt[:, COL_B:],
        wq=jnp.concatenate([wq_a, wq_b], -1), wkv=jnp.concatenate([pad_half(wkv[..., :QK_NOPE]), pad_half(wkv[..., QK_NOPE:])], -1),
        w_br_mla_p=w_br_mla_p, w_br_sg=w['w_br_sg'].astype(md), w_br_conv=w['w_br_conv'].astype(md),
        w_br_pool=w['w_br_pool'].astype(md), wbd=wbd, w_out=w['w_out'].astype(md),
        w_ff1=w['w_ff1'].astype(md), w_ff1t=jnp.swapaxes(w['w_ff1'].astype(md), 1, 2),
        w_ff2=w['w_ff2'].astype(md), w_ff2t=jnp.swapaxes(w['w_ff2'].astype(md), 1, 2),
        norm_mix_pre=row(w['norm_mix_pre']), gate_b=row(w['gate_b']), q_norm=row(w['q_norm']), kv_norm=row(w['kv_norm']),
        sg_ln_g=row(w['sg_ln_g']), sg_ln_b=row(w['sg_ln_b']), sg_w=w['sg_w'].astype(F32),
        sg_bias=jnp.repeat(jnp.swapaxes(w['sg_b'].astype(F32), 1, 2), BR_WIDTH // SG_GROUPS, axis=2),
        conv_w8=jnp.pad(w['conv_w'].astype(F32), ((0, 0), (0, 5), (0, 0))), pool_scale=row(w['pool_scale']),
        norm_mix_post=row(w['norm_mix_post']), norm_ffn_pre=row(w['norm_ffn_pre']), norm_ffn_post=row(w['norm_ffn_post']),
    )


def _rope_tables(positions):
    inv_freq = ROPE_BASE ** (-jnp.arange(0, QK_ROPE, 2, dtype=F32) / QK_ROPE)
    ang = positions.astype(F32)[:, None] * inv_freq
    cos, sin = jnp.cos(ang), jnp.sin(ang)
    n = positions.shape[0]
    ones, z64, z32 = jnp.ones((n, QK_NOPE), F32), jnp.zeros((n, QK_NOPE), F32), jnp.zeros((n, 32), F32)
    return (jnp.concatenate([ones, cos, cos, z32], 1), jnp.concatenate([z64, sin, sin, z32], 1),
            jnp.concatenate([z64, cos, cos, z32], 1))


def _reference_layout_grads(g):
    gate, dm1, dm2, dpb = g['dw_in_pieces']
    nl = gate.shape[0]
    sg_u, sg_v, cv_b = jnp.split(dm1, 3, axis=-1)
    cv_x, cv_c, pool = jnp.split(dm2, 3, axis=-1)
    c_q, c_kv, kr, krs = jnp.split(dpb, [Q_RANK, Q_RANK + KV_RANK, Q_RANK + KV_RANK + HEAD_PAD], axis=-1)
    rope_cols = slice(QK_NOPE, QK_NOPE + QK_ROPE)
    k_r = kr[..., rope_cols] + _rope_unswap(krs[..., rope_cols])
    w_in = jnp.concatenate([c_q, c_kv, k_r, sg_u, sg_v, cv_x, cv_b, cv_c, pool, gate], -1)
    hw = N_HEADS * HEAD_PAD
    dqa = g['dwq'][..., :hw].reshape(nl, Q_RANK, N_HEADS, HEAD_PAD)
    dqb = g['dwq'][..., hw:].reshape(nl, Q_RANK, N_HEADS, HEAD_PAD)
    w_uq = jnp.concatenate([dqa[..., :QK_NOPE], dqa[..., rope_cols] + _rope_unswap(dqb[..., rope_cols])], -1)
    dka = g['dwkv'][..., :hw].reshape(nl, KV_RANK, N_HEADS, HEAD_PAD)
    dva = g['dwkv'][..., hw:].reshape(nl, KV_RANK, N_HEADS, HEAD_PAD)
    w_ukv = jnp.concatenate([dka[..., :QK_NOPE], dva[..., :V_HEAD]], -1)
    w_br_mla = g['dw_br_mla_p'].reshape(nl, N_HEADS, HEAD_PAD, D_MODEL)[:, :, :V_HEAD]
    dwbd = g['dwbd'].reshape(nl, 4, 64, 4, 64)
    pool_w = jnp.stack([dwbd[:, k, :, k, :] for k in range(4)], axis=1)
    sq = lambda a: a[:, 0, :]
    return dict(
        norm_mix_pre=sq(g['dg_pre']), w_in=w_in, gate_b=sq(g['dgate_b']), q_norm=sq(g['dq_norm']),
        w_uq=w_uq.reshape(nl, Q_RANK, -1), kv_norm=sq(g['dkv_norm']), w_ukv=w_ukv.reshape(nl, KV_RANK, -1),
        w_br_mla=w_br_mla.reshape(nl, N_HEADS * V_HEAD, D_MODEL), sg_ln_g=sq(g['dln_g']), sg_ln_b=sq(g['dln_b']),
        sg_w=g['dsg_w'], sg_b=jnp.swapaxes(g['dsg_b'][:, :, :SG_GROUPS], 1, 2), w_br_sg=g['dw_br_sg'],
        conv_w=g['dconv_w'][:, :3], w_br_conv=g['dw_br_conv'], pool_w=pool_w, pool_scale=sq(g['dpool_scale']),
        w_br_pool=g['dw_br_pool'], w_out=g['dw_out'], norm_mix_post=sq(g['dg_post']), norm_ffn_pre=sq(g['dg_fpre']),
        w_ff1=g['dw_ff1'], w_ff2=g['dw_ff2'], norm_ffn_post=sq(g['dg_fpost']))


def _layer_forward(x0, lw, tabs, carried):
    proj = _mm("in_proj", x0, lw['w_in_p'], tm=1024, tn=N_PROJ, prologue=_rms, rows=(lw['norm_mix_pre'],))
    q, k, v = _qkv_prep(proj, lw['q_norm'], lw['kv_norm'], lw['wq'], lw['wkv'], *tabs)
    (o, lse), carried_out = _attn_fwd(q, k, v, carried)
    x1 = _mix_fwd(x0, proj, o, lw)
    a = _mm("ffn1", x1, lw['w_ff1'], tm=1024, tn=D_FF, prologue=_rms, rows=(lw['norm_ffn_pre'],))
    x2, f = _ffn2(a, lw['w_ff2'], x1, lw['norm_ffn_post'])
    return x2, dict(x0=x0, proj=proj, q=q, k=k, v=v, o=o, lse=lse, x1=x1, a=a, f=f), carried_out


def _layer_backward(dx2, lw, sv, tabs, early, late):
    g = {}
    (df, da, g['dg_fpost']), early_out = _ffn2_bwd(dx2, sv['f'], lw['norm_ffn_post'], sv['a'], lw['w_ff2t'], early)
    carried = late(early_out)
    g['dw_ff2'] = _mm_tn("dw_ff2", sv['a'], df, tm=512, tn=1024, prologue=_relu_sq)
    dx1, g['dg_fpre'] = _norm_in_bwd("ffn1_bwd", [(da, lw['w_ff1t'])], sv['x1'], lw['norm_ffn_pre'], dx2)
    g['dw_ff1'] = _mm_tn("dw_ff1", sv['x1'], da, tm=512, tn=2048, prologue=_rms, rows=(lw['norm_ffn_pre'],))
    (dgate, dm1, dyv, upool, do, delta, g['dgate_b'], g['dln_g'], g['dln_b'], g['dsg_w'], g['dsg_b'], g['dconv_w'],
     g['dwbd'], g['dpool_scale'], g['dw_br_mla_p'], g['dw_br_sg'], g['dw_br_conv'], g['dw_br_pool'], g['dw_out'],
     g['dg_post']) = _mix_bwd(dx1, sv['proj'], sv['o'], lw)
    dm2 = _shift_bwd(dyv, upool, sv['proj'], lw['conv_w8'])
    (dq, dk, dv), carried_out = _attn_bwd(sv['q'], sv['k'], sv['v'], do, sv['lse'], delta, carried)
    dpb, g['dwq'], g['dwkv'], g['dq_norm'], g['dkv_norm'] = _qkv_bwd(
        dq, dk, dv, sv['proj'], lw['q_norm'], lw['kv_norm'], lw['wq'], lw['wkv'], *tabs)
    pieces = [(dgate, lw['wt_g']), (dm1, lw['wt_m1']), (dm2, lw['wt_m2']), (dpb, lw['wt_b'])]
    dx0, g['dg_pre'] = _norm_in_bwd("in_proj_bwd", pieces, sv['x0'], lw['norm_mix_pre'], dx1)
    g['dw_in_pieces'] = [_mm_tn("dw_in_%d" % n, sv['x0'], d, tm=512, tn=2048, prologue=_rms, rows=(lw['norm_mix_pre'],))
                         for n, (d, _) in enumerate(pieces)]
    return dx0, g, carried_out


class _StepPlan(NamedTuple):
    n_layers: int
    weights_exchange: Callable
    weights_from: Callable
    grads_ready: Callable
    first_done: Callable
    second_done: Callable


def _local_step(x, positions, target, plan):
    tabs = _rope_tables(positions)
    derive = lambda w: {n: a[0] for n, a in _derive_weights(w).items()}
    first = plan.weights_exchange(0)
    weights = plan.weights_from(0, None if first is None else _run_exchange("gather_weights", first))
    derived, saved = [], []
    for l in range(plan.n_layers):
        derived.append(derive(weights))
        coming = plan.weights_exchange(l + 1) if l + 1 < plan.n_layers else None
        x, sv, arrived = _layer_forward(x, derived[l], tabs, coming)
        saved.append(sv)
        if l + 1 < plan.n_layers:
            weights = plan.weights_from(l + 1, arrived)
    loss, dx = _loss_and_grad(x, target)
    pending = None
    for l in reversed(range(plan.n_layers)):
        if pending is None:
            dx, g, _ = _layer_backward(dx, derived[l], saved[l], tabs, None, lambda _: None)
        else:
            dx, g, arrived = _layer_backward(dx, derived[l], saved[l], tabs, pending[1],
                                             functools.partial(plan.first_done, pending[0]))
            plan.second_done(pending[0], arrived)
        lead = lambda a: [b[None] for b in a] if isinstance(a, list) else a[None]
        going = plan.grads_ready(l, _reference_layout_grads({n: lead(a) for n, a in g.items()}))
        pending = None if going is None else (l, going)
    if pending is not None:
        second = plan.first_done(pending[0], _run_exchange("grads_to_sibling", pending[1]))
        plan.second_done(pending[0], _run_exchange("grads_to_chips", second))
    return loss, dx


def _relative_peers():
    x, y = lax.axis_index("x"), lax.axis_index("y")
    return {1: (x, 1 - y), 2: (1 - x, y), 3: (1 - x, 1 - y)}


def _for_my_core(fn):
    def run():
        for half in (0, 1):
            pl.when(lax.axis_index("c") == half)(functools.partial(fn, half))
    return run


def _gather_exchange(split, whole):
    ns, nw = len(split), len(whole)
    n = ns + nw

    def phases(ins, outs, sems):
        ici_send, ici_recv, d2d_send, d2d_recv, own_send, own_recv = sems
        x, y, c = lax.axis_index("x"), lax.axis_index("y"), lax.axis_index("c")
        peers = _relative_peers()

        def rows(ref, which):
            h = ref.shape[-2] // 2
            return ref.at[(slice(None),) * (len(ref.shape) - 2) + (slice(which * h, (which + 1) * h), slice(None))]

        def own(k):
            return pltpu.make_async_remote_copy(src_ref=ins[k], dst_ref=outs[k].at[0], send_sem=own_send.at[k],
                                                recv_sem=own_recv.at[k], device_id=(x, y, 1 - c), device_id_type=MESH)

        def over_ici(k, r, half):
            src = rows(ins[k], half) if k < ns else ins[k]
            dst = rows(outs[k].at[r], half) if k < ns else outs[k].at[r]
            return pltpu.make_async_remote_copy(src_ref=src, dst_ref=dst, send_sem=ici_send.at[3 * k + r - 1],
                                                recv_sem=ici_recv.at[3 * k + r - 1], device_id=(*peers[r], c), device_id_type=MESH)

        def to_sibling(k, r, half):
            landed = rows(outs[k].at[r], half)
            return pltpu.make_async_remote_copy(src_ref=landed, dst_ref=landed, send_sem=d2d_send.at[3 * k + r - 1],
                                                recv_sem=d2d_recv.at[3 * k + r - 1], device_id=(x, y, 1 - c), device_id_type=MESH)

        def start(half):
            for k in range(n):
                own(k).start()
                for r in peers:
                    over_ici(k, r, half).start()

        def middle(half):
            for k in range(n):
                for r in peers:
                    over_ici(k, r, half).wait_recv()
                    if k < ns:
                        to_sibling(k, r, half).start()

        def finish(half):
            for k in range(n):
                own(k).wait()
                for r in peers:
                    if k < ns:
                        to_sibling(k, r, 1 - half).wait_recv()
                        to_sibling(k, r, half).wait_send()
                    over_ici(k, r, half).wait_send()

        return _for_my_core(start), _for_my_core(middle), _for_my_core(finish)

    arrs = list(split) + list(whole)
    return _Exchange(
        operands=arrs, out_shape=[jax.ShapeDtypeStruct((4,) + a.shape, a.dtype) for a in arrs],
        scratch_shapes=[pltpu.SemaphoreType.DMA((3 * n,)), pltpu.SemaphoreType.DMA((3 * n,)), pltpu.SemaphoreType.DMA((3 * ns,)),
                        pltpu.SemaphoreType.DMA((3 * ns,)), pltpu.SemaphoreType.DMA((n,)), pltpu.SemaphoreType.DMA((n,))],
        phases=phases)


def _absolute_chip_order(relative):
    me = 2 * lax.axis_index("x") + lax.axis_index("y")
    return jnp.stack([lax.dynamic_index_in_dim(relative, jnp.bitwise_xor(me, chip), 0, keepdims=False) for chip in range(4)])


REDUCE_STEPS = 8


def _sibling_exchange(arrs):
    n = len(arrs)

    def phases(ins, theirs, sems):
        send_sems, recv_sems = sems
        x, y, c = lax.axis_index("x"), lax.axis_index("y"), lax.axis_index("c")

        def copy(k, my_half):
            h = ins[k].shape[1] // 2
            return pltpu.make_async_remote_copy(src_ref=ins[k].at[:, (1 - my_half) * h:(2 - my_half) * h, :], dst_ref=theirs[k],
                                                send_sem=send_sems.at[k], recv_sem=recv_sems.at[k],
                                                device_id=(x, y, 1 - c), device_id_type=MESH)

        def start(my_half):
            for k in range(n):
                copy(k, my_half).start()

        def finish(my_half):
            for k in range(n):
                copy(k, my_half).wait()

        return _for_my_core(start), lambda: None, _for_my_core(finish)

    return _Exchange(operands=list(arrs),
                     out_shape=[jax.ShapeDtypeStruct((a.shape[0], a.shape[1] // 2, a.shape[2]), a.dtype) for a in arrs],
                     scratch_shapes=[pltpu.SemaphoreType.DMA((n,)), pltpu.SemaphoreType.DMA((n,))], phases=phases)


def _add_sibling(name, arrs, theirs):
    n, steps = len(arrs), REDUCE_STEPS

    def body(*refs):
        for mine_ref, theirs_ref, out_ref in zip(refs[:n], refs[n:2 * n], refs[2 * n:]):
            out_ref[...] = (mine_ref[...] + theirs_ref[...]).astype(out_ref.dtype)

    block = lambda t: (4, t.shape[1] // steps, t.shape[2])
    return pl.pallas_call(
        body, name=name, grid=(steps,),
        in_specs=[pl.BlockSpec(block(t), lambda i: (0, lax.axis_index("c") * steps + i, 0)) for t in theirs]
        + [pl.BlockSpec(block(t), lambda i: (0, i, 0)) for t in theirs],
        out_specs=[pl.BlockSpec(block(t), lambda i: (0, i, 0)) for t in theirs],
        out_shape=[jax.ShapeDtypeStruct(t.shape, WIRE_DTYPE) for t in theirs],
        compiler_params=_params(("parallel",)))(*arrs, *theirs)


def _scatter_exchange(arrs):
    n = len(arrs)

    def phases(ins, outs, sems):
        send_sems, recv_sems = sems
        c = lax.axis_index("c")
        peers = _relative_peers()

        def copy(k, r):
            px, py = peers[r]
            return pltpu.make_async_remote_copy(src_ref=ins[k].at[2 * px + py], dst_ref=outs[k].at[r - 1],
                                                send_sem=send_sems.at[3 * k + r - 1], recv_sem=recv_sems.at[3 * k + r - 1],
                                                device_id=(px, py, c), device_id_type=MESH)

        def start():
            for k in range(n):
                for r in peers:
                    copy(k, r).start()

        def finish():
            for k in range(n):
                for r in peers:
                    copy(k, r).wait()

        return start, lambda: None, finish

    return _Exchange(operands=list(arrs), out_shape=[jax.ShapeDtypeStruct((3,) + a.shape[1:], a.dtype) for a in arrs],
                     scratch_shapes=[pltpu.SemaphoreType.DMA((3 * n,)), pltpu.SemaphoreType.DMA((3 * n,))], phases=phases)


def _sum_chips(name, chip_sums, arrived):
    n, steps = len(chip_sums), REDUCE_STEPS

    def body(*refs):
        for own_ref, arrived_ref, out_ref in zip(refs[:n], refs[n:2 * n], refs[2 * n:]):
            acc = own_ref[...].astype(F32)
            for r in range(3):
                acc = acc + arrived_ref[r].astype(F32)
            out_ref[...] = acc

    rows = lambda s: s.shape[1] // steps
    chip = lambda: 2 * lax.axis_index("x") + lax.axis_index("y")
    return pl.pallas_call(
        body, name=name, grid=(steps,),
        in_specs=[pl.BlockSpec((None, rows(s), s.shape[2]), lambda i: (chip(), i, 0)) for s in chip_sums]
        + [pl.BlockSpec((3, rows(s), s.shape[2]), lambda i: (0, i, 0)) for s in chip_sums],
        out_specs=[pl.BlockSpec((rows(s), s.shape[2]), lambda i: (lax.axis_index("c") * steps + i, 0)) for s in chip_sums],
        out_shape=[jax.ShapeDtypeStruct((2 * s.shape[1], s.shape[2]), F32) for s in chip_sums],
        compiler_params=_params(("parallel",)))(*chip_sums, *arrived)


def _join_siblings(name, bufs):
    n = len(bufs)

    def body(*refs):
        outs = refs[n:2 * n]
        send_sems, recv_sems = refs[2 * n:]
        x, y, c = lax.axis_index("x"), lax.axis_index("y"), lax.axis_index("c")

        def exchange(my_half):
            copies = []
            for k in range(n):
                h = outs[k].shape[0] // 2
                mine = outs[k].at[my_half * h:(my_half + 1) * h, :]
                theirs = outs[k].at[(1 - my_half) * h:(2 - my_half) * h, :]
                cp = pltpu.make_async_remote_copy(src_ref=mine, dst_ref=mine, send_sem=send_sems.at[k],
                                                  recv_sem=recv_sems.at[k], device_id=(x, y, 1 - c), device_id_type=MESH)
                cp.start()
                arrival = pltpu.make_async_remote_copy(src_ref=theirs, dst_ref=theirs, send_sem=send_sems.at[k],
                                                       recv_sem=recv_sems.at[k], device_id=(x, y, 1 - c), device_id_type=MESH)
                copies.append((cp, arrival))
            for cp, arrival in copies:
                arrival.wait_recv()
                cp.wait_send()

        for half in (0, 1):
            pl.when(c == half)(functools.partial(exchange, half))

    return pl.pallas_call(
        body, name=name, in_specs=[HBM] * n, out_specs=[HBM] * n,
        out_shape=[jax.ShapeDtypeStruct(b.shape, b.dtype) for b in bufs], input_output_aliases={k: k for k in range(n)},
        scratch_shapes=[pltpu.SemaphoreType.DMA((n,)), pltpu.SemaphoreType.DMA((n,))],
    )(*bufs)


def _gather_all_exchange(a):
    def phases(ins, outs, scratch):
        (a_ref,), (out_ref,) = ins, outs
        staging, send_sems, recv_sems, local_sem = scratch
        x, y, c = lax.axis_index("x"), lax.axis_index("y"), lax.axis_index("c")
        me = 4 * x + 2 * y + c
        flips = [(fx, fy, fc) for fx in (0, 1) for fy in (0, 1) for fc in (0, 1)][1:]
        peers = [(x ^ fx, y ^ fy, c ^ fc) for fx, fy, fc in flips]

        def copy(j):
            px, py, pc = peers[j]
            return pltpu.make_async_remote_copy(src_ref=a_ref, dst_ref=out_ref.at[me], send_sem=send_sems.at[j],
                                                recv_sem=recv_sems.at[j], device_id=(px, py, pc), device_id_type=MESH)

        def arrival(j):
            px, py, pc = peers[j]
            return pltpu.make_async_remote_copy(src_ref=a_ref, dst_ref=out_ref.at[4 * px + 2 * py + pc], send_sem=send_sems.at[j],
                                                recv_sem=recv_sems.at[j], device_id=(px, py, pc), device_id_type=MESH)

        own = pltpu.make_async_copy(staging, out_ref.at[me], local_sem)

        def start():
            load = pltpu.make_async_copy(a_ref, staging, local_sem)
            load.start()
            load.wait()
            own.start()
            for j in range(7):
                copy(j).start()

        def finish():
            for j in range(7):
                arrival(j).wait_recv()
            for j in range(7):
                copy(j).wait_send()
            own.wait()

        return start, lambda: None, finish

    return _Exchange(operands=[a], out_shape=[jax.ShapeDtypeStruct((8,) + a.shape, a.dtype)],
                     scratch_shapes=[pltpu.VMEM(a.shape, a.dtype), pltpu.SemaphoreType.DMA((7,)), pltpu.SemaphoreType.DMA((7,)),
                                     pltpu.SemaphoreType.DMA], phases=phases)


def _rowwise_call(name, fn, slots, out_shapes, steps, carried=None):
    n_in, n_out = [len(s) for s in slots], [len(o) for o in out_shapes]

    def spec(shape):
        if len(shape) == 3:
            return pl.BlockSpec((shape[0], shape[1] // steps, shape[2]), lambda i: (0, i, 0))
        return pl.BlockSpec((shape[0] // steps, shape[1]), lambda i: (i, 0))

    def body(*refs):
        ins, outs = refs[:sum(n_in)], refs[sum(n_in):]
        a = b = 0
        for k in range(len(slots)):
            for o_ref, val in zip(outs[b:b + n_out[k]], fn(*[r[...] for r in ins[a:a + n_in[k]]])):
                o_ref[...] = val
            a, b = a + n_in[k], b + n_out[k]

    flat_in = [arr for s in slots for arr in s]
    flat_out = [shp for o in out_shapes for shp in o]
    out, carried_out = _call_carrying(
        body, carried, name=name, grid=(steps,), in_specs=[spec(a.shape) for a in flat_in],
        out_specs=[spec(s) for s in flat_out], out_shape=[jax.ShapeDtypeStruct(s, F32) for s in flat_out],
        operands=flat_in, semantics=("parallel",))
    grouped, b = [], 0
    for k in range(len(slots)):
        grouped.append(out[b:b + n_out[k]])
        b += n_out[k]
    return grouped if carried is None else (grouped, carried_out)


def _sum_in_order(a):
    acc = a[0].astype(F32)
    for k in range(1, a.shape[0]):
        acc = acc + a[k].astype(F32)
    return (acc,)


def _adamw_math(w, g, m, v):
    m_new = ADAM_B1 * m + (1.0 - ADAM_B1) * g
    v_new = ADAM_B2 * v + (1.0 - ADAM_B2) * (g * g)
    m_hat = m_new / (1.0 - ADAM_B1 ** ADAM_STEP)
    v_hat = v_new / (1.0 - ADAM_B2 ** ADAM_STEP)
    return -ADAM_LR * (m_hat / (jnp.sqrt(v_hat) + ADAM_EPS) + ADAM_WD * w), m_new, v_new


SMALL_PACK_COLS = 256
SMALL_PACK_ROWS = 2048


def _pack_small(parts):
    wide = [jnp.pad(p, ((0, 0), (0, 0), (0, SMALL_PACK_COLS - p.shape[2]))) for p in parts]
    rows = jnp.concatenate(wide, axis=1)
    return jnp.pad(rows, ((0, 0), (0, SMALL_PACK_ROWS - rows.shape[1]), (0, 0)))


def _unpack_small(packed, shapes):
    out, row = [], 0
    for a, b in shapes:
        out.append(packed[:, row:row + a, :b])
        row += a
    return out


def _pack(arrs, rows_per_layer, dtype):
    nl = arrs[0].shape[0]
    flat = jnp.concatenate([a.astype(dtype).reshape(nl, -1) for a in arrs], axis=1)
    flat = jnp.pad(flat, ((0, 0), (0, rows_per_layer * PACK_COLS - flat.shape[1])))
    return flat.reshape(nl * rows_per_layer, PACK_COLS)


def _unpack(packed, shapes, rows_per_layer):
    nl = shapes[0][0]
    flat = packed.reshape(packed.shape[:-2] + (nl, rows_per_layer * PACK_COLS))
    out, off = [], 0
    for shp in shapes:
        size = math.prod(shp[1:])
        out.append(flat[..., off:off + size].reshape(packed.shape[:-2] + tuple(shp)))
        off += size
    return out


def _rows_needed(shapes, multiple):
    per_layer = sum(math.prod(s[1:]) for s in shapes)
    rows = -(-per_layer // PACK_COLS)
    return -(-rows // multiple) * multiple


CONV_TILE = (8, 128)


def _layer_shard_exchange(w, l):
    conv = w['conv_w'][l].reshape(-1)
    conv = jnp.pad(conv, (0, math.prod(CONV_TILE) - conv.shape[0])).reshape(CONV_TILE)
    return _gather_exchange([w[n][l].astype(MXU_DTYPE) for n in MATMUL_SHARDED], [conv])


def _layer_full_weights(w, l, gathered):
    gathered = [_absolute_chip_order(g) for g in gathered]
    full = {n: w[n][l:l + 1] for n in WEIGHTS}
    for n, part in zip(MATMUL_SHARDED, gathered):
        if n in ROW_SHARDED:
            full[n] = part.reshape(1, 4 * part.shape[1], part.shape[2])
        else:
            full[n] = jnp.swapaxes(part, 0, 1).reshape(1, part.shape[1], 4 * part.shape[2])
    rows, cols = w['conv_w'].shape[1:]
    conv = gathered[-1].reshape(4, -1)[:, :rows * cols].reshape(4, rows, cols)
    full['conv_w'] = jnp.swapaxes(conv, 0, 1).reshape(1, rows, 4 * cols)
    return full


def _chip_major(n, g):
    nl = g.shape[0]
    if n in ROW_SHARDED:
        return jnp.swapaxes(g.reshape(nl, 4, g.shape[1] // 4, g.shape[2]), 0, 1)
    return jnp.transpose(g.reshape(nl, g.shape[1], 4, g.shape[2] // 4), (2, 0, 1, 3))


def kernel(x, positions, norm_mix_pre, w_in, gate_b, q_norm, w_uq, kv_norm, w_ukv, w_br_mla, sg_ln_g, sg_ln_b, sg_w, sg_b, w_br_sg, conv_w, w_br_conv, pool_w, pool_scale, w_br_pool, w_out, norm_mix_post, norm_ffn_pre, w_ff1, w_ff2, norm_ffn_post, loss_target, m_norm_mix_pre, m_w_in, m_gate_b, m_q_norm, m_w_uq, m_kv_norm, m_w_ukv, m_w_br_mla, m_sg_ln_g, m_sg_ln_b, m_sg_w, m_sg_b, m_w_br_sg, m_conv_w, m_w_br_conv, m_pool_w, m_pool_scale, m_w_br_pool, m_w_out, m_norm_mix_post, m_norm_ffn_pre, m_w_ff1, m_w_ff2, m_norm_ffn_post, v_norm_mix_pre, v_w_in, v_gate_b, v_q_norm, v_w_uq, v_kv_norm, v_w_ukv, v_w_br_mla, v_sg_ln_g, v_sg_ln_b, v_sg_w, v_sg_b, v_w_br_sg, v_conv_w, v_w_br_conv, v_pool_w, v_pool_scale, v_w_br_pool, v_w_out, v_norm_mix_post, v_norm_ffn_pre, v_w_ff1, v_w_ff2, v_norm_ffn_post):
    given = dict(locals())
    w = {n: given[n] for n in WEIGHTS}
    mom = {n: given['m_' + n] for n in WEIGHTS}
    var = {n: given['v_' + n] for n in WEIGHTS}
    nl = w['w_in'].shape[0]
    sharded, chip_sums, reduced, local_small = {}, {}, {}, {}

    def grads_ready(l, g):
        sharded[l] = [_chip_major(n, g[n])[:, 0] for n in BIG_SHARDED]
        sharded[l].append(_pack_small([_chip_major(n, g[n])[:, 0] for n in SMALL_SHARDED]))
        local_small[l] = [g[n] for n in REPLICATED + ['conv_w']]
        return _sibling_exchange(sharded[l])

    def sibling_done(l, theirs):
        chip_sums[l] = _add_sibling("add_sibling", sharded[l], theirs)
        return _scatter_exchange(chip_sums[l])

    def chips_done(l, arrived):
        reduced[l] = _join_siblings("join_halves", _sum_chips("sum_chips", chip_sums[l], arrived))

    plan = _StepPlan(n_layers=nl, weights_exchange=functools.partial(_layer_shard_exchange, w),
                     weights_from=functools.partial(_layer_full_weights, w), grads_ready=grads_ready,
                     first_done=sibling_done, second_done=chips_done)
    loss, dx = _local_step(x[0], positions[0], loss_target[0], plan)
    loss = lax.psum(loss, ("x", "y", "c"))

    grad, delta, new_m, new_v = {}, {}, {}, {}
    names = REPLICATED + ['conv_w']
    local = [jnp.concatenate([local_small[l][k] for l in range(nl)]) for k in range(len(names))]
    rows = _rows_needed([a.shape for a in local], 32)
    small_grads = _gather_all_exchange(_pack(local, rows, F32))
    for k, n in enumerate(BIG_SHARDED):
        grad[n] = jnp.stack([reduced[l][k] for l in range(nl)])
    slots = [[w[n], grad[n], mom[n], var[n]] for n in BIG_SHARDED]
    small_pack = lambda d: _pack_small([d[n] for n in SMALL_SHARDED])
    g_small = jnp.stack([reduced[l][-1] for l in range(nl)])
    slots.append([small_pack(w), g_small, small_pack(mom), small_pack(var)])
    updated, (everyone,) = _rowwise_call("adamw_sharded", _adamw_math, slots, [[s_[0].shape] * 3 for s_ in slots], 32,
                                         carried=small_grads)
    for k, n in enumerate(BIG_SHARDED):
        delta[n], new_m[n], new_v[n] = updated[k]
    for d, packed in zip((grad, delta, new_m, new_v), [g_small] + list(updated[-1])):
        d.update(zip(SMALL_SHARDED, _unpack_small(packed, [w[n].shape[1:] for n in SMALL_SHARDED])))

    (summed,), = _rowwise_call("sum_devices", _sum_in_order, [[everyone]], [[everyone.shape[1:]]], 4)
    g_rep = _unpack(summed, [a.shape for a in local], rows)
    chip = 2 * lax.axis_index("x") + lax.axis_index("y")
    g_rep[-1] = lax.dynamic_slice_in_dim(g_rep[-1], chip * w['conv_w'].shape[2], w['conv_w'].shape[2], axis=2)
    rep_pack = lambda arrs: _pack(arrs, rows, F32)
    (rep_out,) = _rowwise_call("adamw_replicated", _adamw_math,
                               [[rep_pack([w[n] for n in names]), rep_pack(g_rep), rep_pack([mom[n] for n in names]),
                                 rep_pack([var[n] for n in names])]], [[(nl * rows, PACK_COLS)] * 3], 4)
    grad.update(zip(names, g_rep))
    for d, packed in zip((delta, new_m, new_v), rep_out):
        d.update(zip(names, _unpack(packed, [w[n].shape for n in names], rows)))

    return (loss, dx[None], *[grad[n] for n in WEIGHTS], *[delta[n] for n in WEIGHTS], *[new_m[n] for n in WEIGHTS],
            *[new_v[n] for n in WEIGHTS])
```

```python
import functools
import math
from typing import Any, Callable, NamedTuple, Sequence

import jax
import jax.numpy as jnp
from jax import lax
from jax.experimental import pallas as pl
from jax.experimental.pallas import tpu as pltpu

F32 = jnp.float32
MXU_DTYPE = jnp.bfloat16
WIRE_DTYPE = jnp.bfloat16
ACT_DTYPE = jnp.bfloat16
MESH = pl.DeviceIdType.MESH

D_MODEL = 1024
D_FF = 4096
N_HEADS = 4
QK_NOPE = 64
QK_ROPE = 32
V_HEAD = 64
HEAD_PAD = 128
Q_RANK = 256
KV_RANK = 128
SG_CHUNK = 128
SG_GROUPS = 4
BR_WIDTH = 256
N_BRANCH = 4
POOL_WINDOWS = (2, 4, 8, 16)
HALO = 16
ROPE_BASE = 10000.0
EPS = 1e-6
ATTN_SCALE = (QK_NOPE + QK_ROPE) ** -0.5
LOG2_E = math.log2(math.e)
FWD_HEADS_PER_STEP = 4
HEADS_PER_STEP = 2
N_PROJ = N_BRANCH * D_MODEL + 6 * BR_WIDTH + Q_RANK + KV_RANK + 2 * HEAD_PAD
COL_G, COL_M1, COL_M2, COL_B = 0, 4096, 4864, 5632

ADAM_LR, ADAM_B1, ADAM_B2, ADAM_EPS, ADAM_WD, ADAM_STEP = 0.001, 0.9, 0.999, 1e-08, 0.01, 10

VMEM_LIMIT = 56 * 1024 * 1024

WEIGHTS = ['norm_mix_pre', 'w_in', 'gate_b', 'q_norm', 'w_uq', 'kv_norm', 'w_ukv', 'w_br_mla', 'sg_ln_g', 'sg_ln_b',
           'sg_w', 'sg_b', 'w_br_sg', 'conv_w', 'w_br_conv', 'pool_w', 'pool_scale', 'w_br_pool', 'w_out',
           'norm_mix_post', 'norm_ffn_pre', 'w_ff1', 'w_ff2', 'norm_ffn_post']
COL_SHARDED = ['w_in', 'w_uq', 'w_ukv', 'w_br_mla', 'w_br_sg', 'w_br_conv', 'w_br_pool', 'w_ff1']
ROW_SHARDED = ['w_out', 'w_ff2']
MATMUL_SHARDED = ['w_in', 'w_uq', 'w_ukv', 'w_br_mla', 'w_br_sg', 'w_br_conv', 'w_br_pool', 'w_out', 'w_ff1', 'w_ff2']
BIG_SHARDED = ['w_in', 'w_ff1', 'w_ff2', 'w_out']
SMALL_SHARDED = ['w_uq', 'w_ukv', 'w_br_mla', 'w_br_sg', 'w_br_conv', 'w_br_pool']
SHARDED = MATMUL_SHARDED + ['conv_w']
REPLICATED = [n for n in WEIGHTS if n not in SHARDED]
PACK_COLS = 1024


def _params(sem, vmem=VMEM_LIMIT):
    return pltpu.CompilerParams(dimension_semantics=sem, vmem_limit_bytes=vmem)


def _mxu(a):
    return a.astype(MXU_DTYPE)


def _dot(a, b):
    return jnp.dot(_mxu(a), _mxu(b), preferred_element_type=F32)


def _dot_nt(a, b):
    return lax.dot_general(_mxu(a), _mxu(b), (((1,), (1,)), ((), ())), preferred_element_type=F32)


def _dot_tn(a, b):
    return lax.dot_general(_mxu(a), _mxu(b), (((0,), (0,)), ((), ())), preferred_element_type=F32)


def _rms(x, g):
    r = lax.rsqrt(jnp.mean(x * x, axis=-1, keepdims=True) + EPS)
    return x * r * g


def _rms_bwd(x, g, dy):
    r = lax.rsqrt(jnp.mean(x * x, axis=-1, keepdims=True) + EPS)
    xh = x * r
    gdy = dy * g
    dx = r * (gdy - xh * jnp.mean(gdy * xh, axis=-1, keepdims=True))
    return dx, jnp.sum(dy * xh, axis=0, keepdims=True)


_GELU_C = math.sqrt(2.0 / math.pi)


def _gelu(x):
    t = jnp.tanh(_GELU_C * (x + 0.044715 * (x * x * x)))
    return x * (0.5 * (1.0 + t)), t


def _gelu_grad(x, t):
    return 0.5 * (1.0 + t) + 0.5 * x * (1.0 - t * t) * (_GELU_C * (1.0 + 3.0 * 0.044715 * x * x))


def _sigmoid(x):
    return 1.0 / (1.0 + jnp.exp(-x))


def _full(shape):
    return pl.BlockSpec(shape, lambda *_: (0,) * len(shape))


def _resident(shape):
    return pl.BlockSpec(shape, lambda *_: (0,) * len(shape), pipeline_mode=pl.Buffered(1))


def _rows(ts, width, col=0):
    return pl.BlockSpec((ts, width), lambda i: (i, col))


def _tile(n, pref):
    return min(n, pref)


def _mm(name, a, w, *, tm, tn, prologue=None, rows=()):
    m, k = a.shape
    n = w.shape[1]
    tm, tn = _tile(m, tm), _tile(n, tn)

    def body(a_ref, *rest):
        row_refs, w_ref, o_ref = rest[:len(rows)], rest[len(rows)], rest[len(rows) + 1]
        av = a_ref[...]
        if prologue is not None:
            av = prologue(av, *[r[...] for r in row_refs])
        o_ref[...] = _dot(av, w_ref[...]).astype(o_ref.dtype)

    return pl.pallas_call(
        body, name=name, grid=(m // tm, n // tn),
        in_specs=[pl.BlockSpec((tm, k), lambda i, j: (i, 0))] + [pl.BlockSpec((1, k), lambda i, j: (0, 0)) for _ in rows]
        + [_resident((k, n)) if tn == n else pl.BlockSpec((k, tn), lambda i, j: (0, j))],
        out_specs=pl.BlockSpec((tm, tn), lambda i, j: (i, j)),
        out_shape=jax.ShapeDtypeStruct((m, n), ACT_DTYPE),
        compiler_params=_params(("parallel", "parallel")),
    )(a, *rows, w)


def _mm_tn(name, a, b, *, tm, tn, prologue=None, rows=()):
    m, k = a.shape
    n = b.shape[1]
    tm, tn = _tile(m, tm), _tile(n, tn)

    def body(a_ref, *rest):
        row_refs, b_ref, o_ref = rest[:len(rows)], rest[len(rows)], rest[len(rows) + 1]

        @pl.when(pl.program_id(1) == 0)
        def _():
            o_ref[...] = jnp.zeros_like(o_ref)

        av = a_ref[...]
        if prologue is not None:
            av = prologue(av, *[r[...] for r in row_refs])
        o_ref[...] += _dot_tn(av, b_ref[...])

    return pl.pallas_call(
        body, name=name, grid=(n // tn, m // tm),
        in_specs=[pl.BlockSpec((tm, k), lambda j, i: (i, 0))] + [pl.BlockSpec((1, k), lambda j, i: (0, 0)) for _ in rows]
        + [pl.BlockSpec((tm, tn), lambda j, i: (i, j))],
        out_specs=pl.BlockSpec((k, tn), lambda j, i: (0, j)),
        out_shape=jax.ShapeDtypeStruct((k, n), F32),
        compiler_params=_params(("parallel", "arbitrary")),
    )(a, *rows, b)


def _relu_sq(a):
    r = jnp.maximum(a.astype(F32), 0.0)
    return r * r


HBM = pl.BlockSpec(memory_space=pl.ANY)


class _Exchange(NamedTuple):
    operands: Sequence[Any]
    out_shape: Sequence[Any]
    scratch_shapes: Sequence[Any]
    phases: Callable


def _run_exchange(name, ex):
    n_in, n_out = len(ex.operands), len(ex.out_shape)

    def body(*refs):
        for phase in ex.phases(refs[:n_in], refs[n_in:n_in + n_out], refs[n_in + n_out:]):
            phase()

    return pl.pallas_call(body, name=name, in_specs=[HBM] * n_in, out_specs=[HBM] * n_out, out_shape=list(ex.out_shape),
                          scratch_shapes=list(ex.scratch_shapes))(*ex.operands)


def _call_carrying(body, carried, *, name, grid, in_specs, out_specs, out_shape, operands, semantics, middle_step=None):
    if carried is None:
        return pl.pallas_call(body, name=name, grid=grid, in_specs=in_specs, out_specs=out_specs, out_shape=out_shape,
                              compiler_params=_params(semantics))(*operands), None
    n_main_in, n_main_out = len(operands), len(out_shape)
    n_in, n_out = len(carried.operands), len(carried.out_shape)
    steps = math.prod(grid)

    def wrapped(*refs):
        main_in, refs = refs[:n_main_in], refs[n_main_in:]
        ex_in, refs = refs[:n_in], refs[n_in:]
        main_out, refs = refs[:n_main_out], refs[n_main_out:]
        ex_out, sems = refs[:n_out], refs[n_out:]
        step = pl.program_id(0)
        for axis in range(1, len(grid)):
            step = step * grid[axis] + pl.program_id(axis)
        start, middle, finish = carried.phases(ex_in, ex_out, sems)
        pl.when(step == 0)(start)
        pl.when(step == ((steps - 1) // 2 if middle_step is None else middle_step))(middle)
        body(*main_in, *main_out)
        pl.when(step == steps - 1)(finish)

    out = pl.pallas_call(
        wrapped, name=name + "_carrying", grid=grid, in_specs=list(in_specs) + [HBM] * n_in,
        out_specs=list(out_specs) + [HBM] * n_out, out_shape=list(out_shape) + list(carried.out_shape),
        scratch_shapes=list(carried.scratch_shapes), compiler_params=_params(("arbitrary",) * len(grid)))(*operands, *carried.operands)
    return out[:n_main_out], out[n_main_out:]


def _qkv_prep(proj, q_norm, kv_norm, wq, wkv, cq_tab, s_tab, cr_tab):
    s = proj.shape[0]
    ts = _tile(s, 512)
    hw = N_HEADS * HEAD_PAD

    def body(cq_ref, ckv_ref, kr_ref, krs_ref, gq_ref, gkv_ref, wq_ref, wkv_ref, ct_ref, st_ref, crt_ref,
             q_ref, k_ref, v_ref):
        ct, st, crt = ct_ref[...], st_ref[...], crt_ref[...]
        qn = _rms(cq_ref[...].astype(F32), gq_ref[...])
        qab = _dot(qn, wq_ref[...])
        kvn = _rms(ckv_ref[...].astype(F32), gkv_ref[...])
        kav = _dot(kvn, wkv_ref[...])
        k_rope = kr_ref[...].astype(F32) * crt + krs_ref[...].astype(F32) * st
        ones_lane = (lax.broadcasted_iota(jnp.int32, (1, HEAD_PAD), 1) == V_HEAD).astype(F32)
        for h in range(N_HEADS):
            lo = h * HEAD_PAD
            q_ref[h] = (qab[:, lo:lo + HEAD_PAD] * ct + qab[:, hw + lo:hw + lo + HEAD_PAD] * st).astype(q_ref.dtype)
            k_ref[h] = (kav[:, lo:lo + HEAD_PAD] + k_rope).astype(k_ref.dtype)
            v_ref[h] = (kav[:, hw + lo:hw + lo + HEAD_PAD] + ones_lane).astype(v_ref.dtype)

    head_spec = pl.BlockSpec((N_HEADS, ts, HEAD_PAD), lambda i: (0, i, 0))
    head_shape = jax.ShapeDtypeStruct((N_HEADS, s, HEAD_PAD), MXU_DTYPE)
    return pl.pallas_call(
        body, name="qkv_prep", grid=(s // ts,),
        in_specs=[_rows(ts, Q_RANK, COL_B // Q_RANK), _rows(ts, KV_RANK, (COL_B + Q_RANK) // KV_RANK),
                  _rows(ts, HEAD_PAD, (COL_B + Q_RANK + KV_RANK) // HEAD_PAD),
                  _rows(ts, HEAD_PAD, (COL_B + Q_RANK + KV_RANK + HEAD_PAD) // HEAD_PAD),
                  _full((1, Q_RANK)), _full((1, KV_RANK)), _full((Q_RANK, 2 * hw)), _full((KV_RANK, 2 * hw)),
                  _rows(ts, HEAD_PAD), _rows(ts, HEAD_PAD), _rows(ts, HEAD_PAD)],
        out_specs=[head_spec, head_spec, head_spec],
        out_shape=[head_shape, head_shape, head_shape],
        compiler_params=_params(("parallel",)),
    )(proj, proj, proj, proj, q_norm, kv_norm, wq, wkv, cq_tab, s_tab, cr_tab)


def _diagonal_mask(t, keys_on_rows=False):
    key_axis = 0 if keys_on_rows else 1
    return lax.broadcasted_iota(jnp.int32, (t, t), key_axis) <= lax.broadcasted_iota(jnp.int32, (t, t), 1 - key_axis)


MIDDLE_WORK_SHARE = 0.65


def _causal_work_step(nq, groups):
    total = groups * nq * (nq + 1) // 2
    done = 0
    for step in range(groups * nq):
        if done >= MIDDLE_WORK_SHARE * total:
            return step
        done += step % nq + 1
    return groups * nq - 1


def _attn_fwd(q, k, v, carried=None):
    s = q.shape[1]
    t = _tile(s, 1024)

    def body(q_ref, k_ref, v_ref, o_ref, lse_ref):
        i = pl.program_id(1)
        lane = lax.broadcasted_iota(jnp.int32, (1, HEAD_PAD), 1)

        def step(j, carry, on_diagonal):
            rows = pl.ds(pl.multiple_of(j * t, t), t)
            out = []
            for h in range(FWD_HEADS_PER_STEP):
                m, acc = carry[h]
                sc = _dot_nt(q_ref[h], k_ref[h, rows, :]) * (ATTN_SCALE * LOG2_E)
                if on_diagonal:
                    sc = jnp.where(_diagonal_mask(t), sc, -jnp.inf)
                m_new = jnp.maximum(m, jnp.max(sc, axis=1, keepdims=True))
                out.append((m_new, jnp.exp2(m - m_new) * acc + _dot(jnp.exp2(sc - m_new), v_ref[h, rows, :])))
            return tuple(out)

        init = ((jnp.full((t, 1), -jnp.inf, F32), jnp.zeros((t, HEAD_PAD), F32)),) * FWD_HEADS_PER_STEP
        below = lax.fori_loop(0, i, functools.partial(step, on_diagonal=False), init)
        for h, (m, acc) in enumerate(step(i, below, True)):
            l = jnp.sum(jnp.where(lane == V_HEAD, acc, 0.0), axis=1, keepdims=True)
            o_ref[:, h * HEAD_PAD:(h + 1) * HEAD_PAD] = jnp.where(lane < V_HEAD, acc / l, 0.0)
            lse_ref[h] = m + jnp.log2(l)

    group = FWD_HEADS_PER_STEP
    return _call_carrying(
        body, carried, name="attn_fwd", grid=(N_HEADS // group, s // t),
        in_specs=[pl.BlockSpec((group, t, HEAD_PAD), lambda h, i: (h, i, 0)),
                  pl.BlockSpec((group, s, HEAD_PAD), lambda h, i: (h, 0, 0)),
                  pl.BlockSpec((group, s, HEAD_PAD), lambda h, i: (h, 0, 0))],
        out_specs=[pl.BlockSpec((t, group * HEAD_PAD), lambda h, i: (i, h)), pl.BlockSpec((group, t, 1), lambda h, i: (h, i, 0))],
        out_shape=[jax.ShapeDtypeStruct((s, N_HEADS * HEAD_PAD), F32), jax.ShapeDtypeStruct((N_HEADS, s, 1), F32)],
        operands=(q, k, v), semantics=("parallel", "parallel"), middle_step=_causal_work_step(s // t, N_HEADS // group))


def _attn_bwd(q, k, v, do, lse, delta, carried=None):
    s = q.shape[1]
    t = _tile(s, 1024)
    nq = s // t

    def body(q_ref, do_ref, lse_ref, dl_ref, k_ref, v_ref, dq_ref, dk_ref, dv_ref):
        j = pl.program_id(1)

        @pl.when(j == 0)
        def _():
            dq_ref[...] = jnp.zeros_like(dq_ref)

        def step(i, carry, on_diagonal):
            rows = pl.ds(pl.multiple_of(i * t, t), t)
            out = []
            for h in range(HEADS_PER_STEP):
                dk, dv = carry[h]
                qi, doi = q_ref[h, rows, :], do_ref[rows, h * HEAD_PAD:(h + 1) * HEAD_PAD]
                sc = _dot_nt(k_ref[h], qi) * (ATTN_SCALE * LOG2_E)
                if on_diagonal:
                    sc = jnp.where(_diagonal_mask(t, keys_on_rows=True), sc, -jnp.inf)
                p = jnp.exp2(sc - lse_ref[h, i])
                dv = dv + _dot(p, doi)
                ds = p * (_dot_nt(v_ref[h], doi) - dl_ref[h, i])
                dk = dk + _dot(ds, qi)
                dq_ref[h, rows, :] += _dot_tn(ds, k_ref[h]) * ATTN_SCALE
                out.append((dk, dv))
            return tuple(out)

        zero = ((jnp.zeros((t, HEAD_PAD), F32),) * 2,) * HEADS_PER_STEP
        sums = lax.fori_loop(j + 1, nq, functools.partial(step, on_diagonal=False), step(j, zero, True))
        for h, (dk, dv) in enumerate(sums):
            dk_ref[h] = dk * ATTN_SCALE
            dv_ref[h] = dv

    group = HEADS_PER_STEP
    whole = lambda w: pl.BlockSpec((group, s, w), lambda h, j: (h, 0, 0), pipeline_mode=pl.Buffered(1))
    tile = pl.BlockSpec((group, t, HEAD_PAD), lambda h, j: (h, j, 0))
    per_query = pl.BlockSpec((group, nq, 1, t), lambda h, j: (h, 0, 0, 0))
    head_shape = jax.ShapeDtypeStruct((N_HEADS, s, HEAD_PAD), F32)
    return _call_carrying(
        body, carried, name="attn_bwd", grid=(N_HEADS // group, nq),
        in_specs=[whole(HEAD_PAD), pl.BlockSpec((s, group * HEAD_PAD), lambda h, j: (0, h), pipeline_mode=pl.Buffered(1)),
                  per_query, per_query, tile, tile],
        out_specs=[whole(HEAD_PAD), tile, tile],
        out_shape=[head_shape, head_shape, head_shape],
        operands=(q, do, lse.reshape(N_HEADS, nq, 1, t), delta.reshape(N_HEADS, nq, 1, t), k, v),
        semantics=("parallel", "arbitrary"))


def _qkv_bwd(dq, dk, dv, proj, q_norm, kv_norm, wq, wkv, cq_tab, s_tab, cr_tab):
    s = proj.shape[0]
    ts = _tile(s, 512)
    hw = N_HEADS * HEAD_PAD

    def body(dq_ref, dk_ref, dv_ref, cq_ref, ckv_ref, gq_ref, gkv_ref, wq_ref, wkv_ref, ct_ref, st_ref, crt_ref,
             dpb_ref, dwq_ref, dwkv_ref, dgq_ref, dgkv_ref):
        @pl.when(pl.program_id(0) == 0)
        def _():
            for r in (dwq_ref, dwkv_ref, dgq_ref, dgkv_ref):
                r[...] = jnp.zeros_like(r)

        ct, st, crt = ct_ref[...], st_ref[...], crt_ref[...]
        dqs = [dq_ref[h] for h in range(N_HEADS)]
        dks = [dk_ref[h] for h in range(N_HEADS)]
        dqab = jnp.concatenate([d * ct for d in dqs] + [d * st for d in dqs], axis=1)
        dkav = jnp.concatenate(dks + [dv_ref[h] for h in range(N_HEADS)], axis=1)
        dk_sum = dks[0] + dks[1] + dks[2] + dks[3]
        cq, ckv, gq, gkv = cq_ref[...].astype(F32), ckv_ref[...].astype(F32), gq_ref[...], gkv_ref[...]
        dwq_ref[...] += _dot_tn(_rms(cq, gq), dqab)
        dwkv_ref[...] += _dot_tn(_rms(ckv, gkv), dkav)
        dcq, dgq = _rms_bwd(cq, gq, _dot_nt(dqab, wq_ref[...]))
        dckv, dgkv = _rms_bwd(ckv, gkv, _dot_nt(dkav, wkv_ref[...]))
        dgq_ref[...] += dgq
        dgkv_ref[...] += dgkv
        dpb_ref[...] = jnp.concatenate([dcq, dckv, dk_sum * crt, dk_sum * st], axis=1).astype(dpb_ref.dtype)

    head_spec = pl.BlockSpec((N_HEADS, ts, HEAD_PAD), lambda i: (0, i, 0))
    wb = Q_RANK + KV_RANK + 2 * HEAD_PAD
    return pl.pallas_call(
        body, name="qkv_bwd", grid=(s // ts,),
        in_specs=[head_spec, head_spec, head_spec,
                  _rows(ts, Q_RANK, COL_B // Q_RANK), _rows(ts, KV_RANK, (COL_B + Q_RANK) // KV_RANK),
                  _full((1, Q_RANK)), _full((1, KV_RANK)), _full((Q_RANK, 2 * hw)), _full((KV_RANK, 2 * hw)),
                  _rows(ts, HEAD_PAD), _rows(ts, HEAD_PAD), _rows(ts, HEAD_PAD)],
        out_specs=[_rows(ts, wb), _full((Q_RANK, 2 * hw)), _full((KV_RANK, 2 * hw)), _full((1, Q_RANK)), _full((1, KV_RANK))],
        out_shape=[jax.ShapeDtypeStruct((s, wb), MXU_DTYPE), jax.ShapeDtypeStruct((Q_RANK, 2 * hw), F32),
                   jax.ShapeDtypeStruct((KV_RANK, 2 * hw), F32), jax.ShapeDtypeStruct((1, Q_RANK), F32),
                   jax.ShapeDtypeStruct((1, KV_RANK), F32)],
        compiler_params=_params(("arbitrary",)),
    )(dq, dk, dv, proj, proj, q_norm, kv_norm, wq, wkv, cq_tab, s_tab, cr_tab)


def _lane_group(width):
    return lax.broadcasted_iota(jnp.int32, (1, width), 1) // (width // 4)


def _shift_down(a, k):
    return pltpu.roll(a, k, 0)


def _shift_up(a, k):
    return pltpu.roll(a, a.shape[0] - k, 0)


def _window_sums(xh, shift):
    s2 = xh + shift(xh, 1)
    s4 = s2 + shift(s2, 2)
    s8 = s4 + shift(s4, 4)
    s16 = s8 + shift(s8, 8)
    grp = _lane_group(xh.shape[1])
    return jnp.where(grp == 0, s2, jnp.where(grp == 1, s4, jnp.where(grp == 2, s8, s16)))


def _pool_count(i, ts):
    grp = _lane_group(BR_WIDTH)
    win = jnp.where(grp == 0, 2.0, jnp.where(grp == 1, 4.0, jnp.where(grp == 2, 8.0, 16.0)))
    t = (i * ts + lax.broadcasted_iota(jnp.int32, (ts, 1), 0)).astype(F32)
    return jnp.minimum(t + 1.0, win)


def _mix_forward(i, ts, r):
    f = {}
    act = lambda name: r[name][...].astype(F32)
    f['gates'] = _sigmoid(act('gate') + r['gate_b'][...])
    sgu, sgv = act('sgu'), act('sgv')
    f['sgu'], f['sgv'] = sgu, sgv
    u_act, f['tu'] = _gelu(sgu)
    vg, f['tv'] = _gelu(sgv)
    mu = jnp.mean(vg, axis=-1, keepdims=True)
    xc = vg - mu
    f['ln_r'] = lax.rsqrt(jnp.mean(xc * xc, axis=-1, keepdims=True) + EPS)
    f['ln_xh'] = xc * f['ln_r']
    vln = f['ln_xh'] * r['ln_g'][...] + r['ln_b'][...]
    tril = lax.broadcasted_iota(jnp.int32, (SG_CHUNK, SG_CHUNK), 1) <= lax.broadcasted_iota(jnp.int32, (SG_CHUNK, SG_CHUNK), 0)
    f['wm'] = [_mxu(jnp.where(tril, r['sg_w'][g], 0.0)) for g in range(SG_GROUPS)]
    f['tril'] = tril
    grp = _lane_group(BR_WIDTH)
    bias = r['sg_bias'][...]
    parts = []
    for ci in range(ts // SG_CHUNK):
        vc = vln[ci * SG_CHUNK:(ci + 1) * SG_CHUNK]
        sc = bias
        for g in range(SG_GROUPS):
            sc = sc + jnp.where(grp == g, _dot(f['wm'][g], vc), 0.0)
        parts.append(sc)
    f['vln'] = vln
    f['sg_s'] = parts[0] if len(parts) == 1 else jnp.concatenate(parts, axis=0)
    f['u_act'] = u_act
    out_b = u_act * f['sg_s']
    first = (i > 0).astype(F32)
    cvx, cvc, cvb = act('cvx'), act('cvc'), act('cvb')
    f['cvx'], f['cvc'], f['cvb'] = cvx, cvc, cvb
    zh = jnp.concatenate([act('hx') * act('hc') * first, cvc * cvx], axis=0)
    f['z1'] = _shift_down(zh, 1)[HALO:]
    f['z2'] = _shift_down(zh, 2)[HALO:]
    f['z0'] = zh[HALO:]
    f['yv'] = r['conv_w'][0:1, :] * f['z2'] + r['conv_w'][1:2, :] * f['z1'] + r['conv_w'][2:3, :] * f['z0']
    out_c = cvb * f['yv']
    p = act('pool')
    ph = jnp.concatenate([act('hp') * first, p], axis=0)
    f['cnt'] = _pool_count(i, ts)
    f['pooled'] = _window_sums(ph, _shift_down)[HALO:] / f['cnt'] - p
    f['mixed'] = _dot(f['pooled'], r['wbd'][...])
    out_d = f['mixed'] * r['pool_scale'][...]
    f['outs'] = [r['o'][...], out_b, out_c, out_d]
    f['ys'] = [_dot(f['outs'][b], r['w_br'][b][...]) for b in range(N_BRANCH)]
    merged = f['gates'][:, 0:D_MODEL] * f['ys'][0]
    for b in range(1, N_BRANCH):
        merged = merged + f['gates'][:, b * D_MODEL:(b + 1) * D_MODEL] * f['ys'][b]
    f['merged'] = merged
    f['mo'] = _dot(merged, r['w_out'][...])
    return f


_MIX_TILE_INPUTS = ['gate', 'sgu', 'sgv', 'cvb', 'cvx', 'cvc', 'pool', 'hx', 'hc', 'hp', 'o']
_MIX_WEIGHTS = ['gate_b', 'ln_g', 'ln_b', 'sg_w', 'sg_bias', 'conv_w', 'wbd', 'pool_scale', 'w_br0', 'w_br1', 'w_br2',
                'w_br3', 'w_out', 'g_post']


def _mix_specs(s, ts):
    c0 = COL_M1 // BR_WIDTH
    prev = lambda col: pl.BlockSpec((HALO, BR_WIDTH), lambda i: (jnp.maximum(i * (ts // HALO) - 1, 0), col))
    tiles = [_rows(ts, N_BRANCH * D_MODEL, 0), _rows(ts, BR_WIDTH, c0), _rows(ts, BR_WIDTH, c0 + 1), _rows(ts, BR_WIDTH, c0 + 2),
             _rows(ts, BR_WIDTH, c0 + 3), _rows(ts, BR_WIDTH, c0 + 4), _rows(ts, BR_WIDTH, c0 + 5),
             prev(c0 + 3), prev(c0 + 4), prev(c0 + 5), _rows(ts, N_HEADS * HEAD_PAD)]
    weights = [_full((1, N_BRANCH * D_MODEL)), _full((1, BR_WIDTH)), _full((1, BR_WIDTH)),
               _full((SG_GROUPS, SG_CHUNK, SG_CHUNK)), _full((SG_CHUNK, BR_WIDTH)), _full((8, BR_WIDTH)),
               _resident((BR_WIDTH, BR_WIDTH)), _full((1, BR_WIDTH)), _resident((N_HEADS * HEAD_PAD, D_MODEL)),
               _resident((BR_WIDTH, D_MODEL)), _resident((BR_WIDTH, D_MODEL)), _resident((BR_WIDTH, D_MODEL)),
               _resident((D_MODEL, D_MODEL)), _full((1, D_MODEL))]
    return tiles, weights


def _mix_refs(refs):
    names = _MIX_TILE_INPUTS + _MIX_WEIGHTS
    r = dict(zip(names, refs[:len(names)]))
    r['w_br'] = [r['w_br0'], r['w_br1'], r['w_br2'], r['w_br3']]
    return r, refs[len(names):]


def _mix_operands(proj, o, lw):
    return ([proj] * 10 + [o] + [lw[n] for n in ['gate_b', 'sg_ln_g', 'sg_ln_b', 'sg_w', 'sg_bias', 'conv_w8', 'wbd',
                                                 'pool_scale', 'w_br_mla_p', 'w_br_sg', 'w_br_conv', 'w_br_pool', 'w_out',
                                                 'norm_mix_post']])


def _mix_fwd(x0, proj, o, lw):
    s = x0.shape[0]
    ts = _tile(s, 512)
    tiles, weights = _mix_specs(s, ts)

    def body(*refs):
        r, (x0_ref, x1_ref) = _mix_refs(refs)
        f = _mix_forward(pl.program_id(0), ts, r)
        x1_ref[...] = x0_ref[...] + _rms(f['mo'], r['g_post'][...])

    return pl.pallas_call(
        body, name="mix_fwd", grid=(s // ts,),
        in_specs=tiles + weights + [_rows(ts, D_MODEL)],
        out_specs=_rows(ts, D_MODEL),
        out_shape=jax.ShapeDtypeStruct((s, D_MODEL), F32),
        compiler_params=_params(("parallel",)),
    )(*_mix_operands(proj, o, lw), x0)


def _mix_bwd(dx1, proj, o, lw):
    s = dx1.shape[0]
    ts = _tile(s, 256)
    tiles, weights = _mix_specs(s, ts)
    hw = N_HEADS * HEAD_PAD

    def body(*refs):
        r, rest = _mix_refs(refs)
        (dx1_ref, dg_ref, dm1_ref, dyv_ref, up_ref, do_ref, delta_ref,
         dgate_b_ref, dln_g_ref, dln_b_ref, dsgw_ref, dsgb_ref, dconv_ref, dwbd_ref, dps_ref,
         dwbr0_ref, dwbr1_ref, dwbr2_ref, dwbr3_ref, dwout_ref, dgpost_ref, dbias_acc) = rest
        i = pl.program_id(0)
        acc_refs = [dgate_b_ref, dln_g_ref, dln_b_ref, dsgw_ref, dsgb_ref, dconv_ref, dwbd_ref, dps_ref,
                    dwbr0_ref, dwbr1_ref, dwbr2_ref, dwbr3_ref, dwout_ref, dgpost_ref, dbias_acc]

        @pl.when(i == 0)
        def _():
            for a in acc_refs:
                a[...] = jnp.zeros_like(a)

        f = _mix_forward(i, ts, r)
        dmo, dgpost = _rms_bwd(f['mo'], r['g_post'][...], dx1_ref[...])
        dgpost_ref[...] += dgpost
        dwout_ref[...] += _dot_tn(f['merged'], dmo)
        dmerged = _dot_nt(dmo, r['w_out'][...])
        dwbr = [dwbr0_ref, dwbr1_ref, dwbr2_ref, dwbr3_ref]
        douts = []
        for b in range(N_BRANCH):
            gb = f['gates'][:, b * D_MODEL:(b + 1) * D_MODEL]
            dgate = dmerged * f['ys'][b] * gb * (1.0 - gb)
            dg_ref[:, b * D_MODEL:(b + 1) * D_MODEL] = dgate.astype(dg_ref.dtype)
            dgate_b_ref[:, b * D_MODEL:(b + 1) * D_MODEL] += jnp.sum(dgate, axis=0, keepdims=True)
            dy = dmerged * gb
            dwbr[b][...] += _dot_tn(f['outs'][b], dy)
            douts.append(_dot_nt(dy, r['w_br'][b][...]))
        do = douts[0]
        do_ref[...] = do.astype(do_ref.dtype)
        prod = do * f['outs'][0]
        for h in range(N_HEADS):
            delta_ref[h] = jnp.sum(prod[:, h * HEAD_PAD:(h + 1) * HEAD_PAD], axis=1, keepdims=True)
        grp = _lane_group(BR_WIDTH)
        ds = douts[1] * f['u_act']
        dsgu = douts[1] * f['sg_s'] * _gelu_grad(f['sgu'], f['tu'])
        dvln_parts = []
        for ci in range(ts // SG_CHUNK):
            rows = slice(ci * SG_CHUNK, (ci + 1) * SG_CHUNK)
            ds_c, vln_c = ds[rows], f['vln'][rows]
            dvln_c = jnp.zeros((SG_CHUNK, BR_WIDTH), F32)
            for g in range(SG_GROUPS):
                dvln_c = dvln_c + jnp.where(grp == g, _dot_tn(f['wm'][g], ds_c), 0.0)
                dsgw_ref[g] += jnp.where(f['tril'], _dot_nt(jnp.where(grp == g, ds_c, 0.0), vln_c), 0.0)
            dbias_acc[...] += ds_c
            dvln_parts.append(dvln_c)
        dvln = dvln_parts[0] if len(dvln_parts) == 1 else jnp.concatenate(dvln_parts, axis=0)
        dln_g_ref[...] += jnp.sum(dvln * f['ln_xh'], axis=0, keepdims=True)
        dln_b_ref[...] += jnp.sum(dvln, axis=0, keepdims=True)
        dxh = dvln * r['ln_g'][...]
        dvg = f['ln_r'] * (dxh - jnp.mean(dxh, axis=-1, keepdims=True)
                           - f['ln_xh'] * jnp.mean(dxh * f['ln_xh'], axis=-1, keepdims=True))
        dsgv = dvg * _gelu_grad(f['sgv'], f['tv'])
        dcvb = douts[2] * f['yv']
        dyv = douts[2] * f['cvb']
        dyv_ref[...] = dyv
        for kk, zk in enumerate((f['z2'], f['z1'], f['z0'])):
            dconv_ref[kk:kk + 1, :] += jnp.sum(dyv * zk, axis=0, keepdims=True)
        dps_ref[...] += jnp.sum(douts[3] * f['mixed'], axis=0, keepdims=True)
        dmixed = douts[3] * r['pool_scale'][...]
        dwbd_ref[...] += _dot_tn(f['pooled'], dmixed)
        up_ref[...] = _dot_nt(dmixed, r['wbd'][...]) / f['cnt']
        dm1_ref[...] = jnp.concatenate([dsgu, dsgv, dcvb], axis=1).astype(dm1_ref.dtype)

        @pl.when(i == pl.num_programs(0) - 1)
        def _():
            lane = lax.broadcasted_iota(jnp.int32, (1, SG_CHUNK), 1)
            db = dbias_acc[...]
            out = jnp.zeros((SG_CHUNK, SG_CHUNK), F32)
            for g in range(SG_GROUPS):
                out = out + jnp.where(lane == g, jnp.sum(jnp.where(grp == g, db, 0.0), axis=1, keepdims=True), 0.0)
            dsgb_ref[...] = out

    acc = lambda shape: (_full(shape), jax.ShapeDtypeStruct(shape, F32))
    accs = [acc((1, N_BRANCH * D_MODEL)), acc((1, BR_WIDTH)), acc((1, BR_WIDTH)), acc((SG_GROUPS, SG_CHUNK, SG_CHUNK)),
            acc((SG_CHUNK, SG_CHUNK)), acc((8, BR_WIDTH)), acc((BR_WIDTH, BR_WIDTH)), acc((1, BR_WIDTH)),
            acc((hw, D_MODEL)), acc((BR_WIDTH, D_MODEL)), acc((BR_WIDTH, D_MODEL)), acc((BR_WIDTH, D_MODEL)),
            acc((D_MODEL, D_MODEL)), acc((1, D_MODEL))]
    tile_outs = [(_rows(ts, N_BRANCH * D_MODEL), jax.ShapeDtypeStruct((s, N_BRANCH * D_MODEL), MXU_DTYPE)),
                 (_rows(ts, 3 * BR_WIDTH), jax.ShapeDtypeStruct((s, 3 * BR_WIDTH), MXU_DTYPE)),
                 (_rows(ts, BR_WIDTH), jax.ShapeDtypeStruct((s, BR_WIDTH), F32)),
                 (_rows(ts, BR_WIDTH), jax.ShapeDtypeStruct((s, BR_WIDTH), F32)),
                 (_rows(ts, hw), jax.ShapeDtypeStruct((s, hw), MXU_DTYPE)),
                 (pl.BlockSpec((N_HEADS, ts, 1), lambda i: (0, i, 0)), jax.ShapeDtypeStruct((N_HEADS, s, 1), F32))]
    outs = tile_outs + accs
    return pl.pallas_call(
        body, name="mix_bwd", grid=(s // ts,),
        in_specs=tiles + weights + [_rows(ts, D_MODEL)],
        out_specs=[o_[0] for o_ in outs], out_shape=[o_[1] for o_ in outs],
        scratch_shapes=[pltpu.VMEM((SG_CHUNK, BR_WIDTH), F32)],
        compiler_params=_params(("arbitrary",), 60 * 1024 * 1024),
    )(*_mix_operands(proj, o, lw), dx1)


def _shift_bwd(dyv, upool, proj, conv_w8):
    s = dyv.shape[0]
    ts = _tile(s, 512)
    nb = s // HALO
    c0 = COL_M1 // BR_WIDTH

    def body(dyv_ref, dyvn_ref, up_ref, upn_ref, cvx_ref, cvc_ref, cw_ref, out_ref):
        i = pl.program_id(0)
        last = (i < pl.num_programs(0) - 1).astype(F32)
        dh = jnp.concatenate([dyv_ref[...], dyvn_ref[...] * last], axis=0)
        dz = (cw_ref[2:3, :] * dh + cw_ref[1:2, :] * _shift_up(dh, 1) + cw_ref[0:1, :] * _shift_up(dh, 2))[:ts]
        up = up_ref[...]
        uh = jnp.concatenate([up, upn_ref[...] * last], axis=0)
        dpool = _window_sums(uh, _shift_up)[:ts] - up * _pool_count(i, ts)
        out_ref[...] = jnp.concatenate([dz * cvc_ref[...].astype(F32), dz * cvx_ref[...].astype(F32), dpool],
                                       axis=1).astype(out_ref.dtype)

    nxt = pl.BlockSpec((HALO, BR_WIDTH), lambda i: (jnp.minimum((i + 1) * (ts // HALO), nb - 1), 0))
    return pl.pallas_call(
        body, name="shift_bwd", grid=(s // ts,),
        in_specs=[_rows(ts, BR_WIDTH), nxt, _rows(ts, BR_WIDTH), nxt, _rows(ts, BR_WIDTH, c0 + 3), _rows(ts, BR_WIDTH, c0 + 4),
                  _full((8, BR_WIDTH))],
        out_specs=_rows(ts, 3 * BR_WIDTH),
        out_shape=jax.ShapeDtypeStruct((s, 3 * BR_WIDTH), MXU_DTYPE),
        compiler_params=_params(("parallel",)),
    )(dyv, dyv, upool, upool, proj, proj, conv_w8)


def _ffn2(a, w2, x1, g):
    s = a.shape[0]
    ts = _tile(s, 512)

    def body(a_ref, w_ref, x1_ref, g_ref, x2_ref, f_ref):
        f = _dot(_relu_sq(a_ref[...]), w_ref[...])
        f_ref[...] = f
        x2_ref[...] = x1_ref[...] + _rms(f, g_ref[...])

    return pl.pallas_call(
        body, name="ffn2", grid=(s // ts,),
        in_specs=[_rows(ts, D_FF), _resident((D_FF, D_MODEL)), _rows(ts, D_MODEL), _full((1, D_MODEL))],
        out_specs=[_rows(ts, D_MODEL), _rows(ts, D_MODEL)],
        out_shape=[jax.ShapeDtypeStruct((s, D_MODEL), F32)] * 2,
        compiler_params=_params(("parallel",)),
    )(a, w2, x1, g)


def _ffn2_bwd(dx2, f, g, a, w2t, carried=None):
    s = a.shape[0]
    ts = _tile(s, 512)

    def body(dx2_ref, f_ref, g_ref, a_ref, w_ref, df_ref, da_ref, dg_ref):
        @pl.when(pl.program_id(0) == 0)
        def _():
            dg_ref[...] = jnp.zeros_like(dg_ref)

        df, dg = _rms_bwd(f_ref[...], g_ref[...], dx2_ref[...])
        dg_ref[...] += dg
        df_ref[...] = df.astype(df_ref.dtype)
        da_ref[...] = (_dot(df, w_ref[...]) * (2.0 * jnp.maximum(a_ref[...].astype(F32), 0.0))).astype(da_ref.dtype)

    return _call_carrying(
        body, carried, name="ffn2_bwd", grid=(s // ts,),
        in_specs=[_rows(ts, D_MODEL), _rows(ts, D_MODEL), _full((1, D_MODEL)), _rows(ts, D_FF), _resident((D_MODEL, D_FF))],
        out_specs=[_rows(ts, D_MODEL), _rows(ts, D_FF), _full((1, D_MODEL))],
        out_shape=[jax.ShapeDtypeStruct((s, D_MODEL), MXU_DTYPE), jax.ShapeDtypeStruct((s, D_FF), MXU_DTYPE),
                   jax.ShapeDtypeStruct((1, D_MODEL), F32)],
        operands=(dx2, f, g, a, w2t), semantics=("arbitrary",))


def _norm_in_bwd(name, pieces, x, g, dres):
    s = x.shape[0]
    ts = _tile(s, 512)
    n = len(pieces)

    def body(*refs):
        d_refs, w_refs = refs[:n], refs[n:2 * n]
        x_ref, g_ref, dres_ref, dx_ref, dg_ref = refs[2 * n:]

        @pl.when(pl.program_id(0) == 0)
        def _():
            dg_ref[...] = jnp.zeros_like(dg_ref)

        dh = _dot(d_refs[0][...], w_refs[0][...])
        for p in range(1, n):
            dh = dh + _dot(d_refs[p][...], w_refs[p][...])
        dx, dg = _rms_bwd(x_ref[...], g_ref[...], dh)
        dg_ref[...] += dg
        dx_ref[...] = dres_ref[...] + dx

    return pl.pallas_call(
        body, name=name, grid=(s // ts,),
        in_specs=[_rows(ts, d.shape[1]) for d, _ in pieces] + [_resident(w.shape) for _, w in pieces]
        + [_rows(ts, D_MODEL), _full((1, D_MODEL)), _rows(ts, D_MODEL)],
        out_specs=[_rows(ts, D_MODEL), _full((1, D_MODEL))],
        out_shape=[jax.ShapeDtypeStruct((s, D_MODEL), F32), jax.ShapeDtypeStruct((1, D_MODEL), F32)],
        compiler_params=_params(("arbitrary",)),
    )(*[d for d, _ in pieces], *[w for _, w in pieces], x, g, dres)


def _loss_and_grad(y, target):
    s = y.shape[0]
    ts = _tile(s, 512)

    def body(y_ref, t_ref, dy_ref, loss_ref):
        @pl.when(pl.program_id(0) == 0)
        def _():
            loss_ref[...] = jnp.zeros_like(loss_ref)

        err = y_ref[...] - t_ref[...]
        dy_ref[...] = err * (1.0 / D_MODEL)
        loss_ref[...] += 0.5 * jnp.sum(jnp.mean(err * err, axis=-1, keepdims=True), axis=0, keepdims=True)

    dy, loss = pl.pallas_call(
        body, name="loss", grid=(s // ts,),
        in_specs=[_rows(ts, D_MODEL), _rows(ts, D_MODEL)],
        out_specs=[_rows(ts, D_MODEL), _full((8, 128))],
        out_shape=[jax.ShapeDtypeStruct((s, D_MODEL), F32), jax.ShapeDtypeStruct((8, 128), F32)],
        compiler_params=_params(("arbitrary",)),
    )(y, target)
    return loss[0, 0], dy


_W_IN_SPLITS = [256, 384, 416, 672, 928, 1184, 1440, 1696, 1952]


def _rope_swap(w):
    half = QK_ROPE // 2
    return jnp.concatenate([-w[..., half:], w[..., :half]], axis=-1)


def _rope_unswap(d):
    half = QK_ROPE // 2
    return jnp.concatenate([d[..., half:], -d[..., :half]], axis=-1)


def _zeros_like_cols(w, n):
    return jnp.zeros(w.shape[:-1] + (n,), w.dtype)


def _derive_weights(w):
    md = MXU_DTYPE
    nl = w['w_in'].shape[0]
    c_q, c_kv, k_r, sg_u, sg_v, cv_x, cv_b, cv_c, pool, gate = jnp.split(w['w_in'].astype(md), _W_IN_SPLITS, axis=-1)
    pad_rope = lambda r: jnp.concatenate([_zeros_like_cols(r, QK_NOPE), r, _zeros_like_cols(r, HEAD_PAD - QK_NOPE - QK_ROPE)], -1)
    w_in_p = jnp.concatenate([gate, sg_u, sg_v, cv_b, cv_x, cv_c, pool, c_q, c_kv, pad_rope(k_r), pad_rope(_rope_swap(k_r))], -1)
    wq = w['w_uq'].astype(md).reshape(nl, Q_RANK, N_HEADS, QK_NOPE + QK_ROPE)
    nope, rope_w = wq[..., :QK_NOPE], wq[..., QK_NOPE:]
    wq_a = jnp.concatenate([nope, rope_w, _zeros_like_cols(nope, 32)], -1).reshape(nl, Q_RANK, N_HEADS * HEAD_PAD)
    wq_b = pad_rope(_rope_swap(rope_w)).reshape(nl, Q_RANK, N_HEADS * HEAD_PAD)
    wkv = w['w_ukv'].astype(md).reshape(nl, KV_RANK, N_HEADS, QK_NOPE + V_HEAD)
    pad_half = lambda r: jnp.concatenate([r, _zeros_like_cols(r, HEAD_PAD - r.shape[-1])], -1).reshape(nl, KV_RANK, N_HEADS * HEAD_PAD)
    w_br_mla = w['w_br_mla'].astype(md).reshape(nl, N_HEADS, V_HEAD, D_MODEL)
    w_br_mla_p = jnp.concatenate([w_br_mla, jnp.zeros_like(w_br_mla)], axis=2).reshape(nl, N_HEADS * HEAD_PAD, D_MODEL)
    eye = jnp.eye(4, dtype=md)
    wbd = (w['pool_w'].astype(md)[:, :, :, None, :] * eye[None, :, None, :, None]).reshape(nl, BR_WIDTH, BR_WIDTH)
    row = lambda a: a.astype(F32)[:, None, :]
    w_in_pt = jnp.swapaxes(w_in_p, 1, 2)
    return dict(
        w_in_p=w_in_p, wt_g=w_in_pt[:, COL_G:COL_M1], wt_m1=w_in_pt[:, COL_M1:COL_M2], wt_m2=w_in_pt[:, COL_M2:COL_B],
        wt_b=w_in_pt[:, COL_B:],
        wq=jnp.concatenate([wq_a, wq_b], -1), wkv=jnp.concatenate([pad_half(wkv[..., :QK_NOPE]), pad_half(wkv[..., QK_NOPE:])], -1),
        w_br_mla_p=w_br_mla_p, w_br_sg=w['w_br_sg'].astype(md), w_br_conv=w['w_br_conv'].astype(md),
        w_br_pool=w['w_br_pool'].astype(md), wbd=wbd, w_out=w['w_out'].astype(md),
        w_ff1=w['w_ff1'].astype(md), w_ff1t=jnp.swapaxes(w['w_ff1'].astype(md), 1, 2),
        w_ff2=w['w_ff2'].astype(md), w_ff2t=jnp.swapaxes(w['w_ff2'].astype(md), 1, 2),
        norm_mix_pre=row(w['norm_mix_pre']), gate_b=row(w['gate_b']), q_norm=row(w['q_norm']), kv_norm=row(w['kv_norm']),
        sg_ln_g=row(w['sg_ln_g']), sg_ln_b=row(w['sg_ln_b']), sg_w=w['sg_w'].astype(F32),
        sg_bias=jnp.repeat(jnp.swapaxes(w['sg_b'].astype(F32), 1, 2), BR_WIDTH // SG_GROUPS, axis=2),
        conv_w8=jnp.pad(w['conv_w'].astype(F32), ((0, 0), (0, 5), (0, 0))), pool_scale=row(w['pool_scale']),
        norm_mix_post=row(w['norm_mix_post']), norm_ffn_pre=row(w['norm_ffn_pre']), norm_ffn_post=row(w['norm_ffn_post']),
    )


def _rope_tables(positions):
    inv_freq = ROPE_BASE ** (-jnp.arange(0, QK_ROPE, 2, dtype=F32) / QK_ROPE)
    ang = positions.astype(F32)[:, None] * inv_freq
    cos, sin = jnp.cos(ang), jnp.sin(ang)
    n = positions.shape[0]
    ones, z64, z32 = jnp.ones((n, QK_NOPE), F32), jnp.zeros((n, QK_NOPE), F32), jnp.zeros((n, 32), F32)
    return (jnp.concatenate([ones, cos, cos, z32], 1), jnp.concatenate([z64, sin, sin, z32], 1),
            jnp.concatenate([z64, cos, cos, z32], 1))


def _reference_layout_grads(g):
    gate, dm1, dm2, dpb = g['dw_in_pieces']
    nl = gate.shape[0]
    sg_u, sg_v, cv_b = jnp.split(dm1, 3, axis=-1)
    cv_x, cv_c, pool = jnp.split(dm2, 3, axis=-1)
    c_q, c_kv, kr, krs = jnp.split(dpb, [Q_RANK, Q_RANK + KV_RANK, Q_RANK + KV_RANK + HEAD_PAD], axis=-1)
    rope_cols = slice(QK_NOPE, QK_NOPE + QK_ROPE)
    k_r = kr[..., rope_cols] + _rope_unswap(krs[..., rope_cols])
    w_in = jnp.concatenate([c_q, c_kv, k_r, sg_u, sg_v, cv_x, cv_b, cv_c, pool, gate], -1)
    hw = N_HEADS * HEAD_PAD
    dqa = g['dwq'][..., :hw].reshape(nl, Q_RANK, N_HEADS, HEAD_PAD)
    dqb = g['dwq'][..., hw:].reshape(nl, Q_RANK, N_HEADS, HEAD_PAD)
    w_uq = jnp.concatenate([dqa[..., :QK_NOPE], dqa[..., rope_cols] + _rope_unswap(dqb[..., rope_cols])], -1)
    dka = g['dwkv'][..., :hw].reshape(nl, KV_RANK, N_HEADS, HEAD_PAD)
    dva = g['dwkv'][..., hw:].reshape(nl, KV_RANK, N_HEADS, HEAD_PAD)
    w_ukv = jnp.concatenate([dka[..., :QK_NOPE], dva[..., :V_HEAD]], -1)
    w_br_mla = g['dw_br_mla_p'].reshape(nl, N_HEADS, HEAD_PAD, D_MODEL)[:, :, :V_HEAD]
    dwbd = g['dwbd'].reshape(nl, 4, 64, 4, 64)
    pool_w = jnp.stack([dwbd[:, k, :, k, :] for k in range(4)], axis=1)
    sq = lambda a: a[:, 0, :]
    return dict(
        norm_mix_pre=sq(g['dg_pre']), w_in=w_in, gate_b=sq(g['dgate_b']), q_norm=sq(g['dq_norm']),
        w_uq=w_uq.reshape(nl, Q_RANK, -1), kv_norm=sq(g['dkv_norm']), w_ukv=w_ukv.reshape(nl, KV_RANK, -1),
        w_br_mla=w_br_mla.reshape(nl, N_HEADS * V_HEAD, D_MODEL), sg_ln_g=sq(g['dln_g']), sg_ln_b=sq(g['dln_b']),
        sg_w=g['dsg_w'], sg_b=jnp.swapaxes(g['dsg_b'][:, :, :SG_GROUPS], 1, 2), w_br_sg=g['dw_br_sg'],
        conv_w=g['dconv_w'][:, :3], w_br_conv=g['dw_br_conv'], pool_w=pool_w, pool_scale=sq(g['dpool_scale']),
        w_br_pool=g['dw_br_pool'], w_out=g['dw_out'], norm_mix_post=sq(g['dg_post']), norm_ffn_pre=sq(g['dg_fpre']),
        w_ff1=g['dw_ff1'], w_ff2=g['dw_ff2'], norm_ffn_post=sq(g['dg_fpost']))


def _layer_forward(x0, lw, tabs, carried):
    proj = _mm("in_proj", x0, lw['w_in_p'], tm=1024, tn=N_PROJ, prologue=_rms, rows=(lw['norm_mix_pre'],))
    q, k, v = _qkv_prep(proj, lw['q_norm'], lw['kv_norm'], lw['wq'], lw['wkv'], *tabs)
    (o, lse), carried_out = _attn_fwd(q, k, v, carried)
    x1 = _mix_fwd(x0, proj, o, lw)
    a = _mm("ffn1", x1, lw['w_ff1'], tm=1024, tn=D_FF, prologue=_rms, rows=(lw['norm_ffn_pre'],))
    x2, f = _ffn2(a, lw['w_ff2'], x1, lw['norm_ffn_post'])
    return x2, dict(x0=x0, proj=proj, q=q, k=k, v=v, o=o, lse=lse, x1=x1, a=a, f=f), carried_out


def _layer_backward(dx2, lw, sv, tabs, early, late):
    g = {}
    (df, da, g['dg_fpost']), early_out = _ffn2_bwd(dx2, sv['f'], lw['norm_ffn_post'], sv['a'], lw['w_ff2t'], early)
    carried = late(early_out)
    g['dw_ff2'] = _mm_tn("dw_ff2", sv['a'], df, tm=512, tn=1024, prologue=_relu_sq)
    dx1, g['dg_fpre'] = _norm_in_bwd("ffn1_bwd", [(da, lw['w_ff1t'])], sv['x1'], lw['norm_ffn_pre'], dx2)
    g['dw_ff1'] = _mm_tn("dw_ff1", sv['x1'], da, tm=512, tn=2048, prologue=_rms, rows=(lw['norm_ffn_pre'],))
    (dgate, dm1, dyv, upool, do, delta, g['dgate_b'], g['dln_g'], g['dln_b'], g['dsg_w'], g['dsg_b'], g['dconv_w'],
     g['dwbd'], g['dpool_scale'], g['dw_br_mla_p'], g['dw_br_sg'], g['dw_br_conv'], g['dw_br_pool'], g['dw_out'],
     g['dg_post']) = _mix_bwd(dx1, sv['proj'], sv['o'], lw)
    dm2 = _shift_bwd(dyv, upool, sv['proj'], lw['conv_w8'])
    (dq, dk, dv), carried_out = _attn_bwd(sv['q'], sv['k'], sv['v'], do, sv['lse'], delta, carried)
    dpb, g['dwq'], g['dwkv'], g['dq_norm'], g['dkv_norm'] = _qkv_bwd(
        dq, dk, dv, sv['proj'], lw['q_norm'], lw['kv_norm'], lw['wq'], lw['wkv'], *tabs)
    pieces = [(dgate, lw['wt_g']), (dm1, lw['wt_m1']), (dm2, lw['wt_m2']), (dpb, lw['wt_b'])]
    dx0, g['dg_pre'] = _norm_in_bwd("in_proj_bwd", pieces, sv['x0'], lw['norm_mix_pre'], dx1)
    g['dw_in_pieces'] = [_mm_tn("dw_in_%d" % n, sv['x0'], d, tm=512, tn=2048, prologue=_rms, rows=(lw['norm_mix_pre'],))
                         for n, (d, _) in enumerate(pieces)]
    return dx0, g, carried_out


class _StepPlan(NamedTuple):
    n_layers: int
    weights_exchange: Callable
    weights_from: Callable
    grads_ready: Callable
    first_done: Callable
    second_done: Callable


def _local_step(x, positions, target, plan):
    tabs = _rope_tables(positions)
    derive = lambda w: {n: a[0] for n, a in _derive_weights(w).items()}
    first = plan.weights_exchange(0)
    weights = plan.weights_from(0, None if first is None else _run_exchange("gather_weights", first))
    derived, saved = [], []
    for l in range(plan.n_layers):
        derived.append(derive(weights))
        coming = plan.weights_exchange(l + 1) if l + 1 < plan.n_layers else None
        x, sv, arrived = _layer_forward(x, derived[l], tabs, coming)
        saved.append(sv)
        if l + 1 < plan.n_layers:
            weights = plan.weights_from(l + 1, arrived)
    loss, dx = _loss_and_grad(x, target)
    pending = None
    for l in reversed(range(plan.n_layers)):
        if pending is None:
            dx, g, _ = _layer_backward(dx, derived[l], saved[l], tabs, None, lambda _: None)
        else:
            dx, g, arrived = _layer_backward(dx, derived[l], saved[l], tabs, pending[1],
                                             functools.partial(plan.first_done, pending[0]))
            plan.second_done(pending[0], arrived)
        lead = lambda a: [b[None] for b in a] if isinstance(a, list) else a[None]
        going = plan.grads_ready(l, _reference_layout_grads({n: lead(a) for n, a in g.items()}))
        pending = None if going is None else (l, going)
    if pending is not None:
        second = plan.first_done(pending[0], _run_exchange("grads_to_sibling", pending[1]))
        plan.second_done(pending[0], _run_exchange("grads_to_chips", second))
    return loss, dx


def _relative_peers():
    x, y = lax.axis_index("x"), lax.axis_index("y")
    return {1: (x, 1 - y), 2: (1 - x, y), 3: (1 - x, 1 - y)}


def _for_my_core(fn):
    def run():
        for half in (0, 1):
            pl.when(lax.axis_index("c") == half)(functools.partial(fn, half))
    return run


def _gather_exchange(split, whole):
    ns, nw = len(split), len(whole)
    n = ns + nw

    def phases(ins, outs, sems):
        ici_send, ici_recv, d2d_send, d2d_recv, own_send, own_recv = sems
        x, y, c = lax.axis_index("x"), lax.axis_index("y"), lax.axis_index("c")
        peers = _relative_peers()

        def rows(ref, which):
            h = ref.shape[-2] // 2
            return ref.at[(slice(None),) * (len(ref.shape) - 2) + (slice(which * h, (which + 1) * h), slice(None))]

        def own(k):
            return pltpu.make_async_remote_copy(src_ref=ins[k], dst_ref=outs[k].at[0], send_sem=own_send.at[k],
                                                recv_sem=own_recv.at[k], device_id=(x, y, 1 - c), device_id_type=MESH)

        def over_ici(k, r, half):
            src = rows(ins[k], half) if k < ns else ins[k]
            dst = rows(outs[k].at[r], half) if k < ns else outs[k].at[r]
            return pltpu.make_async_remote_copy(src_ref=src, dst_ref=dst, send_sem=ici_send.at[3 * k + r - 1],
                                                recv_sem=ici_recv.at[3 * k + r - 1], device_id=(*peers[r], c), device_id_type=MESH)

        def to_sibling(k, r, half):
            landed = rows(outs[k].at[r], half)
            return pltpu.make_async_remote_copy(src_ref=landed, dst_ref=landed, send_sem=d2d_send.at[3 * k + r - 1],
                                                recv_sem=d2d_recv.at[3 * k + r - 1], device_id=(x, y, 1 - c), device_id_type=MESH)

        def start(half):
            for k in range(n):
                own(k).start()
                for r in peers:
                    over_ici(k, r, half).start()

        def middle(half):
            for k in range(n):
                for r in peers:
                    over_ici(k, r, half).wait_recv()
                    if k < ns:
                        to_sibling(k, r, half).start()

        def finish(half):
            for k in range(n):
                own(k).wait()
                for r in peers:
                    if k < ns:
                        to_sibling(k, r, 1 - half).wait_recv()
                        to_sibling(k, r, half).wait_send()
                    over_ici(k, r, half).wait_send()

        return _for_my_core(start), _for_my_core(middle), _for_my_core(finish)

    arrs = list(split) + list(whole)
    return _Exchange(
        operands=arrs, out_shape=[jax.ShapeDtypeStruct((4,) + a.shape, a.dtype) for a in arrs],
        scratch_shapes=[pltpu.SemaphoreType.DMA((3 * n,)), pltpu.SemaphoreType.DMA((3 * n,)), pltpu.SemaphoreType.DMA((3 * ns,)),
                        pltpu.SemaphoreType.DMA((3 * ns,)), pltpu.SemaphoreType.DMA((n,)), pltpu.SemaphoreType.DMA((n,))],
        phases=phases)


def _absolute_chip_order(relative):
    me = 2 * lax.axis_index("x") + lax.axis_index("y")
    return jnp.stack([lax.dynamic_index_in_dim(relative, jnp.bitwise_xor(me, chip), 0, keepdims=False) for chip in range(4)])


REDUCE_STEPS = 8


def _sibling_exchange(arrs):
    n = len(arrs)

    def phases(ins, theirs, sems):
        send_sems, recv_sems = sems
        x, y, c = lax.axis_index("x"), lax.axis_index("y"), lax.axis_index("c")

        def copy(k, my_half):
            h = ins[k].shape[1] // 2
            return pltpu.make_async_remote_copy(src_ref=ins[k].at[:, (1 - my_half) * h:(2 - my_half) * h, :], dst_ref=theirs[k],
                                                send_sem=send_sems.at[k], recv_sem=recv_sems.at[k],
                                                device_id=(x, y, 1 - c), device_id_type=MESH)

        def start(my_half):
            for k in range(n):
                copy(k, my_half).start()

        def finish(my_half):
            for k in range(n):
                copy(k, my_half).wait()

        return _for_my_core(start), lambda: None, _for_my_core(finish)

    return _Exchange(operands=list(arrs),
                     out_shape=[jax.ShapeDtypeStruct((a.shape[0], a.shape[1] // 2, a.shape[2]), a.dtype) for a in arrs],
                     scratch_shapes=[pltpu.SemaphoreType.DMA((n,)), pltpu.SemaphoreType.DMA((n,))], phases=phases)


def _add_sibling(name, arrs, theirs):
    n, steps = len(arrs), REDUCE_STEPS

    def body(*refs):
        for mine_ref, theirs_ref, out_ref in zip(refs[:n], refs[n:2 * n], refs[2 * n:]):
            out_ref[...] = (mine_ref[...] + theirs_ref[...]).astype(out_ref.dtype)

    block = lambda t: (4, t.shape[1] // steps, t.shape[2])
    return pl.pallas_call(
        body, name=name, grid=(steps,),
        in_specs=[pl.BlockSpec(block(t), lambda i: (0, lax.axis_index("c") * steps + i, 0)) for t in theirs]
        + [pl.BlockSpec(block(t), lambda i: (0, i, 0)) for t in theirs],
        out_specs=[pl.BlockSpec(block(t), lambda i: (0, i, 0)) for t in theirs],
        out_shape=[jax.ShapeDtypeStruct(t.shape, WIRE_DTYPE) for t in theirs],
        compiler_params=_params(("parallel",)))(*arrs, *theirs)


def _scatter_exchange(arrs):
    n = len(arrs)

    def phases(ins, outs, sems):
        send_sems, recv_sems = sems
        c = lax.axis_index("c")
        peers = _relative_peers()

        def copy(k, r):
            px, py = peers[r]
            return pltpu.make_async_remote_copy(src_ref=ins[k].at[2 * px + py], dst_ref=outs[k].at[r - 1],
                                                send_sem=send_sems.at[3 * k + r - 1], recv_sem=recv_sems.at[3 * k + r - 1],
                                                device_id=(px, py, c), device_id_type=MESH)

        def start():
            for k in range(n):
                for r in peers:
                    copy(k, r).start()

        def finish():
            for k in range(n):
                for r in peers:
                    copy(k, r).wait()

        return start, lambda: None, finish

    return _Exchange(operands=list(arrs), out_shape=[jax.ShapeDtypeStruct((3,) + a.shape[1:], a.dtype) for a in arrs],
                     scratch_shapes=[pltpu.SemaphoreType.DMA((3 * n,)), pltpu.SemaphoreType.DMA((3 * n,))], phases=phases)


def _sum_chips(name, chip_sums, arrived):
    n, steps = len(chip_sums), REDUCE_STEPS

    def body(*refs):
        for own_ref, arrived_ref, out_ref in zip(refs[:n], refs[n:2 * n], refs[2 * n:]):
            acc = own_ref[...].astype(F32)
            for r in range(3):
                acc = acc + arrived_ref[r].astype(F32)
            out_ref[...] = acc

    rows = lambda s: s.shape[1] // steps
    chip = lambda: 2 * lax.axis_index("x") + lax.axis_index("y")
    return pl.pallas_call(
        body, name=name, grid=(steps,),
        in_specs=[pl.BlockSpec((None, rows(s), s.shape[2]), lambda i: (chip(), i, 0)) for s in chip_sums]
        + [pl.BlockSpec((3, rows(s), s.shape[2]), lambda i: (0, i, 0)) for s in chip_sums],
        out_specs=[pl.BlockSpec((rows(s), s.shape[2]), lambda i: (lax.axis_index("c") * steps + i, 0)) for s in chip_sums],
        out_shape=[jax.ShapeDtypeStruct((2 * s.shape[1], s.shape[2]), F32) for s in chip_sums],
        compiler_params=_params(("parallel",)))(*chip_sums, *arrived)


def _join_siblings(name, bufs):
    n = len(bufs)

    def body(*refs):
        outs = refs[n:2 * n]
        send_sems, recv_sems = refs[2 * n:]
        x, y, c = lax.axis_index("x"), lax.axis_index("y"), lax.axis_index("c")

        def exchange(my_half):
            copies = []
            for k in range(n):
                h = outs[k].shape[0] // 2
                mine = outs[k].at[my_half * h:(my_half + 1) * h, :]
                theirs = outs[k].at[(1 - my_half) * h:(2 - my_half) * h, :]
                cp = pltpu.make_async_remote_copy(src_ref=mine, dst_ref=mine, send_sem=send_sems.at[k],
                                                  recv_sem=recv_sems.at[k], device_id=(x, y, 1 - c), device_id_type=MESH)
                cp.start()
                arrival = pltpu.make_async_remote_copy(src_ref=theirs, dst_ref=theirs, send_sem=send_sems.at[k],
                                                       recv_sem=recv_sems.at[k], device_id=(x, y, 1 - c), device_id_type=MESH)
                copies.append((cp, arrival))
            for cp, arrival in copies:
                arrival.wait_recv()
                cp.wait_send()

        for half in (0, 1):
            pl.when(c == half)(functools.partial(exchange, half))

    return pl.pallas_call(
        body, name=name, in_specs=[HBM] * n, out_specs=[HBM] * n,
        out_shape=[jax.ShapeDtypeStruct(b.shape, b.dtype) for b in bufs], input_output_aliases={k: k for k in range(n)},
        scratch_shapes=[pltpu.SemaphoreType.DMA((n,)), pltpu.SemaphoreType.DMA((n,))],
    )(*bufs)


def _gather_all_exchange(a):
    def phases(ins, outs, scratch):
        (a_ref,), (out_ref,) = ins, outs
        staging, send_sems, recv_sems, local_sem = scratch
        x, y, c = lax.axis_index("x"), lax.axis_index("y"), lax.axis_index("c")
        me = 4 * x + 2 * y + c
        flips = [(fx, fy, fc) for fx in (0, 1) for fy in (0, 1) for fc in (0, 1)][1:]
        peers = [(x ^ fx, y ^ fy, c ^ fc) for fx, fy, fc in flips]

        def copy(j):
            px, py, pc = peers[j]
            return pltpu.make_async_remote_copy(src_ref=a_ref, dst_ref=out_ref.at[me], send_sem=send_sems.at[j],
                                                recv_sem=recv_sems.at[j], device_id=(px, py, pc), device_id_type=MESH)

        def arrival(j):
            px, py, pc = peers[j]
            return pltpu.make_async_remote_copy(src_ref=a_ref, dst_ref=out_ref.at[4 * px + 2 * py + pc], send_sem=send_sems.at[j],
                                                recv_sem=recv_sems.at[j], device_id=(px, py, pc), device_id_type=MESH)

        own = pltpu.make_async_copy(staging, out_ref.at[me], local_sem)

        def start():
            load = pltpu.make_async_copy(a_ref, staging, local_sem)
            load.start()
            load.wait()
            own.start()
            for j in range(7):
                copy(j).start()

        def finish():
            for j in range(7):
                arrival(j).wait_recv()
            for j in range(7):
                copy(j).wait_send()
            own.wait()

        return start, lambda: None, finish

    return _Exchange(operands=[a], out_shape=[jax.ShapeDtypeStruct((8,) + a.shape, a.dtype)],
                     scratch_shapes=[pltpu.VMEM(a.shape, a.dtype), pltpu.SemaphoreType.DMA((7,)), pltpu.SemaphoreType.DMA((7,)),
                                     pltpu.SemaphoreType.DMA], phases=phases)


def _rowwise_call(name, fn, slots, out_shapes, steps, carried=None):
    n_in, n_out = [len(s) for s in slots], [len(o) for o in out_shapes]

    def spec(shape):
        if len(shape) == 3:
            return pl.BlockSpec((shape[0], shape[1] // steps, shape[2]), lambda i: (0, i, 0))
        return pl.BlockSpec((shape[0] // steps, shape[1]), lambda i: (i, 0))

    def body(*refs):
        ins, outs = refs[:sum(n_in)], refs[sum(n_in):]
        a = b = 0
        for k in range(len(slots)):
            for o_ref, val in zip(outs[b:b + n_out[k]], fn(*[r[...] for r in ins[a:a + n_in[k]]])):
                o_ref[...] = val
            a, b = a + n_in[k], b + n_out[k]

    flat_in = [arr for s in slots for arr in s]
    flat_out = [shp for o in out_shapes for shp in o]
    out, carried_out = _call_carrying(
        body, carried, name=name, grid=(steps,), in_specs=[spec(a.shape) for a in flat_in],
        out_specs=[spec(s) for s in flat_out], out_shape=[jax.ShapeDtypeStruct(s, F32) for s in flat_out],
        operands=flat_in, semantics=("parallel",))
    grouped, b = [], 0
    for k in range(len(slots)):
        grouped.append(out[b:b + n_out[k]])
        b += n_out[k]
    return grouped if carried is None else (grouped, carried_out)


def _sum_in_order(a):
    acc = a[0].astype(F32)
    for k in range(1, a.shape[0]):
        acc = acc + a[k].astype(F32)
    return (acc,)


def _adamw_math(w, g, m, v):
    m_new = ADAM_B1 * m + (1.0 - ADAM_B1) * g
    v_new = ADAM_B2 * v + (1.0 - ADAM_B2) * (g * g)
    m_hat = m_new / (1.0 - ADAM_B1 ** ADAM_STEP)
    v_hat = v_new / (1.0 - ADAM_B2 ** ADAM_STEP)
    return -ADAM_LR * (m_hat / (jnp.sqrt(v_hat) + ADAM_EPS) + ADAM_WD * w), m_new, v_new


SMALL_PACK_COLS = 256
SMALL_PACK_ROWS = 2048


def _pack_small(parts):
    wide = [jnp.pad(p, ((0, 0), (0, 0), (0, SMALL_PACK_COLS - p.shape[2]))) for p in parts]
    rows = jnp.concatenate(wide, axis=1)
    return jnp.pad(rows, ((0, 0), (0, SMALL_PACK_ROWS - rows.shape[1]), (0, 0)))


def _unpack_small(packed, shapes):
    out, row = [], 0
    for a, b in shapes:
        out.append(packed[:, row:row + a, :b])
        row += a
    return out


def _pack(arrs, rows_per_layer, dtype):
    nl = arrs[0].shape[0]
    flat = jnp.concatenate([a.astype(dtype).reshape(nl, -1) for a in arrs], axis=1)
    flat = jnp.pad(flat, ((0, 0), (0, rows_per_layer * PACK_COLS - flat.shape[1])))
    return flat.reshape(nl * rows_per_layer, PACK_COLS)


def _unpack(packed, shapes, rows_per_layer):
    nl = shapes[0][0]
    flat = packed.reshape(packed.shape[:-2] + (nl, rows_per_layer * PACK_COLS))
    out, off = [], 0
    for shp in shapes:
        size = math.prod(shp[1:])
        out.append(flat[..., off:off + size].reshape(packed.shape[:-2] + tuple(shp)))
        off += size
    return out


def _rows_needed(shapes, multiple):
    per_layer = sum(math.prod(s[1:]) for s in shapes)
    rows = -(-per_layer // PACK_COLS)
    return -(-rows // multiple) * multiple


CONV_TILE = (8, 128)


def _layer_shard_exchange(w, l):
    conv = w['conv_w'][l].reshape(-1)
    conv = jnp.pad(conv, (0, math.prod(CONV_TILE) - conv.shape[0])).reshape(CONV_TILE)
    return _gather_exchange([w[n][l].astype(MXU_DTYPE) for n in MATMUL_SHARDED], [conv])


def _layer_full_weights(w, l, gathered):
    gathered = [_absolute_chip_order(g) for g in gathered]
    full = {n: w[n][l:l + 1] for n in WEIGHTS}
    for n, part in zip(MATMUL_SHARDED, gathered):
        if n in ROW_SHARDED:
            full[n] = part.reshape(1, 4 * part.shape[1], part.shape[2])
        else:
            full[n] = jnp.swapaxes(part, 0, 1).reshape(1, part.shape[1], 4 * part.shape[2])
    rows, cols = w['conv_w'].shape[1:]
    conv = gathered[-1].reshape(4, -1)[:, :rows * cols].reshape(4, rows, cols)
    full['conv_w'] = jnp.swapaxes(conv, 0, 1).reshape(1, rows, 4 * cols)
    return full


def _chip_major(n, g):
    nl = g.shape[0]
    if n in ROW_SHARDED:
        return jnp.swapaxes(g.reshape(nl, 4, g.shape[1] // 4, g.shape[2]), 0, 1)
    return jnp.transpose(g.reshape(nl, g.shape[1], 4, g.shape[2] // 4), (2, 0, 1, 3))


def kernel(x, positions, norm_mix_pre, w_in, gate_b, q_norm, w_uq, kv_norm, w_ukv, w_br_mla, sg_ln_g, sg_ln_b, sg_w, sg_b, w_br_sg, conv_w, w_br_conv, pool_w, pool_scale, w_br_pool, w_out, norm_mix_post, norm_ffn_pre, w_ff1, w_ff2, norm_ffn_post, loss_target, m_norm_mix_pre, m_w_in, m_gate_b, m_q_norm, m_w_uq, m_kv_norm, m_w_ukv, m_w_br_mla, m_sg_ln_g, m_sg_ln_b, m_sg_w, m_sg_b, m_w_br_sg, m_conv_w, m_w_br_conv, m_pool_w, m_pool_scale, m_w_br_pool, m_w_out, m_norm_mix_post, m_norm_ffn_pre, m_w_ff1, m_w_ff2, m_norm_ffn_post, v_norm_mix_pre, v_w_in, v_gate_b, v_q_norm, v_w_uq, v_kv_norm, v_w_ukv, v_w_br_mla, v_sg_ln_g, v_sg_ln_b, v_sg_w, v_sg_b, v_w_br_sg, v_conv_w, v_w_br_conv, v_pool_w, v_pool_scale, v_w_br_pool, v_w_out, v_norm_mix_post, v_norm_ffn_pre, v_w_ff1, v_w_ff2, v_norm_ffn_post):
    given = dict(locals())
    w = {n: given[n] for n in WEIGHTS}
    mom = {n: given['m_' + n] for n in WEIGHTS}
    var = {n: given['v_' + n] for n in WEIGHTS}
    nl = w['w_in'].shape[0]
    sharded, chip_sums, reduced, local_small = {}, {}, {}, {}

    def grads_ready(l, g):
        sharded[l] = [_chip_major(n, g[n])[:, 0] for n in BIG_SHARDED]
        sharded[l].append(_pack_small([_chip_major(n, g[n])[:, 0] for n in SMALL_SHARDED]))
        local_small[l] = [g[n] for n in REPLICATED + ['conv_w']]
        return _sibling_exchange(sharded[l])

    def sibling_done(l, theirs):
        chip_sums[l] = _add_sibling("add_sibling", sharded[l], theirs)
        return _scatter_exchange(chip_sums[l])

    def chips_done(l, arrived):
        reduced[l] = _join_siblings("join_halves", _sum_chips("sum_chips", chip_sums[l], arrived))

    plan = _StepPlan(n_layers=nl, weights_exchange=functools.partial(_layer_shard_exchange, w),
                     weights_from=functools.partial(_layer_full_weights, w), grads_ready=grads_ready,
                     first_done=sibling_done, second_done=chips_done)
    loss, dx = _local_step(x[0], positions[0], loss_target[0], plan)
    loss = lax.psum(loss, ("x", "y", "c"))

    grad, delta, new_m, new_v = {}, {}, {}, {}
    names = REPLICATED + ['conv_w']
    local = [jnp.concatenate([local_small[l][k] for l in range(nl)]) for k in range(len(names))]
    rows = _rows_needed([a.shape for a in local], 32)
    small_grads = _gather_all_exchange(_pack(local, rows, F32))
    for k, n in enumerate(BIG_SHARDED):
        grad[n] = jnp.stack([reduced[l][k] for l in range(nl)])
    slots = [[w[n], grad[n], mom[n], var[n]] for n in BIG_SHARDED]
    small_pack = lambda d: _pack_small([d[n] for n in SMALL_SHARDED])
    g_small = jnp.stack([reduced[l][-1] for l in range(nl)])
    slots.append([small_pack(w), g_small, small_pack(mom), small_pack(var)])
    updated, (everyone,) = _rowwise_call("adamw_sharded", _adamw_math, slots, [[s_[0].shape] * 3 for s_ in slots], 32,
                                         carried=small_grads)
    for k, n in enumerate(BIG_SHARDED):
        delta[n], new_m[n], new_v[n] = updated[k]
    for d, packed in zip((grad, delta, new_m, new_v), [g_small] + list(updated[-1])):
        d.update(zip(SMALL_SHARDED, _unpack_small(packed, [w[n].shape[1:] for n in SMALL_SHARDED])))

    (summed,), = _rowwise_call("sum_devices", _sum_in_order, [[everyone]], [[everyone.shape[1:]]], 4)
    g_rep = _unpack(summed, [a.shape for a in local], rows)
    chip = 2 * lax.axis_index("x") + lax.axis_index("y")
    g_rep[-1] = lax.dynamic_slice_in_dim(g_rep[-1], chip * w['conv_w'].shape[2], w['conv_w'].shape[2], axis=2)
    rep_pack = lambda arrs: _pack(arrs, rows, F32)
    (rep_out,) = _rowwise_call("adamw_replicated", _adamw_math,
                               [[rep_pack([w[n] for n in names]), rep_pack(g_rep), rep_pack([mom[n] for n in names]),
                                 rep_pack([var[n] for n in names])]], [[(nl * rows, PACK_COLS)] * 3], 4)
    grad.update(zip(names, g_rep))
    for d, packed in zip((delta, new_m, new_v), rep_out):
        d.update(zip(names, _unpack(packed, [w[n].shape for n in names], rows)))

    return (loss, dx[None], *[grad[n] for n in WEIGHTS], *[delta[n] for n in WEIGHTS], *[new_m[n] for n in WEIGHTS],
            *[new_v[n] for n in WEIGHTS])
```

```python
import functools
import math
from typing import Any, Callable, NamedTuple, Sequence

import jax
import jax.numpy as jnp
from jax import lax
from jax.experimental import pallas as pl
from jax.experimental.pallas import tpu as pltpu

F32 = jnp.float32
MXU_DTYPE = jnp.bfloat16
WIRE_DTYPE = jnp.bfloat16
ACT_DTYPE = jnp.bfloat16
MESH = pl.DeviceIdType.MESH

D_MODEL = 1024
D_FF = 4096
N_HEADS = 4
QK_NOPE = 64
QK_ROPE = 32
V_HEAD = 64
HEAD_PAD = 128
Q_RANK = 256
KV_RANK = 128
SG_CHUNK = 128
SG_GROUPS = 4
BR_WIDTH = 256
N_BRANCH = 4
POOL_WINDOWS = (2, 4, 8, 16)
HALO = 16
ROPE_BASE = 10000.0
EPS = 1e-6
ATTN_SCALE = (QK_NOPE + QK_ROPE) ** -0.5
LOG2_E = math.log2(math.e)
FWD_HEADS_PER_STEP = 4
HEADS_PER_STEP = 2
N_PROJ = N_BRANCH * D_MODEL + 6 * BR_WIDTH + Q_RANK + KV_RANK + 2 * HEAD_PAD
COL_G, COL_M1, COL_M2, COL_B = 0, 4096, 4864, 5632

ADAM_LR, ADAM_B1, ADAM_B2, ADAM_EPS, ADAM_WD, ADAM_STEP = 0.001, 0.9, 0.999, 1e-08, 0.01, 10

VMEM_LIMIT = 56 * 1024 * 1024

WEIGHTS = ['norm_mix_pre', 'w_in', 'gate_b', 'q_norm', 'w_uq', 'kv_norm', 'w_ukv', 'w_br_mla', 'sg_ln_g', 'sg_ln_b',
           'sg_w', 'sg_b', 'w_br_sg', 'conv_w', 'w_br_conv', 'pool_w', 'pool_scale', 'w_br_pool', 'w_out',
           'norm_mix_post', 'norm_ffn_pre', 'w_ff1', 'w_ff2', 'norm_ffn_post']
COL_SHARDED = ['w_in', 'w_uq', 'w_ukv', 'w_br_mla', 'w_br_sg', 'w_br_conv', 'w_br_pool', 'w_ff1']
ROW_SHARDED = ['w_out', 'w_ff2']
MATMUL_SHARDED = ['w_in', 'w_uq', 'w_ukv', 'w_br_mla', 'w_br_sg', 'w_br_conv', 'w_br_pool', 'w_out', 'w_ff1', 'w_ff2']
BIG_SHARDED = ['w_in', 'w_ff1', 'w_ff2', 'w_out']
SMALL_SHARDED = ['w_uq', 'w_ukv', 'w_br_mla', 'w_br_sg', 'w_br_conv', 'w_br_pool']
SHARDED = MATMUL_SHARDED + ['conv_w']
REPLICATED = [n for n in WEIGHTS if n not in SHARDED]
PACK_COLS = 1024


def _params(sem, vmem=VMEM_LIMIT):
    return pltpu.CompilerParams(dimension_semantics=sem, vmem_limit_bytes=vmem)


def _mxu(a):
    return a.astype(MXU_DTYPE)


def _dot(a, b):
    return jnp.dot(_mxu(a), _mxu(b), preferred_element_type=F32)


def _dot_nt(a, b):
    return lax.dot_general(_mxu(a), _mxu(b), (((1,), (1,)), ((), ())), preferred_element_type=F32)


def _dot_tn(a, b):
    return lax.dot_general(_mxu(a), _mxu(b), (((0,), (0,)), ((), ())), preferred_element_type=F32)


def _rms(x, g):
    r = lax.rsqrt(jnp.mean(x * x, axis=-1, keepdims=True) + EPS)
    return x * r * g


def _rms_bwd(x, g, dy):
    r = lax.rsqrt(jnp.mean(x * x, axis=-1, keepdims=True) + EPS)
    xh = x * r
    gdy = dy * g
    dx = r * (gdy - xh * jnp.mean(gdy * xh, axis=-1, keepdims=True))
    return dx, jnp.sum(dy * xh, axis=0, keepdims=True)


_GELU_C = math.sqrt(2.0 / math.pi)


def _gelu(x):
    t = jnp.tanh(_GELU_C * (x + 0.044715 * (x * x * x)))
    return x * (0.5 * (1.0 + t)), t


def _gelu_grad(x, t):
    return 0.5 * (1.0 + t) + 0.5 * x * (1.0 - t * t) * (_GELU_C * (1.0 + 3.0 * 0.044715 * x * x))


def _sigmoid(x):
    return 1.0 / (1.0 + jnp.exp(-x))


def _full(shape):
    return pl.BlockSpec(shape, lambda *_: (0,) * len(shape))


def _resident(shape):
    return pl.BlockSpec(shape, lambda *_: (0,) * len(shape), pipeline_mode=pl.Buffered(1))


def _rows(ts, width, col=0):
    return pl.BlockSpec((ts, width), lambda i: (i, col))


def _tile(n, pref):
    return min(n, pref)


def _mm(name, a, w, *, tm, tn, prologue=None, rows=()):
    m, k = a.shape
    n = w.shape[1]
    tm, tn = _tile(m, tm), _tile(n, tn)

    def body(a_ref, *rest):
        row_refs, w_ref, o_ref = rest[:len(rows)], rest[len(rows)], rest[len(rows) + 1]
        av = a_ref[...]
        if prologue is not None:
            av = prologue(av, *[r[...] for r in row_refs])
        o_ref[...] = _dot(av, w_ref[...]).astype(o_ref.dtype)

    return pl.pallas_call(
        body, name=name, grid=(m // tm, n // tn),
        in_specs=[pl.BlockSpec((tm, k), lambda i, j: (i, 0))] + [pl.BlockSpec((1, k), lambda i, j: (0, 0)) for _ in rows]
        + [_resident((k, n)) if tn == n else pl.BlockSpec((k, tn), lambda i, j: (0, j))],
        out_specs=pl.BlockSpec((tm, tn), lambda i, j: (i, j)),
        out_shape=jax.ShapeDtypeStruct((m, n), ACT_DTYPE),
        compiler_params=_params(("parallel", "parallel")),
    )(a, *rows, w)


def _mm_tn(name, a, b, *, tm, tn, prologue=None, rows=()):
    m, k = a.shape
    n = b.shape[1]
    tm, tn = _tile(m, tm), _tile(n, tn)

    def body(a_ref, *rest):
        row_refs, b_ref, o_ref = rest[:len(rows)], rest[len(rows)], rest[len(rows) + 1]

        @pl.when(pl.program_id(1) == 0)
        def _():
            o_ref[...] = jnp.zeros_like(o_ref)

        av = a_ref[...]
        if prologue is not None:
            av = prologue(av, *[r[...] for r in row_refs])
        o_ref[...] += _dot_tn(av, b_ref[...])

    return pl.pallas_call(
        body, name=name, grid=(n // tn, m // tm),
        in_specs=[pl.BlockSpec((tm, k), lambda j, i: (i, 0))] + [pl.BlockSpec((1, k), lambda j, i: (0, 0)) for _ in rows]
        + [pl.BlockSpec((tm, tn), lambda j, i: (i, j))],
        out_specs=pl.BlockSpec((k, tn), lambda j, i: (0, j)),
        out_shape=jax.ShapeDtypeStruct((k, n), F32),
        compiler_params=_params(("parallel", "arbitrary")),
    )(a, *rows, b)


def _relu_sq(a):
    r = jnp.maximum(a.astype(F32), 0.0)
    return r * r


HBM = pl.BlockSpec(memory_space=pl.ANY)


class _Exchange(NamedTuple):
    operands: Sequence[Any]
    out_shape: Sequence[Any]
    scratch_shapes: Sequence[Any]
    phases: Callable


def _run_exchange(name, ex):
    n_in, n_out = len(ex.operands), len(ex.out_shape)

    def body(*refs):
        for phase in ex.phases(refs[:n_in], refs[n_in:n_in + n_out], refs[n_in + n_out:]):
            phase()

    return pl.pallas_call(body, name=name, in_specs=[HBM] * n_in, out_specs=[HBM] * n_out, out_shape=list(ex.out_shape),
                          scratch_shapes=list(ex.scratch_shapes))(*ex.operands)


def _call_carrying(body, carried, *, name, grid, in_specs, out_specs, out_shape, operands, semantics, middle_step=None):
    if carried is None:
        return pl.pallas_call(body, name=name, grid=grid, in_specs=in_specs, out_specs=out_specs, out_shape=out_shape,
                              compiler_params=_params(semantics))(*operands), None
    n_main_in, n_main_out = len(operands), len(out_shape)
    n_in, n_out = len(carried.operands), len(carried.out_shape)
    steps = math.prod(grid)

    def wrapped(*refs):
        main_in, refs = refs[:n_main_in], refs[n_main_in:]
        ex_in, refs = refs[:n_in], refs[n_in:]
        main_out, refs = refs[:n_main_out], refs[n_main_out:]
        ex_out, sems = refs[:n_out], refs[n_out:]
        step = pl.program_id(0)
        for axis in range(1, len(grid)):
            step = step * grid[axis] + pl.program_id(axis)
        start, middle, finish = carried.phases(ex_in, ex_out, sems)
        pl.when(step == 0)(start)
        pl.when(step == ((steps - 1) // 2 if middle_step is None else middle_step))(middle)
        body(*main_in, *main_out)
        pl.when(step == steps - 1)(finish)

    out = pl.pallas_call(
        wrapped, name=name + "_carrying", grid=grid, in_specs=list(in_specs) + [HBM] * n_in,
        out_specs=list(out_specs) + [HBM] * n_out, out_shape=list(out_shape) + list(carried.out_shape),
        scratch_shapes=list(carried.scratch_shapes), compiler_params=_params(("arbitrary",) * len(grid)))(*operands, *carried.operands)
    return out[:n_main_out], out[n_main_out:]


def _qkv_prep(proj, q_norm, kv_norm, wq, wkv, cq_tab, s_tab, cr_tab):
    s = proj.shape[0]
    ts = _tile(s, 512)
    hw = N_HEADS * HEAD_PAD

    def body(cq_ref, ckv_ref, kr_ref, krs_ref, gq_ref, gkv_ref, wq_ref, wkv_ref, ct_ref, st_ref, crt_ref,
             q_ref, k_ref, v_ref):
        ct, st, crt = ct_ref[...], st_ref[...], crt_ref[...]
        qn = _rms(cq_ref[...].astype(F32), gq_ref[...])
        qab = _dot(qn, wq_ref[...])
        kvn = _rms(ckv_ref[...].astype(F32), gkv_ref[...])
        kav = _dot(kvn, wkv_ref[...])
        k_rope = kr_ref[...].astype(F32) * crt + krs_ref[...].astype(F32) * st
        ones_lane = (lax.broadcasted_iota(jnp.int32, (1, HEAD_PAD), 1) == V_HEAD).astype(F32)
        for h in range(N_HEADS):
            lo = h * HEAD_PAD
            q_ref[h] = (qab[:, lo:lo + HEAD_PAD] * ct + qab[:, hw + lo:hw + lo + HEAD_PAD] * st).astype(q_ref.dtype)
            k_ref[h] = (kav[:, lo:lo + HEAD_PAD] + k_rope).astype(k_ref.dtype)
            v_ref[h] = (kav[:, hw + lo:hw + lo + HEAD_PAD] + ones_lane).astype(v_ref.dtype)

    head_spec = pl.BlockSpec((N_HEADS, ts, HEAD_PAD), lambda i: (0, i, 0))
    head_shape = jax.ShapeDtypeStruct((N_HEADS, s, HEAD_PAD), MXU_DTYPE)
    return pl.pallas_call(
        body, name="qkv_prep", grid=(s // ts,),
        in_specs=[_rows(ts, Q_RANK, COL_B // Q_RANK), _rows(ts, KV_RANK, (COL_B + Q_RANK) // KV_RANK),
                  _rows(ts, HEAD_PAD, (COL_B + Q_RANK + KV_RANK) // HEAD_PAD),
                  _rows(ts, HEAD_PAD, (COL_B + Q_RANK + KV_RANK + HEAD_PAD) // HEAD_PAD),
                  _full((1, Q_RANK)), _full((1, KV_RANK)), _full((Q_RANK, 2 * hw)), _full((KV_RANK, 2 * hw)),
                  _rows(ts, HEAD_PAD), _rows(ts, HEAD_PAD), _rows(ts, HEAD_PAD)],
        out_specs=[head_spec, head_spec, head_spec],
        out_shape=[head_shape, head_shape, head_shape],
        compiler_params=_params(("parallel",)),
    )(proj, proj, proj, proj, q_norm, kv_norm, wq, wkv, cq_tab, s_tab, cr_tab)


def _diagonal_mask(t, keys_on_rows=False):
    key_axis = 0 if keys_on_rows else 1
    return lax.broadcasted_iota(jnp.int32, (t, t), key_axis) <= lax.broadcasted_iota(jnp.int32, (t, t), 1 - key_axis)


MIDDLE_WORK_SHARE = 0.5


def _causal_work_step(nq, groups):
    total = groups * nq * (nq + 1) // 2
    done = 0
    for step in range(groups * nq):
        if done >= MIDDLE_WORK_SHARE * total:
            return step
        done += step % nq + 1
    return groups * nq - 1


def _attn_fwd(q, k, v, carried=None):
    s = q.shape[1]
    t = _tile(s, 1024)

    def body(q_ref, k_ref, v_ref, o_ref, lse_ref):
        i = pl.program_id(1)
        lane = lax.broadcasted_iota(jnp.int32, (1, HEAD_PAD), 1)

        def step(j, carry, on_diagonal):
            rows = pl.ds(pl.multiple_of(j * t, t), t)
            out = []
            for h in range(FWD_HEADS_PER_STEP):
                m, acc = carry[h]
                sc = _dot_nt(q_ref[h], k_ref[h, rows, :]) * (ATTN_SCALE * LOG2_E)
                if on_diagonal:
                    sc = jnp.where(_diagonal_mask(t), sc, -jnp.inf)
                m_new = jnp.maximum(m, jnp.max(sc, axis=1, keepdims=True))
                out.append((m_new, jnp.exp2(m - m_new) * acc + _dot(jnp.exp2(sc - m_new), v_ref[h, rows, :])))
            return tuple(out)

        init = ((jnp.full((t, 1), -jnp.inf, F32), jnp.zeros((t, HEAD_PAD), F32)),) * FWD_HEADS_PER_STEP
        below = lax.fori_loop(0, i, functools.partial(step, on_diagonal=False), init)
        for h, (m, acc) in enumerate(step(i, below, True)):
            l = jnp.sum(jnp.where(lane == V_HEAD, acc, 0.0), axis=1, keepdims=True)
            o_ref[:, h * HEAD_PAD:(h + 1) * HEAD_PAD] = jnp.where(lane < V_HEAD, acc / l, 0.0)
            lse_ref[h] = m + jnp.log2(l)

    group = FWD_HEADS_PER_STEP
    return _call_carrying(
        body, carried, name="attn_fwd", grid=(N_HEADS // group, s // t),
        in_specs=[pl.BlockSpec((group, t, HEAD_PAD), lambda h, i: (h, i, 0)),
                  pl.BlockSpec((group, s, HEAD_PAD), lambda h, i: (h, 0, 0)),
                  pl.BlockSpec((group, s, HEAD_PAD), lambda h, i: (h, 0, 0))],
        out_specs=[pl.BlockSpec((t, group * HEAD_PAD), lambda h, i: (i, h)), pl.BlockSpec((group, t, 1), lambda h, i: (h, i, 0))],
        out_shape=[jax.ShapeDtypeStruct((s, N_HEADS * HEAD_PAD), F32), jax.ShapeDtypeStruct((N_HEADS, s, 1), F32)],
        operands=(q, k, v), semantics=("parallel", "parallel"), middle_step=_causal_work_step(s // t, N_HEADS // group))


def _attn_bwd(q, k, v, do, lse, delta, carried=None):
    s = q.shape[1]
    t = _tile(s, 1024)
    nq = s // t

    def body(q_ref, do_ref, lse_ref, dl_ref, k_ref, v_ref, dq_ref, dk_ref, dv_ref):
        j = pl.program_id(1)

        @pl.when(j == 0)
        def _():
            dq_ref[...] = jnp.zeros_like(dq_ref)

        def step(i, carry, on_diagonal):
            rows = pl.ds(pl.multiple_of(i * t, t), t)
            out = []
            for h in range(HEADS_PER_STEP):
                dk, dv = carry[h]
                qi, doi = q_ref[h, rows, :], do_ref[rows, h * HEAD_PAD:(h + 1) * HEAD_PAD]
                sc = _dot_nt(k_ref[h], qi) * (ATTN_SCALE * LOG2_E)
                if on_diagonal:
                    sc = jnp.where(_diagonal_mask(t, keys_on_rows=True), sc, -jnp.inf)
                p = jnp.exp2(sc - lse_ref[h, i])
                dv = dv + _dot(p, doi)
                ds = p * (_dot_nt(v_ref[h], doi) - dl_ref[h, i])
                dk = dk + _dot(ds, qi)
                dq_ref[h, rows, :] += _dot_tn(ds, k_ref[h]) * ATTN_SCALE
                out.append((dk, dv))
            return tuple(out)

        zero = ((jnp.zeros((t, HEAD_PAD), F32),) * 2,) * HEADS_PER_STEP
        sums = lax.fori_loop(j + 1, nq, functools.partial(step, on_diagonal=False), step(j, zero, True))
        for h, (dk, dv) in enumerate(sums):
            dk_ref[h] = dk * ATTN_SCALE
            dv_ref[h] = dv

    group = HEADS_PER_STEP
    whole = lambda w: pl.BlockSpec((group, s, w), lambda h, j: (h, 0, 0), pipeline_mode=pl.Buffered(1))
    tile = pl.BlockSpec((group, t, HEAD_PAD), lambda h, j: (h, j, 0))
    per_query = pl.BlockSpec((group, nq, 1, t), lambda h, j: (h, 0, 0, 0))
    head_shape = jax.ShapeDtypeStruct((N_HEADS, s, HEAD_PAD), F32)
    return _call_carrying(
        body, carried, name="attn_bwd", grid=(N_HEADS // group, nq),
        in_specs=[whole(HEAD_PAD), pl.BlockSpec((s, group * HEAD_PAD), lambda h, j: (0, h), pipeline_mode=pl.Buffered(1)),
                  per_query, per_query, tile, tile],
        out_specs=[whole(HEAD_PAD), tile, tile],
        out_shape=[head_shape, head_shape, head_shape],
        operands=(q, do, lse.reshape(N_HEADS, nq, 1, t), delta.reshape(N_HEADS, nq, 1, t), k, v),
        semantics=("parallel", "arbitrary"))


def _qkv_bwd(dq, dk, dv, proj, q_norm, kv_norm, wq, wkv, cq_tab, s_tab, cr_tab):
    s = proj.shape[0]
    ts = _tile(s, 512)
    hw = N_HEADS * HEAD_PAD

    def body(dq_ref, dk_ref, dv_ref, cq_ref, ckv_ref, gq_ref, gkv_ref, wq_ref, wkv_ref, ct_ref, st_ref, crt_ref,
             dpb_ref, dwq_ref, dwkv_ref, dgq_ref, dgkv_ref):
        @pl.when(pl.program_id(0) == 0)
        def _():
            for r in (dwq_ref, dwkv_ref, dgq_ref, dgkv_ref):
                r[...] = jnp.zeros_like(r)

        ct, st, crt = ct_ref[...], st_ref[...], crt_ref[...]
        dqs = [dq_ref[h] for h in range(N_HEADS)]
        dks = [dk_ref[h] for h in range(N_HEADS)]
        dqab = jnp.concatenate([d * ct for d in dqs] + [d * st for d in dqs], axis=1)
        dkav = jnp.concatenate(dks + [dv_ref[h] for h in range(N_HEADS)], axis=1)
        dk_sum = dks[0] + dks[1] + dks[2] + dks[3]
        cq, ckv, gq, gkv = cq_ref[...].astype(F32), ckv_ref[...].astype(F32), gq_ref[...], gkv_ref[...]
        dwq_ref[...] += _dot_tn(_rms(cq, gq), dqab)
        dwkv_ref[...] += _dot_tn(_rms(ckv, gkv), dkav)
        dcq, dgq = _rms_bwd(cq, gq, _dot_nt(dqab, wq_ref[...]))
        dckv, dgkv = _rms_bwd(ckv, gkv, _dot_nt(dkav, wkv_ref[...]))
        dgq_ref[...] += dgq
        dgkv_ref[...] += dgkv
        dpb_ref[...] = jnp.concatenate([dcq, dckv, dk_sum * crt, dk_sum * st], axis=1).astype(dpb_ref.dtype)

    head_spec = pl.BlockSpec((N_HEADS, ts, HEAD_PAD), lambda i: (0, i, 0))
    wb = Q_RANK + KV_RANK + 2 * HEAD_PAD
    return pl.pallas_call(
        body, name="qkv_bwd", grid=(s // ts,),
        in_specs=[head_spec, head_spec, head_spec,
                  _rows(ts, Q_RANK, COL_B // Q_RANK), _rows(ts, KV_RANK, (COL_B + Q_RANK) // KV_RANK),
                  _full((1, Q_RANK)), _full((1, KV_RANK)), _full((Q_RANK, 2 * hw)), _full((KV_RANK, 2 * hw)),
                  _rows(ts, HEAD_PAD), _rows(ts, HEAD_PAD), _rows(ts, HEAD_PAD)],
        out_specs=[_rows(ts, wb), _full((Q_RANK, 2 * hw)), _full((KV_RANK, 2 * hw)), _full((1, Q_RANK)), _full((1, KV_RANK))],
        out_shape=[jax.ShapeDtypeStruct((s, wb), MXU_DTYPE), jax.ShapeDtypeStruct((Q_RANK, 2 * hw), F32),
                   jax.ShapeDtypeStruct((KV_RANK, 2 * hw), F32), jax.ShapeDtypeStruct((1, Q_RANK), F32),
                   jax.ShapeDtypeStruct((1, KV_RANK), F32)],
        compiler_params=_params(("arbitrary",)),
    )(dq, dk, dv, proj, proj, q_norm, kv_norm, wq, wkv, cq_tab, s_tab, cr_tab)


def _lane_group(width):
    return lax.broadcasted_iota(jnp.int32, (1, width), 1) // (width // 4)


def _shift_down(a, k):
    return pltpu.roll(a, k, 0)


def _shift_up(a, k):
    return pltpu.roll(a, a.shape[0] - k, 0)


def _window_sums(xh, shift):
    s2 = xh + shift(xh, 1)
    s4 = s2 + shift(s2, 2)
    s8 = s4 + shift(s4, 4)
    s16 = s8 + shift(s8, 8)
    grp = _lane_group(xh.shape[1])
    return jnp.where(grp == 0, s2, jnp.where(grp == 1, s4, jnp.where(grp == 2, s8, s16)))


def _pool_count(i, ts):
    grp = _lane_group(BR_WIDTH)
    win = jnp.where(grp == 0, 2.0, jnp.where(grp == 1, 4.0, jnp.where(grp == 2, 8.0, 16.0)))
    t = (i * ts + lax.broadcasted_iota(jnp.int32, (ts, 1), 0)).astype(F32)
    return jnp.minimum(t + 1.0, win)


def _mix_forward(i, ts, r):
    f = {}
    act = lambda name: r[name][...].astype(F32)
    f['gates'] = _sigmoid(act('gate') + r['gate_b'][...])
    sgu, sgv = act('sgu'), act('sgv')
    f['sgu'], f['sgv'] = sgu, sgv
    u_act, f['tu'] = _gelu(sgu)
    vg, f['tv'] = _gelu(sgv)
    mu = jnp.mean(vg, axis=-1, keepdims=True)
    xc = vg - mu
    f['ln_r'] = lax.rsqrt(jnp.mean(xc * xc, axis=-1, keepdims=True) + EPS)
    f['ln_xh'] = xc * f['ln_r']
    vln = f['ln_xh'] * r['ln_g'][...] + r['ln_b'][...]
    tril = lax.broadcasted_iota(jnp.int32, (SG_CHUNK, SG_CHUNK), 1) <= lax.broadcasted_iota(jnp.int32, (SG_CHUNK, SG_CHUNK), 0)
    f['wm'] = [_mxu(jnp.where(tril, r['sg_w'][g], 0.0)) for g in range(SG_GROUPS)]
    f['tril'] = tril
    grp = _lane_group(BR_WIDTH)
    bias = r['sg_bias'][...]
    parts = []
    for ci in range(ts // SG_CHUNK):
        vc = vln[ci * SG_CHUNK:(ci + 1) * SG_CHUNK]
        sc = bias
        for g in range(SG_GROUPS):
            sc = sc + jnp.where(grp == g, _dot(f['wm'][g], vc), 0.0)
        parts.append(sc)
    f['vln'] = vln
    f['sg_s'] = parts[0] if len(parts) == 1 else jnp.concatenate(parts, axis=0)
    f['u_act'] = u_act
    out_b = u_act * f['sg_s']
    first = (i > 0).astype(F32)
    cvx, cvc, cvb = act('cvx'), act('cvc'), act('cvb')
    f['cvx'], f['cvc'], f['cvb'] = cvx, cvc, cvb
    zh = jnp.concatenate([act('hx') * act('hc') * first, cvc * cvx], axis=0)
    f['z1'] = _shift_down(zh, 1)[HALO:]
    f['z2'] = _shift_down(zh, 2)[HALO:]
    f['z0'] = zh[HALO:]
    f['yv'] = r['conv_w'][0:1, :] * f['z2'] + r['conv_w'][1:2, :] * f['z1'] + r['conv_w'][2:3, :] * f['z0']
    out_c = cvb * f['yv']
    p = act('pool')
    ph = jnp.concatenate([act('hp') * first, p], axis=0)
    f['cnt'] = _pool_count(i, ts)
    f['pooled'] = _window_sums(ph, _shift_down)[HALO:] / f['cnt'] - p
    f['mixed'] = _dot(f['pooled'], r['wbd'][...])
    out_d = f['mixed'] * r['pool_scale'][...]
    f['outs'] = [r['o'][...], out_b, out_c, out_d]
    f['ys'] = [_dot(f['outs'][b], r['w_br'][b][...]) for b in range(N_BRANCH)]
    merged = f['gates'][:, 0:D_MODEL] * f['ys'][0]
    for b in range(1, N_BRANCH):
        merged = merged + f['gates'][:, b * D_MODEL:(b + 1) * D_MODEL] * f['ys'][b]
    f['merged'] = merged
    f['mo'] = _dot(merged, r['w_out'][...])
    return f


_MIX_TILE_INPUTS = ['gate', 'sgu', 'sgv', 'cvb', 'cvx', 'cvc', 'pool', 'hx', 'hc', 'hp', 'o']
_MIX_WEIGHTS = ['gate_b', 'ln_g', 'ln_b', 'sg_w', 'sg_bias', 'conv_w', 'wbd', 'pool_scale', 'w_br0', 'w_br1', 'w_br2',
                'w_br3', 'w_out', 'g_post']


def _mix_specs(s, ts):
    c0 = COL_M1 // BR_WIDTH
    prev = lambda col: pl.BlockSpec((HALO, BR_WIDTH), lambda i: (jnp.maximum(i * (ts // HALO) - 1, 0), col))
    tiles = [_rows(ts, N_BRANCH * D_MODEL, 0), _rows(ts, BR_WIDTH, c0), _rows(ts, BR_WIDTH, c0 + 1), _rows(ts, BR_WIDTH, c0 + 2),
             _rows(ts, BR_WIDTH, c0 + 3), _rows(ts, BR_WIDTH, c0 + 4), _rows(ts, BR_WIDTH, c0 + 5),
             prev(c0 + 3), prev(c0 + 4), prev(c0 + 5), _rows(ts, N_HEADS * HEAD_PAD)]
    weights = [_full((1, N_BRANCH * D_MODEL)), _full((1, BR_WIDTH)), _full((1, BR_WIDTH)),
               _full((SG_GROUPS, SG_CHUNK, SG_CHUNK)), _full((SG_CHUNK, BR_WIDTH)), _full((8, BR_WIDTH)),
               _resident((BR_WIDTH, BR_WIDTH)), _full((1, BR_WIDTH)), _resident((N_HEADS * HEAD_PAD, D_MODEL)),
               _resident((BR_WIDTH, D_MODEL)), _resident((BR_WIDTH, D_MODEL)), _resident((BR_WIDTH, D_MODEL)),
               _resident((D_MODEL, D_MODEL)), _full((1, D_MODEL))]
    return tiles, weights


def _mix_refs(refs):
    names = _MIX_TILE_INPUTS + _MIX_WEIGHTS
    r = dict(zip(names, refs[:len(names)]))
    r['w_br'] = [r['w_br0'], r['w_br1'], r['w_br2'], r['w_br3']]
    return r, refs[len(names):]


def _mix_operands(proj, o, lw):
    return ([proj] * 10 + [o] + [lw[n] for n in ['gate_b', 'sg_ln_g', 'sg_ln_b', 'sg_w', 'sg_bias', 'conv_w8', 'wbd',
                                                 'pool_scale', 'w_br_mla_p', 'w_br_sg', 'w_br_conv', 'w_br_pool', 'w_out',
                                                 'norm_mix_post']])


def _mix_fwd(x0, proj, o, lw):
    s = x0.shape[0]
    ts = _tile(s, 512)
    tiles, weights = _mix_specs(s, ts)

    def body(*refs):
        r, (x0_ref, x1_ref) = _mix_refs(refs)
        f = _mix_forward(pl.program_id(0), ts, r)
        x1_ref[...] = x0_ref[...] + _rms(f['mo'], r['g_post'][...])

    return pl.pallas_call(
        body, name="mix_fwd", grid=(s // ts,),
        in_specs=tiles + weights + [_rows(ts, D_MODEL)],
        out_specs=_rows(ts, D_MODEL),
        out_shape=jax.ShapeDtypeStruct((s, D_MODEL), F32),
        compiler_params=_params(("parallel",)),
    )(*_mix_operands(proj, o, lw), x0)


def _mix_bwd(dx1, proj, o, lw):
    s = dx1.shape[0]
    ts = _tile(s, 256)
    tiles, weights = _mix_specs(s, ts)
    hw = N_HEADS * HEAD_PAD

    def body(*refs):
        r, rest = _mix_refs(refs)
        (dx1_ref, dg_ref, dm1_ref, dyv_ref, up_ref, do_ref, delta_ref,
         dgate_b_ref, dln_g_ref, dln_b_ref, dsgw_ref, dsgb_ref, dconv_ref, dwbd_ref, dps_ref,
         dwbr0_ref, dwbr1_ref, dwbr2_ref, dwbr3_ref, dwout_ref, dgpost_ref, dbias_acc) = rest
        i = pl.program_id(0)
        acc_refs = [dgate_b_ref, dln_g_ref, dln_b_ref, dsgw_ref, dsgb_ref, dconv_ref, dwbd_ref, dps_ref,
                    dwbr0_ref, dwbr1_ref, dwbr2_ref, dwbr3_ref, dwout_ref, dgpost_ref, dbias_acc]

        @pl.when(i == 0)
        def _():
            for a in acc_refs:
                a[...] = jnp.zeros_like(a)

        f = _mix_forward(i, ts, r)
        dmo, dgpost = _rms_bwd(f['mo'], r['g_post'][...], dx1_ref[...])
        dgpost_ref[...] += dgpost
        dwout_ref[...] += _dot_tn(f['merged'], dmo)
        dmerged = _dot_nt(dmo, r['w_out'][...])
        dwbr = [dwbr0_ref, dwbr1_ref, dwbr2_ref, dwbr3_ref]
        douts = []
        for b in range(N_BRANCH):
            gb = f['gates'][:, b * D_MODEL:(b + 1) * D_MODEL]
            dgate = dmerged * f['ys'][b] * gb * (1.0 - gb)
            dg_ref[:, b * D_MODEL:(b + 1) * D_MODEL] = dgate.astype(dg_ref.dtype)
            dgate_b_ref[:, b * D_MODEL:(b + 1) * D_MODEL] += jnp.sum(dgate, axis=0, keepdims=True)
            dy = dmerged * gb
            dwbr[b][...] += _dot_tn(f['outs'][b], dy)
            douts.append(_dot_nt(dy, r['w_br'][b][...]))
        do = douts[0]
        do_ref[...] = do.astype(do_ref.dtype)
        prod = do * f['outs'][0]
        for h in range(N_HEADS):
            delta_ref[h] = jnp.sum(prod[:, h * HEAD_PAD:(h + 1) * HEAD_PAD], axis=1, keepdims=True)
        grp = _lane_group(BR_WIDTH)
        ds = douts[1] * f['u_act']
        dsgu = douts[1] * f['sg_s'] * _gelu_grad(f['sgu'], f['tu'])
        dvln_parts = []
        for ci in range(ts // SG_CHUNK):
            rows = slice(ci * SG_CHUNK, (ci + 1) * SG_CHUNK)
            ds_c, vln_c = ds[rows], f['vln'][rows]
            dvln_c = jnp.zeros((SG_CHUNK, BR_WIDTH), F32)
            for g in range(SG_GROUPS):
                dvln_c = dvln_c + jnp.where(grp == g, _dot_tn(f['wm'][g], ds_c), 0.0)
                dsgw_ref[g] += jnp.where(f['tril'], _dot_nt(jnp.where(grp == g, ds_c, 0.0), vln_c), 0.0)
            dbias_acc[...] += ds_c
            dvln_parts.append(dvln_c)
        dvln = dvln_parts[0] if len(dvln_parts) == 1 else jnp.concatenate(dvln_parts, axis=0)
        dln_g_ref[...] += jnp.sum(dvln * f['ln_xh'], axis=0, keepdims=True)
        dln_b_ref[...] += jnp.sum(dvln, axis=0, keepdims=True)
        dxh = dvln * r['ln_g'][...]
        dvg = f['ln_r'] * (dxh - jnp.mean(dxh, axis=-1, keepdims=True)
                           - f['ln_xh'] * jnp.mean(dxh * f['ln_xh'], axis=-1, keepdims=True))
        dsgv = dvg * _gelu_grad(f['sgv'], f['tv'])
        dcvb = douts[2] * f['yv']
        dyv = douts[2] * f['cvb']
        dyv_ref[...] = dyv
        for kk, zk in enumerate((f['z2'], f['z1'], f['z0'])):
            dconv_ref[kk:kk + 1, :] += jnp.sum(dyv * zk, axis=0, keepdims=True)
        dps_ref[...] += jnp.sum(douts[3] * f['mixed'], axis=0, keepdims=True)
        dmixed = douts[3] * r['pool_scale'][...]
        dwbd_ref[...] += _dot_tn(f['pooled'], dmixed)
        up_ref[...] = _dot_nt(dmixed, r['wbd'][...]) / f['cnt']
        dm1_ref[...] = jnp.concatenate([dsgu, dsgv, dcvb], axis=1).astype(dm1_ref.dtype)

        @pl.when(i == pl.num_programs(0) - 1)
        def _():
            lane = lax.broadcasted_iota(jnp.int32, (1, SG_CHUNK), 1)
            db = dbias_acc[...]
            out = jnp.zeros((SG_CHUNK, SG_CHUNK), F32)
            for g in range(SG_GROUPS):
                out = out + jnp.where(lane == g, jnp.sum(jnp.where(grp == g, db, 0.0), axis=1, keepdims=True), 0.0)
            dsgb_ref[...] = out

    acc = lambda shape: (_full(shape), jax.ShapeDtypeStruct(shape, F32))
    accs = [acc((1, N_BRANCH * D_MODEL)), acc((1, BR_WIDTH)), acc((1, BR_WIDTH)), acc((SG_GROUPS, SG_CHUNK, SG_CHUNK)),
            acc((SG_CHUNK, SG_CHUNK)), acc((8, BR_WIDTH)), acc((BR_WIDTH, BR_WIDTH)), acc((1, BR_WIDTH)),
            acc((hw, D_MODEL)), acc((BR_WIDTH, D_MODEL)), acc((BR_WIDTH, D_MODEL)), acc((BR_WIDTH, D_MODEL)),
            acc((D_MODEL, D_MODEL)), acc((1, D_MODEL))]
    tile_outs = [(_rows(ts, N_BRANCH * D_MODEL), jax.ShapeDtypeStruct((s, N_BRANCH * D_MODEL), MXU_DTYPE)),
                 (_rows(ts, 3 * BR_WIDTH), jax.ShapeDtypeStruct((s, 3 * BR_WIDTH), MXU_DTYPE)),
                 (_rows(ts, BR_WIDTH), jax.ShapeDtypeStruct((s, BR_WIDTH), F32)),
                 (_rows(ts, BR_WIDTH), jax.ShapeDtypeStruct((s, BR_WIDTH), F32)),
                 (_rows(ts, hw), jax.ShapeDtypeStruct((s, hw), MXU_DTYPE)),
                 (pl.BlockSpec((N_HEADS, ts, 1), lambda i: (0, i, 0)), jax.ShapeDtypeStruct((N_HEADS, s, 1), F32))]
    outs = tile_outs + accs
    return pl.pallas_call(
        body, name="mix_bwd", grid=(s // ts,),
        in_specs=tiles + weights + [_rows(ts, D_MODEL)],
        out_specs=[o_[0] for o_ in outs], out_shape=[o_[1] for o_ in outs],
        scratch_shapes=[pltpu.VMEM((SG_CHUNK, BR_WIDTH), F32)],
        compiler_params=_params(("arbitrary",), 60 * 1024 * 1024),
    )(*_mix_operands(proj, o, lw), dx1)


def _shift_bwd(dyv, upool, proj, conv_w8):
    s = dyv.shape[0]
    ts = _tile(s, 512)
    nb = s // HALO
    c0 = COL_M1 // BR_WIDTH

    def body(dyv_ref, dyvn_ref, up_ref, upn_ref, cvx_ref, cvc_ref, cw_ref, out_ref):
        i = pl.program_id(0)
        last = (i < pl.num_programs(0) - 1).astype(F32)
        dh = jnp.concatenate([dyv_ref[...], dyvn_ref[...] * last], axis=0)
        dz = (cw_ref[2:3, :] * dh + cw_ref[1:2, :] * _shift_up(dh, 1) + cw_ref[0:1, :] * _shift_up(dh, 2))[:ts]
        up = up_ref[...]
        uh = jnp.concatenate([up, upn_ref[...] * last], axis=0)
        dpool = _window_sums(uh, _shift_up)[:ts] - up * _pool_count(i, ts)
        out_ref[...] = jnp.concatenate([dz * cvc_ref[...].astype(F32), dz * cvx_ref[...].astype(F32), dpool],
                                       axis=1).astype(out_ref.dtype)

    nxt = pl.BlockSpec((HALO, BR_WIDTH), lambda i: (jnp.minimum((i + 1) * (ts // HALO), nb - 1), 0))
    return pl.pallas_call(
        body, name="shift_bwd", grid=(s // ts,),
        in_specs=[_rows(ts, BR_WIDTH), nxt, _rows(ts, BR_WIDTH), nxt, _rows(ts, BR_WIDTH, c0 + 3), _rows(ts, BR_WIDTH, c0 + 4),
                  _full((8, BR_WIDTH))],
        out_specs=_rows(ts, 3 * BR_WIDTH),
        out_shape=jax.ShapeDtypeStruct((s, 3 * BR_WIDTH), MXU_DTYPE),
        compiler_params=_params(("parallel",)),
    )(dyv, dyv, upool, upool, proj, proj, conv_w8)


def _ffn2(a, w2, x1, g):
    s = a.shape[0]
    ts = _tile(s, 512)

    def body(a_ref, w_ref, x1_ref, g_ref, x2_ref, f_ref):
        f = _dot(_relu_sq(a_ref[...]), w_ref[...])
        f_ref[...] = f
        x2_ref[...] = x1_ref[...] + _rms(f, g_ref[...])

    return pl.pallas_call(
        body, name="ffn2", grid=(s // ts,),
        in_specs=[_rows(ts, D_FF), _resident((D_FF, D_MODEL)), _rows(ts, D_MODEL), _full((1, D_MODEL))],
        out_specs=[_rows(ts, D_MODEL), _rows(ts, D_MODEL)],
        out_shape=[jax.ShapeDtypeStruct((s, D_MODEL), F32)] * 2,
        compiler_params=_params(("parallel",)),
    )(a, w2, x1, g)


def _ffn2_bwd(dx2, f, g, a, w2t, carried=None):
    s = a.shape[0]
    ts = _tile(s, 512)

    def body(dx2_ref, f_ref, g_ref, a_ref, w_ref, df_ref, da_ref, dg_ref):
        @pl.when(pl.program_id(0) == 0)
        def _():
            dg_ref[...] = jnp.zeros_like(dg_ref)

        df, dg = _rms_bwd(f_ref[...], g_ref[...], dx2_ref[...])
        dg_ref[...] += dg
        df_ref[...] = df.astype(df_ref.dtype)
        da_ref[...] = (_dot(df, w_ref[...]) * (2.0 * jnp.maximum(a_ref[...].astype(F32), 0.0))).astype(da_ref.dtype)

    return _call_carrying(
        body, carried, name="ffn2_bwd", grid=(s // ts,),
        in_specs=[_rows(ts, D_MODEL), _rows(ts, D_MODEL), _full((1, D_MODEL)), _rows(ts, D_FF), _resident((D_MODEL, D_FF))],
        out_specs=[_rows(ts, D_MODEL), _rows(ts, D_FF), _full((1, D_MODEL))],
        out_shape=[jax.ShapeDtypeStruct((s, D_MODEL), MXU_DTYPE), jax.ShapeDtypeStruct((s, D_FF), MXU_DTYPE),
                   jax.ShapeDtypeStruct((1, D_MODEL), F32)],
        operands=(dx2, f, g, a, w2t), semantics=("arbitrary",))


def _norm_in_bwd(name, pieces, x, g, dres):
    s = x.shape[0]
    ts = _tile(s, 512)
    n = len(pieces)

    def body(*refs):
        d_refs, w_refs = refs[:n], refs[n:2 * n]
        x_ref, g_ref, dres_ref, dx_ref, dg_ref = refs[2 * n:]

        @pl.when(pl.program_id(0) == 0)
        def _():
            dg_ref[...] = jnp.zeros_like(dg_ref)

        dh = _dot(d_refs[0][...], w_refs[0][...])
        for p in range(1, n):
            dh = dh + _dot(d_refs[p][...], w_refs[p][...])
        dx, dg = _rms_bwd(x_ref[...], g_ref[...], dh)
        dg_ref[...] += dg
        dx_ref[...] = dres_ref[...] + dx

    return pl.pallas_call(
        body, name=name, grid=(s // ts,),
        in_specs=[_rows(ts, d.shape[1]) for d, _ in pieces] + [_resident(w.shape) for _, w in pieces]
        + [_rows(ts, D_MODEL), _full((1, D_MODEL)), _rows(ts, D_MODEL)],
        out_specs=[_rows(ts, D_MODEL), _full((1, D_MODEL))],
        out_shape=[jax.ShapeDtypeStruct((s, D_MODEL), F32), jax.ShapeDtypeStruct((1, D_MODEL), F32)],
        compiler_params=_params(("arbitrary",)),
    )(*[d for d, _ in pieces], *[w for _, w in pieces], x, g, dres)


def _loss_and_grad(y, target):
    s = y.shape[0]
    ts = _tile(s, 512)

    def body(y_ref, t_ref, dy_ref, loss_ref):
        @pl.when(pl.program_id(0) == 0)
        def _():
            loss_ref[...] = jnp.zeros_like(loss_ref)

        err = y_ref[...] - t_ref[...]
        dy_ref[...] = err * (1.0 / D_MODEL)
        loss_ref[...] += 0.5 * jnp.sum(jnp.mean(err * err, axis=-1, keepdims=True), axis=0, keepdims=True)

    dy, loss = pl.pallas_call(
        body, name="loss", grid=(s // ts,),
        in_specs=[_rows(ts, D_MODEL), _rows(ts, D_MODEL)],
        out_specs=[_rows(ts, D_MODEL), _full((8, 128))],
        out_shape=[jax.ShapeDtypeStruct((s, D_MODEL), F32), jax.ShapeDtypeStruct((8, 128), F32)],
        compiler_params=_params(("arbitrary",)),
    )(y, target)
    return loss[0, 0], dy


_W_IN_SPLITS = [256, 384, 416, 672, 928, 1184, 1440, 1696, 1952]


def _rope_swap(w):
    half = QK_ROPE // 2
    return jnp.concatenate([-w[..., half:], w[..., :half]], axis=-1)


def _rope_unswap(d):
    half = QK_ROPE // 2
    return jnp.concatenate([d[..., half:], -d[..., :half]], axis=-1)


def _zeros_like_cols(w, n):
    return jnp.zeros(w.shape[:-1] + (n,), w.dtype)


def _derive_weights(w):
    md = MXU_DTYPE
    nl = w['w_in'].shape[0]
    c_q, c_kv, k_r, sg_u, sg_v, cv_x, cv_b, cv_c, pool, gate = jnp.split(w['w_in'].astype(md), _W_IN_SPLITS, axis=-1)
    pad_rope = lambda r: jnp.concatenate([_zeros_like_cols(r, QK_NOPE), r, _zeros_like_cols(r, HEAD_PAD - QK_NOPE - QK_ROPE)], -1)
    w_in_p = jnp.concatenate([gate, sg_u, sg_v, cv_b, cv_x, cv_c, pool, c_q, c_kv, pad_rope(k_r), pad_rope(_rope_swap(k_r))], -1)
    wq = w['w_uq'].astype(md).reshape(nl, Q_RANK, N_HEADS, QK_NOPE + QK_ROPE)
    nope, rope_w = wq[..., :QK_NOPE], wq[..., QK_NOPE:]
    wq_a = jnp.concatenate([nope, rope_w, _zeros_like_cols(nope, 32)], -1).reshape(nl, Q_RANK, N_HEADS * HEAD_PAD)
    wq_b = pad_rope(_rope_swap(rope_w)).reshape(nl, Q_RANK, N_HEADS * HEAD_PAD)
    wkv = w['w_ukv'].astype(md).reshape(nl, KV_RANK, N_HEADS, QK_NOPE + V_HEAD)
    pad_half = lambda r: jnp.concatenate([r, _zeros_like_cols(r, HEAD_PAD - r.shape[-1])], -1).reshape(nl, KV_RANK, N_HEADS * HEAD_PAD)
    w_br_mla = w['w_br_mla'].astype(md).reshape(nl, N_HEADS, V_HEAD, D_MODEL)
    w_br_mla_p = jnp.concatenate([w_br_mla, jnp.zeros_like(w_br_mla)], axis=2).reshape(nl, N_HEADS * HEAD_PAD, D_MODEL)
    eye = jnp.eye(4, dtype=md)
    wbd = (w['pool_w'].astype(md)[:, :, :, None, :] * eye[None, :, None, :, None]).reshape(nl, BR_WIDTH, BR_WIDTH)
    row = lambda a: a.astype(F32)[:, None, :]
    w_in_pt = jnp.swapaxes(w_in_p, 1, 2)
    return dict(
        w_in_p=w_in_p, wt_g=w_in_pt[:, COL_G:COL_M1], wt_m1=w_in_pt[:, COL_M1:COL_M2], wt_m2=w_in_pt[:, COL_M2:COL_B],
        wt_b=w_in_pt[:, COL_B:],
        wq=jnp.concatenate([wq_a, wq_b], -1), wkv=jnp.concatenate([pad_half(wkv[..., :QK_NOPE]), pad_half(wkv[..., QK_NOPE:])], -1),
        w_br_mla_p=w_br_mla_p, w_br_sg=w['w_br_sg'].astype(md), w_br_conv=w['w_br_conv'].astype(md),
        w_br_pool=w['w_br_pool'].astype(md), wbd=wbd, w_out=w['w_out'].astype(md),
        w_ff1=w['w_ff1'].astype(md), w_ff1t=jnp.swapaxes(w['w_ff1'].astype(md), 1, 2),
        w_ff2=w['w_ff2'].astype(md), w_ff2t=jnp.swapaxes(w['w_ff2'].astype(md), 1, 2),
        norm_mix_pre=row(w['norm_mix_pre']), gate_b=row(w['gate_b']), q_norm=row(w['q_norm']), kv_norm=row(w['kv_norm']),
        sg_ln_g=row(w['sg_ln_g']), sg_ln_b=row(w['sg_ln_b']), sg_w=w['sg_w'].astype(F32),
        sg_bias=jnp.repeat(jnp.swapaxes(w['sg_b'].astype(F32), 1, 2), BR_WIDTH // SG_GROUPS, axis=2),
        conv_w8=jnp.pad(w['conv_w'].astype(F32), ((0, 0), (0, 5), (0, 0))), pool_scale=row(w['pool_scale']),
        norm_mix_post=row(w['norm_mix_post']), norm_ffn_pre=row(w['norm_ffn_pre']), norm_ffn_post=row(w['norm_ffn_post']),
    )


def _rope_tables(positions):
    inv_freq = ROPE_BASE ** (-jnp.arange(0, QK_ROPE, 2, dtype=F32) / QK_ROPE)
    ang = positions.astype(F32)[:, None] * inv_freq
    cos, sin = jnp.cos(ang), jnp.sin(ang)
    n = positions.shape[0]
    ones, z64, z32 = jnp.ones((n, QK_NOPE), F32), jnp.zeros((n, QK_NOPE), F32), jnp.zeros((n, 32), F32)
    return (jnp.concatenate([ones, cos, cos, z32], 1), jnp.concatenate([z64, sin, sin, z32], 1),
            jnp.concatenate([z64, cos, cos, z32], 1))


def _reference_layout_grads(g):
    gate, dm1, dm2, dpb = g['dw_in_pieces']
    nl = gate.shape[0]
    sg_u, sg_v, cv_b = jnp.split(dm1, 3, axis=-1)
    cv_x, cv_c, pool = jnp.split(dm2, 3, axis=-1)
    c_q, c_kv, kr, krs = jnp.split(dpb, [Q_RANK, Q_RANK + KV_RANK, Q_RANK + KV_RANK + HEAD_PAD], axis=-1)
    rope_cols = slice(QK_NOPE, QK_NOPE + QK_ROPE)
    k_r = kr[..., rope_cols] + _rope_unswap(krs[..., rope_cols])
    w_in = jnp.concatenate([c_q, c_kv, k_r, sg_u, sg_v, cv_x, cv_b, cv_c, pool, gate], -1)
    hw = N_HEADS * HEAD_PAD
    dqa = g['dwq'][..., :hw].reshape(nl, Q_RANK, N_HEADS, HEAD_PAD)
    dqb = g['dwq'][..., hw:].reshape(nl, Q_RANK, N_HEADS, HEAD_PAD)
    w_uq = jnp.concatenate([dqa[..., :QK_NOPE], dqa[..., rope_cols] + _rope_unswap(dqb[..., rope_cols])], -1)
    dka = g['dwkv'][..., :hw].reshape(nl, KV_RANK, N_HEADS, HEAD_PAD)
    dva = g['dwkv'][..., hw:].reshape(nl, KV_RANK, N_HEADS, HEAD_PAD)
    w_ukv = jnp.concatenate([dka[..., :QK_NOPE], dva[..., :V_HEAD]], -1)
    w_br_mla = g['dw_br_mla_p'].reshape(nl, N_HEADS, HEAD_PAD, D_MODEL)[:, :, :V_HEAD]
    dwbd = g['dwbd'].reshape(nl, 4, 64, 4, 64)
    pool_w = jnp.stack([dwbd[:, k, :, k, :] for k in range(4)], axis=1)
    sq = lambda a: a[:, 0, :]
    return dict(
        norm_mix_pre=sq(g['dg_pre']), w_in=w_in, gate_b=sq(g['dgate_b']), q_norm=sq(g['dq_norm']),
        w_uq=w_uq.reshape(nl, Q_RANK, -1), kv_norm=sq(g['dkv_norm']), w_ukv=w_ukv.reshape(nl, KV_RANK, -1),
        w_br_mla=w_br_mla.reshape(nl, N_HEADS * V_HEAD, D_MODEL), sg_ln_g=sq(g['dln_g']), sg_ln_b=sq(g['dln_b']),
        sg_w=g['dsg_w'], sg_b=jnp.swapaxes(g['dsg_b'][:, :, :SG_GROUPS], 1, 2), w_br_sg=g['dw_br_sg'],
        conv_w=g['dconv_w'][:, :3], w_br_conv=g['dw_br_conv'], pool_w=pool_w, pool_scale=sq(g['dpool_scale']),
        w_br_pool=g['dw_br_pool'], w_out=g['dw_out'], norm_mix_post=sq(g['dg_post']), norm_ffn_pre=sq(g['dg_fpre']),
        w_ff1=g['dw_ff1'], w_ff2=g['dw_ff2'], norm_ffn_post=sq(g['dg_fpost']))


def _layer_forward(x0, lw, tabs, carried):
    proj = _mm("in_proj", x0, lw['w_in_p'], tm=1024, tn=N_PROJ, prologue=_rms, rows=(lw['norm_mix_pre'],))
    q, k, v = _qkv_prep(proj, lw['q_norm'], lw['kv_norm'], lw['wq'], lw['wkv'], *tabs)
    (o, lse), carried_out = _attn_fwd(q, k, v, carried)
    x1 = _mix_fwd(x0, proj, o, lw)
    a = _mm("ffn1", x1, lw['w_ff1'], tm=1024, tn=D_FF, prologue=_rms, rows=(lw['norm_ffn_pre'],))
    x2, f = _ffn2(a, lw['w_ff2'], x1, lw['norm_ffn_post'])
    return x2, dict(x0=x0, proj=proj, q=q, k=k, v=v, o=o, lse=lse, x1=x1, a=a, f=f), carried_out


def _layer_backward(dx2, lw, sv, tabs, early, late):
    g = {}
    (df, da, g['dg_fpost']), early_out = _ffn2_bwd(dx2, sv['f'], lw['norm_ffn_post'], sv['a'], lw['w_ff2t'], early)
    carried = late(early_out)
    g['dw_ff2'] = _mm_tn("dw_ff2", sv['a'], df, tm=512, tn=1024, prologue=_relu_sq)
    dx1, g['dg_fpre'] = _norm_in_bwd("ffn1_bwd", [(da, lw['w_ff1t'])], sv['x1'], lw['norm_ffn_pre'], dx2)
    g['dw_ff1'] = _mm_tn("dw_ff1", sv['x1'], da, tm=512, tn=2048, prologue=_rms, rows=(lw['norm_ffn_pre'],))
    (dgate, dm1, dyv, upool, do, delta, g['dgate_b'], g['dln_g'], g['dln_b'], g['dsg_w'], g['dsg_b'], g['dconv_w'],
     g['dwbd'], g['dpool_scale'], g['dw_br_mla_p'], g['dw_br_sg'], g['dw_br_conv'], g['dw_br_pool'], g['dw_out'],
     g['dg_post']) = _mix_bwd(dx1, sv['proj'], sv['o'], lw)
    dm2 = _shift_bwd(dyv, upool, sv['proj'], lw['conv_w8'])
    (dq, dk, dv), carried_out = _attn_bwd(sv['q'], sv['k'], sv['v'], do, sv['lse'], delta, carried)
    dpb, g['dwq'], g['dwkv'], g['dq_norm'], g['dkv_norm'] = _qkv_bwd(
        dq, dk, dv, sv['proj'], lw['q_norm'], lw['kv_norm'], lw['wq'], lw['wkv'], *tabs)
    pieces = [(dgate, lw['wt_g']), (dm1, lw['wt_m1']), (dm2, lw['wt_m2']), (dpb, lw['wt_b'])]
    dx0, g['dg_pre'] = _norm_in_bwd("in_proj_bwd", pieces, sv['x0'], lw['norm_mix_pre'], dx1)
    g['dw_in_pieces'] = [_mm_tn("dw_in_%d" % n, sv['x0'], d, tm=512, tn=2048, prologue=_rms, rows=(lw['norm_mix_pre'],))
                         for n, (d, _) in enumerate(pieces)]
    return dx0, g, carried_out


class _StepPlan(NamedTuple):
    n_layers: int
    weights_exchange: Callable
    weights_from: Callable
    grads_ready: Callable
    first_done: Callable
    second_done: Callable


def _local_step(x, positions, target, plan):
    tabs = _rope_tables(positions)
    derive = lambda w: {n: a[0] for n, a in _derive_weights(w).items()}
    first = plan.weights_exchange(0)
    weights = plan.weights_from(0, None if first is None else _run_exchange("gather_weights", first))
    derived, saved = [], []
    for l in range(plan.n_layers):
        derived.append(derive(weights))
        coming = plan.weights_exchange(l + 1) if l + 1 < plan.n_layers else None
        x, sv, arrived = _layer_forward(x, derived[l], tabs, coming)
        saved.append(sv)
        if l + 1 < plan.n_layers:
            weights = plan.weights_from(l + 1, arrived)
    loss, dx = _loss_and_grad(x, target)
    pending = None
    for l in reversed(range(plan.n_layers)):
        if pending is None:
            dx, g, _ = _layer_backward(dx, derived[l], saved[l], tabs, None, lambda _: None)
        else:
            dx, g, arrived = _layer_backward(dx, derived[l], saved[l], tabs, pending[1],
                                             functools.partial(plan.first_done, pending[0]))
            plan.second_done(pending[0], arrived)
        lead = lambda a: [b[None] for b in a] if isinstance(a, list) else a[None]
        going = plan.grads_ready(l, _reference_layout_grads({n: lead(a) for n, a in g.items()}))
        pending = None if going is None else (l, going)
    if pending is not None:
        second = plan.first_done(pending[0], _run_exchange("grads_to_sibling", pending[1]))
        plan.second_done(pending[0], _run_exchange("grads_to_chips", second))
    return loss, dx


def _relative_peers():
    x, y = lax.axis_index("x"), lax.axis_index("y")
    return {1: (x, 1 - y), 2: (1 - x, y), 3: (1 - x, 1 - y)}


def _for_my_core(fn):
    def run():
        for half in (0, 1):
            pl.when(lax.axis_index("c") == half)(functools.partial(fn, half))
    return run


def _gather_exchange(split, whole):
    ns, nw = len(split), len(whole)
    n = ns + nw

    def phases(ins, outs, sems):
        ici_send, ici_recv, d2d_send, d2d_recv, own_send, own_recv = sems
        x, y, c = lax.axis_index("x"), lax.axis_index("y"), lax.axis_index("c")
        peers = _relative_peers()

        def rows(ref, which):
            h = ref.shape[-2] // 2
            return ref.at[(slice(None),) * (len(ref.shape) - 2) + (slice(which * h, (which + 1) * h), slice(None))]

        def own(k):
            return pltpu.make_async_remote_copy(src_ref=ins[k], dst_ref=outs[k].at[0], send_sem=own_send.at[k],
                                                recv_sem=own_recv.at[k], device_id=(x, y, 1 - c), device_id_type=MESH)

        def over_ici(k, r, half):
            src = rows(ins[k], half) if k < ns else ins[k]
            dst = rows(outs[k].at[r], half) if k < ns else outs[k].at[r]
            return pltpu.make_async_remote_copy(src_ref=src, dst_ref=dst, send_sem=ici_send.at[3 * k + r - 1],
                                                recv_sem=ici_recv.at[3 * k + r - 1], device_id=(*peers[r], c), device_id_type=MESH)

        def to_sibling(k, r, half):
            landed = rows(outs[k].at[r], half)
            return pltpu.make_async_remote_copy(src_ref=landed, dst_ref=landed, send_sem=d2d_send.at[3 * k + r - 1],
                                                recv_sem=d2d_recv.at[3 * k + r - 1], device_id=(x, y, 1 - c), device_id_type=MESH)

        def start(half):
            for k in range(n):
                own(k).start()
                for r in peers:
                    over_ici(k, r, half).start()

        def middle(half):
            for k in range(n):
                for r in peers:
                    over_ici(k, r, half).wait_recv()
                    if k < ns:
                        to_sibling(k, r, half).start()

        def finish(half):
            for k in range(n):
                own(k).wait()
                for r in peers:
                    if k < ns:
                        to_sibling(k, r, 1 - half).wait_recv()
                        to_sibling(k, r, half).wait_send()
                    over_ici(k, r, half).wait_send()

        return _for_my_core(start), _for_my_core(middle), _for_my_core(finish)

    arrs = list(split) + list(whole)
    return _Exchange(
        operands=arrs, out_shape=[jax.ShapeDtypeStruct((4,) + a.shape, a.dtype) for a in arrs],
        scratch_shapes=[pltpu.SemaphoreType.DMA((3 * n,)), pltpu.SemaphoreType.DMA((3 * n,)), pltpu.SemaphoreType.DMA((3 * ns,)),
                        pltpu.SemaphoreType.DMA((3 * ns,)), pltpu.SemaphoreType.DMA((n,)), pltpu.SemaphoreType.DMA((n,))],
        phases=phases)


def _absolute_chip_order(relative):
    me = 2 * lax.axis_index("x") + lax.axis_index("y")
    return jnp.stack([lax.dynamic_index_in_dim(relative, jnp.bitwise_xor(me, chip), 0, keepdims=False) for chip in range(4)])


REDUCE_STEPS = 8


def _sibling_exchange(arrs):
    n = len(arrs)

    def phases(ins, theirs, sems):
        send_sems, recv_sems = sems
        x, y, c = lax.axis_index("x"), lax.axis_index("y"), lax.axis_index("c")

        def copy(k, my_half):
            h = ins[k].shape[1] // 2
            return pltpu.make_async_remote_copy(src_ref=ins[k].at[:, (1 - my_half) * h:(2 - my_half) * h, :], dst_ref=theirs[k],
                                                send_sem=send_sems.at[k], recv_sem=recv_sems.at[k],
                                                device_id=(x, y, 1 - c), device_id_type=MESH)

        def start(my_half):
            for k in range(n):
                copy(k, my_half).start()

        def finish(my_half):
            for k in range(n):
                copy(k, my_half).wait()

        return _for_my_core(start), lambda: None, _for_my_core(finish)

    return _Exchange(operands=list(arrs),
                     out_shape=[jax.ShapeDtypeStruct((a.shape[0], a.shape[1] // 2, a.shape[2]), a.dtype) for a in arrs],
                     scratch_shapes=[pltpu.SemaphoreType.DMA((n,)), pltpu.SemaphoreType.DMA((n,))], phases=phases)


def _add_sibling(name, arrs, theirs):
    n, steps = len(arrs), REDUCE_STEPS

    def body(*refs):
        for mine_ref, theirs_ref, out_ref in zip(refs[:n], refs[n:2 * n], refs[2 * n:]):
            out_ref[...] = (mine_ref[...] + theirs_ref[...]).astype(out_ref.dtype)

    block = lambda t: (4, t.shape[1] // steps, t.shape[2])
    return pl.pallas_call(
        body, name=name, grid=(steps,),
        in_specs=[pl.BlockSpec(block(t), lambda i: (0, lax.axis_index("c") * steps + i, 0)) for t in theirs]
        + [pl.BlockSpec(block(t), lambda i: (0, i, 0)) for t in theirs],
        out_specs=[pl.BlockSpec(block(t), lambda i: (0, i, 0)) for t in theirs],
        out_shape=[jax.ShapeDtypeStruct(t.shape, WIRE_DTYPE) for t in theirs],
        compiler_params=_params(("parallel",)))(*arrs, *theirs)


def _scatter_exchange(arrs):
    n = len(arrs)

    def phases(ins, outs, sems):
        send_sems, recv_sems = sems
        c = lax.axis_index("c")
        peers = _relative_peers()

        def copy(k, r):
            px, py = peers[r]
            return pltpu.make_async_remote_copy(src_ref=ins[k].at[2 * px + py], dst_ref=outs[k].at[r - 1],
                                                send_sem=send_sems.at[3 * k + r - 1], recv_sem=recv_sems.at[3 * k + r - 1],
                                                device_id=(px, py, c), device_id_type=MESH)

        def start():
            for k in range(n):
                for r in peers:
                    copy(k, r).start()

        def finish():
            for k in range(n):
                for r in peers:
                    copy(k, r).wait()

        return start, lambda: None, finish

    return _Exchange(operands=list(arrs), out_shape=[jax.ShapeDtypeStruct((3,) + a.shape[1:], a.dtype) for a in arrs],
                     scratch_shapes=[pltpu.SemaphoreType.DMA((3 * n,)), pltpu.SemaphoreType.DMA((3 * n,))], phases=phases)


def _sum_chips(name, chip_sums, arrived):
    n, steps = len(chip_sums), REDUCE_STEPS

    def body(*refs):
        for own_ref, arrived_ref, out_ref in zip(refs[:n], refs[n:2 * n], refs[2 * n:]):
            acc = own_ref[...].astype(F32)
            for r in range(3):
                acc = acc + arrived_ref[r].astype(F32)
            out_ref[...] = acc

    rows = lambda s: s.shape[1] // steps
    chip = lambda: 2 * lax.axis_index("x") + lax.axis_index("y")
    return pl.pallas_call(
        body, name=name, grid=(steps,),
        in_specs=[pl.BlockSpec((None, rows(s), s.shape[2]), lambda i: (chip(), i, 0)) for s in chip_sums]
        + [pl.BlockSpec((3, rows(s), s.shape[2]), lambda i: (0, i, 0)) for s in chip_sums],
        out_specs=[pl.BlockSpec((rows(s), s.shape[2]), lambda i: (lax.axis_index("c") * steps + i, 0)) for s in chip_sums],
        out_shape=[jax.ShapeDtypeStruct((2 * s.shape[1], s.shape[2]), F32) for s in chip_sums],
        compiler_params=_params(("parallel",)))(*chip_sums, *arrived)


def _join_siblings(name, bufs):
    n = len(bufs)

    def body(*refs):
        outs = refs[n:2 * n]
        send_sems, recv_sems = refs[2 * n:]
        x, y, c = lax.axis_index("x"), lax.axis_index("y"), lax.axis_index("c")

        def exchange(my_half):
            copies = []
            for k in range(n):
                h = outs[k].shape[0] // 2
                mine = outs[k].at[my_half * h:(my_half + 1) * h, :]
                theirs = outs[k].at[(1 - my_half) * h:(2 - my_half) * h, :]
                cp = pltpu.make_async_remote_copy(src_ref=mine, dst_ref=mine, send_sem=send_sems.at[k],
                                                  recv_sem=recv_sems.at[k], device_id=(x, y, 1 - c), device_id_type=MESH)
                cp.start()
                arrival = pltpu.make_async_remote_copy(src_ref=theirs, dst_ref=theirs, send_sem=send_sems.at[k],
                                                       recv_sem=recv_sems.at[k], device_id=(x, y, 1 - c), device_id_type=MESH)
                copies.append((cp, arrival))
            for cp, arrival in copies:
                arrival.wait_recv()
                cp.wait_send()

        for half in (0, 1):
            pl.when(c == half)(functools.partial(exchange, half))

    return pl.pallas_call(
        body, name=name, in_specs=[HBM] * n, out_specs=[HBM] * n,
        out_shape=[jax.ShapeDtypeStruct(b.shape, b.dtype) for b in bufs], input_output_aliases={k: k for k in range(n)},
        scratch_shapes=[pltpu.SemaphoreType.DMA((n,)), pltpu.SemaphoreType.DMA((n,))],
    )(*bufs)


def _gather_all_exchange(a):
    def phases(ins, outs, scratch):
        (a_ref,), (out_ref,) = ins, outs
        staging, send_sems, recv_sems, local_sem = scratch
        x, y, c = lax.axis_index("x"), lax.axis_index("y"), lax.axis_index("c")
        me = 4 * x + 2 * y + c
        flips = [(fx, fy, fc) for fx in (0, 1) for fy in (0, 1) for fc in (0, 1)][1:]
        peers = [(x ^ fx, y ^ fy, c ^ fc) for fx, fy, fc in flips]

        def copy(j):
            px, py, pc = peers[j]
            return pltpu.make_async_remote_copy(src_ref=a_ref, dst_ref=out_ref.at[me], send_sem=send_sems.at[j],
                                                recv_sem=recv_sems.at[j], device_id=(px, py, pc), device_id_type=MESH)

        def arrival(j):
            px, py, pc = peers[j]
            return pltpu.make_async_remote_copy(src_ref=a_ref, dst_ref=out_ref.at[4 * px + 2 * py + pc], send_sem=send_sems.at[j],
                                                recv_sem=recv_sems.at[j], device_id=(px, py, pc), device_id_type=MESH)

        own = pltpu.make_async_copy(staging, out_ref.at[me], local_sem)

        def start():
            load = pltpu.make_async_copy(a_ref, staging, local_sem)
            load.start()
            load.wait()
            own.start()
            for j in range(7):
                copy(j).start()

        def finish():
            for j in range(7):
                arrival(j).wait_recv()
            for j in range(7):
                copy(j).wait_send()
            own.wait()

        return start, lambda: None, finish

    return _Exchange(operands=[a], out_shape=[jax.ShapeDtypeStruct((8,) + a.shape, a.dtype)],
                     scratch_shapes=[pltpu.VMEM(a.shape, a.dtype), pltpu.SemaphoreType.DMA((7,)), pltpu.SemaphoreType.DMA((7,)),
                                     pltpu.SemaphoreType.DMA], phases=phases)


def _rowwise_call(name, fn, slots, out_shapes, steps, carried=None):
    n_in, n_out = [len(s) for s in slots], [len(o) for o in out_shapes]

    def spec(shape):
        if len(shape) == 3:
            return pl.BlockSpec((shape[0], shape[1] // steps, shape[2]), lambda i: (0, i, 0))
        return pl.BlockSpec((shape[0] // steps, shape[1]), lambda i: (i, 0))

    def body(*refs):
        ins, outs = refs[:sum(n_in)], refs[sum(n_in):]
        a = b = 0
        for k in range(len(slots)):
            for o_ref, val in zip(outs[b:b + n_out[k]], fn(*[r[...] for r in ins[a:a + n_in[k]]])):
                o_ref[...] = val
            a, b = a + n_in[k], b + n_out[k]

    flat_in = [arr for s in slots for arr in s]
    flat_out = [shp for o in out_shapes for shp in o]
    out, carried_out = _call_carrying(
        body, carried, name=name, grid=(steps,), in_specs=[spec(a.shape) for a in flat_in],
        out_specs=[spec(s) for s in flat_out], out_shape=[jax.ShapeDtypeStruct(s, F32) for s in flat_out],
        operands=flat_in, semantics=("parallel",))
    grouped, b = [], 0
    for k in range(len(slots)):
        grouped.append(out[b:b + n_out[k]])
        b += n_out[k]
    return grouped if carried is None else (grouped, carried_out)


def _sum_in_order(a):
    acc = a[0].astype(F32)
    for k in range(1, a.shape[0]):
        acc = acc + a[k].astype(F32)
    return (acc,)


def _adamw_math(w, g, m, v):
    m_new = ADAM_B1 * m + (1.0 - ADAM_B1) * g
    v_new = ADAM_B2 * v + (1.0 - ADAM_B2) * (g * g)
    m_hat = m_new / (1.0 - ADAM_B1 ** ADAM_STEP)
    v_hat = v_new / (1.0 - ADAM_B2 ** ADAM_STEP)
    return -ADAM_LR * (m_hat / (jnp.sqrt(v_hat) + ADAM_EPS) + ADAM_WD * w), m_new, v_new


W_IN_ADAMW_STEPS = 21
SMALL_PACK_COLS = 256
SMALL_PACK_ROWS = 2048


def _pack_small(parts):
    wide = [jnp.pad(p, ((0, 0), (0, 0), (0, SMALL_PACK_COLS - p.shape[2]))) for p in parts]
    rows = jnp.concatenate(wide, axis=1)
    return jnp.pad(rows, ((0, 0), (0, SMALL_PACK_ROWS - rows.shape[1]), (0, 0)))


def _unpack_small(packed, shapes):
    out, row = [], 0
    for a, b in shapes:
        out.append(packed[:, row:row + a, :b])
        row += a
    return out


def _pack(arrs, rows_per_layer, dtype):
    nl = arrs[0].shape[0]
    flat = jnp.concatenate([a.astype(dtype).reshape(nl, -1) for a in arrs], axis=1)
    flat = jnp.pad(flat, ((0, 0), (0, rows_per_layer * PACK_COLS - flat.shape[1])))
    return flat.reshape(nl * rows_per_layer, PACK_COLS)


def _unpack(packed, shapes, rows_per_layer):
    nl = shapes[0][0]
    flat = packed.reshape(packed.shape[:-2] + (nl, rows_per_layer * PACK_COLS))
    out, off = [], 0
    for shp in shapes:
        size = math.prod(shp[1:])
        out.append(flat[..., off:off + size].reshape(packed.shape[:-2] + tuple(shp)))
        off += size
    return out


def _rows_needed(shapes, multiple):
    per_layer = sum(math.prod(s[1:]) for s in shapes)
    rows = -(-per_layer // PACK_COLS)
    return -(-rows // multiple) * multiple


CONV_TILE = (8, 128)


def _layer_shard_exchange(w, l):
    conv = w['conv_w'][l].reshape(-1)
    conv = jnp.pad(conv, (0, math.prod(CONV_TILE) - conv.shape[0])).reshape(CONV_TILE)
    return _gather_exchange([w[n][l].astype(MXU_DTYPE) for n in MATMUL_SHARDED], [conv])


def _layer_full_weights(w, l, gathered):
    gathered = [_absolute_chip_order(g) for g in gathered]
    full = {n: w[n][l:l + 1] for n in WEIGHTS}
    for n, part in zip(MATMUL_SHARDED, gathered):
        if n in ROW_SHARDED:
            full[n] = part.reshape(1, 4 * part.shape[1], part.shape[2])
        else:
            full[n] = jnp.swapaxes(part, 0, 1).reshape(1, part.shape[1], 4 * part.shape[2])
    rows, cols = w['conv_w'].shape[1:]
    conv = gathered[-1].reshape(4, -1)[:, :rows * cols].reshape(4, rows, cols)
    full['conv_w'] = jnp.swapaxes(conv, 0, 1).reshape(1, rows, 4 * cols)
    return full


def _chip_major(n, g):
    nl = g.shape[0]
    if n in ROW_SHARDED:
        return jnp.swapaxes(g.reshape(nl, 4, g.shape[1] // 4, g.shape[2]), 0, 1)
    return jnp.transpose(g.reshape(nl, g.shape[1], 4, g.shape[2] // 4), (2, 0, 1, 3))


def kernel(x, positions, norm_mix_pre, w_in, gate_b, q_norm, w_uq, kv_norm, w_ukv, w_br_mla, sg_ln_g, sg_ln_b, sg_w, sg_b, w_br_sg, conv_w, w_br_conv, pool_w, pool_scale, w_br_pool, w_out, norm_mix_post, norm_ffn_pre, w_ff1, w_ff2, norm_ffn_post, loss_target, m_norm_mix_pre, m_w_in, m_gate_b, m_q_norm, m_w_uq, m_kv_norm, m_w_ukv, m_w_br_mla, m_sg_ln_g, m_sg_ln_b, m_sg_w, m_sg_b, m_w_br_sg, m_conv_w, m_w_br_conv, m_pool_w, m_pool_scale, m_w_br_pool, m_w_out, m_norm_mix_post, m_norm_ffn_pre, m_w_ff1, m_w_ff2, m_norm_ffn_post, v_norm_mix_pre, v_w_in, v_gate_b, v_q_norm, v_w_uq, v_kv_norm, v_w_ukv, v_w_br_mla, v_sg_ln_g, v_sg_ln_b, v_sg_w, v_sg_b, v_w_br_sg, v_conv_w, v_w_br_conv, v_pool_w, v_pool_scale, v_w_br_pool, v_w_out, v_norm_mix_post, v_norm_ffn_pre, v_w_ff1, v_w_ff2, v_norm_ffn_post):
    given = dict(locals())
    w = {n: given[n] for n in WEIGHTS}
    mom = {n: given['m_' + n] for n in WEIGHTS}
    var = {n: given['v_' + n] for n in WEIGHTS}
    nl = w['w_in'].shape[0]
    sharded, chip_sums, reduced, local_small = {}, {}, {}, {}

    def grads_ready(l, g):
        sharded[l] = [_chip_major(n, g[n])[:, 0] for n in BIG_SHARDED]
        sharded[l].append(_pack_small([_chip_major(n, g[n])[:, 0] for n in SMALL_SHARDED]))
        local_small[l] = [g[n] for n in REPLICATED + ['conv_w']]
        return _sibling_exchange(sharded[l])

    def sibling_done(l, theirs):
        chip_sums[l] = _add_sibling("add_sibling", sharded[l], theirs)
        return _scatter_exchange(chip_sums[l])

    def chips_done(l, arrived):
        reduced[l] = _join_siblings("join_halves", _sum_chips("sum_chips", chip_sums[l], arrived))

    plan = _StepPlan(n_layers=nl, weights_exchange=functools.partial(_layer_shard_exchange, w),
                     weights_from=functools.partial(_layer_full_weights, w), grads_ready=grads_ready,
                     first_done=sibling_done, second_done=chips_done)
    loss, dx = _local_step(x[0], positions[0], loss_target[0], plan)
    loss = lax.psum(loss, ("x", "y", "c"))

    grad, delta, new_m, new_v = {}, {}, {}, {}
    names = REPLICATED + ['conv_w']
    local = [jnp.concatenate([local_small[l][k] for l in range(nl)]) for k in range(len(names))]
    rows = _rows_needed([a.shape for a in local], 32)
    small_grads = _gather_all_exchange(_pack(local, rows, F32))
    for k, n in enumerate(BIG_SHARDED):
        grad[n] = jnp.stack([reduced[l][k] for l in range(nl)])
    turned = [jnp.swapaxes(a, 1, 2) for a in (w['w_in'], grad['w_in'], mom['w_in'], var['w_in'])]
    (w_in_out,) = _rowwise_call("adamw_w_in", _adamw_math, [turned], [[turned[0].shape] * 3], W_IN_ADAMW_STEPS)
    delta['w_in'], new_m['w_in'], new_v['w_in'] = [jnp.swapaxes(a, 1, 2) for a in w_in_out]
    others = [n for n in BIG_SHARDED if n != 'w_in']
    slots = [[w[n], grad[n], mom[n], var[n]] for n in others]
    small_pack = lambda d: _pack_small([d[n] for n in SMALL_SHARDED])
    g_small = jnp.stack([reduced[l][-1] for l in range(nl)])
    slots.append([small_pack(w), g_small, small_pack(mom), small_pack(var)])
    updated, (everyone,) = _rowwise_call("adamw_sharded", _adamw_math, slots, [[s_[0].shape] * 3 for s_ in slots], 32,
                                         carried=small_grads)
    for k, n in enumerate(others):
        delta[n], new_m[n], new_v[n] = updated[k]
    for d, packed in zip((grad, delta, new_m, new_v), [g_small] + list(updated[-1])):
        d.update(zip(SMALL_SHARDED, _unpack_small(packed, [w[n].shape[1:] for n in SMALL_SHARDED])))

    (summed,), = _rowwise_call("sum_devices", _sum_in_order, [[everyone]], [[everyone.shape[1:]]], 4)
    g_rep = _unpack(summed, [a.shape for a in local], rows)
    chip = 2 * lax.axis_index("x") + lax.axis_index("y")
    g_rep[-1] = lax.dynamic_slice_in_dim(g_rep[-1], chip * w['conv_w'].shape[2], w['conv_w'].shape[2], axis=2)
    rep_pack = lambda arrs: _pack(arrs, rows, F32)
    (rep_out,) = _rowwise_call("adamw_replicated", _adamw_math,
                               [[rep_pack([w[n] for n in names]), rep_pack(g_rep), rep_pack([mom[n] for n in names]),
                                 rep_pack([var[n] for n in names])]], [[(nl * rows, PACK_COLS)] * 3], 4)
    grad.update(zip(names, g_rep))
    for d, packed in zip((delta, new_m, new_v), rep_out):
        d.update(zip(names, _unpack(packed, [w[n].shape for n in names], rows)))

    return (loss, dx[None], *[grad[n] for n in WEIGHTS], *[delta[n] for n in WEIGHTS], *[new_m[n] for n in WEIGHTS],
            *[new_v[n] for n in WEIGHTS])
```

```python
import functools
import math
from typing import Any, Callable, NamedTuple, Sequence

import jax
import jax.numpy as jnp
from jax import lax
from jax.experimental import pallas as pl
from jax.experimental.pallas import tpu as pltpu

F32 = jnp.float32
MXU_DTYPE = jnp.bfloat16
WIRE_DTYPE = jnp.bfloat16
ACT_DTYPE = jnp.bfloat16
MESH = pl.DeviceIdType.MESH

D_MODEL = 1024
D_FF = 4096
N_HEADS = 4
QK_NOPE = 64
QK_ROPE = 32
V_HEAD = 64
HEAD_PAD = 128
Q_RANK = 256
KV_RANK = 128
SG_CHUNK = 128
SG_GROUPS = 4
BR_WIDTH = 256
N_BRANCH = 4
POOL_WINDOWS = (2, 4, 8, 16)
HALO = 16
ROPE_BASE = 10000.0
EPS = 1e-6
ATTN_SCALE = (QK_NOPE + QK_ROPE) ** -0.5
LOG2_E = math.log2(math.e)
FWD_HEADS_PER_STEP = 4
HEADS_PER_STEP = 2
N_PROJ = N_BRANCH * D_MODEL + 6 * BR_WIDTH + Q_RANK + KV_RANK + 2 * HEAD_PAD
COL_G, COL_M1, COL_M2, COL_B = 0, 4096, 4864, 5632

ADAM_LR, ADAM_B1, ADAM_B2, ADAM_EPS, ADAM_WD, ADAM_STEP = 0.001, 0.9, 0.999, 1e-08, 0.01, 10

VMEM_LIMIT = 56 * 1024 * 1024

WEIGHTS = ['norm_mix_pre', 'w_in', 'gate_b', 'q_norm', 'w_uq', 'kv_norm', 'w_ukv', 'w_br_mla', 'sg_ln_g', 'sg_ln_b',
           'sg_w', 'sg_b', 'w_br_sg', 'conv_w', 'w_br_conv', 'pool_w', 'pool_scale', 'w_br_pool', 'w_out',
           'norm_mix_post', 'norm_ffn_pre', 'w_ff1', 'w_ff2', 'norm_ffn_post']
COL_SHARDED = ['w_in', 'w_uq', 'w_ukv', 'w_br_mla', 'w_br_sg', 'w_br_conv', 'w_br_pool', 'w_ff1']
ROW_SHARDED = ['w_out', 'w_ff2']
MATMUL_SHARDED = ['w_in', 'w_uq', 'w_ukv', 'w_br_mla', 'w_br_sg', 'w_br_conv', 'w_br_pool', 'w_out', 'w_ff1', 'w_ff2']
BIG_SHARDED = ['w_in', 'w_ff1', 'w_ff2', 'w_out']
SMALL_SHARDED = ['w_uq', 'w_ukv', 'w_br_mla', 'w_br_sg', 'w_br_conv', 'w_br_pool']
SHARDED = MATMUL_SHARDED + ['conv_w']
REPLICATED = [n for n in WEIGHTS if n not in SHARDED]
PACK_COLS = 1024


def _params(sem, vmem=VMEM_LIMIT):
    return pltpu.CompilerParams(dimension_semantics=sem, vmem_limit_bytes=vmem)


def _mxu(a):
    return a.astype(MXU_DTYPE)


def _dot(a, b):
    return jnp.dot(_mxu(a), _mxu(b), preferred_element_type=F32)


def _dot_nt(a, b):
    return lax.dot_general(_mxu(a), _mxu(b), (((1,), (1,)), ((), ())), preferred_element_type=F32)


def _dot_tn(a, b):
    return lax.dot_general(_mxu(a), _mxu(b), (((0,), (0,)), ((), ())), preferred_element_type=F32)


def _rms(x, g):
    r = lax.rsqrt(jnp.mean(x * x, axis=-1, keepdims=True) + EPS)
    return x * r * g


def _rms_bwd(x, g, dy):
    r = lax.rsqrt(jnp.mean(x * x, axis=-1, keepdims=True) + EPS)
    xh = x * r
    gdy = dy * g
    dx = r * (gdy - xh * jnp.mean(gdy * xh, axis=-1, keepdims=True))
    return dx, jnp.sum(dy * xh, axis=0, keepdims=True)


_GELU_C = math.sqrt(2.0 / math.pi)


def _gelu(x):
    t = jnp.tanh(_GELU_C * (x + 0.044715 * (x * x * x)))
    return x * (0.5 * (1.0 + t)), t


def _gelu_grad(x, t):
    return 0.5 * (1.0 + t) + 0.5 * x * (1.0 - t * t) * (_GELU_C * (1.0 + 3.0 * 0.044715 * x * x))


def _sigmoid(x):
    return 1.0 / (1.0 + jnp.exp(-x))


def _full(shape):
    return pl.BlockSpec(shape, lambda *_: (0,) * len(shape))


def _resident(shape):
    return pl.BlockSpec(shape, lambda *_: (0,) * len(shape), pipeline_mode=pl.Buffered(1))


def _rows(ts, width, col=0):
    return pl.BlockSpec((ts, width), lambda i: (i, col))


def _tile(n, pref):
    return min(n, pref)


def _mm(name, a, w, *, tm, tn, prologue=None, rows=()):
    m, k = a.shape
    n = w.shape[1]
    tm, tn = _tile(m, tm), _tile(n, tn)

    def body(a_ref, *rest):
        row_refs, w_ref, o_ref = rest[:len(rows)], rest[len(rows)], rest[len(rows) + 1]
        av = a_ref[...]
        if prologue is not None:
            av = prologue(av, *[r[...] for r in row_refs])
        o_ref[...] = _dot(av, w_ref[...]).astype(o_ref.dtype)

    return pl.pallas_call(
        body, name=name, grid=(m // tm, n // tn),
        in_specs=[pl.BlockSpec((tm, k), lambda i, j: (i, 0))] + [pl.BlockSpec((1, k), lambda i, j: (0, 0)) for _ in rows]
        + [_resident((k, n)) if tn == n else pl.BlockSpec((k, tn), lambda i, j: (0, j))],
        out_specs=pl.BlockSpec((tm, tn), lambda i, j: (i, j)),
        out_shape=jax.ShapeDtypeStruct((m, n), ACT_DTYPE),
        compiler_params=_params(("parallel", "parallel")),
    )(a, *rows, w)


def _mm_tn(name, a, b, *, tm, tn, prologue=None, rows=()):
    m, k = a.shape
    n = b.shape[1]
    tm, tn = _tile(m, tm), _tile(n, tn)

    def body(a_ref, *rest):
        row_refs, b_ref, o_ref = rest[:len(rows)], rest[len(rows)], rest[len(rows) + 1]

        @pl.when(pl.program_id(1) == 0)
        def _():
            o_ref[...] = jnp.zeros_like(o_ref)

        av = a_ref[...]
        if prologue is not None:
            av = prologue(av, *[r[...] for r in row_refs])
        o_ref[...] += _dot_tn(av, b_ref[...])

    return pl.pallas_call(
        body, name=name, grid=(n // tn, m // tm),
        in_specs=[pl.BlockSpec((tm, k), lambda j, i: (i, 0))] + [pl.BlockSpec((1, k), lambda j, i: (0, 0)) for _ in rows]
        + [pl.BlockSpec((tm, tn), lambda j, i: (i, j))],
        out_specs=pl.BlockSpec((k, tn), lambda j, i: (0, j)),
        out_shape=jax.ShapeDtypeStruct((k, n), F32),
        compiler_params=_params(("parallel", "arbitrary")),
    )(a, *rows, b)


def _relu_sq(a):
    r = jnp.maximum(a.astype(F32), 0.0)
    return r * r


HBM = pl.BlockSpec(memory_space=pl.ANY)


class _Exchange(NamedTuple):
    operands: Sequence[Any]
    out_shape: Sequence[Any]
    scratch_shapes: Sequence[Any]
    phases: Callable


def _run_exchange(name, ex):
    n_in, n_out = len(ex.operands), len(ex.out_shape)

    def body(*refs):
        for phase in ex.phases(refs[:n_in], refs[n_in:n_in + n_out], refs[n_in + n_out:]):
            phase()

    return pl.pallas_call(body, name=name, in_specs=[HBM] * n_in, out_specs=[HBM] * n_out, out_shape=list(ex.out_shape),
                          scratch_shapes=list(ex.scratch_shapes))(*ex.operands)


def _call_carrying(body, carried, *, name, grid, in_specs, out_specs, out_shape, operands, semantics, middle_step=None):
    if carried is None:
        return pl.pallas_call(body, name=name, grid=grid, in_specs=in_specs, out_specs=out_specs, out_shape=out_shape,
                              compiler_params=_params(semantics))(*operands), None
    n_main_in, n_main_out = len(operands), len(out_shape)
    n_in, n_out = len(carried.operands), len(carried.out_shape)
    steps = math.prod(grid)

    def wrapped(*refs):
        main_in, refs = refs[:n_main_in], refs[n_main_in:]
        ex_in, refs = refs[:n_in], refs[n_in:]
        main_out, refs = refs[:n_main_out], refs[n_main_out:]
        ex_out, sems = refs[:n_out], refs[n_out:]
        step = pl.program_id(0)
        for axis in range(1, len(grid)):
            step = step * grid[axis] + pl.program_id(axis)
        start, middle, finish = carried.phases(ex_in, ex_out, sems)
        pl.when(step == 0)(start)
        pl.when(step == ((steps - 1) // 2 if middle_step is None else middle_step))(middle)
        body(*main_in, *main_out)
        pl.when(step == steps - 1)(finish)

    out = pl.pallas_call(
        wrapped, name=name + "_carrying", grid=grid, in_specs=list(in_specs) + [HBM] * n_in,
        out_specs=list(out_specs) + [HBM] * n_out, out_shape=list(out_shape) + list(carried.out_shape),
        scratch_shapes=list(carried.scratch_shapes), compiler_params=_params(("arbitrary",) * len(grid)))(*operands, *carried.operands)
    return out[:n_main_out], out[n_main_out:]


def _qkv_prep(proj, q_norm, kv_norm, wq, wkv, cq_tab, s_tab, cr_tab):
    s = proj.shape[0]
    ts = _tile(s, 512)
    hw = N_HEADS * HEAD_PAD

    def body(cq_ref, ckv_ref, kr_ref, krs_ref, gq_ref, gkv_ref, wq_ref, wkv_ref, ct_ref, st_ref, crt_ref,
             q_ref, k_ref, v_ref):
        ct, st, crt = ct_ref[...], st_ref[...], crt_ref[...]
        qn = _rms(cq_ref[...].astype(F32), gq_ref[...])
        qab = _dot(qn, wq_ref[...])
        kvn = _rms(ckv_ref[...].astype(F32), gkv_ref[...])
        kav = _dot(kvn, wkv_ref[...])
        k_rope = kr_ref[...].astype(F32) * crt + krs_ref[...].astype(F32) * st
        ones_lane = (lax.broadcasted_iota(jnp.int32, (1, HEAD_PAD), 1) == V_HEAD).astype(F32)
        for h in range(N_HEADS):
            lo = h * HEAD_PAD
            q_ref[h] = (qab[:, lo:lo + HEAD_PAD] * ct + qab[:, hw + lo:hw + lo + HEAD_PAD] * st).astype(q_ref.dtype)
            k_ref[h] = (kav[:, lo:lo + HEAD_PAD] + k_rope).astype(k_ref.dtype)
            v_ref[h] = (kav[:, hw + lo:hw + lo + HEAD_PAD] + ones_lane).astype(v_ref.dtype)

    head_spec = pl.BlockSpec((N_HEADS, ts, HEAD_PAD), lambda i: (0, i, 0))
    head_shape = jax.ShapeDtypeStruct((N_HEADS, s, HEAD_PAD), MXU_DTYPE)
    return pl.pallas_call(
        body, name="qkv_prep", grid=(s // ts,),
        in_specs=[_rows(ts, Q_RANK, COL_B // Q_RANK), _rows(ts, KV_RANK, (COL_B + Q_RANK) // KV_RANK),
                  _rows(ts, HEAD_PAD, (COL_B + Q_RANK + KV_RANK) // HEAD_PAD),
                  _rows(ts, HEAD_PAD, (COL_B + Q_RANK + KV_RANK + HEAD_PAD) // HEAD_PAD),
                  _full((1, Q_RANK)), _full((1, KV_RANK)), _full((Q_RANK, 2 * hw)), _full((KV_RANK, 2 * hw)),
                  _rows(ts, HEAD_PAD), _rows(ts, HEAD_PAD), _rows(ts, HEAD_PAD)],
        out_specs=[head_spec, head_spec, head_spec],
        out_shape=[head_shape, head_shape, head_shape],
        compiler_params=_params(("parallel",)),
    )(proj, proj, proj, proj, q_norm, kv_norm, wq, wkv, cq_tab, s_tab, cr_tab)


def _diagonal_mask(t, keys_on_rows=False):
    key_axis = 0 if keys_on_rows else 1
    return lax.broadcasted_iota(jnp.int32, (t, t), key_axis) <= lax.broadcasted_iota(jnp.int32, (t, t), 1 - key_axis)


MIDDLE_WORK_SHARE = 0.65


def _causal_work_step(nq, groups):
    total = groups * nq * (nq + 1) // 2
    done = 0
    for step in range(groups * nq):
        if done >= MIDDLE_WORK_SHARE * total:
            return step
        done += step % nq + 1
    return groups * nq - 1


def _attn_fwd(q, k, v, carried=None):
    s = q.shape[1]
    t = _tile(s, 1024)

    def body(q_ref, k_ref, v_ref, o_ref, lse_ref):
        i = pl.program_id(1)
        lane = lax.broadcasted_iota(jnp.int32, (1, HEAD_PAD), 1)

        def step(j, carry, on_diagonal):
            rows = pl.ds(pl.multiple_of(j * t, t), t)
            out = []
            for h in range(FWD_HEADS_PER_STEP):
                m, acc = carry[h]
                sc = _dot_nt(q_ref[h], k_ref[h, rows, :]) * (ATTN_SCALE * LOG2_E)
                if on_diagonal:
                    sc = jnp.where(_diagonal_mask(t), sc, -jnp.inf)
                m_new = jnp.maximum(m, jnp.max(sc, axis=1, keepdims=True))
                out.append((m_new, jnp.exp2(m - m_new) * acc + _dot(jnp.exp2(sc - m_new), v_ref[h, rows, :])))
            return tuple(out)

        init = ((jnp.full((t, 1), -jnp.inf, F32), jnp.zeros((t, HEAD_PAD), F32)),) * FWD_HEADS_PER_STEP
        below = lax.fori_loop(0, i, functools.partial(step, on_diagonal=False), init)
        for h, (m, acc) in enumerate(step(i, below, True)):
            l = jnp.sum(jnp.where(lane == V_HEAD, acc, 0.0), axis=1, keepdims=True)
            o_ref[:, h * HEAD_PAD:(h + 1) * HEAD_PAD] = jnp.where(lane < V_HEAD, acc / l, 0.0)
            lse_ref[h] = m + jnp.log2(l)

    group = FWD_HEADS_PER_STEP
    return _call_carrying(
        body, carried, name="attn_fwd", grid=(N_HEADS // group, s // t),
        in_specs=[pl.BlockSpec((group, t, HEAD_PAD), lambda h, i: (h, i, 0)),
                  pl.BlockSpec((group, s, HEAD_PAD), lambda h, i: (h, 0, 0)),
                  pl.BlockSpec((group, s, HEAD_PAD), lambda h, i: (h, 0, 0))],
        out_specs=[pl.BlockSpec((t, group * HEAD_PAD), lambda h, i: (i, h)), pl.BlockSpec((group, t, 1), lambda h, i: (h, i, 0))],
        out_shape=[jax.ShapeDtypeStruct((s, N_HEADS * HEAD_PAD), F32), jax.ShapeDtypeStruct((N_HEADS, s, 1), F32)],
        operands=(q, k, v), semantics=("parallel", "parallel"), middle_step=_causal_work_step(s // t, N_HEADS // group))


def _attn_bwd(q, k, v, do, lse, delta, carried=None):
    s = q.shape[1]
    t = _tile(s, 1024)
    nq = s // t

    def body(q_ref, do_ref, lse_ref, dl_ref, k_ref, v_ref, dq_ref, dk_ref, dv_ref):
        j = pl.program_id(1)

        @pl.when(j == 0)
        def _():
            dq_ref[...] = jnp.zeros_like(dq_ref)

        def step(i, carry, on_diagonal):
            rows = pl.ds(pl.multiple_of(i * t, t), t)
            out = []
            for h in range(HEADS_PER_STEP):
                dk, dv = carry[h]
                qi, doi = q_ref[h, rows, :], do_ref[rows, h * HEAD_PAD:(h + 1) * HEAD_PAD]
                sc = _dot_nt(k_ref[h], qi) * (ATTN_SCALE * LOG2_E)
                if on_diagonal:
                    sc = jnp.where(_diagonal_mask(t, keys_on_rows=True), sc, -jnp.inf)
                p = jnp.exp2(sc - lse_ref[h, i])
                dv = dv + _dot(p, doi)
                ds = p * (_dot_nt(v_ref[h], doi) - dl_ref[h, i])
                dk = dk + _dot(ds, qi)
                dq_ref[h, rows, :] += _dot_tn(ds, k_ref[h]) * ATTN_SCALE
                out.append((dk, dv))
            return tuple(out)

        zero = ((jnp.zeros((t, HEAD_PAD), F32),) * 2,) * HEADS_PER_STEP
        sums = lax.fori_loop(j + 1, nq, functools.partial(step, on_diagonal=False), step(j, zero, True))
        for h, (dk, dv) in enumerate(sums):
            dk_ref[h] = dk * ATTN_SCALE
            dv_ref[h] = dv

    group = HEADS_PER_STEP
    whole = lambda w: pl.BlockSpec((group, s, w), lambda h, j: (h, 0, 0), pipeline_mode=pl.Buffered(1))
    tile = pl.BlockSpec((group, t, HEAD_PAD), lambda h, j: (h, j, 0))
    per_query = pl.BlockSpec((group, nq, 1, t), lambda h, j: (h, 0, 0, 0))
    head_shape = jax.ShapeDtypeStruct((N_HEADS, s, HEAD_PAD), F32)
    return _call_carrying(
        body, carried, name="attn_bwd", grid=(N_HEADS // group, nq),
        in_specs=[whole(HEAD_PAD), pl.BlockSpec((s, group * HEAD_PAD), lambda h, j: (0, h), pipeline_mode=pl.Buffered(1)),
                  per_query, per_query, tile, tile],
        out_specs=[whole(HEAD_PAD), tile, tile],
        out_shape=[head_shape, head_shape, head_shape],
        operands=(q, do, lse.reshape(N_HEADS, nq, 1, t), delta.reshape(N_HEADS, nq, 1, t), k, v),
        semantics=("parallel", "arbitrary"))


def _qkv_bwd(dq, dk, dv, proj, q_norm, kv_norm, wq, wkv, cq_tab, s_tab, cr_tab):
    s = proj.shape[0]
    ts = _tile(s, 512)
    hw = N_HEADS * HEAD_PAD

    def body(dq_ref, dk_ref, dv_ref, cq_ref, ckv_ref, gq_ref, gkv_ref, wq_ref, wkv_ref, ct_ref, st_ref, crt_ref,
             dpb_ref, dwq_ref, dwkv_ref, dgq_ref, dgkv_ref):
        @pl.when(pl.program_id(0) == 0)
        def _():
            for r in (dwq_ref, dwkv_ref, dgq_ref, dgkv_ref):
                r[...] = jnp.zeros_like(r)

        ct, st, crt = ct_ref[...], st_ref[...], crt_ref[...]
        dqs = [dq_ref[h] for h in range(N_HEADS)]
        dks = [dk_ref[h] for h in range(N_HEADS)]
        dqab = jnp.concatenate([d * ct for d in dqs] + [d * st for d in dqs], axis=1)
        dkav = jnp.concatenate(dks + [dv_ref[h] for h in range(N_HEADS)], axis=1)
        dk_sum = dks[0] + dks[1] + dks[2] + dks[3]
        cq, ckv, gq, gkv = cq_ref[...].astype(F32), ckv_ref[...].astype(F32), gq_ref[...], gkv_ref[...]
        dwq_ref[...] += _dot_tn(_rms(cq, gq), dqab)
        dwkv_ref[...] += _dot_tn(_rms(ckv, gkv), dkav)
        dcq, dgq = _rms_bwd(cq, gq, _dot_nt(dqab, wq_ref[...]))
        dckv, dgkv = _rms_bwd(ckv, gkv, _dot_nt(dkav, wkv_ref[...]))
        dgq_ref[...] += dgq
        dgkv_ref[...] += dgkv
        dpb_ref[...] = jnp.concatenate([dcq, dckv, dk_sum * crt, dk_sum * st], axis=1).astype(dpb_ref.dtype)

    head_spec = pl.BlockSpec((N_HEADS, ts, HEAD_PAD), lambda i: (0, i, 0))
    wb = Q_RANK + KV_RANK + 2 * HEAD_PAD
    return pl.pallas_call(
        body, name="qkv_bwd", grid=(s // ts,),
        in_specs=[head_spec, head_spec, head_spec,
                  _rows(ts, Q_RANK, COL_B // Q_RANK), _rows(ts, KV_RANK, (COL_B + Q_RANK) // KV_RANK),
                  _full((1, Q_RANK)), _full((1, KV_RANK)), _full((Q_RANK, 2 * hw)), _full((KV_RANK, 2 * hw)),
                  _rows(ts, HEAD_PAD), _rows(ts, HEAD_PAD), _rows(ts, HEAD_PAD)],
        out_specs=[_rows(ts, wb), _full((Q_RANK, 2 * hw)), _full((KV_RANK, 2 * hw)), _full((1, Q_RANK)), _full((1, KV_RANK))],
        out_shape=[jax.ShapeDtypeStruct((s, wb), MXU_DTYPE), jax.ShapeDtypeStruct((Q_RANK, 2 * hw), F32),
                   jax.ShapeDtypeStruct((KV_RANK, 2 * hw), F32), jax.ShapeDtypeStruct((1, Q_RANK), F32),
                   jax.ShapeDtypeStruct((1, KV_RANK), F32)],
        compiler_params=_params(("arbitrary",)),
    )(dq, dk, dv, proj, proj, q_norm, kv_norm, wq, wkv, cq_tab, s_tab, cr_tab)


def _lane_group(width):
    return lax.broadcasted_iota(jnp.int32, (1, width), 1) // (width // 4)


def _shift_down(a, k):
    return pltpu.roll(a, k, 0)


def _shift_up(a, k):
    return pltpu.roll(a, a.shape[0] - k, 0)


def _window_sums(xh, shift):
    s2 = xh + shift(xh, 1)
    s4 = s2 + shift(s2, 2)
    s8 = s4 + shift(s4, 4)
    s16 = s8 + shift(s8, 8)
    grp = _lane_group(xh.shape[1])
    return jnp.where(grp == 0, s2, jnp.where(grp == 1, s4, jnp.where(grp == 2, s8, s16)))


def _pool_count(i, ts):
    grp = _lane_group(BR_WIDTH)
    win = jnp.where(grp == 0, 2.0, jnp.where(grp == 1, 4.0, jnp.where(grp == 2, 8.0, 16.0)))
    t = (i * ts + lax.broadcasted_iota(jnp.int32, (ts, 1), 0)).astype(F32)
    return jnp.minimum(t + 1.0, win)


def _mix_forward(i, ts, r):
    f = {}
    act = lambda name: r[name][...].astype(F32)
    f['gates'] = _sigmoid(act('gate') + r['gate_b'][...])
    sgu, sgv = act('sgu'), act('sgv')
    f['sgu'], f['sgv'] = sgu, sgv
    u_act, f['tu'] = _gelu(sgu)
    vg, f['tv'] = _gelu(sgv)
    mu = jnp.mean(vg, axis=-1, keepdims=True)
    xc = vg - mu
    f['ln_r'] = lax.rsqrt(jnp.mean(xc * xc, axis=-1, keepdims=True) + EPS)
    f['ln_xh'] = xc * f['ln_r']
    vln = f['ln_xh'] * r['ln_g'][...] + r['ln_b'][...]
    tril = lax.broadcasted_iota(jnp.int32, (SG_CHUNK, SG_CHUNK), 1) <= lax.broadcasted_iota(jnp.int32, (SG_CHUNK, SG_CHUNK), 0)
    f['wm'] = [_mxu(jnp.where(tril, r['sg_w'][g], 0.0)) for g in range(SG_GROUPS)]
    f['tril'] = tril
    grp = _lane_group(BR_WIDTH)
    bias = r['sg_bias'][...]
    parts = []
    for ci in range(ts // SG_CHUNK):
        vc = vln[ci * SG_CHUNK:(ci + 1) * SG_CHUNK]
        sc = bias
        for g in range(SG_GROUPS):
            sc = sc + jnp.where(grp == g, _dot(f['wm'][g], vc), 0.0)
        parts.append(sc)
    f['vln'] = vln
    f['sg_s'] = parts[0] if len(parts) == 1 else jnp.concatenate(parts, axis=0)
    f['u_act'] = u_act
    out_b = u_act * f['sg_s']
    first = (i > 0).astype(F32)
    cvx, cvc, cvb = act('cvx'), act('cvc'), act('cvb')
    f['cvx'], f['cvc'], f['cvb'] = cvx, cvc, cvb
    zh = jnp.concatenate([act('hx') * act('hc') * first, cvc * cvx], axis=0)
    f['z1'] = _shift_down(zh, 1)[HALO:]
    f['z2'] = _shift_down(zh, 2)[HALO:]
    f['z0'] = zh[HALO:]
    f['yv'] = r['conv_w'][0:1, :] * f['z2'] + r['conv_w'][1:2, :] * f['z1'] + r['conv_w'][2:3, :] * f['z0']
    out_c = cvb * f['yv']
    p = act('pool')
    ph = jnp.concatenate([act('hp') * first, p], axis=0)
    f['cnt'] = _pool_count(i, ts)
    f['pooled'] = _window_sums(ph, _shift_down)[HALO:] / f['cnt'] - p
    f['mixed'] = _dot(f['pooled'], r['wbd'][...])
    out_d = f['mixed'] * r['pool_scale'][...]
    f['outs'] = [r['o'][...], out_b, out_c, out_d]
    f['ys'] = [_dot(f['outs'][b], r['w_br'][b][...]) for b in range(N_BRANCH)]
    merged = f['gates'][:, 0:D_MODEL] * f['ys'][0]
    for b in range(1, N_BRANCH):
        merged = merged + f['gates'][:, b * D_MODEL:(b + 1) * D_MODEL] * f['ys'][b]
    f['merged'] = merged
    f['mo'] = _dot(merged, r['w_out'][...])
    return f


_MIX_TILE_INPUTS = ['gate', 'sgu', 'sgv', 'cvb', 'cvx', 'cvc', 'pool', 'hx', 'hc', 'hp', 'o']
_MIX_WEIGHTS = ['gate_b', 'ln_g', 'ln_b', 'sg_w', 'sg_bias', 'conv_w', 'wbd', 'pool_scale', 'w_br0', 'w_br1', 'w_br2',
                'w_br3', 'w_out', 'g_post']


def _mix_specs(s, ts):
    c0 = COL_M1 // BR_WIDTH
    prev = lambda col: pl.BlockSpec((HALO, BR_WIDTH), lambda i: (jnp.maximum(i * (ts // HALO) - 1, 0), col))
    tiles = [_rows(ts, N_BRANCH * D_MODEL, 0), _rows(ts, BR_WIDTH, c0), _rows(ts, BR_WIDTH, c0 + 1), _rows(ts, BR_WIDTH, c0 + 2),
             _rows(ts, BR_WIDTH, c0 + 3), _rows(ts, BR_WIDTH, c0 + 4), _rows(ts, BR_WIDTH, c0 + 5),
             prev(c0 + 3), prev(c0 + 4), prev(c0 + 5), _rows(ts, N_HEADS * HEAD_PAD)]
    weights = [_full((1, N_BRANCH * D_MODEL)), _full((1, BR_WIDTH)), _full((1, BR_WIDTH)),
               _full((SG_GROUPS, SG_CHUNK, SG_CHUNK)), _full((SG_CHUNK, BR_WIDTH)), _full((8, BR_WIDTH)),
               _resident((BR_WIDTH, BR_WIDTH)), _full((1, BR_WIDTH)), _resident((N_HEADS * HEAD_PAD, D_MODEL)),
               _resident((BR_WIDTH, D_MODEL)), _resident((BR_WIDTH, D_MODEL)), _resident((BR_WIDTH, D_MODEL)),
               _resident((D_MODEL, D_MODEL)), _full((1, D_MODEL))]
    return tiles, weights


def _mix_refs(refs):
    names = _MIX_TILE_INPUTS + _MIX_WEIGHTS
    r = dict(zip(names, refs[:len(names)]))
    r['w_br'] = [r['w_br0'], r['w_br1'], r['w_br2'], r['w_br3']]
    return r, refs[len(names):]


def _mix_operands(proj, o, lw):
    return ([proj] * 10 + [o] + [lw[n] for n in ['gate_b', 'sg_ln_g', 'sg_ln_b', 'sg_w', 'sg_bias', 'conv_w8', 'wbd',
                                                 'pool_scale', 'w_br_mla_p', 'w_br_sg', 'w_br_conv', 'w_br_pool', 'w_out',
                                                 'norm_mix_post']])


def _mix_fwd(x0, proj, o, lw):
    s = x0.shape[0]
    ts = _tile(s, 512)
    tiles, weights = _mix_specs(s, ts)

    def body(*refs):
        r, (x0_ref, x1_ref) = _mix_refs(refs)
        f = _mix_forward(pl.program_id(0), ts, r)
        x1_ref[...] = x0_ref[...] + _rms(f['mo'], r['g_post'][...])

    return pl.pallas_call(
        body, name="mix_fwd", grid=(s // ts,),
        in_specs=tiles + weights + [_rows(ts, D_MODEL)],
        out_specs=_rows(ts, D_MODEL),
        out_shape=jax.ShapeDtypeStruct((s, D_MODEL), F32),
        compiler_params=_params(("parallel",)),
    )(*_mix_operands(proj, o, lw), x0)


def _mix_bwd(dx1, proj, o, lw):
    s = dx1.shape[0]
    ts = _tile(s, 256)
    tiles, weights = _mix_specs(s, ts)
    hw = N_HEADS * HEAD_PAD

    def body(*refs):
        r, rest = _mix_refs(refs)
        (dx1_ref, dg_ref, dm1_ref, dyv_ref, up_ref, do_ref, delta_ref,
         dgate_b_ref, dln_g_ref, dln_b_ref, dsgw_ref, dsgb_ref, dconv_ref, dwbd_ref, dps_ref,
         dwbr0_ref, dwbr1_ref, dwbr2_ref, dwbr3_ref, dwout_ref, dgpost_ref, dbias_acc) = rest
        i = pl.program_id(0)
        acc_refs = [dgate_b_ref, dln_g_ref, dln_b_ref, dsgw_ref, dsgb_ref, dconv_ref, dwbd_ref, dps_ref,
                    dwbr0_ref, dwbr1_ref, dwbr2_ref, dwbr3_ref, dwout_ref, dgpost_ref, dbias_acc]

        @pl.when(i == 0)
        def _():
            for a in acc_refs:
                a[...] = jnp.zeros_like(a)

        f = _mix_forward(i, ts, r)
        dmo, dgpost = _rms_bwd(f['mo'], r['g_post'][...], dx1_ref[...])
        dgpost_ref[...] += dgpost
        dwout_ref[...] += _dot_tn(f['merged'], dmo)
        dmerged = _dot_nt(dmo, r['w_out'][...])
        dwbr = [dwbr0_ref, dwbr1_ref, dwbr2_ref, dwbr3_ref]
        douts = []
        for b in range(N_BRANCH):
            gb = f['gates'][:, b * D_MODEL:(b + 1) * D_MODEL]
            dgate = dmerged * f['ys'][b] * gb * (1.0 - gb)
            dg_ref[:, b * D_MODEL:(b + 1) * D_MODEL] = dgate.astype(dg_ref.dtype)
            dgate_b_ref[:, b * D_MODEL:(b + 1) * D_MODEL] += jnp.sum(dgate, axis=0, keepdims=True)
            dy = dmerged * gb
            dwbr[b][...] += _dot_tn(f['outs'][b], dy)
            douts.append(_dot_nt(dy, r['w_br'][b][...]))
        do = douts[0]
        do_ref[...] = do.astype(do_ref.dtype)
        prod = do * f['outs'][0]
        for h in range(N_HEADS):
            delta_ref[h] = jnp.sum(prod[:, h * HEAD_PAD:(h + 1) * HEAD_PAD], axis=1, keepdims=True)
        grp = _lane_group(BR_WIDTH)
        ds = douts[1] * f['u_act']
        dsgu = douts[1] * f['sg_s'] * _gelu_grad(f['sgu'], f['tu'])
        dvln_parts = []
        for ci in range(ts // SG_CHUNK):
            rows = slice(ci * SG_CHUNK, (ci + 1) * SG_CHUNK)
            ds_c, vln_c = ds[rows], f['vln'][rows]
            dvln_c = jnp.zeros((SG_CHUNK, BR_WIDTH), F32)
            for g in range(SG_GROUPS):
                dvln_c = dvln_c + jnp.where(grp == g, _dot_tn(f['wm'][g], ds_c), 0.0)
                dsgw_ref[g] += jnp.where(f['tril'], _dot_nt(jnp.where(grp == g, ds_c, 0.0), vln_c), 0.0)
            dbias_acc[...] += ds_c
            dvln_parts.append(dvln_c)
        dvln = dvln_parts[0] if len(dvln_parts) == 1 else jnp.concatenate(dvln_parts, axis=0)
        dln_g_ref[...] += jnp.sum(dvln * f['ln_xh'], axis=0, keepdims=True)
        dln_b_ref[...] += jnp.sum(dvln, axis=0, keepdims=True)
        dxh = dvln * r['ln_g'][...]
        dvg = f['ln_r'] * (dxh - jnp.mean(dxh, axis=-1, keepdims=True)
                           - f['ln_xh'] * jnp.mean(dxh * f['ln_xh'], axis=-1, keepdims=True))
        dsgv = dvg * _gelu_grad(f['sgv'], f['tv'])
        dcvb = douts[2] * f['yv']
        dyv = douts[2] * f['cvb']
        dyv_ref[...] = dyv
        for kk, zk in enumerate((f['z2'], f['z1'], f['z0'])):
            dconv_ref[kk:kk + 1, :] += jnp.sum(dyv * zk, axis=0, keepdims=True)
        dps_ref[...] += jnp.sum(douts[3] * f['mixed'], axis=0, keepdims=True)
        dmixed = douts[3] * r['pool_scale'][...]
        dwbd_ref[...] += _dot_tn(f['pooled'], dmixed)
        up_ref[...] = _dot_nt(dmixed, r['wbd'][...]) / f['cnt']
        dm1_ref[...] = jnp.concatenate([dsgu, dsgv, dcvb], axis=1).astype(dm1_ref.dtype)

        @pl.when(i == pl.num_programs(0) - 1)
        def _():
            lane = lax.broadcasted_iota(jnp.int32, (1, SG_CHUNK), 1)
            db = dbias_acc[...]
            out = jnp.zeros((SG_CHUNK, SG_CHUNK), F32)
            for g in range(SG_GROUPS):
                out = out + jnp.where(lane == g, jnp.sum(jnp.where(grp == g, db, 0.0), axis=1, keepdims=True), 0.0)
            dsgb_ref[...] = out

    acc = lambda shape: (_full(shape), jax.ShapeDtypeStruct(shape, F32))
    accs = [acc((1, N_BRANCH * D_MODEL)), acc((1, BR_WIDTH)), acc((1, BR_WIDTH)), acc((SG_GROUPS, SG_CHUNK, SG_CHUNK)),
            acc((SG_CHUNK, SG_CHUNK)), acc((8, BR_WIDTH)), acc((BR_WIDTH, BR_WIDTH)), acc((1, BR_WIDTH)),
            acc((hw, D_MODEL)), acc((BR_WIDTH, D_MODEL)), acc((BR_WIDTH, D_MODEL)), acc((BR_WIDTH, D_MODEL)),
            acc((D_MODEL, D_MODEL)), acc((1, D_MODEL))]
    tile_outs = [(_rows(ts, N_BRANCH * D_MODEL), jax.ShapeDtypeStruct((s, N_BRANCH * D_MODEL), MXU_DTYPE)),
                 (_rows(ts, 3 * BR_WIDTH), jax.ShapeDtypeStruct((s, 3 * BR_WIDTH), MXU_DTYPE)),
                 (_rows(ts, BR_WIDTH), jax.ShapeDtypeStruct((s, BR_WIDTH), F32)),
                 (_rows(ts, BR_WIDTH), jax.ShapeDtypeStruct((s, BR_WIDTH), F32)),
                 (_rows(ts, hw), jax.ShapeDtypeStruct((s, hw), MXU_DTYPE)),
                 (pl.BlockSpec((N_HEADS, ts, 1), lambda i: (0, i, 0)), jax.ShapeDtypeStruct((N_HEADS, s, 1), F32))]
    outs = tile_outs + accs
    return pl.pallas_call(
        body, name="mix_bwd", grid=(s // ts,),
        in_specs=tiles + weights + [_rows(ts, D_MODEL)],
        out_specs=[o_[0] for o_ in outs], out_shape=[o_[1] for o_ in outs],
        scratch_shapes=[pltpu.VMEM((SG_CHUNK, BR_WIDTH), F32)],
        compiler_params=_params(("arbitrary",), 60 * 1024 * 1024),
    )(*_mix_operands(proj, o, lw), dx1)


def _shift_bwd(dyv, upool, proj, conv_w8):
    s = dyv.shape[0]
    ts = _tile(s, 512)
    nb = s // HALO
    c0 = COL_M1 // BR_WIDTH

    def body(dyv_ref, dyvn_ref, up_ref, upn_ref, cvx_ref, cvc_ref, cw_ref, out_ref):
        i = pl.program_id(0)
        last = (i < pl.num_programs(0) - 1).astype(F32)
        dh = jnp.concatenate([dyv_ref[...], dyvn_ref[...] * last], axis=0)
        dz = (cw_ref[2:3, :] * dh + cw_ref[1:2, :] * _shift_up(dh, 1) + cw_ref[0:1, :] * _shift_up(dh, 2))[:ts]
        up = up_ref[...]
        uh = jnp.concatenate([up, upn_ref[...] * last], axis=0)
        dpool = _window_sums(uh, _shift_up)[:ts] - up * _pool_count(i, ts)
        out_ref[...] = jnp.concatenate([dz * cvc_ref[...].astype(F32), dz * cvx_ref[...].astype(F32), dpool],
                                       axis=1).astype(out_ref.dtype)

    nxt = pl.BlockSpec((HALO, BR_WIDTH), lambda i: (jnp.minimum((i + 1) * (ts // HALO), nb - 1), 0))
    return pl.pallas_call(
        body, name="shift_bwd", grid=(s // ts,),
        in_specs=[_rows(ts, BR_WIDTH), nxt, _rows(ts, BR_WIDTH), nxt, _rows(ts, BR_WIDTH, c0 + 3), _rows(ts, BR_WIDTH, c0 + 4),
                  _full((8, BR_WIDTH))],
        out_specs=_rows(ts, 3 * BR_WIDTH),
        out_shape=jax.ShapeDtypeStruct((s, 3 * BR_WIDTH), MXU_DTYPE),
        compiler_params=_params(("parallel",)),
    )(dyv, dyv, upool, upool, proj, proj, conv_w8)


def _ffn2(a, w2, x1, g):
    s = a.shape[0]
    ts = _tile(s, 512)

    def body(a_ref, w_ref, x1_ref, g_ref, x2_ref, f_ref):
        f = _dot(_relu_sq(a_ref[...]), w_ref[...])
        f_ref[...] = f
        x2_ref[...] = x1_ref[...] + _rms(f, g_ref[...])

    return pl.pallas_call(
        body, name="ffn2", grid=(s // ts,),
        in_specs=[_rows(ts, D_FF), _resident((D_FF, D_MODEL)), _rows(ts, D_MODEL), _full((1, D_MODEL))],
        out_specs=[_rows(ts, D_MODEL), _rows(ts, D_MODEL)],
        out_shape=[jax.ShapeDtypeStruct((s, D_MODEL), F32)] * 2,
        compiler_params=_params(("parallel",)),
    )(a, w2, x1, g)


def _ffn2_bwd(dx2, f, g, a, w2t, carried=None):
    s = a.shape[0]
    ts = _tile(s, 512)

    def body(dx2_ref, f_ref, g_ref, a_ref, w_ref, df_ref, da_ref, dg_ref):
        @pl.when(pl.program_id(0) == 0)
        def _():
            dg_ref[...] = jnp.zeros_like(dg_ref)

        df, dg = _rms_bwd(f_ref[...], g_ref[...], dx2_ref[...])
        dg_ref[...] += dg
        df_ref[...] = df.astype(df_ref.dtype)
        da_ref[...] = (_dot(df, w_ref[...]) * (2.0 * jnp.maximum(a_ref[...].astype(F32), 0.0))).astype(da_ref.dtype)

    return _call_carrying(
        body, carried, name="ffn2_bwd", grid=(s // ts,),
        in_specs=[_rows(ts, D_MODEL), _rows(ts, D_MODEL), _full((1, D_MODEL)), _rows(ts, D_FF), _resident((D_MODEL, D_FF))],
        out_specs=[_rows(ts, D_MODEL), _rows(ts, D_FF), _full((1, D_MODEL))],
        out_shape=[jax.ShapeDtypeStruct((s, D_MODEL), MXU_DTYPE), jax.ShapeDtypeStruct((s, D_FF), MXU_DTYPE),
                   jax.ShapeDtypeStruct((1, D_MODEL), F32)],
        operands=(dx2, f, g, a, w2t), semantics=("arbitrary",))


def _norm_in_bwd(name, pieces, x, g, dres):
    s = x.shape[0]
    ts = _tile(s, 512)
    n = len(pieces)

    def body(*refs):
        d_refs, w_refs = refs[:n], refs[n:2 * n]
        x_ref, g_ref, dres_ref, dx_ref, dg_ref = refs[2 * n:]

        @pl.when(pl.program_id(0) == 0)
        def _():
            dg_ref[...] = jnp.zeros_like(dg_ref)

        dh = _dot(d_refs[0][...], w_refs[0][...])
        for p in range(1, n):
            dh = dh + _dot(d_refs[p][...], w_refs[p][...])
        dx, dg = _rms_bwd(x_ref[...], g_ref[...], dh)
        dg_ref[...] += dg
        dx_ref[...] = dres_ref[...] + dx

    return pl.pallas_call(
        body, name=name, grid=(s // ts,),
        in_specs=[_rows(ts, d.shape[1]) for d, _ in pieces] + [_resident(w.shape) for _, w in pieces]
        + [_rows(ts, D_MODEL), _full((1, D_MODEL)), _rows(ts, D_MODEL)],
        out_specs=[_rows(ts, D_MODEL), _full((1, D_MODEL))],
        out_shape=[jax.ShapeDtypeStruct((s, D_MODEL), F32), jax.ShapeDtypeStruct((1, D_MODEL), F32)],
        compiler_params=_params(("arbitrary",)),
    )(*[d for d, _ in pieces], *[w for _, w in pieces], x, g, dres)


def _loss_and_grad(y, target):
    s = y.shape[0]
    ts = _tile(s, 512)

    def body(y_ref, t_ref, dy_ref, loss_ref):
        @pl.when(pl.program_id(0) == 0)
        def _():
            loss_ref[...] = jnp.zeros_like(loss_ref)

        err = y_ref[...] - t_ref[...]
        dy_ref[...] = err * (1.0 / D_MODEL)
        loss_ref[...] += 0.5 * jnp.sum(jnp.mean(err * err, axis=-1, keepdims=True), axis=0, keepdims=True)

    dy, loss = pl.pallas_call(
        body, name="loss", grid=(s // ts,),
        in_specs=[_rows(ts, D_MODEL), _rows(ts, D_MODEL)],
        out_specs=[_rows(ts, D_MODEL), _full((8, 128))],
        out_shape=[jax.ShapeDtypeStruct((s, D_MODEL), F32), jax.ShapeDtypeStruct((8, 128), F32)],
        compiler_params=_params(("arbitrary",)),
    )(y, target)
    return loss[0, 0], dy


_W_IN_SPLITS = [256, 384, 416, 672, 928, 1184, 1440, 1696, 1952]


def _rope_swap(w):
    half = QK_ROPE // 2
    return jnp.concatenate([-w[..., half:], w[..., :half]], axis=-1)


def _rope_unswap(d):
    half = QK_ROPE // 2
    return jnp.concatenate([d[..., half:], -d[..., :half]], axis=-1)


def _zeros_like_cols(w, n):
    return jnp.zeros(w.shape[:-1] + (n,), w.dtype)


def _derive_weights(w):
    md = MXU_DTYPE
    nl = w['w_in'].shape[0]
    c_q, c_kv, k_r, sg_u, sg_v, cv_x, cv_b, cv_c, pool, gate = jnp.split(w['w_in'].astype(md), _W_IN_SPLITS, axis=-1)
    pad_rope = lambda r: jnp.concatenate([_zeros_like_cols(r, QK_NOPE), r, _zeros_like_cols(r, HEAD_PAD - QK_NOPE - QK_ROPE)], -1)
    w_in_p = jnp.concatenate([gate, sg_u, sg_v, cv_b, cv_x, cv_c, pool, c_q, c_kv, pad_rope(k_r), pad_rope(_rope_swap(k_r))], -1)
    wq = w['w_uq'].astype(md).reshape(nl, Q_RANK, N_HEADS, QK_NOPE + QK_ROPE)
    nope, rope_w = wq[..., :QK_NOPE], wq[..., QK_NOPE:]
    wq_a = jnp.concatenate([nope, rope_w, _zeros_like_cols(nope, 32)], -1).reshape(nl, Q_RANK, N_HEADS * HEAD_PAD)
    wq_b = pad_rope(_rope_swap(rope_w)).reshape(nl, Q_RANK, N_HEADS * HEAD_PAD)
    wkv = w['w_ukv'].astype(md).reshape(nl, KV_RANK, N_HEADS, QK_NOPE + V_HEAD)
    pad_half = lambda r: jnp.concatenate([r, _zeros_like_cols(r, HEAD_PAD - r.shape[-1])], -1).reshape(nl, KV_RANK, N_HEADS * HEAD_PAD)
    w_br_mla = w['w_br_mla'].astype(md).reshape(nl, N_HEADS, V_HEAD, D_MODEL)
    w_br_mla_p = jnp.concatenate([w_br_mla, jnp.zeros_like(w_br_mla)], axis=2).reshape(nl, N_HEADS * HEAD_PAD, D_MODEL)
    eye = jnp.eye(4, dtype=md)
    wbd = (w['pool_w'].astype(md)[:, :, :, None, :] * eye[None, :, None, :, None]).reshape(nl, BR_WIDTH, BR_WIDTH)
    row = lambda a: a.astype(F32)[:, None, :]
    w_in_pt = jnp.swapaxes(w_in_p, 1, 2)
    return dict(
        w_in_p=w_in_p, wt_g=w_in_pt[:, COL_G:COL_M1], wt_m1=w_in_pt[:, COL_M1:COL_M2], wt_m2=w_in_pt[:, COL_M2:COL_B],
        wt_b=w_in_pt[:, COL_B:],
        wq=jnp.concatenate([wq_a, wq_b], -1), wkv=jnp.concatenate([pad_half(wkv[..., :QK_NOPE]), pad_half(wkv[..., QK_NOPE:])], -1),
        w_br_mla_p=w_br_mla_p, w_br_sg=w['w_br_sg'].astype(md), w_br_conv=w['w_br_conv'].astype(md),
        w_br_pool=w['w_br_pool'].astype(md), wbd=wbd, w_out=w['w_out'].astype(md),
        w_ff1=w['w_ff1'].astype(md), w_ff1t=jnp.swapaxes(w['w_ff1'].astype(md), 1, 2),
        w_ff2=w['w_ff2'].astype(md), w_ff2t=jnp.swapaxes(w['w_ff2'].astype(md), 1, 2),
        norm_mix_pre=row(w['norm_mix_pre']), gate_b=row(w['gate_b']), q_norm=row(w['q_norm']), kv_norm=row(w['kv_norm']),
        sg_ln_g=row(w['sg_ln_g']), sg_ln_b=row(w['sg_ln_b']), sg_w=w['sg_w'].astype(F32),
        sg_bias=jnp.repeat(jnp.swapaxes(w['sg_b'].astype(F32), 1, 2), BR_WIDTH // SG_GROUPS, axis=2),
        conv_w8=jnp.pad(w['conv_w'].astype(F32), ((0, 0), (0, 5), (0, 0))), pool_scale=row(w['pool_scale']),
        norm_mix_post=row(w['norm_mix_post']), norm_ffn_pre=row(w['norm_ffn_pre']), norm_ffn_post=row(w['norm_ffn_post']),
    )


def _rope_tables(positions):
    inv_freq = ROPE_BASE ** (-jnp.arange(0, QK_ROPE, 2, dtype=F32) / QK_ROPE)
    ang = positions.astype(F32)[:, None] * inv_freq
    cos, sin = jnp.cos(ang), jnp.sin(ang)
    n = positions.shape[0]
    ones, z64, z32 = jnp.ones((n, QK_NOPE), F32), jnp.zeros((n, QK_NOPE), F32), jnp.zeros((n, 32), F32)
    return (jnp.concatenate([ones, cos, cos, z32], 1), jnp.concatenate([z64, sin, sin, z32], 1),
            jnp.concatenate([z64, cos, cos, z32], 1))


def _reference_layout_grads(g):
    gate, dm1, dm2, dpb = g['dw_in_pieces']
    nl = gate.shape[0]
    sg_u, sg_v, cv_b = jnp.split(dm1, 3, axis=-1)
    cv_x, cv_c, pool = jnp.split(dm2, 3, axis=-1)
    c_q, c_kv, kr, krs = jnp.split(dpb, [Q_RANK, Q_RANK + KV_RANK, Q_RANK + KV_RANK + HEAD_PAD], axis=-1)
    rope_cols = slice(QK_NOPE, QK_NOPE + QK_ROPE)
    k_r = kr[..., rope_cols] + _rope_unswap(krs[..., rope_cols])
    w_in = jnp.concatenate([c_q, c_kv, k_r, sg_u, sg_v, cv_x, cv_b, cv_c, pool, gate], -1)
    hw = N_HEADS * HEAD_PAD
    dqa = g['dwq'][..., :hw].reshape(nl, Q_RANK, N_HEADS, HEAD_PAD)
    dqb = g['dwq'][..., hw:].reshape(nl, Q_RANK, N_HEADS, HEAD_PAD)
    w_uq = jnp.concatenate([dqa[..., :QK_NOPE], dqa[..., rope_cols] + _rope_unswap(dqb[..., rope_cols])], -1)
    dka = g['dwkv'][..., :hw].reshape(nl, KV_RANK, N_HEADS, HEAD_PAD)
    dva = g['dwkv'][..., hw:].reshape(nl, KV_RANK, N_HEADS, HEAD_PAD)
    w_ukv = jnp.concatenate([dka[..., :QK_NOPE], dva[..., :V_HEAD]], -1)
    w_br_mla = g['dw_br_mla_p'].reshape(nl, N_HEADS, HEAD_PAD, D_MODEL)[:, :, :V_HEAD]
    dwbd = g['dwbd'].reshape(nl, 4, 64, 4, 64)
    pool_w = jnp.stack([dwbd[:, k, :, k, :] for k in range(4)], axis=1)
    sq = lambda a: a[:, 0, :]
    return dict(
        norm_mix_pre=sq(g['dg_pre']), w_in=w_in, gate_b=sq(g['dgate_b']), q_norm=sq(g['dq_norm']),
        w_uq=w_uq.reshape(nl, Q_RANK, -1), kv_norm=sq(g['dkv_norm']), w_ukv=w_ukv.reshape(nl, KV_RANK, -1),
        w_br_mla=w_br_mla.reshape(nl, N_HEADS * V_HEAD, D_MODEL), sg_ln_g=sq(g['dln_g']), sg_ln_b=sq(g['dln_b']),
        sg_w=g['dsg_w'], sg_b=jnp.swapaxes(g['dsg_b'][:, :, :SG_GROUPS], 1, 2), w_br_sg=g['dw_br_sg'],
        conv_w=g['dconv_w'][:, :3], w_br_conv=g['dw_br_conv'], pool_w=pool_w, pool_scale=sq(g['dpool_scale']),
        w_br_pool=g['dw_br_pool'], w_out=g['dw_out'], norm_mix_post=sq(g['dg_post']), norm_ffn_pre=sq(g['dg_fpre']),
        w_ff1=g['dw_ff1'], w_ff2=g['dw_ff2'], norm_ffn_post=sq(g['dg_fpost']))


def _layer_forward(x0, lw, tabs, carried):
    proj = _mm("in_proj", x0, lw['w_in_p'], tm=1024, tn=N_PROJ, prologue=_rms, rows=(lw['norm_mix_pre'],))
    q, k, v = _qkv_prep(proj, lw['q_norm'], lw['kv_norm'], lw['wq'], lw['wkv'], *tabs)
    (o, lse), carried_out = _attn_fwd(q, k, v, carried)
    x1 = _mix_fwd(x0, proj, o, lw)
    a = _mm("ffn1", x1, lw['w_ff1'], tm=1024, tn=D_FF, prologue=_rms, rows=(lw['norm_ffn_pre'],))
    x2, f = _ffn2(a, lw['w_ff2'], x1, lw['norm_ffn_post'])
    return x2, dict(x0=x0, proj=proj, q=q, k=k, v=v, o=o, lse=lse, x1=x1, a=a, f=f), carried_out


def _layer_backward(dx2, lw, sv, tabs, early, late):
    g = {}
    (df, da, g['dg_fpost']), early_out = _ffn2_bwd(dx2, sv['f'], lw['norm_ffn_post'], sv['a'], lw['w_ff2t'], early)
    carried = late(early_out)
    g['dw_ff2'] = _mm_tn("dw_ff2", sv['a'], df, tm=512, tn=1024, prologue=_relu_sq)
    dx1, g['dg_fpre'] = _norm_in_bwd("ffn1_bwd", [(da, lw['w_ff1t'])], sv['x1'], lw['norm_ffn_pre'], dx2)
    g['dw_ff1'] = _mm_tn("dw_ff1", sv['x1'], da, tm=512, tn=2048, prologue=_rms, rows=(lw['norm_ffn_pre'],))
    (dgate, dm1, dyv, upool, do, delta, g['dgate_b'], g['dln_g'], g['dln_b'], g['dsg_w'], g['dsg_b'], g['dconv_w'],
     g['dwbd'], g['dpool_scale'], g['dw_br_mla_p'], g['dw_br_sg'], g['dw_br_conv'], g['dw_br_pool'], g['dw_out'],
     g['dg_post']) = _mix_bwd(dx1, sv['proj'], sv['o'], lw)
    dm2 = _shift_bwd(dyv, upool, sv['proj'], lw['conv_w8'])
    (dq, dk, dv), carried_out = _attn_bwd(sv['q'], sv['k'], sv['v'], do, sv['lse'], delta, carried)
    dpb, g['dwq'], g['dwkv'], g['dq_norm'], g['dkv_norm'] = _qkv_bwd(
        dq, dk, dv, sv['proj'], lw['q_norm'], lw['kv_norm'], lw['wq'], lw['wkv'], *tabs)
    pieces = [(dgate, lw['wt_g']), (dm1, lw['wt_m1']), (dm2, lw['wt_m2']), (dpb, lw['wt_b'])]
    dx0, g['dg_pre'] = _norm_in_bwd("in_proj_bwd", pieces, sv['x0'], lw['norm_mix_pre'], dx1)
    g['dw_in_pieces'] = [_mm_tn("dw_in_%d" % n, sv['x0'], d, tm=512, tn=2048, prologue=_rms, rows=(lw['norm_mix_pre'],))
                         for n, (d, _) in enumerate(pieces)]
    return dx0, g, carried_out


class _StepPlan(NamedTuple):
    n_layers: int
    weights_exchange: Callable
    weights_from: Callable
    grads_ready: Callable
    first_done: Callable
    second_done: Callable


def _local_step(x, positions, target, plan):
    tabs = _rope_tables(positions)
    derive = lambda w: {n: a[0] for n, a in _derive_weights(w).items()}
    first = plan.weights_exchange(0)
    weights = plan.weights_from(0, None if first is None else _run_exchange("gather_weights", first))
    derived, saved = [], []
    for l in range(plan.n_layers):
        derived.append(derive(weights))
        coming = plan.weights_exchange(l + 1) if l + 1 < plan.n_layers else None
        x, sv, arrived = _layer_forward(x, derived[l], tabs, coming)
        saved.append(sv)
        if l + 1 < plan.n_layers:
            weights = plan.weights_from(l + 1, arrived)
    loss, dx = _loss_and_grad(x, target)
    pending = None
    for l in reversed(range(plan.n_layers)):
        if pending is None:
            dx, g, _ = _layer_backward(dx, derived[l], saved[l], tabs, None, lambda _: None)
        else:
            dx, g, arrived = _layer_backward(dx, derived[l], saved[l], tabs, pending[1],
                                             functools.partial(plan.first_done, pending[0]))
            plan.second_done(pending[0], arrived)
        lead = lambda a: [b[None] for b in a] if isinstance(a, list) else a[None]
        going = plan.grads_ready(l, _reference_layout_grads({n: lead(a) for n, a in g.items()}))
        pending = None if going is None else (l, going)
    if pending is not None:
        second = plan.first_done(pending[0], _run_exchange("grads_to_sibling", pending[1]))
        plan.second_done(pending[0], _run_exchange("grads_to_chips", second))
    return loss, dx


def _relative_peers():
    x, y = lax.axis_index("x"), lax.axis_index("y")
    return {1: (x, 1 - y), 2: (1 - x, y), 3: (1 - x, 1 - y)}


def _for_my_core(fn):
    def run():
        for half in (0, 1):
            pl.when(lax.axis_index("c") == half)(functools.partial(fn, half))
    return run


def _gather_exchange(split, whole):
    ns, nw = len(split), len(whole)
    n = ns + nw

    def phases(ins, outs, sems):
        ici_send, ici_recv, d2d_send, d2d_recv, own_send, own_recv = sems
        x, y, c = lax.axis_index("x"), lax.axis_index("y"), lax.axis_index("c")
        peers = _relative_peers()

        def rows(ref, which):
            h = ref.shape[-2] // 2
            return ref.at[(slice(None),) * (len(ref.shape) - 2) + (slice(which * h, (which + 1) * h), slice(None))]

        def own(k):
            return pltpu.make_async_remote_copy(src_ref=ins[k], dst_ref=outs[k].at[0], send_sem=own_send.at[k],
                                                recv_sem=own_recv.at[k], device_id=(x, y, 1 - c), device_id_type=MESH)

        def over_ici(k, r, half):
            src = rows(ins[k], half) if k < ns else ins[k]
            dst = rows(outs[k].at[r], half) if k < ns else outs[k].at[r]
            return pltpu.make_async_remote_copy(src_ref=src, dst_ref=dst, send_sem=ici_send.at[3 * k + r - 1],
                                                recv_sem=ici_recv.at[3 * k + r - 1], device_id=(*peers[r], c), device_id_type=MESH)

        def to_sibling(k, r, half):
            landed = rows(outs[k].at[r], half)
            return pltpu.make_async_remote_copy(src_ref=landed, dst_ref=landed, send_sem=d2d_send.at[3 * k + r - 1],
                                                recv_sem=d2d_recv.at[3 * k + r - 1], device_id=(x, y, 1 - c), device_id_type=MESH)

        def start(half):
            for k in range(n):
                own(k).start()
                for r in peers:
                    over_ici(k, r, half).start()

        def middle(half):
            for k in range(n):
                for r in peers:
                    over_ici(k, r, half).wait_recv()
                    if k < ns:
                        to_sibling(k, r, half).start()

        def finish(half):
            for k in range(n):
                own(k).wait()
                for r in peers:
                    if k < ns:
                        to_sibling(k, r, 1 - half).wait_recv()
                        to_sibling(k, r, half).wait_send()
                    over_ici(k, r, half).wait_send()

        return _for_my_core(start), _for_my_core(middle), _for_my_core(finish)

    arrs = list(split) + list(whole)
    return _Exchange(
        operands=arrs, out_shape=[jax.ShapeDtypeStruct((4,) + a.shape, a.dtype) for a in arrs],
        scratch_shapes=[pltpu.SemaphoreType.DMA((3 * n,)), pltpu.SemaphoreType.DMA((3 * n,)), pltpu.SemaphoreType.DMA((3 * ns,)),
                        pltpu.SemaphoreType.DMA((3 * ns,)), pltpu.SemaphoreType.DMA((n,)), pltpu.SemaphoreType.DMA((n,))],
        phases=phases)


def _absolute_chip_order(relative):
    me = 2 * lax.axis_index("x") + lax.axis_index("y")
    return jnp.stack([lax.dynamic_index_in_dim(relative, jnp.bitwise_xor(me, chip), 0, keepdims=False) for chip in range(4)])


REDUCE_STEPS = 8


def _sibling_exchange(arrs):
    n = len(arrs)

    def phases(ins, theirs, sems):
        send_sems, recv_sems = sems
        x, y, c = lax.axis_index("x"), lax.axis_index("y"), lax.axis_index("c")

        def copy(k, my_half):
            h = ins[k].shape[1] // 2
            return pltpu.make_async_remote_copy(src_ref=ins[k].at[:, (1 - my_half) * h:(2 - my_half) * h, :], dst_ref=theirs[k],
                                                send_sem=send_sems.at[k], recv_sem=recv_sems.at[k],
                                                device_id=(x, y, 1 - c), device_id_type=MESH)

        def start(my_half):
            for k in range(n):
                copy(k, my_half).start()

        def finish(my_half):
            for k in range(n):
                copy(k, my_half).wait()

        return _for_my_core(start), lambda: None, _for_my_core(finish)

    return _Exchange(operands=list(arrs),
                     out_shape=[jax.ShapeDtypeStruct((a.shape[0], a.shape[1] // 2, a.shape[2]), a.dtype) for a in arrs],
                     scratch_shapes=[pltpu.SemaphoreType.DMA((n,)), pltpu.SemaphoreType.DMA((n,))], phases=phases)


def _add_sibling(name, arrs, theirs):
    n, steps = len(arrs), REDUCE_STEPS

    def body(*refs):
        for mine_ref, theirs_ref, out_ref in zip(refs[:n], refs[n:2 * n], refs[2 * n:]):
            out_ref[...] = (mine_ref[...] + theirs_ref[...]).astype(out_ref.dtype)

    block = lambda t: (4, t.shape[1] // steps, t.shape[2])
    return pl.pallas_call(
        body, name=name, grid=(steps,),
        in_specs=[pl.BlockSpec(block(t), lambda i: (0, lax.axis_index("c") * steps + i, 0)) for t in theirs]
        + [pl.BlockSpec(block(t), lambda i: (0, i, 0)) for t in theirs],
        out_specs=[pl.BlockSpec(block(t), lambda i: (0, i, 0)) for t in theirs],
        out_shape=[jax.ShapeDtypeStruct(t.shape, WIRE_DTYPE) for t in theirs],
        compiler_params=_params(("parallel",)))(*arrs, *theirs)


def _scatter_exchange(arrs):
    n = len(arrs)

    def phases(ins, outs, sems):
        send_sems, recv_sems = sems
        c = lax.axis_index("c")
        peers = _relative_peers()

        def copy(k, r):
            px, py = peers[r]
            return pltpu.make_async_remote_copy(src_ref=ins[k].at[2 * px + py], dst_ref=outs[k].at[r - 1],
                                                send_sem=send_sems.at[3 * k + r - 1], recv_sem=recv_sems.at[3 * k + r - 1],
                                                device_id=(px, py, c), device_id_type=MESH)

        def start():
            for k in range(n):
                for r in peers:
                    copy(k, r).start()

        def finish():
            for k in range(n):
                for r in peers:
                    copy(k, r).wait()

        return start, lambda: None, finish

    return _Exchange(operands=list(arrs), out_shape=[jax.ShapeDtypeStruct((3,) + a.shape[1:], a.dtype) for a in arrs],
                     scratch_shapes=[pltpu.SemaphoreType.DMA((3 * n,)), pltpu.SemaphoreType.DMA((3 * n,))], phases=phases)


def _sum_chips(name, chip_sums, arrived):
    n, steps = len(chip_sums), REDUCE_STEPS

    def body(*refs):
        for own_ref, arrived_ref, out_ref in zip(refs[:n], refs[n:2 * n], refs[2 * n:]):
            acc = own_ref[...].astype(F32)
            for r in range(3):
                acc = acc + arrived_ref[r].astype(F32)
            out_ref[...] = acc

    rows = lambda s: s.shape[1] // steps
    chip = lambda: 2 * lax.axis_index("x") + lax.axis_index("y")
    return pl.pallas_call(
        body, name=name, grid=(steps,),
        in_specs=[pl.BlockSpec((None, rows(s), s.shape[2]), lambda i: (chip(), i, 0)) for s in chip_sums]
        + [pl.BlockSpec((3, rows(s), s.shape[2]), lambda i: (0, i, 0)) for s in chip_sums],
        out_specs=[pl.BlockSpec((rows(s), s.shape[2]), lambda i: (lax.axis_index("c") * steps + i, 0)) for s in chip_sums],
        out_shape=[jax.ShapeDtypeStruct((2 * s.shape[1], s.shape[2]), F32) for s in chip_sums],
        compiler_params=_params(("parallel",)))(*chip_sums, *arrived)


def _join_siblings(name, bufs):
    n = len(bufs)

    def body(*refs):
        outs = refs[n:2 * n]
        send_sems, recv_sems = refs[2 * n:]
        x, y, c = lax.axis_index("x"), lax.axis_index("y"), lax.axis_index("c")

        def exchange(my_half):
            copies = []
            for k in range(n):
                h = outs[k].shape[0] // 2
                mine = outs[k].at[my_half * h:(my_half + 1) * h, :]
                theirs = outs[k].at[(1 - my_half) * h:(2 - my_half) * h, :]
                cp = pltpu.make_async_remote_copy(src_ref=mine, dst_ref=mine, send_sem=send_sems.at[k],
                                                  recv_sem=recv_sems.at[k], device_id=(x, y, 1 - c), device_id_type=MESH)
                cp.start()
                arrival = pltpu.make_async_remote_copy(src_ref=theirs, dst_ref=theirs, send_sem=send_sems.at[k],
                                                       recv_sem=recv_sems.at[k], device_id=(x, y, 1 - c), device_id_type=MESH)
                copies.append((cp, arrival))
            for cp, arrival in copies:
                arrival.wait_recv()
                cp.wait_send()

        for half in (0, 1):
            pl.when(c == half)(functools.partial(exchange, half))

    return pl.pallas_call(
        body, name=name, in_specs=[HBM] * n, out_specs=[HBM] * n,
        out_shape=[jax.ShapeDtypeStruct(b.shape, b.dtype) for b in bufs], input_output_aliases={k: k for k in range(n)},
        scratch_shapes=[pltpu.SemaphoreType.DMA((n,)), pltpu.SemaphoreType.DMA((n,))],
    )(*bufs)


def _gather_all_exchange(a):
    def phases(ins, outs, scratch):
        (a_ref,), (out_ref,) = ins, outs
        staging, send_sems, recv_sems, local_sem = scratch
        x, y, c = lax.axis_index("x"), lax.axis_index("y"), lax.axis_index("c")
        me = 4 * x + 2 * y + c
        flips = [(fx, fy, fc) for fx in (0, 1) for fy in (0, 1) for fc in (0, 1)][1:]
        peers = [(x ^ fx, y ^ fy, c ^ fc) for fx, fy, fc in flips]

        def copy(j):
            px, py, pc = peers[j]
            return pltpu.make_async_remote_copy(src_ref=a_ref, dst_ref=out_ref.at[me], send_sem=send_sems.at[j],
                                                recv_sem=recv_sems.at[j], device_id=(px, py, pc), device_id_type=MESH)

        def arrival(j):
            px, py, pc = peers[j]
            return pltpu.make_async_remote_copy(src_ref=a_ref, dst_ref=out_ref.at[4 * px + 2 * py + pc], send_sem=send_sems.at[j],
                                                recv_sem=recv_sems.at[j], device_id=(px, py, pc), device_id_type=MESH)

        own = pltpu.make_async_copy(staging, out_ref.at[me], local_sem)

        def start():
            load = pltpu.make_async_copy(a_ref, staging, local_sem)
            load.start()
            load.wait()
            own.start()
            for j in range(7):
                copy(j).start()

        def finish():
            for j in range(7):
                arrival(j).wait_recv()
            for j in range(7):
                copy(j).wait_send()
            own.wait()

        return start, lambda: None, finish

    return _Exchange(operands=[a], out_shape=[jax.ShapeDtypeStruct((8,) + a.shape, a.dtype)],
                     scratch_shapes=[pltpu.VMEM(a.shape, a.dtype), pltpu.SemaphoreType.DMA((7,)), pltpu.SemaphoreType.DMA((7,)),
                                     pltpu.SemaphoreType.DMA], phases=phases)


def _rowwise_call(name, fn, slots, out_shapes, steps, carried=None):
    n_in, n_out = [len(s) for s in slots], [len(o) for o in out_shapes]

    def spec(shape):
        if len(shape) == 3:
            return pl.BlockSpec((shape[0], shape[1] // steps, shape[2]), lambda i: (0, i, 0))
        return pl.BlockSpec((shape[0] // steps, shape[1]), lambda i: (i, 0))

    def body(*refs):
        ins, outs = refs[:sum(n_in)], refs[sum(n_in):]
        a = b = 0
        for k in range(len(slots)):
            for o_ref, val in zip(outs[b:b + n_out[k]], fn(*[r[...] for r in ins[a:a + n_in[k]]])):
                o_ref[...] = val
            a, b = a + n_in[k], b + n_out[k]

    flat_in = [arr for s in slots for arr in s]
    flat_out = [shp for o in out_shapes for shp in o]
    out, carried_out = _call_carrying(
        body, carried, name=name, grid=(steps,), in_specs=[spec(a.shape) for a in flat_in],
        out_specs=[spec(s) for s in flat_out], out_shape=[jax.ShapeDtypeStruct(s, F32) for s in flat_out],
        operands=flat_in, semantics=("parallel",))
    grouped, b = [], 0
    for k in range(len(slots)):
        grouped.append(out[b:b + n_out[k]])
        b += n_out[k]
    return grouped if carried is None else (grouped, carried_out)


def _sum_in_order(a):
    acc = a[0].astype(F32)
    for k in range(1, a.shape[0]):
        acc = acc + a[k].astype(F32)
    return (acc,)


def _adamw_math(w, g, m, v):
    m_new = ADAM_B1 * m + (1.0 - ADAM_B1) * g
    v_new = ADAM_B2 * v + (1.0 - ADAM_B2) * (g * g)
    m_hat = m_new / (1.0 - ADAM_B1 ** ADAM_STEP)
    v_hat = v_new / (1.0 - ADAM_B2 ** ADAM_STEP)
    return -ADAM_LR * (m_hat / (jnp.sqrt(v_hat) + ADAM_EPS) + ADAM_WD * w), m_new, v_new


W_IN_ADAMW_STEPS = 21
SMALL_PACK_COLS = 256
SMALL_PACK_ROWS = 2048


def _pack_small(parts):
    wide = [jnp.pad(p, ((0, 0), (0, 0), (0, SMALL_PACK_COLS - p.shape[2]))) for p in parts]
    rows = jnp.concatenate(wide, axis=1)
    return jnp.pad(rows, ((0, 0), (0, SMALL_PACK_ROWS - rows.shape[1]), (0, 0)))


def _unpack_small(packed, shapes):
    out, row = [], 0
    for a, b in shapes:
        out.append(packed[:, row:row + a, :b])
        row += a
    return out


def _pack(arrs, rows_per_layer, dtype):
    nl = arrs[0].shape[0]
    flat = jnp.concatenate([a.astype(dtype).reshape(nl, -1) for a in arrs], axis=1)
    flat = jnp.pad(flat, ((0, 0), (0, rows_per_layer * PACK_COLS - flat.shape[1])))
    return flat.reshape(nl * rows_per_layer, PACK_COLS)


def _unpack(packed, shapes, rows_per_layer):
    nl = shapes[0][0]
    flat = packed.reshape(packed.shape[:-2] + (nl, rows_per_layer * PACK_COLS))
    out, off = [], 0
    for shp in shapes:
        size = math.prod(shp[1:])
        out.append(flat[..., off:off + size].reshape(packed.shape[:-2] + tuple(shp)))
        off += size
    return out


def _rows_needed(shapes, multiple):
    per_layer = sum(math.prod(s[1:]) for s in shapes)
    rows = -(-per_layer // PACK_COLS)
    return -(-rows // multiple) * multiple


CONV_TILE = (8, 128)


def _layer_shard_exchange(w, l):
    conv = w['conv_w'][l].reshape(-1)
    conv = jnp.pad(conv, (0, math.prod(CONV_TILE) - conv.shape[0])).reshape(CONV_TILE)
    return _gather_exchange([w[n][l].astype(MXU_DTYPE) for n in MATMUL_SHARDED], [conv])


def _layer_full_weights(w, l, gathered):
    gathered = [_absolute_chip_order(g) for g in gathered]
    full = {n: w[n][l:l + 1] for n in WEIGHTS}
    for n, part in zip(MATMUL_SHARDED, gathered):
        if n in ROW_SHARDED:
            full[n] = part.reshape(1, 4 * part.shape[1], part.shape[2])
        else:
            full[n] = jnp.swapaxes(part, 0, 1).reshape(1, part.shape[1], 4 * part.shape[2])
    rows, cols = w['conv_w'].shape[1:]
    conv = gathered[-1].reshape(4, -1)[:, :rows * cols].reshape(4, rows, cols)
    full['conv_w'] = jnp.swapaxes(conv, 0, 1).reshape(1, rows, 4 * cols)
    return full


def _chip_major(n, g):
    nl = g.shape[0]
    if n in ROW_SHARDED:
        return jnp.swapaxes(g.reshape(nl, 4, g.shape[1] // 4, g.shape[2]), 0, 1)
    return jnp.transpose(g.reshape(nl, g.shape[1], 4, g.shape[2] // 4), (2, 0, 1, 3))


def kernel(x, positions, norm_mix_pre, w_in, gate_b, q_norm, w_uq, kv_norm, w_ukv, w_br_mla, sg_ln_g, sg_ln_b, sg_w, sg_b, w_br_sg, conv_w, w_br_conv, pool_w, pool_scale, w_br_pool, w_out, norm_mix_post, norm_ffn_pre, w_ff1, w_ff2, norm_ffn_post, loss_target, m_norm_mix_pre, m_w_in, m_gate_b, m_q_norm, m_w_uq, m_kv_norm, m_w_ukv, m_w_br_mla, m_sg_ln_g, m_sg_ln_b, m_sg_w, m_sg_b, m_w_br_sg, m_conv_w, m_w_br_conv, m_pool_w, m_pool_scale, m_w_br_pool, m_w_out, m_norm_mix_post, m_norm_ffn_pre, m_w_ff1, m_w_ff2, m_norm_ffn_post, v_norm_mix_pre, v_w_in, v_gate_b, v_q_norm, v_w_uq, v_kv_norm, v_w_ukv, v_w_br_mla, v_sg_ln_g, v_sg_ln_b, v_sg_w, v_sg_b, v_w_br_sg, v_conv_w, v_w_br_conv, v_pool_w, v_pool_scale, v_w_br_pool, v_w_out, v_norm_mix_post, v_norm_ffn_pre, v_w_ff1, v_w_ff2, v_norm_ffn_post):
    given = dict(locals())
    w = {n: given[n] for n in WEIGHTS}
    mom = {n: given['m_' + n] for n in WEIGHTS}
    var = {n: given['v_' + n] for n in WEIGHTS}
    nl = w['w_in'].shape[0]
    sharded, chip_sums, reduced, local_small = {}, {}, {}, {}

    def grads_ready(l, g):
        sharded[l] = [_chip_major(n, g[n])[:, 0] for n in BIG_SHARDED]
        sharded[l].append(_pack_small([_chip_major(n, g[n])[:, 0] for n in SMALL_SHARDED]))
        local_small[l] = [g[n] for n in REPLICATED + ['conv_w']]
        return _sibling_exchange(sharded[l])

    def sibling_done(l, theirs):
        chip_sums[l] = _add_sibling("add_sibling", sharded[l], theirs)
        return _scatter_exchange(chip_sums[l])

    def chips_done(l, arrived):
        reduced[l] = _join_siblings("join_halves", _sum_chips("sum_chips", chip_sums[l], arrived))

    plan = _StepPlan(n_layers=nl, weights_exchange=functools.partial(_layer_shard_exchange, w),
                     weights_from=functools.partial(_layer_full_weights, w), grads_ready=grads_ready,
                     first_done=sibling_done, second_done=chips_done)
    loss, dx = _local_step(x[0], positions[0], loss_target[0], plan)
    loss = lax.psum(loss, ("x", "y", "c"))

    grad, delta, new_m, new_v = {}, {}, {}, {}
    names = REPLICATED + ['conv_w']
    local = [jnp.concatenate([local_small[l][k] for l in range(nl)]) for k in range(len(names))]
    rows = _rows_needed([a.shape for a in local], 32)
    small_grads = _gather_all_exchange(_pack(local, rows, F32))
    for k, n in enumerate(BIG_SHARDED):
        grad[n] = jnp.stack([reduced[l][k] for l in range(nl)])
    turned = [jnp.swapaxes(a, 1, 2) for a in (w['w_in'], grad['w_in'], mom['w_in'], var['w_in'])]
    (w_in_out,) = _rowwise_call("adamw_w_in", _adamw_math, [turned], [[turned[0].shape] * 3], W_IN_ADAMW_STEPS)
    delta['w_in'], new_m['w_in'], new_v['w_in'] = [jnp.swapaxes(a, 1, 2) for a in w_in_out]
    others = [n for n in BIG_SHARDED if n != 'w_in']
    slots = [[w[n], grad[n], mom[n], var[n]] for n in others]
    small_pack = lambda d: _pack_small([d[n] for n in SMALL_SHARDED])
    g_small = jnp.stack([reduced[l][-1] for l in range(nl)])
    slots.append([small_pack(w), g_small, small_pack(mom), small_pack(var)])
    updated, (everyone,) = _rowwise_call("adamw_sharded", _adamw_math, slots, [[s_[0].shape] * 3 for s_ in slots], 32,
                                         carried=small_grads)
    for k, n in enumerate(others):
        delta[n], new_m[n], new_v[n] = updated[k]
    for d, packed in zip((grad, delta, new_m, new_v), [g_small] + list(updated[-1])):
        d.update(zip(SMALL_SHARDED, _unpack_small(packed, [w[n].shape[1:] for n in SMALL_SHARDED])))

    (summed,), = _rowwise_call("sum_devices", _sum_in_order, [[everyone]], [[everyone.shape[1:]]], 4)
    g_rep = _unpack(summed, [a.shape for a in local], rows)
    chip = 2 * lax.axis_index("x") + lax.axis_index("y")
    g_rep[-1] = lax.dynamic_slice_in_dim(g_rep[-1], chip * w['conv_w'].shape[2], w['conv_w'].shape[2], axis=2)
    rep_pack = lambda arrs: _pack(arrs, rows, F32)
    (rep_out,) = _rowwise_call("adamw_replicated", _adamw_math,
                               [[rep_pack([w[n] for n in names]), rep_pack(g_rep), rep_pack([mom[n] for n in names]),
                                 rep_pack([var[n] for n in names])]], [[(nl * rows, PACK_COLS)] * 3], 4)
    grad.update(zip(names, g_rep))
    for d, packed in zip((delta, new_m, new_v), rep_out):
        d.update(zip(names, _unpack(packed, [w[n].shape for n in names], rows)))

    return (loss, dx[None], *[grad[n] for n in WEIGHTS], *[delta[n] for n in WEIGHTS], *[new_m[n] for n in WEIGHTS],
            *[new_v[n] for n in WEIGHTS])
```

```python
import functools
import math
from typing import Any, Callable, NamedTuple, Sequence

import jax
import jax.numpy as jnp
from jax import lax
from jax.experimental import pallas as pl
from jax.experimental.pallas import tpu as pltpu

F32 = jnp.float32
MXU_DTYPE = jnp.bfloat16
WIRE_DTYPE = jnp.bfloat16
ACT_DTYPE = jnp.bfloat16
MESH = pl.DeviceIdType.MESH

D_MODEL = 1024
D_FF = 4096
N_HEADS = 4
QK_NOPE = 64
QK_ROPE = 32
V_HEAD = 64
HEAD_PAD = 128
Q_RANK = 256
KV_RANK = 128
SG_CHUNK = 128
SG_GROUPS = 4
BR_WIDTH = 256
N_BRANCH = 4
HALO = 16
ROPE_BASE = 10000.0
EPS = 1e-6
ATTN_SCALE = (QK_NOPE + QK_ROPE) ** -0.5
LOG2_E = math.log2(math.e)
FWD_HEADS_PER_STEP = 4
HEADS_PER_STEP = 2
N_PROJ = N_BRANCH * D_MODEL + 6 * BR_WIDTH + Q_RANK + KV_RANK + 2 * HEAD_PAD
COL_G, COL_M1, COL_M2, COL_B = 0, 4096, 4864, 5632

ADAM_LR, ADAM_B1, ADAM_B2, ADAM_EPS, ADAM_WD, ADAM_STEP = 0.001, 0.9, 0.999, 1e-08, 0.01, 10

VMEM_LIMIT = 56 * 1024 * 1024

WEIGHTS = ['norm_mix_pre', 'w_in', 'gate_b', 'q_norm', 'w_uq', 'kv_norm', 'w_ukv', 'w_br_mla', 'sg_ln_g', 'sg_ln_b',
           'sg_w', 'sg_b', 'w_br_sg', 'conv_w', 'w_br_conv', 'pool_w', 'pool_scale', 'w_br_pool', 'w_out',
           'norm_mix_post', 'norm_ffn_pre', 'w_ff1', 'w_ff2', 'norm_ffn_post']
ROW_SHARDED = ['w_out', 'w_ff2']
MATMUL_SHARDED = ['w_in', 'w_uq', 'w_ukv', 'w_br_mla', 'w_br_sg', 'w_br_conv', 'w_br_pool', 'w_out', 'w_ff1', 'w_ff2']
BIG_SHARDED = ['w_in', 'w_ff1', 'w_ff2', 'w_out']
SMALL_SHARDED = ['w_uq', 'w_ukv', 'w_br_mla', 'w_br_sg', 'w_br_conv', 'w_br_pool']
SHARDED = MATMUL_SHARDED + ['conv_w']
REPLICATED = [n for n in WEIGHTS if n not in SHARDED]
PACK_COLS = 1024


def _params(sem, vmem=VMEM_LIMIT):
    return pltpu.CompilerParams(dimension_semantics=sem, vmem_limit_bytes=vmem)


def _mxu(a):
    return a.astype(MXU_DTYPE)


def _dot(a, b):
    return jnp.dot(_mxu(a), _mxu(b), preferred_element_type=F32)


def _dot_nt(a, b):
    return lax.dot_general(_mxu(a), _mxu(b), (((1,), (1,)), ((), ())), preferred_element_type=F32)


def _dot_tn(a, b):
    return lax.dot_general(_mxu(a), _mxu(b), (((0,), (0,)), ((), ())), preferred_element_type=F32)


def _rms(x, g):
    r = lax.rsqrt(jnp.mean(x * x, axis=-1, keepdims=True) + EPS)
    return x * r * g


def _rms_bwd(x, g, dy):
    r = lax.rsqrt(jnp.mean(x * x, axis=-1, keepdims=True) + EPS)
    xh = x * r
    gdy = dy * g
    dx = r * (gdy - xh * jnp.mean(gdy * xh, axis=-1, keepdims=True))
    return dx, jnp.sum(dy * xh, axis=0, keepdims=True)


_GELU_C = math.sqrt(2.0 / math.pi)


def _gelu(x):
    t = jnp.tanh(_GELU_C * (x + 0.044715 * (x * x * x)))
    return x * (0.5 * (1.0 + t)), t


def _gelu_grad(x, t):
    return 0.5 * (1.0 + t) + 0.5 * x * (1.0 - t * t) * (_GELU_C * (1.0 + 3.0 * 0.044715 * x * x))


def _sigmoid(x):
    return 1.0 / (1.0 + jnp.exp(-x))


def _full(shape):
    return pl.BlockSpec(shape, lambda *_: (0,) * len(shape))


def _resident(shape):
    return pl.BlockSpec(shape, lambda *_: (0,) * len(shape), pipeline_mode=pl.Buffered(1))


def _rows(ts, width, col=0):
    return pl.BlockSpec((ts, width), lambda i: (i, col))


def _tile(n, pref):
    return min(n, pref)


def _mm(name, a, w, *, tm, tn, prologue=None, rows=()):
    m, k = a.shape
    n = w.shape[1]
    tm, tn = _tile(m, tm), _tile(n, tn)

    def body(a_ref, *rest):
        row_refs, w_ref, o_ref = rest[:len(rows)], rest[len(rows)], rest[len(rows) + 1]
        av = a_ref[...]
        if prologue is not None:
            av = prologue(av, *[r[...] for r in row_refs])
        o_ref[...] = _dot(av, w_ref[...]).astype(o_ref.dtype)

    return pl.pallas_call(
        body, name=name, grid=(m // tm, n // tn),
        in_specs=[pl.BlockSpec((tm, k), lambda i, j: (i, 0))] + [pl.BlockSpec((1, k), lambda i, j: (0, 0)) for _ in rows]
        + [_resident((k, n)) if tn == n else pl.BlockSpec((k, tn), lambda i, j: (0, j))],
        out_specs=pl.BlockSpec((tm, tn), lambda i, j: (i, j)),
        out_shape=jax.ShapeDtypeStruct((m, n), ACT_DTYPE),
        compiler_params=_params(("parallel", "parallel")),
    )(a, *rows, w)


def _mm_tn(name, a, b, *, tm, tn, prologue=None, rows=()):
    m, k = a.shape
    n = b.shape[1]
    tm, tn = _tile(m, tm), _tile(n, tn)

    def body(a_ref, *rest):
        row_refs, b_ref, o_ref = rest[:len(rows)], rest[len(rows)], rest[len(rows) + 1]

        @pl.when(pl.program_id(1) == 0)
        def _():
            o_ref[...] = jnp.zeros_like(o_ref)

        av = a_ref[...]
        if prologue is not None:
            av = prologue(av, *[r[...] for r in row_refs])
        o_ref[...] += _dot_tn(av, b_ref[...])

    return pl.pallas_call(
        body, name=name, grid=(n // tn, m // tm),
        in_specs=[pl.BlockSpec((tm, k), lambda j, i: (i, 0))] + [pl.BlockSpec((1, k), lambda j, i: (0, 0)) for _ in rows]
        + [pl.BlockSpec((tm, tn), lambda j, i: (i, j))],
        out_specs=pl.BlockSpec((k, tn), lambda j, i: (0, j)),
        out_shape=jax.ShapeDtypeStruct((k, n), F32),
        compiler_params=_params(("parallel", "arbitrary")),
    )(a, *rows, b)


def _relu_sq(a):
    r = jnp.maximum(a.astype(F32), 0.0)
    return r * r


HBM = pl.BlockSpec(memory_space=pl.ANY)


class _Exchange(NamedTuple):
    operands: Sequence[Any]
    out_shape: Sequence[Any]
    scratch_shapes: Sequence[Any]
    phases: Callable


def _merge_exchanges(a, b):
    n_in, n_out, n_scratch = len(a.operands), len(a.out_shape), len(a.scratch_shapes)

    def phases(ins, outs, scratch):
        first = a.phases(ins[:n_in], outs[:n_out], scratch[:n_scratch])
        second = b.phases(ins[n_in:], outs[n_out:], scratch[n_scratch:])

        def both(k):
            def run():
                first[k]()
                second[k]()
            return run

        return both(0), both(1), both(2)

    return _Exchange(operands=list(a.operands) + list(b.operands), out_shape=list(a.out_shape) + list(b.out_shape),
                     scratch_shapes=list(a.scratch_shapes) + list(b.scratch_shapes), phases=phases)


def _run_exchange(name, ex):
    n_in, n_out = len(ex.operands), len(ex.out_shape)

    def body(*refs):
        for phase in ex.phases(refs[:n_in], refs[n_in:n_in + n_out], refs[n_in + n_out:]):
            phase()

    return pl.pallas_call(body, name=name, in_specs=[HBM] * n_in, out_specs=[HBM] * n_out, out_shape=list(ex.out_shape),
                          scratch_shapes=list(ex.scratch_shapes))(*ex.operands)


def _call_carrying(body, carried, *, name, grid, in_specs, out_specs, out_shape, operands, semantics, middle_step=None):
    if carried is None:
        return pl.pallas_call(body, name=name, grid=grid, in_specs=in_specs, out_specs=out_specs, out_shape=out_shape,
                              compiler_params=_params(semantics))(*operands), None
    n_main_in, n_main_out = len(operands), len(out_shape)
    n_in, n_out = len(carried.operands), len(carried.out_shape)
    steps = math.prod(grid)

    def wrapped(*refs):
        main_in, refs = refs[:n_main_in], refs[n_main_in:]
        ex_in, refs = refs[:n_in], refs[n_in:]
        main_out, refs = refs[:n_main_out], refs[n_main_out:]
        ex_out, sems = refs[:n_out], refs[n_out:]
        step = pl.program_id(0)
        for axis in range(1, len(grid)):
            step = step * grid[axis] + pl.program_id(axis)
        start, middle, finish = carried.phases(ex_in, ex_out, sems)
        pl.when(step == 0)(start)
        pl.when(step == ((steps - 1) // 2 if middle_step is None else middle_step))(middle)
        body(*main_in, *main_out)
        pl.when(step == steps - 1)(finish)

    out = pl.pallas_call(
        wrapped, name=name + "_carrying", grid=grid, in_specs=list(in_specs) + [HBM] * n_in,
        out_specs=list(out_specs) + [HBM] * n_out, out_shape=list(out_shape) + list(carried.out_shape),
        scratch_shapes=list(carried.scratch_shapes), compiler_params=_params(("arbitrary",) * len(grid)))(*operands, *carried.operands)
    return out[:n_main_out], out[n_main_out:]


def _qkv_prep(proj, q_norm, kv_norm, wq, wkv, cq_tab, s_tab, cr_tab):
    s = proj.shape[0]
    ts = _tile(s, 512)
    hw = N_HEADS * HEAD_PAD

    def body(cq_ref, ckv_ref, kr_ref, krs_ref, gq_ref, gkv_ref, wq_ref, wkv_ref, ct_ref, st_ref, crt_ref,
             q_ref, k_ref, v_ref):
        ct, st, crt = ct_ref[...], st_ref[...], crt_ref[...]
        qn = _rms(cq_ref[...].astype(F32), gq_ref[...])
        qab = _dot(qn, wq_ref[...])
        kvn = _rms(ckv_ref[...].astype(F32), gkv_ref[...])
        kav = _dot(kvn, wkv_ref[...])
        k_rope = kr_ref[...].astype(F32) * crt + krs_ref[...].astype(F32) * st
        ones_lane = (lax.broadcasted_iota(jnp.int32, (1, HEAD_PAD), 1) == V_HEAD).astype(F32)
        for h in range(N_HEADS):
            lo = h * HEAD_PAD
            q_ref[h] = (qab[:, lo:lo + HEAD_PAD] * ct + qab[:, hw + lo:hw + lo + HEAD_PAD] * st).astype(q_ref.dtype)
            k_ref[h] = (kav[:, lo:lo + HEAD_PAD] + k_rope).astype(k_ref.dtype)
            v_ref[h] = (kav[:, hw + lo:hw + lo + HEAD_PAD] + ones_lane).astype(v_ref.dtype)

    head_spec = pl.BlockSpec((N_HEADS, ts, HEAD_PAD), lambda i: (0, i, 0))
    head_shape = jax.ShapeDtypeStruct((N_HEADS, s, HEAD_PAD), MXU_DTYPE)
    return pl.pallas_call(
        body, name="qkv_prep", grid=(s // ts,),
        in_specs=[_rows(ts, Q_RANK, COL_B // Q_RANK), _rows(ts, KV_RANK, (COL_B + Q_RANK) // KV_RANK),
                  _rows(ts, HEAD_PAD, (COL_B + Q_RANK + KV_RANK) // HEAD_PAD),
                  _rows(ts, HEAD_PAD, (COL_B + Q_RANK + KV_RANK + HEAD_PAD) // HEAD_PAD),
                  _full((1, Q_RANK)), _full((1, KV_RANK)), _full((Q_RANK, 2 * hw)), _full((KV_RANK, 2 * hw)),
                  _rows(ts, HEAD_PAD), _rows(ts, HEAD_PAD), _rows(ts, HEAD_PAD)],
        out_specs=[head_spec, head_spec, head_spec],
        out_shape=[head_shape, head_shape, head_shape],
        compiler_params=_params(("parallel",)),
    )(proj, proj, proj, proj, q_norm, kv_norm, wq, wkv, cq_tab, s_tab, cr_tab)


def _diagonal_mask(t, keys_on_rows=False):
    key_axis = 0 if keys_on_rows else 1
    return lax.broadcasted_iota(jnp.int32, (t, t), key_axis) <= lax.broadcasted_iota(jnp.int32, (t, t), 1 - key_axis)


MIDDLE_WORK_SHARE = 0.65


def _causal_work_step(nq, groups):
    total = groups * nq * (nq + 1) // 2
    done = 0
    for step in range(groups * nq):
        if done >= MIDDLE_WORK_SHARE * total:
            return step
        done += step % nq + 1
    return groups * nq - 1


def _attn_fwd(q, k, v, carried=None):
    s = q.shape[1]
    t = _tile(s, 1024)

    def body(q_ref, k_ref, v_ref, o_ref, lse_ref):
        i = pl.program_id(1)
        lane = lax.broadcasted_iota(jnp.int32, (1, HEAD_PAD), 1)

        def step(j, carry, on_diagonal):
            rows = pl.ds(pl.multiple_of(j * t, t), t)
            out = []
            for h in range(FWD_HEADS_PER_STEP):
                m, acc = carry[h]
                sc = _dot_nt(q_ref[h], k_ref[h, rows, :]) * (ATTN_SCALE * LOG2_E)
                if on_diagonal:
                    sc = jnp.where(_diagonal_mask(t), sc, -jnp.inf)
                m_new = jnp.maximum(m, jnp.max(sc, axis=1, keepdims=True))
                out.append((m_new, jnp.exp2(m - m_new) * acc + _dot(jnp.exp2(sc - m_new), v_ref[h, rows, :])))
            return tuple(out)

        init = ((jnp.full((t, 1), -jnp.inf, F32), jnp.zeros((t, HEAD_PAD), F32)),) * FWD_HEADS_PER_STEP
        below = lax.fori_loop(0, i, functools.partial(step, on_diagonal=False), init)
        for h, (m, acc) in enumerate(step(i, below, True)):
            l = jnp.sum(jnp.where(lane == V_HEAD, acc, 0.0), axis=1, keepdims=True)
            o_ref[:, h * HEAD_PAD:(h + 1) * HEAD_PAD] = jnp.where(lane < V_HEAD, acc / l, 0.0)
            lse_ref[h] = m + jnp.log2(l)

    group = FWD_HEADS_PER_STEP
    return _call_carrying(
        body, carried, name="attn_fwd", grid=(N_HEADS // group, s // t),
        in_specs=[pl.BlockSpec((group, t, HEAD_PAD), lambda h, i: (h, i, 0)),
                  pl.BlockSpec((group, s, HEAD_PAD), lambda h, i: (h, 0, 0)),
                  pl.BlockSpec((group, s, HEAD_PAD), lambda h, i: (h, 0, 0))],
        out_specs=[pl.BlockSpec((t, group * HEAD_PAD), lambda h, i: (i, h)), pl.BlockSpec((group, t, 1), lambda h, i: (h, i, 0))],
        out_shape=[jax.ShapeDtypeStruct((s, N_HEADS * HEAD_PAD), F32), jax.ShapeDtypeStruct((N_HEADS, s, 1), F32)],
        operands=(q, k, v), semantics=("parallel", "parallel"), middle_step=_causal_work_step(s // t, N_HEADS // group))


def _attn_bwd(q, k, v, do, lse, delta, carried=None):
    s = q.shape[1]
    t = _tile(s, 1024)
    nq = s // t

    def body(q_ref, do_ref, lse_ref, dl_ref, k_ref, v_ref, dq_ref, dk_ref, dv_ref):
        j = pl.program_id(1)

        @pl.when(j == 0)
        def _():
            dq_ref[...] = jnp.zeros_like(dq_ref)

        def step(i, carry, on_diagonal):
            rows = pl.ds(pl.multiple_of(i * t, t), t)
            out = []
            for h in range(HEADS_PER_STEP):
                dk, dv = carry[h]
                qi, doi = q_ref[h, rows, :], do_ref[rows, h * HEAD_PAD:(h + 1) * HEAD_PAD]
                sc = _dot_nt(k_ref[h], qi) * (ATTN_SCALE * LOG2_E)
                if on_diagonal:
                    sc = jnp.where(_diagonal_mask(t, keys_on_rows=True), sc, -jnp.inf)
                p = jnp.exp2(sc - lse_ref[h, i])
                dv = dv + _dot(p, doi)
                ds = p * (_dot_nt(v_ref[h], doi) - dl_ref[h, i])
                dk = dk + _dot(ds, qi)
                dq_ref[h, rows, :] += _dot_tn(ds, k_ref[h]) * ATTN_SCALE
                out.append((dk, dv))
            return tuple(out)

        zero = ((jnp.zeros((t, HEAD_PAD), F32),) * 2,) * HEADS_PER_STEP
        sums = lax.fori_loop(j + 1, nq, functools.partial(step, on_diagonal=False), step(j, zero, True))
        for h, (dk, dv) in enumerate(sums):
            dk_ref[h] = dk * ATTN_SCALE
            dv_ref[h] = dv

    group = HEADS_PER_STEP
    whole = lambda w: pl.BlockSpec((group, s, w), lambda h, j: (h, 0, 0), pipeline_mode=pl.Buffered(1))
    tile = pl.BlockSpec((group, t, HEAD_PAD), lambda h, j: (h, j, 0))
    per_query = pl.BlockSpec((group, nq, 1, t), lambda h, j: (h, 0, 0, 0))
    head_shape = jax.ShapeDtypeStruct((N_HEADS, s, HEAD_PAD), F32)
    return _call_carrying(
        body, carried, name="attn_bwd", grid=(N_HEADS // group, nq),
        in_specs=[whole(HEAD_PAD), pl.BlockSpec((s, group * HEAD_PAD), lambda h, j: (0, h), pipeline_mode=pl.Buffered(1)),
                  per_query, per_query, tile, tile],
        out_specs=[whole(HEAD_PAD), tile, tile],
        out_shape=[head_shape, head_shape, head_shape],
        operands=(q, do, lse.reshape(N_HEADS, nq, 1, t), delta.reshape(N_HEADS, nq, 1, t), k, v),
        semantics=("parallel", "arbitrary"))


def _qkv_bwd(dq, dk, dv, proj, q_norm, kv_norm, wq, wkv, cq_tab, s_tab, cr_tab):
    s = proj.shape[0]
    ts = _tile(s, 512)
    hw = N_HEADS * HEAD_PAD

    def body(dq_ref, dk_ref, dv_ref, cq_ref, ckv_ref, gq_ref, gkv_ref, wq_ref, wkv_ref, ct_ref, st_ref, crt_ref,
             dpb_ref, dwq_ref, dwkv_ref, dgq_ref, dgkv_ref):
        @pl.when(pl.program_id(0) == 0)
        def _():
            for r in (dwq_ref, dwkv_ref, dgq_ref, dgkv_ref):
                r[...] = jnp.zeros_like(r)

        ct, st, crt = ct_ref[...], st_ref[...], crt_ref[...]
        dqs = [dq_ref[h] for h in range(N_HEADS)]
        dks = [dk_ref[h] for h in range(N_HEADS)]
        dqab = jnp.concatenate([d * ct for d in dqs] + [d * st for d in dqs], axis=1)
        dkav = jnp.concatenate(dks + [dv_ref[h] for h in range(N_HEADS)], axis=1)
        dk_sum = dks[0] + dks[1] + dks[2] + dks[3]
        cq, ckv, gq, gkv = cq_ref[...].astype(F32), ckv_ref[...].astype(F32), gq_ref[...], gkv_ref[...]
        dwq_ref[...] += _dot_tn(_rms(cq, gq), dqab)
        dwkv_ref[...] += _dot_tn(_rms(ckv, gkv), dkav)
        dcq, dgq = _rms_bwd(cq, gq, _dot_nt(dqab, wq_ref[...]))
        dckv, dgkv = _rms_bwd(ckv, gkv, _dot_nt(dkav, wkv_ref[...]))
        dgq_ref[...] += dgq
        dgkv_ref[...] += dgkv
        dpb_ref[...] = jnp.concatenate([dcq, dckv, dk_sum * crt, dk_sum * st], axis=1).astype(dpb_ref.dtype)

    head_spec = pl.BlockSpec((N_HEADS, ts, HEAD_PAD), lambda i: (0, i, 0))
    wb = Q_RANK + KV_RANK + 2 * HEAD_PAD
    return pl.pallas_call(
        body, name="qkv_bwd", grid=(s // ts,),
        in_specs=[head_spec, head_spec, head_spec,
                  _rows(ts, Q_RANK, COL_B // Q_RANK), _rows(ts, KV_RANK, (COL_B + Q_RANK) // KV_RANK),
                  _full((1, Q_RANK)), _full((1, KV_RANK)), _full((Q_RANK, 2 * hw)), _full((KV_RANK, 2 * hw)),
                  _rows(ts, HEAD_PAD), _rows(ts, HEAD_PAD), _rows(ts, HEAD_PAD)],
        out_specs=[_rows(ts, wb), _full((Q_RANK, 2 * hw)), _full((KV_RANK, 2 * hw)), _full((1, Q_RANK)), _full((1, KV_RANK))],
        out_shape=[jax.ShapeDtypeStruct((s, wb), MXU_DTYPE), jax.ShapeDtypeStruct((Q_RANK, 2 * hw), F32),
                   jax.ShapeDtypeStruct((KV_RANK, 2 * hw), F32), jax.ShapeDtypeStruct((1, Q_RANK), F32),
                   jax.ShapeDtypeStruct((1, KV_RANK), F32)],
        compiler_params=_params(("arbitrary",)),
    )(dq, dk, dv, proj, proj, q_norm, kv_norm, wq, wkv, cq_tab, s_tab, cr_tab)


def _lane_group(width):
    return lax.broadcasted_iota(jnp.int32, (1, width), 1) // (width // 4)


def _shift_down(a, k):
    return pltpu.roll(a, k, 0)


def _shift_up(a, k):
    return pltpu.roll(a, a.shape[0] - k, 0)


def _window_sums(xh, shift):
    s2 = xh + shift(xh, 1)
    s4 = s2 + shift(s2, 2)
    s8 = s4 + shift(s4, 4)
    s16 = s8 + shift(s8, 8)
    grp = _lane_group(xh.shape[1])
    return jnp.where(grp == 0, s2, jnp.where(grp == 1, s4, jnp.where(grp == 2, s8, s16)))


def _pool_count(i, ts):
    grp = _lane_group(BR_WIDTH)
    win = jnp.where(grp == 0, 2.0, jnp.where(grp == 1, 4.0, jnp.where(grp == 2, 8.0, 16.0)))
    t = (i * ts + lax.broadcasted_iota(jnp.int32, (ts, 1), 0)).astype(F32)
    return jnp.minimum(t + 1.0, win)


def _mix_forward(i, ts, r):
    f = {}
    act = lambda name: r[name][...].astype(F32)
    f['gates'] = _sigmoid(act('gate') + r['gate_b'][...])
    sgu, sgv = act('sgu'), act('sgv')
    f['sgu'], f['sgv'] = sgu, sgv
    u_act, f['tu'] = _gelu(sgu)
    vg, f['tv'] = _gelu(sgv)
    mu = jnp.mean(vg, axis=-1, keepdims=True)
    xc = vg - mu
    f['ln_r'] = lax.rsqrt(jnp.mean(xc * xc, axis=-1, keepdims=True) + EPS)
    f['ln_xh'] = xc * f['ln_r']
    vln = f['ln_xh'] * r['ln_g'][...] + r['ln_b'][...]
    tril = lax.broadcasted_iota(jnp.int32, (SG_CHUNK, SG_CHUNK), 1) <= lax.broadcasted_iota(jnp.int32, (SG_CHUNK, SG_CHUNK), 0)
    f['wm'] = [_mxu(jnp.where(tril, r['sg_w'][g], 0.0)) for g in range(SG_GROUPS)]
    f['tril'] = tril
    grp = _lane_group(BR_WIDTH)
    bias = r['sg_bias'][...]
    parts = []
    for ci in range(ts // SG_CHUNK):
        vc = vln[ci * SG_CHUNK:(ci + 1) * SG_CHUNK]
        sc = bias
        for g in range(SG_GROUPS):
            sc = sc + jnp.where(grp == g, _dot(f['wm'][g], vc), 0.0)
        parts.append(sc)
    f['vln'] = vln
    f['sg_s'] = parts[0] if len(parts) == 1 else jnp.concatenate(parts, axis=0)
    f['u_act'] = u_act
    out_b = u_act * f['sg_s']
    first = (i > 0).astype(F32)
    cvx, cvc, cvb = act('cvx'), act('cvc'), act('cvb')
    f['cvx'], f['cvc'], f['cvb'] = cvx, cvc, cvb
    zh = jnp.concatenate([act('hx') * act('hc') * first, cvc * cvx], axis=0)
    f['z1'] = _shift_down(zh, 1)[HALO:]
    f['z2'] = _shift_down(zh, 2)[HALO:]
    f['z0'] = zh[HALO:]
    f['yv'] = r['conv_w'][0:1, :] * f['z2'] + r['conv_w'][1:2, :] * f['z1'] + r['conv_w'][2:3, :] * f['z0']
    out_c = cvb * f['yv']
    p = act('pool')
    ph = jnp.concatenate([act('hp') * first, p], axis=0)
    f['cnt'] = _pool_count(i, ts)
    f['pooled'] = _window_sums(ph, _shift_down)[HALO:] / f['cnt'] - p
    f['mixed'] = _dot(f['pooled'], r['wbd'][...])
    out_d = f['mixed'] * r['pool_scale'][...]
    f['outs'] = [r['o'][...], out_b, out_c, out_d]
    f['ys'] = [_dot(f['outs'][b], r['w_br'][b][...]) for b in range(N_BRANCH)]
    merged = f['gates'][:, 0:D_MODEL] * f['ys'][0]
    for b in range(1, N_BRANCH):
        merged = merged + f['gates'][:, b * D_MODEL:(b + 1) * D_MODEL] * f['ys'][b]
    f['merged'] = merged
    f['mo'] = _dot(merged, r['w_out'][...])
    return f


_MIX_TILE_INPUTS = ['gate', 'sgu', 'sgv', 'cvb', 'cvx', 'cvc', 'pool', 'hx', 'hc', 'hp', 'o']
_MIX_WEIGHTS = ['gate_b', 'ln_g', 'ln_b', 'sg_w', 'sg_bias', 'conv_w', 'wbd', 'pool_scale', 'w_br0', 'w_br1', 'w_br2',
                'w_br3', 'w_out', 'g_post']


def _mix_specs(s, ts):
    c0 = COL_M1 // BR_WIDTH
    prev = lambda col: pl.BlockSpec((HALO, BR_WIDTH), lambda i: (jnp.maximum(i * (ts // HALO) - 1, 0), col))
    tiles = [_rows(ts, N_BRANCH * D_MODEL, 0), _rows(ts, BR_WIDTH, c0), _rows(ts, BR_WIDTH, c0 + 1), _rows(ts, BR_WIDTH, c0 + 2),
             _rows(ts, BR_WIDTH, c0 + 3), _rows(ts, BR_WIDTH, c0 + 4), _rows(ts, BR_WIDTH, c0 + 5),
             prev(c0 + 3), prev(c0 + 4), prev(c0 + 5), _rows(ts, N_HEADS * HEAD_PAD)]
    weights = [_full((1, N_BRANCH * D_MODEL)), _full((1, BR_WIDTH)), _full((1, BR_WIDTH)),
               _full((SG_GROUPS, SG_CHUNK, SG_CHUNK)), _full((SG_CHUNK, BR_WIDTH)), _full((8, BR_WIDTH)),
               _resident((BR_WIDTH, BR_WIDTH)), _full((1, BR_WIDTH)), _resident((N_HEADS * HEAD_PAD, D_MODEL)),
               _resident((BR_WIDTH, D_MODEL)), _resident((BR_WIDTH, D_MODEL)), _resident((BR_WIDTH, D_MODEL)),
               _resident((D_MODEL, D_MODEL)), _full((1, D_MODEL))]
    return tiles, weights


def _mix_refs(refs):
    names = _MIX_TILE_INPUTS + _MIX_WEIGHTS
    r = dict(zip(names, refs[:len(names)]))
    r['w_br'] = [r['w_br0'], r['w_br1'], r['w_br2'], r['w_br3']]
    return r, refs[len(names):]


def _mix_operands(proj, o, lw):
    return ([proj] * 10 + [o] + [lw[n] for n in ['gate_b', 'sg_ln_g', 'sg_ln_b', 'sg_w', 'sg_bias', 'conv_w8', 'wbd',
                                                 'pool_scale', 'w_br_mla_p', 'w_br_sg', 'w_br_conv', 'w_br_pool', 'w_out',
                                                 'norm_mix_post']])


def _mix_fwd(x0, proj, o, lw):
    s = x0.shape[0]
    ts = _tile(s, 512)
    tiles, weights = _mix_specs(s, ts)

    def body(*refs):
        r, (x0_ref, x1_ref) = _mix_refs(refs)
        f = _mix_forward(pl.program_id(0), ts, r)
        x1_ref[...] = x0_ref[...] + _rms(f['mo'], r['g_post'][...])

    return pl.pallas_call(
        body, name="mix_fwd", grid=(s // ts,),
        in_specs=tiles + weights + [_rows(ts, D_MODEL)],
        out_specs=_rows(ts, D_MODEL),
        out_shape=jax.ShapeDtypeStruct((s, D_MODEL), F32),
        compiler_params=_params(("parallel",)),
    )(*_mix_operands(proj, o, lw), x0)


def _mix_bwd(dx1, proj, o, lw):
    s = dx1.shape[0]
    ts = _tile(s, 256)
    tiles, weights = _mix_specs(s, ts)
    hw = N_HEADS * HEAD_PAD

    def body(*refs):
        r, rest = _mix_refs(refs)
        (dx1_ref, dg_ref, dm1_ref, dyv_ref, up_ref, do_ref, delta_ref,
         dgate_b_ref, dln_g_ref, dln_b_ref, dsgw_ref, dsgb_ref, dconv_ref, dwbd_ref, dps_ref,
         dwbr0_ref, dwbr1_ref, dwbr2_ref, dwbr3_ref, dwout_ref, dgpost_ref, dbias_acc) = rest
        i = pl.program_id(0)
        acc_refs = [dgate_b_ref, dln_g_ref, dln_b_ref, dsgw_ref, dsgb_ref, dconv_ref, dwbd_ref, dps_ref,
                    dwbr0_ref, dwbr1_ref, dwbr2_ref, dwbr3_ref, dwout_ref, dgpost_ref, dbias_acc]

        @pl.when(i == 0)
        def _():
            for a in acc_refs:
                a[...] = jnp.zeros_like(a)

        f = _mix_forward(i, ts, r)
        dmo, dgpost = _rms_bwd(f['mo'], r['g_post'][...], dx1_ref[...])
        dgpost_ref[...] += dgpost
        dwout_ref[...] += _dot_tn(f['merged'], dmo)
        dmerged = _dot_nt(dmo, r['w_out'][...])
        dwbr = [dwbr0_ref, dwbr1_ref, dwbr2_ref, dwbr3_ref]
        douts = []
        for b in range(N_BRANCH):
            gb = f['gates'][:, b * D_MODEL:(b + 1) * D_MODEL]
            dgate = dmerged * f['ys'][b] * gb * (1.0 - gb)
            dg_ref[:, b * D_MODEL:(b + 1) * D_MODEL] = dgate.astype(dg_ref.dtype)
            dgate_b_ref[:, b * D_MODEL:(b + 1) * D_MODEL] += jnp.sum(dgate, axis=0, keepdims=True)
            dy = dmerged * gb
            dwbr[b][...] += _dot_tn(f['outs'][b], dy)
            douts.append(_dot_nt(dy, r['w_br'][b][...]))
        do = douts[0]
        do_ref[...] = do.astype(do_ref.dtype)
        prod = do * f['outs'][0]
        for h in range(N_HEADS):
            delta_ref[h] = jnp.sum(prod[:, h * HEAD_PAD:(h + 1) * HEAD_PAD], axis=1, keepdims=True)
        grp = _lane_group(BR_WIDTH)
        ds = douts[1] * f['u_act']
        dsgu = douts[1] * f['sg_s'] * _gelu_grad(f['sgu'], f['tu'])
        dvln_parts = []
        for ci in range(ts // SG_CHUNK):
            rows = slice(ci * SG_CHUNK, (ci + 1) * SG_CHUNK)
            ds_c, vln_c = ds[rows], f['vln'][rows]
            dvln_c = jnp.zeros((SG_CHUNK, BR_WIDTH), F32)
            for g in range(SG_GROUPS):
                dvln_c = dvln_c + jnp.where(grp == g, _dot_tn(f['wm'][g], ds_c), 0.0)
                dsgw_ref[g] += jnp.where(f['tril'], _dot_nt(jnp.where(grp == g, ds_c, 0.0), vln_c), 0.0)
            dbias_acc[...] += ds_c
            dvln_parts.append(dvln_c)
        dvln = dvln_parts[0] if len(dvln_parts) == 1 else jnp.concatenate(dvln_parts, axis=0)
        dln_g_ref[...] += jnp.sum(dvln * f['ln_xh'], axis=0, keepdims=True)
        dln_b_ref[...] += jnp.sum(dvln, axis=0, keepdims=True)
        dxh = dvln * r['ln_g'][...]
        dvg = f['ln_r'] * (dxh - jnp.mean(dxh, axis=-1, keepdims=True)
                           - f['ln_xh'] * jnp.mean(dxh * f['ln_xh'], axis=-1, keepdims=True))
        dsgv = dvg * _gelu_grad(f['sgv'], f['tv'])
        dcvb = douts[2] * f['yv']
        dyv = douts[2] * f['cvb']
        dyv_ref[...] = dyv
        for kk, zk in enumerate((f['z2'], f['z1'], f['z0'])):
            dconv_ref[kk:kk + 1, :] += jnp.sum(dyv * zk, axis=0, keepdims=True)
        dps_ref[...] += jnp.sum(douts[3] * f['mixed'], axis=0, keepdims=True)
        dmixed = douts[3] * r['pool_scale'][...]
        dwbd_ref[...] += _dot_tn(f['pooled'], dmixed)
        up_ref[...] = _dot_nt(dmixed, r['wbd'][...]) / f['cnt']
        dm1_ref[...] = jnp.concatenate([dsgu, dsgv, dcvb], axis=1).astype(dm1_ref.dtype)

        @pl.when(i == pl.num_programs(0) - 1)
        def _():
            lane = lax.broadcasted_iota(jnp.int32, (1, SG_CHUNK), 1)
            db = dbias_acc[...]
            out = jnp.zeros((SG_CHUNK, SG_CHUNK), F32)
            for g in range(SG_GROUPS):
                out = out + jnp.where(lane == g, jnp.sum(jnp.where(grp == g, db, 0.0), axis=1, keepdims=True), 0.0)
            dsgb_ref[...] = out

    acc = lambda shape: (_full(shape), jax.ShapeDtypeStruct(shape, F32))
    accs = [acc((1, N_BRANCH * D_MODEL)), acc((1, BR_WIDTH)), acc((1, BR_WIDTH)), acc((SG_GROUPS, SG_CHUNK, SG_CHUNK)),
            acc((SG_CHUNK, SG_CHUNK)), acc((8, BR_WIDTH)), acc((BR_WIDTH, BR_WIDTH)), acc((1, BR_WIDTH)),
            acc((hw, D_MODEL)), acc((BR_WIDTH, D_MODEL)), acc((BR_WIDTH, D_MODEL)), acc((BR_WIDTH, D_MODEL)),
            acc((D_MODEL, D_MODEL)), acc((1, D_MODEL))]
    tile_outs = [(_rows(ts, N_BRANCH * D_MODEL), jax.ShapeDtypeStruct((s, N_BRANCH * D_MODEL), MXU_DTYPE)),
                 (_rows(ts, 3 * BR_WIDTH), jax.ShapeDtypeStruct((s, 3 * BR_WIDTH), MXU_DTYPE)),
                 (_rows(ts, BR_WIDTH), jax.ShapeDtypeStruct((s, BR_WIDTH), F32)),
                 (_rows(ts, BR_WIDTH), jax.ShapeDtypeStruct((s, BR_WIDTH), F32)),
                 (_rows(ts, hw), jax.ShapeDtypeStruct((s, hw), MXU_DTYPE)),
                 (pl.BlockSpec((N_HEADS, ts, 1), lambda i: (0, i, 0)), jax.ShapeDtypeStruct((N_HEADS, s, 1), F32))]
    outs = tile_outs + accs
    return pl.pallas_call(
        body, name="mix_bwd", grid=(s // ts,),
        in_specs=tiles + weights + [_rows(ts, D_MODEL)],
        out_specs=[o_[0] for o_ in outs], out_shape=[o_[1] for o_ in outs],
        scratch_shapes=[pltpu.VMEM((SG_CHUNK, BR_WIDTH), F32)],
        compiler_params=_params(("arbitrary",), 60 * 1024 * 1024),
    )(*_mix_operands(proj, o, lw), dx1)


def _shift_bwd(dyv, upool, proj, conv_w8):
    s = dyv.shape[0]
    ts = _tile(s, 512)
    nb = s // HALO
    c0 = COL_M1 // BR_WIDTH

    def body(dyv_ref, dyvn_ref, up_ref, upn_ref, cvx_ref, cvc_ref, cw_ref, out_ref):
        i = pl.program_id(0)
        last = (i < pl.num_programs(0) - 1).astype(F32)
        dh = jnp.concatenate([dyv_ref[...], dyvn_ref[...] * last], axis=0)
        dz = (cw_ref[2:3, :] * dh + cw_ref[1:2, :] * _shift_up(dh, 1) + cw_ref[0:1, :] * _shift_up(dh, 2))[:ts]
        up = up_ref[...]
        uh = jnp.concatenate([up, upn_ref[...] * last], axis=0)
        dpool = _window_sums(uh, _shift_up)[:ts] - up * _pool_count(i, ts)
        out_ref[...] = jnp.concatenate([dz * cvc_ref[...].astype(F32), dz * cvx_ref[...].astype(F32), dpool],
                                       axis=1).astype(out_ref.dtype)

    nxt = pl.BlockSpec((HALO, BR_WIDTH), lambda i: (jnp.minimum((i + 1) * (ts // HALO), nb - 1), 0))
    return pl.pallas_call(
        body, name="shift_bwd", grid=(s // ts,),
        in_specs=[_rows(ts, BR_WIDTH), nxt, _rows(ts, BR_WIDTH), nxt, _rows(ts, BR_WIDTH, c0 + 3), _rows(ts, BR_WIDTH, c0 + 4),
                  _full((8, BR_WIDTH))],
        out_specs=_rows(ts, 3 * BR_WIDTH),
        out_shape=jax.ShapeDtypeStruct((s, 3 * BR_WIDTH), MXU_DTYPE),
        compiler_params=_params(("parallel",)),
    )(dyv, dyv, upool, upool, proj, proj, conv_w8)


def _ffn2(a, w2, x1, g):
    s = a.shape[0]
    ts = _tile(s, 512)

    def body(a_ref, w_ref, x1_ref, g_ref, x2_ref, f_ref):
        f = _dot(_relu_sq(a_ref[...]), w_ref[...])
        f_ref[...] = f
        x2_ref[...] = x1_ref[...] + _rms(f, g_ref[...])

    return pl.pallas_call(
        body, name="ffn2", grid=(s // ts,),
        in_specs=[_rows(ts, D_FF), _resident((D_FF, D_MODEL)), _rows(ts, D_MODEL), _full((1, D_MODEL))],
        out_specs=[_rows(ts, D_MODEL), _rows(ts, D_MODEL)],
        out_shape=[jax.ShapeDtypeStruct((s, D_MODEL), F32)] * 2,
        compiler_params=_params(("parallel",)),
    )(a, w2, x1, g)


def _ffn2_bwd(dx2, f, g, a, w2t, carried=None):
    s = a.shape[0]
    ts = _tile(s, 512)

    def body(dx2_ref, f_ref, g_ref, a_ref, w_ref, df_ref, da_ref, dg_ref):
        @pl.when(pl.program_id(0) == 0)
        def _():
            dg_ref[...] = jnp.zeros_like(dg_ref)

        df, dg = _rms_bwd(f_ref[...], g_ref[...], dx2_ref[...])
        dg_ref[...] += dg
        df_ref[...] = df.astype(df_ref.dtype)
        da_ref[...] = (_dot(df, w_ref[...]) * (2.0 * jnp.maximum(a_ref[...].astype(F32), 0.0))).astype(da_ref.dtype)

    return _call_carrying(
        body, carried, name="ffn2_bwd", grid=(s // ts,),
        in_specs=[_rows(ts, D_MODEL), _rows(ts, D_MODEL), _full((1, D_MODEL)), _rows(ts, D_FF), _resident((D_MODEL, D_FF))],
        out_specs=[_rows(ts, D_MODEL), _rows(ts, D_FF), _full((1, D_MODEL))],
        out_shape=[jax.ShapeDtypeStruct((s, D_MODEL), MXU_DTYPE), jax.ShapeDtypeStruct((s, D_FF), MXU_DTYPE),
                   jax.ShapeDtypeStruct((1, D_MODEL), F32)],
        operands=(dx2, f, g, a, w2t), semantics=("arbitrary",))


def _norm_in_bwd(name, pieces, x, g, dres):
    s = x.shape[0]
    ts = _tile(s, 512)
    n = len(pieces)

    def body(*refs):
        d_refs, w_refs = refs[:n], refs[n:2 * n]
        x_ref, g_ref, dres_ref, dx_ref, dg_ref = refs[2 * n:]

        @pl.when(pl.program_id(0) == 0)
        def _():
            dg_ref[...] = jnp.zeros_like(dg_ref)

        dh = _dot(d_refs[0][...], w_refs[0][...])
        for p in range(1, n):
            dh = dh + _dot(d_refs[p][...], w_refs[p][...])
        dx, dg = _rms_bwd(x_ref[...], g_ref[...], dh)
        dg_ref[...] += dg
        dx_ref[...] = dres_ref[...] + dx

    return pl.pallas_call(
        body, name=name, grid=(s // ts,),
        in_specs=[_rows(ts, d.shape[1]) for d, _ in pieces] + [_resident(w.shape) for _, w in pieces]
        + [_rows(ts, D_MODEL), _full((1, D_MODEL)), _rows(ts, D_MODEL)],
        out_specs=[_rows(ts, D_MODEL), _full((1, D_MODEL))],
        out_shape=[jax.ShapeDtypeStruct((s, D_MODEL), F32), jax.ShapeDtypeStruct((1, D_MODEL), F32)],
        compiler_params=_params(("arbitrary",)),
    )(*[d for d, _ in pieces], *[w for _, w in pieces], x, g, dres)


def _loss_and_grad(y, target):
    s = y.shape[0]
    ts = _tile(s, 512)

    def body(y_ref, t_ref, dy_ref, loss_ref):
        @pl.when(pl.program_id(0) == 0)
        def _():
            loss_ref[...] = jnp.zeros_like(loss_ref)

        err = y_ref[...] - t_ref[...]
        dy_ref[...] = err * (1.0 / D_MODEL)
        loss_ref[...] += 0.5 * jnp.sum(jnp.mean(err * err, axis=-1, keepdims=True), axis=0, keepdims=True)

    dy, loss = pl.pallas_call(
        body, name="loss", grid=(s // ts,),
        in_specs=[_rows(ts, D_MODEL), _rows(ts, D_MODEL)],
        out_specs=[_rows(ts, D_MODEL), _full((8, 128))],
        out_shape=[jax.ShapeDtypeStruct((s, D_MODEL), F32), jax.ShapeDtypeStruct((8, 128), F32)],
        compiler_params=_params(("arbitrary",)),
    )(y, target)
    return loss[0, 0], dy


_W_IN_SPLITS = [256, 384, 416, 672, 928, 1184, 1440, 1696, 1952]


def _rope_swap(w):
    half = QK_ROPE // 2
    return jnp.concatenate([-w[..., half:], w[..., :half]], axis=-1)


def _rope_unswap(d):
    half = QK_ROPE // 2
    return jnp.concatenate([d[..., half:], -d[..., :half]], axis=-1)


def _zeros_like_cols(w, n):
    return jnp.zeros(w.shape[:-1] + (n,), w.dtype)


def _derive_weights(w):
    md = MXU_DTYPE
    nl = w['w_in'].shape[0]
    c_q, c_kv, k_r, sg_u, sg_v, cv_x, cv_b, cv_c, pool, gate = jnp.split(w['w_in'].astype(md), _W_IN_SPLITS, axis=-1)
    pad_rope = lambda r: jnp.concatenate([_zeros_like_cols(r, QK_NOPE), r, _zeros_like_cols(r, HEAD_PAD - QK_NOPE - QK_ROPE)], -1)
    w_in_p = jnp.concatenate([gate, sg_u, sg_v, cv_b, cv_x, cv_c, pool, c_q, c_kv, pad_rope(k_r), pad_rope(_rope_swap(k_r))], -1)
    wq = w['w_uq'].astype(md).reshape(nl, Q_RANK, N_HEADS, QK_NOPE + QK_ROPE)
    nope, rope_w = wq[..., :QK_NOPE], wq[..., QK_NOPE:]
    wq_a = jnp.concatenate([nope, rope_w, _zeros_like_cols(nope, 32)], -1).reshape(nl, Q_RANK, N_HEADS * HEAD_PAD)
    wq_b = pad_rope(_rope_swap(rope_w)).reshape(nl, Q_RANK, N_HEADS * HEAD_PAD)
    wkv = w['w_ukv'].astype(md).reshape(nl, KV_RANK, N_HEADS, QK_NOPE + V_HEAD)
    pad_half = lambda r: jnp.concatenate([r, _zeros_like_cols(r, HEAD_PAD - r.shape[-1])], -1).reshape(nl, KV_RANK, N_HEADS * HEAD_PAD)
    w_br_mla = w['w_br_mla'].astype(md).reshape(nl, N_HEADS, V_HEAD, D_MODEL)
    w_br_mla_p = jnp.concatenate([w_br_mla, jnp.zeros_like(w_br_mla)], axis=2).reshape(nl, N_HEADS * HEAD_PAD, D_MODEL)
    eye = jnp.eye(4, dtype=md)
    wbd = (w['pool_w'].astype(md)[:, :, :, None, :] * eye[None, :, None, :, None]).reshape(nl, BR_WIDTH, BR_WIDTH)
    row = lambda a: a.astype(F32)[:, None, :]
    w_in_pt = jnp.swapaxes(w_in_p, 1, 2)
    return dict(
        w_in_p=w_in_p, wt_g=w_in_pt[:, COL_G:COL_M1], wt_m1=w_in_pt[:, COL_M1:COL_M2], wt_m2=w_in_pt[:, COL_M2:COL_B],
        wt_b=w_in_pt[:, COL_B:],
        wq=jnp.concatenate([wq_a, wq_b], -1), wkv=jnp.concatenate([pad_half(wkv[..., :QK_NOPE]), pad_half(wkv[..., QK_NOPE:])], -1),
        w_br_mla_p=w_br_mla_p, w_br_sg=w['w_br_sg'].astype(md), w_br_conv=w['w_br_conv'].astype(md),
        w_br_pool=w['w_br_pool'].astype(md), wbd=wbd, w_out=w['w_out'].astype(md),
        w_ff1=w['w_ff1'].astype(md), w_ff1t=jnp.swapaxes(w['w_ff1'].astype(md), 1, 2),
        w_ff2=w['w_ff2'].astype(md), w_ff2t=jnp.swapaxes(w['w_ff2'].astype(md), 1, 2),
        norm_mix_pre=row(w['norm_mix_pre']), gate_b=row(w['gate_b']), q_norm=row(w['q_norm']), kv_norm=row(w['kv_norm']),
        sg_ln_g=row(w['sg_ln_g']), sg_ln_b=row(w['sg_ln_b']), sg_w=w['sg_w'].astype(F32),
        sg_bias=jnp.repeat(jnp.swapaxes(w['sg_b'].astype(F32), 1, 2), BR_WIDTH // SG_GROUPS, axis=2),
        conv_w8=jnp.pad(w['conv_w'].astype(F32), ((0, 0), (0, 5), (0, 0))), pool_scale=row(w['pool_scale']),
        norm_mix_post=row(w['norm_mix_post']), norm_ffn_pre=row(w['norm_ffn_pre']), norm_ffn_post=row(w['norm_ffn_post']),
    )


def _rope_tables(positions):
    inv_freq = ROPE_BASE ** (-jnp.arange(0, QK_ROPE, 2, dtype=F32) / QK_ROPE)
    ang = positions.astype(F32)[:, None] * inv_freq
    cos, sin = jnp.cos(ang), jnp.sin(ang)
    n = positions.shape[0]
    ones, z64, z32 = jnp.ones((n, QK_NOPE), F32), jnp.zeros((n, QK_NOPE), F32), jnp.zeros((n, 32), F32)
    return (jnp.concatenate([ones, cos, cos, z32], 1), jnp.concatenate([z64, sin, sin, z32], 1),
            jnp.concatenate([z64, cos, cos, z32], 1))


def _reference_layout_grads(g):
    gate, dm1, dm2, dpb = g['dw_in_pieces']
    nl = gate.shape[0]
    sg_u, sg_v, cv_b = jnp.split(dm1, 3, axis=-1)
    cv_x, cv_c, pool = jnp.split(dm2, 3, axis=-1)
    c_q, c_kv, kr, krs = jnp.split(dpb, [Q_RANK, Q_RANK + KV_RANK, Q_RANK + KV_RANK + HEAD_PAD], axis=-1)
    rope_cols = slice(QK_NOPE, QK_NOPE + QK_ROPE)
    k_r = kr[..., rope_cols] + _rope_unswap(krs[..., rope_cols])
    w_in = jnp.concatenate([c_q, c_kv, k_r, sg_u, sg_v, cv_x, cv_b, cv_c, pool, gate], -1)
    hw = N_HEADS * HEAD_PAD
    dqa = g['dwq'][..., :hw].reshape(nl, Q_RANK, N_HEADS, HEAD_PAD)
    dqb = g['dwq'][..., hw:].reshape(nl, Q_RANK, N_HEADS, HEAD_PAD)
    w_uq = jnp.concatenate([dqa[..., :QK_NOPE], dqa[..., rope_cols] + _rope_unswap(dqb[..., rope_cols])], -1)
    dka = g['dwkv'][..., :hw].reshape(nl, KV_RANK, N_HEADS, HEAD_PAD)
    dva = g['dwkv'][..., hw:].reshape(nl, KV_RANK, N_HEADS, HEAD_PAD)
    w_ukv = jnp.concatenate([dka[..., :QK_NOPE], dva[..., :V_HEAD]], -1)
    w_br_mla = g['dw_br_mla_p'].reshape(nl, N_HEADS, HEAD_PAD, D_MODEL)[:, :, :V_HEAD]
    dwbd = g['dwbd'].reshape(nl, 4, 64, 4, 64)
    pool_w = jnp.stack([dwbd[:, k, :, k, :] for k in range(4)], axis=1)
    sq = lambda a: a[:, 0, :]
    return dict(
        norm_mix_pre=sq(g['dg_pre']), w_in=w_in, gate_b=sq(g['dgate_b']), q_norm=sq(g['dq_norm']),
        w_uq=w_uq.reshape(nl, Q_RANK, -1), kv_norm=sq(g['dkv_norm']), w_ukv=w_ukv.reshape(nl, KV_RANK, -1),
        w_br_mla=w_br_mla.reshape(nl, N_HEADS * V_HEAD, D_MODEL), sg_ln_g=sq(g['dln_g']), sg_ln_b=sq(g['dln_b']),
        sg_w=g['dsg_w'], sg_b=jnp.swapaxes(g['dsg_b'][:, :, :SG_GROUPS], 1, 2), w_br_sg=g['dw_br_sg'],
        conv_w=g['dconv_w'][:, :3], w_br_conv=g['dw_br_conv'], pool_w=pool_w, pool_scale=sq(g['dpool_scale']),
        w_br_pool=g['dw_br_pool'], w_out=g['dw_out'], norm_mix_post=sq(g['dg_post']), norm_ffn_pre=sq(g['dg_fpre']),
        w_ff1=g['dw_ff1'], w_ff2=g['dw_ff2'], norm_ffn_post=sq(g['dg_fpost']))


def _layer_forward(x0, lw, tabs, carried):
    proj = _mm("in_proj", x0, lw['w_in_p'], tm=1024, tn=N_PROJ, prologue=_rms, rows=(lw['norm_mix_pre'],))
    q, k, v = _qkv_prep(proj, lw['q_norm'], lw['kv_norm'], lw['wq'], lw['wkv'], *tabs)
    (o, lse), carried_out = _attn_fwd(q, k, v, carried)
    x1 = _mix_fwd(x0, proj, o, lw)
    a = _mm("ffn1", x1, lw['w_ff1'], tm=1024, tn=D_FF, prologue=_rms, rows=(lw['norm_ffn_pre'],))
    x2, f = _ffn2(a, lw['w_ff2'], x1, lw['norm_ffn_post'])
    return x2, dict(x0=x0, proj=proj, q=q, k=k, v=v, o=o, lse=lse, x1=x1, a=a, f=f), carried_out


def _layer_backward(dx2, lw, sv, tabs, early, late):
    g = {}
    (df, da, g['dg_fpost']), early_out = _ffn2_bwd(dx2, sv['f'], lw['norm_ffn_post'], sv['a'], lw['w_ff2t'], early)
    carried = late(early_out)
    g['dw_ff2'] = _mm_tn("dw_ff2", sv['a'], df, tm=512, tn=1024, prologue=_relu_sq)
    dx1, g['dg_fpre'] = _norm_in_bwd("ffn1_bwd", [(da, lw['w_ff1t'])], sv['x1'], lw['norm_ffn_pre'], dx2)
    g['dw_ff1'] = _mm_tn("dw_ff1", sv['x1'], da, tm=512, tn=2048, prologue=_rms, rows=(lw['norm_ffn_pre'],))
    (dgate, dm1, dyv, upool, do, delta, g['dgate_b'], g['dln_g'], g['dln_b'], g['dsg_w'], g['dsg_b'], g['dconv_w'],
     g['dwbd'], g['dpool_scale'], g['dw_br_mla_p'], g['dw_br_sg'], g['dw_br_conv'], g['dw_br_pool'], g['dw_out'],
     g['dg_post']) = _mix_bwd(dx1, sv['proj'], sv['o'], lw)
    dm2 = _shift_bwd(dyv, upool, sv['proj'], lw['conv_w8'])
    (dq, dk, dv), carried_out = _attn_bwd(sv['q'], sv['k'], sv['v'], do, sv['lse'], delta, carried)
    dpb, g['dwq'], g['dwkv'], g['dq_norm'], g['dkv_norm'] = _qkv_bwd(
        dq, dk, dv, sv['proj'], lw['q_norm'], lw['kv_norm'], lw['wq'], lw['wkv'], *tabs)
    pieces = [(dgate, lw['wt_g']), (dm1, lw['wt_m1']), (dm2, lw['wt_m2']), (dpb, lw['wt_b'])]
    dx0, g['dg_pre'] = _norm_in_bwd("in_proj_bwd", pieces, sv['x0'], lw['norm_mix_pre'], dx1)
    g['dw_in_pieces'] = [_mm_tn("dw_in_%d" % n, sv['x0'], d, tm=512, tn=2048, prologue=_rms, rows=(lw['norm_mix_pre'],))
                         for n, (d, _) in enumerate(pieces)]
    return dx0, g, carried_out


class _StepPlan(NamedTuple):
    n_layers: int
    weights_exchange: Callable
    weights_from: Callable
    grads_ready: Callable
    first_done: Callable
    second_done: Callable


def _local_step(x, positions, target, plan):
    tabs = _rope_tables(positions)
    derive = lambda w: {n: a[0] for n, a in _derive_weights(w).items()}
    first = plan.weights_exchange(0)
    weights = plan.weights_from(0, None if first is None else _run_exchange("gather_weights", first))
    derived, saved = [], []
    for l in range(plan.n_layers):
        derived.append(derive(weights))
        coming = plan.weights_exchange(l + 1) if l + 1 < plan.n_layers else None
        x, sv, arrived = _layer_forward(x, derived[l], tabs, coming)
        saved.append(sv)
        if l + 1 < plan.n_layers:
            weights = plan.weights_from(l + 1, arrived)
    loss, dx = _loss_and_grad(x, target)
    pending = None
    for l in reversed(range(plan.n_layers)):
        if pending is None:
            dx, g, _ = _layer_backward(dx, derived[l], saved[l], tabs, None, lambda _: None)
        else:
            dx, g, arrived = _layer_backward(dx, derived[l], saved[l], tabs, pending[1],
                                             functools.partial(plan.first_done, pending[0]))
            plan.second_done(pending[0], arrived)
        lead = lambda a: [b[None] for b in a] if isinstance(a, list) else a[None]
        going = plan.grads_ready(l, _reference_layout_grads({n: lead(a) for n, a in g.items()}))
        pending = None if going is None else (l, going)
    if pending is not None:
        second = plan.first_done(pending[0], _run_exchange("grads_to_sibling", pending[1]))
        plan.second_done(pending[0], _run_exchange("grads_to_chips", second))
    return loss, dx


def _relative_peers():
    x, y = lax.axis_index("x"), lax.axis_index("y")
    return {1: (x, 1 - y), 2: (1 - x, y), 3: (1 - x, 1 - y)}


def _for_my_core(fn):
    def run():
        for half in (0, 1):
            pl.when(lax.axis_index("c") == half)(functools.partial(fn, half))
    return run


def _gather_exchange(split, whole):
    ns, nw = len(split), len(whole)
    n = ns + nw

    def phases(ins, outs, sems):
        ici_send, ici_recv, d2d_send, d2d_recv, own_send, own_recv = sems
        x, y, c = lax.axis_index("x"), lax.axis_index("y"), lax.axis_index("c")
        peers = _relative_peers()

        def rows(ref, which):
            h = ref.shape[-2] // 2
            return ref.at[(slice(None),) * (len(ref.shape) - 2) + (slice(which * h, (which + 1) * h), slice(None))]

        def own(k):
            return pltpu.make_async_remote_copy(src_ref=ins[k], dst_ref=outs[k].at[0], send_sem=own_send.at[k],
                                                recv_sem=own_recv.at[k], device_id=(x, y, 1 - c), device_id_type=MESH)

        def over_ici(k, r, half):
            src = rows(ins[k], half) if k < ns else ins[k]
            dst = rows(outs[k].at[r], half) if k < ns else outs[k].at[r]
            return pltpu.make_async_remote_copy(src_ref=src, dst_ref=dst, send_sem=ici_send.at[3 * k + r - 1],
                                                recv_sem=ici_recv.at[3 * k + r - 1], device_id=(*peers[r], c), device_id_type=MESH)

        def to_sibling(k, r, half):
            landed = rows(outs[k].at[r], half)
            return pltpu.make_async_remote_copy(src_ref=landed, dst_ref=landed, send_sem=d2d_send.at[3 * k + r - 1],
                                                recv_sem=d2d_recv.at[3 * k + r - 1], device_id=(x, y, 1 - c), device_id_type=MESH)

        def start(half):
            for k in range(n):
                own(k).start()
                for r in peers:
                    over_ici(k, r, half).start()

        def middle(half):
            for k in range(n):
                for r in peers:
                    over_ici(k, r, half).wait_recv()
                    if k < ns:
                        to_sibling(k, r, half).start()

        def finish(half):
            for k in range(n):
                own(k).wait()
                for r in peers:
                    if k < ns:
                        to_sibling(k, r, 1 - half).wait_recv()
                        to_sibling(k, r, half).wait_send()
                    over_ici(k, r, half).wait_send()

        return _for_my_core(start), _for_my_core(middle), _for_my_core(finish)

    arrs = list(split) + list(whole)
    return _Exchange(
        operands=arrs, out_shape=[jax.ShapeDtypeStruct((4,) + a.shape, a.dtype) for a in arrs],
        scratch_shapes=[pltpu.SemaphoreType.DMA((3 * n,)), pltpu.SemaphoreType.DMA((3 * n,)), pltpu.SemaphoreType.DMA((3 * ns,)),
                        pltpu.SemaphoreType.DMA((3 * ns,)), pltpu.SemaphoreType.DMA((n,)), pltpu.SemaphoreType.DMA((n,))],
        phases=phases)


def _absolute_chip_order(relative):
    me = 2 * lax.axis_index("x") + lax.axis_index("y")
    return jnp.stack([lax.dynamic_index_in_dim(relative, jnp.bitwise_xor(me, chip), 0, keepdims=False) for chip in range(4)])


REDUCE_STEPS = 8


def _sibling_exchange(arrs):
    n = len(arrs)

    def phases(ins, theirs, sems):
        send_sems, recv_sems = sems
        x, y, c = lax.axis_index("x"), lax.axis_index("y"), lax.axis_index("c")

        def copy(k, my_half):
            h = ins[k].shape[1] // 2
            return pltpu.make_async_remote_copy(src_ref=ins[k].at[:, (1 - my_half) * h:(2 - my_half) * h, :], dst_ref=theirs[k],
                                                send_sem=send_sems.at[k], recv_sem=recv_sems.at[k],
                                                device_id=(x, y, 1 - c), device_id_type=MESH)

        def start(my_half):
            for k in range(n):
                copy(k, my_half).start()

        def finish(my_half):
            for k in range(n):
                copy(k, my_half).wait()

        return _for_my_core(start), lambda: None, _for_my_core(finish)

    return _Exchange(operands=list(arrs),
                     out_shape=[jax.ShapeDtypeStruct((a.shape[0], a.shape[1] // 2, a.shape[2]), a.dtype) for a in arrs],
                     scratch_shapes=[pltpu.SemaphoreType.DMA((n,)), pltpu.SemaphoreType.DMA((n,))], phases=phases)


def _add_sibling(name, arrs, theirs):
    n, steps = len(arrs), REDUCE_STEPS

    def body(*refs):
        for mine_ref, theirs_ref, out_ref in zip(refs[:n], refs[n:2 * n], refs[2 * n:]):
            out_ref[...] = (mine_ref[...] + theirs_ref[...]).astype(out_ref.dtype)

    block = lambda t: (4, t.shape[1] // steps, t.shape[2])
    return pl.pallas_call(
        body, name=name, grid=(steps,),
        in_specs=[pl.BlockSpec(block(t), lambda i: (0, lax.axis_index("c") * steps + i, 0)) for t in theirs]
        + [pl.BlockSpec(block(t), lambda i: (0, i, 0)) for t in theirs],
        out_specs=[pl.BlockSpec(block(t), lambda i: (0, i, 0)) for t in theirs],
        out_shape=[jax.ShapeDtypeStruct(t.shape, WIRE_DTYPE) for t in theirs],
        compiler_params=_params(("parallel",)))(*arrs, *theirs)


def _scatter_exchange(arrs):
    n = len(arrs)

    def phases(ins, outs, sems):
        send_sems, recv_sems = sems
        c = lax.axis_index("c")
        peers = _relative_peers()

        def copy(k, r):
            px, py = peers[r]
            return pltpu.make_async_remote_copy(src_ref=ins[k].at[2 * px + py], dst_ref=outs[k].at[r - 1],
                                                send_sem=send_sems.at[3 * k + r - 1], recv_sem=recv_sems.at[3 * k + r - 1],
                                                device_id=(px, py, c), device_id_type=MESH)

        def start():
            for k in range(n):
                for r in peers:
                    copy(k, r).start()

        def finish():
            for k in range(n):
                for r in peers:
                    copy(k, r).wait()

        return start, lambda: None, finish

    return _Exchange(operands=list(arrs), out_shape=[jax.ShapeDtypeStruct((3,) + a.shape[1:], a.dtype) for a in arrs],
                     scratch_shapes=[pltpu.SemaphoreType.DMA((3 * n,)), pltpu.SemaphoreType.DMA((3 * n,))], phases=phases)


def _sum_chips(name, chip_sums, arrived):
    n, steps = len(chip_sums), REDUCE_STEPS

    def body(*refs):
        for own_ref, arrived_ref, out_ref in zip(refs[:n], refs[n:2 * n], refs[2 * n:]):
            acc = own_ref[...].astype(F32)
            for r in range(3):
                acc = acc + arrived_ref[r].astype(F32)
            out_ref[...] = acc

    rows = lambda s: s.shape[1] // steps
    chip = lambda: 2 * lax.axis_index("x") + lax.axis_index("y")
    return pl.pallas_call(
        body, name=name, grid=(steps,),
        in_specs=[pl.BlockSpec((None, rows(s), s.shape[2]), lambda i: (chip(), i, 0)) for s in chip_sums]
        + [pl.BlockSpec((3, rows(s), s.shape[2]), lambda i: (0, i, 0)) for s in chip_sums],
        out_specs=[pl.BlockSpec((rows(s), s.shape[2]), lambda i: (lax.axis_index("c") * steps + i, 0)) for s in chip_sums],
        out_shape=[jax.ShapeDtypeStruct((2 * s.shape[1], s.shape[2]), F32) for s in chip_sums],
        compiler_params=_params(("parallel",)))(*chip_sums, *arrived)


def _join_siblings(name, bufs):
    n = len(bufs)

    def body(*refs):
        outs = refs[n:2 * n]
        send_sems, recv_sems = refs[2 * n:]
        x, y, c = lax.axis_index("x"), lax.axis_index("y"), lax.axis_index("c")

        def exchange(my_half):
            copies = []
            for k in range(n):
                h = outs[k].shape[0] // 2
                mine = outs[k].at[my_half * h:(my_half + 1) * h, :]
                theirs = outs[k].at[(1 - my_half) * h:(2 - my_half) * h, :]
                cp = pltpu.make_async_remote_copy(src_ref=mine, dst_ref=mine, send_sem=send_sems.at[k],
                                                  recv_sem=recv_sems.at[k], device_id=(x, y, 1 - c), device_id_type=MESH)
                cp.start()
                arrival = pltpu.make_async_remote_copy(src_ref=theirs, dst_ref=theirs, send_sem=send_sems.at[k],
                                                       recv_sem=recv_sems.at[k], device_id=(x, y, 1 - c), device_id_type=MESH)
                copies.append((cp, arrival))
            for cp, arrival in copies:
                arrival.wait_recv()
                cp.wait_send()

        for half in (0, 1):
            pl.when(c == half)(functools.partial(exchange, half))

    return pl.pallas_call(
        body, name=name, in_specs=[HBM] * n, out_specs=[HBM] * n,
        out_shape=[jax.ShapeDtypeStruct(b.shape, b.dtype) for b in bufs], input_output_aliases={k: k for k in range(n)},
        scratch_shapes=[pltpu.SemaphoreType.DMA((n,)), pltpu.SemaphoreType.DMA((n,))],
    )(*bufs)


def _gather_all_exchange(a):
    def phases(ins, outs, scratch):
        (a_ref,), (out_ref,) = ins, outs
        staging, send_sems, recv_sems, local_sem = scratch
        x, y, c = lax.axis_index("x"), lax.axis_index("y"), lax.axis_index("c")
        me = 4 * x + 2 * y + c
        flips = [(fx, fy, fc) for fx in (0, 1) for fy in (0, 1) for fc in (0, 1)][1:]
        peers = [(x ^ fx, y ^ fy, c ^ fc) for fx, fy, fc in flips]

        def copy(j):
            px, py, pc = peers[j]
            return pltpu.make_async_remote_copy(src_ref=a_ref, dst_ref=out_ref.at[me], send_sem=send_sems.at[j],
                                                recv_sem=recv_sems.at[j], device_id=(px, py, pc), device_id_type=MESH)

        def arrival(j):
            px, py, pc = peers[j]
            return pltpu.make_async_remote_copy(src_ref=a_ref, dst_ref=out_ref.at[4 * px + 2 * py + pc], send_sem=send_sems.at[j],
                                                recv_sem=recv_sems.at[j], device_id=(px, py, pc), device_id_type=MESH)

        own = pltpu.make_async_copy(staging, out_ref.at[me], local_sem)

        def start():
            load = pltpu.make_async_copy(a_ref, staging, local_sem)
            load.start()
            load.wait()
            own.start()
            for j in range(7):
                copy(j).start()

        def finish():
            for j in range(7):
                arrival(j).wait_recv()
            for j in range(7):
                copy(j).wait_send()
            own.wait()

        return start, lambda: None, finish

    return _Exchange(operands=[a], out_shape=[jax.ShapeDtypeStruct((8,) + a.shape, a.dtype)],
                     scratch_shapes=[pltpu.VMEM(a.shape, a.dtype), pltpu.SemaphoreType.DMA((7,)), pltpu.SemaphoreType.DMA((7,)),
                                     pltpu.SemaphoreType.DMA], phases=phases)


def _rowwise_call(name, fn, slots, out_shapes, steps, carried=None):
    n_in, n_out = [len(s) for s in slots], [len(o) for o in out_shapes]

    def spec(shape):
        if len(shape) == 3:
            return pl.BlockSpec((shape[0], shape[1] // steps, shape[2]), lambda i: (0, i, 0))
        return pl.BlockSpec((shape[0] // steps, shape[1]), lambda i: (i, 0))

    def body(*refs):
        ins, outs = refs[:sum(n_in)], refs[sum(n_in):]
        a = b = 0
        for k in range(len(slots)):
            for o_ref, val in zip(outs[b:b + n_out[k]], fn(*[r[...] for r in ins[a:a + n_in[k]]])):
                o_ref[...] = val
            a, b = a + n_in[k], b + n_out[k]

    flat_in = [arr for s in slots for arr in s]
    flat_out = [shp for o in out_shapes for shp in o]
    out, carried_out = _call_carrying(
        body, carried, name=name, grid=(steps,), in_specs=[spec(a.shape) for a in flat_in],
        out_specs=[spec(s) for s in flat_out], out_shape=[jax.ShapeDtypeStruct(s, F32) for s in flat_out],
        operands=flat_in, semantics=("parallel",))
    grouped, b = [], 0
    for k in range(len(slots)):
        grouped.append(out[b:b + n_out[k]])
        b += n_out[k]
    return grouped if carried is None else (grouped, carried_out)


def _sum_in_order(a):
    acc = a[0].astype(F32)
    for k in range(1, a.shape[0]):
        acc = acc + a[k].astype(F32)
    return (acc,)


def _adamw_math(w, g, m, v):
    m_new = ADAM_B1 * m + (1.0 - ADAM_B1) * g
    v_new = ADAM_B2 * v + (1.0 - ADAM_B2) * (g * g)
    m_hat = m_new / (1.0 - ADAM_B1 ** ADAM_STEP)
    v_hat = v_new / (1.0 - ADAM_B2 ** ADAM_STEP)
    return -ADAM_LR * (m_hat / (jnp.sqrt(v_hat) + ADAM_EPS) + ADAM_WD * w), m_new, v_new


W_IN_ADAMW_STEPS = 21
SMALL_PACK_COLS = 256
SMALL_PACK_ROWS = 2048


def _pack_small(parts):
    wide = [jnp.pad(p, ((0, 0), (0, 0), (0, SMALL_PACK_COLS - p.shape[2]))) for p in parts]
    rows = jnp.concatenate(wide, axis=1)
    return jnp.pad(rows, ((0, 0), (0, SMALL_PACK_ROWS - rows.shape[1]), (0, 0)))


def _unpack_small(packed, shapes):
    out, row = [], 0
    for a, b in shapes:
        out.append(packed[:, row:row + a, :b])
        row += a
    return out


def _pack(arrs, rows_per_layer, dtype):
    nl = arrs[0].shape[0]
    flat = jnp.concatenate([a.astype(dtype).reshape(nl, -1) for a in arrs], axis=1)
    flat = jnp.pad(flat, ((0, 0), (0, rows_per_layer * PACK_COLS - flat.shape[1])))
    return flat.reshape(nl * rows_per_layer, PACK_COLS)


def _unpack(packed, shapes, rows_per_layer):
    nl = shapes[0][0]
    flat = packed.reshape(packed.shape[:-2] + (nl, rows_per_layer * PACK_COLS))
    out, off = [], 0
    for shp in shapes:
        size = math.prod(shp[1:])
        out.append(flat[..., off:off + size].reshape(packed.shape[:-2] + tuple(shp)))
        off += size
    return out


def _rows_needed(shapes, multiple):
    per_layer = sum(math.prod(s[1:]) for s in shapes)
    rows = -(-per_layer // PACK_COLS)
    return -(-rows // multiple) * multiple


CONV_TILE = (8, 128)


def _layer_shard_exchange(w, l):
    conv = w['conv_w'][l].reshape(-1)
    conv = jnp.pad(conv, (0, math.prod(CONV_TILE) - conv.shape[0])).reshape(CONV_TILE)
    return _gather_exchange([w[n][l].astype(MXU_DTYPE) for n in MATMUL_SHARDED], [conv])


def _layer_full_weights(w, l, gathered):
    gathered = [_absolute_chip_order(g) for g in gathered]
    full = {n: w[n][l:l + 1] for n in WEIGHTS}
    for n, part in zip(MATMUL_SHARDED, gathered):
        if n in ROW_SHARDED:
            full[n] = part.reshape(1, 4 * part.shape[1], part.shape[2])
        else:
            full[n] = jnp.swapaxes(part, 0, 1).reshape(1, part.shape[1], 4 * part.shape[2])
    rows, cols = w['conv_w'].shape[1:]
    conv = gathered[-1].reshape(4, -1)[:, :rows * cols].reshape(4, rows, cols)
    full['conv_w'] = jnp.swapaxes(conv, 0, 1).reshape(1, rows, 4 * cols)
    return full


def _chip_major(n, g):
    nl = g.shape[0]
    if n in ROW_SHARDED:
        return jnp.swapaxes(g.reshape(nl, 4, g.shape[1] // 4, g.shape[2]), 0, 1)
    return jnp.transpose(g.reshape(nl, g.shape[1], 4, g.shape[2] // 4), (2, 0, 1, 3))


def kernel(x, positions, norm_mix_pre, w_in, gate_b, q_norm, w_uq, kv_norm, w_ukv, w_br_mla, sg_ln_g, sg_ln_b, sg_w, sg_b, w_br_sg, conv_w, w_br_conv, pool_w, pool_scale, w_br_pool, w_out, norm_mix_post, norm_ffn_pre, w_ff1, w_ff2, norm_ffn_post, loss_target, m_norm_mix_pre, m_w_in, m_gate_b, m_q_norm, m_w_uq, m_kv_norm, m_w_ukv, m_w_br_mla, m_sg_ln_g, m_sg_ln_b, m_sg_w, m_sg_b, m_w_br_sg, m_conv_w, m_w_br_conv, m_pool_w, m_pool_scale, m_w_br_pool, m_w_out, m_norm_mix_post, m_norm_ffn_pre, m_w_ff1, m_w_ff2, m_norm_ffn_post, v_norm_mix_pre, v_w_in, v_gate_b, v_q_norm, v_w_uq, v_kv_norm, v_w_ukv, v_w_br_mla, v_sg_ln_g, v_sg_ln_b, v_sg_w, v_sg_b, v_w_br_sg, v_conv_w, v_w_br_conv, v_pool_w, v_pool_scale, v_w_br_pool, v_w_out, v_norm_mix_post, v_norm_ffn_pre, v_w_ff1, v_w_ff2, v_norm_ffn_post):
    given = dict(locals())
    w = {n: given[n] for n in WEIGHTS}
    mom = {n: given['m_' + n] for n in WEIGHTS}
    var = {n: given['v_' + n] for n in WEIGHTS}
    nl = w['w_in'].shape[0]
    sharded, chip_sums, reduced, local_small = {}, {}, {}, {}

    def grads_ready(l, g):
        sharded[l] = [_chip_major(n, g[n])[:, 0] for n in BIG_SHARDED]
        sharded[l].append(_pack_small([_chip_major(n, g[n])[:, 0] for n in SMALL_SHARDED]))
        local_small[l] = [g[n] for n in REPLICATED + ['conv_w']]
        return _sibling_exchange(sharded[l])

    names = REPLICATED + ['conv_w']
    small = {}

    def sibling_done(l, theirs):
        chip_sums[l] = _add_sibling("add_sibling", sharded[l], theirs)
        scatter = _scatter_exchange(chip_sums[l])
        if l > 0:
            return scatter
        small['local'] = [jnp.concatenate([local_small[k][j] for k in range(nl)]) for j in range(len(names))]
        small['rows'] = _rows_needed([a.shape for a in small['local']], 32)
        return _merge_exchanges(scatter, _gather_all_exchange(_pack(small['local'], small['rows'], F32)))

    def chips_done(l, arrived):
        if l == 0:
            arrived, small['everyone'] = arrived[:-1], arrived[-1]
        reduced[l] = _join_siblings("join_halves", _sum_chips("sum_chips", chip_sums[l], arrived))

    plan = _StepPlan(n_layers=nl, weights_exchange=functools.partial(_layer_shard_exchange, w),
                     weights_from=functools.partial(_layer_full_weights, w), grads_ready=grads_ready,
                     first_done=sibling_done, second_done=chips_done)
    loss, dx = _local_step(x[0], positions[0], loss_target[0], plan)
    loss = lax.psum(loss, ("x", "y", "c"))

    grad, delta, new_m, new_v = {}, {}, {}, {}
    local, rows, everyone = small['local'], small['rows'], small['everyone']
    for k, n in enumerate(BIG_SHARDED):
        grad[n] = jnp.stack([reduced[l][k] for l in range(nl)])
    turned = [jnp.swapaxes(a, 1, 2) for a in (w['w_in'], grad['w_in'], mom['w_in'], var['w_in'])]
    (w_in_out,) = _rowwise_call("adamw_w_in", _adamw_math, [turned], [[turned[0].shape] * 3], W_IN_ADAMW_STEPS)
    delta['w_in'], new_m['w_in'], new_v['w_in'] = [jnp.swapaxes(a, 1, 2) for a in w_in_out]
    others = [n for n in BIG_SHARDED if n != 'w_in']
    slots = [[w[n], grad[n], mom[n], var[n]] for n in others]
    small_pack = lambda d: _pack_small([d[n] for n in SMALL_SHARDED])
    g_small = jnp.stack([reduced[l][-1] for l in range(nl)])
    slots.append([small_pack(w), g_small, small_pack(mom), small_pack(var)])
    updated = _rowwise_call("adamw_sharded", _adamw_math, slots, [[s_[0].shape] * 3 for s_ in slots], 32)
    for k, n in enumerate(others):
        delta[n], new_m[n], new_v[n] = updated[k]
    for d, packed in zip((grad, delta, new_m, new_v), [g_small] + list(updated[-1])):
        d.update(zip(SMALL_SHARDED, _unpack_small(packed, [w[n].shape[1:] for n in SMALL_SHARDED])))

    (summed,), = _rowwise_call("sum_devices", _sum_in_order, [[everyone]], [[everyone.shape[1:]]], 4)
    g_rep = _unpack(summed, [a.shape for a in local], rows)
    chip = 2 * lax.axis_index("x") + lax.axis_index("y")
    g_rep[-1] = lax.dynamic_slice_in_dim(g_rep[-1], chip * w['conv_w'].shape[2], w['conv_w'].shape[2], axis=2)
    rep_pack = lambda arrs: _pack(arrs, rows, F32)
    (rep_out,) = _rowwise_call("adamw_replicated", _adamw_math,
                               [[rep_pack([w[n] for n in names]), rep_pack(g_rep), rep_pack([mom[n] for n in names]),
                                 rep_pack([var[n] for n in names])]], [[(nl * rows, PACK_COLS)] * 3], 4)
    grad.update(zip(names, g_rep))
    for d, packed in zip((delta, new_m, new_v), rep_out):
        d.update(zip(names, _unpack(packed, [w[n].shape for n in names], rows)))

    return (loss, dx[None], *[grad[n] for n in WEIGHTS], *[delta[n] for n in WEIGHTS], *[new_m[n] for n in WEIGHTS],
            *[new_v[n] for n in WEIGHTS])
```

```python
import functools
import math
from typing import Any, Callable, NamedTuple, Sequence

import jax
import jax.numpy as jnp
from jax import lax
from jax.experimental import pallas as pl
from jax.experimental.pallas import tpu as pltpu

F32 = jnp.float32
MXU_DTYPE = jnp.bfloat16
WIRE_DTYPE = jnp.bfloat16
ACT_DTYPE = jnp.bfloat16
MESH = pl.DeviceIdType.MESH

D_MODEL = 1024
D_FF = 4096
N_HEADS = 4
QK_NOPE = 64
QK_ROPE = 32
V_HEAD = 64
HEAD_PAD = 128
Q_RANK = 256
KV_RANK = 128
SG_CHUNK = 128
SG_GROUPS = 4
BR_WIDTH = 256
N_BRANCH = 4
HALO = 16
ROPE_BASE = 10000.0
EPS = 1e-6
ATTN_SCALE = (QK_NOPE + QK_ROPE) ** -0.5
LOG2_E = math.log2(math.e)
FWD_HEADS_PER_STEP = 4
HEADS_PER_STEP = 2
N_PROJ = N_BRANCH * D_MODEL + 6 * BR_WIDTH + Q_RANK + KV_RANK + 2 * HEAD_PAD
COL_G, COL_M1, COL_M2, COL_B = 0, 4096, 4864, 5632

ADAM_LR, ADAM_B1, ADAM_B2, ADAM_EPS, ADAM_WD, ADAM_STEP = 0.001, 0.9, 0.999, 1e-08, 0.01, 10

VMEM_LIMIT = 56 * 1024 * 1024

WEIGHTS = ['norm_mix_pre', 'w_in', 'gate_b', 'q_norm', 'w_uq', 'kv_norm', 'w_ukv', 'w_br_mla', 'sg_ln_g', 'sg_ln_b',
           'sg_w', 'sg_b', 'w_br_sg', 'conv_w', 'w_br_conv', 'pool_w', 'pool_scale', 'w_br_pool', 'w_out',
           'norm_mix_post', 'norm_ffn_pre', 'w_ff1', 'w_ff2', 'norm_ffn_post']
ROW_SHARDED = ['w_out', 'w_ff2']
MATMUL_SHARDED = ['w_in', 'w_uq', 'w_ukv', 'w_br_mla', 'w_br_sg', 'w_br_conv', 'w_br_pool', 'w_out', 'w_ff1', 'w_ff2']
BIG_SHARDED = ['w_in', 'w_ff1', 'w_ff2', 'w_out']
SMALL_SHARDED = ['w_uq', 'w_ukv', 'w_br_mla', 'w_br_sg', 'w_br_conv', 'w_br_pool']
SHARDED = MATMUL_SHARDED + ['conv_w']
REPLICATED = [n for n in WEIGHTS if n not in SHARDED]
PACK_COLS = 1024


def _params(sem, vmem=VMEM_LIMIT):
    return pltpu.CompilerParams(dimension_semantics=sem, vmem_limit_bytes=vmem)


def _mxu(a):
    return a.astype(MXU_DTYPE)


def _dot(a, b):
    return jnp.dot(_mxu(a), _mxu(b), preferred_element_type=F32)


def _dot_nt(a, b):
    return lax.dot_general(_mxu(a), _mxu(b), (((1,), (1,)), ((), ())), preferred_element_type=F32)


def _dot_tn(a, b):
    return lax.dot_general(_mxu(a), _mxu(b), (((0,), (0,)), ((), ())), preferred_element_type=F32)


def _rms(x, g):
    r = lax.rsqrt(jnp.mean(x * x, axis=-1, keepdims=True) + EPS)
    return x * r * g


def _rms_bwd(x, g, dy):
    r = lax.rsqrt(jnp.mean(x * x, axis=-1, keepdims=True) + EPS)
    xh = x * r
    gdy = dy * g
    dx = r * (gdy - xh * jnp.mean(gdy * xh, axis=-1, keepdims=True))
    return dx, jnp.sum(dy * xh, axis=0, keepdims=True)


_GELU_C = math.sqrt(2.0 / math.pi)


def _gelu(x):
    t = jnp.tanh(_GELU_C * (x + 0.044715 * (x * x * x)))
    return x * (0.5 * (1.0 + t)), t


def _gelu_grad(x, t):
    return 0.5 * (1.0 + t) + 0.5 * x * (1.0 - t * t) * (_GELU_C * (1.0 + 3.0 * 0.044715 * x * x))


def _sigmoid(x):
    return 1.0 / (1.0 + jnp.exp(-x))


def _full(shape):
    return pl.BlockSpec(shape, lambda *_: (0,) * len(shape))


def _resident(shape):
    return pl.BlockSpec(shape, lambda *_: (0,) * len(shape), pipeline_mode=pl.Buffered(1))


def _rows(ts, width, col=0):
    return pl.BlockSpec((ts, width), lambda i: (i, col))


def _tile(n, pref):
    return min(n, pref)


def _mm(name, a, w, *, tm, tn, prologue=None, rows=()):
    m, k = a.shape
    n = w.shape[1]
    tm, tn = _tile(m, tm), _tile(n, tn)

    def body(a_ref, *rest):
        row_refs, w_ref, o_ref = rest[:len(rows)], rest[len(rows)], rest[len(rows) + 1]
        av = a_ref[...]
        if prologue is not None:
            av = prologue(av, *[r[...] for r in row_refs])
        o_ref[...] = _dot(av, w_ref[...]).astype(o_ref.dtype)

    return pl.pallas_call(
        body, name=name, grid=(m // tm, n // tn),
        in_specs=[pl.BlockSpec((tm, k), lambda i, j: (i, 0))] + [pl.BlockSpec((1, k), lambda i, j: (0, 0)) for _ in rows]
        + [_resident((k, n)) if tn == n else pl.BlockSpec((k, tn), lambda i, j: (0, j))],
        out_specs=pl.BlockSpec((tm, tn), lambda i, j: (i, j)),
        out_shape=jax.ShapeDtypeStruct((m, n), ACT_DTYPE),
        compiler_params=_params(("parallel", "parallel")),
    )(a, *rows, w)


def _mm_tn(name, a, b, *, tm, tn, prologue=None, rows=()):
    m, k = a.shape
    n = b.shape[1]
    tm, tn = _tile(m, tm), _tile(n, tn)

    def body(a_ref, *rest):
        row_refs, b_ref, o_ref = rest[:len(rows)], rest[len(rows)], rest[len(rows) + 1]

        @pl.when(pl.program_id(1) == 0)
        def _():
            o_ref[...] = jnp.zeros_like(o_ref)

        av = a_ref[...]
        if prologue is not None:
            av = prologue(av, *[r[...] for r in row_refs])
        o_ref[...] += _dot_tn(av, b_ref[...])

    return pl.pallas_call(
        body, name=name, grid=(n // tn, m // tm),
        in_specs=[pl.BlockSpec((tm, k), lambda j, i: (i, 0))] + [pl.BlockSpec((1, k), lambda j, i: (0, 0)) for _ in rows]
        + [pl.BlockSpec((tm, tn), lambda j, i: (i, j))],
        out_specs=pl.BlockSpec((k, tn), lambda j, i: (0, j)),
        out_shape=jax.ShapeDtypeStruct((k, n), F32),
        compiler_params=_params(("parallel", "arbitrary")),
    )(a, *rows, b)


def _relu_sq(a):
    r = jnp.maximum(a.astype(F32), 0.0)
    return r * r


HBM = pl.BlockSpec(memory_space=pl.ANY)


class _Exchange(NamedTuple):
    operands: Sequence[Any]
    out_shape: Sequence[Any]
    scratch_shapes: Sequence[Any]
    phases: Callable


def _run_exchange(name, ex):
    n_in, n_out = len(ex.operands), len(ex.out_shape)

    def body(*refs):
        for phase in ex.phases(refs[:n_in], refs[n_in:n_in + n_out], refs[n_in + n_out:]):
            phase()

    return pl.pallas_call(body, name=name, in_specs=[HBM] * n_in, out_specs=[HBM] * n_out, out_shape=list(ex.out_shape),
                          scratch_shapes=list(ex.scratch_shapes))(*ex.operands)


def _call_carrying(body, carried, *, name, grid, in_specs, out_specs, out_shape, operands, semantics, middle_step=None):
    if carried is None:
        return pl.pallas_call(body, name=name, grid=grid, in_specs=in_specs, out_specs=out_specs, out_shape=out_shape,
                              compiler_params=_params(semantics))(*operands), None
    n_main_in, n_main_out = len(operands), len(out_shape)
    n_in, n_out = len(carried.operands), len(carried.out_shape)
    steps = math.prod(grid)

    def wrapped(*refs):
        main_in, refs = refs[:n_main_in], refs[n_main_in:]
        ex_in, refs = refs[:n_in], refs[n_in:]
        main_out, refs = refs[:n_main_out], refs[n_main_out:]
        ex_out, sems = refs[:n_out], refs[n_out:]
        step = pl.program_id(0)
        for axis in range(1, len(grid)):
            step = step * grid[axis] + pl.program_id(axis)
        start, middle, finish = carried.phases(ex_in, ex_out, sems)
        pl.when(step == 0)(start)
        pl.when(step == ((steps - 1) // 2 if middle_step is None else middle_step))(middle)
        body(*main_in, *main_out)
        pl.when(step == steps - 1)(finish)

    out = pl.pallas_call(
        wrapped, name=name + "_carrying", grid=grid, in_specs=list(in_specs) + [HBM] * n_in,
        out_specs=list(out_specs) + [HBM] * n_out, out_shape=list(out_shape) + list(carried.out_shape),
        scratch_shapes=list(carried.scratch_shapes), compiler_params=_params(("arbitrary",) * len(grid)))(*operands, *carried.operands)
    return out[:n_main_out], out[n_main_out:]


def _qkv_prep(proj, q_norm, kv_norm, wq, wkv, cq_tab, s_tab, cr_tab):
    s = proj.shape[0]
    ts = _tile(s, 512)
    hw = N_HEADS * HEAD_PAD

    def body(cq_ref, ckv_ref, kr_ref, krs_ref, gq_ref, gkv_ref, wq_ref, wkv_ref, ct_ref, st_ref, crt_ref,
             q_ref, k_ref, v_ref):
        ct, st, crt = ct_ref[...], st_ref[...], crt_ref[...]
        qn = _rms(cq_ref[...].astype(F32), gq_ref[...])
        qab = _dot(qn, wq_ref[...])
        kvn = _rms(ckv_ref[...].astype(F32), gkv_ref[...])
        kav = _dot(kvn, wkv_ref[...])
        k_rope = kr_ref[...].astype(F32) * crt + krs_ref[...].astype(F32) * st
        ones_lane = (lax.broadcasted_iota(jnp.int32, (1, HEAD_PAD), 1) == V_HEAD).astype(F32)
        for h in range(N_HEADS):
            lo = h * HEAD_PAD
            q_ref[h] = (qab[:, lo:lo + HEAD_PAD] * ct + qab[:, hw + lo:hw + lo + HEAD_PAD] * st).astype(q_ref.dtype)
            k_ref[h] = (kav[:, lo:lo + HEAD_PAD] + k_rope).astype(k_ref.dtype)
            v_ref[h] = (kav[:, hw + lo:hw + lo + HEAD_PAD] + ones_lane).astype(v_ref.dtype)

    head_spec = pl.BlockSpec((N_HEADS, ts, HEAD_PAD), lambda i: (0, i, 0))
    head_shape = jax.ShapeDtypeStruct((N_HEADS, s, HEAD_PAD), MXU_DTYPE)
    return pl.pallas_call(
        body, name="qkv_prep", grid=(s // ts,),
        in_specs=[_rows(ts, Q_RANK, COL_B // Q_RANK), _rows(ts, KV_RANK, (COL_B + Q_RANK) // KV_RANK),
                  _rows(ts, HEAD_PAD, (COL_B + Q_RANK + KV_RANK) // HEAD_PAD),
                  _rows(ts, HEAD_PAD, (COL_B + Q_RANK + KV_RANK + HEAD_PAD) // HEAD_PAD),
                  _full((1, Q_RANK)), _full((1, KV_RANK)), _full((Q_RANK, 2 * hw)), _full((KV_RANK, 2 * hw)),
                  _rows(ts, HEAD_PAD), _rows(ts, HEAD_PAD), _rows(ts, HEAD_PAD)],
        out_specs=[head_spec, head_spec, head_spec],
        out_shape=[head_shape, head_shape, head_shape],
        compiler_params=_params(("parallel",)),
    )(proj, proj, proj, proj, q_norm, kv_norm, wq, wkv, cq_tab, s_tab, cr_tab)


def _diagonal_mask(t, keys_on_rows=False):
    key_axis = 0 if keys_on_rows else 1
    return lax.broadcasted_iota(jnp.int32, (t, t), key_axis) <= lax.broadcasted_iota(jnp.int32, (t, t), 1 - key_axis)


MIDDLE_WORK_SHARE = 0.65


def _causal_work_step(nq, groups):
    total = groups * nq * (nq + 1) // 2
    done = 0
    for step in range(groups * nq):
        if done >= MIDDLE_WORK_SHARE * total:
            return step
        done += step % nq + 1
    return groups * nq - 1


def _attn_fwd(q, k, v, carried=None):
    s = q.shape[1]
    t = _tile(s, 1024)

    def body(q_ref, k_ref, v_ref, o_ref, lse_ref):
        i = pl.program_id(1)
        lane = lax.broadcasted_iota(jnp.int32, (1, HEAD_PAD), 1)

        def step(j, carry, on_diagonal):
            rows = pl.ds(pl.multiple_of(j * t, t), t)
            out = []
            for h in range(FWD_HEADS_PER_STEP):
                m, acc = carry[h]
                sc = _dot_nt(q_ref[h], k_ref[h, rows, :]) * (ATTN_SCALE * LOG2_E)
                if on_diagonal:
                    sc = jnp.where(_diagonal_mask(t), sc, -jnp.inf)
                m_new = jnp.maximum(m, jnp.max(sc, axis=1, keepdims=True))
                out.append((m_new, jnp.exp2(m - m_new) * acc + _dot(jnp.exp2(sc - m_new), v_ref[h, rows, :])))
            return tuple(out)

        init = ((jnp.full((t, 1), -jnp.inf, F32), jnp.zeros((t, HEAD_PAD), F32)),) * FWD_HEADS_PER_STEP
        below = lax.fori_loop(0, i, functools.partial(step, on_diagonal=False), init)
        for h, (m, acc) in enumerate(step(i, below, True)):
            l = jnp.sum(jnp.where(lane == V_HEAD, acc, 0.0), axis=1, keepdims=True)
            o_ref[:, h * HEAD_PAD:(h + 1) * HEAD_PAD] = jnp.where(lane < V_HEAD, acc / l, 0.0)
            lse_ref[h] = m + jnp.log2(l)

    group = FWD_HEADS_PER_STEP
    return _call_carrying(
        body, carried, name="attn_fwd", grid=(N_HEADS // group, s // t),
        in_specs=[pl.BlockSpec((group, t, HEAD_PAD), lambda h, i: (h, i, 0)),
                  pl.BlockSpec((group, s, HEAD_PAD), lambda h, i: (h, 0, 0)),
                  pl.BlockSpec((group, s, HEAD_PAD), lambda h, i: (h, 0, 0))],
        out_specs=[pl.BlockSpec((t, group * HEAD_PAD), lambda h, i: (i, h)), pl.BlockSpec((group, t, 1), lambda h, i: (h, i, 0))],
        out_shape=[jax.ShapeDtypeStruct((s, N_HEADS * HEAD_PAD), F32), jax.ShapeDtypeStruct((N_HEADS, s, 1), F32)],
        operands=(q, k, v), semantics=("parallel", "parallel"), middle_step=_causal_work_step(s // t, N_HEADS // group))


def _attn_bwd(q, k, v, do, lse, delta, carried=None):
    s = q.shape[1]
    t = _tile(s, 1024)
    nq = s // t

    def body(q_ref, do_ref, lse_ref, dl_ref, k_ref, v_ref, dq_ref, dk_ref, dv_ref):
        j = pl.program_id(1)

        @pl.when(j == 0)
        def _():
            dq_ref[...] = jnp.zeros_like(dq_ref)

        def step(i, carry, on_diagonal):
            rows = pl.ds(pl.multiple_of(i * t, t), t)
            out = []
            for h in range(HEADS_PER_STEP):
                dk, dv = carry[h]
                qi, doi = q_ref[h, rows, :], do_ref[rows, h * HEAD_PAD:(h + 1) * HEAD_PAD]
                sc = _dot_nt(k_ref[h], qi) * (ATTN_SCALE * LOG2_E)
                if on_diagonal:
                    sc = jnp.where(_diagonal_mask(t, keys_on_rows=True), sc, -jnp.inf)
                p = jnp.exp2(sc - lse_ref[h, i])
                dv = dv + _dot(p, doi)
                ds = p * (_dot_nt(v_ref[h], doi) - dl_ref[h, i])
                dk = dk + _dot(ds, qi)
                dq_ref[h, rows, :] += _dot_tn(ds, k_ref[h]) * ATTN_SCALE
                out.append((dk, dv))
            return tuple(out)

        zero = ((jnp.zeros((t, HEAD_PAD), F32),) * 2,) * HEADS_PER_STEP
        sums = lax.fori_loop(j + 1, nq, functools.partial(step, on_diagonal=False), step(j, zero, True))
        for h, (dk, dv) in enumerate(sums):
            dk_ref[h] = dk * ATTN_SCALE
            dv_ref[h] = dv

    group = HEADS_PER_STEP
    whole = lambda w: pl.BlockSpec((group, s, w), lambda h, j: (h, 0, 0), pipeline_mode=pl.Buffered(1))
    tile = pl.BlockSpec((group, t, HEAD_PAD), lambda h, j: (h, j, 0))
    per_query = pl.BlockSpec((group, nq, 1, t), lambda h, j: (h, 0, 0, 0))
    head_shape = jax.ShapeDtypeStruct((N_HEADS, s, HEAD_PAD), F32)
    return _call_carrying(
        body, carried, name="attn_bwd", grid=(N_HEADS // group, nq),
        in_specs=[whole(HEAD_PAD), pl.BlockSpec((s, group * HEAD_PAD), lambda h, j: (0, h), pipeline_mode=pl.Buffered(1)),
                  per_query, per_query, tile, tile],
        out_specs=[whole(HEAD_PAD), tile, tile],
        out_shape=[head_shape, head_shape, head_shape],
        operands=(q, do, lse.reshape(N_HEADS, nq, 1, t), delta.reshape(N_HEADS, nq, 1, t), k, v),
        semantics=("parallel", "arbitrary"))


def _qkv_bwd(dq, dk, dv, proj, q_norm, kv_norm, wq, wkv, cq_tab, s_tab, cr_tab):
    s = proj.shape[0]
    ts = _tile(s, 512)
    hw = N_HEADS * HEAD_PAD

    def body(dq_ref, dk_ref, dv_ref, cq_ref, ckv_ref, gq_ref, gkv_ref, wq_ref, wkv_ref, ct_ref, st_ref, crt_ref,
             dpb_ref, dwq_ref, dwkv_ref, dgq_ref, dgkv_ref):
        @pl.when(pl.program_id(0) == 0)
        def _():
            for r in (dwq_ref, dwkv_ref, dgq_ref, dgkv_ref):
                r[...] = jnp.zeros_like(r)

        ct, st, crt = ct_ref[...], st_ref[...], crt_ref[...]
        dqs = [dq_ref[h] for h in range(N_HEADS)]
        dks = [dk_ref[h] for h in range(N_HEADS)]
        dqab = jnp.concatenate([d * ct for d in dqs] + [d * st for d in dqs], axis=1)
        dkav = jnp.concatenate(dks + [dv_ref[h] for h in range(N_HEADS)], axis=1)
        dk_sum = dks[0] + dks[1] + dks[2] + dks[3]
        cq, ckv, gq, gkv = cq_ref[...].astype(F32), ckv_ref[...].astype(F32), gq_ref[...], gkv_ref[...]
        dwq_ref[...] += _dot_tn(_rms(cq, gq), dqab)
        dwkv_ref[...] += _dot_tn(_rms(ckv, gkv), dkav)
        dcq, dgq = _rms_bwd(cq, gq, _dot_nt(dqab, wq_ref[...]))
        dckv, dgkv = _rms_bwd(ckv, gkv, _dot_nt(dkav, wkv_ref[...]))
        dgq_ref[...] += dgq
        dgkv_ref[...] += dgkv
        dpb_ref[...] = jnp.concatenate([dcq, dckv, dk_sum * crt, dk_sum * st], axis=1).astype(dpb_ref.dtype)

    head_spec = pl.BlockSpec((N_HEADS, ts, HEAD_PAD), lambda i: (0, i, 0))
    wb = Q_RANK + KV_RANK + 2 * HEAD_PAD
    return pl.pallas_call(
        body, name="qkv_bwd", grid=(s // ts,),
        in_specs=[head_spec, head_spec, head_spec,
                  _rows(ts, Q_RANK, COL_B // Q_RANK), _rows(ts, KV_RANK, (COL_B + Q_RANK) // KV_RANK),
                  _full((1, Q_RANK)), _full((1, KV_RANK)), _full((Q_RANK, 2 * hw)), _full((KV_RANK, 2 * hw)),
                  _rows(ts, HEAD_PAD), _rows(ts, HEAD_PAD), _rows(ts, HEAD_PAD)],
        out_specs=[_rows(ts, wb), _full((Q_RANK, 2 * hw)), _full((KV_RANK, 2 * hw)), _full((1, Q_RANK)), _full((1, KV_RANK))],
        out_shape=[jax.ShapeDtypeStruct((s, wb), MXU_DTYPE), jax.ShapeDtypeStruct((Q_RANK, 2 * hw), F32),
                   jax.ShapeDtypeStruct((KV_RANK, 2 * hw), F32), jax.ShapeDtypeStruct((1, Q_RANK), F32),
                   jax.ShapeDtypeStruct((1, KV_RANK), F32)],
        compiler_params=_params(("arbitrary",)),
    )(dq, dk, dv, proj, proj, q_norm, kv_norm, wq, wkv, cq_tab, s_tab, cr_tab)


def _lane_group(width):
    return lax.broadcasted_iota(jnp.int32, (1, width), 1) // (width // 4)


def _shift_down(a, k):
    return pltpu.roll(a, k, 0)


def _shift_up(a, k):
    return pltpu.roll(a, a.shape[0] - k, 0)


def _window_sums(xh, shift):
    s2 = xh + shift(xh, 1)
    s4 = s2 + shift(s2, 2)
    s8 = s4 + shift(s4, 4)
    s16 = s8 + shift(s8, 8)
    grp = _lane_group(xh.shape[1])
    return jnp.where(grp == 0, s2, jnp.where(grp == 1, s4, jnp.where(grp == 2, s8, s16)))


def _pool_count(i, ts):
    grp = _lane_group(BR_WIDTH)
    win = jnp.where(grp == 0, 2.0, jnp.where(grp == 1, 4.0, jnp.where(grp == 2, 8.0, 16.0)))
    t = (i * ts + lax.broadcasted_iota(jnp.int32, (ts, 1), 0)).astype(F32)
    return jnp.minimum(t + 1.0, win)


def _mix_forward(i, ts, r):
    f = {}
    act = lambda name: r[name][...].astype(F32)
    f['gates'] = _sigmoid(act('gate') + r['gate_b'][...])
    sgu, sgv = act('sgu'), act('sgv')
    f['sgu'], f['sgv'] = sgu, sgv
    u_act, f['tu'] = _gelu(sgu)
    vg, f['tv'] = _gelu(sgv)
    mu = jnp.mean(vg, axis=-1, keepdims=True)
    xc = vg - mu
    f['ln_r'] = lax.rsqrt(jnp.mean(xc * xc, axis=-1, keepdims=True) + EPS)
    f['ln_xh'] = xc * f['ln_r']
    vln = f['ln_xh'] * r['ln_g'][...] + r['ln_b'][...]
    tril = lax.broadcasted_iota(jnp.int32, (SG_CHUNK, SG_CHUNK), 1) <= lax.broadcasted_iota(jnp.int32, (SG_CHUNK, SG_CHUNK), 0)
    f['wm'] = [_mxu(jnp.where(tril, r['sg_w'][g], 0.0)) for g in range(SG_GROUPS)]
    f['tril'] = tril
    grp = _lane_group(BR_WIDTH)
    bias = r['sg_bias'][...]
    parts = []
    for ci in range(ts // SG_CHUNK):
        vc = vln[ci * SG_CHUNK:(ci + 1) * SG_CHUNK]
        sc = bias
        for g in range(SG_GROUPS):
            sc = sc + jnp.where(grp == g, _dot(f['wm'][g], vc), 0.0)
        parts.append(sc)
    f['vln'] = vln
    f['sg_s'] = parts[0] if len(parts) == 1 else jnp.concatenate(parts, axis=0)
    f['u_act'] = u_act
    out_b = u_act * f['sg_s']
    first = (i > 0).astype(F32)
    cvx, cvc, cvb = act('cvx'), act('cvc'), act('cvb')
    f['cvx'], f['cvc'], f['cvb'] = cvx, cvc, cvb
    zh = jnp.concatenate([act('hx') * act('hc') * first, cvc * cvx], axis=0)
    f['z1'] = _shift_down(zh, 1)[HALO:]
    f['z2'] = _shift_down(zh, 2)[HALO:]
    f['z0'] = zh[HALO:]
    f['yv'] = r['conv_w'][0:1, :] * f['z2'] + r['conv_w'][1:2, :] * f['z1'] + r['conv_w'][2:3, :] * f['z0']
    out_c = cvb * f['yv']
    p = act('pool')
    ph = jnp.concatenate([act('hp') * first, p], axis=0)
    f['cnt'] = _pool_count(i, ts)
    f['pooled'] = _window_sums(ph, _shift_down)[HALO:] / f['cnt'] - p
    f['mixed'] = _dot(f['pooled'], r['wbd'][...])
    out_d = f['mixed'] * r['pool_scale'][...]
    f['outs'] = [r['o'][...], out_b, out_c, out_d]
    f['ys'] = [_dot(f['outs'][b], r['w_br'][b][...]) for b in range(N_BRANCH)]
    merged = f['gates'][:, 0:D_MODEL] * f['ys'][0]
    for b in range(1, N_BRANCH):
        merged = merged + f['gates'][:, b * D_MODEL:(b + 1) * D_MODEL] * f['ys'][b]
    f['merged'] = merged
    f['mo'] = _dot(merged, r['w_out'][...])
    return f


_MIX_TILE_INPUTS = ['gate', 'sgu', 'sgv', 'cvb', 'cvx', 'cvc', 'pool', 'hx', 'hc', 'hp', 'o']
_MIX_WEIGHTS = ['gate_b', 'ln_g', 'ln_b', 'sg_w', 'sg_bias', 'conv_w', 'wbd', 'pool_scale', 'w_br0', 'w_br1', 'w_br2',
                'w_br3', 'w_out', 'g_post']


def _mix_specs(s, ts):
    c0 = COL_M1 // BR_WIDTH
    prev = lambda col: pl.BlockSpec((HALO, BR_WIDTH), lambda i: (jnp.maximum(i * (ts // HALO) - 1, 0), col))
    tiles = [_rows(ts, N_BRANCH * D_MODEL, 0), _rows(ts, BR_WIDTH, c0), _rows(ts, BR_WIDTH, c0 + 1), _rows(ts, BR_WIDTH, c0 + 2),
             _rows(ts, BR_WIDTH, c0 + 3), _rows(ts, BR_WIDTH, c0 + 4), _rows(ts, BR_WIDTH, c0 + 5),
             prev(c0 + 3), prev(c0 + 4), prev(c0 + 5), _rows(ts, N_HEADS * HEAD_PAD)]
    weights = [_full((1, N_BRANCH * D_MODEL)), _full((1, BR_WIDTH)), _full((1, BR_WIDTH)),
               _full((SG_GROUPS, SG_CHUNK, SG_CHUNK)), _full((SG_CHUNK, BR_WIDTH)), _full((8, BR_WIDTH)),
               _resident((BR_WIDTH, BR_WIDTH)), _full((1, BR_WIDTH)), _resident((N_HEADS * HEAD_PAD, D_MODEL)),
               _resident((BR_WIDTH, D_MODEL)), _resident((BR_WIDTH, D_MODEL)), _resident((BR_WIDTH, D_MODEL)),
               _resident((D_MODEL, D_MODEL)), _full((1, D_MODEL))]
    return tiles, weights


def _mix_refs(refs):
    names = _MIX_TILE_INPUTS + _MIX_WEIGHTS
    r = dict(zip(names, refs[:len(names)]))
    r['w_br'] = [r['w_br0'], r['w_br1'], r['w_br2'], r['w_br3']]
    return r, refs[len(names):]


def _mix_operands(proj, o, lw):
    return ([proj] * 10 + [o] + [lw[n] for n in ['gate_b', 'sg_ln_g', 'sg_ln_b', 'sg_w', 'sg_bias', 'conv_w8', 'wbd',
                                                 'pool_scale', 'w_br_mla_p', 'w_br_sg', 'w_br_conv', 'w_br_pool', 'w_out',
                                                 'norm_mix_post']])


def _mix_fwd(x0, proj, o, lw):
    s = x0.shape[0]
    ts = _tile(s, 512)
    tiles, weights = _mix_specs(s, ts)

    def body(*refs):
        r, (x0_ref, x1_ref) = _mix_refs(refs)
        f = _mix_forward(pl.program_id(0), ts, r)
        x1_ref[...] = x0_ref[...] + _rms(f['mo'], r['g_post'][...])

    return pl.pallas_call(
        body, name="mix_fwd", grid=(s // ts,),
        in_specs=tiles + weights + [_rows(ts, D_MODEL)],
        out_specs=_rows(ts, D_MODEL),
        out_shape=jax.ShapeDtypeStruct((s, D_MODEL), F32),
        compiler_params=_params(("parallel",)),
    )(*_mix_operands(proj, o, lw), x0)


def _mix_bwd(dx1, proj, o, lw):
    s = dx1.shape[0]
    ts = _tile(s, 256)
    tiles, weights = _mix_specs(s, ts)
    hw = N_HEADS * HEAD_PAD

    def body(*refs):
        r, rest = _mix_refs(refs)
        (dx1_ref, dg_ref, dm1_ref, dyv_ref, up_ref, do_ref, delta_ref,
         dgate_b_ref, dln_g_ref, dln_b_ref, dsgw_ref, dsgb_ref, dconv_ref, dwbd_ref, dps_ref,
         dwbr0_ref, dwbr1_ref, dwbr2_ref, dwbr3_ref, dwout_ref, dgpost_ref, dbias_acc) = rest
        i = pl.program_id(0)
        acc_refs = [dgate_b_ref, dln_g_ref, dln_b_ref, dsgw_ref, dsgb_ref, dconv_ref, dwbd_ref, dps_ref,
                    dwbr0_ref, dwbr1_ref, dwbr2_ref, dwbr3_ref, dwout_ref, dgpost_ref, dbias_acc]

        @pl.when(i == 0)
        def _():
            for a in acc_refs:
                a[...] = jnp.zeros_like(a)

        f = _mix_forward(i, ts, r)
        dmo, dgpost = _rms_bwd(f['mo'], r['g_post'][...], dx1_ref[...])
        dgpost_ref[...] += dgpost
        dwout_ref[...] += _dot_tn(f['merged'], dmo)
        dmerged = _dot_nt(dmo, r['w_out'][...])
        dwbr = [dwbr0_ref, dwbr1_ref, dwbr2_ref, dwbr3_ref]
        douts = []
        for b in range(N_BRANCH):
            gb = f['gates'][:, b * D_MODEL:(b + 1) * D_MODEL]
            dgate = dmerged * f['ys'][b] * gb * (1.0 - gb)
            dg_ref[:, b * D_MODEL:(b + 1) * D_MODEL] = dgate.astype(dg_ref.dtype)
            dgate_b_ref[:, b * D_MODEL:(b + 1) * D_MODEL] += jnp.sum(dgate, axis=0, keepdims=True)
            dy = dmerged * gb
            dwbr[b][...] += _dot_tn(f['outs'][b], dy)
            douts.append(_dot_nt(dy, r['w_br'][b][...]))
        do = douts[0]
        do_ref[...] = do.astype(do_ref.dtype)
        prod = do * f['outs'][0]
        for h in range(N_HEADS):
            delta_ref[h] = jnp.sum(prod[:, h * HEAD_PAD:(h + 1) * HEAD_PAD], axis=1, keepdims=True)
        grp = _lane_group(BR_WIDTH)
        ds = douts[1] * f['u_act']
        dsgu = douts[1] * f['sg_s'] * _gelu_grad(f['sgu'], f['tu'])
        dvln_parts = []
        for ci in range(ts // SG_CHUNK):
            rows = slice(ci * SG_CHUNK, (ci + 1) * SG_CHUNK)
            ds_c, vln_c = ds[rows], f['vln'][rows]
            dvln_c = jnp.zeros((SG_CHUNK, BR_WIDTH), F32)
            for g in range(SG_GROUPS):
                dvln_c = dvln_c + jnp.where(grp == g, _dot_tn(f['wm'][g], ds_c), 0.0)
                dsgw_ref[g] += jnp.where(f['tril'], _dot_nt(jnp.where(grp == g, ds_c, 0.0), vln_c), 0.0)
            dbias_acc[...] += ds_c
            dvln_parts.append(dvln_c)
        dvln = dvln_parts[0] if len(dvln_parts) == 1 else jnp.concatenate(dvln_parts, axis=0)
        dln_g_ref[...] += jnp.sum(dvln * f['ln_xh'], axis=0, keepdims=True)
        dln_b_ref[...] += jnp.sum(dvln, axis=0, keepdims=True)
        dxh = dvln * r['ln_g'][...]
        dvg = f['ln_r'] * (dxh - jnp.mean(dxh, axis=-1, keepdims=True)
                           - f['ln_xh'] * jnp.mean(dxh * f['ln_xh'], axis=-1, keepdims=True))
        dsgv = dvg * _gelu_grad(f['sgv'], f['tv'])
        dcvb = douts[2] * f['yv']
        dyv = douts[2] * f['cvb']
        dyv_ref[...] = dyv
        for kk, zk in enumerate((f['z2'], f['z1'], f['z0'])):
            dconv_ref[kk:kk + 1, :] += jnp.sum(dyv * zk, axis=0, keepdims=True)
        dps_ref[...] += jnp.sum(douts[3] * f['mixed'], axis=0, keepdims=True)
        dmixed = douts[3] * r['pool_scale'][...]
        dwbd_ref[...] += _dot_tn(f['pooled'], dmixed)
        up_ref[...] = _dot_nt(dmixed, r['wbd'][...]) / f['cnt']
        dm1_ref[...] = jnp.concatenate([dsgu, dsgv, dcvb], axis=1).astype(dm1_ref.dtype)

        @pl.when(i == pl.num_programs(0) - 1)
        def _():
            lane = lax.broadcasted_iota(jnp.int32, (1, SG_CHUNK), 1)
            db = dbias_acc[...]
            out = jnp.zeros((SG_CHUNK, SG_CHUNK), F32)
            for g in range(SG_GROUPS):
                out = out + jnp.where(lane == g, jnp.sum(jnp.where(grp == g, db, 0.0), axis=1, keepdims=True), 0.0)
            dsgb_ref[...] = out

    acc = lambda shape: (_full(shape), jax.ShapeDtypeStruct(shape, F32))
    accs = [acc((1, N_BRANCH * D_MODEL)), acc((1, BR_WIDTH)), acc((1, BR_WIDTH)), acc((SG_GROUPS, SG_CHUNK, SG_CHUNK)),
            acc((SG_CHUNK, SG_CHUNK)), acc((8, BR_WIDTH)), acc((BR_WIDTH, BR_WIDTH)), acc((1, BR_WIDTH)),
            acc((hw, D_MODEL)), acc((BR_WIDTH, D_MODEL)), acc((BR_WIDTH, D_MODEL)), acc((BR_WIDTH, D_MODEL)),
            acc((D_MODEL, D_MODEL)), acc((1, D_MODEL))]
    tile_outs = [(_rows(ts, N_BRANCH * D_MODEL), jax.ShapeDtypeStruct((s, N_BRANCH * D_MODEL), MXU_DTYPE)),
                 (_rows(ts, 3 * BR_WIDTH), jax.ShapeDtypeStruct((s, 3 * BR_WIDTH), MXU_DTYPE)),
                 (_rows(ts, BR_WIDTH), jax.ShapeDtypeStruct((s, BR_WIDTH), F32)),
                 (_rows(ts, BR_WIDTH), jax.ShapeDtypeStruct((s, BR_WIDTH), F32)),
                 (_rows(ts, hw), jax.ShapeDtypeStruct((s, hw), MXU_DTYPE)),
                 (pl.BlockSpec((N_HEADS, ts, 1), lambda i: (0, i, 0)), jax.ShapeDtypeStruct((N_HEADS, s, 1), F32))]
    outs = tile_outs + accs
    return pl.pallas_call(
        body, name="mix_bwd", grid=(s // ts,),
        in_specs=tiles + weights + [_rows(ts, D_MODEL)],
        out_specs=[o_[0] for o_ in outs], out_shape=[o_[1] for o_ in outs],
        scratch_shapes=[pltpu.VMEM((SG_CHUNK, BR_WIDTH), F32)],
        compiler_params=_params(("arbitrary",), 60 * 1024 * 1024),
    )(*_mix_operands(proj, o, lw), dx1)


def _shift_bwd(dyv, upool, proj, conv_w8):
    s = dyv.shape[0]
    ts = _tile(s, 512)
    nb = s // HALO
    c0 = COL_M1 // BR_WIDTH

    def body(dyv_ref, dyvn_ref, up_ref, upn_ref, cvx_ref, cvc_ref, cw_ref, out_ref):
        i = pl.program_id(0)
        last = (i < pl.num_programs(0) - 1).astype(F32)
        dh = jnp.concatenate([dyv_ref[...], dyvn_ref[...] * last], axis=0)
        dz = (cw_ref[2:3, :] * dh + cw_ref[1:2, :] * _shift_up(dh, 1) + cw_ref[0:1, :] * _shift_up(dh, 2))[:ts]
        up = up_ref[...]
        uh = jnp.concatenate([up, upn_ref[...] * last], axis=0)
        dpool = _window_sums(uh, _shift_up)[:ts] - up * _pool_count(i, ts)
        out_ref[...] = jnp.concatenate([dz * cvc_ref[...].astype(F32), dz * cvx_ref[...].astype(F32), dpool],
                                       axis=1).astype(out_ref.dtype)

    nxt = pl.BlockSpec((HALO, BR_WIDTH), lambda i: (jnp.minimum((i + 1) * (ts // HALO), nb - 1), 0))
    return pl.pallas_call(
        body, name="shift_bwd", grid=(s // ts,),
        in_specs=[_rows(ts, BR_WIDTH), nxt, _rows(ts, BR_WIDTH), nxt, _rows(ts, BR_WIDTH, c0 + 3), _rows(ts, BR_WIDTH, c0 + 4),
                  _full((8, BR_WIDTH))],
        out_specs=_rows(ts, 3 * BR_WIDTH),
        out_shape=jax.ShapeDtypeStruct((s, 3 * BR_WIDTH), MXU_DTYPE),
        compiler_params=_params(("parallel",)),
    )(dyv, dyv, upool, upool, proj, proj, conv_w8)


def _ffn2(a, w2, x1, g):
    s = a.shape[0]
    ts = _tile(s, 512)

    def body(a_ref, w_ref, x1_ref, g_ref, x2_ref, f_ref):
        f = _dot(_relu_sq(a_ref[...]), w_ref[...])
        f_ref[...] = f
        x2_ref[...] = x1_ref[...] + _rms(f, g_ref[...])

    return pl.pallas_call(
        body, name="ffn2", grid=(s // ts,),
        in_specs=[_rows(ts, D_FF), _resident((D_FF, D_MODEL)), _rows(ts, D_MODEL), _full((1, D_MODEL))],
        out_specs=[_rows(ts, D_MODEL), _rows(ts, D_MODEL)],
        out_shape=[jax.ShapeDtypeStruct((s, D_MODEL), F32)] * 2,
        compiler_params=_params(("parallel",)),
    )(a, w2, x1, g)


def _ffn2_loss(a, w2, x1, g, target):
    s = a.shape[0]
    ts = _tile(s, 512)

    def body(a_ref, w_ref, x1_ref, g_ref, t_ref, dy_ref, f_ref, loss_ref):
        @pl.when(pl.program_id(0) == 0)
        def _():
            loss_ref[...] = jnp.zeros_like(loss_ref)

        f = _dot(_relu_sq(a_ref[...]), w_ref[...])
        f_ref[...] = f
        err = x1_ref[...] + _rms(f, g_ref[...]) - t_ref[...]
        dy_ref[...] = err * (1.0 / D_MODEL)
        loss_ref[...] += 0.5 * jnp.sum(jnp.mean(err * err, axis=-1, keepdims=True), axis=0, keepdims=True)

    dy, f, loss = pl.pallas_call(
        body, name="ffn2_loss", grid=(s // ts,),
        in_specs=[_rows(ts, D_FF), _resident((D_FF, D_MODEL)), _rows(ts, D_MODEL), _full((1, D_MODEL)), _rows(ts, D_MODEL)],
        out_specs=[_rows(ts, D_MODEL), _rows(ts, D_MODEL), _full((8, 128))],
        out_shape=[jax.ShapeDtypeStruct((s, D_MODEL), F32)] * 2 + [jax.ShapeDtypeStruct((8, 128), F32)],
        compiler_params=_params(("arbitrary",)),
    )(a, w2, x1, g, target)
    return loss[0, 0], dy, f


def _ffn2_bwd(dx2, f, g, a, w2t, carried=None):
    s = a.shape[0]
    ts = _tile(s, 512)

    def body(dx2_ref, f_ref, g_ref, a_ref, w_ref, df_ref, da_ref, dg_ref):
        @pl.when(pl.program_id(0) == 0)
        def _():
            dg_ref[...] = jnp.zeros_like(dg_ref)

        df, dg = _rms_bwd(f_ref[...], g_ref[...], dx2_ref[...])
        dg_ref[...] += dg
        df_ref[...] = df.astype(df_ref.dtype)
        da_ref[...] = (_dot(df, w_ref[...]) * (2.0 * jnp.maximum(a_ref[...].astype(F32), 0.0))).astype(da_ref.dtype)

    return _call_carrying(
        body, carried, name="ffn2_bwd", grid=(s // ts,),
        in_specs=[_rows(ts, D_MODEL), _rows(ts, D_MODEL), _full((1, D_MODEL)), _rows(ts, D_FF), _resident((D_MODEL, D_FF))],
        out_specs=[_rows(ts, D_MODEL), _rows(ts, D_FF), _full((1, D_MODEL))],
        out_shape=[jax.ShapeDtypeStruct((s, D_MODEL), MXU_DTYPE), jax.ShapeDtypeStruct((s, D_FF), MXU_DTYPE),
                   jax.ShapeDtypeStruct((1, D_MODEL), F32)],
        operands=(dx2, f, g, a, w2t), semantics=("arbitrary",))


def _norm_in_bwd(name, pieces, x, g, dres):
    s = x.shape[0]
    ts = _tile(s, 512)
    n = len(pieces)

    def body(*refs):
        d_refs, w_refs = refs[:n], refs[n:2 * n]
        x_ref, g_ref, dres_ref, dx_ref, dg_ref = refs[2 * n:]

        @pl.when(pl.program_id(0) == 0)
        def _():
            dg_ref[...] = jnp.zeros_like(dg_ref)

        dh = _dot(d_refs[0][...], w_refs[0][...])
        for p in range(1, n):
            dh = dh + _dot(d_refs[p][...], w_refs[p][...])
        dx, dg = _rms_bwd(x_ref[...], g_ref[...], dh)
        dg_ref[...] += dg
        dx_ref[...] = dres_ref[...] + dx

    return pl.pallas_call(
        body, name=name, grid=(s // ts,),
        in_specs=[_rows(ts, d.shape[1]) for d, _ in pieces] + [_resident(w.shape) for _, w in pieces]
        + [_rows(ts, D_MODEL), _full((1, D_MODEL)), _rows(ts, D_MODEL)],
        out_specs=[_rows(ts, D_MODEL), _full((1, D_MODEL))],
        out_shape=[jax.ShapeDtypeStruct((s, D_MODEL), F32), jax.ShapeDtypeStruct((1, D_MODEL), F32)],
        compiler_params=_params(("arbitrary",)),
    )(*[d for d, _ in pieces], *[w for _, w in pieces], x, g, dres)


_W_IN_SPLITS = [256, 384, 416, 672, 928, 1184, 1440, 1696, 1952]


def _rope_swap(w):
    half = QK_ROPE // 2
    return jnp.concatenate([-w[..., half:], w[..., :half]], axis=-1)


def _rope_unswap(d):
    half = QK_ROPE // 2
    return jnp.concatenate([d[..., half:], -d[..., :half]], axis=-1)


def _zeros_like_cols(w, n):
    return jnp.zeros(w.shape[:-1] + (n,), w.dtype)


def _derive_weights(w):
    md = MXU_DTYPE
    nl = w['w_in'].shape[0]
    c_q, c_kv, k_r, sg_u, sg_v, cv_x, cv_b, cv_c, pool, gate = jnp.split(w['w_in'].astype(md), _W_IN_SPLITS, axis=-1)
    pad_rope = lambda r: jnp.concatenate([_zeros_like_cols(r, QK_NOPE), r, _zeros_like_cols(r, HEAD_PAD - QK_NOPE - QK_ROPE)], -1)
    w_in_p = jnp.concatenate([gate, sg_u, sg_v, cv_b, cv_x, cv_c, pool, c_q, c_kv, pad_rope(k_r), pad_rope(_rope_swap(k_r))], -1)
    wq = w['w_uq'].astype(md).reshape(nl, Q_RANK, N_HEADS, QK_NOPE + QK_ROPE)
    nope, rope_w = wq[..., :QK_NOPE], wq[..., QK_NOPE:]
    wq_a = jnp.concatenate([nope, rope_w, _zeros_like_cols(nope, 32)], -1).reshape(nl, Q_RANK, N_HEADS * HEAD_PAD)
    wq_b = pad_rope(_rope_swap(rope_w)).reshape(nl, Q_RANK, N_HEADS * HEAD_PAD)
    wkv = w['w_ukv'].astype(md).reshape(nl, KV_RANK, N_HEADS, QK_NOPE + V_HEAD)
    pad_half = lambda r: jnp.concatenate([r, _zeros_like_cols(r, HEAD_PAD - r.shape[-1])], -1).reshape(nl, KV_RANK, N_HEADS * HEAD_PAD)
    w_br_mla = w['w_br_mla'].astype(md).reshape(nl, N_HEADS, V_HEAD, D_MODEL)
    w_br_mla_p = jnp.concatenate([w_br_mla, jnp.zeros_like(w_br_mla)], axis=2).reshape(nl, N_HEADS * HEAD_PAD, D_MODEL)
    eye = jnp.eye(4, dtype=md)
    wbd = (w['pool_w'].astype(md)[:, :, :, None, :] * eye[None, :, None, :, None]).reshape(nl, BR_WIDTH, BR_WIDTH)
    row = lambda a: a.astype(F32)[:, None, :]
    w_in_pt = jnp.swapaxes(w_in_p, 1, 2)
    return dict(
        w_in_p=w_in_p, wt_g=w_in_pt[:, COL_G:COL_M1], wt_m1=w_in_pt[:, COL_M1:COL_M2], wt_m2=w_in_pt[:, COL_M2:COL_B],
        wt_b=w_in_pt[:, COL_B:],
        wq=jnp.concatenate([wq_a, wq_b], -1), wkv=jnp.concatenate([pad_half(wkv[..., :QK_NOPE]), pad_half(wkv[..., QK_NOPE:])], -1),
        w_br_mla_p=w_br_mla_p, w_br_sg=w['w_br_sg'].astype(md), w_br_conv=w['w_br_conv'].astype(md),
        w_br_pool=w['w_br_pool'].astype(md), wbd=wbd, w_out=w['w_out'].astype(md),
        w_ff1=w['w_ff1'].astype(md), w_ff1t=jnp.swapaxes(w['w_ff1'].astype(md), 1, 2),
        w_ff2=w['w_ff2'].astype(md), w_ff2t=jnp.swapaxes(w['w_ff2'].astype(md), 1, 2),
        norm_mix_pre=row(w['norm_mix_pre']), gate_b=row(w['gate_b']), q_norm=row(w['q_norm']), kv_norm=row(w['kv_norm']),
        sg_ln_g=row(w['sg_ln_g']), sg_ln_b=row(w['sg_ln_b']), sg_w=w['sg_w'].astype(F32),
        sg_bias=jnp.repeat(jnp.swapaxes(w['sg_b'].astype(F32), 1, 2), BR_WIDTH // SG_GROUPS, axis=2),
        conv_w8=jnp.pad(w['conv_w'].astype(F32), ((0, 0), (0, 5), (0, 0))), pool_scale=row(w['pool_scale']),
        norm_mix_post=row(w['norm_mix_post']), norm_ffn_pre=row(w['norm_ffn_pre']), norm_ffn_post=row(w['norm_ffn_post']),
    )


def _rope_tables(positions):
    inv_freq = ROPE_BASE ** (-jnp.arange(0, QK_ROPE, 2, dtype=F32) / QK_ROPE)
    ang = positions.astype(F32)[:, None] * inv_freq
    cos, sin = jnp.cos(ang), jnp.sin(ang)
    n = positions.shape[0]
    ones, z64, z32 = jnp.ones((n, QK_NOPE), F32), jnp.zeros((n, QK_NOPE), F32), jnp.zeros((n, 32), F32)
    return (jnp.concatenate([ones, cos, cos, z32], 1), jnp.concatenate([z64, sin, sin, z32], 1),
            jnp.concatenate([z64, cos, cos, z32], 1))


def _reference_layout_grads(g):
    gate, dm1, dm2, dpb = g['dw_in_pieces']
    nl = gate.shape[0]
    sg_u, sg_v, cv_b = jnp.split(dm1, 3, axis=-1)
    cv_x, cv_c, pool = jnp.split(dm2, 3, axis=-1)
    c_q, c_kv, kr, krs = jnp.split(dpb, [Q_RANK, Q_RANK + KV_RANK, Q_RANK + KV_RANK + HEAD_PAD], axis=-1)
    rope_cols = slice(QK_NOPE, QK_NOPE + QK_ROPE)
    k_r = kr[..., rope_cols] + _rope_unswap(krs[..., rope_cols])
    w_in = jnp.concatenate([c_q, c_kv, k_r, sg_u, sg_v, cv_x, cv_b, cv_c, pool, gate], -1)
    hw = N_HEADS * HEAD_PAD
    dqa = g['dwq'][..., :hw].reshape(nl, Q_RANK, N_HEADS, HEAD_PAD)
    dqb = g['dwq'][..., hw:].reshape(nl, Q_RANK, N_HEADS, HEAD_PAD)
    w_uq = jnp.concatenate([dqa[..., :QK_NOPE], dqa[..., rope_cols] + _rope_unswap(dqb[..., rope_cols])], -1)
    dka = g['dwkv'][..., :hw].reshape(nl, KV_RANK, N_HEADS, HEAD_PAD)
    dva = g['dwkv'][..., hw:].reshape(nl, KV_RANK, N_HEADS, HEAD_PAD)
    w_ukv = jnp.concatenate([dka[..., :QK_NOPE], dva[..., :V_HEAD]], -1)
    w_br_mla = g['dw_br_mla_p'].reshape(nl, N_HEADS, HEAD_PAD, D_MODEL)[:, :, :V_HEAD]
    dwbd = g['dwbd'].reshape(nl, 4, 64, 4, 64)
    pool_w = jnp.stack([dwbd[:, k, :, k, :] for k in range(4)], axis=1)
    sq = lambda a: a[:, 0, :]
    return dict(
        norm_mix_pre=sq(g['dg_pre']), w_in=w_in, gate_b=sq(g['dgate_b']), q_norm=sq(g['dq_norm']),
        w_uq=w_uq.reshape(nl, Q_RANK, -1), kv_norm=sq(g['dkv_norm']), w_ukv=w_ukv.reshape(nl, KV_RANK, -1),
        w_br_mla=w_br_mla.reshape(nl, N_HEADS * V_HEAD, D_MODEL), sg_ln_g=sq(g['dln_g']), sg_ln_b=sq(g['dln_b']),
        sg_w=g['dsg_w'], sg_b=jnp.swapaxes(g['dsg_b'][:, :, :SG_GROUPS], 1, 2), w_br_sg=g['dw_br_sg'],
        conv_w=g['dconv_w'][:, :3], w_br_conv=g['dw_br_conv'], pool_w=pool_w, pool_scale=sq(g['dpool_scale']),
        w_br_pool=g['dw_br_pool'], w_out=g['dw_out'], norm_mix_post=sq(g['dg_post']), norm_ffn_pre=sq(g['dg_fpre']),
        w_ff1=g['dw_ff1'], w_ff2=g['dw_ff2'], norm_ffn_post=sq(g['dg_fpost']))


def _layer_forward(x0, lw, tabs, carried, target=None):
    proj = _mm("in_proj", x0, lw['w_in_p'], tm=1024, tn=N_PROJ, prologue=_rms, rows=(lw['norm_mix_pre'],))
    q, k, v = _qkv_prep(proj, lw['q_norm'], lw['kv_norm'], lw['wq'], lw['wkv'], *tabs)
    (o, lse), carried_out = _attn_fwd(q, k, v, carried)
    x1 = _mix_fwd(x0, proj, o, lw)
    a = _mm("ffn1", x1, lw['w_ff1'], tm=1024, tn=D_FF, prologue=_rms, rows=(lw['norm_ffn_pre'],))
    if target is None:
        x2, f = _ffn2(a, lw['w_ff2'], x1, lw['norm_ffn_post'])
    else:
        loss, dy, f = _ffn2_loss(a, lw['w_ff2'], x1, lw['norm_ffn_post'], target)
        x2 = (loss, dy)
    return x2, dict(x0=x0, proj=proj, q=q, k=k, v=v, o=o, lse=lse, x1=x1, a=a, f=f), carried_out


def _layer_backward(dx2, lw, sv, tabs, early, late):
    g = {}
    (df, da, g['dg_fpost']), early_out = _ffn2_bwd(dx2, sv['f'], lw['norm_ffn_post'], sv['a'], lw['w_ff2t'], early)
    carried = late(early_out)
    g['dw_ff2'] = _mm_tn("dw_ff2", sv['a'], df, tm=512, tn=1024, prologue=_relu_sq)
    dx1, g['dg_fpre'] = _norm_in_bwd("ffn1_bwd", [(da, lw['w_ff1t'])], sv['x1'], lw['norm_ffn_pre'], dx2)
    g['dw_ff1'] = _mm_tn("dw_ff1", sv['x1'], da, tm=512, tn=2048, prologue=_rms, rows=(lw['norm_ffn_pre'],))
    (dgate, dm1, dyv, upool, do, delta, g['dgate_b'], g['dln_g'], g['dln_b'], g['dsg_w'], g['dsg_b'], g['dconv_w'],
     g['dwbd'], g['dpool_scale'], g['dw_br_mla_p'], g['dw_br_sg'], g['dw_br_conv'], g['dw_br_pool'], g['dw_out'],
     g['dg_post']) = _mix_bwd(dx1, sv['proj'], sv['o'], lw)
    dm2 = _shift_bwd(dyv, upool, sv['proj'], lw['conv_w8'])
    (dq, dk, dv), carried_out = _attn_bwd(sv['q'], sv['k'], sv['v'], do, sv['lse'], delta, carried)
    dpb, g['dwq'], g['dwkv'], g['dq_norm'], g['dkv_norm'] = _qkv_bwd(
        dq, dk, dv, sv['proj'], lw['q_norm'], lw['kv_norm'], lw['wq'], lw['wkv'], *tabs)
    pieces = [(dgate, lw['wt_g']), (dm1, lw['wt_m1']), (dm2, lw['wt_m2']), (dpb, lw['wt_b'])]
    dx0, g['dg_pre'] = _norm_in_bwd("in_proj_bwd", pieces, sv['x0'], lw['norm_mix_pre'], dx1)
    g['dw_in_pieces'] = [_mm_tn("dw_in_%d" % n, sv['x0'], d, tm=512, tn=2048, prologue=_rms, rows=(lw['norm_mix_pre'],))
                         for n, (d, _) in enumerate(pieces)]
    return dx0, g, carried_out


class _StepPlan(NamedTuple):
    n_layers: int
    weights_exchange: Callable
    weights_from: Callable
    grads_ready: Callable
    first_done: Callable
    second_done: Callable


def _local_step(x, positions, target, plan):
    tabs = _rope_tables(positions)
    derive = lambda w: {n: a[0] for n, a in _derive_weights(w).items()}
    first = plan.weights_exchange(0)
    weights = plan.weights_from(0, None if first is None else _run_exchange("gather_weights", first))
    derived, saved = [], []
    for l in range(plan.n_layers):
        derived.append(derive(weights))
        coming = plan.weights_exchange(l + 1) if l + 1 < plan.n_layers else None
        x, sv, arrived = _layer_forward(x, derived[l], tabs, coming, target if l + 1 == plan.n_layers else None)
        saved.append(sv)
        if l + 1 < plan.n_layers:
            weights = plan.weights_from(l + 1, arrived)
    loss, dx = x
    pending = None
    for l in reversed(range(plan.n_layers)):
        if pending is None:
            dx, g, _ = _layer_backward(dx, derived[l], saved[l], tabs, None, lambda _: None)
        else:
            dx, g, arrived = _layer_backward(dx, derived[l], saved[l], tabs, pending[1],
                                             functools.partial(plan.first_done, pending[0]))
            plan.second_done(pending[0], arrived)
        lead = lambda a: [b[None] for b in a] if isinstance(a, list) else a[None]
        going = plan.grads_ready(l, _reference_layout_grads({n: lead(a) for n, a in g.items()}))
        pending = None if going is None else (l, going)
    if pending is not None:
        second = plan.first_done(pending[0], _run_exchange("grads_to_sibling", pending[1]))
        plan.second_done(pending[0], _run_exchange("grads_to_chips", second))
    return loss, dx


def _relative_peers():
    x, y = lax.axis_index("x"), lax.axis_index("y")
    return {1: (x, 1 - y), 2: (1 - x, y), 3: (1 - x, 1 - y)}


def _for_my_core(fn):
    def run():
        for half in (0, 1):
            pl.when(lax.axis_index("c") == half)(functools.partial(fn, half))
    return run


def _gather_exchange(split, whole):
    ns, nw = len(split), len(whole)
    n = ns + nw

    def phases(ins, outs, sems):
        ici_send, ici_recv, d2d_send, d2d_recv, own_send, own_recv = sems
        x, y, c = lax.axis_index("x"), lax.axis_index("y"), lax.axis_index("c")
        peers = _relative_peers()

        def rows(ref, which):
            h = ref.shape[-2] // 2
            return ref.at[(slice(None),) * (len(ref.shape) - 2) + (slice(which * h, (which + 1) * h), slice(None))]

        def own(k):
            return pltpu.make_async_remote_copy(src_ref=ins[k], dst_ref=outs[k].at[0], send_sem=own_send.at[k],
                                                recv_sem=own_recv.at[k], device_id=(x, y, 1 - c), device_id_type=MESH)

        def over_ici(k, r, half):
            src = rows(ins[k], half) if k < ns else ins[k]
            dst = rows(outs[k].at[r], half) if k < ns else outs[k].at[r]
            return pltpu.make_async_remote_copy(src_ref=src, dst_ref=dst, send_sem=ici_send.at[3 * k + r - 1],
                                                recv_sem=ici_recv.at[3 * k + r - 1], device_id=(*peers[r], c), device_id_type=MESH)

        def to_sibling(k, r, half):
            landed = rows(outs[k].at[r], half)
            return pltpu.make_async_remote_copy(src_ref=landed, dst_ref=landed, send_sem=d2d_send.at[3 * k + r - 1],
                                                recv_sem=d2d_recv.at[3 * k + r - 1], device_id=(x, y, 1 - c), device_id_type=MESH)

        def start(half):
            for k in range(n):
                own(k).start()
                for r in peers:
                    over_ici(k, r, half).start()

        def middle(half):
            for k in range(n):
                for r in peers:
                    over_ici(k, r, half).wait_recv()
                    if k < ns:
                        to_sibling(k, r, half).start()

        def finish(half):
            for k in range(n):
                own(k).wait()
                for r in peers:
                    if k < ns:
                        to_sibling(k, r, 1 - half).wait_recv()
                        to_sibling(k, r, half).wait_send()
                    over_ici(k, r, half).wait_send()

        return _for_my_core(start), _for_my_core(middle), _for_my_core(finish)

    arrs = list(split) + list(whole)
    return _Exchange(
        operands=arrs, out_shape=[jax.ShapeDtypeStruct((4,) + a.shape, a.dtype) for a in arrs],
        scratch_shapes=[pltpu.SemaphoreType.DMA((3 * n,)), pltpu.SemaphoreType.DMA((3 * n,)), pltpu.SemaphoreType.DMA((3 * ns,)),
                        pltpu.SemaphoreType.DMA((3 * ns,)), pltpu.SemaphoreType.DMA((n,)), pltpu.SemaphoreType.DMA((n,))],
        phases=phases)


def _absolute_chip_order(relative):
    me = 2 * lax.axis_index("x") + lax.axis_index("y")
    return jnp.stack([lax.dynamic_index_in_dim(relative, jnp.bitwise_xor(me, chip), 0, keepdims=False) for chip in range(4)])


REDUCE_STEPS = 8


def _sibling_exchange(arrs):
    n = len(arrs)

    def phases(ins, theirs, sems):
        send_sems, recv_sems = sems
        x, y, c = lax.axis_index("x"), lax.axis_index("y"), lax.axis_index("c")

        def copy(k, my_half):
            h = ins[k].shape[1] // 2
            return pltpu.make_async_remote_copy(src_ref=ins[k].at[:, (1 - my_half) * h:(2 - my_half) * h, :], dst_ref=theirs[k],
                                                send_sem=send_sems.at[k], recv_sem=recv_sems.at[k],
                                                device_id=(x, y, 1 - c), device_id_type=MESH)

        def start(my_half):
            for k in range(n):
                copy(k, my_half).start()

        def finish(my_half):
            for k in range(n):
                copy(k, my_half).wait()

        return _for_my_core(start), lambda: None, _for_my_core(finish)

    return _Exchange(operands=list(arrs),
                     out_shape=[jax.ShapeDtypeStruct((a.shape[0], a.shape[1] // 2, a.shape[2]), a.dtype) for a in arrs],
                     scratch_shapes=[pltpu.SemaphoreType.DMA((n,)), pltpu.SemaphoreType.DMA((n,))], phases=phases)


def _add_sibling(name, arrs, theirs):
    n, steps = len(arrs), REDUCE_STEPS

    def body(*refs):
        for mine_ref, theirs_ref, out_ref in zip(refs[:n], refs[n:2 * n], refs[2 * n:]):
            out_ref[...] = (mine_ref[...] + theirs_ref[...]).astype(out_ref.dtype)

    block = lambda t: (4, t.shape[1] // steps, t.shape[2])
    return pl.pallas_call(
        body, name=name, grid=(steps,),
        in_specs=[pl.BlockSpec(block(t), lambda i: (0, lax.axis_index("c") * steps + i, 0)) for t in theirs]
        + [pl.BlockSpec(block(t), lambda i: (0, i, 0)) for t in theirs],
        out_specs=[pl.BlockSpec(block(t), lambda i: (0, i, 0)) for t in theirs],
        out_shape=[jax.ShapeDtypeStruct(t.shape, WIRE_DTYPE) for t in theirs],
        compiler_params=_params(("parallel",)))(*arrs, *theirs)


def _scatter_exchange(arrs):
    n = len(arrs)

    def phases(ins, outs, sems):
        send_sems, recv_sems = sems
        c = lax.axis_index("c")
        peers = _relative_peers()

        def copy(k, r):
            px, py = peers[r]
            return pltpu.make_async_remote_copy(src_ref=ins[k].at[2 * px + py], dst_ref=outs[k].at[r - 1],
                                                send_sem=send_sems.at[3 * k + r - 1], recv_sem=recv_sems.at[3 * k + r - 1],
                                                device_id=(px, py, c), device_id_type=MESH)

        def start():
            for k in range(n):
                for r in peers:
                    copy(k, r).start()

        def finish():
            for k in range(n):
                for r in peers:
                    copy(k, r).wait()

        return start, lambda: None, finish

    return _Exchange(operands=list(arrs), out_shape=[jax.ShapeDtypeStruct((3,) + a.shape[1:], a.dtype) for a in arrs],
                     scratch_shapes=[pltpu.SemaphoreType.DMA((3 * n,)), pltpu.SemaphoreType.DMA((3 * n,))], phases=phases)


def _sum_chips(name, chip_sums, arrived):
    n, steps = len(chip_sums), REDUCE_STEPS

    def body(*refs):
        for own_ref, arrived_ref, out_ref in zip(refs[:n], refs[n:2 * n], refs[2 * n:]):
            acc = own_ref[...].astype(F32)
            for r in range(3):
                acc = acc + arrived_ref[r].astype(F32)
            out_ref[...] = acc

    rows = lambda s: s.shape[1] // steps
    chip = lambda: 2 * lax.axis_index("x") + lax.axis_index("y")
    return pl.pallas_call(
        body, name=name, grid=(steps,),
        in_specs=[pl.BlockSpec((None, rows(s), s.shape[2]), lambda i: (chip(), i, 0)) for s in chip_sums]
        + [pl.BlockSpec((3, rows(s), s.shape[2]), lambda i: (0, i, 0)) for s in chip_sums],
        out_specs=[pl.BlockSpec((rows(s), s.shape[2]), lambda i: (lax.axis_index("c") * steps + i, 0)) for s in chip_sums],
        out_shape=[jax.ShapeDtypeStruct((2 * s.shape[1], s.shape[2]), F32) for s in chip_sums],
        compiler_params=_params(("parallel",)))(*chip_sums, *arrived)


def _join_siblings(name, bufs):
    n = len(bufs)

    def body(*refs):
        outs = refs[n:2 * n]
        send_sems, recv_sems = refs[2 * n:]
        x, y, c = lax.axis_index("x"), lax.axis_index("y"), lax.axis_index("c")

        def exchange(my_half):
            copies = []
            for k in range(n):
                h = outs[k].shape[0] // 2
                mine = outs[k].at[my_half * h:(my_half + 1) * h, :]
                theirs = outs[k].at[(1 - my_half) * h:(2 - my_half) * h, :]
                cp = pltpu.make_async_remote_copy(src_ref=mine, dst_ref=mine, send_sem=send_sems.at[k],
                                                  recv_sem=recv_sems.at[k], device_id=(x, y, 1 - c), device_id_type=MESH)
                cp.start()
                arrival = pltpu.make_async_remote_copy(src_ref=theirs, dst_ref=theirs, send_sem=send_sems.at[k],
                                                       recv_sem=recv_sems.at[k], device_id=(x, y, 1 - c), device_id_type=MESH)
                copies.append((cp, arrival))
            for cp, arrival in copies:
                arrival.wait_recv()
                cp.wait_send()

        for half in (0, 1):
            pl.when(c == half)(functools.partial(exchange, half))

    return pl.pallas_call(
        body, name=name, in_specs=[HBM] * n, out_specs=[HBM] * n,
        out_shape=[jax.ShapeDtypeStruct(b.shape, b.dtype) for b in bufs], input_output_aliases={k: k for k in range(n)},
        scratch_shapes=[pltpu.SemaphoreType.DMA((n,)), pltpu.SemaphoreType.DMA((n,))],
    )(*bufs)


def _gather_all_exchange(a):
    def phases(ins, outs, scratch):
        (a_ref,), (out_ref,) = ins, outs
        staging, send_sems, recv_sems, local_sem = scratch
        x, y, c = lax.axis_index("x"), lax.axis_index("y"), lax.axis_index("c")
        me = 4 * x + 2 * y + c
        flips = [(fx, fy, fc) for fx in (0, 1) for fy in (0, 1) for fc in (0, 1)][1:]
        peers = [(x ^ fx, y ^ fy, c ^ fc) for fx, fy, fc in flips]

        def copy(j):
            px, py, pc = peers[j]
            return pltpu.make_async_remote_copy(src_ref=a_ref, dst_ref=out_ref.at[me], send_sem=send_sems.at[j],
                                                recv_sem=recv_sems.at[j], device_id=(px, py, pc), device_id_type=MESH)

        def arrival(j):
            px, py, pc = peers[j]
            return pltpu.make_async_remote_copy(src_ref=a_ref, dst_ref=out_ref.at[4 * px + 2 * py + pc], send_sem=send_sems.at[j],
                                                recv_sem=recv_sems.at[j], device_id=(px, py, pc), device_id_type=MESH)

        own = pltpu.make_async_copy(staging, out_ref.at[me], local_sem)

        def start():
            load = pltpu.make_async_copy(a_ref, staging, local_sem)
            load.start()
            load.wait()
            own.start()
            for j in range(7):
                copy(j).start()

        def finish():
            for j in range(7):
                arrival(j).wait_recv()
            for j in range(7):
                copy(j).wait_send()
            own.wait()

        return start, lambda: None, finish

    return _Exchange(operands=[a], out_shape=[jax.ShapeDtypeStruct((8,) + a.shape, a.dtype)],
                     scratch_shapes=[pltpu.VMEM(a.shape, a.dtype), pltpu.SemaphoreType.DMA((7,)), pltpu.SemaphoreType.DMA((7,)),
                                     pltpu.SemaphoreType.DMA], phases=phases)


def _rowwise_call(name, fn, slots, out_shapes, steps, carried=None):
    n_in, n_out = [len(s) for s in slots], [len(o) for o in out_shapes]

    def spec(shape):
        if len(shape) == 3:
            return pl.BlockSpec((shape[0], shape[1] // steps, shape[2]), lambda i: (0, i, 0))
        return pl.BlockSpec((shape[0] // steps, shape[1]), lambda i: (i, 0))

    def body(*refs):
        ins, outs = refs[:sum(n_in)], refs[sum(n_in):]
        a = b = 0
        for k in range(len(slots)):
            for o_ref, val in zip(outs[b:b + n_out[k]], fn(*[r[...] for r in ins[a:a + n_in[k]]])):
                o_ref[...] = val
            a, b = a + n_in[k], b + n_out[k]

    flat_in = [arr for s in slots for arr in s]
    flat_out = [shp for o in out_shapes for shp in o]
    out, carried_out = _call_carrying(
        body, carried, name=name, grid=(steps,), in_specs=[spec(a.shape) for a in flat_in],
        out_specs=[spec(s) for s in flat_out], out_shape=[jax.ShapeDtypeStruct(s, F32) for s in flat_out],
        operands=flat_in, semantics=("parallel",))
    grouped, b = [], 0
    for k in range(len(slots)):
        grouped.append(out[b:b + n_out[k]])
        b += n_out[k]
    return grouped if carried is None else (grouped, carried_out)


def _sum_in_order(a):
    acc = a[0].astype(F32)
    for k in range(1, a.shape[0]):
        acc = acc + a[k].astype(F32)
    return (acc,)


def _adamw_math(w, g, m, v):
    m_new = ADAM_B1 * m + (1.0 - ADAM_B1) * g
    v_new = ADAM_B2 * v + (1.0 - ADAM_B2) * (g * g)
    m_hat = m_new / (1.0 - ADAM_B1 ** ADAM_STEP)
    v_hat = v_new / (1.0 - ADAM_B2 ** ADAM_STEP)
    return -ADAM_LR * (m_hat / (jnp.sqrt(v_hat) + ADAM_EPS) + ADAM_WD * w), m_new, v_new


W_IN_ADAMW_STEPS = 21
SMALL_PACK_COLS = 256
SMALL_PACK_ROWS = 2048


def _pack_small(parts):
    wide = [jnp.pad(p, ((0, 0), (0, 0), (0, SMALL_PACK_COLS - p.shape[2]))) for p in parts]
    rows = jnp.concatenate(wide, axis=1)
    return jnp.pad(rows, ((0, 0), (0, SMALL_PACK_ROWS - rows.shape[1]), (0, 0)))


def _unpack_small(packed, shapes):
    out, row = [], 0
    for a, b in shapes:
        out.append(packed[:, row:row + a, :b])
        row += a
    return out


def _pack(arrs, rows_per_layer, dtype):
    nl = arrs[0].shape[0]
    flat = jnp.concatenate([a.astype(dtype).reshape(nl, -1) for a in arrs], axis=1)
    flat = jnp.pad(flat, ((0, 0), (0, rows_per_layer * PACK_COLS - flat.shape[1])))
    return flat.reshape(nl * rows_per_layer, PACK_COLS)


def _unpack(packed, shapes, rows_per_layer):
    nl = shapes[0][0]
    flat = packed.reshape(packed.shape[:-2] + (nl, rows_per_layer * PACK_COLS))
    out, off = [], 0
    for shp in shapes:
        size = math.prod(shp[1:])
        out.append(flat[..., off:off + size].reshape(packed.shape[:-2] + tuple(shp)))
        off += size
    return out


def _rows_needed(shapes, multiple):
    per_layer = sum(math.prod(s[1:]) for s in shapes)
    rows = -(-per_layer // PACK_COLS)
    return -(-rows // multiple) * multiple


CONV_TILE = (8, 128)


def _layer_shard_exchange(w, l):
    conv = w['conv_w'][l].reshape(-1)
    conv = jnp.pad(conv, (0, math.prod(CONV_TILE) - conv.shape[0])).reshape(CONV_TILE)
    return _gather_exchange([w[n][l].astype(MXU_DTYPE) for n in MATMUL_SHARDED], [conv])


def _layer_full_weights(w, l, gathered):
    gathered = [_absolute_chip_order(g) for g in gathered]
    full = {n: w[n][l:l + 1] for n in WEIGHTS}
    for n, part in zip(MATMUL_SHARDED, gathered):
        if n in ROW_SHARDED:
            full[n] = part.reshape(1, 4 * part.shape[1], part.shape[2])
        else:
            full[n] = jnp.swapaxes(part, 0, 1).reshape(1, part.shape[1], 4 * part.shape[2])
    rows, cols = w['conv_w'].shape[1:]
    conv = gathered[-1].reshape(4, -1)[:, :rows * cols].reshape(4, rows, cols)
    full['conv_w'] = jnp.swapaxes(conv, 0, 1).reshape(1, rows, 4 * cols)
    return full


def _chip_major(n, g):
    nl = g.shape[0]
    if n in ROW_SHARDED:
        return jnp.swapaxes(g.reshape(nl, 4, g.shape[1] // 4, g.shape[2]), 0, 1)
    return jnp.transpose(g.reshape(nl, g.shape[1], 4, g.shape[2] // 4), (2, 0, 1, 3))


def kernel(x, positions, norm_mix_pre, w_in, gate_b, q_norm, w_uq, kv_norm, w_ukv, w_br_mla, sg_ln_g, sg_ln_b, sg_w, sg_b, w_br_sg, conv_w, w_br_conv, pool_w, pool_scale, w_br_pool, w_out, norm_mix_post, norm_ffn_pre, w_ff1, w_ff2, norm_ffn_post, loss_target, m_norm_mix_pre, m_w_in, m_gate_b, m_q_norm, m_w_uq, m_kv_norm, m_w_ukv, m_w_br_mla, m_sg_ln_g, m_sg_ln_b, m_sg_w, m_sg_b, m_w_br_sg, m_conv_w, m_w_br_conv, m_pool_w, m_pool_scale, m_w_br_pool, m_w_out, m_norm_mix_post, m_norm_ffn_pre, m_w_ff1, m_w_ff2, m_norm_ffn_post, v_norm_mix_pre, v_w_in, v_gate_b, v_q_norm, v_w_uq, v_kv_norm, v_w_ukv, v_w_br_mla, v_sg_ln_g, v_sg_ln_b, v_sg_w, v_sg_b, v_w_br_sg, v_conv_w, v_w_br_conv, v_pool_w, v_pool_scale, v_w_br_pool, v_w_out, v_norm_mix_post, v_norm_ffn_pre, v_w_ff1, v_w_ff2, v_norm_ffn_post):
    given = dict(locals())
    w = {n: given[n] for n in WEIGHTS}
    mom = {n: given['m_' + n] for n in WEIGHTS}
    var = {n: given['v_' + n] for n in WEIGHTS}
    nl = w['w_in'].shape[0]
    sharded, chip_sums, reduced, local_small = {}, {}, {}, {}

    def grads_ready(l, g):
        sharded[l] = [_chip_major(n, g[n])[:, 0] for n in BIG_SHARDED]
        sharded[l].append(_pack_small([_chip_major(n, g[n])[:, 0] for n in SMALL_SHARDED]))
        local_small[l] = [g[n] for n in REPLICATED + ['conv_w']]
        return _sibling_exchange(sharded[l])

    def sibling_done(l, theirs):
        chip_sums[l] = _add_sibling("add_sibling", sharded[l], theirs)
        return _scatter_exchange(chip_sums[l])

    def chips_done(l, arrived):
        reduced[l] = _join_siblings("join_halves", _sum_chips("sum_chips", chip_sums[l], arrived))

    plan = _StepPlan(n_layers=nl, weights_exchange=functools.partial(_layer_shard_exchange, w),
                     weights_from=functools.partial(_layer_full_weights, w), grads_ready=grads_ready,
                     first_done=sibling_done, second_done=chips_done)
    loss, dx = _local_step(x[0], positions[0], loss_target[0], plan)
    loss = lax.psum(loss, ("x", "y", "c"))

    grad, delta, new_m, new_v = {}, {}, {}, {}
    names = REPLICATED + ['conv_w']
    local = [jnp.concatenate([local_small[l][k] for l in range(nl)]) for k in range(len(names))]
    rows = _rows_needed([a.shape for a in local], 32)
    small_grads = _gather_all_exchange(_pack(local, rows, F32))
    for k, n in enumerate(BIG_SHARDED):
        grad[n] = jnp.stack([reduced[l][k] for l in range(nl)])
    turned = [jnp.swapaxes(a, 1, 2) for a in (w['w_in'], grad['w_in'], mom['w_in'], var['w_in'])]
    (w_in_out,) = _rowwise_call("adamw_w_in", _adamw_math, [turned], [[turned[0].shape] * 3], W_IN_ADAMW_STEPS)
    delta['w_in'], new_m['w_in'], new_v['w_in'] = [jnp.swapaxes(a, 1, 2) for a in w_in_out]
    others = [n for n in BIG_SHARDED if n != 'w_in']
    slots = [[w[n], grad[n], mom[n], var[n]] for n in others]
    small_pack = lambda d: _pack_small([d[n] for n in SMALL_SHARDED])
    g_small = jnp.stack([reduced[l][-1] for l in range(nl)])
    slots.append([small_pack(w), g_small, small_pack(mom), small_pack(var)])
    updated, (everyone,) = _rowwise_call("adamw_sharded", _adamw_math, slots, [[s_[0].shape] * 3 for s_ in slots], 32,
                                         carried=small_grads)
    for k, n in enumerate(others):
        delta[n], new_m[n], new_v[n] = updated[k]
    for d, packed in zip((grad, delta, new_m, new_v), [g_small] + list(updated[-1])):
        d.update(zip(SMALL_SHARDED, _unpack_small(packed, [w[n].shape[1:] for n in SMALL_SHARDED])))

    (summed,), = _rowwise_call("sum_devices", _sum_in_order, [[everyone]], [[everyone.shape[1:]]], 4)
    g_rep = _unpack(summed, [a.shape for a in local], rows)
    chip = 2 * lax.axis_index("x") + lax.axis_index("y")
    g_rep[-1] = lax.dynamic_slice_in_dim(g_rep[-1], chip * w['conv_w'].shape[2], w['conv_w'].shape[2], axis=2)
    rep_pack = lambda arrs: _pack(arrs, rows, F32)
    (rep_out,) = _rowwise_call("adamw_replicated", _adamw_math,
                               [[rep_pack([w[n] for n in names]), rep_pack(g_rep), rep_pack([mom[n] for n in names]),
                                 rep_pack([var[n] for n in names])]], [[(nl * rows, PACK_COLS)] * 3], 4)
    grad.update(zip(names, g_rep))
    for d, packed in zip((delta, new_m, new_v), rep_out):
        d.update(zip(names, _unpack(packed, [w[n].shape for n in names], rows)))

    return (loss, dx[None], *[grad[n] for n in WEIGHTS], *[delta[n] for n in WEIGHTS], *[new_m[n] for n in WEIGHTS],
            *[new_v[n] for n in WEIGHTS])
```
